```python
import math
import jax, jax.numpy as jnp
from jax import lax
import numpy as np

D_MODEL = 1024
BATCH = 8
SEQ = 8192
DEPTH = 2

PLE_DIM = 256
NORM_EPS = 1e-6
S5_WIDTH = 512
S5_GROUP = 16
S5_GROUPS = S5_WIDTH // S5_GROUP
S5_STATE = 64
S5_DT_MIN = 1e-3
S5_DT_MAX = 1e-1
LRU_WIDTH = 1280
LRU_HEADS = 10
LRU_HEAD_DIM = LRU_WIDTH // LRU_HEADS
LRU_C = 8.0
CONV_WIDTH = 4
IN_SPLITS = (
    S5_WIDTH,
    2 * S5_WIDTH,
    2 * S5_WIDTH + LRU_WIDTH,
    2 * S5_WIDTH + 2 * LRU_WIDTH,
    2 * S5_WIDTH + 2 * LRU_WIDTH + D_MODEL,
)
IN_COLS = 2 * S5_WIDTH + 2 * LRU_WIDTH + 2 * D_MODEL

kernel_name = 'hybrid_s5_rglru_gated_parallel'


def rms_norm(x, g):
    xf = x.astype(jnp.float32)
    y = xf * lax.rsqrt(jnp.mean(xf * xf, axis=-1, keepdims=True) + NORM_EPS)
    return (y * g.astype(jnp.float32)).astype(x.dtype)


def _linear_combine(e1, e2):
    a1, b1 = e1
    a2, b2 = e2
    return (a1 * a2, a2 * b1 + b2)


def s5_ssm(u, a_re, a_im, log_dt, b_re, b_im, c_re, c_im, d_skip):
    f32 = jnp.float32
    bsz, seqlen, _ = u.shape
    uf = u.astype(f32)
    ug = uf.reshape(bsz, seqlen, S5_GROUPS, S5_GROUP)
    lam = lax.complex(a_re.astype(f32), a_im.astype(f32))
    dt = jnp.exp(log_dt.astype(f32))[:, None]
    a_bar = jnp.exp(lam * dt)
    zoh = (a_bar - 1.0) / lam
    b = lax.complex(b_re.astype(f32), b_im.astype(f32))
    b_bar = zoh[..., None] * b
    bu = lax.complex(jnp.einsum('blgc,gnc->lbgn', ug, jnp.real(b_bar)),
                     jnp.einsum('blgc,gnc->lbgn', ug, jnp.imag(b_bar)))
    a_seq = jnp.broadcast_to(a_bar[None, None], (seqlen, 1, S5_GROUPS, S5_STATE))
    _, states = lax.associative_scan(_linear_combine, (a_seq, bu), axis=0)
    y = (jnp.einsum('lbgn,gcn->blgc', jnp.real(states), c_re.astype(f32))
         - jnp.einsum('lbgn,gcn->blgc', jnp.imag(states), c_im.astype(f32)))
    y = y.reshape(bsz, seqlen, S5_WIDTH) + d_skip.astype(f32) * uf
    return y.astype(u.dtype)


def causal_depthwise_conv(x, w, b):
    y = lax.conv_general_dilated(
        x, w[:, None, :].astype(x.dtype), window_strides=(1,),
        padding=[(CONV_WIDTH - 1, 0)],
        dimension_numbers=('NWC', 'WIO', 'NWC'),
        feature_group_count=x.shape[-1])
    return y + b


def rg_lru(x, w_a, b_a, w_x, b_x, lam):
    f32 = jnp.float32
    bsz, seqlen, _ = x.shape
    xf = x.astype(f32)
    xh = xf.reshape(bsz, seqlen, LRU_HEADS, LRU_HEAD_DIM)
    r = jax.nn.sigmoid(jnp.einsum('blhi,hij->blhj', xh, w_a.astype(f32)).reshape(bsz, seqlen, LRU_WIDTH)
                       + b_a.astype(f32))
    i = jax.nn.sigmoid(jnp.einsum('blhi,hij->blhj', xh, w_x.astype(f32)).reshape(bsz, seqlen, LRU_WIDTH)
                       + b_x.astype(f32))
    log_a = -LRU_C * r * jax.nn.softplus(-lam.astype(f32))
    a = jnp.exp(log_a)
    mult = jnp.sqrt(-jnp.expm1(2.0 * log_a))
    _, h = lax.associative_scan(_linear_combine, (a, mult * (i * xf)), axis=1)
    return h.astype(x.dtype)


def _fwd_setup_inputs(seed: int = 0) -> dict:
    key = jax.random.key(seed)
    ks = jax.random.split(key, 32)
    f32 = jnp.float32

    def nrm(k, shape, scale):
        return jax.random.normal(k, shape, f32) * scale

    L, D, G, N, C, HD = DEPTH, D_MODEL, S5_GROUPS, S5_STATE, S5_GROUP, LRU_HEAD_DIM
    x = nrm(ks[0], (BATCH, SEQ, D), 1.0)
    p = nrm(ks[1], (DEPTH, BATCH, SEQ, PLE_DIM), 1.0)
    g_pre = 1.0 + nrm(ks[2], (L, D), 0.05)
    w_in = nrm(ks[3], (L, D, IN_COLS), D ** -0.5)
    s5_a_re = -0.5 + nrm(ks[4], (L, G, N), 0.01)
    s5_a_im = math.pi * jnp.arange(N, dtype=f32)[None, None, :] + nrm(ks[5], (L, G, N), 0.01)
    s5_log_dt = jax.random.uniform(ks[6], (L, G), f32, math.log(S5_DT_MIN), math.log(S5_DT_MAX))
    s5_b_re = nrm(ks[7], (L, G, N, C), C ** -0.5)
    s5_b_im = nrm(ks[8], (L, G, N, C), C ** -0.5)
    s5_c_re = nrm(ks[9], (L, G, C, N), N ** -0.5)
    s5_c_im = nrm(ks[10], (L, G, C, N), N ** -0.5)
    s5_d = nrm(ks[11], (L, S5_WIDTH), 1.0)
    w_glu = nrm(ks[12], (L, S5_WIDTH, 2 * S5_WIDTH), S5_WIDTH ** -0.5)
    w_bs = nrm(ks[13], (L, S5_WIDTH, D), S5_WIDTH ** -0.5)
    conv_w = nrm(ks[14], (L, CONV_WIDTH, LRU_WIDTH), CONV_WIDTH ** -0.5)
    conv_b = nrm(ks[15], (L, LRU_WIDTH), 0.01)
    lru_w_a = nrm(ks[16], (L, LRU_HEADS, HD, HD), HD ** -0.5)
    lru_b_a = nrm(ks[17], (L, LRU_WIDTH), 0.01)
    lru_w_x = nrm(ks[18], (L, LRU_HEADS, HD, HD), HD ** -0.5)
    lru_b_x = nrm(ks[19], (L, LRU_WIDTH), 0.01)
    a_c = jax.random.uniform(ks[20], (L, LRU_WIDTH), f32, 0.9, 0.999)
    sig = a_c ** (1.0 / LRU_C)
    lru_lambda = jnp.log(sig) - jnp.log1p(-sig)
    w_bl = nrm(ks[21], (L, LRU_WIDTH, D), LRU_WIDTH ** -0.5)
    w_out = nrm(ks[22], (L, D, D), D ** -0.5)
    g_post = 1.0 + nrm(ks[23], (L, D), 0.05)
    w_ple = nrm(ks[24], (L, PLE_DIM, D), PLE_DIM ** -0.5)
    w_ple_gate = nrm(ks[25], (L, D, D), D ** -0.5)
    return {
        'x': x, 'p': p, 'g_pre': g_pre, 'w_in': w_in,
        's5_a_re': s5_a_re, 's5_a_im': s5_a_im, 's5_log_dt': s5_log_dt,
        's5_b_re': s5_b_re, 's5_b_im': s5_b_im, 's5_c_re': s5_c_re, 's5_c_im': s5_c_im,
        's5_d': s5_d, 'w_glu': w_glu, 'w_bs': w_bs,
        'conv_w': conv_w, 'conv_b': conv_b,
        'lru_w_a': lru_w_a, 'lru_b_a': lru_b_a, 'lru_w_x': lru_w_x, 'lru_b_x': lru_b_x,
        'lru_lambda': lru_lambda, 'w_bl': w_bl, 'w_out': w_out, 'g_post': g_post,
        'w_ple': w_ple, 'w_ple_gate': w_ple_gate,
    }


def _fwd_reference(x, p, g_pre, w_in, s5_a_re, s5_a_im, s5_log_dt, s5_b_re, s5_b_im,
              s5_c_re, s5_c_im, s5_d, w_glu, w_bs, conv_w, conv_b,
              lru_w_a, lru_b_a, lru_w_x, lru_b_x, lru_lambda, w_bl, w_out, g_post,
              w_ple, w_ple_gate):
    for i in range(DEPTH):
        h = rms_norm(x, g_pre[i])
        proj = h @ w_in[i]
        s5_x, s5_g, lru_x, lru_g, gate_s, gate_l = jnp.split(proj, IN_SPLITS, axis=-1)

        y_s = s5_ssm(s5_x, s5_a_re[i], s5_a_im[i], s5_log_dt[i], s5_b_re[i], s5_b_im[i],
                     s5_c_re[i], s5_c_im[i], s5_d[i])
        glu_a, glu_b = jnp.split(jax.nn.gelu(y_s) @ w_glu[i], 2, axis=-1)
        y_s = glu_a * jax.nn.sigmoid(glu_b) * jax.nn.silu(s5_g)
        z_s = y_s @ w_bs[i]

        c = causal_depthwise_conv(lru_x, conv_w[i], conv_b[i])
        y_l = rg_lru(c, lru_w_a[i], lru_b_a[i], lru_w_x[i], lru_b_x[i], lru_lambda[i])
        z_l = (y_l * jax.nn.silu(lru_g)) @ w_bl[i]

        merged = jax.nn.sigmoid(gate_s) * z_s + jax.nn.sigmoid(gate_l) * z_l
        x = x + rms_norm(merged @ w_out[i], g_post[i])

        x = x + (p[i] @ w_ple[i]) * jax.nn.sigmoid(x @ w_ple_gate[i])
    return x


import jax as _jax
import jax.numpy as _jnp

TWIN_FORMAT = 'train_step'
FWD_PARAMS = ['x', 'p', 'g_pre', 'w_in', 's5_a_re', 's5_a_im', 's5_log_dt', 's5_b_re', 's5_b_im', 's5_c_re', 's5_c_im', 's5_d', 'w_glu', 'w_bs', 'conv_w', 'conv_b', 'lru_w_a', 'lru_b_a', 'lru_w_x', 'lru_b_x', 'lru_lambda', 'w_bl', 'w_out', 'g_post', 'w_ple', 'w_ple_gate']
TWIN_WEIGHTS = ['g_pre', 'w_in', 's5_a_re', 's5_a_im', 's5_log_dt', 's5_b_re', 's5_b_im', 's5_c_re', 's5_c_im', 's5_d', 'w_glu', 'w_bs', 'conv_w', 'conv_b', 'lru_w_a', 'lru_b_a', 'lru_w_x', 'lru_b_x', 'lru_lambda', 'w_bl', 'w_out', 'g_post', 'w_ple', 'w_ple_gate']
TWIN_DIFF_INPUT = 'x'
TWIN_INPUTS = ['x', 'p', 'g_pre', 'w_in', 's5_a_re', 's5_a_im', 's5_log_dt', 's5_b_re', 's5_b_im', 's5_c_re', 's5_c_im', 's5_d', 'w_glu', 'w_bs', 'conv_w', 'conv_b', 'lru_w_a', 'lru_b_a', 'lru_w_x', 'lru_b_x', 'lru_lambda', 'w_bl', 'w_out', 'g_post', 'w_ple', 'w_ple_gate', 'loss_target', 'm_g_pre', 'm_w_in', 'm_s5_a_re', 'm_s5_a_im', 'm_s5_log_dt', 'm_s5_b_re', 'm_s5_b_im', 'm_s5_c_re', 'm_s5_c_im', 'm_s5_d', 'm_w_glu', 'm_w_bs', 'm_conv_w', 'm_conv_b', 'm_lru_w_a', 'm_lru_b_a', 'm_lru_w_x', 'm_lru_b_x', 'm_lru_lambda', 'm_w_bl', 'm_w_out', 'm_g_post', 'm_w_ple', 'm_w_ple_gate', 'v_g_pre', 'v_w_in', 'v_s5_a_re', 'v_s5_a_im', 'v_s5_log_dt', 'v_s5_b_re', 'v_s5_b_im', 'v_s5_c_re', 'v_s5_c_im', 'v_s5_d', 'v_w_glu', 'v_w_bs', 'v_conv_w', 'v_conv_b', 'v_lru_w_a', 'v_lru_b_a', 'v_lru_w_x', 'v_lru_b_x', 'v_lru_lambda', 'v_w_bl', 'v_w_out', 'v_g_post', 'v_w_ple', 'v_w_ple_gate']
TWIN_OUTPUTS = ['loss', 'grad_x', 'grad_g_pre', 'grad_w_in', 'grad_s5_a_re', 'grad_s5_a_im', 'grad_s5_log_dt', 'grad_s5_b_re', 'grad_s5_b_im', 'grad_s5_c_re', 'grad_s5_c_im', 'grad_s5_d', 'grad_w_glu', 'grad_w_bs', 'grad_conv_w', 'grad_conv_b', 'grad_lru_w_a', 'grad_lru_b_a', 'grad_lru_w_x', 'grad_lru_b_x', 'grad_lru_lambda', 'grad_w_bl', 'grad_w_out', 'grad_g_post', 'grad_w_ple', 'grad_w_ple_gate', 'delta_g_pre', 'delta_w_in', 'delta_s5_a_re', 'delta_s5_a_im', 'delta_s5_log_dt', 'delta_s5_b_re', 'delta_s5_b_im', 'delta_s5_c_re', 'delta_s5_c_im', 'delta_s5_d', 'delta_w_glu', 'delta_w_bs', 'delta_conv_w', 'delta_conv_b', 'delta_lru_w_a', 'delta_lru_b_a', 'delta_lru_w_x', 'delta_lru_b_x', 'delta_lru_lambda', 'delta_w_bl', 'delta_w_out', 'delta_g_post', 'delta_w_ple', 'delta_w_ple_gate', 'new_m_g_pre', 'new_m_w_in', 'new_m_s5_a_re', 'new_m_s5_a_im', 'new_m_s5_log_dt', 'new_m_s5_b_re', 'new_m_s5_b_im', 'new_m_s5_c_re', 'new_m_s5_c_im', 'new_m_s5_d', 'new_m_w_glu', 'new_m_w_bs', 'new_m_conv_w', 'new_m_conv_b', 'new_m_lru_w_a', 'new_m_lru_b_a', 'new_m_lru_w_x', 'new_m_lru_b_x', 'new_m_lru_lambda', 'new_m_w_bl', 'new_m_w_out', 'new_m_g_post', 'new_m_w_ple', 'new_m_w_ple_gate', 'new_v_g_pre', 'new_v_w_in', 'new_v_s5_a_re', 'new_v_s5_a_im', 'new_v_s5_log_dt', 'new_v_s5_b_re', 'new_v_s5_b_im', 'new_v_s5_c_re', 'new_v_s5_c_im', 'new_v_s5_d', 'new_v_w_glu', 'new_v_w_bs', 'new_v_conv_w', 'new_v_conv_b', 'new_v_lru_w_a', 'new_v_lru_b_a', 'new_v_lru_w_x', 'new_v_lru_b_x', 'new_v_lru_lambda', 'new_v_w_bl', 'new_v_w_out', 'new_v_g_post', 'new_v_w_ple', 'new_v_w_ple_gate']
TWIN_LEAF_KINDS = {'loss': 'loss', 'grad_x': 'grad_x', 'grad_g_pre': 'grad_w', 'grad_w_in': 'grad_w', 'grad_s5_a_re': 'grad_w', 'grad_s5_a_im': 'grad_w', 'grad_s5_log_dt': 'grad_w', 'grad_s5_b_re': 'grad_w', 'grad_s5_b_im': 'grad_w', 'grad_s5_c_re': 'grad_w', 'grad_s5_c_im': 'grad_w', 'grad_s5_d': 'grad_w', 'grad_w_glu': 'grad_w', 'grad_w_bs': 'grad_w', 'grad_conv_w': 'grad_w', 'grad_conv_b': 'grad_w', 'grad_lru_w_a': 'grad_w', 'grad_lru_b_a': 'grad_w', 'grad_lru_w_x': 'grad_w', 'grad_lru_b_x': 'grad_w', 'grad_lru_lambda': 'grad_w', 'grad_w_bl': 'grad_w', 'grad_w_out': 'grad_w', 'grad_g_post': 'grad_w', 'grad_w_ple': 'grad_w', 'grad_w_ple_gate': 'grad_w', 'delta_g_pre': 'delta_w', 'delta_w_in': 'delta_w', 'delta_s5_a_re': 'delta_w', 'delta_s5_a_im': 'delta_w', 'delta_s5_log_dt': 'delta_w', 'delta_s5_b_re': 'delta_w', 'delta_s5_b_im': 'delta_w', 'delta_s5_c_re': 'delta_w', 'delta_s5_c_im': 'delta_w', 'delta_s5_d': 'delta_w', 'delta_w_glu': 'delta_w', 'delta_w_bs': 'delta_w', 'delta_conv_w': 'delta_w', 'delta_conv_b': 'delta_w', 'delta_lru_w_a': 'delta_w', 'delta_lru_b_a': 'delta_w', 'delta_lru_w_x': 'delta_w', 'delta_lru_b_x': 'delta_w', 'delta_lru_lambda': 'delta_w', 'delta_w_bl': 'delta_w', 'delta_w_out': 'delta_w', 'delta_g_post': 'delta_w', 'delta_w_ple': 'delta_w', 'delta_w_ple_gate': 'delta_w', 'new_m_g_pre': 'new_m', 'new_m_w_in': 'new_m', 'new_m_s5_a_re': 'new_m', 'new_m_s5_a_im': 'new_m', 'new_m_s5_log_dt': 'new_m', 'new_m_s5_b_re': 'new_m', 'new_m_s5_b_im': 'new_m', 'new_m_s5_c_re': 'new_m', 'new_m_s5_c_im': 'new_m', 'new_m_s5_d': 'new_m', 'new_m_w_glu': 'new_m', 'new_m_w_bs': 'new_m', 'new_m_conv_w': 'new_m', 'new_m_conv_b': 'new_m', 'new_m_lru_w_a': 'new_m', 'new_m_lru_b_a': 'new_m', 'new_m_lru_w_x': 'new_m', 'new_m_lru_b_x': 'new_m', 'new_m_lru_lambda': 'new_m', 'new_m_w_bl': 'new_m', 'new_m_w_out': 'new_m', 'new_m_g_post': 'new_m', 'new_m_w_ple': 'new_m', 'new_m_w_ple_gate': 'new_m', 'new_v_g_pre': 'new_v', 'new_v_w_in': 'new_v', 'new_v_s5_a_re': 'new_v', 'new_v_s5_a_im': 'new_v', 'new_v_s5_log_dt': 'new_v', 'new_v_s5_b_re': 'new_v', 'new_v_s5_b_im': 'new_v', 'new_v_s5_c_re': 'new_v', 'new_v_s5_c_im': 'new_v', 'new_v_s5_d': 'new_v', 'new_v_w_glu': 'new_v', 'new_v_w_bs': 'new_v', 'new_v_conv_w': 'new_v', 'new_v_conv_b': 'new_v', 'new_v_lru_w_a': 'new_v', 'new_v_lru_b_a': 'new_v', 'new_v_lru_w_x': 'new_v', 'new_v_lru_b_x': 'new_v', 'new_v_lru_lambda': 'new_v', 'new_v_w_bl': 'new_v', 'new_v_w_out': 'new_v', 'new_v_g_post': 'new_v', 'new_v_w_ple': 'new_v', 'new_v_w_ple_gate': 'new_v'}


def _forward(args):
    return _fwd_reference(*[args[k] for k in FWD_PARAMS])


def _output_shape():
    def fwd():
        inp = _fwd_setup_inputs(0)
        return _fwd_reference(*[inp[k] for k in FWD_PARAMS])
    out = _jax.eval_shape(fwd)
    return out.shape, out.dtype

N_MICROBATCH = 1
ADAM_LR = 0.001
ADAM_B1 = 0.9
ADAM_B2 = 0.999
ADAM_EPS = 1e-08
ADAM_WD = 0.01
ADAM_STEP = 10
PER_EXAMPLE_BATCH_AXIS = {'x': 0, 'p': 1, 'loss_target': 0}
SHARED_INPUTS = []
_WEIGHT_DTYPES = {'g_pre': _jnp.float32, 'w_in': _jnp.float32, 's5_a_re': _jnp.float32, 's5_a_im': _jnp.float32, 's5_log_dt': _jnp.float32, 's5_b_re': _jnp.float32, 's5_b_im': _jnp.float32, 's5_c_re': _jnp.float32, 's5_c_im': _jnp.float32, 's5_d': _jnp.float32, 'w_glu': _jnp.float32, 'w_bs': _jnp.float32, 'conv_w': _jnp.float32, 'conv_b': _jnp.float32, 'lru_w_a': _jnp.float32, 'lru_b_a': _jnp.float32, 'lru_w_x': _jnp.float32, 'lru_b_x': _jnp.float32, 'lru_lambda': _jnp.float32, 'w_bl': _jnp.float32, 'w_out': _jnp.float32, 'g_post': _jnp.float32, 'w_ple': _jnp.float32, 'w_ple_gate': _jnp.float32}
MOMENT_SCALE = {'g_pre': 1.683035e+00, 'w_in': 7.215988e-01, 's5_a_re': 5.330047e-02, 's5_a_im': 5.459883e-02, 's5_log_dt': 2.420781e+01, 's5_b_re': 1.995783e-02, 's5_b_im': 1.927473e-02, 's5_c_re': 3.981057e-02, 's5_c_im': 3.644661e-02, 's5_d': 1.269136e+00, 'w_glu': 9.843101e-01, 'w_bs': 9.921015e-01, 'conv_w': 1.947785e+00, 'conv_b': 3.084178e+01, 'lru_w_a': 9.398799e-01, 'lru_b_a': 6.501836e-01, 'lru_w_x': 1.687740e+00, 'lru_b_x': 6.230453e-01, 'lru_lambda': 9.964682e-01, 'w_bl': 2.515644e+00, 'w_out': 2.764655e+00, 'g_post': 6.577398e+01, 'w_ple': 9.789919e-01, 'w_ple_gate': 9.153867e-01}


def _to_microbatches(a, axis):
    t = _jnp.moveaxis(a, axis, 0)
    t = t.reshape((N_MICROBATCH, t.shape[0] // N_MICROBATCH) + t.shape[1:])
    return _jnp.moveaxis(t, 1, axis + 1)


def setup_inputs(seed: int = 0) -> dict:
    inp = _fwd_setup_inputs(seed)
    key = _jax.random.fold_in(_jax.random.key(seed), 7919)
    shape, _ = _output_shape()
    out = dict(inp)
    out["loss_target"] = _jax.random.normal(_jax.random.fold_in(key, 0), shape, _jnp.float32)
    for i, name in enumerate(TWIN_WEIGHTS):
        w = inp[name].astype(_jnp.float32)
        if MOMENT_SCALE is None:
            s = _jnp.sqrt(_jnp.mean(_jnp.square(w)) + 1e-30)
        else:
            s = MOMENT_SCALE[name]
        km, kv = _jax.random.split(_jax.random.fold_in(key, i + 1))
        out[name] = w
        out["m_" + name] = s * _jax.random.normal(km, w.shape, _jnp.float32)
        out["v_" + name] = (s * s) * _jax.random.uniform(kv, w.shape, _jnp.float32, 0.5, 1.5)
    if N_MICROBATCH > 1:
        for name, axis in PER_EXAMPLE_BATCH_AXIS.items():
            out[name] = _to_microbatches(out[name], axis)
    return {'x': out['x'], 'p': out['p'], 'g_pre': out['g_pre'], 'w_in': out['w_in'], 's5_a_re': out['s5_a_re'], 's5_a_im': out['s5_a_im'], 's5_log_dt': out['s5_log_dt'], 's5_b_re': out['s5_b_re'], 's5_b_im': out['s5_b_im'], 's5_c_re': out['s5_c_re'], 's5_c_im': out['s5_c_im'], 's5_d': out['s5_d'], 'w_glu': out['w_glu'], 'w_bs': out['w_bs'], 'conv_w': out['conv_w'], 'conv_b': out['conv_b'], 'lru_w_a': out['lru_w_a'], 'lru_b_a': out['lru_b_a'], 'lru_w_x': out['lru_w_x'], 'lru_b_x': out['lru_b_x'], 'lru_lambda': out['lru_lambda'], 'w_bl': out['w_bl'], 'w_out': out['w_out'], 'g_post': out['g_post'], 'w_ple': out['w_ple'], 'w_ple_gate': out['w_ple_gate'], 'loss_target': out['loss_target'], 'm_g_pre': out['m_g_pre'], 'm_w_in': out['m_w_in'], 'm_s5_a_re': out['m_s5_a_re'], 'm_s5_a_im': out['m_s5_a_im'], 'm_s5_log_dt': out['m_s5_log_dt'], 'm_s5_b_re': out['m_s5_b_re'], 'm_s5_b_im': out['m_s5_b_im'], 'm_s5_c_re': out['m_s5_c_re'], 'm_s5_c_im': out['m_s5_c_im'], 'm_s5_d': out['m_s5_d'], 'm_w_glu': out['m_w_glu'], 'm_w_bs': out['m_w_bs'], 'm_conv_w': out['m_conv_w'], 'm_conv_b': out['m_conv_b'], 'm_lru_w_a': out['m_lru_w_a'], 'm_lru_b_a': out['m_lru_b_a'], 'm_lru_w_x': out['m_lru_w_x'], 'm_lru_b_x': out['m_lru_b_x'], 'm_lru_lambda': out['m_lru_lambda'], 'm_w_bl': out['m_w_bl'], 'm_w_out': out['m_w_out'], 'm_g_post': out['m_g_post'], 'm_w_ple': out['m_w_ple'], 'm_w_ple_gate': out['m_w_ple_gate'], 'v_g_pre': out['v_g_pre'], 'v_w_in': out['v_w_in'], 'v_s5_a_re': out['v_s5_a_re'], 'v_s5_a_im': out['v_s5_a_im'], 'v_s5_log_dt': out['v_s5_log_dt'], 'v_s5_b_re': out['v_s5_b_re'], 'v_s5_b_im': out['v_s5_b_im'], 'v_s5_c_re': out['v_s5_c_re'], 'v_s5_c_im': out['v_s5_c_im'], 'v_s5_d': out['v_s5_d'], 'v_w_glu': out['v_w_glu'], 'v_w_bs': out['v_w_bs'], 'v_conv_w': out['v_conv_w'], 'v_conv_b': out['v_conv_b'], 'v_lru_w_a': out['v_lru_w_a'], 'v_lru_b_a': out['v_lru_b_a'], 'v_lru_w_x': out['v_lru_w_x'], 'v_lru_b_x': out['v_lru_b_x'], 'v_lru_lambda': out['v_lru_lambda'], 'v_w_bl': out['v_w_bl'], 'v_w_out': out['v_w_out'], 'v_g_post': out['v_g_post'], 'v_w_ple': out['v_w_ple'], 'v_w_ple_gate': out['v_w_ple_gate']}


def _loss(weights, diff, rest, loss_target):
    with _jax.named_scope("forward"):
        args = {**rest, TWIN_DIFF_INPUT: diff, **{k: w.astype(_WEIGHT_DTYPES[k]) for k, w in weights.items()}}
        y = _forward(args)
    with _jax.named_scope("loss_head"):
        err = _jnp.square(y.astype(_jnp.float32) - loss_target)
        return 0.5 * _jnp.sum(_jnp.mean(err, axis=-1)) if err.ndim else 0.5 * err


def _adamw(w, g, m, v):
    m = ADAM_B1 * m + (1.0 - ADAM_B1) * g
    v = ADAM_B2 * v + (1.0 - ADAM_B2) * _jnp.square(g)
    m_hat = m / (1.0 - ADAM_B1 ** ADAM_STEP)
    v_hat = v / (1.0 - ADAM_B2 ** ADAM_STEP)
    delta = -ADAM_LR * (m_hat / (_jnp.sqrt(v_hat) + ADAM_EPS) + ADAM_WD * w)
    return delta, m, v


def reference(x, p, g_pre, w_in, s5_a_re, s5_a_im, s5_log_dt, s5_b_re, s5_b_im, s5_c_re, s5_c_im, s5_d, w_glu, w_bs, conv_w, conv_b, lru_w_a, lru_b_a, lru_w_x, lru_b_x, lru_lambda, w_bl, w_out, g_post, w_ple, w_ple_gate, loss_target, m_g_pre, m_w_in, m_s5_a_re, m_s5_a_im, m_s5_log_dt, m_s5_b_re, m_s5_b_im, m_s5_c_re, m_s5_c_im, m_s5_d, m_w_glu, m_w_bs, m_conv_w, m_conv_b, m_lru_w_a, m_lru_b_a, m_lru_w_x, m_lru_b_x, m_lru_lambda, m_w_bl, m_w_out, m_g_post, m_w_ple, m_w_ple_gate, v_g_pre, v_w_in, v_s5_a_re, v_s5_a_im, v_s5_log_dt, v_s5_b_re, v_s5_b_im, v_s5_c_re, v_s5_c_im, v_s5_d, v_w_glu, v_w_bs, v_conv_w, v_conv_b, v_lru_w_a, v_lru_b_a, v_lru_w_x, v_lru_b_x, v_lru_lambda, v_w_bl, v_w_out, v_g_post, v_w_ple, v_w_ple_gate):
    given = dict(x=x, p=p, g_pre=g_pre, w_in=w_in, s5_a_re=s5_a_re, s5_a_im=s5_a_im, s5_log_dt=s5_log_dt, s5_b_re=s5_b_re, s5_b_im=s5_b_im, s5_c_re=s5_c_re, s5_c_im=s5_c_im, s5_d=s5_d, w_glu=w_glu, w_bs=w_bs, conv_w=conv_w, conv_b=conv_b, lru_w_a=lru_w_a, lru_b_a=lru_b_a, lru_w_x=lru_w_x, lru_b_x=lru_b_x, lru_lambda=lru_lambda, w_bl=w_bl, w_out=w_out, g_post=g_post, w_ple=w_ple, w_ple_gate=w_ple_gate, loss_target=loss_target, m_g_pre=m_g_pre, m_w_in=m_w_in, m_s5_a_re=m_s5_a_re, m_s5_a_im=m_s5_a_im, m_s5_log_dt=m_s5_log_dt, m_s5_b_re=m_s5_b_re, m_s5_b_im=m_s5_b_im, m_s5_c_re=m_s5_c_re, m_s5_c_im=m_s5_c_im, m_s5_d=m_s5_d, m_w_glu=m_w_glu, m_w_bs=m_w_bs, m_conv_w=m_conv_w, m_conv_b=m_conv_b, m_lru_w_a=m_lru_w_a, m_lru_b_a=m_lru_b_a, m_lru_w_x=m_lru_w_x, m_lru_b_x=m_lru_b_x, m_lru_lambda=m_lru_lambda, m_w_bl=m_w_bl, m_w_out=m_w_out, m_g_post=m_g_post, m_w_ple=m_w_ple, m_w_ple_gate=m_w_ple_gate, v_g_pre=v_g_pre, v_w_in=v_w_in, v_s5_a_re=v_s5_a_re, v_s5_a_im=v_s5_a_im, v_s5_log_dt=v_s5_log_dt, v_s5_b_re=v_s5_b_re, v_s5_b_im=v_s5_b_im, v_s5_c_re=v_s5_c_re, v_s5_c_im=v_s5_c_im, v_s5_d=v_s5_d, v_w_glu=v_w_glu, v_w_bs=v_w_bs, v_conv_w=v_conv_w, v_conv_b=v_conv_b, v_lru_w_a=v_lru_w_a, v_lru_b_a=v_lru_b_a, v_lru_w_x=v_lru_w_x, v_lru_b_x=v_lru_b_x, v_lru_lambda=v_lru_lambda, v_w_bl=v_w_bl, v_w_out=v_w_out, v_g_post=v_g_post, v_w_ple=v_w_ple, v_w_ple_gate=v_w_ple_gate)
    weights = {n: given[n] for n in TWIN_WEIGHTS}
    shared = {n: given[n] for n in SHARED_INPUTS}
    per_example = {n: given[n] for n in ['x', 'p']}
    grad_fn = _jax.value_and_grad(_loss, argnums=(0, 1))

    def one_microbatch(ex, loss_target):
        ex = dict(ex)
        diff = ex.pop(TWIN_DIFF_INPUT)
        return grad_fn(weights, diff, {**shared, **ex}, loss_target)

    if N_MICROBATCH == 1:
        loss, (grad_w, grad_x) = one_microbatch(per_example, given["loss_target"])
    else:
        def body(carry, xs):
            loss_sum, grad_sum = carry
            l_k, (gw_k, gx_k) = one_microbatch(xs[0], xs[1])
            with _jax.named_scope("update"):
                return (loss_sum + l_k, _jax.tree.map(_jnp.add, grad_sum, gw_k)), gx_k

        init = (_jnp.zeros((), _jnp.float32), _jax.tree.map(_jnp.zeros_like, weights))
        (loss, grad_w), grad_x = _jax.lax.scan(body, init, (per_example, given["loss_target"]))
    with _jax.named_scope("update"):
        delta_w, new_m, new_v = {}, {}, {}
        for n in TWIN_WEIGHTS:
            delta_w[n], new_m[n], new_v[n] = _adamw(weights[n], grad_w[n], given["m_" + n], given["v_" + n])
    return (loss, grad_x, *[grad_w[n] for n in TWIN_WEIGHTS], *[delta_w[n] for n in TWIN_WEIGHTS],
            *[new_m[n] for n in TWIN_WEIGHTS], *[new_v[n] for n in TWIN_WEIGHTS])
```

```python
import functools
import math

import jax
import jax.numpy as jnp
from jax import lax
from jax.experimental import pallas as pl
from jax.experimental.pallas import tpu as pltpu

f32 = jnp.float32
bf16 = jnp.bfloat16

D_MODEL = 1024
DEPTH = 2
PLE_DIM = 256
NORM_EPS = 1e-6
S5_WIDTH = 512
S5_GROUP = 16
S5_GROUPS = 32
S5_STATE = 64
S5_NS = S5_GROUPS * S5_STATE
LRU_WIDTH = 1280
LRU_HEADS = 10
LRU_HEAD_DIM = 128
LRU_C = 8.0
CONV_WIDTH = 4
IN_COLS = 2 * S5_WIDTH + 2 * LRU_WIDTH + 2 * D_MODEL
N_DEV = 8

ADAM_LR = 0.001
ADAM_B1 = 0.9
ADAM_B2 = 0.999
ADAM_EPS = 1e-08
ADAM_WD = 0.01
ADAM_STEP = 10

LANES = 128
SUBLANES = 8
SEGS = SUBLANES
SCAN_CHUNK = 256
TOKEN_TILE = 256
MM_TILE_M = 512
HALO = SUBLANES

WEIGHTS = ['g_pre', 'w_in', 's5_a_re', 's5_a_im', 's5_log_dt', 's5_b_re', 's5_b_im', 's5_c_re', 's5_c_im',
           's5_d', 'w_glu', 'w_bs', 'conv_w', 'conv_b', 'lru_w_a', 'lru_b_a', 'lru_w_x', 'lru_b_x',
           'lru_lambda', 'w_bl', 'w_out', 'g_post', 'w_ple', 'w_ple_gate']
SHARDED = {'w_in': 2, 'w_glu': 2, 'w_bs': 2, 'conv_w': 2, 'w_bl': 1, 'w_out': 1, 'w_ple': 2, 'w_ple_gate': 1}
GATHER_BF16 = ['w_in', 'w_glu', 'w_bs', 'w_bl', 'w_out', 'w_ple', 'w_ple_gate']
REPLICATED = [n for n in WEIGHTS if n not in SHARDED]


def _sig(x):
    return 1.0 / (1.0 + jnp.exp(-x))


def _gelu_parts(x):
    k = math.sqrt(2.0 / math.pi)
    t = jnp.tanh(k * (x + 0.044715 * x * x * x))
    return t, k


def _gelu(x):
    t, _ = _gelu_parts(x)
    return 0.5 * x * (1.0 + t)


def _gelu_grad(x):
    t, k = _gelu_parts(x)
    return 0.5 * (1.0 + t) + 0.5 * x * (1.0 - t * t) * k * (1.0 + 3.0 * 0.044715 * x * x)


def _one_minus_exp(z):
    series = -z * (1.0 + z * (0.5 + z * (1.0 / 6.0 + z * (1.0 / 24.0 + z * (1.0 / 120.0)))))
    return jnp.where(z > -0.05, series, 1.0 - jnp.exp(z))


def _dot(a, b):
    return jnp.dot(a, b, preferred_element_type=f32)


def _dot_nt(a, b):
    return lax.dot_general(a, b, (((1,), (1,)), ((), ())), preferred_element_type=f32)


def _dot_tn(a, b):
    return lax.dot_general(a, b, (((0,), (0,)), ((), ())), preferred_element_type=f32)


def _S(shape, dtype=f32):
    return jax.ShapeDtypeStruct(shape, dtype)


def _full(shape):
    nd = len(shape)
    return pl.BlockSpec(shape, lambda *_: (0,) * nd)


def _rows(tile, width, col=0):
    return pl.BlockSpec((tile, width), lambda i: (i, col))


VMEM_LIMIT_BYTES = 56 * 1024 * 1024


def _cp(*semantics):
    return pltpu.CompilerParams(dimension_semantics=semantics or None, vmem_limit_bytes=VMEM_LIMIT_BYTES)


def _tile(n, want):
    t = min(n, want)
    assert n % t == 0, (n, want)
    return t


def mm_nn(a, b, name, out_dtype=f32, tn=512, tk=1024):
    M, K = a.shape
    N = b.shape[1]
    tm, tn, tk = _tile(M, MM_TILE_M), _tile(N, tn), _tile(K, tk)
    nk = K // tk

    def body(a_ref, b_ref, o_ref, acc_ref):
        k = pl.program_id(2)
        part = _dot(a_ref[...].astype(bf16), b_ref[...].astype(bf16))

        @pl.when(k == 0)
        def _():
            acc_ref[...] = part

        @pl.when(k > 0)
        def _():
            acc_ref[...] += part

        @pl.when(k == nk - 1)
        def _():
            o_ref[...] = acc_ref[...].astype(out_dtype)

    return pl.pallas_call(
        body, name=name, grid=(M // tm, N // tn, nk),
        in_specs=[pl.BlockSpec((tm, tk), lambda i, j, k: (i, k)), pl.BlockSpec((tk, tn), lambda i, j, k: (k, j))],
        out_specs=pl.BlockSpec((tm, tn), lambda i, j, k: (i, j)),
        out_shape=_S((M, N), out_dtype), scratch_shapes=[pltpu.VMEM((tm, tn), f32)],
        compiler_params=_cp("parallel", "parallel", "arbitrary"),
    )(a, b)


def mm_nt(a, b, name, out_dtype=f32, tn=512, tk=1408):
    M, K = a.shape
    N = b.shape[0]
    tm, tn, tk = _tile(M, MM_TILE_M), _tile(N, tn), _tile(K, tk)
    nk = K // tk

    def body(a_ref, b_ref, o_ref, acc_ref):
        k = pl.program_id(2)
        part = _dot_nt(a_ref[...].astype(bf16), b_ref[...].astype(bf16))

        @pl.when(k == 0)
        def _():
            acc_ref[...] = part

        @pl.when(k > 0)
        def _():
            acc_ref[...] += part

        @pl.when(k == nk - 1)
        def _():
            o_ref[...] = acc_ref[...].astype(out_dtype)

    return pl.pallas_call(
        body, name=name, grid=(M // tm, N // tn, nk),
        in_specs=[pl.BlockSpec((tm, tk), lambda i, j, k: (i, k)), pl.BlockSpec((tn, tk), lambda i, j, k: (j, k))],
        out_specs=pl.BlockSpec((tm, tn), lambda i, j, k: (i, j)),
        out_shape=_S((M, N), out_dtype), scratch_shapes=[pltpu.VMEM((tm, tn), f32)],
        compiler_params=_cp("parallel", "parallel", "arbitrary"),
    )(a, b)


def mm_tn(a, b, name, tk=512, tn=512):
    M, K = a.shape
    N = b.shape[1]
    tm, tk, tn = _tile(M, MM_TILE_M), _tile(K, tk), _tile(N, tn)

    def body(a_ref, b_ref, o_ref):
        m = pl.program_id(2)
        part = _dot_tn(a_ref[...].astype(bf16), b_ref[...].astype(bf16))

        @pl.when(m == 0)
        def _():
            o_ref[...] = part

        @pl.when(m > 0)
        def _():
            o_ref[...] += part

    return pl.pallas_call(
        body, name=name, grid=(K // tk, N // tn, M // tm),
        in_specs=[pl.BlockSpec((tm, tk), lambda i, j, m: (m, i)), pl.BlockSpec((tm, tn), lambda i, j, m: (m, j))],
        out_specs=pl.BlockSpec((tk, tn), lambda i, j, m: (i, j)),
        out_shape=_S((K, N), f32),
        compiler_params=_cp("parallel", "parallel", "arbitrary"),
    )(a, b)


def rms_fwd(x, g, name):
    T = x.shape[0]
    tm = _tile(T, TOKEN_TILE)

    def body(x_ref, g_ref, h_ref):
        xv = x_ref[...]
        r = lax.rsqrt(jnp.mean(xv * xv, axis=-1, keepdims=True) + NORM_EPS)
        h_ref[...] = (xv * r * g_ref[...]).astype(bf16)

    return pl.pallas_call(
        body, name=name, grid=(T // tm,),
        in_specs=[_rows(tm, D_MODEL), _full((1, D_MODEL))], out_specs=_rows(tm, D_MODEL),
        out_shape=_S((T, D_MODEL), bf16), compiler_params=_cp("parallel"),
    )(x, g)


def rms_bwd(gh, x, g, gres, name):
    T = x.shape[0]
    tm = _tile(T, TOKEN_TILE)

    def body(gh_ref, x_ref, g_ref, gres_ref, gx_ref, gg_ref):
        i = pl.program_id(0)
        xv = x_ref[...]
        ghv = gh_ref[...]
        r = lax.rsqrt(jnp.mean(xv * xv, axis=-1, keepdims=True) + NORM_EPS)
        nrm = xv * r
        gy = ghv * g_ref[...]
        gx_ref[...] = gres_ref[...] + r * (gy - nrm * jnp.mean(gy * nrm, axis=-1, keepdims=True))
        part = jnp.sum(ghv * nrm, axis=0, keepdims=True)

        @pl.when(i == 0)
        def _():
            gg_ref[...] = part

        @pl.when(i > 0)
        def _():
            gg_ref[...] += part

    return pl.pallas_call(
        body, name=name, grid=(T // tm,),
        in_specs=[_rows(tm, D_MODEL), _rows(tm, D_MODEL), _full((1, D_MODEL)), _rows(tm, D_MODEL)],
        out_specs=[_rows(tm, D_MODEL), _full((1, D_MODEL))],
        out_shape=[_S((T, D_MODEL)), _S((1, D_MODEL))], compiler_params=_cp("arbitrary"),
    )(gh, x, g, gres)


def _s5_discretise(a_re, a_im, log_dt, b_re_t, b_im_t):
    dt = jnp.exp(log_dt)
    mag = jnp.exp(a_re * dt)
    ab_re = mag * jnp.cos(a_im * dt)
    ab_im = mag * jnp.sin(a_im * dt)
    den = a_re * a_re + a_im * a_im
    nr, ni = ab_re - 1.0, ab_im
    z_re = (nr * a_re + ni * a_im) / den
    z_im = (ni * a_re - nr * a_im) / den
    bb_re = z_re[None] * b_re_t - z_im[None] * b_im_t
    bb_im = z_re[None] * b_im_t + z_im[None] * b_re_t
    return ab_re, ab_im, bb_re, bb_im


def s5_prep(a_re, a_im, log_dt, b_re_t, b_im_t, m, name):
    G, N = a_re.shape

    def body(are_ref, aim_ref, ldt_ref, bre_ref, bim_ref, ab_ref, pw_ref, bb_ref):
        are, aim, ldt = are_ref[...], aim_ref[...], ldt_ref[...]
        ab_re, ab_im, bb_re, bb_im = _s5_discretise(are, aim, ldt, bre_ref[...], bim_ref[...])
        ab_ref[0], ab_ref[1] = ab_re, ab_im
        bb_ref[0], bb_ref[1] = bb_re, bb_im
        dt = jnp.exp(ldt)
        for k in range(m):
            mag = jnp.exp(are * dt * (k + 1.0))
            pw_ref[0, k] = mag * jnp.cos(aim * dt * (k + 1.0))
            pw_ref[1, k] = mag * jnp.sin(aim * dt * (k + 1.0))

    return pl.pallas_call(
        body, name=name,
        out_shape=[_S((2, G, N)), _S((2, m, G, N)), _S((2, S5_GROUP, G, N))], compiler_params=_cp(),
    )(a_re, a_im, log_dt, b_re_t, b_im_t)


def s5_prep_bwd(a_re, a_im, log_dt, b_re_t, b_im_t, g_ab, g_bb, name):
    G, N = a_re.shape

    def body(are_ref, aim_ref, ldt_ref, bre_ref, bim_ref, gab_ref, gbb_ref, o_are, o_aim, o_ldt, o_bre, o_bim):
        _, vjp = jax.vjp(_s5_discretise, are_ref[...], aim_ref[...], ldt_ref[...], bre_ref[...], bim_ref[...])
        g_are, g_aim, g_ldt, g_bre, g_bim = vjp((gab_ref[0], gab_ref[1], gbb_ref[0], gbb_ref[1]))
        o_are[...], o_aim[...], o_ldt[...], o_bre[...], o_bim[...] = g_are, g_aim, g_ldt, g_bre, g_bim

    return pl.pallas_call(
        body, name=name,
        out_shape=[_S((G, N)), _S((G, N)), _S((G, 1)), _S((S5_GROUP, G, N)), _S((S5_GROUP, G, N))],
        compiler_params=_cp(),
    )(a_re, a_im, log_dt, b_re_t, b_im_t, g_ab, g_bb)


NB_S5 = S5_NS // LANES
SCAN_JB = 4


def _lanes(j):
    return slice(LANES * j, LANES * (j + 1))


def s5_fwd(usg, bd, cdt, dvec, abar, ptab, name):
    T = usg.shape[0]
    tc = _tile(T, SCAN_CHUNK)
    m = tc // SEGS
    assert ptab.shape == (2, m, S5_NS)

    def body(u_ref, bd_ref, cdt_ref, d_ref, a_ref, p_ref, ys_ref, sre_ref, sim_ref, sbf_ref, scr_re, scr_im, carry_ref):
        i = pl.program_id(0)

        @pl.when(i == 0)
        def _():
            carry_ref[...] = jnp.zeros_like(carry_ref)

        u = u_ref[...]
        bu = _dot(u.astype(bf16), bd_ref[...])
        for j in range(NB_S5):
            scr_re[j * tc:(j + 1) * tc, :] = bu[:, _lanes(j)]
            scr_im[j * tc:(j + 1) * tc, :] = bu[:, S5_NS + LANES * j:S5_NS + LANES * (j + 1)]
        for j0 in range(0, NB_S5, SCAN_JB):
            ars = [jnp.broadcast_to(a_ref[0, :, _lanes(j0 + q)], (SEGS, LANES)) for q in range(SCAN_JB)]
            ais = [jnp.broadcast_to(a_ref[1, :, _lanes(j0 + q)], (SEGS, LANES)) for q in range(SCAN_JB)]

            def kstep(k, st):
                out = []
                for q in range(SCAN_JB):
                    sr, si = st[2 * q], st[2 * q + 1]
                    rows = pl.ds((j0 + q) * tc + k, SEGS, stride=m)
                    nr = ars[q] * sr - ais[q] * si + scr_re[rows, :]
                    ni = ars[q] * si + ais[q] * sr + scr_im[rows, :]
                    scr_re[rows, :] = nr
                    scr_im[rows, :] = ni
                    out += [nr, ni]
                return tuple(out)

            ends = lax.fori_loop(0, m, kstep, tuple(jnp.zeros((SEGS, LANES), f32) for _ in range(2 * SCAN_JB)))
            for q in range(SCAN_JB):
                j = j0 + q
                er, ei = ends[2 * q], ends[2 * q + 1]
                cr, ci = carry_ref[0, :, _lanes(j)], carry_ref[1, :, _lanes(j)]
                pr, pi = p_ref[0, :, _lanes(j)], p_ref[1, :, _lanes(j)]
                amr, ami = pr[m - 1:m, :], pi[m - 1:m, :]
                for s in range(SEGS):
                    rows = slice(s * m, (s + 1) * m)
                    sr = scr_re[j * tc + s * m:j * tc + (s + 1) * m, :] + pr * cr - pi * ci
                    si = scr_im[j * tc + s * m:j * tc + (s + 1) * m, :] + pr * ci + pi * cr
                    sre_ref[j, rows, :] = sr
                    sim_ref[j, rows, :] = si
                    sbf_ref[rows, _lanes(j)] = sr.astype(bf16)
                    sbf_ref[rows, S5_NS + LANES * j:S5_NS + LANES * (j + 1)] = si.astype(bf16)
                    cr, ci = (er[s:s + 1, :] + amr * cr - ami * ci, ei[s:s + 1, :] + amr * ci + ami * cr)
                carry_ref[0, :, _lanes(j)] = cr
                carry_ref[1, :, _lanes(j)] = ci
        ys_ref[...] = _dot(sbf_ref[...], cdt_ref[...]) + d_ref[...] * u

    return pl.pallas_call(
        body, name=name, grid=(T // tc,),
        in_specs=[_rows(tc, S5_WIDTH, 0), _full((S5_WIDTH, 2 * S5_NS)), _full((2 * S5_NS, S5_WIDTH)),
                  _full((1, S5_WIDTH)), _full((2, 1, S5_NS)), _full((2, m, S5_NS))],
        out_specs=[_rows(tc, S5_WIDTH), pl.BlockSpec((NB_S5, tc, LANES), lambda i: (0, i, 0)),
                   pl.BlockSpec((NB_S5, tc, LANES), lambda i: (0, i, 0)), _rows(tc, 2 * S5_NS)],
        out_shape=[_S((T, S5_WIDTH)), _S((NB_S5, T, LANES)), _S((NB_S5, T, LANES)), _S((T, 2 * S5_NS), bf16)],
        scratch_shapes=[pltpu.VMEM((NB_S5 * tc, LANES), f32), pltpu.VMEM((NB_S5 * tc, LANES), f32),
                        pltpu.VMEM((2, 1, S5_NS), f32)],
        compiler_params=_cp("arbitrary"),
    )(usg, bd, cdt, dvec, abar, ptab)


def s5_bwd(gys, usg, s_re, s_im, bd, cdt, dvec, abar, ptab_rev, name):
    T = gys.shape[0]
    tc = _tile(T, SCAN_CHUNK)
    m = tc // SEGS
    nch = T // tc
    hb = tc // HALO

    def body(gy_ref, u_ref, sre_ref, sim_ref, hre_ref, him_ref, bd_ref, cdt_ref, d_ref, a_ref, p_ref,
             gu_ref, lam_ref, gab_ref, gd_ref, scr_re, scr_im, sh_re, sh_im, carry_ref):
        i = pl.program_id(0)

        @pl.when(i == 0)
        def _():
            carry_ref[...] = jnp.zeros_like(carry_ref)
            gab_ref[...] = jnp.zeros_like(gab_ref)
            gd_ref[...] = jnp.zeros_like(gd_ref)

        first = i == nch - 1
        gy = gy_ref[...]
        u = u_ref[...]
        gs = _dot_nt(gy.astype(bf16), cdt_ref[...])
        for j in range(NB_S5):
            scr_re[j * tc:(j + 1) * tc, :] = gs[:, _lanes(j)]
            scr_im[j * tc:(j + 1) * tc, :] = gs[:, S5_NS + LANES * j:S5_NS + LANES * (j + 1)]
            sh_re[j, 0:HALO, :] = jnp.where(first, 0.0, hre_ref[j])
            sh_im[j, 0:HALO, :] = jnp.where(first, 0.0, him_ref[j])
            sh_re[j, HALO:, :] = sre_ref[j]
            sh_im[j, HALO:, :] = sim_ref[j]
        for j0 in range(0, NB_S5, SCAN_JB):
            ars = [jnp.broadcast_to(a_ref[0, :, _lanes(j0 + q)], (SEGS, LANES)) for q in range(SCAN_JB)]
            ais = [jnp.broadcast_to(a_ref[1, :, _lanes(j0 + q)], (SEGS, LANES)) for q in range(SCAN_JB)]

            def kstep(kk, st):
                k = m - 1 - kk
                out = []
                for q in range(SCAN_JB):
                    lr, li = st[2 * q], st[2 * q + 1]
                    rows = pl.ds((j0 + q) * tc + k, SEGS, stride=m)
                    nr = ars[q] * lr + ais[q] * li + scr_re[rows, :]
                    ni = ars[q] * li - ais[q] * lr + scr_im[rows, :]
                    scr_re[rows, :] = nr
                    scr_im[rows, :] = ni
                    out += [nr, ni]
                return tuple(out)

            ends = lax.fori_loop(0, m, kstep, tuple(jnp.zeros((SEGS, LANES), f32) for _ in range(2 * SCAN_JB)))
            for q in range(SCAN_JB):
                j = j0 + q
                er, ei = ends[2 * q], ends[2 * q + 1]
                cr, ci = carry_ref[0, :, _lanes(j)], carry_ref[1, :, _lanes(j)]
                pr, pi = p_ref[0, :, _lanes(j)], p_ref[1, :, _lanes(j)]
                amr, ami = pr[0:1, :], pi[0:1, :]
                acc_re = jnp.zeros((m, LANES), f32)
                acc_im = jnp.zeros((m, LANES), f32)
                for s in reversed(range(SEGS)):
                    rows = slice(s * m, (s + 1) * m)
                    lr = scr_re[j * tc + s * m:j * tc + (s + 1) * m, :] + pr * cr + pi * ci
                    li = scr_im[j * tc + s * m:j * tc + (s + 1) * m, :] + pr * ci - pi * cr
                    lam_ref[rows, _lanes(j)] = lr.astype(bf16)
                    lam_ref[rows, S5_NS + LANES * j:S5_NS + LANES * (j + 1)] = li.astype(bf16)
                    prev = slice(HALO - 1 + s * m, HALO - 1 + (s + 1) * m)
                    pre, pim = sh_re[j, prev, :], sh_im[j, prev, :]
                    acc_re += lr * pre + li * pim
                    acc_im += li * pre - lr * pim
                    cr, ci = (er[s:s + 1, :] + amr * cr + ami * ci, ei[s:s + 1, :] + amr * ci - ami * cr)
                carry_ref[0, :, _lanes(j)] = cr
                carry_ref[1, :, _lanes(j)] = ci
                gab_ref[0, :, _lanes(j)] += jnp.sum(acc_re, axis=0, keepdims=True)
                gab_ref[1, :, _lanes(j)] += jnp.sum(acc_im, axis=0, keepdims=True)
        gu_ref[...] = (gy * d_ref[...] + _dot_nt(lam_ref[...], bd_ref[...])).astype(bf16)
        gd_ref[...] += jnp.sum(gy * u, axis=0, keepdims=True)

    rev = lambda i: (nch - 1 - i, 0)
    rev3 = lambda i: (0, nch - 1 - i, 0)
    halo3 = lambda i: (0, jnp.maximum((nch - 1 - i) * hb - 1, 0), 0)
    return pl.pallas_call(
        body, name=name, grid=(nch,),
        in_specs=[pl.BlockSpec((tc, S5_WIDTH), rev), pl.BlockSpec((tc, S5_WIDTH), rev),
                  pl.BlockSpec((NB_S5, tc, LANES), rev3), pl.BlockSpec((NB_S5, tc, LANES), rev3),
                  pl.BlockSpec((NB_S5, HALO, LANES), halo3), pl.BlockSpec((NB_S5, HALO, LANES), halo3),
                  _full((S5_WIDTH, 2 * S5_NS)), _full((2 * S5_NS, S5_WIDTH)), _full((1, S5_WIDTH)),
                  _full((2, 1, S5_NS)), _full((2, m, S5_NS))],
        out_specs=[pl.BlockSpec((tc, S5_WIDTH), rev), pl.BlockSpec((tc, 2 * S5_NS), rev),
                   _full((2, 1, S5_NS)), _full((1, S5_WIDTH))],
        out_shape=[_S((T, S5_WIDTH), bf16), _S((T, 2 * S5_NS), bf16), _S((2, 1, S5_NS)), _S((1, S5_WIDTH))],
        scratch_shapes=[pltpu.VMEM((NB_S5 * tc, LANES), f32), pltpu.VMEM((NB_S5 * tc, LANES), f32),
                        pltpu.VMEM((NB_S5, tc + HALO, LANES), f32), pltpu.VMEM((NB_S5, tc + HALO, LANES), f32),
                        pltpu.VMEM((2, 1, S5_NS), f32)],
        compiler_params=_cp("arbitrary"),
    )(gys, usg, s_re, s_im, s_re, s_im, bd, cdt, dvec, abar, ptab_rev)


def s5_post_fwd(ys, usg, wglu, wbs, name):
    T = ys.shape[0]
    tm = _tile(T, TOKEN_TILE)

    def body(ys_ref, sg_ref, wglu_ref, wbs_ref, glu_ref, zs_ref):
        glu = _dot(_gelu(ys_ref[...]).astype(bf16), wglu_ref[...])
        sg = sg_ref[...]
        y2 = glu[:, :S5_WIDTH] * _sig(glu[:, S5_WIDTH:]) * (sg * _sig(sg))
        glu_ref[...] = glu
        zs_ref[...] = _dot(y2.astype(bf16), wbs_ref[...])

    return pl.pallas_call(
        body, name=name, grid=(T // tm,),
        in_specs=[_rows(tm, S5_WIDTH), _rows(tm, S5_WIDTH, 1), _full((S5_WIDTH, 2 * S5_WIDTH)), _full((S5_WIDTH, D_MODEL))],
        out_specs=[_rows(tm, 2 * S5_WIDTH), _rows(tm, D_MODEL)],
        out_shape=[_S((T, 2 * S5_WIDTH)), _S((T, D_MODEL))], compiler_params=_cp("parallel"),
    )(ys, usg, wglu, wbs)


def s5_post_bwd(gzs, glu, usg, ys, wbs, wglu, name):
    T = ys.shape[0]
    tm = _tile(T, TOKEN_TILE)

    def body(gzs_ref, glu_ref, sg_ref, ys_ref, wbs_ref, wglu_ref, y2_ref, gglu_ref, ge_ref, gys_ref, gsg_ref):
        glu = glu_ref[...]
        a, b = glu[:, :S5_WIDTH], glu[:, S5_WIDTH:]
        sg = sg_ref[...]
        ys = ys_ref[...]
        sb, ssg = _sig(b), _sig(sg)
        silu = sg * ssg
        y2_ref[...] = (a * sb * silu).astype(bf16)
        gy2 = _dot_nt(gzs_ref[...], wbs_ref[...])
        g_a = gy2 * sb * silu
        g_b = gy2 * a * sb * (1.0 - sb) * silu
        gsg_ref[...] = (gy2 * a * sb * ssg * (1.0 + sg * (1.0 - ssg))).astype(bf16)
        gglu = jnp.concatenate([g_a, g_b], axis=1).astype(bf16)
        gglu_ref[...] = gglu
        ge_ref[...] = _gelu(ys).astype(bf16)
        gys_ref[...] = _dot_nt(gglu, wglu_ref[...]) * _gelu_grad(ys)

    return pl.pallas_call(
        body, name=name, grid=(T // tm,),
        in_specs=[_rows(tm, D_MODEL), _rows(tm, 2 * S5_WIDTH), _rows(tm, S5_WIDTH, 1), _rows(tm, S5_WIDTH),
                  _full((S5_WIDTH, D_MODEL)), _full((S5_WIDTH, 2 * S5_WIDTH))],
        out_specs=[_rows(tm, S5_WIDTH), _rows(tm, 2 * S5_WIDTH), _rows(tm, S5_WIDTH), _rows(tm, S5_WIDTH), _rows(tm, S5_WIDTH)],
        out_shape=[_S((T, S5_WIDTH), bf16), _S((T, 2 * S5_WIDTH), bf16), _S((T, S5_WIDTH), bf16), _S((T, S5_WIDTH)),
                   _S((T, S5_WIDTH), bf16)],
        compiler_params=_cp("parallel"),
    )(gzs, glu, usg, ys, wbs, wglu)


NB_LRU = LRU_WIDTH // LANES
LRU_JB = 5


def _softplus_neg(lam):
    return jnp.maximum(-lam, 0.0) + jnp.log(1.0 + jnp.exp(-jnp.abs(lam)))


def lru_fwd(lx, convw, convb, wa, wx, ba, bx, lam, name):
    T = lx.shape[0]
    tc = _tile(T, SCAN_CHUNK)
    m = tc // SEGS
    hb = tc // HALO

    def body(x_ref, xh_ref, cw_ref, cb_ref, wa_ref, wx_ref, ba_ref, bx_ref, lam_ref,
             c_ref, r_ref, i_ref, h_ref, xbuf, scr_a, scr_b, carry_ref):
        i = pl.program_id(0)

        @pl.when(i == 0)
        def _():
            carry_ref[...] = jnp.zeros_like(carry_ref)

        xbuf[0:HALO, :] = jnp.where(i == 0, 0.0, xh_ref[...])
        xbuf[HALO:, :] = x_ref[...]
        c = cb_ref[...] + cw_ref[0:1, :] * xbuf[HALO - 3:HALO - 3 + tc, :]
        for k in range(1, CONV_WIDTH):
            c = c + cw_ref[k:k + 1, :] * xbuf[HALO - 3 + k:HALO - 3 + k + tc, :]
        c_ref[...] = c
        sp = _softplus_neg(lam_ref[...])
        for j in range(NB_LRU):
            cj = c[:, _lanes(j)]
            cjb = cj.astype(bf16)
            r = _sig(_dot(cjb, wa_ref[j]) + ba_ref[:, _lanes(j)])
            g = _sig(_dot(cjb, wx_ref[j]) + bx_ref[:, _lanes(j)])
            r_ref[:, _lanes(j)] = r
            i_ref[:, _lanes(j)] = g
            log_a = -LRU_C * r * sp[:, _lanes(j)]
            scr_a[j * tc:(j + 1) * tc, :] = jnp.exp(log_a)
            scr_b[j * tc:(j + 1) * tc, :] = jnp.sqrt(_one_minus_exp(2.0 * log_a)) * (g * cj)
        for j0 in range(0, NB_LRU, LRU_JB):
            def kstep(k, st):
                out = []
                for q in range(LRU_JB):
                    hh, ac = st[2 * q], st[2 * q + 1]
                    rows = pl.ds((j0 + q) * tc + k, SEGS, stride=m)
                    a = scr_a[rows, :]
                    hh = a * hh + scr_b[rows, :]
                    ac = a * ac
                    scr_b[rows, :] = hh
                    scr_a[rows, :] = ac
                    out += [hh, ac]
                return tuple(out)

            init = tuple(jnp.zeros((SEGS, LANES), f32) if q % 2 == 0 else jnp.ones((SEGS, LANES), f32)
                         for q in range(2 * LRU_JB))
            ends = lax.fori_loop(0, m, kstep, init)
            for q in range(LRU_JB):
                j = j0 + q
                eh, ea = ends[2 * q], ends[2 * q + 1]
                cr = carry_ref[:, _lanes(j)]
                for s in range(SEGS):
                    rows = slice(s * m, (s + 1) * m)
                    loc = slice(j * tc + s * m, j * tc + (s + 1) * m)
                    h_ref[rows, _lanes(j)] = scr_b[loc, :] + scr_a[loc, :] * cr
                    cr = eh[s:s + 1, :] + ea[s:s + 1, :] * cr
                carry_ref[:, _lanes(j)] = cr

    wide = lambda: _rows(tc, LRU_WIDTH)
    return pl.pallas_call(
        body, name=name, grid=(T // tc,),
        in_specs=[wide(), pl.BlockSpec((HALO, LRU_WIDTH), lambda i: (jnp.maximum(i * hb - 1, 0), 0)),
                  _full((CONV_WIDTH, LRU_WIDTH)), _full((1, LRU_WIDTH)),
                  _full((LRU_HEADS, LRU_HEAD_DIM, LRU_HEAD_DIM)), _full((LRU_HEADS, LRU_HEAD_DIM, LRU_HEAD_DIM)),
                  _full((1, LRU_WIDTH)), _full((1, LRU_WIDTH)), _full((1, LRU_WIDTH))],
        out_specs=[wide(), wide(), wide(), wide()],
        out_shape=[_S((T, LRU_WIDTH))] * 4,
        scratch_shapes=[pltpu.VMEM((tc + HALO, LRU_WIDTH), f32), pltpu.VMEM((NB_LRU * tc, LANES), f32),
                        pltpu.VMEM((NB_LRU * tc, LANES), f32), pltpu.VMEM((1, LRU_WIDTH), f32)],
        compiler_params=_cp("arbitrary"),
    )(lx, lx, convw, convb, wa, wx, ba, bx, lam)


def lru_bwd(gh, h, c, r, gi, lx, convw, wa, wx, lam, name):
    T = gh.shape[0]
    tc = _tile(T, SCAN_CHUNK)
    m = tc // SEGS
    nch = T // tc
    hb = tc // HALO
    tch = tc + HALO

    def body(gh_ref, h_ref, hh_ref, c_ref, r_ref, i_ref, x_ref, xh_ref, cw_ref, wa_ref, wx_ref, lam_ref,
             glx_ref, gwa_ref, gwx_ref, gba_ref, gbx_ref, glam_ref, gcb_ref, gcw_ref,
             scr_a, scr_m, hbuf, xbuf, gcbuf, carry_ref):
        i = pl.program_id(0)

        @pl.when(i == 0)
        def _():
            carry_ref[...] = jnp.zeros_like(carry_ref)
            gcbuf[...] = jnp.zeros_like(gcbuf)
            for ref in (gwa_ref, gwx_ref, gba_ref, gbx_ref, glam_ref, gcb_ref, gcw_ref):
                ref[...] = jnp.zeros_like(ref)

        first = i == nch - 1
        hbuf[0:HALO, :] = jnp.where(first, 0.0, hh_ref[...])
        hbuf[HALO:, :] = h_ref[...]
        xbuf[0:HALO, :] = jnp.where(first, 0.0, xh_ref[...])
        xbuf[HALO:, :] = x_ref[...]
        lam_v = lam_ref[...]
        sp = _softplus_neg(lam_v)
        for j in range(NB_LRU):
            a = jnp.exp(-LRU_C * r_ref[:, _lanes(j)] * sp[:, _lanes(j)])
            scr_a[j * tc:(j + 1) * tc, :] = a
            scr_m[j * tch:j * tch + tc, :] = a * gh_ref[:, _lanes(j)]
        for j0 in range(0, NB_LRU, LRU_JB):
            def kstep(kk, st):
                k = m - 1 - kk
                out = []
                for q in range(LRU_JB):
                    mu, ac = st[2 * q], st[2 * q + 1]
                    rows_a = pl.ds((j0 + q) * tc + k, SEGS, stride=m)
                    rows_m = pl.ds((j0 + q) * tch + k, SEGS, stride=m)
                    a = scr_a[rows_a, :]
                    mu = a * mu + scr_m[rows_m, :]
                    ac = a * ac
                    scr_m[rows_m, :] = mu
                    scr_a[rows_a, :] = ac
                    out += [mu, ac]
                return tuple(out)

            init = tuple(jnp.zeros((SEGS, LANES), f32) if q % 2 == 0 else jnp.ones((SEGS, LANES), f32)
                         for q in range(2 * LRU_JB))
            ends = lax.fori_loop(0, m, kstep, init)
            for q in range(LRU_JB):
                j = j0 + q
                em, ea = ends[2 * q], ends[2 * q + 1]
                cr = carry_ref[:, _lanes(j)]
                scr_m[j * tch + tc:j * tch + tc + 1, :] = cr
                for s in reversed(range(SEGS)):
                    rows_m = slice(j * tch + s * m, j * tch + (s + 1) * m)
                    rows_a = slice(j * tc + s * m, j * tc + (s + 1) * m)
                    scr_m[rows_m, :] = scr_m[rows_m, :] + scr_a[rows_a, :] * cr
                    cr = em[s:s + 1, :] + ea[s:s + 1, :] * cr
                carry_ref[:, _lanes(j)] = cr
        sneg = _sig(-lam_v)
        for j in range(NB_LRU):
            ln = _lanes(j)
            lamt = gh_ref[:, ln] + scr_m[j * tch + 1:j * tch + tc + 1, :]
            rj, ij, cj = r_ref[:, ln], i_ref[:, ln], c_ref[:, ln]
            log_a = -LRU_C * rj * sp[:, ln]
            a = jnp.exp(log_a)
            mult = jnp.sqrt(_one_minus_exp(2.0 * log_a))
            g_a = lamt * hbuf[HALO - 1:HALO - 1 + tc, ln]
            g_mult = lamt * ij * cj
            g_i = lamt * mult * cj
            g_c = lamt * mult * ij
            g_log_a = g_a * a - g_mult * a * a / mult
            glam_ref[:, ln] += jnp.sum(g_log_a * rj, axis=0, keepdims=True) * LRU_C * sneg[:, ln]
            g_ra = g_log_a * (-LRU_C) * sp[:, ln] * rj * (1.0 - rj)
            g_ia = g_i * ij * (1.0 - ij)
            gba_ref[:, ln] += jnp.sum(g_ra, axis=0, keepdims=True)
            gbx_ref[:, ln] += jnp.sum(g_ia, axis=0, keepdims=True)
            cjb, grb, gib = cj.astype(bf16), g_ra.astype(bf16), g_ia.astype(bf16)
            gwa_ref[j] += _dot_tn(cjb, grb)
            gwx_ref[j] += _dot_tn(cjb, gib)
            g_c = g_c + _dot_nt(grb, wa_ref[j]) + _dot_nt(gib, wx_ref[j])
            gcbuf[0:tc, ln] = g_c
            gcb_ref[:, ln] += jnp.sum(g_c, axis=0, keepdims=True)
        gc = gcbuf[0:tc, :]
        glx = cw_ref[CONV_WIDTH - 1:CONV_WIDTH, :] * gc
        gcw_ref[CONV_WIDTH - 1:CONV_WIDTH, :] += jnp.sum(gc * xbuf[HALO:HALO + tc, :], axis=0, keepdims=True)
        for k in range(CONV_WIDTH - 1):
            off = CONV_WIDTH - 1 - k
            glx = glx + cw_ref[k:k + 1, :] * gcbuf[off:off + tc, :]
            gcw_ref[k:k + 1, :] += jnp.sum(gc * xbuf[HALO - off:HALO - off + tc, :], axis=0, keepdims=True)
        glx_ref[...] = glx.astype(bf16)
        gcbuf[tc:tc + HALO, :] = gcbuf[0:HALO, :]

    rev = lambda i: (nch - 1 - i, 0)
    halo = lambda i: (jnp.maximum((nch - 1 - i) * hb - 1, 0), 0)
    wide = lambda: pl.BlockSpec((tc, LRU_WIDTH), rev)
    vec = lambda: _full((1, LRU_WIDTH))
    hd = lambda: _full((LRU_HEADS, LRU_HEAD_DIM, LRU_HEAD_DIM))
    return pl.pallas_call(
        body, name=name, grid=(nch,),
        in_specs=[wide(), wide(), pl.BlockSpec((HALO, LRU_WIDTH), halo), wide(), wide(), wide(), wide(),
                  pl.BlockSpec((HALO, LRU_WIDTH), halo), _full((CONV_WIDTH, LRU_WIDTH)), hd(), hd(), vec()],
        out_specs=[wide(), hd(), hd(), vec(), vec(), vec(), vec(), _full((CONV_WIDTH, LRU_WIDTH))],
        out_shape=[_S((T, LRU_WIDTH), bf16), _S((LRU_HEADS, LRU_HEAD_DIM, LRU_HEAD_DIM)),
                   _S((LRU_HEADS, LRU_HEAD_DIM, LRU_HEAD_DIM)), _S((1, LRU_WIDTH)), _S((1, LRU_WIDTH)),
                   _S((1, LRU_WIDTH)), _S((1, LRU_WIDTH)), _S((CONV_WIDTH, LRU_WIDTH))],
        scratch_shapes=[pltpu.VMEM((NB_LRU * tc, LANES), f32), pltpu.VMEM((NB_LRU * tch, LANES), f32),
                        pltpu.VMEM((tc + HALO, LRU_WIDTH), f32), pltpu.VMEM((tc + HALO, LRU_WIDTH), f32),
                        pltpu.VMEM((tc + HALO, LRU_WIDTH), f32), pltpu.VMEM((1, LRU_WIDTH), f32)],
        compiler_params=_cp("arbitrary"),
    )(gh, h, h, c, r, gi, lx, lx, convw, wa, wx, lam)


def merge_fwd(h, lg, zs, gsl, x, p, wbl, wout, gpost, wple, wpg, name):
    T = x.shape[0]
    tm = _tile(T, TOKEN_TILE)

    def body(h_ref, lg_ref, zs_ref, gs_ref, gl_ref, x_ref, p_ref, wbl_ref, wout_ref, gp_ref, wple_ref, wpg_ref,
             zl_ref, mix_ref, q_ref, pe_ref, xo_ref):
        lg_v = lg_ref[...]
        yl = h_ref[...] * (lg_v * _sig(lg_v))
        zl = _dot(yl.astype(bf16), wbl_ref[...])
        merged = _sig(gs_ref[...]) * zs_ref[...] + _sig(gl_ref[...]) * zl
        mix = _dot(merged.astype(bf16), wout_ref[...])
        r2 = lax.rsqrt(jnp.mean(mix * mix, axis=-1, keepdims=True) + NORM_EPS)
        x1 = x_ref[...] + mix * r2 * gp_ref[...]
        q = _dot(x1.astype(bf16), wpg_ref[...])
        pe = _dot(p_ref[...].astype(bf16), wple_ref[...])
        zl_ref[...], mix_ref[...], q_ref[...], pe_ref[...] = zl, mix, q, pe
        xo_ref[...] = x1 + pe * _sig(q)

    dm = lambda: _rows(tm, D_MODEL)
    return pl.pallas_call(
        body, name=name, grid=(T // tm,),
        in_specs=[_rows(tm, LRU_WIDTH), _rows(tm, LRU_WIDTH), dm(), _rows(tm, D_MODEL, 0), _rows(tm, D_MODEL, 1), dm(),
                  _rows(tm, PLE_DIM), _full((LRU_WIDTH, D_MODEL)), _full((D_MODEL, D_MODEL)), _full((1, D_MODEL)),
                  _full((PLE_DIM, D_MODEL)), _full((D_MODEL, D_MODEL))],
        out_specs=[dm(), dm(), dm(), dm(), dm()],
        out_shape=[_S((T, D_MODEL))] * 5, compiler_params=_cp("parallel"),
    )(h, lg, zs, gsl, gsl, x, p, wbl, wout, gpost, wple, wpg)


def merge_bwd(gx2, q, pe, mix, x, zl, zs, gsl, h, lg, wpg, wout, wbl, gpost, name):
    T = x.shape[0]
    tm = _tile(T, TOKEN_TILE)

    def body(gx2_ref, q_ref, pe_ref, mix_ref, x_ref, zl_ref, zs_ref, gs_ref, gl_ref, h_ref, lg_ref,
             wpg_ref, wout_ref, wbl_ref, gp_ref,
             gres_ref, gpe_ref, gq_ref, x1_ref, gmix_ref, mrg_ref, gzl_ref, yl_ref, gzs_ref, ggsl_ref, gh_ref, glg_ref,
             ggp_ref):
        i = pl.program_id(0)
        gx2 = gx2_ref[...]
        sq = _sig(q_ref[...])
        pe = pe_ref[...]
        gpe_ref[...] = (gx2 * sq).astype(bf16)
        gq = (gx2 * pe * sq * (1.0 - sq)).astype(bf16)
        gq_ref[...] = gq
        mix = mix_ref[...]
        gp = gp_ref[...]
        r2 = lax.rsqrt(jnp.mean(mix * mix, axis=-1, keepdims=True) + NORM_EPS)
        nrm = mix * r2
        x1_ref[...] = (x_ref[...] + nrm * gp).astype(bf16)
        gx1 = gx2 + _dot_nt(gq, wpg_ref[...])
        gres_ref[...] = gx1
        part = jnp.sum(gx1 * nrm, axis=0, keepdims=True)

        @pl.when(i == 0)
        def _():
            ggp_ref[...] = part

        @pl.when(i > 0)
        def _():
            ggp_ref[...] += part

        gy = gx1 * gp
        gmix = (r2 * (gy - nrm * jnp.mean(gy * nrm, axis=-1, keepdims=True))).astype(bf16)
        gmix_ref[...] = gmix
        gmerged = _dot_nt(gmix, wout_ref[...])
        zs, zl = zs_ref[...], zl_ref[...]
        ss, sl = _sig(gs_ref[...]), _sig(gl_ref[...])
        mrg_ref[...] = (ss * zs + sl * zl).astype(bf16)
        gzs_ref[...] = (gmerged * ss).astype(bf16)
        gzl = (gmerged * sl).astype(bf16)
        gzl_ref[...] = gzl
        ggsl_ref[:, :D_MODEL] = (gmerged * zs * ss * (1.0 - ss)).astype(bf16)
        ggsl_ref[:, D_MODEL:] = (gmerged * zl * sl * (1.0 - sl)).astype(bf16)
        lg_v, hv = lg_ref[...], h_ref[...]
        slg = _sig(lg_v)
        silu = lg_v * slg
        yl_ref[...] = (hv * silu).astype(bf16)
        gyl = _dot_nt(gzl, wbl_ref[...])
        gh_ref[...] = gyl * silu
        glg_ref[...] = (gyl * hv * slg * (1.0 + lg_v * (1.0 - slg))).astype(bf16)

    dm = lambda: _rows(tm, D_MODEL)
    lw = lambda: _rows(tm, LRU_WIDTH)
    return pl.pallas_call(
        body, name=name, grid=(T // tm,),
        in_specs=[dm(), dm(), dm(), dm(), dm(), dm(), dm(), _rows(tm, D_MODEL, 0), _rows(tm, D_MODEL, 1), lw(), lw(),
                  _full((D_MODEL, D_MODEL)), _full((D_MODEL, D_MODEL)), _full((LRU_WIDTH, D_MODEL)), _full((1, D_MODEL))],
        out_specs=[dm(), dm(), dm(), dm(), dm(), dm(), dm(), lw(), dm(), _rows(tm, 2 * D_MODEL), lw(), lw(),
                   _full((1, D_MODEL))],
        out_shape=[_S((T, D_MODEL)), _S((T, D_MODEL), bf16), _S((T, D_MODEL), bf16), _S((T, D_MODEL), bf16),
                   _S((T, D_MODEL), bf16), _S((T, D_MODEL), bf16), _S((T, D_MODEL), bf16), _S((T, LRU_WIDTH), bf16),
                   _S((T, D_MODEL), bf16), _S((T, 2 * D_MODEL), bf16), _S((T, LRU_WIDTH)), _S((T, LRU_WIDTH), bf16),
                   _S((1, D_MODEL))],
        compiler_params=_cp("arbitrary"),
    )(gx2, q, pe, mix, x, zl, zs, gsl, gsl, h, lg, wpg, wout, wbl, gpost)


def loss_head(y, target, name):
    T = y.shape[0]
    tm = _tile(T, TOKEN_TILE)

    def body(y_ref, t_ref, l_ref, g_ref):
        i = pl.program_id(0)
        e = y_ref[...] - t_ref[...]
        g_ref[...] = e * (1.0 / D_MODEL)
        part = 0.5 * jnp.sum(jnp.sum(e * e, axis=-1, keepdims=True) * (1.0 / D_MODEL), axis=0, keepdims=True)

        @pl.when(i == 0)
        def _():
            l_ref[...] = part

        @pl.when(i > 0)
        def _():
            l_ref[...] += part

    return pl.pallas_call(
        body, name=name, grid=(T // tm,),
        in_specs=[_rows(tm, D_MODEL), _rows(tm, D_MODEL)], out_specs=[_full((1, 1)), _rows(tm, D_MODEL)],
        out_shape=[_S((1, 1)), _S((T, D_MODEL))],
        compiler_params=_cp("arbitrary"),
    )(y, target)


def _block_diag_b(bb_t):
    eye = jnp.eye(S5_GROUPS, dtype=f32)
    d = eye[:, None, :, None] * jnp.transpose(bb_t, (1, 0, 2))[:, :, None, :]
    return d.reshape(S5_WIDTH, S5_NS)


def _block_diag_c(c):
    eye = jnp.eye(S5_GROUPS, dtype=f32)
    d = eye[:, None, :, None] * jnp.transpose(c, (0, 2, 1))[:, :, None, :]
    return d.reshape(S5_NS, S5_WIDTH)


def _diag_blocks(dense, rows_per, cols_per):
    d4 = dense.reshape(S5_GROUPS, rows_per, S5_GROUPS, cols_per)
    eye = jnp.eye(S5_GROUPS, dtype=f32)
    return jnp.sum(d4 * eye[:, None, :, None], axis=2)


def _s5_operands(w, m, tag):
    b_re_t = jnp.transpose(w['s5_b_re'], (2, 0, 1))
    b_im_t = jnp.transpose(w['s5_b_im'], (2, 0, 1))
    ldt = w['s5_log_dt'][:, None]
    ab, pw, bb = s5_prep(w['s5_a_re'], w['s5_a_im'], ldt, b_re_t, b_im_t, m, "s5_prep" + tag)
    abar = ab.reshape(2, 1, S5_NS)
    ptab = pw.reshape(2, m, S5_NS)
    bd = jnp.concatenate([_block_diag_b(bb[0]), _block_diag_b(bb[1])], axis=1).astype(bf16)
    cdt = jnp.concatenate([_block_diag_c(w['s5_c_re']), -_block_diag_c(w['s5_c_im'])], axis=0).astype(bf16)
    return dict(abar=abar, ptab=ptab, ptab_rev=ptab[:, ::-1, :], bd=bd, cdt=cdt, dvec=w['s5_d'][None, :],
                prep_in=(w['s5_a_re'], w['s5_a_im'], ldt, b_re_t, b_im_t))


def layer_fwd(x, p, w, tag):
    T = x.shape[0]
    m = _tile(T, SCAN_CHUNK) // SEGS
    s5 = _s5_operands(w, m, tag)
    h_bf = rms_fwd(x, w['g_pre'][None, :], "rms_fwd" + tag)
    win = w['w_in']
    usg = mm_nn(h_bf, win[:, :2 * S5_WIDTH], "proj_s5" + tag)
    lx = mm_nn(h_bf, win[:, 2 * S5_WIDTH:2 * S5_WIDTH + LRU_WIDTH], "proj_lx" + tag, tn=640)
    lg = mm_nn(h_bf, win[:, 2 * S5_WIDTH + LRU_WIDTH:2 * S5_WIDTH + 2 * LRU_WIDTH], "proj_lg" + tag, tn=640)
    gsl = mm_nn(h_bf, win[:, 2 * S5_WIDTH + 2 * LRU_WIDTH:], "proj_gate" + tag)
    ys, s_re, s_im, s_bf = s5_fwd(usg, s5['bd'], s5['cdt'], s5['dvec'], s5['abar'], s5['ptab'], "s5_fwd" + tag)
    glu, zs = s5_post_fwd(ys, usg, w['w_glu'], w['w_bs'], "s5_post_fwd" + tag)
    wa, wx = w['lru_w_a'].astype(bf16), w['lru_w_x'].astype(bf16)
    c, r, gi, hs = lru_fwd(lx, w['conv_w'], w['conv_b'][None, :], wa, wx, w['lru_b_a'][None, :], w['lru_b_x'][None, :],
                           w['lru_lambda'][None, :], "lru_fwd" + tag)
    zl, mix, q, pe, x_out = merge_fwd(hs, lg, zs, gsl, x, p, w['w_bl'], w['w_out'], w['g_post'][None, :],
                                      w['w_ple'], w['w_ple_gate'], "merge_fwd" + tag)
    saved = dict(x=x, p=p, h_bf=h_bf, usg=usg, lx=lx, lg=lg, gsl=gsl, ys=ys, s_re=s_re, s_im=s_im, s_bf=s_bf, glu=glu,
                 zs=zs, c=c, r=r, gi=gi, hs=hs, zl=zl, mix=mix, q=q, pe=pe, s5=s5, wa=wa, wx=wx)
    return x_out, saved


def layer_bwd(gx_out, w, sv, tag):
    s5 = sv['s5']
    (gres, gpe, gq, x1_bf, gmix, merged, gzl, yl, gzs, ggsl, g_h, g_lg, g_gpost) = merge_bwd(
        gx_out, sv['q'], sv['pe'], sv['mix'], sv['x'], sv['zl'], sv['zs'], sv['gsl'], sv['hs'], sv['lg'],
        w['w_ple_gate'], w['w_out'], w['w_bl'], w['g_post'][None, :], "merge_bwd" + tag)
    g = {}
    g['w_ple'] = mm_tn(sv['p'], gpe, "gw_ple" + tag, tk=256)
    g['w_ple_gate'] = mm_tn(x1_bf, gq, "gw_ple_gate" + tag)
    g['w_out'] = mm_tn(merged, gmix, "gw_out" + tag)
    g['w_bl'] = mm_tn(yl, gzl, "gw_bl" + tag, tk=640)
    g['g_post'] = g_gpost[0]
    (g_lx, g_wa, g_wx, g_ba, g_bx, g_lam, g_cb, g_cw) = lru_bwd(
        g_h, sv['hs'], sv['c'], sv['r'], sv['gi'], sv['lx'], w['conv_w'], sv['wa'], sv['wx'],
        w['lru_lambda'][None, :], "lru_bwd" + tag)
    g['lru_w_a'], g['lru_w_x'] = g_wa, g_wx
    g['lru_b_a'], g['lru_b_x'], g['lru_lambda'], g['conv_b'], g['conv_w'] = g_ba[0], g_bx[0], g_lam[0], g_cb[0], g_cw
    y2, gglu, ge, g_ys, g_sg = s5_post_bwd(gzs, sv['glu'], sv['usg'], sv['ys'], w['w_bs'], w['w_glu'], "s5_post_bwd" + tag)
    g['w_bs'] = mm_tn(y2, gzs, "gw_bs" + tag)
    g['w_glu'] = mm_tn(ge, gglu, "gw_glu" + tag)
    g_u, lam_bf, g_ab, g_d = s5_bwd(g_ys, sv['usg'], sv['s_re'], sv['s_im'], s5['bd'], s5['cdt'], s5['dvec'],
                                    s5['abar'], s5['ptab_rev'], "s5_bwd" + tag)
    g['s5_d'] = g_d[0]
    g_cdt = mm_tn(sv['s_bf'], g_ys, "gw_s5c" + tag, tk=1024)
    g['s5_c_re'] = jnp.transpose(_diag_blocks(g_cdt[:S5_NS], S5_STATE, S5_GROUP), (0, 2, 1))
    g['s5_c_im'] = -jnp.transpose(_diag_blocks(g_cdt[S5_NS:], S5_STATE, S5_GROUP), (0, 2, 1))
    g_bd = mm_tn(sv['usg'][:, :S5_WIDTH], lam_bf, "gw_s5b" + tag, tn=1024)
    g_bb = jnp.stack([jnp.transpose(_diag_blocks(g_bd[:, :S5_NS], S5_GROUP, S5_STATE), (1, 0, 2)),
                      jnp.transpose(_diag_blocks(g_bd[:, S5_NS:], S5_GROUP, S5_STATE), (1, 0, 2))])
    g_are, g_aim, g_ldt, g_bre_t, g_bim_t = s5_prep_bwd(*s5['prep_in'], g_ab.reshape(2, S5_GROUPS, S5_STATE), g_bb,
                                                       "s5_prep_bwd" + tag)
    g['s5_a_re'], g['s5_a_im'], g['s5_log_dt'] = g_are, g_aim, g_ldt
    g['s5_b_re'] = jnp.transpose(g_bre_t, (1, 2, 0))
    g['s5_b_im'] = jnp.transpose(g_bim_t, (1, 2, 0))
    gproj = jnp.concatenate([g_u, g_sg, g_lx, g_lg, ggsl], axis=1)
    g['w_in'] = mm_tn(sv['h_bf'], gproj, "gw_in" + tag, tn=1408)
    gh = mm_nt(gproj, w['w_in'], "g_h" + tag)
    gx, g_gpre = rms_bwd(gh, sv['x'], w['g_pre'][None, :], gres, "rms_bwd" + tag)
    g['g_pre'] = g_gpre[0]
    return gx, g


def _as_2d(a):
    return a.reshape((-1, a.shape[-1])) if a.ndim > 1 else a.reshape((1, -1))


def adamw(w, g, m, v, name):
    shape = w.shape
    w2, g2, m2, v2 = _as_2d(w), _as_2d(g), _as_2d(m), _as_2d(v)
    R, C = w2.shape
    tr = R
    for cand in (512, 256, 128, 64, 32, 16, 8):
        if R % cand == 0:
            tr = cand
            break
    bc1 = 1.0 - ADAM_B1 ** ADAM_STEP
    bc2 = 1.0 - ADAM_B2 ** ADAM_STEP

    def body(w_ref, g_ref, m_ref, v_ref, d_ref, nm_ref, nv_ref):
        gv = g_ref[...]
        nm = ADAM_B1 * m_ref[...] + (1.0 - ADAM_B1) * gv
        nv = ADAM_B2 * v_ref[...] + (1.0 - ADAM_B2) * (gv * gv)
        nm_ref[...] = nm
        nv_ref[...] = nv
        d_ref[...] = -ADAM_LR * ((nm / bc1) / (jnp.sqrt(nv / bc2) + ADAM_EPS) + ADAM_WD * w_ref[...])

    spec = lambda: pl.BlockSpec((tr, C), lambda i: (i, 0))
    d, nm, nv = pl.pallas_call(
        body, name=name, grid=(R // tr,), in_specs=[spec() for _ in range(4)], out_specs=[spec() for _ in range(3)],
        out_shape=[_S((R, C))] * 3, compiler_params=_cp("parallel"),
    )(w2, g2, m2, v2)
    return d.reshape(shape), nm.reshape(shape), nv.reshape(shape)


MESH = pl.DeviceIdType.MESH
ANY = pl.BlockSpec(memory_space=pl.ANY)
FLAT_COLS = 1024
FLAT_ROW_TILE = 256


def _place():
    return lax.axis_index("x"), lax.axis_index("y"), lax.axis_index("c")


def _other_chips(mx, my):
    return [(1 - mx, my), (mx, 1 - my), (1 - mx, 1 - my)]


def all_gather(x, name):
    def body(x_ref, out_ref, send_sems, recv_sems, local_sem):
        mx, my, mc = _place()
        me, sibling = (mx, my, mc), (mx, my, 1 - mc)
        chips = _other_chips(mx, my)

        def slot(px, py, pc):
            return out_ref.at[4 * px + 2 * py + pc]

        def copy(k, block, to, src=None):
            return pltpu.make_async_remote_copy(
                src_ref=slot(*block) if src is None else src, dst_ref=slot(*block),
                send_sem=send_sems.at[k], recv_sem=recv_sems.at[k], device_id=to, device_id_type=MESH)

        mine = pltpu.make_async_copy(x_ref, slot(*me), local_sem)
        mine.start()
        first = [copy(0, me, sibling, src=x_ref)]
        first += [copy(1 + j, me, (*chip, mc), src=x_ref) for j, chip in enumerate(chips)]
        for cp in first:
            cp.start()
        passed = [copy(4 + j, (*chip, mc), sibling) for j, chip in enumerate(chips)]
        for j, chip in enumerate(chips):
            copy(1 + j, (*chip, mc), me).wait_recv()
            passed[j].start()
        copy(0, sibling, me).wait_recv()
        for j, chip in enumerate(chips):
            copy(4 + j, (*chip, 1 - mc), me).wait_recv()
        for cp in first + passed:
            cp.wait_send()
        mine.wait()

    return pl.pallas_call(
        body, name=name, out_shape=_S((N_DEV,) + x.shape, x.dtype), in_specs=[ANY], out_specs=ANY,
        scratch_shapes=[pltpu.SemaphoreType.DMA((7,)), pltpu.SemaphoreType.DMA((7,)), pltpu.SemaphoreType.DMA],
    )(x)


def exchange_sibling(g, name):
    _, R, C = g.shape

    def body(g_ref, mine_ref, recv_ref, send_sems, recv_sems, local_sems):
        mx, my, mc = _place()
        sibling = (mx, my, 1 - mc)
        local = [pltpu.make_async_copy(g_ref.at[2 * k + mc], mine_ref.at[k], local_sems.at[k]) for k in range(4)]
        remote = [pltpu.make_async_remote_copy(
            src_ref=g_ref.at[2 * k + 1 - mc], dst_ref=recv_ref.at[k], send_sem=send_sems.at[k],
            recv_sem=recv_sems.at[k], device_id=sibling, device_id_type=MESH) for k in range(4)]
        for cp in remote + local:
            cp.start()
        for cp in remote + local:
            cp.wait()

    return pl.pallas_call(
        body, name=name, out_shape=[_S((4, R, C), g.dtype)] * 2, in_specs=[ANY], out_specs=[ANY, ANY],
        scratch_shapes=[pltpu.SemaphoreType.DMA((4,)), pltpu.SemaphoreType.DMA((4,)), pltpu.SemaphoreType.DMA((4,))],
    )(g)


def exchange_chips(a, name):
    _, R, C = a.shape

    def body(a_ref, own_ref, recv_ref, send_sems, recv_sems, local_sem):
        mx, my, mc = _place()
        local = pltpu.make_async_copy(a_ref.at[2 * mx + my], own_ref, local_sem)
        remote = [pltpu.make_async_remote_copy(
            src_ref=a_ref.at[2 * px + py], dst_ref=recv_ref.at[j], send_sem=send_sems.at[j],
            recv_sem=recv_sems.at[j], device_id=(px, py, mc), device_id_type=MESH)
            for j, (px, py) in enumerate(_other_chips(mx, my))]
        for cp in remote + [local]:
            cp.start()
        for cp in remote + [local]:
            cp.wait()

    return pl.pallas_call(
        body, name=name, out_shape=[_S((R, C), a.dtype), _S((3, R, C), a.dtype)], in_specs=[ANY], out_specs=[ANY, ANY],
        scratch_shapes=[pltpu.SemaphoreType.DMA((3,)), pltpu.SemaphoreType.DMA((3,)), pltpu.SemaphoreType.DMA],
    )(a)


def add_pairs(a, b, name):
    K4, R, C = a.shape
    tr = _tile(R, FLAT_ROW_TILE)

    def body(a_ref, b_ref, o_ref):
        o_ref[...] = a_ref[...] + b_ref[...]

    spec = lambda: pl.BlockSpec((1, tr, C), lambda k, i: (k, i, 0))
    return pl.pallas_call(
        body, name=name, grid=(K4, R // tr), in_specs=[spec(), spec()], out_specs=spec(),
        out_shape=_S(a.shape, a.dtype), compiler_params=_cp("parallel", "parallel"),
    )(a, b)


def add_chips(own, theirs, name):
    R, C = own.shape
    tr = _tile(R, FLAT_ROW_TILE)

    def body(o_ref, t_ref, out_ref):
        out_ref[...] = ((o_ref[...] + t_ref[0]) + t_ref[1]) + t_ref[2]

    return pl.pallas_call(
        body, name=name, grid=(R // tr,),
        in_specs=[pl.BlockSpec((tr, C), lambda i: (i, 0)), pl.BlockSpec((3, tr, C), lambda i: (0, i, 0))],
        out_specs=pl.BlockSpec((tr, C), lambda i: (i, 0)), out_shape=_S((R, C), own.dtype),
        compiler_params=_cp("parallel"),
    )(own, theirs)


def _round_up(n, q):
    return (n + q - 1) // q * q


def _shard_to_rows(full, axis):
    shp = full.shape
    s = shp[axis] // N_DEV
    cut = full.reshape(shp[:axis] + (N_DEV, s) + shp[axis + 1:])
    return jnp.moveaxis(cut, axis, 0).reshape(N_DEV, -1)


def _rows_to_full(rows, shard_shape, axis):
    parts = rows.reshape((N_DEV,) + tuple(shard_shape))
    moved = jnp.moveaxis(parts, 0, axis)
    shp = list(shard_shape)
    shp[axis] *= N_DEV
    return moved.reshape(tuple(shp))


def _flat_rows(vec_rows, n_pad):
    k, n = vec_rows.shape
    return jnp.pad(vec_rows, ((0, 0), (0, n_pad - n))).reshape(k, n_pad // FLAT_COLS, FLAT_COLS)


def kernel(x, p, g_pre, w_in, s5_a_re, s5_a_im, s5_log_dt, s5_b_re, s5_b_im, s5_c_re, s5_c_im, s5_d, w_glu, w_bs, conv_w, conv_b, lru_w_a, lru_b_a, lru_w_x, lru_b_x, lru_lambda, w_bl, w_out, g_post, w_ple, w_ple_gate, loss_target, m_g_pre, m_w_in, m_s5_a_re, m_s5_a_im, m_s5_log_dt, m_s5_b_re, m_s5_b_im, m_s5_c_re, m_s5_c_im, m_s5_d, m_w_glu, m_w_bs, m_conv_w, m_conv_b, m_lru_w_a, m_lru_b_a, m_lru_w_x, m_lru_b_x, m_lru_lambda, m_w_bl, m_w_out, m_g_post, m_w_ple, m_w_ple_gate, v_g_pre, v_w_in, v_s5_a_re, v_s5_a_im, v_s5_log_dt, v_s5_b_re, v_s5_b_im, v_s5_c_re, v_s5_c_im, v_s5_d, v_w_glu, v_w_bs, v_conv_w, v_conv_b, v_lru_w_a, v_lru_b_a, v_lru_w_x, v_lru_b_x, v_lru_lambda, v_w_bl, v_w_out, v_g_post, v_w_ple, v_w_ple_gate):
    given = dict(locals())
    W = {n: given[n] for n in WEIGHTS}
    M = {n: given["m_" + n] for n in WEIGHTS}
    V = {n: given["v_" + n] for n in WEIGHTS}
    xs, ps, target = x[0], p[:, 0], loss_target[0]

    sizes = {n: math.prod(W[n].shape) for n in SHARDED}
    flat_w = jnp.concatenate([W[n].astype(bf16).reshape(-1) for n in GATHER_BF16])
    n_w = _round_up(flat_w.shape[0], FLAT_COLS)
    gathered = all_gather(_flat_rows(flat_w[None], n_w)[0], "comm_gather_weights").reshape(N_DEV, n_w)
    full, off = {}, 0
    for n in GATHER_BF16:
        full[n] = _rows_to_full(gathered[:, off:off + sizes[n]], W[n].shape, SHARDED[n])
        off += sizes[n]
    n_cw = _round_up(sizes['conv_w'], SUBLANES * LANES)
    conv_rows = all_gather(_flat_rows(conv_w.reshape(1, -1), n_cw)[0].reshape(-1, LANES), "comm_gather_conv")
    full['conv_w'] = _rows_to_full(conv_rows.reshape(N_DEV, n_cw)[:, :sizes['conv_w']], conv_w.shape, SHARDED['conv_w'])

    def layer_weights(i):
        return {n: (full[n][i] if n in SHARDED else W[n][i]) for n in WEIGHTS}

    act, saved = xs, []
    for i in range(DEPTH):
        act, sv = layer_fwd(act, ps[i], layer_weights(i), "_l%d" % i)
        saved.append(sv)
    loss_part, gact = loss_head(act, target, "loss_head")
    grads = [None] * DEPTH
    for i in reversed(range(DEPTH)):
        gact, grads[i] = layer_bwd(gact, layer_weights(i), saved[i], "_l%d" % i)
    loss = lax.psum(loss_part[0, 0], ("x", "y", "c"))
    gfull = {n: jnp.stack([grads[i][n].reshape(full[n].shape[1:] if n in SHARDED else W[n].shape[1:])
                           for i in range(DEPTH)]) for n in WEIGHTS}

    rep = jnp.concatenate([gfull[n].reshape(-1) for n in REPLICATED])
    n_rep = _round_up(rep.shape[0], N_DEV * LANES)
    rep_rows = jnp.pad(rep, (0, n_rep - rep.shape[0])).reshape(N_DEV, -1)
    rows = jnp.concatenate([_shard_to_rows(gfull[n], SHARDED[n]) for n in SHARDED] + [rep_rows], axis=1)
    n_g = _round_up(rows.shape[1], FLAT_COLS * FLAT_ROW_TILE)
    mine, theirs = exchange_sibling(_flat_rows(rows, n_g), "comm_reduce_sibling")
    own, others = exchange_chips(add_pairs(mine, theirs, "reduce_add_sibling"), "comm_reduce_chips")
    reduced = add_chips(own, others, "reduce_add_chips").reshape(-1)

    red, off = {}, 0
    for n in SHARDED:
        red[n] = reduced[off:off + sizes[n]].reshape(W[n].shape)
        off += sizes[n]
    piece = reduced[off:off + n_rep // N_DEV].reshape(-1, LANES)
    rep_all = all_gather(piece, "comm_gather_replicated").reshape(-1)
    off = 0
    for n in REPLICATED:
        k = math.prod(W[n].shape)
        red[n] = rep_all[off:off + k].reshape(W[n].shape)
        off += k

    deltas, new_m, new_v = {}, {}, {}
    for n in WEIGHTS:
        deltas[n], new_m[n], new_v[n] = adamw(W[n], red[n], M[n], V[n], "adamw_" + n)
    return (loss, gact[None], *[red[n] for n in WEIGHTS], *[deltas[n] for n in WEIGHTS],
            *[new_m[n] for n in WEIGHTS], *[new_v[n] for n in WEIGHTS])
```

```python
import functools
import math

import jax
import jax.numpy as jnp
from jax import lax
from jax.experimental import pallas as pl
from jax.experimental.pallas import tpu as pltpu

f32 = jnp.float32
bf16 = jnp.bfloat16

D_MODEL = 1024
DEPTH = 2
PLE_DIM = 256
NORM_EPS = 1e-6
S5_WIDTH = 512
S5_GROUP = 16
S5_GROUPS = 32
S5_STATE = 64
S5_NS = S5_GROUPS * S5_STATE
LRU_WIDTH = 1280
LRU_HEADS = 10
LRU_HEAD_DIM = 128
LRU_C = 8.0
CONV_WIDTH = 4
IN_COLS = 2 * S5_WIDTH + 2 * LRU_WIDTH + 2 * D_MODEL
N_DEV = 8

ADAM_LR = 0.001
ADAM_B1 = 0.9
ADAM_B2 = 0.999
ADAM_EPS = 1e-08
ADAM_WD = 0.01
ADAM_STEP = 10

LANES = 128
SUBLANES = 8
SEGS = SUBLANES
SCAN_CHUNK = 256
TOKEN_TILE = 256
MM_TILE_M = 512
HALO = SUBLANES

WEIGHTS = ['g_pre', 'w_in', 's5_a_re', 's5_a_im', 's5_log_dt', 's5_b_re', 's5_b_im', 's5_c_re', 's5_c_im',
           's5_d', 'w_glu', 'w_bs', 'conv_w', 'conv_b', 'lru_w_a', 'lru_b_a', 'lru_w_x', 'lru_b_x',
           'lru_lambda', 'w_bl', 'w_out', 'g_post', 'w_ple', 'w_ple_gate']
SHARDED = {'w_in': 2, 'w_glu': 2, 'w_bs': 2, 'conv_w': 2, 'w_bl': 1, 'w_out': 1, 'w_ple': 2, 'w_ple_gate': 1}
GATHER_BF16 = ['w_in', 'w_glu', 'w_bs', 'w_bl', 'w_out', 'w_ple', 'w_ple_gate']
REPLICATED = [n for n in WEIGHTS if n not in SHARDED]


def _sig(x):
    return 1.0 / (1.0 + jnp.exp(-x))


def _gelu_parts(x):
    k = math.sqrt(2.0 / math.pi)
    t = jnp.tanh(k * (x + 0.044715 * x * x * x))
    return t, k


def _gelu(x):
    t, _ = _gelu_parts(x)
    return 0.5 * x * (1.0 + t)


def _gelu_grad(x):
    t, k = _gelu_parts(x)
    return 0.5 * (1.0 + t) + 0.5 * x * (1.0 - t * t) * k * (1.0 + 3.0 * 0.044715 * x * x)


def _one_minus_exp(z):
    series = -z * (1.0 + z * (0.5 + z * (1.0 / 6.0 + z * (1.0 / 24.0 + z * (1.0 / 120.0)))))
    return jnp.where(z > -0.05, series, 1.0 - jnp.exp(z))


def _dot(a, b):
    return jnp.dot(a, b, preferred_element_type=f32)


def _dot_nt(a, b):
    return lax.dot_general(a, b, (((1,), (1,)), ((), ())), preferred_element_type=f32)


def _dot_tn(a, b):
    return lax.dot_general(a, b, (((0,), (0,)), ((), ())), preferred_element_type=f32)


def _S(shape, dtype=f32):
    return jax.ShapeDtypeStruct(shape, dtype)


def _full(shape):
    nd = len(shape)
    return pl.BlockSpec(shape, lambda *_: (0,) * nd)


def _rows(tile, width, col=0):
    return pl.BlockSpec((tile, width), lambda i: (i, col))


VMEM_LIMIT_BYTES = 56 * 1024 * 1024


def _cp(*semantics):
    return pltpu.CompilerParams(dimension_semantics=semantics or None, vmem_limit_bytes=VMEM_LIMIT_BYTES)


def _tile(n, want):
    t = min(n, want)
    assert n % t == 0, (n, want)
    return t


def mm_nn(a, b, name, out_dtype=f32, tn=512, tk=1024):
    M, K = a.shape
    N = b.shape[1]
    tm, tn, tk = _tile(M, MM_TILE_M), _tile(N, tn), _tile(K, tk)
    nk = K // tk

    def body(a_ref, b_ref, o_ref, acc_ref):
        k = pl.program_id(2)
        part = _dot(a_ref[...].astype(bf16), b_ref[...].astype(bf16))

        @pl.when(k == 0)
        def _():
            acc_ref[...] = part

        @pl.when(k > 0)
        def _():
            acc_ref[...] += part

        @pl.when(k == nk - 1)
        def _():
            o_ref[...] = acc_ref[...].astype(out_dtype)

    return pl.pallas_call(
        body, name=name, grid=(M // tm, N // tn, nk),
        in_specs=[pl.BlockSpec((tm, tk), lambda i, j, k: (i, k)), pl.BlockSpec((tk, tn), lambda i, j, k: (k, j))],
        out_specs=pl.BlockSpec((tm, tn), lambda i, j, k: (i, j)),
        out_shape=_S((M, N), out_dtype), scratch_shapes=[pltpu.VMEM((tm, tn), f32)],
        compiler_params=_cp("parallel", "parallel", "arbitrary"),
    )(a, b)


def mm_nt(a, b, name, out_dtype=f32, tn=512, tk=1408):
    M, K = a.shape
    N = b.shape[0]
    tm, tn, tk = _tile(M, MM_TILE_M), _tile(N, tn), _tile(K, tk)
    nk = K // tk

    def body(a_ref, b_ref, o_ref, acc_ref):
        k = pl.program_id(2)
        part = _dot_nt(a_ref[...].astype(bf16), b_ref[...].astype(bf16))

        @pl.when(k == 0)
        def _():
            acc_ref[...] = part

        @pl.when(k > 0)
        def _():
            acc_ref[...] += part

        @pl.when(k == nk - 1)
        def _():
            o_ref[...] = acc_ref[...].astype(out_dtype)

    return pl.pallas_call(
        body, name=name, grid=(M // tm, N // tn, nk),
        in_specs=[pl.BlockSpec((tm, tk), lambda i, j, k: (i, k)), pl.BlockSpec((tn, tk), lambda i, j, k: (j, k))],
        out_specs=pl.BlockSpec((tm, tn), lambda i, j, k: (i, j)),
        out_shape=_S((M, N), out_dtype), scratch_shapes=[pltpu.VMEM((tm, tn), f32)],
        compiler_params=_cp("parallel", "parallel", "arbitrary"),
    )(a, b)


def mm_tn(a, b, name, tk=512, tn=512):
    M, K = a.shape
    N = b.shape[1]
    tm, tk, tn = _tile(M, MM_TILE_M), _tile(K, tk), _tile(N, tn)

    def body(a_ref, b_ref, o_ref):
        m = pl.program_id(2)
        part = _dot_tn(a_ref[...].astype(bf16), b_ref[...].astype(bf16))

        @pl.when(m == 0)
        def _():
            o_ref[...] = part

        @pl.when(m > 0)
        def _():
            o_ref[...] += part

    return pl.pallas_call(
        body, name=name, grid=(K // tk, N // tn, M // tm),
        in_specs=[pl.BlockSpec((tm, tk), lambda i, j, m: (m, i)), pl.BlockSpec((tm, tn), lambda i, j, m: (m, j))],
        out_specs=pl.BlockSpec((tk, tn), lambda i, j, m: (i, j)),
        out_shape=_S((K, N), f32),
        compiler_params=_cp("parallel", "parallel", "arbitrary"),
    )(a, b)


def rms_fwd(x, g, name):
    T = x.shape[0]
    tm = _tile(T, TOKEN_TILE)

    def body(x_ref, g_ref, h_ref):
        xv = x_ref[...]
        r = lax.rsqrt(jnp.mean(xv * xv, axis=-1, keepdims=True) + NORM_EPS)
        h_ref[...] = (xv * r * g_ref[...]).astype(bf16)

    return pl.pallas_call(
        body, name=name, grid=(T // tm,),
        in_specs=[_rows(tm, D_MODEL), _full((1, D_MODEL))], out_specs=_rows(tm, D_MODEL),
        out_shape=_S((T, D_MODEL), bf16), compiler_params=_cp("parallel"),
    )(x, g)


def rms_bwd(gh, x, g, gres, name):
    T = x.shape[0]
    tm = _tile(T, TOKEN_TILE)

    def body(gh_ref, x_ref, g_ref, gres_ref, gx_ref, gg_ref):
        i = pl.program_id(0)
        xv = x_ref[...]
        ghv = gh_ref[...]
        r = lax.rsqrt(jnp.mean(xv * xv, axis=-1, keepdims=True) + NORM_EPS)
        nrm = xv * r
        gy = ghv * g_ref[...]
        gx_ref[...] = gres_ref[...] + r * (gy - nrm * jnp.mean(gy * nrm, axis=-1, keepdims=True))
        part = jnp.sum(ghv * nrm, axis=0, keepdims=True)

        @pl.when(i == 0)
        def _():
            gg_ref[...] = part

        @pl.when(i > 0)
        def _():
            gg_ref[...] += part

    return pl.pallas_call(
        body, name=name, grid=(T // tm,),
        in_specs=[_rows(tm, D_MODEL), _rows(tm, D_MODEL), _full((1, D_MODEL)), _rows(tm, D_MODEL)],
        out_specs=[_rows(tm, D_MODEL), _full((1, D_MODEL))],
        out_shape=[_S((T, D_MODEL)), _S((1, D_MODEL))], compiler_params=_cp("arbitrary"),
    )(gh, x, g, gres)


def _s5_discretise(a_re, a_im, log_dt, b_re_t, b_im_t):
    dt = jnp.exp(log_dt)
    mag = jnp.exp(a_re * dt)
    ab_re = mag * jnp.cos(a_im * dt)
    ab_im = mag * jnp.sin(a_im * dt)
    den = a_re * a_re + a_im * a_im
    nr, ni = ab_re - 1.0, ab_im
    z_re = (nr * a_re + ni * a_im) / den
    z_im = (ni * a_re - nr * a_im) / den
    bb_re = z_re[None] * b_re_t - z_im[None] * b_im_t
    bb_im = z_re[None] * b_im_t + z_im[None] * b_re_t
    return ab_re, ab_im, bb_re, bb_im


def s5_prep(a_re, a_im, log_dt, b_re_t, b_im_t, m, name):
    G, N = a_re.shape

    def body(are_ref, aim_ref, ldt_ref, bre_ref, bim_ref, ab_ref, pw_ref, bb_ref):
        are, aim, ldt = are_ref[...], aim_ref[...], ldt_ref[...]
        ab_re, ab_im, bb_re, bb_im = _s5_discretise(are, aim, ldt, bre_ref[...], bim_ref[...])
        ab_ref[0], ab_ref[1] = ab_re, ab_im
        bb_ref[0], bb_ref[1] = bb_re, bb_im
        dt = jnp.exp(ldt)
        for k in range(m):
            mag = jnp.exp(are * dt * (k + 1.0))
            pw_ref[0, k] = mag * jnp.cos(aim * dt * (k + 1.0))
            pw_ref[1, k] = mag * jnp.sin(aim * dt * (k + 1.0))

    return pl.pallas_call(
        body, name=name,
        out_shape=[_S((2, G, N)), _S((2, m, G, N)), _S((2, S5_GROUP, G, N))], compiler_params=_cp(),
    )(a_re, a_im, log_dt, b_re_t, b_im_t)


def s5_prep_bwd(a_re, a_im, log_dt, b_re_t, b_im_t, g_ab, g_bb, name):
    G, N = a_re.shape

    def body(are_ref, aim_ref, ldt_ref, bre_ref, bim_ref, gab_ref, gbb_ref, o_are, o_aim, o_ldt, o_bre, o_bim):
        _, vjp = jax.vjp(_s5_discretise, are_ref[...], aim_ref[...], ldt_ref[...], bre_ref[...], bim_ref[...])
        g_are, g_aim, g_ldt, g_bre, g_bim = vjp((gab_ref[0], gab_ref[1], gbb_ref[0], gbb_ref[1]))
        o_are[...], o_aim[...], o_ldt[...], o_bre[...], o_bim[...] = g_are, g_aim, g_ldt, g_bre, g_bim

    return pl.pallas_call(
        body, name=name,
        out_shape=[_S((G, N)), _S((G, N)), _S((G, 1)), _S((S5_GROUP, G, N)), _S((S5_GROUP, G, N))],
        compiler_params=_cp(),
    )(a_re, a_im, log_dt, b_re_t, b_im_t, g_ab, g_bb)


NB_S5 = S5_NS // LANES
SCAN_JB = 4


def _lanes(j):
    return slice(LANES * j, LANES * (j + 1))


def s5_fwd(usg, bd, cdt, dvec, abar, ptab, name):
    T = usg.shape[0]
    tc = _tile(T, SCAN_CHUNK)
    m = tc // SEGS
    assert ptab.shape == (2, m, S5_NS)

    def body(u_ref, bd_ref, cdt_ref, d_ref, a_ref, p_ref, ys_ref, sre_ref, sim_ref, sbf_ref, scr_re, scr_im, carry_ref):
        i = pl.program_id(0)

        @pl.when(i == 0)
        def _():
            carry_ref[...] = jnp.zeros_like(carry_ref)

        u = u_ref[...]
        bu = _dot(u.astype(bf16), bd_ref[...])
        for j in range(NB_S5):
            scr_re[j * tc:(j + 1) * tc, :] = bu[:, _lanes(j)]
            scr_im[j * tc:(j + 1) * tc, :] = bu[:, S5_NS + LANES * j:S5_NS + LANES * (j + 1)]
        for j0 in range(0, NB_S5, SCAN_JB):
            ars = [jnp.broadcast_to(a_ref[0, :, _lanes(j0 + q)], (SEGS, LANES)) for q in range(SCAN_JB)]
            ais = [jnp.broadcast_to(a_ref[1, :, _lanes(j0 + q)], (SEGS, LANES)) for q in range(SCAN_JB)]

            def kstep(k, st):
                out = []
                for q in range(SCAN_JB):
                    sr, si = st[2 * q], st[2 * q + 1]
                    rows = pl.ds((j0 + q) * tc + k, SEGS, stride=m)
                    nr = ars[q] * sr - ais[q] * si + scr_re[rows, :]
                    ni = ars[q] * si + ais[q] * sr + scr_im[rows, :]
                    scr_re[rows, :] = nr
                    scr_im[rows, :] = ni
                    out += [nr, ni]
                return tuple(out)

            ends = lax.fori_loop(0, m, kstep, tuple(jnp.zeros((SEGS, LANES), f32) for _ in range(2 * SCAN_JB)))
            for q in range(SCAN_JB):
                j = j0 + q
                er, ei = ends[2 * q], ends[2 * q + 1]
                cr, ci = carry_ref[0, :, _lanes(j)], carry_ref[1, :, _lanes(j)]
                pr, pi = p_ref[0, :, _lanes(j)], p_ref[1, :, _lanes(j)]
                amr, ami = pr[m - 1:m, :], pi[m - 1:m, :]
                for s in range(SEGS):
                    rows = slice(s * m, (s + 1) * m)
                    sr = scr_re[j * tc + s * m:j * tc + (s + 1) * m, :] + pr * cr - pi * ci
                    si = scr_im[j * tc + s * m:j * tc + (s + 1) * m, :] + pr * ci + pi * cr
                    sre_ref[j, rows, :] = sr
                    sim_ref[j, rows, :] = si
                    sbf_ref[rows, _lanes(j)] = sr.astype(bf16)
                    sbf_ref[rows, S5_NS + LANES * j:S5_NS + LANES * (j + 1)] = si.astype(bf16)
                    cr, ci = (er[s:s + 1, :] + amr * cr - ami * ci, ei[s:s + 1, :] + amr * ci + ami * cr)
                carry_ref[0, :, _lanes(j)] = cr
                carry_ref[1, :, _lanes(j)] = ci
        ys_ref[...] = _dot(sbf_ref[...], cdt_ref[...]) + d_ref[...] * u

    return pl.pallas_call(
        body, name=name, grid=(T // tc,),
        in_specs=[_rows(tc, S5_WIDTH, 0), _full((S5_WIDTH, 2 * S5_NS)), _full((2 * S5_NS, S5_WIDTH)),
                  _full((1, S5_WIDTH)), _full((2, 1, S5_NS)), _full((2, m, S5_NS))],
        out_specs=[_rows(tc, S5_WIDTH), pl.BlockSpec((NB_S5, tc, LANES), lambda i: (0, i, 0)),
                   pl.BlockSpec((NB_S5, tc, LANES), lambda i: (0, i, 0)), _rows(tc, 2 * S5_NS)],
        out_shape=[_S((T, S5_WIDTH)), _S((NB_S5, T, LANES)), _S((NB_S5, T, LANES)), _S((T, 2 * S5_NS), bf16)],
        scratch_shapes=[pltpu.VMEM((NB_S5 * tc, LANES), f32), pltpu.VMEM((NB_S5 * tc, LANES), f32),
                        pltpu.VMEM((2, 1, S5_NS), f32)],
        compiler_params=_cp("arbitrary"),
    )(usg, bd, cdt, dvec, abar, ptab)


def s5_bwd(gys, usg, s_re, s_im, bd, cdt, dvec, abar, ptab_rev, name):
    T = gys.shape[0]
    tc = _tile(T, SCAN_CHUNK)
    m = tc // SEGS
    nch = T // tc
    hb = tc // HALO

    def body(gy_ref, u_ref, sre_ref, sim_ref, hre_ref, him_ref, bd_ref, cdt_ref, d_ref, a_ref, p_ref,
             gu_ref, lam_ref, gab_ref, gd_ref, scr_re, scr_im, sh_re, sh_im, carry_ref):
        i = pl.program_id(0)

        @pl.when(i == 0)
        def _():
            carry_ref[...] = jnp.zeros_like(carry_ref)
            gab_ref[...] = jnp.zeros_like(gab_ref)
            gd_ref[...] = jnp.zeros_like(gd_ref)

        first = i == nch - 1
        gy = gy_ref[...]
        u = u_ref[...]
        gs = _dot_nt(gy.astype(bf16), cdt_ref[...])
        for j in range(NB_S5):
            scr_re[j * tc:(j + 1) * tc, :] = gs[:, _lanes(j)]
            scr_im[j * tc:(j + 1) * tc, :] = gs[:, S5_NS + LANES * j:S5_NS + LANES * (j + 1)]
            sh_re[j, 0:HALO, :] = jnp.where(first, 0.0, hre_ref[j])
            sh_im[j, 0:HALO, :] = jnp.where(first, 0.0, him_ref[j])
            sh_re[j, HALO:, :] = sre_ref[j]
            sh_im[j, HALO:, :] = sim_ref[j]
        for j0 in range(0, NB_S5, SCAN_JB):
            ars = [jnp.broadcast_to(a_ref[0, :, _lanes(j0 + q)], (SEGS, LANES)) for q in range(SCAN_JB)]
            ais = [jnp.broadcast_to(a_ref[1, :, _lanes(j0 + q)], (SEGS, LANES)) for q in range(SCAN_JB)]

            def kstep(kk, st):
                k = m - 1 - kk
                out = []
                for q in range(SCAN_JB):
                    lr, li = st[2 * q], st[2 * q + 1]
                    rows = pl.ds((j0 + q) * tc + k, SEGS, stride=m)
                    nr = ars[q] * lr + ais[q] * li + scr_re[rows, :]
                    ni = ars[q] * li - ais[q] * lr + scr_im[rows, :]
                    scr_re[rows, :] = nr
                    scr_im[rows, :] = ni
                    out += [nr, ni]
                return tuple(out)

            ends = lax.fori_loop(0, m, kstep, tuple(jnp.zeros((SEGS, LANES), f32) for _ in range(2 * SCAN_JB)))
            for q in range(SCAN_JB):
                j = j0 + q
                er, ei = ends[2 * q], ends[2 * q + 1]
                cr, ci = carry_ref[0, :, _lanes(j)], carry_ref[1, :, _lanes(j)]
                pr, pi = p_ref[0, :, _lanes(j)], p_ref[1, :, _lanes(j)]
                amr, ami = pr[0:1, :], pi[0:1, :]
                acc_re = jnp.zeros((m, LANES), f32)
                acc_im = jnp.zeros((m, LANES), f32)
                for s in reversed(range(SEGS)):
                    rows = slice(s * m, (s + 1) * m)
                    lr = scr_re[j * tc + s * m:j * tc + (s + 1) * m, :] + pr * cr + pi * ci
                    li = scr_im[j * tc + s * m:j * tc + (s + 1) * m, :] + pr * ci - pi * cr
                    lam_ref[rows, _lanes(j)] = lr.astype(bf16)
                    lam_ref[rows, S5_NS + LANES * j:S5_NS + LANES * (j + 1)] = li.astype(bf16)
                    prev = slice(HALO - 1 + s * m, HALO - 1 + (s + 1) * m)
                    pre, pim = sh_re[j, prev, :], sh_im[j, prev, :]
                    acc_re += lr * pre + li * pim
                    acc_im += li * pre - lr * pim
                    cr, ci = (er[s:s + 1, :] + amr * cr + ami * ci, ei[s:s + 1, :] + amr * ci - ami * cr)
                carry_ref[0, :, _lanes(j)] = cr
                carry_ref[1, :, _lanes(j)] = ci
                gab_ref[0, :, _lanes(j)] += jnp.sum(acc_re, axis=0, keepdims=True)
                gab_ref[1, :, _lanes(j)] += jnp.sum(acc_im, axis=0, keepdims=True)
        gu_ref[...] = (gy * d_ref[...] + _dot_nt(lam_ref[...], bd_ref[...])).astype(bf16)
        gd_ref[...] += jnp.sum(gy * u, axis=0, keepdims=True)

    rev = lambda i: (nch - 1 - i, 0)
    rev3 = lambda i: (0, nch - 1 - i, 0)
    halo3 = lambda i: (0, jnp.maximum((nch - 1 - i) * hb - 1, 0), 0)
    return pl.pallas_call(
        body, name=name, grid=(nch,),
        in_specs=[pl.BlockSpec((tc, S5_WIDTH), rev), pl.BlockSpec((tc, S5_WIDTH), rev),
                  pl.BlockSpec((NB_S5, tc, LANES), rev3), pl.BlockSpec((NB_S5, tc, LANES), rev3),
                  pl.BlockSpec((NB_S5, HALO, LANES), halo3), pl.BlockSpec((NB_S5, HALO, LANES), halo3),
                  _full((S5_WIDTH, 2 * S5_NS)), _full((2 * S5_NS, S5_WIDTH)), _full((1, S5_WIDTH)),
                  _full((2, 1, S5_NS)), _full((2, m, S5_NS))],
        out_specs=[pl.BlockSpec((tc, S5_WIDTH), rev), pl.BlockSpec((tc, 2 * S5_NS), rev),
                   _full((2, 1, S5_NS)), _full((1, S5_WIDTH))],
        out_shape=[_S((T, S5_WIDTH), bf16), _S((T, 2 * S5_NS), bf16), _S((2, 1, S5_NS)), _S((1, S5_WIDTH))],
        scratch_shapes=[pltpu.VMEM((NB_S5 * tc, LANES), f32), pltpu.VMEM((NB_S5 * tc, LANES), f32),
                        pltpu.VMEM((NB_S5, tc + HALO, LANES), f32), pltpu.VMEM((NB_S5, tc + HALO, LANES), f32),
                        pltpu.VMEM((2, 1, S5_NS), f32)],
        compiler_params=_cp("arbitrary"),
    )(gys, usg, s_re, s_im, s_re, s_im, bd, cdt, dvec, abar, ptab_rev)


def s5_post_fwd(ys, usg, wglu, wbs, name):
    T = ys.shape[0]
    tm = _tile(T, TOKEN_TILE)

    def body(ys_ref, sg_ref, wglu_ref, wbs_ref, glu_ref, zs_ref):
        glu = _dot(_gelu(ys_ref[...]).astype(bf16), wglu_ref[...])
        sg = sg_ref[...]
        y2 = glu[:, :S5_WIDTH] * _sig(glu[:, S5_WIDTH:]) * (sg * _sig(sg))
        glu_ref[...] = glu
        zs_ref[...] = _dot(y2.astype(bf16), wbs_ref[...])

    return pl.pallas_call(
        body, name=name, grid=(T // tm,),
        in_specs=[_rows(tm, S5_WIDTH), _rows(tm, S5_WIDTH, 1), _full((S5_WIDTH, 2 * S5_WIDTH)), _full((S5_WIDTH, D_MODEL))],
        out_specs=[_rows(tm, 2 * S5_WIDTH), _rows(tm, D_MODEL)],
        out_shape=[_S((T, 2 * S5_WIDTH)), _S((T, D_MODEL))], compiler_params=_cp("parallel"),
    )(ys, usg, wglu, wbs)


def s5_post_bwd(gzs, glu, usg, ys, wbs, wglu, name):
    T = ys.shape[0]
    tm = _tile(T, TOKEN_TILE)

    def body(gzs_ref, glu_ref, sg_ref, ys_ref, wbs_ref, wglu_ref, y2_ref, gglu_ref, ge_ref, gys_ref, gsg_ref):
        glu = glu_ref[...]
        a, b = glu[:, :S5_WIDTH], glu[:, S5_WIDTH:]
        sg = sg_ref[...]
        ys = ys_ref[...]
        sb, ssg = _sig(b), _sig(sg)
        silu = sg * ssg
        y2_ref[...] = (a * sb * silu).astype(bf16)
        gy2 = _dot_nt(gzs_ref[...], wbs_ref[...])
        g_a = gy2 * sb * silu
        g_b = gy2 * a * sb * (1.0 - sb) * silu
        gsg_ref[...] = (gy2 * a * sb * ssg * (1.0 + sg * (1.0 - ssg))).astype(bf16)
        gglu = jnp.concatenate([g_a, g_b], axis=1).astype(bf16)
        gglu_ref[...] = gglu
        ge_ref[...] = _gelu(ys).astype(bf16)
        gys_ref[...] = _dot_nt(gglu, wglu_ref[...]) * _gelu_grad(ys)

    return pl.pallas_call(
        body, name=name, grid=(T // tm,),
        in_specs=[_rows(tm, D_MODEL), _rows(tm, 2 * S5_WIDTH), _rows(tm, S5_WIDTH, 1), _rows(tm, S5_WIDTH),
                  _full((S5_WIDTH, D_MODEL)), _full((S5_WIDTH, 2 * S5_WIDTH))],
        out_specs=[_rows(tm, S5_WIDTH), _rows(tm, 2 * S5_WIDTH), _rows(tm, S5_WIDTH), _rows(tm, S5_WIDTH), _rows(tm, S5_WIDTH)],
        out_shape=[_S((T, S5_WIDTH), bf16), _S((T, 2 * S5_WIDTH), bf16), _S((T, S5_WIDTH), bf16), _S((T, S5_WIDTH)),
                   _S((T, S5_WIDTH), bf16)],
        compiler_params=_cp("parallel"),
    )(gzs, glu, usg, ys, wbs, wglu)


NB_LRU = LRU_WIDTH // LANES
LRU_JB = 5


def _softplus_neg(lam):
    return jnp.maximum(-lam, 0.0) + jnp.log(1.0 + jnp.exp(-jnp.abs(lam)))


def lru_fwd(lx, convw, convb, wa, wx, ba, bx, lam, name):
    T = lx.shape[0]
    tc = _tile(T, SCAN_CHUNK)
    m = tc // SEGS
    hb = tc // HALO

    def body(x_ref, xh_ref, cw_ref, cb_ref, wa_ref, wx_ref, ba_ref, bx_ref, lam_ref,
             c_ref, r_ref, i_ref, h_ref, xbuf, scr_a, scr_b, carry_ref):
        i = pl.program_id(0)

        @pl.when(i == 0)
        def _():
            carry_ref[...] = jnp.zeros_like(carry_ref)

        xbuf[0:HALO, :] = jnp.where(i == 0, 0.0, xh_ref[...])
        xbuf[HALO:, :] = x_ref[...]
        c = cb_ref[...] + cw_ref[0:1, :] * xbuf[HALO - 3:HALO - 3 + tc, :]
        for k in range(1, CONV_WIDTH):
            c = c + cw_ref[k:k + 1, :] * xbuf[HALO - 3 + k:HALO - 3 + k + tc, :]
        c_ref[...] = c
        sp = _softplus_neg(lam_ref[...])
        for j in range(NB_LRU):
            cj = c[:, _lanes(j)]
            cjb = cj.astype(bf16)
            r = _sig(_dot(cjb, wa_ref[j]) + ba_ref[:, _lanes(j)])
            g = _sig(_dot(cjb, wx_ref[j]) + bx_ref[:, _lanes(j)])
            r_ref[:, _lanes(j)] = r
            i_ref[:, _lanes(j)] = g
            log_a = -LRU_C * r * sp[:, _lanes(j)]
            scr_a[j * tc:(j + 1) * tc, :] = jnp.exp(log_a)
            scr_b[j * tc:(j + 1) * tc, :] = jnp.sqrt(_one_minus_exp(2.0 * log_a)) * (g * cj)
        for j0 in range(0, NB_LRU, LRU_JB):
            def kstep(k, st):
                out = []
                for q in range(LRU_JB):
                    hh, ac = st[2 * q], st[2 * q + 1]
                    rows = pl.ds((j0 + q) * tc + k, SEGS, stride=m)
                    a = scr_a[rows, :]
                    hh = a * hh + scr_b[rows, :]
                    ac = a * ac
                    scr_b[rows, :] = hh
                    scr_a[rows, :] = ac
                    out += [hh, ac]
                return tuple(out)

            init = tuple(jnp.zeros((SEGS, LANES), f32) if q % 2 == 0 else jnp.ones((SEGS, LANES), f32)
                         for q in range(2 * LRU_JB))
            ends = lax.fori_loop(0, m, kstep, init)
            for q in range(LRU_JB):
                j = j0 + q
                eh, ea = ends[2 * q], ends[2 * q + 1]
                cr = carry_ref[:, _lanes(j)]
                for s in range(SEGS):
                    rows = slice(s * m, (s + 1) * m)
                    loc = slice(j * tc + s * m, j * tc + (s + 1) * m)
                    h_ref[rows, _lanes(j)] = scr_b[loc, :] + scr_a[loc, :] * cr
                    cr = eh[s:s + 1, :] + ea[s:s + 1, :] * cr
                carry_ref[:, _lanes(j)] = cr

    wide = lambda: _rows(tc, LRU_WIDTH)
    return pl.pallas_call(
        body, name=name, grid=(T // tc,),
        in_specs=[wide(), pl.BlockSpec((HALO, LRU_WIDTH), lambda i: (jnp.maximum(i * hb - 1, 0), 0)),
                  _full((CONV_WIDTH, LRU_WIDTH)), _full((1, LRU_WIDTH)),
                  _full((LRU_HEADS, LRU_HEAD_DIM, LRU_HEAD_DIM)), _full((LRU_HEADS, LRU_HEAD_DIM, LRU_HEAD_DIM)),
                  _full((1, LRU_WIDTH)), _full((1, LRU_WIDTH)), _full((1, LRU_WIDTH))],
        out_specs=[wide(), wide(), wide(), wide()],
        out_shape=[_S((T, LRU_WIDTH))] * 4,
        scratch_shapes=[pltpu.VMEM((tc + HALO, LRU_WIDTH), f32), pltpu.VMEM((NB_LRU * tc, LANES), f32),
                        pltpu.VMEM((NB_LRU * tc, LANES), f32), pltpu.VMEM((1, LRU_WIDTH), f32)],
        compiler_params=_cp("arbitrary"),
    )(lx, lx, convw, convb, wa, wx, ba, bx, lam)


def lru_bwd(gh, h, c, r, gi, lx, convw, wa, wx, lam, name):
    T = gh.shape[0]
    tc = _tile(T, SCAN_CHUNK)
    m = tc // SEGS
    nch = T // tc
    hb = tc // HALO
    tch = tc + HALO

    def body(gh_ref, h_ref, hh_ref, c_ref, r_ref, i_ref, x_ref, xh_ref, cw_ref, wa_ref, wx_ref, lam_ref,
             glx_ref, gwa_ref, gwx_ref, gba_ref, gbx_ref, glam_ref, gcb_ref, gcw_ref,
             scr_a, scr_m, hbuf, xbuf, gcbuf, carry_ref):
        i = pl.program_id(0)

        @pl.when(i == 0)
        def _():
            carry_ref[...] = jnp.zeros_like(carry_ref)
            gcbuf[...] = jnp.zeros_like(gcbuf)
            for ref in (gwa_ref, gwx_ref, gba_ref, gbx_ref, glam_ref, gcb_ref, gcw_ref):
                ref[...] = jnp.zeros_like(ref)

        first = i == nch - 1
        hbuf[0:HALO, :] = jnp.where(first, 0.0, hh_ref[...])
        hbuf[HALO:, :] = h_ref[...]
        xbuf[0:HALO, :] = jnp.where(first, 0.0, xh_ref[...])
        xbuf[HALO:, :] = x_ref[...]
        lam_v = lam_ref[...]
        sp = _softplus_neg(lam_v)
        for j in range(NB_LRU):
            a = jnp.exp(-LRU_C * r_ref[:, _lanes(j)] * sp[:, _lanes(j)])
            scr_a[j * tc:(j + 1) * tc, :] = a
            scr_m[j * tch:j * tch + tc, :] = a * gh_ref[:, _lanes(j)]
        for j0 in range(0, NB_LRU, LRU_JB):
            def kstep(kk, st):
                k = m - 1 - kk
                out = []
                for q in range(LRU_JB):
                    mu, ac = st[2 * q], st[2 * q + 1]
                    rows_a = pl.ds((j0 + q) * tc + k, SEGS, stride=m)
                    rows_m = pl.ds((j0 + q) * tch + k, SEGS, stride=m)
                    a = scr_a[rows_a, :]
                    mu = a * mu + scr_m[rows_m, :]
                    ac = a * ac
                    scr_m[rows_m, :] = mu
                    scr_a[rows_a, :] = ac
                    out += [mu, ac]
                return tuple(out)

            init = tuple(jnp.zeros((SEGS, LANES), f32) if q % 2 == 0 else jnp.ones((SEGS, LANES), f32)
                         for q in range(2 * LRU_JB))
            ends = lax.fori_loop(0, m, kstep, init)
            for q in range(LRU_JB):
                j = j0 + q
                em, ea = ends[2 * q], ends[2 * q + 1]
                cr = carry_ref[:, _lanes(j)]
                scr_m[j * tch + tc:j * tch + tc + 1, :] = cr
                for s in reversed(range(SEGS)):
                    rows_m = slice(j * tch + s * m, j * tch + (s + 1) * m)
                    rows_a = slice(j * tc + s * m, j * tc + (s + 1) * m)
                    scr_m[rows_m, :] = scr_m[rows_m, :] + scr_a[rows_a, :] * cr
                    cr = em[s:s + 1, :] + ea[s:s + 1, :] * cr
                carry_ref[:, _lanes(j)] = cr
        sneg = _sig(-lam_v)
        for j in range(NB_LRU):
            ln = _lanes(j)
            lamt = gh_ref[:, ln] + scr_m[j * tch + 1:j * tch + tc + 1, :]
            rj, ij, cj = r_ref[:, ln], i_ref[:, ln], c_ref[:, ln]
            log_a = -LRU_C * rj * sp[:, ln]
            a = jnp.exp(log_a)
            mult = jnp.sqrt(_one_minus_exp(2.0 * log_a))
            g_a = lamt * hbuf[HALO - 1:HALO - 1 + tc, ln]
            g_mult = lamt * ij * cj
            g_i = lamt * mult * cj
            g_c = lamt * mult * ij
            g_log_a = g_a * a - g_mult * a * a / mult
            glam_ref[:, ln] += jnp.sum(g_log_a * rj, axis=0, keepdims=True) * LRU_C * sneg[:, ln]
            g_ra = g_log_a * (-LRU_C) * sp[:, ln] * rj * (1.0 - rj)
            g_ia = g_i * ij * (1.0 - ij)
            gba_ref[:, ln] += jnp.sum(g_ra, axis=0, keepdims=True)
            gbx_ref[:, ln] += jnp.sum(g_ia, axis=0, keepdims=True)
            cjb, grb, gib = cj.astype(bf16), g_ra.astype(bf16), g_ia.astype(bf16)
            gwa_ref[j] += _dot_tn(cjb, grb)
            gwx_ref[j] += _dot_tn(cjb, gib)
            g_c = g_c + _dot_nt(grb, wa_ref[j]) + _dot_nt(gib, wx_ref[j])
            gcbuf[0:tc, ln] = g_c
            gcb_ref[:, ln] += jnp.sum(g_c, axis=0, keepdims=True)
        gc = gcbuf[0:tc, :]
        glx = cw_ref[CONV_WIDTH - 1:CONV_WIDTH, :] * gc
        gcw_ref[CONV_WIDTH - 1:CONV_WIDTH, :] += jnp.sum(gc * xbuf[HALO:HALO + tc, :], axis=0, keepdims=True)
        for k in range(CONV_WIDTH - 1):
            off = CONV_WIDTH - 1 - k
            glx = glx + cw_ref[k:k + 1, :] * gcbuf[off:off + tc, :]
            gcw_ref[k:k + 1, :] += jnp.sum(gc * xbuf[HALO - off:HALO - off + tc, :], axis=0, keepdims=True)
        glx_ref[...] = glx.astype(bf16)
        gcbuf[tc:tc + HALO, :] = gcbuf[0:HALO, :]

    rev = lambda i: (nch - 1 - i, 0)
    halo = lambda i: (jnp.maximum((nch - 1 - i) * hb - 1, 0), 0)
    wide = lambda: pl.BlockSpec((tc, LRU_WIDTH), rev)
    vec = lambda: _full((1, LRU_WIDTH))
    hd = lambda: _full((LRU_HEADS, LRU_HEAD_DIM, LRU_HEAD_DIM))
    return pl.pallas_call(
        body, name=name, grid=(nch,),
        in_specs=[wide(), wide(), pl.BlockSpec((HALO, LRU_WIDTH), halo), wide(), wide(), wide(), wide(),
                  pl.BlockSpec((HALO, LRU_WIDTH), halo), _full((CONV_WIDTH, LRU_WIDTH)), hd(), hd(), vec()],
        out_specs=[wide(), hd(), hd(), vec(), vec(), vec(), vec(), _full((CONV_WIDTH, LRU_WIDTH))],
        out_shape=[_S((T, LRU_WIDTH), bf16), _S((LRU_HEADS, LRU_HEAD_DIM, LRU_HEAD_DIM)),
                   _S((LRU_HEADS, LRU_HEAD_DIM, LRU_HEAD_DIM)), _S((1, LRU_WIDTH)), _S((1, LRU_WIDTH)),
                   _S((1, LRU_WIDTH)), _S((1, LRU_WIDTH)), _S((CONV_WIDTH, LRU_WIDTH))],
        scratch_shapes=[pltpu.VMEM((NB_LRU * tc, LANES), f32), pltpu.VMEM((NB_LRU * tch, LANES), f32),
                        pltpu.VMEM((tc + HALO, LRU_WIDTH), f32), pltpu.VMEM((tc + HALO, LRU_WIDTH), f32),
                        pltpu.VMEM((tc + HALO, LRU_WIDTH), f32), pltpu.VMEM((1, LRU_WIDTH), f32)],
        compiler_params=_cp("arbitrary"),
    )(gh, h, h, c, r, gi, lx, lx, convw, wa, wx, lam)


def merge_fwd(h, lg, zs, gsl, x, p, wbl, wout, gpost, wple, wpg, name):
    T = x.shape[0]
    tm = _tile(T, TOKEN_TILE)

    def body(h_ref, lg_ref, zs_ref, gs_ref, gl_ref, x_ref, p_ref, wbl_ref, wout_ref, gp_ref, wple_ref, wpg_ref,
             zl_ref, mix_ref, q_ref, pe_ref, xo_ref):
        lg_v = lg_ref[...]
        yl = h_ref[...] * (lg_v * _sig(lg_v))
        zl = _dot(yl.astype(bf16), wbl_ref[...])
        merged = _sig(gs_ref[...]) * zs_ref[...] + _sig(gl_ref[...]) * zl
        mix = _dot(merged.astype(bf16), wout_ref[...])
        r2 = lax.rsqrt(jnp.mean(mix * mix, axis=-1, keepdims=True) + NORM_EPS)
        x1 = x_ref[...] + mix * r2 * gp_ref[...]
        q = _dot(x1.astype(bf16), wpg_ref[...])
        pe = _dot(p_ref[...].astype(bf16), wple_ref[...])
        zl_ref[...], mix_ref[...], q_ref[...], pe_ref[...] = zl, mix, q, pe
        xo_ref[...] = x1 + pe * _sig(q)

    dm = lambda: _rows(tm, D_MODEL)
    return pl.pallas_call(
        body, name=name, grid=(T // tm,),
        in_specs=[_rows(tm, LRU_WIDTH), _rows(tm, LRU_WIDTH), dm(), _rows(tm, D_MODEL, 0), _rows(tm, D_MODEL, 1), dm(),
                  _rows(tm, PLE_DIM), _full((LRU_WIDTH, D_MODEL)), _full((D_MODEL, D_MODEL)), _full((1, D_MODEL)),
                  _full((PLE_DIM, D_MODEL)), _full((D_MODEL, D_MODEL))],
        out_specs=[dm(), dm(), dm(), dm(), dm()],
        out_shape=[_S((T, D_MODEL))] * 5, compiler_params=_cp("parallel"),
    )(h, lg, zs, gsl, gsl, x, p, wbl, wout, gpost, wple, wpg)


def merge_bwd(gx2, q, pe, mix, x, zl, zs, gsl, h, lg, wpg, wout, wbl, gpost, name):
    T = x.shape[0]
    tm = _tile(T, TOKEN_TILE)

    def body(gx2_ref, q_ref, pe_ref, mix_ref, x_ref, zl_ref, zs_ref, gs_ref, gl_ref, h_ref, lg_ref,
             wpg_ref, wout_ref, wbl_ref, gp_ref,
             gres_ref, gpe_ref, gq_ref, x1_ref, gmix_ref, mrg_ref, gzl_ref, yl_ref, gzs_ref, ggsl_ref, gh_ref, glg_ref,
             ggp_ref):
        i = pl.program_id(0)
        gx2 = gx2_ref[...]
        sq = _sig(q_ref[...])
        pe = pe_ref[...]
        gpe_ref[...] = (gx2 * sq).astype(bf16)
        gq = (gx2 * pe * sq * (1.0 - sq)).astype(bf16)
        gq_ref[...] = gq
        mix = mix_ref[...]
        gp = gp_ref[...]
        r2 = lax.rsqrt(jnp.mean(mix * mix, axis=-1, keepdims=True) + NORM_EPS)
        nrm = mix * r2
        x1_ref[...] = (x_ref[...] + nrm * gp).astype(bf16)
        gx1 = gx2 + _dot_nt(gq, wpg_ref[...])
        gres_ref[...] = gx1
        part = jnp.sum(gx1 * nrm, axis=0, keepdims=True)

        @pl.when(i == 0)
        def _():
            ggp_ref[...] = part

        @pl.when(i > 0)
        def _():
            ggp_ref[...] += part

        gy = gx1 * gp
        gmix = (r2 * (gy - nrm * jnp.mean(gy * nrm, axis=-1, keepdims=True))).astype(bf16)
        gmix_ref[...] = gmix
        gmerged = _dot_nt(gmix, wout_ref[...])
        zs, zl = zs_ref[...], zl_ref[...]
        ss, sl = _sig(gs_ref[...]), _sig(gl_ref[...])
        mrg_ref[...] = (ss * zs + sl * zl).astype(bf16)
        gzs_ref[...] = (gmerged * ss).astype(bf16)
        gzl = (gmerged * sl).astype(bf16)
        gzl_ref[...] = gzl
        ggsl_ref[:, :D_MODEL] = (gmerged * zs * ss * (1.0 - ss)).astype(bf16)
        ggsl_ref[:, D_MODEL:] = (gmerged * zl * sl * (1.0 - sl)).astype(bf16)
        lg_v, hv = lg_ref[...], h_ref[...]
        slg = _sig(lg_v)
        silu = lg_v * slg
        yl_ref[...] = (hv * silu).astype(bf16)
        gyl = _dot_nt(gzl, wbl_ref[...])
        gh_ref[...] = gyl * silu
        glg_ref[...] = (gyl * hv * slg * (1.0 + lg_v * (1.0 - slg))).astype(bf16)

    dm = lambda: _rows(tm, D_MODEL)
    lw = lambda: _rows(tm, LRU_WIDTH)
    return pl.pallas_call(
        body, name=name, grid=(T // tm,),
        in_specs=[dm(), dm(), dm(), dm(), dm(), dm(), dm(), _rows(tm, D_MODEL, 0), _rows(tm, D_MODEL, 1), lw(), lw(),
                  _full((D_MODEL, D_MODEL)), _full((D_MODEL, D_MODEL)), _full((LRU_WIDTH, D_MODEL)), _full((1, D_MODEL))],
        out_specs=[dm(), dm(), dm(), dm(), dm(), dm(), dm(), lw(), dm(), _rows(tm, 2 * D_MODEL), lw(), lw(),
                   _full((1, D_MODEL))],
        out_shape=[_S((T, D_MODEL)), _S((T, D_MODEL), bf16), _S((T, D_MODEL), bf16), _S((T, D_MODEL), bf16),
                   _S((T, D_MODEL), bf16), _S((T, D_MODEL), bf16), _S((T, D_MODEL), bf16), _S((T, LRU_WIDTH), bf16),
                   _S((T, D_MODEL), bf16), _S((T, 2 * D_MODEL), bf16), _S((T, LRU_WIDTH)), _S((T, LRU_WIDTH), bf16),
                   _S((1, D_MODEL))],
        compiler_params=_cp("arbitrary"),
    )(gx2, q, pe, mix, x, zl, zs, gsl, gsl, h, lg, wpg, wout, wbl, gpost)


def loss_head(y, target, name):
    T = y.shape[0]
    tm = _tile(T, TOKEN_TILE)

    def body(y_ref, t_ref, l_ref, g_ref):
        i = pl.program_id(0)
        e = y_ref[...] - t_ref[...]
        g_ref[...] = e * (1.0 / D_MODEL)
        part = 0.5 * jnp.sum(jnp.sum(e * e, axis=-1, keepdims=True) * (1.0 / D_MODEL), axis=0, keepdims=True)

        @pl.when(i == 0)
        def _():
            l_ref[...] = part

        @pl.when(i > 0)
        def _():
            l_ref[...] += part

    return pl.pallas_call(
        body, name=name, grid=(T // tm,),
        in_specs=[_rows(tm, D_MODEL), _rows(tm, D_MODEL)], out_specs=[_full((1, 1)), _rows(tm, D_MODEL)],
        out_shape=[_S((1, 1)), _S((T, D_MODEL))],
        compiler_params=_cp("arbitrary"),
    )(y, target)


def _block_diag_b(bb_t):
    eye = jnp.eye(S5_GROUPS, dtype=f32)
    d = eye[:, None, :, None] * jnp.transpose(bb_t, (1, 0, 2))[:, :, None, :]
    return d.reshape(S5_WIDTH, S5_NS)


def _block_diag_c(c):
    eye = jnp.eye(S5_GROUPS, dtype=f32)
    d = eye[:, None, :, None] * jnp.transpose(c, (0, 2, 1))[:, :, None, :]
    return d.reshape(S5_NS, S5_WIDTH)


def fold_groups(dense, name):
    NC = dense.shape[1]
    tn = _tile(NC, 512)

    def body(d_ref, o_ref):
        j = pl.program_id(0)
        d3 = d_ref[...].reshape(S5_GROUPS, S5_GROUP, tn)
        col = lax.broadcasted_iota(jnp.int32, d3.shape, 2) + j * tn
        grp = lax.broadcasted_iota(jnp.int32, d3.shape, 0)
        o_ref[...] = jnp.sum(jnp.where((col % S5_NS) // S5_STATE == grp, d3, 0.0), axis=0)

    return pl.pallas_call(
        body, name=name, grid=(NC // tn,), in_specs=[pl.BlockSpec((S5_WIDTH, tn), lambda j: (0, j))],
        out_specs=pl.BlockSpec((S5_GROUP, tn), lambda j: (0, j)), out_shape=_S((S5_GROUP, NC)),
        compiler_params=_cp("parallel"),
    )(dense)


def _s5_operands(w, m, tag):
    b_re_t = jnp.transpose(w['s5_b_re'], (2, 0, 1))
    b_im_t = jnp.transpose(w['s5_b_im'], (2, 0, 1))
    ldt = w['s5_log_dt'][:, None]
    ab, pw, bb = s5_prep(w['s5_a_re'], w['s5_a_im'], ldt, b_re_t, b_im_t, m, "s5_prep" + tag)
    abar = ab.reshape(2, 1, S5_NS)
    ptab = pw.reshape(2, m, S5_NS)
    bd = jnp.concatenate([_block_diag_b(bb[0]), _block_diag_b(bb[1])], axis=1).astype(bf16)
    cdt = jnp.concatenate([_block_diag_c(w['s5_c_re']), -_block_diag_c(w['s5_c_im'])], axis=0).astype(bf16)
    return dict(abar=abar, ptab=ptab, ptab_rev=ptab[:, ::-1, :], bd=bd, cdt=cdt, dvec=w['s5_d'][None, :],
                prep_in=(w['s5_a_re'], w['s5_a_im'], ldt, b_re_t, b_im_t))


def layer_fwd(x, p, w, tag):
    T = x.shape[0]
    m = _tile(T, SCAN_CHUNK) // SEGS
    s5 = _s5_operands(w, m, tag)
    h_bf = rms_fwd(x, w['g_pre'][None, :], "rms_fwd" + tag)
    win = w['w_in']
    usg = mm_nn(h_bf, win[:, :2 * S5_WIDTH], "proj_s5" + tag)
    lx = mm_nn(h_bf, win[:, 2 * S5_WIDTH:2 * S5_WIDTH + LRU_WIDTH], "proj_lx" + tag, tn=640)
    lg = mm_nn(h_bf, win[:, 2 * S5_WIDTH + LRU_WIDTH:2 * S5_WIDTH + 2 * LRU_WIDTH], "proj_lg" + tag, tn=640)
    gsl = mm_nn(h_bf, win[:, 2 * S5_WIDTH + 2 * LRU_WIDTH:], "proj_gate" + tag)
    ys, s_re, s_im, s_bf = s5_fwd(usg, s5['bd'], s5['cdt'], s5['dvec'], s5['abar'], s5['ptab'], "s5_fwd" + tag)
    glu, zs = s5_post_fwd(ys, usg, w['w_glu'], w['w_bs'], "s5_post_fwd" + tag)
    wa, wx = w['lru_w_a'].astype(bf16), w['lru_w_x'].astype(bf16)
    c, r, gi, hs = lru_fwd(lx, w['conv_w'], w['conv_b'][None, :], wa, wx, w['lru_b_a'][None, :], w['lru_b_x'][None, :],
                           w['lru_lambda'][None, :], "lru_fwd" + tag)
    zl, mix, q, pe, x_out = merge_fwd(hs, lg, zs, gsl, x, p, w['w_bl'], w['w_out'], w['g_post'][None, :],
                                      w['w_ple'], w['w_ple_gate'], "merge_fwd" + tag)
    saved = dict(x=x, p=p, h_bf=h_bf, usg=usg, lx=lx, lg=lg, gsl=gsl, ys=ys, s_re=s_re, s_im=s_im, s_bf=s_bf, glu=glu,
                 zs=zs, c=c, r=r, gi=gi, hs=hs, zl=zl, mix=mix, q=q, pe=pe, s5=s5, wa=wa, wx=wx)
    return x_out, saved


def layer_bwd(gx_out, w, sv, tag):
    s5 = sv['s5']
    (gres, gpe, gq, x1_bf, gmix, merged, gzl, yl, gzs, ggsl, g_h, g_lg, g_gpost) = merge_bwd(
        gx_out, sv['q'], sv['pe'], sv['mix'], sv['x'], sv['zl'], sv['zs'], sv['gsl'], sv['hs'], sv['lg'],
        w['w_ple_gate'], w['w_out'], w['w_bl'], w['g_post'][None, :], "merge_bwd" + tag)
    g = {}
    g['w_ple'] = mm_tn(sv['p'], gpe, "gw_ple" + tag, tk=256)
    g['w_ple_gate'] = mm_tn(x1_bf, gq, "gw_ple_gate" + tag)
    g['w_out'] = mm_tn(merged, gmix, "gw_out" + tag)
    g['w_bl'] = mm_tn(yl, gzl, "gw_bl" + tag, tk=640)
    g['g_post'] = g_gpost[0]
    (g_lx, g_wa, g_wx, g_ba, g_bx, g_lam, g_cb, g_cw) = lru_bwd(
        g_h, sv['hs'], sv['c'], sv['r'], sv['gi'], sv['lx'], w['conv_w'], sv['wa'], sv['wx'],
        w['lru_lambda'][None, :], "lru_bwd" + tag)
    g['lru_w_a'], g['lru_w_x'] = g_wa, g_wx
    g['lru_b_a'], g['lru_b_x'], g['lru_lambda'], g['conv_b'], g['conv_w'] = g_ba[0], g_bx[0], g_lam[0], g_cb[0], g_cw
    y2, gglu, ge, g_ys, g_sg = s5_post_bwd(gzs, sv['glu'], sv['usg'], sv['ys'], w['w_bs'], w['w_glu'], "s5_post_bwd" + tag)
    g['w_bs'] = mm_tn(y2, gzs, "gw_bs" + tag)
    g['w_glu'] = mm_tn(ge, gglu, "gw_glu" + tag)
    g_u, lam_bf, g_ab, g_d = s5_bwd(g_ys, sv['usg'], sv['s_re'], sv['s_im'], s5['bd'], s5['cdt'], s5['dvec'],
                                    s5['abar'], s5['ptab_rev'], "s5_bwd" + tag)
    g['s5_d'] = g_d[0]
    g_c = fold_groups(mm_tn(g_ys, sv['s_bf'], "gw_s5c" + tag, tn=1024), "fold_s5c" + tag)
    g_c = g_c.reshape(S5_GROUP, 2, S5_GROUPS, S5_STATE)
    g['s5_c_re'] = jnp.transpose(g_c[:, 0], (1, 0, 2))
    g['s5_c_im'] = -jnp.transpose(g_c[:, 1], (1, 0, 2))
    g_b = fold_groups(mm_tn(sv['usg'][:, :S5_WIDTH], lam_bf, "gw_s5b" + tag, tn=1024), "fold_s5b" + tag)
    g_bb = jnp.transpose(g_b.reshape(S5_GROUP, 2, S5_GROUPS, S5_STATE), (1, 0, 2, 3))
    g_are, g_aim, g_ldt, g_bre_t, g_bim_t = s5_prep_bwd(*s5['prep_in'], g_ab.reshape(2, S5_GROUPS, S5_STATE), g_bb,
                                                       "s5_prep_bwd" + tag)
    g['s5_a_re'], g['s5_a_im'], g['s5_log_dt'] = g_are, g_aim, g_ldt
    g['s5_b_re'] = jnp.transpose(g_bre_t, (1, 2, 0))
    g['s5_b_im'] = jnp.transpose(g_bim_t, (1, 2, 0))
    gproj = jnp.concatenate([g_u, g_sg, g_lx, g_lg, ggsl], axis=1)
    g['w_in'] = mm_tn(sv['h_bf'], gproj, "gw_in" + tag, tn=1408)
    gh = mm_nt(gproj, w['w_in'], "g_h" + tag)
    gx, g_gpre = rms_bwd(gh, sv['x'], w['g_pre'][None, :], gres, "rms_bwd" + tag)
    g['g_pre'] = g_gpre[0]
    return gx, g


def _as_2d(a):
    return a.reshape((-1, a.shape[-1])) if a.ndim > 1 else a.reshape((1, -1))


ELEMENTWISE_BLOCK_BYTES = 1024 * 1024


def _row_tile(R, C=LANES):
    cap = max(SUBLANES, min(R, ELEMENTWISE_BLOCK_BYTES // (4 * C)))
    for t in range(cap - cap % SUBLANES, 0, -SUBLANES):
        if R % t == 0:
            return t
    return R


def _adamw_update(w, gv, m, v):
    nm = ADAM_B1 * m + (1.0 - ADAM_B1) * gv
    nv = ADAM_B2 * v + (1.0 - ADAM_B2) * (gv * gv)
    bc1 = 1.0 - ADAM_B1 ** ADAM_STEP
    bc2 = 1.0 - ADAM_B2 ** ADAM_STEP
    return -ADAM_LR * ((nm / bc1) / (jnp.sqrt(nv / bc2) + ADAM_EPS) + ADAM_WD * w), nm, nv


def adamw(w, g, m, v, name):
    shape = w.shape
    w2, g2, m2, v2 = _as_2d(w), _as_2d(g), _as_2d(m), _as_2d(v)
    R, C = w2.shape
    tr = _row_tile(R, C)

    def body(w_ref, g_ref, m_ref, v_ref, d_ref, nm_ref, nv_ref):
        d_ref[...], nm_ref[...], nv_ref[...] = _adamw_update(w_ref[...], g_ref[...], m_ref[...], v_ref[...])

    spec = lambda: pl.BlockSpec((tr, C), lambda i: (i, 0))
    d, nm, nv = pl.pallas_call(
        body, name=name, grid=(R // tr,), in_specs=[spec() for _ in range(4)], out_specs=[spec() for _ in range(3)],
        out_shape=[_S((R, C))] * 3, compiler_params=_cp("parallel"),
    )(w2, g2, m2, v2)
    return d.reshape(shape), nm.reshape(shape), nv.reshape(shape)


def adamw_reduce(w, a, theirs, m, v, chip, name):
    shape = w.shape
    w2, m2, v2 = _as_2d(w), _as_2d(m), _as_2d(v)
    R, C = w2.shape
    a3, t3 = a.reshape(4, R, C), theirs.reshape(3, R, C)
    tr = _row_tile(R, C)

    def body(chip_ref, w_ref, a_ref, t_ref, m_ref, v_ref, g_ref, d_ref, nm_ref, nv_ref):
        gv = ((a_ref[0] + t_ref[0]) + t_ref[1]) + t_ref[2]
        g_ref[...] = gv
        d_ref[...], nm_ref[...], nv_ref[...] = _adamw_update(w_ref[...], gv, m_ref[...], v_ref[...])

    spec = lambda: pl.BlockSpec((tr, C), lambda i, c: (i, 0))
    grid_spec = pltpu.PrefetchScalarGridSpec(
        num_scalar_prefetch=1, grid=(R // tr,),
        in_specs=[spec(), pl.BlockSpec((1, tr, C), lambda i, c: (c[0], i, 0)), pl.BlockSpec((3, tr, C), lambda i, c: (0, i, 0)),
                  spec(), spec()],
        out_specs=[spec() for _ in range(4)])
    g, d, nm, nv = pl.pallas_call(
        body, name=name, grid_spec=grid_spec, out_shape=[_S((R, C))] * 4, compiler_params=_cp("parallel"),
    )(chip, w2, a3, t3, m2, v2)
    return g.reshape(shape), d.reshape(shape), nm.reshape(shape), nv.reshape(shape)


MESH = pl.DeviceIdType.MESH
ANY = pl.BlockSpec(memory_space=pl.ANY)


def _place():
    return lax.axis_index("x"), lax.axis_index("y"), lax.axis_index("c")


def _other_chips(mx, my):
    return [(1 - mx, my), (mx, 1 - my), (1 - mx, 1 - my)]


def all_gather(bufs, name):
    nb = len(bufs)

    def body(*refs):
        out_refs, send_sems, recv_sems = refs[nb:2 * nb], refs[2 * nb], refs[2 * nb + 1]
        mx, my, mc = _place()
        me, sibling = (mx, my, mc), (mx, my, 1 - mc)
        chips = _other_chips(mx, my)

        def copy(b, k, block, to):
            px, py, pc = block
            rows = out_refs[b].at[4 * px + 2 * py + pc]
            return pltpu.make_async_remote_copy(
                src_ref=rows, dst_ref=rows, send_sem=send_sems.at[b, k], recv_sem=recv_sems.at[b, k],
                device_id=to, device_id_type=MESH)

        first = []
        for b in range(nb):
            first.append(copy(b, 0, me, sibling))
            first += [copy(b, 1 + j, me, (*chip, mc)) for j, chip in enumerate(chips)]
        for cp in first:
            cp.start()
        passed = []
        for j, chip in enumerate(chips):
            for b in range(nb):
                copy(b, 1 + j, (*chip, mc), me).wait_recv()
                passed.append(copy(b, 4 + j, (*chip, mc), sibling))
                passed[-1].start()
        for b in range(nb):
            copy(b, 0, sibling, me).wait_recv()
            for j, chip in enumerate(chips):
                copy(b, 4 + j, (*chip, 1 - mc), me).wait_recv()
        for cp in first + passed:
            cp.wait_send()

    outs = pl.pallas_call(
        body, name=name, out_shape=[_S(b.shape, b.dtype) for b in bufs], in_specs=[ANY] * nb, out_specs=[ANY] * nb,
        input_output_aliases={i: i for i in range(nb)},
        scratch_shapes=[pltpu.SemaphoreType.DMA((nb, 7)), pltpu.SemaphoreType.DMA((nb, 7))],
    )(*bufs)
    return list(outs)


def own_block_in_place(shard, me):
    buf = lax.empty((N_DEV,) + shard.shape, shard.dtype)
    return lax.dynamic_update_slice(buf, shard[None], (me,) + (0,) * shard.ndim)


def exchange_sibling(gs, name):
    nb = len(gs)

    def body(*refs):
        g_refs, recv_refs, send_sems, recv_sems = refs[:nb], refs[nb:2 * nb], refs[2 * nb], refs[2 * nb + 1]
        mx, my, mc = _place()
        copies = [pltpu.make_async_remote_copy(
            src_ref=g_refs[b].at[2 * k + 1 - mc], dst_ref=recv_refs[b].at[k], send_sem=send_sems.at[b, k],
            recv_sem=recv_sems.at[b, k], device_id=(mx, my, 1 - mc), device_id_type=MESH)
            for b in range(nb) for k in range(4)]
        for cp in copies:
            cp.start()
        for cp in copies:
            cp.wait()

    outs = pl.pallas_call(
        body, name=name, out_shape=[_S((4,) + g.shape[1:], g.dtype) for g in gs], in_specs=[ANY] * nb,
        out_specs=[ANY] * nb,
        scratch_shapes=[pltpu.SemaphoreType.DMA((nb, 4)), pltpu.SemaphoreType.DMA((nb, 4))],
    )(*gs)
    return list(outs)


def exchange_chips(parts, name):
    nb = len(parts)

    def body(*refs):
        a_refs, recv_refs, send_sems, recv_sems = refs[:nb], refs[nb:2 * nb], refs[2 * nb], refs[2 * nb + 1]
        mx, my, mc = _place()
        copies = [pltpu.make_async_remote_copy(
            src_ref=a_refs[b].at[2 * px + py], dst_ref=recv_refs[b].at[j], send_sem=send_sems.at[b, j],
            recv_sem=recv_sems.at[b, j], device_id=(px, py, mc), device_id_type=MESH)
            for b in range(nb) for j, (px, py) in enumerate(_other_chips(mx, my))]
        for cp in copies:
            cp.start()
        for cp in copies:
            cp.wait()

    outs = pl.pallas_call(
        body, name=name, out_shape=[_S((3,) + a.shape[1:], a.dtype) for a in parts], in_specs=[ANY] * nb,
        out_specs=[ANY] * nb,
        scratch_shapes=[pltpu.SemaphoreType.DMA((nb, 3)), pltpu.SemaphoreType.DMA((nb, 3))],
    )(*parts)
    return list(outs)


def add_sibling(g, theirs, core, name):
    shp = theirs.shape
    C = shp[-1]
    R = math.prod(shp[1:-1])
    tr = _row_tile(R, C)

    def body(core_ref, g_ref, t_ref, o_ref):
        o_ref[...] = g_ref[...] + t_ref[...]

    grid_spec = pltpu.PrefetchScalarGridSpec(
        num_scalar_prefetch=1, grid=(4, R // tr),
        in_specs=[pl.BlockSpec((1, tr, C), lambda k, i, c: (2 * k + c[0], i, 0)),
                  pl.BlockSpec((1, tr, C), lambda k, i, c: (k, i, 0))],
        out_specs=pl.BlockSpec((1, tr, C), lambda k, i, c: (k, i, 0)))
    out = pl.pallas_call(
        body, name=name, grid_spec=grid_spec, out_shape=_S((4, R, C), g.dtype),
        compiler_params=_cp("parallel", "parallel"),
    )(core, g.reshape(N_DEV, R, C), theirs.reshape(4, R, C))
    return out.reshape(shp)


def add_chips(a, theirs, chip, name):
    _, R, C = a.shape
    tr = _row_tile(R, C)

    def body(chip_ref, a_ref, t_ref, out_ref):
        out_ref[...] = ((a_ref[0] + t_ref[0]) + t_ref[1]) + t_ref[2]

    grid_spec = pltpu.PrefetchScalarGridSpec(
        num_scalar_prefetch=1, grid=(R // tr,),
        in_specs=[pl.BlockSpec((1, tr, C), lambda i, c: (c[0], i, 0)), pl.BlockSpec((3, tr, C), lambda i, c: (0, i, 0))],
        out_specs=pl.BlockSpec((tr, C), lambda i, c: (i, 0)))
    return pl.pallas_call(
        body, name=name, grid_spec=grid_spec, out_shape=_S((R, C), a.dtype), compiler_params=_cp("parallel"),
    )(chip, a, theirs)


def _round_up(n, q):
    return (n + q - 1) // q * q


def _full_to_shards(full, axis):
    shp = full.shape
    s = shp[axis] // N_DEV
    cut = full.reshape(shp[:axis] + (N_DEV, s) + shp[axis + 1:])
    return jnp.moveaxis(cut, axis, 0)


def _shards_to_full(parts, axis):
    shp = list(parts.shape[1:])
    shp[axis] *= N_DEV
    return jnp.moveaxis(parts, 0, axis).reshape(tuple(shp))


def kernel(x, p, g_pre, w_in, s5_a_re, s5_a_im, s5_log_dt, s5_b_re, s5_b_im, s5_c_re, s5_c_im, s5_d, w_glu, w_bs, conv_w, conv_b, lru_w_a, lru_b_a, lru_w_x, lru_b_x, lru_lambda, w_bl, w_out, g_post, w_ple, w_ple_gate, loss_target, m_g_pre, m_w_in, m_s5_a_re, m_s5_a_im, m_s5_log_dt, m_s5_b_re, m_s5_b_im, m_s5_c_re, m_s5_c_im, m_s5_d, m_w_glu, m_w_bs, m_conv_w, m_conv_b, m_lru_w_a, m_lru_b_a, m_lru_w_x, m_lru_b_x, m_lru_lambda, m_w_bl, m_w_out, m_g_post, m_w_ple, m_w_ple_gate, v_g_pre, v_w_in, v_s5_a_re, v_s5_a_im, v_s5_log_dt, v_s5_b_re, v_s5_b_im, v_s5_c_re, v_s5_c_im, v_s5_d, v_w_glu, v_w_bs, v_conv_w, v_conv_b, v_lru_w_a, v_lru_b_a, v_lru_w_x, v_lru_b_x, v_lru_lambda, v_w_bl, v_w_out, v_g_post, v_w_ple, v_w_ple_gate):
    given = dict(locals())
    W = {n: given[n] for n in WEIGHTS}
    M = {n: given["m_" + n] for n in WEIGHTS}
    V = {n: given["v_" + n] for n in WEIGHTS}
    xs, ps, target = x[0], p[:, 0], loss_target[0]

    mx, my, mc = _place()
    me = 4 * mx + 2 * my + mc
    core = jnp.reshape(mc, (1,)).astype(jnp.int32)
    chip = jnp.reshape(2 * mx + my, (1,)).astype(jnp.int32)

    names = list(SHARDED)
    shards = [W[n].astype(bf16) if n in GATHER_BF16 else W[n] for n in names]
    gathered = all_gather([own_block_in_place(s, me) for s in shards], "comm_gather_weights")
    full = {n: _shards_to_full(g, SHARDED[n]) for n, g in zip(names, gathered)}

    def layer_weights(i):
        return {n: (full[n][i] if n in SHARDED else W[n][i]) for n in WEIGHTS}

    act, saved = xs, []
    for i in range(DEPTH):
        act, sv = layer_fwd(act, ps[i], layer_weights(i), "_l%d" % i)
        saved.append(sv)
    loss_part, gact = loss_head(act, target, "loss_head")
    grads = [None] * DEPTH
    for i in reversed(range(DEPTH)):
        gact, grads[i] = layer_bwd(gact, layer_weights(i), saved[i], "_l%d" % i)
    loss = lax.psum(loss_part[0, 0], ("x", "y", "c"))
    gfull = {n: jnp.stack([grads[i][n].reshape(full[n].shape[1:] if n in SHARDED else W[n].shape[1:])
                           for i in range(DEPTH)]) for n in WEIGHTS}

    rep = jnp.concatenate([gfull[n].reshape(-1) for n in REPLICATED])
    n_rep = _round_up(rep.shape[0], N_DEV * SUBLANES * LANES)
    rep_blocks = jnp.pad(rep, (0, n_rep - rep.shape[0])).reshape(N_DEV, -1, LANES)
    blocks = [_full_to_shards(gfull[n], SHARDED[n]) for n in names] + [rep_blocks]
    theirs = exchange_sibling(blocks, "comm_reduce_sibling")
    parts = [add_sibling(b, t, core, "reduce_add_sibling_%d" % k) for k, (b, t) in enumerate(zip(blocks, theirs))]
    others = exchange_chips(parts, "comm_reduce_chips")

    red, deltas, new_m, new_v = {}, {}, {}, {}
    for n, a, t in zip(names, parts, others):
        red[n], deltas[n], new_m[n], new_v[n] = adamw_reduce(W[n], a, t, M[n], V[n], chip, "adamw_" + n)
    piece = add_chips(parts[-1], others[-1], chip, "reduce_add_chips")
    rep_all = all_gather([own_block_in_place(piece, me)], "comm_gather_replicated")[0].reshape(-1)
    off = 0
    for n in REPLICATED:
        k = math.prod(W[n].shape)
        red[n] = rep_all[off:off + k].reshape(W[n].shape)
        off += k
        deltas[n], new_m[n], new_v[n] = adamw(W[n], red[n], M[n], V[n], "adamw_" + n)
    return (loss, gact[None], *[red[n] for n in WEIGHTS], *[deltas[n] for n in WEIGHTS],
            *[new_m[n] for n in WEIGHTS], *[new_v[n] for n in WEIGHTS])
```

```python
import math

import jax
import jax.numpy as jnp
from jax import lax
from jax.experimental import pallas as pl
from jax.experimental.pallas import tpu as pltpu

f32 = jnp.float32
bf16 = jnp.bfloat16

D_MODEL = 1024
DEPTH = 2
PLE_DIM = 256
NORM_EPS = 1e-6
S5_WIDTH = 512
S5_GROUP = 16
S5_GROUPS = 32
S5_STATE = 64
S5_NS = S5_GROUPS * S5_STATE
LRU_WIDTH = 1280
LRU_HEADS = 10
LRU_HEAD_DIM = 128
LRU_C = 8.0
CONV_WIDTH = 4
N_DEV = 8

ADAM_LR = 0.001
ADAM_B1 = 0.9
ADAM_B2 = 0.999
ADAM_EPS = 1e-08
ADAM_WD = 0.01
ADAM_STEP = 10

LANES = 128
SUBLANES = 8
SEGS = SUBLANES
SCAN_CHUNK = 256
TOKEN_TILE = 256
MM_TILE_M = 512
PAIR = 2 * SUBLANES
VMEM_LIMIT_BYTES = 56 * 1024 * 1024
ELEMENTWISE_BLOCK_BYTES = 1024 * 1024

WEIGHTS = ['g_pre', 'w_in', 's5_a_re', 's5_a_im', 's5_log_dt', 's5_b_re', 's5_b_im', 's5_c_re', 's5_c_im',
           's5_d', 'w_glu', 'w_bs', 'conv_w', 'conv_b', 'lru_w_a', 'lru_b_a', 'lru_w_x', 'lru_b_x',
           'lru_lambda', 'w_bl', 'w_out', 'g_post', 'w_ple', 'w_ple_gate']
SHARDED = {'w_in': 2, 'w_glu': 2, 'w_bs': 2, 'conv_w': 2, 'w_bl': 1, 'w_out': 1, 'w_ple': 2, 'w_ple_gate': 1}
GATHER_BF16 = ['w_in', 'w_glu', 'w_bs', 'w_bl', 'w_out', 'w_ple', 'w_ple_gate']
REPLICATED = [n for n in WEIGHTS if n not in SHARDED]


def _sig(x):
    return 1.0 / (1.0 + jnp.exp(-x))


def _gelu_parts(x):
    k = math.sqrt(2.0 / math.pi)
    t = jnp.tanh(k * (x + 0.044715 * x * x * x))
    return t, k


def _gelu(x):
    t, _ = _gelu_parts(x)
    return 0.5 * x * (1.0 + t)


def _gelu_grad(x):
    t, k = _gelu_parts(x)
    return 0.5 * (1.0 + t) + 0.5 * x * (1.0 - t * t) * k * (1.0 + 3.0 * 0.044715 * x * x)


def _one_minus_exp(z):
    series = -z * (1.0 + z * (0.5 + z * (1.0 / 6.0 + z * (1.0 / 24.0 + z * (1.0 / 120.0)))))
    return jnp.where(z > -0.05, series, 1.0 - jnp.exp(z))


def _softplus_neg(lam):
    return jnp.maximum(-lam, 0.0) + jnp.log(1.0 + jnp.exp(-jnp.abs(lam)))


def _dot(a, b):
    return jnp.dot(a, b, preferred_element_type=f32)


def _dot_nt(a, b):
    return lax.dot_general(a, b, (((1,), (1,)), ((), ())), preferred_element_type=f32)


def _dot_tn(a, b):
    return lax.dot_general(a, b, (((0,), (0,)), ((), ())), preferred_element_type=f32)


def _S(shape, dtype=f32):
    return jax.ShapeDtypeStruct(shape, dtype)


def _full(shape):
    nd = len(shape)
    return pl.BlockSpec(shape, lambda *_: (0,) * nd)


def _rows(tile, width, col=0):
    return pl.BlockSpec((tile, width), lambda i: (i, col))


def _cp(*semantics):
    return pltpu.CompilerParams(dimension_semantics=semantics or None, vmem_limit_bytes=VMEM_LIMIT_BYTES)


def _tile(n, want):
    t = min(n, want)
    assert n % t == 0, (n, want)
    return t


def _row_tile(R, C=LANES):
    cap = max(SUBLANES, min(R, ELEMENTWISE_BLOCK_BYTES // (4 * C)))
    for t in range(cap - cap % SUBLANES, 0, -SUBLANES):
        if R % t == 0:
            return t
    return R


def _lanes(j):
    return slice(LANES * j, LANES * (j + 1))


def _step_rows(k, n=SUBLANES):
    return pl.ds(pl.multiple_of(k * n, n), n)


def to_scan_order(a):
    T, C = a.shape
    tc = _tile(T, SCAN_CHUNK)
    return a.reshape(T // tc, SEGS, tc // SEGS, C).transpose(0, 2, 1, 3).reshape(T, C)


def from_scan_order(a):
    T, C = a.shape
    tc = _tile(T, SCAN_CHUNK)
    return a.reshape(T // tc, tc // SEGS, SEGS, C).transpose(0, 2, 1, 3).reshape(T, C)


def mm_nn(a, b, name, out_dtype=f32, tn=512, tk=1024):
    M, K = a.shape
    N = b.shape[1]
    tm, tn, tk = _tile(M, MM_TILE_M), _tile(N, tn), _tile(K, tk)
    nk = K // tk

    def body(a_ref, b_ref, o_ref, acc_ref):
        k = pl.program_id(2)
        part = _dot(a_ref[...].astype(bf16), b_ref[...].astype(bf16))

        @pl.when(k == 0)
        def _():
            acc_ref[...] = part

        @pl.when(k > 0)
        def _():
            acc_ref[...] += part

        @pl.when(k == nk - 1)
        def _():
            o_ref[...] = acc_ref[...].astype(out_dtype)

    return pl.pallas_call(
        body, name=name, grid=(M // tm, N // tn, nk),
        in_specs=[pl.BlockSpec((tm, tk), lambda i, j, k: (i, k)), pl.BlockSpec((tk, tn), lambda i, j, k: (k, j))],
        out_specs=pl.BlockSpec((tm, tn), lambda i, j, k: (i, j)),
        out_shape=_S((M, N), out_dtype), scratch_shapes=[pltpu.VMEM((tm, tn), f32)],
        compiler_params=_cp("parallel", "parallel", "arbitrary"),
    )(a, b)


def mm_nt(a, b, name, out_dtype=f32, tn=512, tk=1408):
    M, K = a.shape
    N = b.shape[0]
    tm, tn, tk = _tile(M, MM_TILE_M), _tile(N, tn), _tile(K, tk)
    nk = K // tk

    def body(a_ref, b_ref, o_ref, acc_ref):
        k = pl.program_id(2)
        part = _dot_nt(a_ref[...].astype(bf16), b_ref[...].astype(bf16))

        @pl.when(k == 0)
        def _():
            acc_ref[...] = part

        @pl.when(k > 0)
        def _():
            acc_ref[...] += part

        @pl.when(k == nk - 1)
        def _():
            o_ref[...] = acc_ref[...].astype(out_dtype)

    return pl.pallas_call(
        body, name=name, grid=(M // tm, N // tn, nk),
        in_specs=[pl.BlockSpec((tm, tk), lambda i, j, k: (i, k)), pl.BlockSpec((tn, tk), lambda i, j, k: (j, k))],
        out_specs=pl.BlockSpec((tm, tn), lambda i, j, k: (i, j)),
        out_shape=_S((M, N), out_dtype), scratch_shapes=[pltpu.VMEM((tm, tn), f32)],
        compiler_params=_cp("parallel", "parallel", "arbitrary"),
    )(a, b)


def mm_tn(a, b, name, tk=512, tn=512):
    M, K = a.shape
    N = b.shape[1]
    tm, tk, tn = _tile(M, MM_TILE_M), _tile(K, tk), _tile(N, tn)

    def body(a_ref, b_ref, o_ref):
        m = pl.program_id(2)
        part = _dot_tn(a_ref[...].astype(bf16), b_ref[...].astype(bf16))

        @pl.when(m == 0)
        def _():
            o_ref[...] = part

        @pl.when(m > 0)
        def _():
            o_ref[...] += part

    return pl.pallas_call(
        body, name=name, grid=(K // tk, N // tn, M // tm),
        in_specs=[pl.BlockSpec((tm, tk), lambda i, j, m: (m, i)), pl.BlockSpec((tm, tn), lambda i, j, m: (m, j))],
        out_specs=pl.BlockSpec((tk, tn), lambda i, j, m: (i, j)),
        out_shape=_S((K, N), f32),
        compiler_params=_cp("parallel", "parallel", "arbitrary"),
    )(a, b)


def rms_fwd(x, g, name):
    T = x.shape[0]
    tm = _tile(T, TOKEN_TILE)

    def body(x_ref, g_ref, h_ref):
        xv = x_ref[...]
        r = lax.rsqrt(jnp.mean(xv * xv, axis=-1, keepdims=True) + NORM_EPS)
        h_ref[...] = (xv * r * g_ref[...]).astype(bf16)

    return pl.pallas_call(
        body, name=name, grid=(T // tm,),
        in_specs=[_rows(tm, D_MODEL), _full((1, D_MODEL))], out_specs=_rows(tm, D_MODEL),
        out_shape=_S((T, D_MODEL), bf16), compiler_params=_cp("parallel"),
    )(x, g)


def rms_bwd(gh, x, g, gres, name):
    T = x.shape[0]
    tm = _tile(T, TOKEN_TILE)

    def body(gh_ref, x_ref, g_ref, gres_ref, gx_ref, gg_ref):
        i = pl.program_id(0)
        xv = x_ref[...]
        ghv = gh_ref[...]
        r = lax.rsqrt(jnp.mean(xv * xv, axis=-1, keepdims=True) + NORM_EPS)
        nrm = xv * r
        gy = ghv * g_ref[...]
        gx_ref[...] = gres_ref[...] + r * (gy - nrm * jnp.mean(gy * nrm, axis=-1, keepdims=True))
        part = jnp.sum(ghv * nrm, axis=0, keepdims=True)

        @pl.when(i == 0)
        def _():
            gg_ref[...] = part

        @pl.when(i > 0)
        def _():
            gg_ref[...] += part

    return pl.pallas_call(
        body, name=name, grid=(T // tm,),
        in_specs=[_rows(tm, D_MODEL), _rows(tm, D_MODEL), _full((1, D_MODEL)), _rows(tm, D_MODEL)],
        out_specs=[_rows(tm, D_MODEL), _full((1, D_MODEL))],
        out_shape=[_S((T, D_MODEL)), _S((1, D_MODEL))], compiler_params=_cp("arbitrary"),
    )(gh, x, g, gres)


def _s5_discretise(a_re, a_im, log_dt, b_re_t, b_im_t):
    dt = jnp.exp(log_dt)
    mag = jnp.exp(a_re * dt)
    ab_re = mag * jnp.cos(a_im * dt)
    ab_im = mag * jnp.sin(a_im * dt)
    den = a_re * a_re + a_im * a_im
    nr, ni = ab_re - 1.0, ab_im
    z_re = (nr * a_re + ni * a_im) / den
    z_im = (ni * a_re - nr * a_im) / den
    bb_re = z_re[None] * b_re_t - z_im[None] * b_im_t
    bb_im = z_re[None] * b_im_t + z_im[None] * b_re_t
    return ab_re, ab_im, bb_re, bb_im


def s5_prep(a_re, a_im, log_dt, b_re_t, b_im_t, m, name):
    G, N = a_re.shape

    def body(are_ref, aim_ref, ldt_ref, bre_ref, bim_ref, ab_ref, pw_ref, bb_ref):
        are, aim, ldt = are_ref[...], aim_ref[...], ldt_ref[...]
        ab_re, ab_im, bb_re, bb_im = _s5_discretise(are, aim, ldt, bre_ref[...], bim_ref[...])
        ab_ref[0], ab_ref[1] = ab_re, ab_im
        bb_ref[0], bb_ref[1] = bb_re, bb_im
        dt = jnp.exp(ldt)
        for k in range(m):
            mag = jnp.exp(are * dt * (k + 1.0))
            pw_ref[0, k] = mag * jnp.cos(aim * dt * (k + 1.0))
            pw_ref[1, k] = mag * jnp.sin(aim * dt * (k + 1.0))

    return pl.pallas_call(
        body, name=name,
        out_shape=[_S((2, G, N)), _S((2, m, G, N)), _S((2, S5_GROUP, G, N))], compiler_params=_cp(),
    )(a_re, a_im, log_dt, b_re_t, b_im_t)


def s5_prep_bwd(a_re, a_im, log_dt, b_re_t, b_im_t, g_ab, g_bb, name):
    G, N = a_re.shape

    def body(are_ref, aim_ref, ldt_ref, bre_ref, bim_ref, gab_ref, gbb_ref, o_are, o_aim, o_ldt, o_bre, o_bim):
        _, vjp = jax.vjp(_s5_discretise, are_ref[...], aim_ref[...], ldt_ref[...], bre_ref[...], bim_ref[...])
        g_are, g_aim, g_ldt, g_bre, g_bim = vjp((gab_ref[0], gab_ref[1], gbb_ref[0], gbb_ref[1]))
        o_are[...], o_aim[...], o_ldt[...], o_bre[...], o_bim[...] = g_are, g_aim, g_ldt, g_bre, g_bim

    return pl.pallas_call(
        body, name=name,
        out_shape=[_S((G, N)), _S((G, N)), _S((G, 1)), _S((S5_GROUP, G, N)), _S((S5_GROUP, G, N))],
        compiler_params=_cp(),
    )(a_re, a_im, log_dt, b_re_t, b_im_t, g_ab, g_bb)


NB_S5 = S5_NS // LANES
CB_S5 = S5_WIDTH // LANES
SB_PER_CB = NB_S5 // CB_S5
GRP_PER_SB = LANES // S5_STATE
S5_JB = 8


def _bdb_mask():
    j = jnp.arange(NB_S5)
    own_rows = (j[:, None] % SB_PER_CB == jnp.arange(SB_PER_CB)[None, :]).astype(f32)
    eye = jnp.eye(GRP_PER_SB, dtype=f32)
    return own_rows[:, :, None, None, None, None, None] * eye[None, None, :, None, None, :, None]


def _pack_bdb(bb):
    v = jnp.transpose(bb.reshape(2, S5_GROUP, NB_S5, GRP_PER_SB, S5_STATE), (2, 3, 1, 0, 4))
    full = v[:, None, :, :, :, None, :] * _bdb_mask()
    return full.reshape(NB_S5, LANES, 2 * LANES)


def _unpack_bdb(g_bdb):
    g7 = g_bdb.reshape(NB_S5, SB_PER_CB, GRP_PER_SB, S5_GROUP, 2, GRP_PER_SB, S5_STATE)
    v = jnp.sum(g7 * _bdb_mask(), axis=(1, 5))
    return jnp.transpose(v, (3, 2, 0, 1, 4)).reshape(2, S5_GROUP, S5_GROUPS, S5_STATE)


def _pack_cdb(c_re, c_im):
    gl = S5_GROUPS // CB_S5
    c2 = jnp.stack([c_re, -c_im]).reshape(2, CB_S5, gl, S5_GROUP, S5_STATE)
    eye = jnp.eye(gl, dtype=f32)
    full = jnp.transpose(c2, (1, 0, 2, 4, 3))[:, :, :, :, None, :] * eye[None, None, :, None, :, None]
    return full.reshape(CB_S5, 2 * SB_PER_CB * LANES, LANES)


def _unpack_cdb(g_cdb):
    gl = S5_GROUPS // CB_S5
    g6 = g_cdb.reshape(CB_S5, 2, gl, S5_STATE, gl, S5_GROUP)
    eye = jnp.eye(gl, dtype=f32)
    v = jnp.sum(g6 * eye[None, None, :, None, :, None], axis=4)
    v = jnp.transpose(v, (1, 0, 2, 4, 3)).reshape(2, S5_GROUPS, S5_GROUP, S5_STATE)
    return v[0], -v[1]


def _state_cat(ref, c):
    w = SB_PER_CB * LANES
    return jnp.concatenate([ref[:, w * c:w * (c + 1)], ref[:, S5_NS + w * c:S5_NS + w * (c + 1)]], axis=1)


def _state_pair(ref, j):
    return jnp.concatenate([ref[:, _lanes(j)], ref[:, S5_NS + LANES * j:S5_NS + LANES * (j + 1)]], axis=1)


def s5_fwd(usg, bdb, cdb, dvec, abar_b, ptab_b, name):
    T = usg.shape[0]
    tc = _tile(T, SCAN_CHUNK)
    m = tc // SEGS
    assert ptab_b.shape == (2, m, SEGS, S5_NS) and m % 2 == 0

    def body(u_ref, bdb_ref, cdb_ref, d_ref, a_ref, p_ref, ys_ref, sre_ref, sim_ref, sbf_ref,
             src_re, src_im, dst_re, dst_im, cin_ref, carry_ref):
        i = pl.program_id(0)

        @pl.when(i == 0)
        def _():
            carry_ref[...] = jnp.zeros_like(carry_ref)

        u = u_ref[...]
        ub = u.astype(bf16)
        for j in range(NB_S5):
            bu = _dot(ub[:, _lanes(j // SB_PER_CB)], bdb_ref[j])
            src_re[:, _lanes(j)] = bu[:, :LANES]
            src_im[:, _lanes(j)] = bu[:, LANES:]
        for j0 in range(0, NB_S5, S5_JB):
            def kstep(k, st):
                rows = _step_rows(k)
                out = []
                for q in range(S5_JB):
                    ln = _lanes(j0 + q)
                    sr, si = st[2 * q], st[2 * q + 1]
                    ar, ai = a_ref[0, :, ln], a_ref[1, :, ln]
                    nr = ar * sr - ai * si + src_re[rows, ln]
                    ni = ar * si + ai * sr + src_im[rows, ln]
                    dst_re[rows, ln] = nr
                    dst_im[rows, ln] = ni
                    out += [nr, ni]
                return tuple(out)

            ends = lax.fori_loop(0, m, kstep, tuple(jnp.zeros((SEGS, LANES), f32) for _ in range(2 * S5_JB)))
            for q in range(S5_JB):
                ln = _lanes(j0 + q)
                er, ei = ends[2 * q], ends[2 * q + 1]
                cr, ci = carry_ref[0, :, ln], carry_ref[1, :, ln]
                amr, ami = p_ref[0, m - 1, 0:1, ln], p_ref[1, m - 1, 0:1, ln]
                rows_r, rows_i = [], []
                for s in range(SEGS):
                    rows_r.append(cr)
                    rows_i.append(ci)
                    cr, ci = (er[s:s + 1, :] + amr * cr - ami * ci, ei[s:s + 1, :] + amr * ci + ami * cr)
                cin_ref[0, 0:SEGS, ln] = _stack_rows(rows_r)
                cin_ref[1, 0:SEGS, ln] = _stack_rows(rows_i)
                carry_ref[0, :, ln] = cr
                carry_ref[1, :, ln] = ci
        cin_ref[:, SEGS:, :] = cin_ref[:, 0:SEGS, :]

        def fix(k2, _):
            rows = _step_rows(k2, PAIR)
            pr = p_ref[0, pl.ds(2 * k2, 2)].reshape(PAIR, S5_NS)
            pi = p_ref[1, pl.ds(2 * k2, 2)].reshape(PAIR, S5_NS)
            cr, ci = cin_ref[0], cin_ref[1]
            sr = dst_re[rows, :] + pr * cr - pi * ci
            si = dst_im[rows, :] + pr * ci + pi * cr
            sre_ref[rows, :] = sr
            sim_ref[rows, :] = si
            sbf_ref[rows, 0:S5_NS] = sr.astype(bf16)
            sbf_ref[rows, S5_NS:] = si.astype(bf16)
            return 0

        lax.fori_loop(0, m // 2, fix, 0)
        for c in range(CB_S5):
            ys_ref[:, _lanes(c)] = _dot(_state_cat(sbf_ref, c), cdb_ref[c]) + d_ref[:, _lanes(c)] * u[:, _lanes(c)]

    st = lambda w: _rows(tc, w)
    return pl.pallas_call(
        body, name=name, grid=(T // tc,),
        in_specs=[_rows(tc, S5_WIDTH, 0), _full(bdb.shape), _full(cdb.shape), _full((1, S5_WIDTH)),
                  _full((2, SEGS, S5_NS)), _full((2, m, SEGS, S5_NS))],
        out_specs=[st(S5_WIDTH), st(S5_NS), st(S5_NS), st(2 * S5_NS)],
        out_shape=[_S((T, S5_WIDTH)), _S((T, S5_NS)), _S((T, S5_NS)), _S((T, 2 * S5_NS), bf16)],
        scratch_shapes=[pltpu.VMEM((tc, S5_NS), f32)] * 4 + [pltpu.VMEM((2, PAIR, S5_NS), f32),
                                                             pltpu.VMEM((2, 1, S5_NS), f32)],
        compiler_params=_cp("arbitrary"),
    )(usg, bdb, cdb, dvec, abar_b, ptab_b)


def s5_bwd(gys, usg, s_re, s_im, s_bf, bdb, cdb, dvec, abar_b, ptab_rev_b, name):
    T = gys.shape[0]
    tc = _tile(T, SCAN_CHUNK)
    m = tc // SEGS
    nch = T // tc
    hb = tc // SUBLANES

    def body(gy_ref, u_ref, sre_ref, sim_ref, hre_ref, him_ref, sbf_ref, bdb_ref, cdb_ref, d_ref, a_ref, p_ref,
             gu_ref, gab_ref, gd_ref, gbdb_ref, gcdb_ref,
             src_re, src_im, dst_re, dst_im, lam_ref, cin_ref, acc_ref, carry_ref):
        i = pl.program_id(0)

        @pl.when(i == 0)
        def _():
            carry_ref[...] = jnp.zeros_like(carry_ref)
            for ref in (gab_ref, gd_ref, gbdb_ref, gcdb_ref):
                ref[...] = jnp.zeros_like(ref)

        first = i == nch - 1
        gy = gy_ref[...]
        gyb = gy.astype(bf16)
        u = u_ref[...]
        ub = u.astype(bf16)
        w = SB_PER_CB * LANES
        for c in range(CB_S5):
            gs = _dot_nt(gyb[:, _lanes(c)], cdb_ref[c])
            src_re[:, w * c:w * (c + 1)] = gs[:, :w]
            src_im[:, w * c:w * (c + 1)] = gs[:, w:]
            gcdb_ref[c] += _dot_tn(_state_cat(sbf_ref, c), gyb[:, _lanes(c)])
        for j0 in range(0, NB_S5, S5_JB):
            def kstep(kk, st):
                rows = _step_rows(m - 1 - kk)
                out = []
                for q in range(S5_JB):
                    ln = _lanes(j0 + q)
                    lr, li = st[2 * q], st[2 * q + 1]
                    ar, ai = a_ref[0, :, ln], a_ref[1, :, ln]
                    nr = ar * lr + ai * li + src_re[rows, ln]
                    ni = ar * li - ai * lr + src_im[rows, ln]
                    dst_re[rows, ln] = nr
                    dst_im[rows, ln] = ni
                    out += [nr, ni]
                return tuple(out)

            ends = lax.fori_loop(0, m, kstep, tuple(jnp.zeros((SEGS, LANES), f32) for _ in range(2 * S5_JB)))
            for q in range(S5_JB):
                ln = _lanes(j0 + q)
                er, ei = ends[2 * q], ends[2 * q + 1]
                cr, ci = carry_ref[0, :, ln], carry_ref[1, :, ln]
                amr, ami = p_ref[0, 0, 0:1, ln], p_ref[1, 0, 0:1, ln]
                rows_r, rows_i = [None] * SEGS, [None] * SEGS
                for s in reversed(range(SEGS)):
                    rows_r[s], rows_i[s] = cr, ci
                    cr, ci = (er[s:s + 1, :] + amr * cr + ami * ci, ei[s:s + 1, :] + amr * ci - ami * cr)
                cin_ref[0, 0:SEGS, ln] = _stack_rows(rows_r)
                cin_ref[1, 0:SEGS, ln] = _stack_rows(rows_i)
                carry_ref[0, :, ln] = cr
                carry_ref[1, :, ln] = ci
        cin_ref[:, SEGS:, :] = cin_ref[:, 0:SEGS, :]
        acc_ref[...] = jnp.zeros_like(acc_ref)

        def fix_rows(rows, k2, prev_re, prev_im):
            pr = p_ref[0, pl.ds(2 * k2, 2)].reshape(PAIR, S5_NS)
            pi = p_ref[1, pl.ds(2 * k2, 2)].reshape(PAIR, S5_NS)
            cr, ci = cin_ref[0], cin_ref[1]
            lr = dst_re[rows, :] + pr * cr + pi * ci
            li = dst_im[rows, :] + pr * ci - pi * cr
            lam_ref[rows, 0:S5_NS] = lr.astype(bf16)
            lam_ref[rows, S5_NS:] = li.astype(bf16)
            acc_ref[0] += lr * prev_re + li * prev_im
            acc_ref[1] += li * prev_re - lr * prev_im

        last = slice(tc - SUBLANES, tc)
        wrap_re = _down_a_segment(sre_ref[last, :], jnp.where(first, 0.0, hre_ref[SUBLANES - 1:SUBLANES, :]))
        wrap_im = _down_a_segment(sim_ref[last, :], jnp.where(first, 0.0, him_ref[SUBLANES - 1:SUBLANES, :]))
        fix_rows(pl.ds(0, PAIR), 0, jnp.concatenate([wrap_re, sre_ref[0:SUBLANES, :]], axis=0),
                 jnp.concatenate([wrap_im, sim_ref[0:SUBLANES, :]], axis=0))

        def fix(k2, _):
            prev = pl.ds(pl.multiple_of(k2 * PAIR - SUBLANES, SUBLANES), PAIR)
            fix_rows(_step_rows(k2, PAIR), k2, sre_ref[prev, :], sim_ref[prev, :])
            return 0

        lax.fori_loop(1, m // 2, fix, 0)
        gab_ref[0] += jnp.sum(acc_ref[0], axis=0, keepdims=True)
        gab_ref[1] += jnp.sum(acc_ref[1], axis=0, keepdims=True)
        for c in range(CB_S5):
            x = gy[:, _lanes(c)] * d_ref[:, _lanes(c)]
            for j in range(SB_PER_CB * c, SB_PER_CB * (c + 1)):
                pair = _state_pair(lam_ref, j)
                x = x + _dot_nt(pair, bdb_ref[j])
                gbdb_ref[j] += _dot_tn(ub[:, _lanes(c)], pair)
            gu_ref[:, _lanes(c)] = x.astype(bf16)
        gd_ref[...] += jnp.sum(gy * u, axis=0, keepdims=True)

    rev = lambda i: (nch - 1 - i, 0)
    halo = lambda i: (jnp.maximum((nch - 1 - i) * hb - 1, 0), 0)
    blk = lambda wd: pl.BlockSpec((tc, wd), rev)
    return pl.pallas_call(
        body, name=name, grid=(nch,),
        in_specs=[blk(S5_WIDTH), blk(S5_WIDTH), blk(S5_NS), blk(S5_NS),
                  pl.BlockSpec((SUBLANES, S5_NS), halo), pl.BlockSpec((SUBLANES, S5_NS), halo), blk(2 * S5_NS),
                  _full(bdb.shape), _full(cdb.shape), _full((1, S5_WIDTH)),
                  _full((2, SEGS, S5_NS)), _full((2, m, SEGS, S5_NS))],
        out_specs=[blk(S5_WIDTH), _full((2, 1, S5_NS)), _full((1, S5_WIDTH)), _full(bdb.shape), _full(cdb.shape)],
        out_shape=[_S((T, S5_WIDTH), bf16), _S((2, 1, S5_NS)), _S((1, S5_WIDTH)), _S(bdb.shape), _S(cdb.shape)],
        scratch_shapes=[pltpu.VMEM((tc, S5_NS), f32)] * 4 + [
            pltpu.VMEM((tc, 2 * S5_NS), bf16), pltpu.VMEM((2, PAIR, S5_NS), f32), pltpu.VMEM((2, PAIR, S5_NS), f32),
            pltpu.VMEM((2, 1, S5_NS), f32)],
        compiler_params=_cp("arbitrary"),
    )(gys, usg, s_re, s_im, s_re, s_im, s_bf, bdb, cdb, dvec, abar_b, ptab_rev_b)


def s5_post_fwd(ys, usg, wglu, wbs, name):
    T = ys.shape[0]
    tm = _tile(T, TOKEN_TILE)

    def body(ys_ref, sg_ref, wglu_ref, wbs_ref, glu_ref, zs_ref):
        glu = _dot(_gelu(ys_ref[...]).astype(bf16), wglu_ref[...])
        sg = sg_ref[...]
        y2 = glu[:, :S5_WIDTH] * _sig(glu[:, S5_WIDTH:]) * (sg * _sig(sg))
        glu_ref[...] = glu
        zs_ref[...] = _dot(y2.astype(bf16), wbs_ref[...])

    return pl.pallas_call(
        body, name=name, grid=(T // tm,),
        in_specs=[_rows(tm, S5_WIDTH), _rows(tm, S5_WIDTH, 1), _full((S5_WIDTH, 2 * S5_WIDTH)), _full((S5_WIDTH, D_MODEL))],
        out_specs=[_rows(tm, 2 * S5_WIDTH), _rows(tm, D_MODEL)],
        out_shape=[_S((T, 2 * S5_WIDTH)), _S((T, D_MODEL))], compiler_params=_cp("parallel"),
    )(ys, usg, wglu, wbs)


def s5_post_bwd(gzs, glu, usg, ys, wbs, wglu, name):
    T = ys.shape[0]
    tm = _tile(T, TOKEN_TILE)

    def body(gzs_ref, glu_ref, sg_ref, ys_ref, wbs_ref, wglu_ref, y2_ref, gglu_ref, ge_ref, gys_ref, gsg_ref):
        glu = glu_ref[...]
        a, b = glu[:, :S5_WIDTH], glu[:, S5_WIDTH:]
        sg = sg_ref[...]
        ys = ys_ref[...]
        sb, ssg = _sig(b), _sig(sg)
        silu = sg * ssg
        y2_ref[...] = (a * sb * silu).astype(bf16)
        gy2 = _dot_nt(gzs_ref[...], wbs_ref[...])
        g_a = gy2 * sb * silu
        g_b = gy2 * a * sb * (1.0 - sb) * silu
        gsg_ref[...] = (gy2 * a * sb * ssg * (1.0 + sg * (1.0 - ssg))).astype(bf16)
        gglu = jnp.concatenate([g_a, g_b], axis=1).astype(bf16)
        gglu_ref[...] = gglu
        ge_ref[...] = _gelu(ys).astype(bf16)
        gys_ref[...] = _dot_nt(gglu, wglu_ref[...]) * _gelu_grad(ys)

    return pl.pallas_call(
        body, name=name, grid=(T // tm,),
        in_specs=[_rows(tm, D_MODEL), _rows(tm, 2 * S5_WIDTH), _rows(tm, S5_WIDTH, 1), _rows(tm, S5_WIDTH),
                  _full((S5_WIDTH, D_MODEL)), _full((S5_WIDTH, 2 * S5_WIDTH))],
        out_specs=[_rows(tm, S5_WIDTH), _rows(tm, 2 * S5_WIDTH), _rows(tm, S5_WIDTH), _rows(tm, S5_WIDTH), _rows(tm, S5_WIDTH)],
        out_shape=[_S((T, S5_WIDTH), bf16), _S((T, 2 * S5_WIDTH), bf16), _S((T, S5_WIDTH), bf16), _S((T, S5_WIDTH)),
                   _S((T, S5_WIDTH), bf16)],
        compiler_params=_cp("parallel"),
    )(gzs, glu, usg, ys, wbs, wglu)


NB_LRU = LRU_WIDTH // LANES
LRU_JB = 5
TAPS_BACK = CONV_WIDTH - 1
EDGE = TAPS_BACK * SUBLANES
HALO_ROWS = 4 * SUBLANES


def _down_a_segment(blk, entering_row):
    sub = lax.broadcasted_iota(jnp.int32, blk.shape, 0)
    return jnp.where(sub == 0, entering_row, pltpu.roll(blk, 1, 0))


def _up_a_segment(blk, entering_row):
    sub = lax.broadcasted_iota(jnp.int32, blk.shape, 0)
    return jnp.where(sub == SUBLANES - 1, entering_row, pltpu.roll(blk, SUBLANES - 1, 0))


def _stack_rows(rows):
    sub = lax.broadcasted_iota(jnp.int32, (SUBLANES,) + rows[0].shape[1:], 0)
    out = jnp.broadcast_to(rows[0], sub.shape)
    for s in range(1, SUBLANES):
        out = jnp.where(sub == s, rows[s], out)
    return out


def _fill_conv_window(xe, x_ref, xh_ref, is_first, tc):
    xe[EDGE:, :] = x_ref[...]
    for i in range(1, TAPS_BACK + 1):
        row = HALO_ROWS - SUBLANES * i + SUBLANES - 1
        entering = jnp.where(is_first, 0.0, xh_ref[row:row + 1, :])
        blk = x_ref[tc - SUBLANES * i:tc - SUBLANES * (i - 1), :]
        xe[EDGE - SUBLANES * i:EDGE - SUBLANES * (i - 1), :] = _down_a_segment(blk, entering)


def lru_fwd(lx, convw, convb, wa, wx, ba, bx, lam, name):
    T = lx.shape[0]
    tc = _tile(T, SCAN_CHUNK)
    m = tc // SEGS
    hb = tc // HALO_ROWS

    def body(x_ref, xh_ref, cw_ref, cb_ref, wa_ref, wx_ref, ba_ref, bx_ref, lam_ref,
             c_ref, r_ref, i_ref, h_ref, xe, src_a, src_b, dst_a, dst_h, cin_ref, carry_ref):
        i = pl.program_id(0)

        @pl.when(i == 0)
        def _():
            carry_ref[...] = jnp.zeros_like(carry_ref)

        _fill_conv_window(xe, x_ref, xh_ref, i == 0, tc)
        c = cb_ref[...] + cw_ref[0:1, :] * xe[0:tc, :]
        for k in range(1, CONV_WIDTH):
            c = c + cw_ref[k:k + 1, :] * xe[SUBLANES * k:SUBLANES * k + tc, :]
        c_ref[...] = c
        sp = _softplus_neg(lam_ref[...])
        for j in range(NB_LRU):
            ln = _lanes(j)
            cj = c[:, ln]
            cjb = cj.astype(bf16)
            r = _sig(_dot(cjb, wa_ref[j]) + ba_ref[:, ln])
            g = _sig(_dot(cjb, wx_ref[j]) + bx_ref[:, ln])
            r_ref[:, ln] = r
            i_ref[:, ln] = g
            log_a = -LRU_C * r * sp[:, ln]
            src_a[:, ln] = jnp.exp(log_a)
            src_b[:, ln] = jnp.sqrt(_one_minus_exp(2.0 * log_a)) * (g * cj)
        for j0 in range(0, NB_LRU, LRU_JB):
            def kstep(k, st):
                rows = _step_rows(k)
                out = []
                for q in range(LRU_JB):
                    ln = _lanes(j0 + q)
                    hh, ac = st[2 * q], st[2 * q + 1]
                    a = src_a[rows, ln]
                    hh = a * hh + src_b[rows, ln]
                    ac = a * ac
                    dst_h[rows, ln] = hh
                    dst_a[rows, ln] = ac
                    out += [hh, ac]
                return tuple(out)

            init = tuple(jnp.zeros((SEGS, LANES), f32) if q % 2 == 0 else jnp.ones((SEGS, LANES), f32)
                         for q in range(2 * LRU_JB))
            ends = lax.fori_loop(0, m, kstep, init)
            for q in range(LRU_JB):
                ln = _lanes(j0 + q)
                eh, ea = ends[2 * q], ends[2 * q + 1]
                cr = carry_ref[:, ln]
                rows_c = []
                for s in range(SEGS):
                    rows_c.append(cr)
                    cr = eh[s:s + 1, :] + ea[s:s + 1, :] * cr
                cin_ref[:, ln] = _stack_rows(rows_c)
                carry_ref[:, ln] = cr

        def fix(k, _):
            rows = _step_rows(k)
            h_ref[rows, :] = dst_h[rows, :] + dst_a[rows, :] * cin_ref[...]
            return 0

        lax.fori_loop(0, m, fix, 0)

    wide = lambda: _rows(tc, LRU_WIDTH)
    buf = lambda rows: pltpu.VMEM((rows, LRU_WIDTH), f32)
    return pl.pallas_call(
        body, name=name, grid=(T // tc,),
        in_specs=[wide(), pl.BlockSpec((HALO_ROWS, LRU_WIDTH), lambda i: (jnp.maximum(i * hb - 1, 0), 0)),
                  _full((CONV_WIDTH, LRU_WIDTH)), _full((1, LRU_WIDTH)),
                  _full((LRU_HEADS, LRU_HEAD_DIM, LRU_HEAD_DIM)), _full((LRU_HEADS, LRU_HEAD_DIM, LRU_HEAD_DIM)),
                  _full((1, LRU_WIDTH)), _full((1, LRU_WIDTH)), _full((1, LRU_WIDTH))],
        out_specs=[wide(), wide(), wide(), wide()],
        out_shape=[_S((T, LRU_WIDTH))] * 4,
        scratch_shapes=[buf(tc + EDGE), buf(tc), buf(tc), buf(tc), buf(tc), buf(SEGS), buf(1)],
        compiler_params=_cp("arbitrary"),
    )(lx, lx, convw, convb, wa, wx, ba, bx, lam)


def lru_bwd(gh, h, c, r, gi, lx, convw, wa, wx, lam, name):
    T = gh.shape[0]
    tc = _tile(T, SCAN_CHUNK)
    m = tc // SEGS
    nch = T // tc

    def body(gh_ref, h_ref, hh_ref, c_ref, r_ref, i_ref, x_ref, xh_ref, cw_ref, wa_ref, wx_ref, lam_ref,
             glx_ref, gwa_ref, gwx_ref, gba_ref, gbx_ref, glam_ref, gcb_ref, gcw_ref,
             src_a, src_m, dst_a, dst_m, mbuf, hbuf, xe, gce, cin_ref, gcc_ref, carry_ref):
        i = pl.program_id(0)

        @pl.when(i == 0)
        def _():
            carry_ref[...] = jnp.zeros_like(carry_ref)
            gcc_ref[...] = jnp.zeros_like(gcc_ref)
            for ref in (gwa_ref, gwx_ref, gba_ref, gbx_ref, glam_ref, gcb_ref, gcw_ref):
                ref[...] = jnp.zeros_like(ref)

        first = i == nch - 1
        last = slice(tc - SUBLANES, tc)
        hbuf[SUBLANES:, :] = h_ref[...]
        hbuf[0:SUBLANES, :] = _down_a_segment(h_ref[last, :], jnp.where(first, 0.0, hh_ref[SUBLANES - 1:SUBLANES, :]))
        _fill_conv_window(xe, x_ref, xh_ref, first, tc)
        lam_v = lam_ref[...]
        sp = _softplus_neg(lam_v)
        a_all = jnp.exp(-LRU_C * r_ref[...] * sp)
        src_a[...] = a_all
        src_m[...] = a_all * gh_ref[...]
        for j0 in range(0, NB_LRU, LRU_JB):
            def kstep(kk, st):
                rows = _step_rows(m - 1 - kk)
                out = []
                for q in range(LRU_JB):
                    ln = _lanes(j0 + q)
                    mu, ac = st[2 * q], st[2 * q + 1]
                    a = src_a[rows, ln]
                    mu = a * mu + src_m[rows, ln]
                    ac = a * ac
                    dst_m[rows, ln] = mu
                    dst_a[rows, ln] = ac
                    out += [mu, ac]
                return tuple(out)

            init = tuple(jnp.zeros((SEGS, LANES), f32) if q % 2 == 0 else jnp.ones((SEGS, LANES), f32)
                         for q in range(2 * LRU_JB))
            ends = lax.fori_loop(0, m, kstep, init)
            for q in range(LRU_JB):
                ln = _lanes(j0 + q)
                em, ea = ends[2 * q], ends[2 * q + 1]
                cr = carry_ref[:, ln]
                rows_c = [None] * SEGS
                for s in reversed(range(SEGS)):
                    rows_c[s] = cr
                    cr = em[s:s + 1, :] + ea[s:s + 1, :] * cr
                cin_ref[:, ln] = _stack_rows(rows_c)
                carry_ref[:, ln] = cr

        def fix(k, _):
            rows = _step_rows(k)
            mbuf[rows, :] = dst_m[rows, :] + dst_a[rows, :] * cin_ref[...]
            return 0

        lax.fori_loop(0, m, fix, 0)
        mbuf[tc:, :] = _up_a_segment(mbuf[0:SUBLANES, :], cin_ref[SUBLANES - 1:SUBLANES, :])
        sneg = _sig(-lam_v)
        for j in range(NB_LRU):
            ln = _lanes(j)
            lamt = gh_ref[:, ln] + mbuf[SUBLANES:, ln]
            rj, ij, cj = r_ref[:, ln], i_ref[:, ln], c_ref[:, ln]
            log_a = -LRU_C * rj * sp[:, ln]
            a = jnp.exp(log_a)
            mult = jnp.sqrt(_one_minus_exp(2.0 * log_a))
            g_a = lamt * hbuf[0:tc, ln]
            g_mult = lamt * ij * cj
            g_i = lamt * mult * cj
            g_c = lamt * mult * ij
            g_log_a = g_a * a - g_mult * a * a / mult
            glam_ref[:, ln] += jnp.sum(g_log_a * rj, axis=0, keepdims=True) * LRU_C * sneg[:, ln]
            g_ra = g_log_a * (-LRU_C) * sp[:, ln] * rj * (1.0 - rj)
            g_ia = g_i * ij * (1.0 - ij)
            gba_ref[:, ln] += jnp.sum(g_ra, axis=0, keepdims=True)
            gbx_ref[:, ln] += jnp.sum(g_ia, axis=0, keepdims=True)
            cjb, grb, gib = cj.astype(bf16), g_ra.astype(bf16), g_ia.astype(bf16)
            gwa_ref[j] += _dot_tn(cjb, grb)
            gwx_ref[j] += _dot_tn(cjb, gib)
            g_c = g_c + _dot_nt(grb, wa_ref[j]) + _dot_nt(gib, wx_ref[j])
            gce[0:tc, ln] = g_c
            gcb_ref[:, ln] += jnp.sum(g_c, axis=0, keepdims=True)
        for d in range(TAPS_BACK):
            blk = slice(SUBLANES * d, SUBLANES * (d + 1))
            gce[tc + SUBLANES * d:tc + SUBLANES * (d + 1), :] = _up_a_segment(gce[blk, :], gcc_ref[SUBLANES * d:SUBLANES * d + 1, :])
        gcc_ref[...] = gce[0:EDGE, :]
        gc = gce[0:tc, :]
        glx = cw_ref[CONV_WIDTH - 1:CONV_WIDTH, :] * gc
        gcw_ref[CONV_WIDTH - 1:CONV_WIDTH, :] += jnp.sum(gc * xe[EDGE:EDGE + tc, :], axis=0, keepdims=True)
        for k in range(CONV_WIDTH - 1):
            off = SUBLANES * (CONV_WIDTH - 1 - k)
            glx = glx + cw_ref[k:k + 1, :] * gce[off:off + tc, :]
            gcw_ref[k:k + 1, :] += jnp.sum(gc * xe[EDGE - off:EDGE - off + tc, :], axis=0, keepdims=True)
        glx_ref[...] = glx.astype(bf16)

    rev = lambda i: (nch - 1 - i, 0)
    halo = lambda rows: (lambda i: (jnp.maximum((nch - 1 - i) * (tc // rows) - 1, 0), 0))
    wide = lambda: pl.BlockSpec((tc, LRU_WIDTH), rev)
    vec = lambda: _full((1, LRU_WIDTH))
    hd = lambda: _full((LRU_HEADS, LRU_HEAD_DIM, LRU_HEAD_DIM))
    buf = lambda rows: pltpu.VMEM((rows, LRU_WIDTH), f32)
    return pl.pallas_call(
        body, name=name, grid=(nch,),
        in_specs=[wide(), wide(), pl.BlockSpec((SUBLANES, LRU_WIDTH), halo(SUBLANES)), wide(), wide(), wide(), wide(),
                  pl.BlockSpec((HALO_ROWS, LRU_WIDTH), halo(HALO_ROWS)), _full((CONV_WIDTH, LRU_WIDTH)), hd(), hd(), vec()],
        out_specs=[wide(), hd(), hd(), vec(), vec(), vec(), vec(), _full((CONV_WIDTH, LRU_WIDTH))],
        out_shape=[_S((T, LRU_WIDTH), bf16), _S((LRU_HEADS, LRU_HEAD_DIM, LRU_HEAD_DIM)),
                   _S((LRU_HEADS, LRU_HEAD_DIM, LRU_HEAD_DIM)), _S((1, LRU_WIDTH)), _S((1, LRU_WIDTH)),
                   _S((1, LRU_WIDTH)), _S((1, LRU_WIDTH)), _S((CONV_WIDTH, LRU_WIDTH))],
        scratch_shapes=[buf(tc), buf(tc), buf(tc), buf(tc), buf(tc + SUBLANES), buf(tc + SUBLANES), buf(tc + EDGE),
                        buf(tc + EDGE), buf(SEGS), buf(EDGE), buf(1)],
        compiler_params=_cp("arbitrary"),
    )(gh, h, h, c, r, gi, lx, lx, convw, wa, wx, lam)


def merge_fwd(h, lg, zs, gsl, x, p, wbl, wout, gpost, wple, wpg, name):
    T = x.shape[0]
    tm = _tile(T, TOKEN_TILE)

    def body(h_ref, lg_ref, zs_ref, gs_ref, gl_ref, x_ref, p_ref, wbl_ref, wout_ref, gp_ref, wple_ref, wpg_ref,
             zl_ref, mix_ref, q_ref, pe_ref, xo_ref):
        lg_v = lg_ref[...]
        yl = h_ref[...] * (lg_v * _sig(lg_v))
        zl = _dot(yl.astype(bf16), wbl_ref[...])
        merged = _sig(gs_ref[...]) * zs_ref[...] + _sig(gl_ref[...]) * zl
        mix = _dot(merged.astype(bf16), wout_ref[...])
        r2 = lax.rsqrt(jnp.mean(mix * mix, axis=-1, keepdims=True) + NORM_EPS)
        x1 = x_ref[...] + mix * r2 * gp_ref[...]
        q = _dot(x1.astype(bf16), wpg_ref[...])
        pe = _dot(p_ref[...].astype(bf16), wple_ref[...])
        zl_ref[...], mix_ref[...], q_ref[...], pe_ref[...] = zl, mix, q, pe
        xo_ref[...] = x1 + pe * _sig(q)

    dm = lambda: _rows(tm, D_MODEL)
    return pl.pallas_call(
        body, name=name, grid=(T // tm,),
        in_specs=[_rows(tm, LRU_WIDTH), _rows(tm, LRU_WIDTH), dm(), _rows(tm, D_MODEL, 0), _rows(tm, D_MODEL, 1), dm(),
                  _rows(tm, PLE_DIM), _full((LRU_WIDTH, D_MODEL)), _full((D_MODEL, D_MODEL)), _full((1, D_MODEL)),
                  _full((PLE_DIM, D_MODEL)), _full((D_MODEL, D_MODEL))],
        out_specs=[dm(), dm(), dm(), dm(), dm()],
        out_shape=[_S((T, D_MODEL))] * 5, compiler_params=_cp("parallel"),
    )(h, lg, zs, gsl, gsl, x, p, wbl, wout, gpost, wple, wpg)


def merge_bwd(gx2, q, pe, mix, x, zl, zs, gsl, h, lg, wpg, wout, wbl, gpost, name):
    T = x.shape[0]
    tm = _tile(T, TOKEN_TILE)

    def body(gx2_ref, q_ref, pe_ref, mix_ref, x_ref, zl_ref, zs_ref, gs_ref, gl_ref, h_ref, lg_ref,
             wpg_ref, wout_ref, wbl_ref, gp_ref,
             gres_ref, gpe_ref, gq_ref, x1_ref, gmix_ref, mrg_ref, gzl_ref, yl_ref, gzs_ref, ggsl_ref, gh_ref, glg_ref,
             ggp_ref):
        i = pl.program_id(0)
        gx2 = gx2_ref[...]
        sq = _sig(q_ref[...])
        pe = pe_ref[...]
        gpe_ref[...] = (gx2 * sq).astype(bf16)
        gq = (gx2 * pe * sq * (1.0 - sq)).astype(bf16)
        gq_ref[...] = gq
        mix = mix_ref[...]
        gp = gp_ref[...]
        r2 = lax.rsqrt(jnp.mean(mix * mix, axis=-1, keepdims=True) + NORM_EPS)
        nrm = mix * r2
        x1_ref[...] = (x_ref[...] + nrm * gp).astype(bf16)
        gx1 = gx2 + _dot_nt(gq, wpg_ref[...])
        gres_ref[...] = gx1
        part = jnp.sum(gx1 * nrm, axis=0, keepdims=True)

        @pl.when(i == 0)
        def _():
            ggp_ref[...] = part

        @pl.when(i > 0)
        def _():
            ggp_ref[...] += part

        gy = gx1 * gp
        gmix = (r2 * (gy - nrm * jnp.mean(gy * nrm, axis=-1, keepdims=True))).astype(bf16)
        gmix_ref[...] = gmix
        gmerged = _dot_nt(gmix, wout_ref[...])
        zs, zl = zs_ref[...], zl_ref[...]
        ss, sl = _sig(gs_ref[...]), _sig(gl_ref[...])
        mrg_ref[...] = (ss * zs + sl * zl).astype(bf16)
        gzs_ref[...] = (gmerged * ss).astype(bf16)
        gzl = (gmerged * sl).astype(bf16)
        gzl_ref[...] = gzl
        ggsl_ref[:, :D_MODEL] = (gmerged * zs * ss * (1.0 - ss)).astype(bf16)
        ggsl_ref[:, D_MODEL:] = (gmerged * zl * sl * (1.0 - sl)).astype(bf16)
        lg_v, hv = lg_ref[...], h_ref[...]
        slg = _sig(lg_v)
        silu = lg_v * slg
        yl_ref[...] = (hv * silu).astype(bf16)
        gyl = _dot_nt(gzl, wbl_ref[...])
        gh_ref[...] = gyl * silu
        glg_ref[...] = (gyl * hv * slg * (1.0 + lg_v * (1.0 - slg))).astype(bf16)

    dm = lambda: _rows(tm, D_MODEL)
    lw = lambda: _rows(tm, LRU_WIDTH)
    return pl.pallas_call(
        body, name=name, grid=(T // tm,),
        in_specs=[dm(), dm(), dm(), dm(), dm(), dm(), dm(), _rows(tm, D_MODEL, 0), _rows(tm, D_MODEL, 1), lw(), lw(),
                  _full((D_MODEL, D_MODEL)), _full((D_MODEL, D_MODEL)), _full((LRU_WIDTH, D_MODEL)), _full((1, D_MODEL))],
        out_specs=[dm(), dm(), dm(), dm(), dm(), dm(), dm(), lw(), dm(), _rows(tm, 2 * D_MODEL), lw(), lw(),
                   _full((1, D_MODEL))],
        out_shape=[_S((T, D_MODEL)), _S((T, D_MODEL), bf16), _S((T, D_MODEL), bf16), _S((T, D_MODEL), bf16),
                   _S((T, D_MODEL), bf16), _S((T, D_MODEL), bf16), _S((T, D_MODEL), bf16), _S((T, LRU_WIDTH), bf16),
                   _S((T, D_MODEL), bf16), _S((T, 2 * D_MODEL), bf16), _S((T, LRU_WIDTH)), _S((T, LRU_WIDTH), bf16),
                   _S((1, D_MODEL))],
        compiler_params=_cp("arbitrary"),
    )(gx2, q, pe, mix, x, zl, zs, gsl, gsl, h, lg, wpg, wout, wbl, gpost)


def loss_head(y, target, name):
    T = y.shape[0]
    tm = _tile(T, TOKEN_TILE)

    def body(y_ref, t_ref, l_ref, g_ref):
        i = pl.program_id(0)
        e = y_ref[...] - t_ref[...]
        g_ref[...] = e * (1.0 / D_MODEL)
        part = 0.5 * jnp.sum(jnp.sum(e * e, axis=-1, keepdims=True) * (1.0 / D_MODEL), axis=0, keepdims=True)

        @pl.when(i == 0)
        def _():
            l_ref[...] = part

        @pl.when(i > 0)
        def _():
            l_ref[...] += part

    return pl.pallas_call(
        body, name=name, grid=(T // tm,),
        in_specs=[_rows(tm, D_MODEL), _rows(tm, D_MODEL)], out_specs=[_full((1, 1)), _rows(tm, D_MODEL)],
        out_shape=[_S((1, 1)), _S((T, D_MODEL))],
        compiler_params=_cp("arbitrary"),
    )(y, target)


def _s5_operands(w, m, tag):
    b_re_t = jnp.transpose(w['s5_b_re'], (2, 0, 1))
    b_im_t = jnp.transpose(w['s5_b_im'], (2, 0, 1))
    ldt = w['s5_log_dt'][:, None]
    ab, pw, bb = s5_prep(w['s5_a_re'], w['s5_a_im'], ldt, b_re_t, b_im_t, m, "s5_prep" + tag)
    over_sublanes = lambda t: jnp.broadcast_to(t[..., None, :], t.shape[:-1] + (SEGS, S5_NS))
    ptab = pw.reshape(2, m, S5_NS)
    return dict(abar_b=over_sublanes(ab.reshape(2, S5_NS)), ptab_b=over_sublanes(ptab),
                ptab_rev_b=over_sublanes(ptab[:, ::-1, :]), bdb=_pack_bdb(bb).astype(bf16),
                cdb=_pack_cdb(w['s5_c_re'], w['s5_c_im']).astype(bf16), dvec=w['s5_d'][None, :],
                prep_in=(w['s5_a_re'], w['s5_a_im'], ldt, b_re_t, b_im_t))


def layer_fwd(x, p, w, tag):
    T = x.shape[0]
    m = _tile(T, SCAN_CHUNK) // SEGS
    s5 = _s5_operands(w, m, tag)
    h_bf = rms_fwd(x, w['g_pre'][None, :], "rms_fwd" + tag)
    win = w['w_in']
    usg = mm_nn(h_bf, win[:, :2 * S5_WIDTH], "proj_s5" + tag)
    lx = mm_nn(h_bf, win[:, 2 * S5_WIDTH:2 * S5_WIDTH + LRU_WIDTH], "proj_lx" + tag, tn=640)
    lg = mm_nn(h_bf, win[:, 2 * S5_WIDTH + LRU_WIDTH:2 * S5_WIDTH + 2 * LRU_WIDTH], "proj_lg" + tag, tn=640)
    gsl = mm_nn(h_bf, win[:, 2 * S5_WIDTH + 2 * LRU_WIDTH:], "proj_gate" + tag)
    ys, s_re, s_im, s_bf = s5_fwd(usg, s5['bdb'], s5['cdb'], s5['dvec'], s5['abar_b'], s5['ptab_b'], "s5_fwd" + tag)
    glu, zs = s5_post_fwd(ys, usg, w['w_glu'], w['w_bs'], "s5_post_fwd" + tag)
    wa, wx = w['lru_w_a'].astype(bf16), w['lru_w_x'].astype(bf16)
    c, r, gi, hs = lru_fwd(lx, w['conv_w'], w['conv_b'][None, :], wa, wx, w['lru_b_a'][None, :], w['lru_b_x'][None, :],
                           w['lru_lambda'][None, :], "lru_fwd" + tag)
    zl, mix, q, pe, x_out = merge_fwd(hs, lg, zs, gsl, x, p, w['w_bl'], w['w_out'], w['g_post'][None, :],
                                      w['w_ple'], w['w_ple_gate'], "merge_fwd" + tag)
    saved = dict(x=x, p=p, h_bf=h_bf, usg=usg, lx=lx, lg=lg, gsl=gsl, ys=ys, s_re=s_re, s_im=s_im, s_bf=s_bf, glu=glu,
                 zs=zs, c=c, r=r, gi=gi, hs=hs, zl=zl, mix=mix, q=q, pe=pe, s5=s5, wa=wa, wx=wx)
    return x_out, saved


def layer_bwd(gx_out, w, sv, tag):
    s5 = sv['s5']
    (gres, gpe, gq, x1_bf, gmix, merged, gzl, yl, gzs, ggsl, g_h, g_lg, g_gpost) = merge_bwd(
        gx_out, sv['q'], sv['pe'], sv['mix'], sv['x'], sv['zl'], sv['zs'], sv['gsl'], sv['hs'], sv['lg'],
        w['w_ple_gate'], w['w_out'], w['w_bl'], w['g_post'][None, :], "merge_bwd" + tag)
    g = {}
    g['w_ple'] = mm_tn(sv['p'], gpe, "gw_ple" + tag, tk=256)
    g['w_ple_gate'] = mm_tn(x1_bf, gq, "gw_ple_gate" + tag)
    g['w_out'] = mm_tn(merged, gmix, "gw_out" + tag)
    g['w_bl'] = mm_tn(yl, gzl, "gw_bl" + tag, tk=640)
    g['g_post'] = g_gpost[0]
    (g_lx, g_wa, g_wx, g_ba, g_bx, g_lam, g_cb, g_cw) = lru_bwd(
        g_h, sv['hs'], sv['c'], sv['r'], sv['gi'], sv['lx'], w['conv_w'], sv['wa'], sv['wx'],
        w['lru_lambda'][None, :], "lru_bwd" + tag)
    g['lru_w_a'], g['lru_w_x'] = g_wa, g_wx
    g['lru_b_a'], g['lru_b_x'], g['lru_lambda'], g['conv_b'], g['conv_w'] = g_ba[0], g_bx[0], g_lam[0], g_cb[0], g_cw
    y2, gglu, ge, g_ys, g_sg = s5_post_bwd(gzs, sv['glu'], sv['usg'], sv['ys'], w['w_bs'], w['w_glu'], "s5_post_bwd" + tag)
    g['w_bs'] = mm_tn(y2, gzs, "gw_bs" + tag)
    g['w_glu'] = mm_tn(ge, gglu, "gw_glu" + tag)
    g_u, g_ab, g_d, g_bdb, g_cdb = s5_bwd(g_ys, sv['usg'], sv['s_re'], sv['s_im'], sv['s_bf'], s5['bdb'], s5['cdb'],
                                          s5['dvec'], s5['abar_b'], s5['ptab_rev_b'], "s5_bwd" + tag)
    g['s5_d'] = g_d[0]
    g['s5_c_re'], g['s5_c_im'] = _unpack_cdb(g_cdb)
    g_are, g_aim, g_ldt, g_bre_t, g_bim_t = s5_prep_bwd(*s5['prep_in'], g_ab.reshape(2, S5_GROUPS, S5_STATE),
                                                       _unpack_bdb(g_bdb), "s5_prep_bwd" + tag)
    g['s5_a_re'], g['s5_a_im'], g['s5_log_dt'] = g_are, g_aim, g_ldt
    g['s5_b_re'] = jnp.transpose(g_bre_t, (1, 2, 0))
    g['s5_b_im'] = jnp.transpose(g_bim_t, (1, 2, 0))
    gproj = jnp.concatenate([g_u, g_sg, g_lx, g_lg, ggsl], axis=1)
    g['w_in'] = mm_tn(sv['h_bf'], gproj, "gw_in" + tag, tn=1408)
    gh = mm_nt(gproj, w['w_in'], "g_h" + tag)
    gx, g_gpre = rms_bwd(gh, sv['x'], w['g_pre'][None, :], gres, "rms_bwd" + tag)
    g['g_pre'] = g_gpre[0]
    return gx, g


def _as_2d(a):
    return a.reshape((-1, a.shape[-1])) if a.ndim > 1 else a.reshape((1, -1))


def _adamw_update(w, gv, m, v):
    nm = ADAM_B1 * m + (1.0 - ADAM_B1) * gv
    nv = ADAM_B2 * v + (1.0 - ADAM_B2) * (gv * gv)
    bc1 = 1.0 - ADAM_B1 ** ADAM_STEP
    bc2 = 1.0 - ADAM_B2 ** ADAM_STEP
    return -ADAM_LR * ((nm / bc1) / (jnp.sqrt(nv / bc2) + ADAM_EPS) + ADAM_WD * w), nm, nv


def adamw(w, g, m, v, name):
    shape = w.shape
    w2, g2, m2, v2 = _as_2d(w), _as_2d(g), _as_2d(m), _as_2d(v)
    R, C = w2.shape
    tr = _row_tile(R, C)

    def body(w_ref, g_ref, m_ref, v_ref, d_ref, nm_ref, nv_ref):
        d_ref[...], nm_ref[...], nv_ref[...] = _adamw_update(w_ref[...], g_ref[...], m_ref[...], v_ref[...])

    spec = lambda: pl.BlockSpec((tr, C), lambda i: (i, 0))
    d, nm, nv = pl.pallas_call(
        body, name=name, grid=(R // tr,), in_specs=[spec() for _ in range(4)], out_specs=[spec() for _ in range(3)],
        out_shape=[_S((R, C))] * 3, compiler_params=_cp("parallel"),
    )(w2, g2, m2, v2)
    return d.reshape(shape), nm.reshape(shape), nv.reshape(shape)


def adamw_reduce(w, a, theirs, m, v, chip, name):
    shape = w.shape
    w2, m2, v2 = _as_2d(w), _as_2d(m), _as_2d(v)
    R, C = w2.shape
    a3, t3 = a.reshape(4, R, C), theirs.reshape(3, R, C)
    tr = _row_tile(R, C)

    def body(chip_ref, w_ref, a_ref, t_ref, m_ref, v_ref, g_ref, d_ref, nm_ref, nv_ref):
        gv = ((a_ref[0] + t_ref[0]) + t_ref[1]) + t_ref[2]
        g_ref[...] = gv
        d_ref[...], nm_ref[...], nv_ref[...] = _adamw_update(w_ref[...], gv, m_ref[...], v_ref[...])

    spec = lambda: pl.BlockSpec((tr, C), lambda i, c: (i, 0))
    grid_spec = pltpu.PrefetchScalarGridSpec(
        num_scalar_prefetch=1, grid=(R // tr,),
        in_specs=[spec(), pl.BlockSpec((1, tr, C), lambda i, c: (c[0], i, 0)), pl.BlockSpec((3, tr, C), lambda i, c: (0, i, 0)),
                  spec(), spec()],
        out_specs=[spec() for _ in range(4)])
    g, d, nm, nv = pl.pallas_call(
        body, name=name, grid_spec=grid_spec, out_shape=[_S((R, C))] * 4, compiler_params=_cp("parallel"),
    )(chip, w2, a3, t3, m2, v2)
    return g.reshape(shape), d.reshape(shape), nm.reshape(shape), nv.reshape(shape)


MESH = pl.DeviceIdType.MESH
ANY = pl.BlockSpec(memory_space=pl.ANY)


def _place():
    return lax.axis_index("x"), lax.axis_index("y"), lax.axis_index("c")


def _other_chips(mx, my):
    return [(1 - mx, my), (mx, 1 - my), (1 - mx, 1 - my)]


def all_gather(bufs, name):
    nb = len(bufs)

    def body(*refs):
        out_refs, send_sems, recv_sems = refs[nb:2 * nb], refs[2 * nb], refs[2 * nb + 1]
        mx, my, mc = _place()
        me, sibling = (mx, my, mc), (mx, my, 1 - mc)
        chips = _other_chips(mx, my)

        def copy(b, k, block, to):
            px, py, pc = block
            rows = out_refs[b].at[4 * px + 2 * py + pc]
            return pltpu.make_async_remote_copy(
                src_ref=rows, dst_ref=rows, send_sem=send_sems.at[b, k], recv_sem=recv_sems.at[b, k],
                device_id=to, device_id_type=MESH)

        first = []
        for b in range(nb):
            first.append(copy(b, 0, me, sibling))
            first += [copy(b, 1 + j, me, (*chip, mc)) for j, chip in enumerate(chips)]
        for cp in first:
            cp.start()
        passed = []
        for j, chip in enumerate(chips):
            for b in range(nb):
                copy(b, 1 + j, (*chip, mc), me).wait_recv()
                passed.append(copy(b, 4 + j, (*chip, mc), sibling))
                passed[-1].start()
        for b in range(nb):
            copy(b, 0, sibling, me).wait_recv()
            for j, chip in enumerate(chips):
                copy(b, 4 + j, (*chip, 1 - mc), me).wait_recv()
        for cp in first + passed:
            cp.wait_send()

    outs = pl.pallas_call(
        body, name=name, out_shape=[_S(b.shape, b.dtype) for b in bufs], in_specs=[ANY] * nb, out_specs=[ANY] * nb,
        input_output_aliases={i: i for i in range(nb)},
        scratch_shapes=[pltpu.SemaphoreType.DMA((nb, 7)), pltpu.SemaphoreType.DMA((nb, 7))],
    )(*bufs)
    return list(outs)


def own_block_in_place(shard, me):
    buf = lax.empty((N_DEV,) + shard.shape, shard.dtype)
    return lax.dynamic_update_slice(buf, shard[None], (me,) + (0,) * shard.ndim)


def exchange_sibling(gs, name):
    nb = len(gs)

    def body(*refs):
        g_refs, recv_refs, send_sems, recv_sems = refs[:nb], refs[nb:2 * nb], refs[2 * nb], refs[2 * nb + 1]
        mx, my, mc = _place()
        copies = [pltpu.make_async_remote_copy(
            src_ref=g_refs[b].at[2 * k + 1 - mc], dst_ref=recv_refs[b].at[k], send_sem=send_sems.at[b, k],
            recv_sem=recv_sems.at[b, k], device_id=(mx, my, 1 - mc), device_id_type=MESH)
            for b in range(nb) for k in range(4)]
        for cp in copies:
            cp.start()
        for cp in copies:
            cp.wait()

    outs = pl.pallas_call(
        body, name=name, out_shape=[_S((4,) + g.shape[1:], g.dtype) for g in gs], in_specs=[ANY] * nb,
        out_specs=[ANY] * nb,
        scratch_shapes=[pltpu.SemaphoreType.DMA((nb, 4)), pltpu.SemaphoreType.DMA((nb, 4))],
    )(*gs)
    return list(outs)


def exchange_chips(parts, name):
    nb = len(parts)

    def body(*refs):
        a_refs, recv_refs, send_sems, recv_sems = refs[:nb], refs[nb:2 * nb], refs[2 * nb], refs[2 * nb + 1]
        mx, my, mc = _place()
        copies = [pltpu.make_async_remote_copy(
            src_ref=a_refs[b].at[2 * px + py], dst_ref=recv_refs[b].at[j], send_sem=send_sems.at[b, j],
            recv_sem=recv_sems.at[b, j], device_id=(px, py, mc), device_id_type=MESH)
            for b in range(nb) for j, (px, py) in enumerate(_other_chips(mx, my))]
        for cp in copies:
            cp.start()
        for cp in copies:
            cp.wait()

    outs = pl.pallas_call(
        body, name=name, out_shape=[_S((3,) + a.shape[1:], a.dtype) for a in parts], in_specs=[ANY] * nb,
        out_specs=[ANY] * nb,
        scratch_shapes=[pltpu.SemaphoreType.DMA((nb, 3)), pltpu.SemaphoreType.DMA((nb, 3))],
    )(*parts)
    return list(outs)


def add_sibling(g, theirs, core, name):
    shp = theirs.shape
    C = shp[-1]
    R = math.prod(shp[1:-1])
    tr = _row_tile(R, C)

    def body(core_ref, g_ref, t_ref, o_ref):
        o_ref[...] = g_ref[...] + t_ref[...]

    grid_spec = pltpu.PrefetchScalarGridSpec(
        num_scalar_prefetch=1, grid=(4, R // tr),
        in_specs=[pl.BlockSpec((1, tr, C), lambda k, i, c: (2 * k + c[0], i, 0)),
                  pl.BlockSpec((1, tr, C), lambda k, i, c: (k, i, 0))],
        out_specs=pl.BlockSpec((1, tr, C), lambda k, i, c: (k, i, 0)))
    out = pl.pallas_call(
        body, name=name, grid_spec=grid_spec, out_shape=_S((4, R, C), g.dtype),
        compiler_params=_cp("parallel", "parallel"),
    )(core, g.reshape(N_DEV, R, C), theirs.reshape(4, R, C))
    return out.reshape(shp)


def add_chips(a, theirs, chip, name):
    _, R, C = a.shape
    tr = _row_tile(R, C)

    def body(chip_ref, a_ref, t_ref, out_ref):
        out_ref[...] = ((a_ref[0] + t_ref[0]) + t_ref[1]) + t_ref[2]

    grid_spec = pltpu.PrefetchScalarGridSpec(
        num_scalar_prefetch=1, grid=(R // tr,),
        in_specs=[pl.BlockSpec((1, tr, C), lambda i, c: (c[0], i, 0)), pl.BlockSpec((3, tr, C), lambda i, c: (0, i, 0))],
        out_specs=pl.BlockSpec((tr, C), lambda i, c: (i, 0)))
    return pl.pallas_call(
        body, name=name, grid_spec=grid_spec, out_shape=_S((R, C), a.dtype), compiler_params=_cp("parallel"),
    )(chip, a, theirs)


def _round_up(n, q):
    return (n + q - 1) // q * q


def _full_to_shards(full, axis):
    shp = full.shape
    s = shp[axis] // N_DEV
    cut = full.reshape(shp[:axis] + (N_DEV, s) + shp[axis + 1:])
    return jnp.moveaxis(cut, axis, 0)


def _shards_to_full(parts, axis):
    shp = list(parts.shape[1:])
    shp[axis] *= N_DEV
    return jnp.moveaxis(parts, 0, axis).reshape(tuple(shp))


def kernel(x, p, g_pre, w_in, s5_a_re, s5_a_im, s5_log_dt, s5_b_re, s5_b_im, s5_c_re, s5_c_im, s5_d, w_glu, w_bs, conv_w, conv_b, lru_w_a, lru_b_a, lru_w_x, lru_b_x, lru_lambda, w_bl, w_out, g_post, w_ple, w_ple_gate, loss_target, m_g_pre, m_w_in, m_s5_a_re, m_s5_a_im, m_s5_log_dt, m_s5_b_re, m_s5_b_im, m_s5_c_re, m_s5_c_im, m_s5_d, m_w_glu, m_w_bs, m_conv_w, m_conv_b, m_lru_w_a, m_lru_b_a, m_lru_w_x, m_lru_b_x, m_lru_lambda, m_w_bl, m_w_out, m_g_post, m_w_ple, m_w_ple_gate, v_g_pre, v_w_in, v_s5_a_re, v_s5_a_im, v_s5_log_dt, v_s5_b_re, v_s5_b_im, v_s5_c_re, v_s5_c_im, v_s5_d, v_w_glu, v_w_bs, v_conv_w, v_conv_b, v_lru_w_a, v_lru_b_a, v_lru_w_x, v_lru_b_x, v_lru_lambda, v_w_bl, v_w_out, v_g_post, v_w_ple, v_w_ple_gate):
    given = dict(locals())
    W = {n: given[n] for n in WEIGHTS}
    M = {n: given["m_" + n] for n in WEIGHTS}
    V = {n: given["v_" + n] for n in WEIGHTS}
    xs, target = to_scan_order(x[0]), to_scan_order(loss_target[0])
    ps = [to_scan_order(p[i, 0]) for i in range(DEPTH)]

    mx, my, mc = _place()
    me = 4 * mx + 2 * my + mc
    core = jnp.reshape(mc, (1,)).astype(jnp.int32)
    chip = jnp.reshape(2 * mx + my, (1,)).astype(jnp.int32)

    names = list(SHARDED)
    shards = [W[n].astype(bf16) if n in GATHER_BF16 else W[n] for n in names]
    gathered = all_gather([own_block_in_place(s, me) for s in shards], "comm_gather_weights")
    full = {n: _shards_to_full(g, SHARDED[n]) for n, g in zip(names, gathered)}

    def layer_weights(i):
        return {n: (full[n][i] if n in SHARDED else W[n][i]) for n in WEIGHTS}

    act, saved = xs, []
    for i in range(DEPTH):
        act, sv = layer_fwd(act, ps[i], layer_weights(i), "_l%d" % i)
        saved.append(sv)
    loss_part, gact = loss_head(act, target, "loss_head")
    grads = [None] * DEPTH
    for i in reversed(range(DEPTH)):
        gact, grads[i] = layer_bwd(gact, layer_weights(i), saved[i], "_l%d" % i)
    loss = lax.psum(loss_part[0, 0], ("x", "y", "c"))
    gfull = {n: jnp.stack([grads[i][n].reshape(full[n].shape[1:] if n in SHARDED else W[n].shape[1:])
                           for i in range(DEPTH)]) for n in WEIGHTS}

    rep = jnp.concatenate([gfull[n].reshape(-1) for n in REPLICATED])
    n_rep = _round_up(rep.shape[0], N_DEV * SUBLANES * LANES)
    rep_blocks = jnp.pad(rep, (0, n_rep - rep.shape[0])).reshape(N_DEV, -1, LANES)
    blocks = [_full_to_shards(gfull[n], SHARDED[n]) for n in names] + [rep_blocks]
    theirs = exchange_sibling(blocks, "comm_reduce_sibling")
    parts = [add_sibling(b, t, core, "reduce_add_sibling_%d" % k) for k, (b, t) in enumerate(zip(blocks, theirs))]
    others = exchange_chips(parts, "comm_reduce_chips")

    red, deltas, new_m, new_v = {}, {}, {}, {}
    for n, a, t in zip(names, parts, others):
        red[n], deltas[n], new_m[n], new_v[n] = adamw_reduce(W[n], a, t, M[n], V[n], chip, "adamw_" + n)
    piece = add_chips(parts[-1], others[-1], chip, "reduce_add_chips")
    rep_all = all_gather([own_block_in_place(piece, me)], "comm_gather_replicated")[0].reshape(-1)
    off = 0
    for n in REPLICATED:
        k = math.prod(W[n].shape)
        red[n] = rep_all[off:off + k].reshape(W[n].shape)
        off += k
        deltas[n], new_m[n], new_v[n] = adamw(W[n], red[n], M[n], V[n], "adamw_" + n)
    return (loss, from_scan_order(gact)[None], *[red[n] for n in WEIGHTS], *[deltas[n] for n in WEIGHTS],
            *[new_m[n] for n in WEIGHTS], *[new_v[n] for n in WEIGHTS])
```

```python
import math

import jax
import jax.numpy as jnp
from jax import lax
from jax.experimental import pallas as pl
from jax.experimental.pallas import tpu as pltpu

f32 = jnp.float32
bf16 = jnp.bfloat16

D_MODEL = 1024
DEPTH = 2
PLE_DIM = 256
NORM_EPS = 1e-6
S5_WIDTH = 512
S5_GROUP = 16
S5_GROUPS = 32
S5_STATE = 64
S5_NS = S5_GROUPS * S5_STATE
LRU_WIDTH = 1280
LRU_HEADS = 10
LRU_HEAD_DIM = 128
LRU_C = 8.0
CONV_WIDTH = 4
N_DEV = 8

ADAM_LR = 0.001
ADAM_B1 = 0.9
ADAM_B2 = 0.999
ADAM_EPS = 1e-08
ADAM_WD = 0.01
ADAM_STEP = 10

LANES = 128
SUBLANES = 8
SEGS = SUBLANES
SCAN_CHUNK = 256
TOKEN_TILE = 256
MM_TILE_M = 1024
MM_TILE_N = 1408
MM_TILE_K_ROWS = 1280
PAIR = 2 * SUBLANES
VMEM_LIMIT_BYTES = 56 * 1024 * 1024
ELEMENTWISE_BLOCK_BYTES = 1024 * 1024

WEIGHTS = ['g_pre', 'w_in', 's5_a_re', 's5_a_im', 's5_log_dt', 's5_b_re', 's5_b_im', 's5_c_re', 's5_c_im',
           's5_d', 'w_glu', 'w_bs', 'conv_w', 'conv_b', 'lru_w_a', 'lru_b_a', 'lru_w_x', 'lru_b_x',
           'lru_lambda', 'w_bl', 'w_out', 'g_post', 'w_ple', 'w_ple_gate']
SHARDED = {'w_in': 2, 'w_glu': 2, 'w_bs': 2, 'conv_w': 2, 'w_bl': 1, 'w_out': 1, 'w_ple': 2, 'w_ple_gate': 1}
GATHER_BF16 = ['w_in', 'w_glu', 'w_bs', 'w_bl', 'w_out', 'w_ple', 'w_ple_gate']
REPLICATED = [n for n in WEIGHTS if n not in SHARDED]


def _sig(x):
    return 1.0 / (1.0 + jnp.exp(-x))


def _gelu_parts(x):
    k = math.sqrt(2.0 / math.pi)
    t = jnp.tanh(k * (x + 0.044715 * x * x * x))
    return t, k


def _gelu(x):
    t, _ = _gelu_parts(x)
    return 0.5 * x * (1.0 + t)


def _gelu_grad(x):
    t, k = _gelu_parts(x)
    return 0.5 * (1.0 + t) + 0.5 * x * (1.0 - t * t) * k * (1.0 + 3.0 * 0.044715 * x * x)


def _one_minus_exp(z):
    series = -z * (1.0 + z * (0.5 + z * (1.0 / 6.0 + z * (1.0 / 24.0 + z * (1.0 / 120.0)))))
    return jnp.where(z > -0.05, series, 1.0 - jnp.exp(z))


def _softplus_neg(lam):
    return jnp.maximum(-lam, 0.0) + jnp.log(1.0 + jnp.exp(-jnp.abs(lam)))


def _dot(a, b):
    return jnp.dot(a, b, preferred_element_type=f32)


def _dot_nt(a, b):
    return lax.dot_general(a, b, (((1,), (1,)), ((), ())), preferred_element_type=f32)


def _dot_tn(a, b):
    return lax.dot_general(a, b, (((0,), (0,)), ((), ())), preferred_element_type=f32)


def _S(shape, dtype=f32):
    return jax.ShapeDtypeStruct(shape, dtype)


def _full(shape):
    nd = len(shape)
    return pl.BlockSpec(shape, lambda *_: (0,) * nd)


def _rows(tile, width, col=0):
    return pl.BlockSpec((tile, width), lambda i: (i, col))


def _cp(*semantics):
    return pltpu.CompilerParams(dimension_semantics=semantics or None, vmem_limit_bytes=VMEM_LIMIT_BYTES)


def _tile(n, want):
    t = min(n, want)
    assert n % t == 0, (n, want)
    return t


def _row_tile(R, C=LANES):
    cap = max(SUBLANES, min(R, ELEMENTWISE_BLOCK_BYTES // (4 * C)))
    for t in range(cap - cap % SUBLANES, 0, -SUBLANES):
        if R % t == 0:
            return t
    return R


def _lanes(j):
    return slice(LANES * j, LANES * (j + 1))


def _step_rows(k, n=SUBLANES):
    return pl.ds(pl.multiple_of(k * n, n), n)


def to_scan_order(a):
    T, C = a.shape
    tc = _tile(T, SCAN_CHUNK)
    return a.reshape(T // tc, SEGS, tc // SEGS, C).transpose(0, 2, 1, 3).reshape(T, C)


def from_scan_order(a):
    T, C = a.shape
    tc = _tile(T, SCAN_CHUNK)
    return a.reshape(T // tc, tc // SEGS, SEGS, C).transpose(0, 2, 1, 3).reshape(T, C)


def _col_tile(n, cap):
    if n <= cap:
        return n
    for t in range(cap - cap % LANES, 0, -LANES):
        if n % t == 0:
            return t
    return n


def _resident(shape):
    nd = len(shape)
    return pl.BlockSpec(shape, lambda *_: (0,) * nd, pipeline_mode=pl.Buffered(1))


def mm_nn(a, b, name, out_dtype=f32):
    M, K = a.shape
    N = b.shape[1]
    tm, tn = _tile(M, MM_TILE_M), _col_tile(N, MM_TILE_N)

    def body(a_ref, b_ref, o_ref):
        o_ref[...] = _dot(a_ref[...].astype(bf16), b_ref[...].astype(bf16)).astype(out_dtype)

    return pl.pallas_call(
        body, name=name, grid=(M // tm, N // tn),
        in_specs=[pl.BlockSpec((tm, K), lambda i, j: (i, 0)), pl.BlockSpec((K, tn), lambda i, j: (0, j))],
        out_specs=pl.BlockSpec((tm, tn), lambda i, j: (i, j)),
        out_shape=_S((M, N), out_dtype), compiler_params=_cp("parallel", "parallel"),
    )(a, b)


def mm_nt(a, b, name, out_dtype=f32):
    M, K = a.shape
    N = b.shape[0]
    tm = _tile(M, MM_TILE_M // 2)

    def body(a_ref, b_ref, o_ref):
        o_ref[...] = _dot_nt(a_ref[...].astype(bf16), b_ref[...].astype(bf16)).astype(out_dtype)

    return pl.pallas_call(
        body, name=name, grid=(M // tm,),
        in_specs=[pl.BlockSpec((tm, K), lambda i: (i, 0)), _resident((N, K))],
        out_specs=pl.BlockSpec((tm, N), lambda i: (i, 0)),
        out_shape=_S((M, N), out_dtype), compiler_params=_cp("parallel"),
    )(a, b)


def mm_tn(a, b, name):
    M, K = a.shape
    N = b.shape[1]
    tm, tk, tn = _tile(M, MM_TILE_M), _col_tile(K, MM_TILE_K_ROWS), _col_tile(N, MM_TILE_N)

    def body(a_ref, b_ref, o_ref):
        m = pl.program_id(2)
        part = _dot_tn(a_ref[...].astype(bf16), b_ref[...].astype(bf16))

        @pl.when(m == 0)
        def _():
            o_ref[...] = part

        @pl.when(m > 0)
        def _():
            o_ref[...] += part

    return pl.pallas_call(
        body, name=name, grid=(K // tk, N // tn, M // tm),
        in_specs=[pl.BlockSpec((tm, tk), lambda i, j, m: (m, i)), pl.BlockSpec((tm, tn), lambda i, j, m: (m, j))],
        out_specs=pl.BlockSpec((tk, tn), lambda i, j, m: (i, j)),
        out_shape=_S((K, N), f32),
        compiler_params=_cp("parallel", "parallel", "arbitrary"),
    )(a, b)


def rms_fwd(x, g, name):
    T = x.shape[0]
    tm = _tile(T, TOKEN_TILE)

    def body(x_ref, g_ref, h_ref):
        xv = x_ref[...]
        r = lax.rsqrt(jnp.mean(xv * xv, axis=-1, keepdims=True) + NORM_EPS)
        h_ref[...] = (xv * r * g_ref[...]).astype(bf16)

    return pl.pallas_call(
        body, name=name, grid=(T // tm,),
        in_specs=[_rows(tm, D_MODEL), _full((1, D_MODEL))], out_specs=_rows(tm, D_MODEL),
        out_shape=_S((T, D_MODEL), bf16), compiler_params=_cp("parallel"),
    )(x, g)


def rms_bwd(gh, x, g, gres, name):
    T = x.shape[0]
    tm = _tile(T, TOKEN_TILE)

    def body(gh_ref, x_ref, g_ref, gres_ref, gx_ref, gg_ref):
        i = pl.program_id(0)
        xv = x_ref[...]
        ghv = gh_ref[...]
        r = lax.rsqrt(jnp.mean(xv * xv, axis=-1, keepdims=True) + NORM_EPS)
        nrm = xv * r
        gy = ghv * g_ref[...]
        gx_ref[...] = gres_ref[...] + r * (gy - nrm * jnp.mean(gy * nrm, axis=-1, keepdims=True))
        part = jnp.sum(ghv * nrm, axis=0, keepdims=True)

        @pl.when(i == 0)
        def _():
            gg_ref[...] = part

        @pl.when(i > 0)
        def _():
            gg_ref[...] += part

    return pl.pallas_call(
        body, name=name, grid=(T // tm,),
        in_specs=[_rows(tm, D_MODEL), _rows(tm, D_MODEL), _full((1, D_MODEL)), _rows(tm, D_MODEL)],
        out_specs=[_rows(tm, D_MODEL), _full((1, D_MODEL))],
        out_shape=[_S((T, D_MODEL)), _S((1, D_MODEL))], compiler_params=_cp("arbitrary"),
    )(gh, x, g, gres)


def _s5_discretise(a_re, a_im, log_dt, b_re_t, b_im_t):
    dt = jnp.exp(log_dt)
    mag = jnp.exp(a_re * dt)
    ab_re = mag * jnp.cos(a_im * dt)
    ab_im = mag * jnp.sin(a_im * dt)
    den = a_re * a_re + a_im * a_im
    nr, ni = ab_re - 1.0, ab_im
    z_re = (nr * a_re + ni * a_im) / den
    z_im = (ni * a_re - nr * a_im) / den
    bb_re = z_re[None] * b_re_t - z_im[None] * b_im_t
    bb_im = z_re[None] * b_im_t + z_im[None] * b_re_t
    return ab_re, ab_im, bb_re, bb_im


def s5_prep(a_re, a_im, log_dt, b_re_t, b_im_t, m, name):
    G, N = a_re.shape

    def body(are_ref, aim_ref, ldt_ref, bre_ref, bim_ref, ab_ref, pw_ref, bb_ref):
        are, aim, ldt = are_ref[...], aim_ref[...], ldt_ref[...]
        ab_re, ab_im, bb_re, bb_im = _s5_discretise(are, aim, ldt, bre_ref[...], bim_ref[...])
        ab_ref[0], ab_ref[1] = ab_re, ab_im
        bb_ref[0], bb_ref[1] = bb_re, bb_im
        dt = jnp.exp(ldt)
        for k in range(m):
            mag = jnp.exp(are * dt * (k + 1.0))
            pw_ref[0, k] = mag * jnp.cos(aim * dt * (k + 1.0))
            pw_ref[1, k] = mag * jnp.sin(aim * dt * (k + 1.0))

    return pl.pallas_call(
        body, name=name,
        out_shape=[_S((2, G, N)), _S((2, m, G, N)), _S((2, S5_GROUP, G, N))], compiler_params=_cp(),
    )(a_re, a_im, log_dt, b_re_t, b_im_t)


def s5_prep_bwd(a_re, a_im, log_dt, b_re_t, b_im_t, g_ab, g_bb, name):
    G, N = a_re.shape

    def body(are_ref, aim_ref, ldt_ref, bre_ref, bim_ref, gab_ref, gbb_ref, o_are, o_aim, o_ldt, o_bre, o_bim):
        _, vjp = jax.vjp(_s5_discretise, are_ref[...], aim_ref[...], ldt_ref[...], bre_ref[...], bim_ref[...])
        g_are, g_aim, g_ldt, g_bre, g_bim = vjp((gab_ref[0], gab_ref[1], gbb_ref[0], gbb_ref[1]))
        o_are[...], o_aim[...], o_ldt[...], o_bre[...], o_bim[...] = g_are, g_aim, g_ldt, g_bre, g_bim

    return pl.pallas_call(
        body, name=name,
        out_shape=[_S((G, N)), _S((G, N)), _S((G, 1)), _S((S5_GROUP, G, N)), _S((S5_GROUP, G, N))],
        compiler_params=_cp(),
    )(a_re, a_im, log_dt, b_re_t, b_im_t, g_ab, g_bb)


NB_S5 = S5_NS // LANES
CB_S5 = S5_WIDTH // LANES
SB_PER_CB = NB_S5 // CB_S5
GRP_PER_SB = LANES // S5_STATE
S5_JB = 8


def _bdb_mask():
    j = jnp.arange(NB_S5)
    own_rows = (j[:, None] % SB_PER_CB == jnp.arange(SB_PER_CB)[None, :]).astype(f32)
    eye = jnp.eye(GRP_PER_SB, dtype=f32)
    return own_rows[:, :, None, None, None, None, None] * eye[None, None, :, None, None, :, None]


def _pack_bdb(bb):
    v = jnp.transpose(bb.reshape(2, S5_GROUP, NB_S5, GRP_PER_SB, S5_STATE), (2, 3, 1, 0, 4))
    full = v[:, None, :, :, :, None, :] * _bdb_mask()
    return full.reshape(NB_S5, LANES, 2 * LANES)


def _unpack_bdb(g_bdb):
    g7 = g_bdb.reshape(NB_S5, SB_PER_CB, GRP_PER_SB, S5_GROUP, 2, GRP_PER_SB, S5_STATE)
    v = jnp.sum(g7 * _bdb_mask(), axis=(1, 5))
    return jnp.transpose(v, (3, 2, 0, 1, 4)).reshape(2, S5_GROUP, S5_GROUPS, S5_STATE)


def _pack_cdb(c_re, c_im):
    gl = S5_GROUPS // CB_S5
    c2 = jnp.stack([c_re, -c_im]).reshape(2, CB_S5, gl, S5_GROUP, S5_STATE)
    eye = jnp.eye(gl, dtype=f32)
    full = jnp.transpose(c2, (1, 0, 2, 4, 3))[:, :, :, :, None, :] * eye[None, None, :, None, :, None]
    return full.reshape(CB_S5, 2 * SB_PER_CB * LANES, LANES)


def _unpack_cdb(g_cdb):
    gl = S5_GROUPS // CB_S5
    g6 = g_cdb.reshape(CB_S5, 2, gl, S5_STATE, gl, S5_GROUP)
    eye = jnp.eye(gl, dtype=f32)
    v = jnp.sum(g6 * eye[None, None, :, None, :, None], axis=4)
    v = jnp.transpose(v, (1, 0, 2, 4, 3)).reshape(2, S5_GROUPS, S5_GROUP, S5_STATE)
    return v[0], -v[1]


def _state_cat(ref, c):
    w = SB_PER_CB * LANES
    return jnp.concatenate([ref[:, w * c:w * (c + 1)], ref[:, S5_NS + w * c:S5_NS + w * (c + 1)]], axis=1)


def _state_pair(ref, j):
    return jnp.concatenate([ref[:, _lanes(j)], ref[:, S5_NS + LANES * j:S5_NS + LANES * (j + 1)]], axis=1)


def s5_fwd(usg, bdb, cdb, dvec, abar_b, ptab_b, name):
    T = usg.shape[0]
    tc = _tile(T, SCAN_CHUNK)
    m = tc // SEGS
    assert ptab_b.shape == (2, m, SEGS, S5_NS) and m % 2 == 0

    def body(u_ref, bdb_ref, cdb_ref, d_ref, a_ref, p_ref, ys_ref, sre_ref, sim_ref, sbf_ref,
             src_re, src_im, dst_re, dst_im, cin_ref, carry_ref):
        i = pl.program_id(0)

        @pl.when(i == 0)
        def _():
            carry_ref[...] = jnp.zeros_like(carry_ref)

        u = u_ref[...]
        ub = u.astype(bf16)
        for j in range(NB_S5):
            bu = _dot(ub[:, _lanes(j // SB_PER_CB)], bdb_ref[j])
            src_re[:, _lanes(j)] = bu[:, :LANES]
            src_im[:, _lanes(j)] = bu[:, LANES:]
        for j0 in range(0, NB_S5, S5_JB):
            def kstep(k, st):
                rows = _step_rows(k)
                out = []
                for q in range(S5_JB):
                    ln = _lanes(j0 + q)
                    sr, si = st[2 * q], st[2 * q + 1]
                    ar, ai = a_ref[0, :, ln], a_ref[1, :, ln]
                    nr = ar * sr - ai * si + src_re[rows, ln]
                    ni = ar * si + ai * sr + src_im[rows, ln]
                    dst_re[rows, ln] = nr
                    dst_im[rows, ln] = ni
                    out += [nr, ni]
                return tuple(out)

            ends = lax.fori_loop(0, m, kstep, tuple(jnp.zeros((SEGS, LANES), f32) for _ in range(2 * S5_JB)))
            for q in range(S5_JB):
                ln = _lanes(j0 + q)
                er, ei = ends[2 * q], ends[2 * q + 1]
                cr, ci = carry_ref[0, :, ln], carry_ref[1, :, ln]
                amr, ami = p_ref[0, m - 1, 0:1, ln], p_ref[1, m - 1, 0:1, ln]
                rows_r, rows_i = [], []
                for s in range(SEGS):
                    rows_r.append(cr)
                    rows_i.append(ci)
                    cr, ci = (er[s:s + 1, :] + amr * cr - ami * ci, ei[s:s + 1, :] + amr * ci + ami * cr)
                cin_ref[0, 0:SEGS, ln] = _stack_rows(rows_r)
                cin_ref[1, 0:SEGS, ln] = _stack_rows(rows_i)
                carry_ref[0, :, ln] = cr
                carry_ref[1, :, ln] = ci
        cin_ref[:, SEGS:, :] = cin_ref[:, 0:SEGS, :]

        def fix(k2, _):
            rows = _step_rows(k2, PAIR)
            pr = p_ref[0, pl.ds(2 * k2, 2)].reshape(PAIR, S5_NS)
            pi = p_ref[1, pl.ds(2 * k2, 2)].reshape(PAIR, S5_NS)
            cr, ci = cin_ref[0], cin_ref[1]
            sr = dst_re[rows, :] + pr * cr - pi * ci
            si = dst_im[rows, :] + pr * ci + pi * cr
            sre_ref[rows, :] = sr
            sim_ref[rows, :] = si
            sbf_ref[rows, 0:S5_NS] = sr.astype(bf16)
            sbf_ref[rows, S5_NS:] = si.astype(bf16)
            return 0

        lax.fori_loop(0, m // 2, fix, 0)
        for c in range(CB_S5):
            ys_ref[:, _lanes(c)] = _dot(_state_cat(sbf_ref, c), cdb_ref[c]) + d_ref[:, _lanes(c)] * u[:, _lanes(c)]

    st = lambda w: _rows(tc, w)
    return pl.pallas_call(
        body, name=name, grid=(T // tc,),
        in_specs=[_rows(tc, S5_WIDTH, 0), _resident(bdb.shape), _resident(cdb.shape), _full((1, S5_WIDTH)),
                  _resident((2, SEGS, S5_NS)), _resident((2, m, SEGS, S5_NS))],
        out_specs=[st(S5_WIDTH), st(S5_NS), st(S5_NS), st(2 * S5_NS)],
        out_shape=[_S((T, S5_WIDTH)), _S((T, S5_NS)), _S((T, S5_NS)), _S((T, 2 * S5_NS), bf16)],
        scratch_shapes=[pltpu.VMEM((tc, S5_NS), f32)] * 4 + [pltpu.VMEM((2, PAIR, S5_NS), f32),
                                                             pltpu.VMEM((2, 1, S5_NS), f32)],
        compiler_params=_cp("arbitrary"),
    )(usg, bdb, cdb, dvec, abar_b, ptab_b)


def s5_bwd(gys, usg, s_re, s_im, s_bf, bdb, cdb, dvec, abar_b, ptab_rev_b, name):
    T = gys.shape[0]
    tc = _tile(T, SCAN_CHUNK)
    m = tc // SEGS
    nch = T // tc
    hb = tc // SUBLANES

    def body(gy_ref, u_ref, sre_ref, sim_ref, hre_ref, him_ref, sbf_ref, bdb_ref, cdb_ref, d_ref, a_ref, p_ref,
             gu_ref, gab_ref, gd_ref, gbdb_ref, gcdb_ref,
             src_re, src_im, dst_re, dst_im, lam_ref, cin_ref, acc_ref, carry_ref):
        i = pl.program_id(0)

        @pl.when(i == 0)
        def _():
            carry_ref[...] = jnp.zeros_like(carry_ref)
            for ref in (gab_ref, gd_ref, gbdb_ref, gcdb_ref):
                ref[...] = jnp.zeros_like(ref)

        first = i == nch - 1
        gy = gy_ref[...]
        gyb = gy.astype(bf16)
        u = u_ref[...]
        ub = u.astype(bf16)
        w = SB_PER_CB * LANES
        for c in range(CB_S5):
            gs = _dot_nt(gyb[:, _lanes(c)], cdb_ref[c])
            src_re[:, w * c:w * (c + 1)] = gs[:, :w]
            src_im[:, w * c:w * (c + 1)] = gs[:, w:]
            gcdb_ref[c] += _dot_tn(_state_cat(sbf_ref, c), gyb[:, _lanes(c)])
        for j0 in range(0, NB_S5, S5_JB):
            def kstep(kk, st):
                rows = _step_rows(m - 1 - kk)
                out = []
                for q in range(S5_JB):
                    ln = _lanes(j0 + q)
                    lr, li = st[2 * q], st[2 * q + 1]
                    ar, ai = a_ref[0, :, ln], a_ref[1, :, ln]
                    nr = ar * lr + ai * li + src_re[rows, ln]
                    ni = ar * li - ai * lr + src_im[rows, ln]
                    dst_re[rows, ln] = nr
                    dst_im[rows, ln] = ni
                    out += [nr, ni]
                return tuple(out)

            ends = lax.fori_loop(0, m, kstep, tuple(jnp.zeros((SEGS, LANES), f32) for _ in range(2 * S5_JB)))
            for q in range(S5_JB):
                ln = _lanes(j0 + q)
                er, ei = ends[2 * q], ends[2 * q + 1]
                cr, ci = carry_ref[0, :, ln], carry_ref[1, :, ln]
                amr, ami = p_ref[0, 0, 0:1, ln], p_ref[1, 0, 0:1, ln]
                rows_r, rows_i = [None] * SEGS, [None] * SEGS
                for s in reversed(range(SEGS)):
                    rows_r[s], rows_i[s] = cr, ci
                    cr, ci = (er[s:s + 1, :] + amr * cr + ami * ci, ei[s:s + 1, :] + amr * ci - ami * cr)
                cin_ref[0, 0:SEGS, ln] = _stack_rows(rows_r)
                cin_ref[1, 0:SEGS, ln] = _stack_rows(rows_i)
                carry_ref[0, :, ln] = cr
                carry_ref[1, :, ln] = ci
        cin_ref[:, SEGS:, :] = cin_ref[:, 0:SEGS, :]
        acc_ref[...] = jnp.zeros_like(acc_ref)

        def fix_rows(rows, k2, prev_re, prev_im):
            pr = p_ref[0, pl.ds(2 * k2, 2)].reshape(PAIR, S5_NS)
            pi = p_ref[1, pl.ds(2 * k2, 2)].reshape(PAIR, S5_NS)
            cr, ci = cin_ref[0], cin_ref[1]
            lr = dst_re[rows, :] + pr * cr + pi * ci
            li = dst_im[rows, :] + pr * ci - pi * cr
            lam_ref[rows, 0:S5_NS] = lr.astype(bf16)
            lam_ref[rows, S5_NS:] = li.astype(bf16)
            acc_ref[0] += lr * prev_re + li * prev_im
            acc_ref[1] += li * prev_re - lr * prev_im

        last = slice(tc - SUBLANES, tc)
        wrap_re = _down_a_segment(sre_ref[last, :], jnp.where(first, 0.0, hre_ref[SUBLANES - 1:SUBLANES, :]))
        wrap_im = _down_a_segment(sim_ref[last, :], jnp.where(first, 0.0, him_ref[SUBLANES - 1:SUBLANES, :]))
        fix_rows(pl.ds(0, PAIR), 0, jnp.concatenate([wrap_re, sre_ref[0:SUBLANES, :]], axis=0),
                 jnp.concatenate([wrap_im, sim_ref[0:SUBLANES, :]], axis=0))

        def fix(k2, _):
            prev = pl.ds(pl.multiple_of(k2 * PAIR - SUBLANES, SUBLANES), PAIR)
            fix_rows(_step_rows(k2, PAIR), k2, sre_ref[prev, :], sim_ref[prev, :])
            return 0

        lax.fori_loop(1, m // 2, fix, 0)
        gab_ref[0] += jnp.sum(acc_ref[0], axis=0, keepdims=True)
        gab_ref[1] += jnp.sum(acc_ref[1], axis=0, keepdims=True)
        for c in range(CB_S5):
            x = gy[:, _lanes(c)] * d_ref[:, _lanes(c)]
            for j in range(SB_PER_CB * c, SB_PER_CB * (c + 1)):
                pair = _state_pair(lam_ref, j)
                x = x + _dot_nt(pair, bdb_ref[j])
                gbdb_ref[j] += _dot_tn(ub[:, _lanes(c)], pair)
            gu_ref[:, _lanes(c)] = x.astype(bf16)
        gd_ref[...] += jnp.sum(gy * u, axis=0, keepdims=True)

    rev = lambda i: (nch - 1 - i, 0)
    halo = lambda i: (jnp.maximum((nch - 1 - i) * hb - 1, 0), 0)
    blk = lambda wd: pl.BlockSpec((tc, wd), rev)
    return pl.pallas_call(
        body, name=name, grid=(nch,),
        in_specs=[blk(S5_WIDTH), blk(S5_WIDTH), blk(S5_NS), blk(S5_NS),
                  pl.BlockSpec((SUBLANES, S5_NS), halo), pl.BlockSpec((SUBLANES, S5_NS), halo), blk(2 * S5_NS),
                  _resident(bdb.shape), _resident(cdb.shape), _full((1, S5_WIDTH)),
                  _resident((2, SEGS, S5_NS)), _resident((2, m, SEGS, S5_NS))],
        out_specs=[blk(S5_WIDTH), _full((2, 1, S5_NS)), _full((1, S5_WIDTH)), _full(bdb.shape), _full(cdb.shape)],
        out_shape=[_S((T, S5_WIDTH), bf16), _S((2, 1, S5_NS)), _S((1, S5_WIDTH)), _S(bdb.shape), _S(cdb.shape)],
        scratch_shapes=[pltpu.VMEM((tc, S5_NS), f32)] * 4 + [
            pltpu.VMEM((tc, 2 * S5_NS), bf16), pltpu.VMEM((2, PAIR, S5_NS), f32), pltpu.VMEM((2, PAIR, S5_NS), f32),
            pltpu.VMEM((2, 1, S5_NS), f32)],
        compiler_params=_cp("arbitrary"),
    )(gys, usg, s_re, s_im, s_re, s_im, s_bf, bdb, cdb, dvec, abar_b, ptab_rev_b)


def s5_post_fwd(ys, usg, wglu, wbs, name):
    T = ys.shape[0]
    tm = _tile(T, TOKEN_TILE)

    def body(ys_ref, sg_ref, wglu_ref, wbs_ref, glu_ref, zs_ref):
        glu = _dot(_gelu(ys_ref[...]).astype(bf16), wglu_ref[...])
        sg = sg_ref[...]
        y2 = glu[:, :S5_WIDTH] * _sig(glu[:, S5_WIDTH:]) * (sg * _sig(sg))
        glu_ref[...] = glu
        zs_ref[...] = _dot(y2.astype(bf16), wbs_ref[...])

    return pl.pallas_call(
        body, name=name, grid=(T // tm,),
        in_specs=[_rows(tm, S5_WIDTH), _rows(tm, S5_WIDTH, 1), _resident((S5_WIDTH, 2 * S5_WIDTH)),
                  _resident((S5_WIDTH, D_MODEL))],
        out_specs=[_rows(tm, 2 * S5_WIDTH), _rows(tm, D_MODEL)],
        out_shape=[_S((T, 2 * S5_WIDTH)), _S((T, D_MODEL))], compiler_params=_cp("parallel"),
    )(ys, usg, wglu, wbs)


def s5_post_bwd(gzs, glu, usg, ys, wbs, wglu, name):
    T = ys.shape[0]
    tm = _tile(T, TOKEN_TILE)

    def body(gzs_ref, glu_ref, sg_ref, ys_ref, wbs_ref, wglu_ref, y2_ref, gglu_ref, ge_ref, gys_ref, gsg_ref):
        glu = glu_ref[...]
        a, b = glu[:, :S5_WIDTH], glu[:, S5_WIDTH:]
        sg = sg_ref[...]
        ys = ys_ref[...]
        sb, ssg = _sig(b), _sig(sg)
        silu = sg * ssg
        y2_ref[...] = (a * sb * silu).astype(bf16)
        gy2 = _dot_nt(gzs_ref[...], wbs_ref[...])
        g_a = gy2 * sb * silu
        g_b = gy2 * a * sb * (1.0 - sb) * silu
        gsg_ref[...] = (gy2 * a * sb * ssg * (1.0 + sg * (1.0 - ssg))).astype(bf16)
        gglu = jnp.concatenate([g_a, g_b], axis=1).astype(bf16)
        gglu_ref[...] = gglu
        ge_ref[...] = _gelu(ys).astype(bf16)
        gys_ref[...] = _dot_nt(gglu, wglu_ref[...]) * _gelu_grad(ys)

    return pl.pallas_call(
        body, name=name, grid=(T // tm,),
        in_specs=[_rows(tm, D_MODEL), _rows(tm, 2 * S5_WIDTH), _rows(tm, S5_WIDTH, 1), _rows(tm, S5_WIDTH),
                  _resident((S5_WIDTH, D_MODEL)), _resident((S5_WIDTH, 2 * S5_WIDTH))],
        out_specs=[_rows(tm, S5_WIDTH), _rows(tm, 2 * S5_WIDTH), _rows(tm, S5_WIDTH), _rows(tm, S5_WIDTH), _rows(tm, S5_WIDTH)],
        out_shape=[_S((T, S5_WIDTH), bf16), _S((T, 2 * S5_WIDTH), bf16), _S((T, S5_WIDTH), bf16), _S((T, S5_WIDTH)),
                   _S((T, S5_WIDTH), bf16)],
        compiler_params=_cp("parallel"),
    )(gzs, glu, usg, ys, wbs, wglu)


NB_LRU = LRU_WIDTH // LANES
LRU_JB = 5
TAPS_BACK = CONV_WIDTH - 1
EDGE = TAPS_BACK * SUBLANES
HALO_ROWS = 4 * SUBLANES


def _down_a_segment(blk, entering_row):
    sub = lax.broadcasted_iota(jnp.int32, blk.shape, 0)
    return jnp.where(sub == 0, entering_row, pltpu.roll(blk, 1, 0))


def _up_a_segment(blk, entering_row):
    sub = lax.broadcasted_iota(jnp.int32, blk.shape, 0)
    return jnp.where(sub == SUBLANES - 1, entering_row, pltpu.roll(blk, SUBLANES - 1, 0))


def _stack_rows(rows):
    sub = lax.broadcasted_iota(jnp.int32, (SUBLANES,) + rows[0].shape[1:], 0)
    out = jnp.broadcast_to(rows[0], sub.shape)
    for s in range(1, SUBLANES):
        out = jnp.where(sub == s, rows[s], out)
    return out


def _fill_conv_window(xe, x_ref, xh_ref, is_first, tc):
    xe[EDGE:, :] = x_ref[...]
    for i in range(1, TAPS_BACK + 1):
        row = HALO_ROWS - SUBLANES * i + SUBLANES - 1
        entering = jnp.where(is_first, 0.0, xh_ref[row:row + 1, :])
        blk = x_ref[tc - SUBLANES * i:tc - SUBLANES * (i - 1), :]
        xe[EDGE - SUBLANES * i:EDGE - SUBLANES * (i - 1), :] = _down_a_segment(blk, entering)


def lru_fwd(lx, convw, convb, wa, wx, ba, bx, lam, name):
    T = lx.shape[0]
    tc = _tile(T, SCAN_CHUNK)
    m = tc // SEGS
    hb = tc // HALO_ROWS

    def body(x_ref, xh_ref, cw_ref, cb_ref, wa_ref, wx_ref, ba_ref, bx_ref, lam_ref,
             c_ref, r_ref, i_ref, h_ref, xe, src_a, src_b, dst_a, dst_h, cin_ref, carry_ref):
        i = pl.program_id(0)

        @pl.when(i == 0)
        def _():
            carry_ref[...] = jnp.zeros_like(carry_ref)

        _fill_conv_window(xe, x_ref, xh_ref, i == 0, tc)
        c = cb_ref[...] + cw_ref[0:1, :] * xe[0:tc, :]
        for k in range(1, CONV_WIDTH):
            c = c + cw_ref[k:k + 1, :] * xe[SUBLANES * k:SUBLANES * k + tc, :]
        c_ref[...] = c
        sp = _softplus_neg(lam_ref[...])
        for j in range(NB_LRU):
            ln = _lanes(j)
            cj = c[:, ln]
            cjb = cj.astype(bf16)
            r = _sig(_dot(cjb, wa_ref[j]) + ba_ref[:, ln])
            g = _sig(_dot(cjb, wx_ref[j]) + bx_ref[:, ln])
            r_ref[:, ln] = r
            i_ref[:, ln] = g
            log_a = -LRU_C * r * sp[:, ln]
            src_a[:, ln] = jnp.exp(log_a)
            src_b[:, ln] = jnp.sqrt(_one_minus_exp(2.0 * log_a)) * (g * cj)
        for j0 in range(0, NB_LRU, LRU_JB):
            def kstep(k, st):
                rows = _step_rows(k)
                out = []
                for q in range(LRU_JB):
                    ln = _lanes(j0 + q)
                    hh, ac = st[2 * q], st[2 * q + 1]
                    a = src_a[rows, ln]
                    hh = a * hh + src_b[rows, ln]
                    ac = a * ac
                    dst_h[rows, ln] = hh
                    dst_a[rows, ln] = ac
                    out += [hh, ac]
                return tuple(out)

            init = tuple(jnp.zeros((SEGS, LANES), f32) if q % 2 == 0 else jnp.ones((SEGS, LANES), f32)
                         for q in range(2 * LRU_JB))
            ends = lax.fori_loop(0, m, kstep, init)
            for q in range(LRU_JB):
                ln = _lanes(j0 + q)
                eh, ea = ends[2 * q], ends[2 * q + 1]
                cr = carry_ref[:, ln]
                rows_c = []
                for s in range(SEGS):
                    rows_c.append(cr)
                    cr = eh[s:s + 1, :] + ea[s:s + 1, :] * cr
                cin_ref[:, ln] = _stack_rows(rows_c)
                carry_ref[:, ln] = cr

        def fix(k, _):
            rows = _step_rows(k)
            h_ref[rows, :] = dst_h[rows, :] + dst_a[rows, :] * cin_ref[...]
            return 0

        lax.fori_loop(0, m, fix, 0)

    wide = lambda: _rows(tc, LRU_WIDTH)
    buf = lambda rows: pltpu.VMEM((rows, LRU_WIDTH), f32)
    return pl.pallas_call(
        body, name=name, grid=(T // tc,),
        in_specs=[wide(), pl.BlockSpec((HALO_ROWS, LRU_WIDTH), lambda i: (jnp.maximum(i * hb - 1, 0), 0)),
                  _full((CONV_WIDTH, LRU_WIDTH)), _full((1, LRU_WIDTH)),
                  _full((LRU_HEADS, LRU_HEAD_DIM, LRU_HEAD_DIM)), _full((LRU_HEADS, LRU_HEAD_DIM, LRU_HEAD_DIM)),
                  _full((1, LRU_WIDTH)), _full((1, LRU_WIDTH)), _full((1, LRU_WIDTH))],
        out_specs=[wide(), wide(), wide(), wide()],
        out_shape=[_S((T, LRU_WIDTH))] * 4,
        scratch_shapes=[buf(tc + EDGE), buf(tc), buf(tc), buf(tc), buf(tc), buf(SEGS), buf(1)],
        compiler_params=_cp("arbitrary"),
    )(lx, lx, convw, convb, wa, wx, ba, bx, lam)


def lru_bwd(gh, h, c, r, gi, lx, convw, wa, wx, lam, name):
    T = gh.shape[0]
    tc = _tile(T, SCAN_CHUNK)
    m = tc // SEGS
    nch = T // tc

    def body(gh_ref, h_ref, hh_ref, c_ref, r_ref, i_ref, x_ref, xh_ref, cw_ref, wa_ref, wx_ref, lam_ref,
             glx_ref, gwa_ref, gwx_ref, gba_ref, gbx_ref, glam_ref, gcb_ref, gcw_ref,
             src_a, src_m, dst_a, dst_m, mbuf, hbuf, xe, gce, cin_ref, gcc_ref, carry_ref):
        i = pl.program_id(0)

        @pl.when(i == 0)
        def _():
            carry_ref[...] = jnp.zeros_like(carry_ref)
            gcc_ref[...] = jnp.zeros_like(gcc_ref)
            for ref in (gwa_ref, gwx_ref, gba_ref, gbx_ref, glam_ref, gcb_ref, gcw_ref):
                ref[...] = jnp.zeros_like(ref)

        first = i == nch - 1
        last = slice(tc - SUBLANES, tc)
        hbuf[SUBLANES:, :] = h_ref[...]
        hbuf[0:SUBLANES, :] = _down_a_segment(h_ref[last, :], jnp.where(first, 0.0, hh_ref[SUBLANES - 1:SUBLANES, :]))
        _fill_conv_window(xe, x_ref, xh_ref, first, tc)
        lam_v = lam_ref[...]
        sp = _softplus_neg(lam_v)
        a_all = jnp.exp(-LRU_C * r_ref[...] * sp)
        src_a[...] = a_all
        src_m[...] = a_all * gh_ref[...]
        for j0 in range(0, NB_LRU, LRU_JB):
            def kstep(kk, st):
                rows = _step_rows(m - 1 - kk)
                out = []
                for q in range(LRU_JB):
                    ln = _lanes(j0 + q)
                    mu, ac = st[2 * q], st[2 * q + 1]
                    a = src_a[rows, ln]
                    mu = a * mu + src_m[rows, ln]
                    ac = a * ac
                    dst_m[rows, ln] = mu
                    dst_a[rows, ln] = ac
                    out += [mu, ac]
                return tuple(out)

            init = tuple(jnp.zeros((SEGS, LANES), f32) if q % 2 == 0 else jnp.ones((SEGS, LANES), f32)
                         for q in range(2 * LRU_JB))
            ends = lax.fori_loop(0, m, kstep, init)
            for q in range(LRU_JB):
                ln = _lanes(j0 + q)
                em, ea = ends[2 * q], ends[2 * q + 1]
                cr = carry_ref[:, ln]
                rows_c = [None] * SEGS
                for s in reversed(range(SEGS)):
                    rows_c[s] = cr
                    cr = em[s:s + 1, :] + ea[s:s + 1, :] * cr
                cin_ref[:, ln] = _stack_rows(rows_c)
                carry_ref[:, ln] = cr

        def fix(k, _):
            rows = _step_rows(k)
            mbuf[rows, :] = dst_m[rows, :] + dst_a[rows, :] * cin_ref[...]
            return 0

        lax.fori_loop(0, m, fix, 0)
        mbuf[tc:, :] = _up_a_segment(mbuf[0:SUBLANES, :], cin_ref[SUBLANES - 1:SUBLANES, :])
        sneg = _sig(-lam_v)
        for j in range(NB_LRU):
            ln = _lanes(j)
            lamt = gh_ref[:, ln] + mbuf[SUBLANES:, ln]
            rj, ij, cj = r_ref[:, ln], i_ref[:, ln], c_ref[:, ln]
            log_a = -LRU_C * rj * sp[:, ln]
            a = jnp.exp(log_a)
            mult = jnp.sqrt(_one_minus_exp(2.0 * log_a))
            g_a = lamt * hbuf[0:tc, ln]
            g_mult = lamt * ij * cj
            g_i = lamt * mult * cj
            g_c = lamt * mult * ij
            g_log_a = g_a * a - g_mult * a * a / mult
            glam_ref[:, ln] += jnp.sum(g_log_a * rj, axis=0, keepdims=True) * LRU_C * sneg[:, ln]
            g_ra = g_log_a * (-LRU_C) * sp[:, ln] * rj * (1.0 - rj)
            g_ia = g_i * ij * (1.0 - ij)
            gba_ref[:, ln] += jnp.sum(g_ra, axis=0, keepdims=True)
            gbx_ref[:, ln] += jnp.sum(g_ia, axis=0, keepdims=True)
            cjb, grb, gib = cj.astype(bf16), g_ra.astype(bf16), g_ia.astype(bf16)
            gwa_ref[j] += _dot_tn(cjb, grb)
            gwx_ref[j] += _dot_tn(cjb, gib)
            g_c = g_c + _dot_nt(grb, wa_ref[j]) + _dot_nt(gib, wx_ref[j])
            gce[0:tc, ln] = g_c
            gcb_ref[:, ln] += jnp.sum(g_c, axis=0, keepdims=True)
        for d in range(TAPS_BACK):
            blk = slice(SUBLANES * d, SUBLANES * (d + 1))
            gce[tc + SUBLANES * d:tc + SUBLANES * (d + 1), :] = _up_a_segment(gce[blk, :], gcc_ref[SUBLANES * d:SUBLANES * d + 1, :])
        gcc_ref[...] = gce[0:EDGE, :]
        gc = gce[0:tc, :]
        glx = cw_ref[CONV_WIDTH - 1:CONV_WIDTH, :] * gc
        gcw_ref[CONV_WIDTH - 1:CONV_WIDTH, :] += jnp.sum(gc * xe[EDGE:EDGE + tc, :], axis=0, keepdims=True)
        for k in range(CONV_WIDTH - 1):
            off = SUBLANES * (CONV_WIDTH - 1 - k)
            glx = glx + cw_ref[k:k + 1, :] * gce[off:off + tc, :]
            gcw_ref[k:k + 1, :] += jnp.sum(gc * xe[EDGE - off:EDGE - off + tc, :], axis=0, keepdims=True)
        glx_ref[...] = glx.astype(bf16)

    rev = lambda i: (nch - 1 - i, 0)
    halo = lambda rows: (lambda i: (jnp.maximum((nch - 1 - i) * (tc // rows) - 1, 0), 0))
    wide = lambda: pl.BlockSpec((tc, LRU_WIDTH), rev)
    vec = lambda: _full((1, LRU_WIDTH))
    hd = lambda: _full((LRU_HEADS, LRU_HEAD_DIM, LRU_HEAD_DIM))
    buf = lambda rows: pltpu.VMEM((rows, LRU_WIDTH), f32)
    return pl.pallas_call(
        body, name=name, grid=(nch,),
        in_specs=[wide(), wide(), pl.BlockSpec((SUBLANES, LRU_WIDTH), halo(SUBLANES)), wide(), wide(), wide(), wide(),
                  pl.BlockSpec((HALO_ROWS, LRU_WIDTH), halo(HALO_ROWS)), _full((CONV_WIDTH, LRU_WIDTH)), hd(), hd(), vec()],
        out_specs=[wide(), hd(), hd(), vec(), vec(), vec(), vec(), _full((CONV_WIDTH, LRU_WIDTH))],
        out_shape=[_S((T, LRU_WIDTH), bf16), _S((LRU_HEADS, LRU_HEAD_DIM, LRU_HEAD_DIM)),
                   _S((LRU_HEADS, LRU_HEAD_DIM, LRU_HEAD_DIM)), _S((1, LRU_WIDTH)), _S((1, LRU_WIDTH)),
                   _S((1, LRU_WIDTH)), _S((1, LRU_WIDTH)), _S((CONV_WIDTH, LRU_WIDTH))],
        scratch_shapes=[buf(tc), buf(tc), buf(tc), buf(tc), buf(tc + SUBLANES), buf(tc + SUBLANES), buf(tc + EDGE),
                        buf(tc + EDGE), buf(SEGS), buf(EDGE), buf(1)],
        compiler_params=_cp("arbitrary"),
    )(gh, h, h, c, r, gi, lx, lx, convw, wa, wx, lam)


def merge_fwd(h, lg, zs, gsl, x, p, wbl, wout, gpost, wple, wpg, name):
    T = x.shape[0]
    tm = _tile(T, TOKEN_TILE)

    def body(h_ref, lg_ref, zs_ref, gs_ref, gl_ref, x_ref, p_ref, wbl_ref, wout_ref, gp_ref, wple_ref, wpg_ref,
             zl_ref, mix_ref, q_ref, pe_ref, xo_ref):
        lg_v = lg_ref[...]
        yl = h_ref[...] * (lg_v * _sig(lg_v))
        zl = _dot(yl.astype(bf16), wbl_ref[...])
        merged = _sig(gs_ref[...]) * zs_ref[...] + _sig(gl_ref[...]) * zl
        mix = _dot(merged.astype(bf16), wout_ref[...])
        r2 = lax.rsqrt(jnp.mean(mix * mix, axis=-1, keepdims=True) + NORM_EPS)
        x1 = x_ref[...] + mix * r2 * gp_ref[...]
        q = _dot(x1.astype(bf16), wpg_ref[...])
        pe = _dot(p_ref[...].astype(bf16), wple_ref[...])
        zl_ref[...], mix_ref[...], q_ref[...], pe_ref[...] = zl, mix, q, pe
        xo_ref[...] = x1 + pe * _sig(q)

    dm = lambda: _rows(tm, D_MODEL)
    return pl.pallas_call(
        body, name=name, grid=(T // tm,),
        in_specs=[_rows(tm, LRU_WIDTH), _rows(tm, LRU_WIDTH), dm(), _rows(tm, D_MODEL, 0), _rows(tm, D_MODEL, 1), dm(),
                  _rows(tm, PLE_DIM), _resident((LRU_WIDTH, D_MODEL)), _resident((D_MODEL, D_MODEL)), _full((1, D_MODEL)),
                  _resident((PLE_DIM, D_MODEL)), _resident((D_MODEL, D_MODEL))],
        out_specs=[dm(), dm(), dm(), dm(), dm()],
        out_shape=[_S((T, D_MODEL))] * 5, compiler_params=_cp("parallel"),
    )(h, lg, zs, gsl, gsl, x, p, wbl, wout, gpost, wple, wpg)


def merge_bwd(gx2, q, pe, mix, x, zl, zs, gsl, h, lg, wpg, wout, wbl, gpost, name):
    T = x.shape[0]
    tm = _tile(T, TOKEN_TILE)

    def body(gx2_ref, q_ref, pe_ref, mix_ref, x_ref, zl_ref, zs_ref, gs_ref, gl_ref, h_ref, lg_ref,
             wpg_ref, wout_ref, wbl_ref, gp_ref,
             gres_ref, gpe_ref, gq_ref, x1_ref, gmix_ref, mrg_ref, gzl_ref, yl_ref, gzs_ref, ggsl_ref, gh_ref, glg_ref,
             ggp_ref):
        i = pl.program_id(0)
        gx2 = gx2_ref[...]
        sq = _sig(q_ref[...])
        pe = pe_ref[...]
        gpe_ref[...] = (gx2 * sq).astype(bf16)
        gq = (gx2 * pe * sq * (1.0 - sq)).astype(bf16)
        gq_ref[...] = gq
        mix = mix_ref[...]
        gp = gp_ref[...]
        r2 = lax.rsqrt(jnp.mean(mix * mix, axis=-1, keepdims=True) + NORM_EPS)
        nrm = mix * r2
        x1_ref[...] = (x_ref[...] + nrm * gp).astype(bf16)
        gx1 = gx2 + _dot_nt(gq, wpg_ref[...])
        gres_ref[...] = gx1
        part = jnp.sum(gx1 * nrm, axis=0, keepdims=True)

        @pl.when(i == 0)
        def _():
            ggp_ref[...] = part

        @pl.when(i > 0)
        def _():
            ggp_ref[...] += part

        gy = gx1 * gp
        gmix = (r2 * (gy - nrm * jnp.mean(gy * nrm, axis=-1, keepdims=True))).astype(bf16)
        gmix_ref[...] = gmix
        gmerged = _dot_nt(gmix, wout_ref[...])
        zs, zl = zs_ref[...], zl_ref[...]
        ss, sl = _sig(gs_ref[...]), _sig(gl_ref[...])
        mrg_ref[...] = (ss * zs + sl * zl).astype(bf16)
        gzs_ref[...] = (gmerged * ss).astype(bf16)
        gzl = (gmerged * sl).astype(bf16)
        gzl_ref[...] = gzl
        ggsl_ref[:, :D_MODEL] = (gmerged * zs * ss * (1.0 - ss)).astype(bf16)
        ggsl_ref[:, D_MODEL:] = (gmerged * zl * sl * (1.0 - sl)).astype(bf16)
        lg_v, hv = lg_ref[...], h_ref[...]
        slg = _sig(lg_v)
        silu = lg_v * slg
        yl_ref[...] = (hv * silu).astype(bf16)
        gyl = _dot_nt(gzl, wbl_ref[...])
        gh_ref[...] = gyl * silu
        glg_ref[...] = (gyl * hv * slg * (1.0 + lg_v * (1.0 - slg))).astype(bf16)

    dm = lambda: _rows(tm, D_MODEL)
    lw = lambda: _rows(tm, LRU_WIDTH)
    return pl.pallas_call(
        body, name=name, grid=(T // tm,),
        in_specs=[dm(), dm(), dm(), dm(), dm(), dm(), dm(), _rows(tm, D_MODEL, 0), _rows(tm, D_MODEL, 1), lw(), lw(),
                  _resident((D_MODEL, D_MODEL)), _resident((D_MODEL, D_MODEL)), _resident((LRU_WIDTH, D_MODEL)),
                  _full((1, D_MODEL))],
        out_specs=[dm(), dm(), dm(), dm(), dm(), dm(), dm(), lw(), dm(), _rows(tm, 2 * D_MODEL), lw(), lw(),
                   _full((1, D_MODEL))],
        out_shape=[_S((T, D_MODEL)), _S((T, D_MODEL), bf16), _S((T, D_MODEL), bf16), _S((T, D_MODEL), bf16),
                   _S((T, D_MODEL), bf16), _S((T, D_MODEL), bf16), _S((T, D_MODEL), bf16), _S((T, LRU_WIDTH), bf16),
                   _S((T, D_MODEL), bf16), _S((T, 2 * D_MODEL), bf16), _S((T, LRU_WIDTH)), _S((T, LRU_WIDTH), bf16),
                   _S((1, D_MODEL))],
        compiler_params=_cp("arbitrary"),
    )(gx2, q, pe, mix, x, zl, zs, gsl, gsl, h, lg, wpg, wout, wbl, gpost)


def loss_head(y, target, name):
    T = y.shape[0]
    tm = _tile(T, TOKEN_TILE)

    def body(y_ref, t_ref, l_ref, g_ref):
        i = pl.program_id(0)
        e = y_ref[...] - t_ref[...]
        g_ref[...] = e * (1.0 / D_MODEL)
        part = 0.5 * jnp.sum(jnp.sum(e * e, axis=-1, keepdims=True) * (1.0 / D_MODEL), axis=0, keepdims=True)

        @pl.when(i == 0)
        def _():
            l_ref[...] = part

        @pl.when(i > 0)
        def _():
            l_ref[...] += part

    return pl.pallas_call(
        body, name=name, grid=(T // tm,),
        in_specs=[_rows(tm, D_MODEL), _rows(tm, D_MODEL)], out_specs=[_full((1, 1)), _rows(tm, D_MODEL)],
        out_shape=[_S((1, 1)), _S((T, D_MODEL))],
        compiler_params=_cp("arbitrary"),
    )(y, target)


def _s5_operands(w, m, tag):
    b_re_t = jnp.transpose(w['s5_b_re'], (2, 0, 1))
    b_im_t = jnp.transpose(w['s5_b_im'], (2, 0, 1))
    ldt = w['s5_log_dt'][:, None]
    ab, pw, bb = s5_prep(w['s5_a_re'], w['s5_a_im'], ldt, b_re_t, b_im_t, m, "s5_prep" + tag)
    over_sublanes = lambda t: jnp.broadcast_to(t[..., None, :], t.shape[:-1] + (SEGS, S5_NS))
    ptab = pw.reshape(2, m, S5_NS)
    return dict(abar_b=over_sublanes(ab.reshape(2, S5_NS)), ptab_b=over_sublanes(ptab),
                ptab_rev_b=over_sublanes(ptab[:, ::-1, :]), bdb=_pack_bdb(bb).astype(bf16),
                cdb=_pack_cdb(w['s5_c_re'], w['s5_c_im']).astype(bf16), dvec=w['s5_d'][None, :],
                prep_in=(w['s5_a_re'], w['s5_a_im'], ldt, b_re_t, b_im_t))


def layer_fwd(x, p, w, tag):
    T = x.shape[0]
    m = _tile(T, SCAN_CHUNK) // SEGS
    s5 = _s5_operands(w, m, tag)
    h_bf = rms_fwd(x, w['g_pre'][None, :], "rms_fwd" + tag)
    win = w['w_in']
    usg = mm_nn(h_bf, win[:, :2 * S5_WIDTH], "proj_s5" + tag)
    lx = mm_nn(h_bf, win[:, 2 * S5_WIDTH:2 * S5_WIDTH + LRU_WIDTH], "proj_lx" + tag)
    lg = mm_nn(h_bf, win[:, 2 * S5_WIDTH + LRU_WIDTH:2 * S5_WIDTH + 2 * LRU_WIDTH], "proj_lg" + tag)
    gsl = mm_nn(h_bf, win[:, 2 * S5_WIDTH + 2 * LRU_WIDTH:], "proj_gate" + tag)
    ys, s_re, s_im, s_bf = s5_fwd(usg, s5['bdb'], s5['cdb'], s5['dvec'], s5['abar_b'], s5['ptab_b'], "s5_fwd" + tag)
    glu, zs = s5_post_fwd(ys, usg, w['w_glu'], w['w_bs'], "s5_post_fwd" + tag)
    wa, wx = w['lru_w_a'].astype(bf16), w['lru_w_x'].astype(bf16)
    c, r, gi, hs = lru_fwd(lx, w['conv_w'], w['conv_b'][None, :], wa, wx, w['lru_b_a'][None, :], w['lru_b_x'][None, :],
                           w['lru_lambda'][None, :], "lru_fwd" + tag)
    zl, mix, q, pe, x_out = merge_fwd(hs, lg, zs, gsl, x, p, w['w_bl'], w['w_out'], w['g_post'][None, :],
                                      w['w_ple'], w['w_ple_gate'], "merge_fwd" + tag)
    saved = dict(x=x, p=p, h_bf=h_bf, usg=usg, lx=lx, lg=lg, gsl=gsl, ys=ys, s_re=s_re, s_im=s_im, s_bf=s_bf, glu=glu,
                 zs=zs, c=c, r=r, gi=gi, hs=hs, zl=zl, mix=mix, q=q, pe=pe, s5=s5, wa=wa, wx=wx)
    return x_out, saved


def layer_bwd(gx_out, w, sv, tag):
    s5 = sv['s5']
    (gres, gpe, gq, x1_bf, gmix, merged, gzl, yl, gzs, ggsl, g_h, g_lg, g_gpost) = merge_bwd(
        gx_out, sv['q'], sv['pe'], sv['mix'], sv['x'], sv['zl'], sv['zs'], sv['gsl'], sv['hs'], sv['lg'],
        w['w_ple_gate'], w['w_out'], w['w_bl'], w['g_post'][None, :], "merge_bwd" + tag)
    g = {}
    g['w_ple'] = mm_tn(sv['p'], gpe, "gw_ple" + tag)
    g['w_ple_gate'] = mm_tn(x1_bf, gq, "gw_ple_gate" + tag)
    g['w_out'] = mm_tn(merged, gmix, "gw_out" + tag)
    g['w_bl'] = mm_tn(yl, gzl, "gw_bl" + tag)
    g['g_post'] = g_gpost[0]
    (g_lx, g_wa, g_wx, g_ba, g_bx, g_lam, g_cb, g_cw) = lru_bwd(
        g_h, sv['hs'], sv['c'], sv['r'], sv['gi'], sv['lx'], w['conv_w'], sv['wa'], sv['wx'],
        w['lru_lambda'][None, :], "lru_bwd" + tag)
    g['lru_w_a'], g['lru_w_x'] = g_wa, g_wx
    g['lru_b_a'], g['lru_b_x'], g['lru_lambda'], g['conv_b'], g['conv_w'] = g_ba[0], g_bx[0], g_lam[0], g_cb[0], g_cw
    y2, gglu, ge, g_ys, g_sg = s5_post_bwd(gzs, sv['glu'], sv['usg'], sv['ys'], w['w_bs'], w['w_glu'], "s5_post_bwd" + tag)
    g['w_bs'] = mm_tn(y2, gzs, "gw_bs" + tag)
    g['w_glu'] = mm_tn(ge, gglu, "gw_glu" + tag)
    g_u, g_ab, g_d, g_bdb, g_cdb = s5_bwd(g_ys, sv['usg'], sv['s_re'], sv['s_im'], sv['s_bf'], s5['bdb'], s5['cdb'],
                                          s5['dvec'], s5['abar_b'], s5['ptab_rev_b'], "s5_bwd" + tag)
    g['s5_d'] = g_d[0]
    g['s5_c_re'], g['s5_c_im'] = _unpack_cdb(g_cdb)
    g_are, g_aim, g_ldt, g_bre_t, g_bim_t = s5_prep_bwd(*s5['prep_in'], g_ab.reshape(2, S5_GROUPS, S5_STATE),
                                                       _unpack_bdb(g_bdb), "s5_prep_bwd" + tag)
    g['s5_a_re'], g['s5_a_im'], g['s5_log_dt'] = g_are, g_aim, g_ldt
    g['s5_b_re'] = jnp.transpose(g_bre_t, (1, 2, 0))
    g['s5_b_im'] = jnp.transpose(g_bim_t, (1, 2, 0))
    gproj = jnp.concatenate([g_u, g_sg, g_lx, g_lg, ggsl], axis=1)
    g['w_in'] = mm_tn(sv['h_bf'], gproj, "gw_in" + tag)
    gh = mm_nt(gproj, w['w_in'], "g_h" + tag)
    gx, g_gpre = rms_bwd(gh, sv['x'], w['g_pre'][None, :], gres, "rms_bwd" + tag)
    g['g_pre'] = g_gpre[0]
    return gx, g


def _as_2d(a):
    return a.reshape((-1, a.shape[-1])) if a.ndim > 1 else a.reshape((1, -1))


def _adamw_update(w, gv, m, v):
    nm = ADAM_B1 * m + (1.0 - ADAM_B1) * gv
    nv = ADAM_B2 * v + (1.0 - ADAM_B2) * (gv * gv)
    bc1 = 1.0 - ADAM_B1 ** ADAM_STEP
    bc2 = 1.0 - ADAM_B2 ** ADAM_STEP
    return -ADAM_LR * ((nm / bc1) / (jnp.sqrt(nv / bc2) + ADAM_EPS) + ADAM_WD * w), nm, nv


def adamw(w, g, m, v, name):
    shape = w.shape
    w2, g2, m2, v2 = _as_2d(w), _as_2d(g), _as_2d(m), _as_2d(v)
    R, C = w2.shape
    tr = _row_tile(R, C)

    def body(w_ref, g_ref, m_ref, v_ref, d_ref, nm_ref, nv_ref):
        d_ref[...], nm_ref[...], nv_ref[...] = _adamw_update(w_ref[...], g_ref[...], m_ref[...], v_ref[...])

    spec = lambda: pl.BlockSpec((tr, C), lambda i: (i, 0))
    d, nm, nv = pl.pallas_call(
        body, name=name, grid=(R // tr,), in_specs=[spec() for _ in range(4)], out_specs=[spec() for _ in range(3)],
        out_shape=[_S((R, C))] * 3, compiler_params=_cp("parallel"),
    )(w2, g2, m2, v2)
    return d.reshape(shape), nm.reshape(shape), nv.reshape(shape)


def adamw_reduce(w, a, theirs, m, v, chip, name):
    shape = w.shape
    w2, m2, v2 = _as_2d(w), _as_2d(m), _as_2d(v)
    R, C = w2.shape
    a3, t3 = a.reshape(4, R, C), theirs.reshape(3, R, C)
    tr = _row_tile(R, C)

    def body(chip_ref, w_ref, a_ref, t_ref, m_ref, v_ref, g_ref, d_ref, nm_ref, nv_ref):
        gv = ((a_ref[0] + t_ref[0]) + t_ref[1]) + t_ref[2]
        g_ref[...] = gv
        d_ref[...], nm_ref[...], nv_ref[...] = _adamw_update(w_ref[...], gv, m_ref[...], v_ref[...])

    spec = lambda: pl.BlockSpec((tr, C), lambda i, c: (i, 0))
    grid_spec = pltpu.PrefetchScalarGridSpec(
        num_scalar_prefetch=1, grid=(R // tr,),
        in_specs=[spec(), pl.BlockSpec((1, tr, C), lambda i, c: (c[0], i, 0)), pl.BlockSpec((3, tr, C), lambda i, c: (0, i, 0)),
                  spec(), spec()],
        out_specs=[spec() for _ in range(4)])
    g, d, nm, nv = pl.pallas_call(
        body, name=name, grid_spec=grid_spec, out_shape=[_S((R, C))] * 4, compiler_params=_cp("parallel"),
    )(chip, w2, a3, t3, m2, v2)
    return g.reshape(shape), d.reshape(shape), nm.reshape(shape), nv.reshape(shape)


MESH = pl.DeviceIdType.MESH
ANY = pl.BlockSpec(memory_space=pl.ANY)


def _place():
    return lax.axis_index("x"), lax.axis_index("y"), lax.axis_index("c")


def _other_chips(mx, my):
    return [(1 - mx, my), (mx, 1 - my), (1 - mx, 1 - my)]


def all_gather(bufs, name):
    nb = len(bufs)

    def body(*refs):
        out_refs, send_sems, recv_sems = refs[nb:2 * nb], refs[2 * nb], refs[2 * nb + 1]
        mx, my, mc = _place()
        me, sibling = (mx, my, mc), (mx, my, 1 - mc)
        chips = _other_chips(mx, my)

        def copy(b, k, block, to):
            px, py, pc = block
            rows = out_refs[b].at[4 * px + 2 * py + pc]
            return pltpu.make_async_remote_copy(
                src_ref=rows, dst_ref=rows, send_sem=send_sems.at[b, k], recv_sem=recv_sems.at[b, k],
                device_id=to, device_id_type=MESH)

        first = []
        for b in range(nb):
            first.append(copy(b, 0, me, sibling))
            first += [copy(b, 1 + j, me, (*chip, mc)) for j, chip in enumerate(chips)]
        for cp in first:
            cp.start()
        passed = []
        for j, chip in enumerate(chips):
            for b in range(nb):
                copy(b, 1 + j, (*chip, mc), me).wait_recv()
                passed.append(copy(b, 4 + j, (*chip, mc), sibling))
                passed[-1].start()
        for b in range(nb):
            copy(b, 0, sibling, me).wait_recv()
            for j, chip in enumerate(chips):
                copy(b, 4 + j, (*chip, 1 - mc), me).wait_recv()
        for cp in first + passed:
            cp.wait_send()

    outs = pl.pallas_call(
        body, name=name, out_shape=[_S(b.shape, b.dtype) for b in bufs], in_specs=[ANY] * nb, out_specs=[ANY] * nb,
        input_output_aliases={i: i for i in range(nb)},
        scratch_shapes=[pltpu.SemaphoreType.DMA((nb, 7)), pltpu.SemaphoreType.DMA((nb, 7))],
    )(*bufs)
    return list(outs)


def own_block_in_place(shard, me):
    buf = lax.empty((N_DEV,) + shard.shape, shard.dtype)
    return lax.dynamic_update_slice(buf, shard[None], (me,) + (0,) * shard.ndim)


def exchange_sibling(gs, name):
    nb = len(gs)

    def body(*refs):
        g_refs, recv_refs, send_sems, recv_sems = refs[:nb], refs[nb:2 * nb], refs[2 * nb], refs[2 * nb + 1]
        mx, my, mc = _place()
        copies = [pltpu.make_async_remote_copy(
            src_ref=g_refs[b].at[2 * k + 1 - mc], dst_ref=recv_refs[b].at[k], send_sem=send_sems.at[b, k],
            recv_sem=recv_sems.at[b, k], device_id=(mx, my, 1 - mc), device_id_type=MESH)
            for b in range(nb) for k in range(4)]
        for cp in copies:
            cp.start()
        for cp in copies:
            cp.wait()

    outs = pl.pallas_call(
        body, name=name, out_shape=[_S((4,) + g.shape[1:], g.dtype) for g in gs], in_specs=[ANY] * nb,
        out_specs=[ANY] * nb,
        scratch_shapes=[pltpu.SemaphoreType.DMA((nb, 4)), pltpu.SemaphoreType.DMA((nb, 4))],
    )(*gs)
    return list(outs)


def exchange_chips(parts, name):
    nb = len(parts)

    def body(*refs):
        a_refs, recv_refs, send_sems, recv_sems = refs[:nb], refs[nb:2 * nb], refs[2 * nb], refs[2 * nb + 1]
        mx, my, mc = _place()
        copies = [pltpu.make_async_remote_copy(
            src_ref=a_refs[b].at[2 * px + py], dst_ref=recv_refs[b].at[j], send_sem=send_sems.at[b, j],
            recv_sem=recv_sems.at[b, j], device_id=(px, py, mc), device_id_type=MESH)
            for b in range(nb) for j, (px, py) in enumerate(_other_chips(mx, my))]
        for cp in copies:
            cp.start()
        for cp in copies:
            cp.wait()

    outs = pl.pallas_call(
        body, name=name, out_shape=[_S((3,) + a.shape[1:], a.dtype) for a in parts], in_specs=[ANY] * nb,
        out_specs=[ANY] * nb,
        scratch_shapes=[pltpu.SemaphoreType.DMA((nb, 3)), pltpu.SemaphoreType.DMA((nb, 3))],
    )(*parts)
    return list(outs)


def add_sibling(g, theirs, core, name):
    shp = theirs.shape
    C = shp[-1]
    R = math.prod(shp[1:-1])
    tr = _row_tile(R, C)

    def body(core_ref, g_ref, t_ref, o_ref):
        o_ref[...] = g_ref[...] + t_ref[...]

    grid_spec = pltpu.PrefetchScalarGridSpec(
        num_scalar_prefetch=1, grid=(4, R // tr),
        in_specs=[pl.BlockSpec((1, tr, C), lambda k, i, c: (2 * k + c[0], i, 0)),
                  pl.BlockSpec((1, tr, C), lambda k, i, c: (k, i, 0))],
        out_specs=pl.BlockSpec((1, tr, C), lambda k, i, c: (k, i, 0)))
    out = pl.pallas_call(
        body, name=name, grid_spec=grid_spec, out_shape=_S((4, R, C), g.dtype),
        compiler_params=_cp("parallel", "parallel"),
    )(core, g.reshape(N_DEV, R, C), theirs.reshape(4, R, C))
    return out.reshape(shp)


def add_chips(a, theirs, chip, name):
    _, R, C = a.shape
    tr = _row_tile(R, C)

    def body(chip_ref, a_ref, t_ref, out_ref):
        out_ref[...] = ((a_ref[0] + t_ref[0]) + t_ref[1]) + t_ref[2]

    grid_spec = pltpu.PrefetchScalarGridSpec(
        num_scalar_prefetch=1, grid=(R // tr,),
        in_specs=[pl.BlockSpec((1, tr, C), lambda i, c: (c[0], i, 0)), pl.BlockSpec((3, tr, C), lambda i, c: (0, i, 0))],
        out_specs=pl.BlockSpec((tr, C), lambda i, c: (i, 0)))
    return pl.pallas_call(
        body, name=name, grid_spec=grid_spec, out_shape=_S((R, C), a.dtype), compiler_params=_cp("parallel"),
    )(chip, a, theirs)


def _round_up(n, q):
    return (n + q - 1) // q * q


def _full_to_shards(full, axis):
    shp = full.shape
    s = shp[axis] // N_DEV
    cut = full.reshape(shp[:axis] + (N_DEV, s) + shp[axis + 1:])
    return jnp.moveaxis(cut, axis, 0)


def _shards_to_full(parts, axis):
    shp = list(parts.shape[1:])
    shp[axis] *= N_DEV
    return jnp.moveaxis(parts, 0, axis).reshape(tuple(shp))


def kernel(x, p, g_pre, w_in, s5_a_re, s5_a_im, s5_log_dt, s5_b_re, s5_b_im, s5_c_re, s5_c_im, s5_d, w_glu, w_bs, conv_w, conv_b, lru_w_a, lru_b_a, lru_w_x, lru_b_x, lru_lambda, w_bl, w_out, g_post, w_ple, w_ple_gate, loss_target, m_g_pre, m_w_in, m_s5_a_re, m_s5_a_im, m_s5_log_dt, m_s5_b_re, m_s5_b_im, m_s5_c_re, m_s5_c_im, m_s5_d, m_w_glu, m_w_bs, m_conv_w, m_conv_b, m_lru_w_a, m_lru_b_a, m_lru_w_x, m_lru_b_x, m_lru_lambda, m_w_bl, m_w_out, m_g_post, m_w_ple, m_w_ple_gate, v_g_pre, v_w_in, v_s5_a_re, v_s5_a_im, v_s5_log_dt, v_s5_b_re, v_s5_b_im, v_s5_c_re, v_s5_c_im, v_s5_d, v_w_glu, v_w_bs, v_conv_w, v_conv_b, v_lru_w_a, v_lru_b_a, v_lru_w_x, v_lru_b_x, v_lru_lambda, v_w_bl, v_w_out, v_g_post, v_w_ple, v_w_ple_gate):
    given = dict(locals())
    W = {n: given[n] for n in WEIGHTS}
    M = {n: given["m_" + n] for n in WEIGHTS}
    V = {n: given["v_" + n] for n in WEIGHTS}
    xs, target = to_scan_order(x[0]), to_scan_order(loss_target[0])
    ps = [to_scan_order(p[i, 0]) for i in range(DEPTH)]

    mx, my, mc = _place()
    me = 4 * mx + 2 * my + mc
    core = jnp.reshape(mc, (1,)).astype(jnp.int32)
    chip = jnp.reshape(2 * mx + my, (1,)).astype(jnp.int32)

    names = list(SHARDED)
    shards = [W[n].astype(bf16) if n in GATHER_BF16 else W[n] for n in names]
    gathered = all_gather([own_block_in_place(s, me) for s in shards], "comm_gather_weights")
    full = {n: _shards_to_full(g, SHARDED[n]) for n, g in zip(names, gathered)}

    def layer_weights(i):
        return {n: (full[n][i] if n in SHARDED else W[n][i]) for n in WEIGHTS}

    act, saved = xs, []
    for i in range(DEPTH):
        act, sv = layer_fwd(act, ps[i], layer_weights(i), "_l%d" % i)
        saved.append(sv)
    loss_part, gact = loss_head(act, target, "loss_head")
    grads = [None] * DEPTH
    for i in reversed(range(DEPTH)):
        gact, grads[i] = layer_bwd(gact, layer_weights(i), saved[i], "_l%d" % i)
    loss = lax.psum(loss_part[0, 0], ("x", "y", "c"))
    gfull = {n: jnp.stack([grads[i][n].reshape(full[n].shape[1:] if n in SHARDED else W[n].shape[1:])
                           for i in range(DEPTH)]) for n in WEIGHTS}

    rep = jnp.concatenate([gfull[n].reshape(-1) for n in REPLICATED])
    n_rep = _round_up(rep.shape[0], N_DEV * SUBLANES * LANES)
    rep_blocks = jnp.pad(rep, (0, n_rep - rep.shape[0])).reshape(N_DEV, -1, LANES)
    blocks = [_full_to_shards(gfull[n], SHARDED[n]) for n in names] + [rep_blocks]
    theirs = exchange_sibling(blocks, "comm_reduce_sibling")
    parts = [add_sibling(b, t, core, "reduce_add_sibling_%d" % k) for k, (b, t) in enumerate(zip(blocks, theirs))]
    others = exchange_chips(parts, "comm_reduce_chips")

    red, deltas, new_m, new_v = {}, {}, {}, {}
    for n, a, t in zip(names, parts, others):
        red[n], deltas[n], new_m[n], new_v[n] = adamw_reduce(W[n], a, t, M[n], V[n], chip, "adamw_" + n)
    piece = add_chips(parts[-1], others[-1], chip, "reduce_add_chips")
    rep_all = all_gather([own_block_in_place(piece, me)], "comm_gather_replicated")[0].reshape(-1)
    off = 0
    for n in REPLICATED:
        k = math.prod(W[n].shape)
        red[n] = rep_all[off:off + k].reshape(W[n].shape)
        off += k
        deltas[n], new_m[n], new_v[n] = adamw(W[n], red[n], M[n], V[n], "adamw_" + n)
    return (loss, from_scan_order(gact)[None], *[red[n] for n in WEIGHTS], *[deltas[n] for n in WEIGHTS],
            *[new_m[n] for n in WEIGHTS], *[new_v[n] for n in WEIGHTS])
```

```python
import math

import jax
import jax.numpy as jnp
from jax import lax
from jax.experimental import pallas as pl
from jax.experimental.pallas import tpu as pltpu

f32 = jnp.float32
bf16 = jnp.bfloat16

D_MODEL = 1024
DEPTH = 2
PLE_DIM = 256
NORM_EPS = 1e-6
S5_WIDTH = 512
S5_GROUP = 16
S5_GROUPS = 32
S5_STATE = 64
S5_NS = S5_GROUPS * S5_STATE
LRU_WIDTH = 1280
LRU_HEADS = 10
LRU_HEAD_DIM = 128
LRU_C = 8.0
CONV_WIDTH = 4
N_DEV = 8

ADAM_LR = 0.001
ADAM_B1 = 0.9
ADAM_B2 = 0.999
ADAM_EPS = 1e-08
ADAM_WD = 0.01
ADAM_STEP = 10

LANES = 128
SUBLANES = 8
SEGS = SUBLANES
SCAN_CHUNK = 256
TOKEN_TILE = 256
MM_TILE_M = 1024
MM_TILE_N = 1408
MM_TILE_K_ROWS = 1280
PAIR = 2 * SUBLANES
VMEM_LIMIT_BYTES = 56 * 1024 * 1024
ELEMENTWISE_BLOCK_BYTES = 1024 * 1024

WEIGHTS = ['g_pre', 'w_in', 's5_a_re', 's5_a_im', 's5_log_dt', 's5_b_re', 's5_b_im', 's5_c_re', 's5_c_im',
           's5_d', 'w_glu', 'w_bs', 'conv_w', 'conv_b', 'lru_w_a', 'lru_b_a', 'lru_w_x', 'lru_b_x',
           'lru_lambda', 'w_bl', 'w_out', 'g_post', 'w_ple', 'w_ple_gate']
SHARDED = {'w_in': 2, 'w_glu': 2, 'w_bs': 2, 'conv_w': 2, 'w_bl': 1, 'w_out': 1, 'w_ple': 2, 'w_ple_gate': 1}
GATHER_BF16 = ['w_in', 'w_glu', 'w_bs', 'w_bl', 'w_out', 'w_ple', 'w_ple_gate']
REPLICATED = [n for n in WEIGHTS if n not in SHARDED]


def _sig(x):
    return 0.5 * jnp.tanh(0.5 * x) + 0.5


def _gelu_parts(x):
    k = math.sqrt(2.0 / math.pi)
    t = jnp.tanh(k * (x + 0.044715 * x * x * x))
    return t, k


def _gelu(x):
    t, _ = _gelu_parts(x)
    return 0.5 * x * (1.0 + t)


def _gelu_grad(x):
    t, k = _gelu_parts(x)
    return 0.5 * (1.0 + t) + 0.5 * x * (1.0 - t * t) * k * (1.0 + 3.0 * 0.044715 * x * x)


def _one_minus_exp(z):
    series = -z * (1.0 + z * (0.5 + z * (1.0 / 6.0 + z * (1.0 / 24.0 + z * (1.0 / 120.0)))))
    return jnp.where(z > -0.05, series, 1.0 - jnp.exp(z))


def _softplus_neg(lam):
    return jnp.maximum(-lam, 0.0) + jnp.log(1.0 + jnp.exp(-jnp.abs(lam)))


def _dot(a, b):
    return jnp.dot(a, b, preferred_element_type=f32)


def _dot_nt(a, b):
    return lax.dot_general(a, b, (((1,), (1,)), ((), ())), preferred_element_type=f32)


def _dot_tn(a, b):
    return lax.dot_general(a, b, (((0,), (0,)), ((), ())), preferred_element_type=f32)


def _S(shape, dtype=f32):
    return jax.ShapeDtypeStruct(shape, dtype)


def _full(shape):
    nd = len(shape)
    return pl.BlockSpec(shape, lambda *_: (0,) * nd)


def _rows(tile, width, col=0):
    return pl.BlockSpec((tile, width), lambda i: (i, col))


def _cp(*semantics):
    return pltpu.CompilerParams(dimension_semantics=semantics or None, vmem_limit_bytes=VMEM_LIMIT_BYTES)


def _tile(n, want):
    t = min(n, want)
    assert n % t == 0, (n, want)
    return t


def _row_tile(R, C=LANES):
    cap = max(SUBLANES, min(R, ELEMENTWISE_BLOCK_BYTES // (4 * C)))
    for t in range(cap - cap % SUBLANES, 0, -SUBLANES):
        if R % t == 0:
            return t
    return R


def _lanes(j):
    return slice(LANES * j, LANES * (j + 1))


def _step_rows(k, n=SUBLANES):
    return pl.ds(pl.multiple_of(k * n, n), n)


def to_scan_order(a):
    T, C = a.shape
    tc = _tile(T, SCAN_CHUNK)
    return a.reshape(T // tc, SEGS, tc // SEGS, C).transpose(0, 2, 1, 3).reshape(T, C)


def from_scan_order(a):
    T, C = a.shape
    tc = _tile(T, SCAN_CHUNK)
    return a.reshape(T // tc, tc // SEGS, SEGS, C).transpose(0, 2, 1, 3).reshape(T, C)


def _col_tile(n, cap):
    if n <= cap:
        return n
    for t in range(cap - cap % LANES, 0, -LANES):
        if n % t == 0:
            return t
    return n


def _resident(shape):
    nd = len(shape)
    return pl.BlockSpec(shape, lambda *_: (0,) * nd, pipeline_mode=pl.Buffered(1))


def mm_nn(a, b, name, out_dtype=f32):
    M, K = a.shape
    N = b.shape[1]
    tm, tn = _tile(M, MM_TILE_M), _col_tile(N, MM_TILE_N)

    def body(a_ref, b_ref, o_ref):
        o_ref[...] = _dot(a_ref[...].astype(bf16), b_ref[...].astype(bf16)).astype(out_dtype)

    return pl.pallas_call(
        body, name=name, grid=(M // tm, N // tn),
        in_specs=[pl.BlockSpec((tm, K), lambda i, j: (i, 0)), pl.BlockSpec((K, tn), lambda i, j: (0, j))],
        out_specs=pl.BlockSpec((tm, tn), lambda i, j: (i, j)),
        out_shape=_S((M, N), out_dtype), compiler_params=_cp("parallel", "parallel"),
    )(a, b)


def mm_nt(a, b, name, out_dtype=f32):
    M, K = a.shape
    N = b.shape[0]
    tm = _tile(M, MM_TILE_M // 2)

    def body(a_ref, b_ref, o_ref):
        o_ref[...] = _dot_nt(a_ref[...].astype(bf16), b_ref[...].astype(bf16)).astype(out_dtype)

    return pl.pallas_call(
        body, name=name, grid=(M // tm,),
        in_specs=[pl.BlockSpec((tm, K), lambda i: (i, 0)), _resident((N, K))],
        out_specs=pl.BlockSpec((tm, N), lambda i: (i, 0)),
        out_shape=_S((M, N), out_dtype), compiler_params=_cp("parallel"),
    )(a, b)


def mm_tn(a, b, name):
    M, K = a.shape
    N = b.shape[1]
    tm, tk, tn = _tile(M, MM_TILE_M), _col_tile(K, MM_TILE_K_ROWS), _col_tile(N, MM_TILE_N)

    def body(a_ref, b_ref, o_ref):
        m = pl.program_id(2)
        part = _dot_tn(a_ref[...].astype(bf16), b_ref[...].astype(bf16))

        @pl.when(m == 0)
        def _():
            o_ref[...] = part

        @pl.when(m > 0)
        def _():
            o_ref[...] += part

    return pl.pallas_call(
        body, name=name, grid=(K // tk, N // tn, M // tm),
        in_specs=[pl.BlockSpec((tm, tk), lambda i, j, m: (m, i)), pl.BlockSpec((tm, tn), lambda i, j, m: (m, j))],
        out_specs=pl.BlockSpec((tk, tn), lambda i, j, m: (i, j)),
        out_shape=_S((K, N), f32),
        compiler_params=_cp("parallel", "parallel", "arbitrary"),
    )(a, b)


def rms_fwd(x, g, name):
    T = x.shape[0]
    tm = _tile(T, TOKEN_TILE)

    def body(x_ref, g_ref, h_ref):
        xv = x_ref[...]
        r = lax.rsqrt(jnp.mean(xv * xv, axis=-1, keepdims=True) + NORM_EPS)
        h_ref[...] = (xv * r * g_ref[...]).astype(bf16)

    return pl.pallas_call(
        body, name=name, grid=(T // tm,),
        in_specs=[_rows(tm, D_MODEL), _full((1, D_MODEL))], out_specs=_rows(tm, D_MODEL),
        out_shape=_S((T, D_MODEL), bf16), compiler_params=_cp("parallel"),
    )(x, g)


def rms_bwd(gh, x, g, gres, name):
    T = x.shape[0]
    tm = _tile(T, TOKEN_TILE)

    def body(gh_ref, x_ref, g_ref, gres_ref, gx_ref, gg_ref):
        i = pl.program_id(0)
        xv = x_ref[...]
        ghv = gh_ref[...]
        r = lax.rsqrt(jnp.mean(xv * xv, axis=-1, keepdims=True) + NORM_EPS)
        nrm = xv * r
        gy = ghv * g_ref[...]
        gx_ref[...] = gres_ref[...] + r * (gy - nrm * jnp.mean(gy * nrm, axis=-1, keepdims=True))
        part = jnp.sum(ghv * nrm, axis=0, keepdims=True)

        @pl.when(i == 0)
        def _():
            gg_ref[...] = part

        @pl.when(i > 0)
        def _():
            gg_ref[...] += part

    return pl.pallas_call(
        body, name=name, grid=(T // tm,),
        in_specs=[_rows(tm, D_MODEL), _rows(tm, D_MODEL), _full((1, D_MODEL)), _rows(tm, D_MODEL)],
        out_specs=[_rows(tm, D_MODEL), _full((1, D_MODEL))],
        out_shape=[_S((T, D_MODEL)), _S((1, D_MODEL))], compiler_params=_cp("arbitrary"),
    )(gh, x, g, gres)


def _s5_discretise(a_re, a_im, log_dt, b_re_t, b_im_t):
    dt = jnp.exp(log_dt)
    mag = jnp.exp(a_re * dt)
    ab_re = mag * jnp.cos(a_im * dt)
    ab_im = mag * jnp.sin(a_im * dt)
    den = a_re * a_re + a_im * a_im
    nr, ni = ab_re - 1.0, ab_im
    z_re = (nr * a_re + ni * a_im) / den
    z_im = (ni * a_re - nr * a_im) / den
    bb_re = z_re[None] * b_re_t - z_im[None] * b_im_t
    bb_im = z_re[None] * b_im_t + z_im[None] * b_re_t
    return ab_re, ab_im, bb_re, bb_im


def s5_prep(a_re, a_im, log_dt, b_re_t, b_im_t, m, name):
    G, N = a_re.shape

    def body(are_ref, aim_ref, ldt_ref, bre_ref, bim_ref, ab_ref, pw_ref, bb_ref):
        are, aim, ldt = are_ref[...], aim_ref[...], ldt_ref[...]
        ab_re, ab_im, bb_re, bb_im = _s5_discretise(are, aim, ldt, bre_ref[...], bim_ref[...])
        ab_ref[0], ab_ref[1] = ab_re, ab_im
        bb_ref[0], bb_ref[1] = bb_re, bb_im
        dt = jnp.exp(ldt)
        for k in range(m):
            mag = jnp.exp(are * dt * (k + 1.0))
            pw_ref[0, k] = mag * jnp.cos(aim * dt * (k + 1.0))
            pw_ref[1, k] = mag * jnp.sin(aim * dt * (k + 1.0))

    return pl.pallas_call(
        body, name=name,
        out_shape=[_S((2, G, N)), _S((2, m, G, N)), _S((2, S5_GROUP, G, N))], compiler_params=_cp(),
    )(a_re, a_im, log_dt, b_re_t, b_im_t)


def s5_prep_bwd(a_re, a_im, log_dt, b_re_t, b_im_t, g_ab, g_bb, name):
    G, N = a_re.shape

    def body(are_ref, aim_ref, ldt_ref, bre_ref, bim_ref, gab_ref, gbb_ref, o_are, o_aim, o_ldt, o_bre, o_bim):
        _, vjp = jax.vjp(_s5_discretise, are_ref[...], aim_ref[...], ldt_ref[...], bre_ref[...], bim_ref[...])
        g_are, g_aim, g_ldt, g_bre, g_bim = vjp((gab_ref[0], gab_ref[1], gbb_ref[0], gbb_ref[1]))
        o_are[...], o_aim[...], o_ldt[...], o_bre[...], o_bim[...] = g_are, g_aim, g_ldt, g_bre, g_bim

    return pl.pallas_call(
        body, name=name,
        out_shape=[_S((G, N)), _S((G, N)), _S((G, 1)), _S((S5_GROUP, G, N)), _S((S5_GROUP, G, N))],
        compiler_params=_cp(),
    )(a_re, a_im, log_dt, b_re_t, b_im_t, g_ab, g_bb)


NB_S5 = S5_NS // LANES
CB_S5 = S5_WIDTH // LANES
SB_PER_CB = NB_S5 // CB_S5
GRP_PER_SB = LANES // S5_STATE
S5_JB = 8


def _bdb_mask():
    j = jnp.arange(NB_S5)
    own_rows = (j[:, None] % SB_PER_CB == jnp.arange(SB_PER_CB)[None, :]).astype(f32)
    eye = jnp.eye(GRP_PER_SB, dtype=f32)
    return own_rows[:, :, None, None, None, None, None] * eye[None, None, :, None, None, :, None]


def _pack_bdb(bb):
    v = jnp.transpose(bb.reshape(2, S5_GROUP, NB_S5, GRP_PER_SB, S5_STATE), (2, 3, 1, 0, 4))
    full = v[:, None, :, :, :, None, :] * _bdb_mask()
    return full.reshape(NB_S5, LANES, 2 * LANES)


def _unpack_bdb(g_bdb):
    g7 = g_bdb.reshape(NB_S5, SB_PER_CB, GRP_PER_SB, S5_GROUP, 2, GRP_PER_SB, S5_STATE)
    v = jnp.sum(g7 * _bdb_mask(), axis=(1, 5))
    return jnp.transpose(v, (3, 2, 0, 1, 4)).reshape(2, S5_GROUP, S5_GROUPS, S5_STATE)


def _pack_cdb(c_re, c_im):
    gl = S5_GROUPS // CB_S5
    c2 = jnp.stack([c_re, -c_im]).reshape(2, CB_S5, gl, S5_GROUP, S5_STATE)
    eye = jnp.eye(gl, dtype=f32)
    full = jnp.transpose(c2, (1, 0, 2, 4, 3))[:, :, :, :, None, :] * eye[None, None, :, None, :, None]
    return full.reshape(CB_S5, 2 * SB_PER_CB * LANES, LANES)


def _unpack_cdb(g_cdb):
    gl = S5_GROUPS // CB_S5
    g6 = g_cdb.reshape(CB_S5, 2, gl, S5_STATE, gl, S5_GROUP)
    eye = jnp.eye(gl, dtype=f32)
    v = jnp.sum(g6 * eye[None, None, :, None, :, None], axis=4)
    v = jnp.transpose(v, (1, 0, 2, 4, 3)).reshape(2, S5_GROUPS, S5_GROUP, S5_STATE)
    return v[0], -v[1]


def _state_cat(ref, c):
    w = SB_PER_CB * LANES
    return jnp.concatenate([ref[:, w * c:w * (c + 1)], ref[:, S5_NS + w * c:S5_NS + w * (c + 1)]], axis=1)


def _state_pair(ref, j):
    return jnp.concatenate([ref[:, _lanes(j)], ref[:, S5_NS + LANES * j:S5_NS + LANES * (j + 1)]], axis=1)


def s5_fwd(usg, bdb, cdb, dvec, abar_b, ptab_b, name):
    T = usg.shape[0]
    tc = _tile(T, SCAN_CHUNK)
    m = tc // SEGS
    assert ptab_b.shape == (2, m, SEGS, S5_NS) and m % 2 == 0

    def body(u_ref, bdb_ref, cdb_ref, d_ref, a_ref, p_ref, ys_ref, sre_ref, sim_ref, sbf_ref,
             src_re, src_im, dst_re, dst_im, cin_ref, carry_ref):
        i = pl.program_id(0)

        @pl.when(i == 0)
        def _():
            carry_ref[...] = jnp.zeros_like(carry_ref)

        u = u_ref[...]
        ub = u.astype(bf16)
        for j in range(NB_S5):
            bu = _dot(ub[:, _lanes(j // SB_PER_CB)], bdb_ref[j])
            src_re[:, _lanes(j)] = bu[:, :LANES]
            src_im[:, _lanes(j)] = bu[:, LANES:]
        for j0 in range(0, NB_S5, S5_JB):
            def kstep(k, st):
                rows = _step_rows(k)
                out = []
                for q in range(S5_JB):
                    ln = _lanes(j0 + q)
                    sr, si = st[2 * q], st[2 * q + 1]
                    ar, ai = a_ref[0, :, ln], a_ref[1, :, ln]
                    nr = ar * sr - ai * si + src_re[rows, ln]
                    ni = ar * si + ai * sr + src_im[rows, ln]
                    dst_re[rows, ln] = nr
                    dst_im[rows, ln] = ni
                    out += [nr, ni]
                return tuple(out)

            ends = lax.fori_loop(0, m, kstep, tuple(jnp.zeros((SEGS, LANES), f32) for _ in range(2 * S5_JB)))
            for q in range(S5_JB):
                ln = _lanes(j0 + q)
                er, ei = ends[2 * q], ends[2 * q + 1]
                cr, ci = carry_ref[0, :, ln], carry_ref[1, :, ln]
                amr, ami = p_ref[0, m - 1, 0:1, ln], p_ref[1, m - 1, 0:1, ln]
                rows_r, rows_i = [], []
                for s in range(SEGS):
                    rows_r.append(cr)
                    rows_i.append(ci)
                    cr, ci = (er[s:s + 1, :] + amr * cr - ami * ci, ei[s:s + 1, :] + amr * ci + ami * cr)
                cin_ref[0, 0:SEGS, ln] = _stack_rows(rows_r)
                cin_ref[1, 0:SEGS, ln] = _stack_rows(rows_i)
                carry_ref[0, :, ln] = cr
                carry_ref[1, :, ln] = ci
        cin_ref[:, SEGS:, :] = cin_ref[:, 0:SEGS, :]

        def fix(k2, _):
            rows = _step_rows(k2, PAIR)
            pr = p_ref[0, pl.ds(2 * k2, 2)].reshape(PAIR, S5_NS)
            pi = p_ref[1, pl.ds(2 * k2, 2)].reshape(PAIR, S5_NS)
            cr, ci = cin_ref[0], cin_ref[1]
            sr = dst_re[rows, :] + pr * cr - pi * ci
            si = dst_im[rows, :] + pr * ci + pi * cr
            sre_ref[rows, :] = sr
            sim_ref[rows, :] = si
            sbf_ref[rows, 0:S5_NS] = sr.astype(bf16)
            sbf_ref[rows, S5_NS:] = si.astype(bf16)
            return 0

        lax.fori_loop(0, m // 2, fix, 0)
        for c in range(CB_S5):
            ys_ref[:, _lanes(c)] = _dot(_state_cat(sbf_ref, c), cdb_ref[c]) + d_ref[:, _lanes(c)] * u[:, _lanes(c)]

    st = lambda w: _rows(tc, w)
    return pl.pallas_call(
        body, name=name, grid=(T // tc,),
        in_specs=[_rows(tc, S5_WIDTH, 0), _resident(bdb.shape), _resident(cdb.shape), _full((1, S5_WIDTH)),
                  _resident((2, SEGS, S5_NS)), _resident((2, m, SEGS, S5_NS))],
        out_specs=[st(S5_WIDTH), st(S5_NS), st(S5_NS), st(2 * S5_NS)],
        out_shape=[_S((T, S5_WIDTH)), _S((T, S5_NS)), _S((T, S5_NS)), _S((T, 2 * S5_NS), bf16)],
        scratch_shapes=[pltpu.VMEM((tc, S5_NS), f32)] * 4 + [pltpu.VMEM((2, PAIR, S5_NS), f32),
                                                             pltpu.VMEM((2, 1, S5_NS), f32)],
        compiler_params=_cp("arbitrary"),
    )(usg, bdb, cdb, dvec, abar_b, ptab_b)


def s5_bwd(gys, usg, s_re, s_im, s_bf, bdb, cdb, dvec, abar_b, ptab_rev_b, name):
    T = gys.shape[0]
    tc = _tile(T, SCAN_CHUNK)
    m = tc // SEGS
    nch = T // tc
    hb = tc // SUBLANES

    def body(gy_ref, u_ref, sre_ref, sim_ref, hre_ref, him_ref, sbf_ref, bdb_ref, cdb_ref, d_ref, a_ref, p_ref,
             gu_ref, gab_ref, gd_ref, gbdb_ref, gcdb_ref,
             src_re, src_im, dst_re, dst_im, lam_ref, cin_ref, acc_ref, carry_ref):
        i = pl.program_id(0)

        @pl.when(i == 0)
        def _():
            carry_ref[...] = jnp.zeros_like(carry_ref)
            for ref in (gab_ref, gd_ref, gbdb_ref, gcdb_ref):
                ref[...] = jnp.zeros_like(ref)

        first = i == nch - 1
        gy = gy_ref[...]
        gyb = gy.astype(bf16)
        u = u_ref[...]
        ub = u.astype(bf16)
        w = SB_PER_CB * LANES
        for c in range(CB_S5):
            gs = _dot_nt(gyb[:, _lanes(c)], cdb_ref[c])
            src_re[:, w * c:w * (c + 1)] = gs[:, :w]
            src_im[:, w * c:w * (c + 1)] = gs[:, w:]
            gcdb_ref[c] += _dot_tn(_state_cat(sbf_ref, c), gyb[:, _lanes(c)])
        for j0 in range(0, NB_S5, S5_JB):
            def kstep(kk, st):
                rows = _step_rows(m - 1 - kk)
                out = []
                for q in range(S5_JB):
                    ln = _lanes(j0 + q)
                    lr, li = st[2 * q], st[2 * q + 1]
                    ar, ai = a_ref[0, :, ln], a_ref[1, :, ln]
                    nr = ar * lr + ai * li + src_re[rows, ln]
                    ni = ar * li - ai * lr + src_im[rows, ln]
                    dst_re[rows, ln] = nr
                    dst_im[rows, ln] = ni
                    out += [nr, ni]
                return tuple(out)

            ends = lax.fori_loop(0, m, kstep, tuple(jnp.zeros((SEGS, LANES), f32) for _ in range(2 * S5_JB)))
            for q in range(S5_JB):
                ln = _lanes(j0 + q)
                er, ei = ends[2 * q], ends[2 * q + 1]
                cr, ci = carry_ref[0, :, ln], carry_ref[1, :, ln]
                amr, ami = p_ref[0, 0, 0:1, ln], p_ref[1, 0, 0:1, ln]
                rows_r, rows_i = [None] * SEGS, [None] * SEGS
                for s in reversed(range(SEGS)):
                    rows_r[s], rows_i[s] = cr, ci
                    cr, ci = (er[s:s + 1, :] + amr * cr + ami * ci, ei[s:s + 1, :] + amr * ci - ami * cr)
                cin_ref[0, 0:SEGS, ln] = _stack_rows(rows_r)
                cin_ref[1, 0:SEGS, ln] = _stack_rows(rows_i)
                carry_ref[0, :, ln] = cr
                carry_ref[1, :, ln] = ci
        cin_ref[:, SEGS:, :] = cin_ref[:, 0:SEGS, :]
        acc_ref[...] = jnp.zeros_like(acc_ref)

        def fix_rows(rows, k2, prev_re, prev_im):
            pr = p_ref[0, pl.ds(2 * k2, 2)].reshape(PAIR, S5_NS)
            pi = p_ref[1, pl.ds(2 * k2, 2)].reshape(PAIR, S5_NS)
            cr, ci = cin_ref[0], cin_ref[1]
            lr = dst_re[rows, :] + pr * cr + pi * ci
            li = dst_im[rows, :] + pr * ci - pi * cr
            lam_ref[rows, 0:S5_NS] = lr.astype(bf16)
            lam_ref[rows, S5_NS:] = li.astype(bf16)
            acc_ref[0] += lr * prev_re + li * prev_im
            acc_ref[1] += li * prev_re - lr * prev_im

        last = slice(tc - SUBLANES, tc)
        wrap_re = _down_a_segment(sre_ref[last, :], jnp.where(first, 0.0, hre_ref[SUBLANES - 1:SUBLANES, :]))
        wrap_im = _down_a_segment(sim_ref[last, :], jnp.where(first, 0.0, him_ref[SUBLANES - 1:SUBLANES, :]))
        fix_rows(pl.ds(0, PAIR), 0, jnp.concatenate([wrap_re, sre_ref[0:SUBLANES, :]], axis=0),
                 jnp.concatenate([wrap_im, sim_ref[0:SUBLANES, :]], axis=0))

        def fix(k2, _):
            prev = pl.ds(pl.multiple_of(k2 * PAIR - SUBLANES, SUBLANES), PAIR)
            fix_rows(_step_rows(k2, PAIR), k2, sre_ref[prev, :], sim_ref[prev, :])
            return 0

        lax.fori_loop(1, m // 2, fix, 0)
        gab_ref[0] += jnp.sum(acc_ref[0], axis=0, keepdims=True)
        gab_ref[1] += jnp.sum(acc_ref[1], axis=0, keepdims=True)
        for c in range(CB_S5):
            x = gy[:, _lanes(c)] * d_ref[:, _lanes(c)]
            for j in range(SB_PER_CB * c, SB_PER_CB * (c + 1)):
                pair = _state_pair(lam_ref, j)
                x = x + _dot_nt(pair, bdb_ref[j])
                gbdb_ref[j] += _dot_tn(ub[:, _lanes(c)], pair)
            gu_ref[:, _lanes(c)] = x.astype(bf16)
        gd_ref[...] += jnp.sum(gy * u, axis=0, keepdims=True)

    rev = lambda i: (nch - 1 - i, 0)
    halo = lambda i: (jnp.maximum((nch - 1 - i) * hb - 1, 0), 0)
    blk = lambda wd: pl.BlockSpec((tc, wd), rev)
    return pl.pallas_call(
        body, name=name, grid=(nch,),
        in_specs=[blk(S5_WIDTH), blk(S5_WIDTH), blk(S5_NS), blk(S5_NS),
                  pl.BlockSpec((SUBLANES, S5_NS), halo), pl.BlockSpec((SUBLANES, S5_NS), halo), blk(2 * S5_NS),
                  _resident(bdb.shape), _resident(cdb.shape), _full((1, S5_WIDTH)),
                  _resident((2, SEGS, S5_NS)), _resident((2, m, SEGS, S5_NS))],
        out_specs=[blk(S5_WIDTH), _full((2, 1, S5_NS)), _full((1, S5_WIDTH)), _full(bdb.shape), _full(cdb.shape)],
        out_shape=[_S((T, S5_WIDTH), bf16), _S((2, 1, S5_NS)), _S((1, S5_WIDTH)), _S(bdb.shape), _S(cdb.shape)],
        scratch_shapes=[pltpu.VMEM((tc, S5_NS), f32)] * 4 + [
            pltpu.VMEM((tc, 2 * S5_NS), bf16), pltpu.VMEM((2, PAIR, S5_NS), f32), pltpu.VMEM((2, PAIR, S5_NS), f32),
            pltpu.VMEM((2, 1, S5_NS), f32)],
        compiler_params=_cp("arbitrary"),
    )(gys, usg, s_re, s_im, s_re, s_im, s_bf, bdb, cdb, dvec, abar_b, ptab_rev_b)


def s5_post_fwd(ys, usg, wglu, wbs, name):
    T = ys.shape[0]
    tm = _tile(T, TOKEN_TILE)

    def body(ys_ref, sg_ref, wglu_ref, wbs_ref, glu_ref, zs_ref):
        glu = _dot(_gelu(ys_ref[...]).astype(bf16), wglu_ref[...])
        sg = sg_ref[...]
        y2 = glu[:, :S5_WIDTH] * _sig(glu[:, S5_WIDTH:]) * (sg * _sig(sg))
        glu_ref[...] = glu
        zs_ref[...] = _dot(y2.astype(bf16), wbs_ref[...])

    return pl.pallas_call(
        body, name=name, grid=(T // tm,),
        in_specs=[_rows(tm, S5_WIDTH), _rows(tm, S5_WIDTH, 1), _resident((S5_WIDTH, 2 * S5_WIDTH)),
                  _resident((S5_WIDTH, D_MODEL))],
        out_specs=[_rows(tm, 2 * S5_WIDTH), _rows(tm, D_MODEL)],
        out_shape=[_S((T, 2 * S5_WIDTH)), _S((T, D_MODEL))], compiler_params=_cp("parallel"),
    )(ys, usg, wglu, wbs)


def s5_post_bwd(gzs, glu, usg, ys, wbs, wglu, name):
    T = ys.shape[0]
    tm = _tile(T, TOKEN_TILE)

    def body(gzs_ref, glu_ref, sg_ref, ys_ref, wbs_ref, wglu_ref, y2_ref, gglu_ref, ge_ref, gys_ref, gsg_ref):
        glu = glu_ref[...]
        a, b = glu[:, :S5_WIDTH], glu[:, S5_WIDTH:]
        sg = sg_ref[...]
        ys = ys_ref[...]
        sb, ssg = _sig(b), _sig(sg)
        silu = sg * ssg
        y2_ref[...] = (a * sb * silu).astype(bf16)
        gy2 = _dot_nt(gzs_ref[...], wbs_ref[...])
        g_a = gy2 * sb * silu
        g_b = gy2 * a * sb * (1.0 - sb) * silu
        gsg_ref[...] = (gy2 * a * sb * ssg * (1.0 + sg * (1.0 - ssg))).astype(bf16)
        gglu = jnp.concatenate([g_a, g_b], axis=1).astype(bf16)
        gglu_ref[...] = gglu
        ge_ref[...] = _gelu(ys).astype(bf16)
        gys_ref[...] = _dot_nt(gglu, wglu_ref[...]) * _gelu_grad(ys)

    return pl.pallas_call(
        body, name=name, grid=(T // tm,),
        in_specs=[_rows(tm, D_MODEL), _rows(tm, 2 * S5_WIDTH), _rows(tm, S5_WIDTH, 1), _rows(tm, S5_WIDTH),
                  _resident((S5_WIDTH, D_MODEL)), _resident((S5_WIDTH, 2 * S5_WIDTH))],
        out_specs=[_rows(tm, S5_WIDTH), _rows(tm, 2 * S5_WIDTH), _rows(tm, S5_WIDTH), _rows(tm, S5_WIDTH), _rows(tm, S5_WIDTH)],
        out_shape=[_S((T, S5_WIDTH), bf16), _S((T, 2 * S5_WIDTH), bf16), _S((T, S5_WIDTH), bf16), _S((T, S5_WIDTH)),
                   _S((T, S5_WIDTH), bf16)],
        compiler_params=_cp("parallel"),
    )(gzs, glu, usg, ys, wbs, wglu)


NB_LRU = LRU_WIDTH // LANES
LRU_JB = 5
TAPS_BACK = CONV_WIDTH - 1
EDGE = TAPS_BACK * SUBLANES
HALO_ROWS = 4 * SUBLANES


def _down_a_segment(blk, entering_row):
    sub = lax.broadcasted_iota(jnp.int32, blk.shape, 0)
    return jnp.where(sub == 0, entering_row, pltpu.roll(blk, 1, 0))


def _up_a_segment(blk, entering_row):
    sub = lax.broadcasted_iota(jnp.int32, blk.shape, 0)
    return jnp.where(sub == SUBLANES - 1, entering_row, pltpu.roll(blk, SUBLANES - 1, 0))


def _stack_rows(rows):
    sub = lax.broadcasted_iota(jnp.int32, (SUBLANES,) + rows[0].shape[1:], 0)
    out = jnp.broadcast_to(rows[0], sub.shape)
    for s in range(1, SUBLANES):
        out = jnp.where(sub == s, rows[s], out)
    return out


def _fill_conv_window(xe, x_ref, xh_ref, is_first, tc):
    xe[EDGE:, :] = x_ref[...]
    for i in range(1, TAPS_BACK + 1):
        row = HALO_ROWS - SUBLANES * i + SUBLANES - 1
        entering = jnp.where(is_first, 0.0, xh_ref[row:row + 1, :])
        blk = x_ref[tc - SUBLANES * i:tc - SUBLANES * (i - 1), :]
        xe[EDGE - SUBLANES * i:EDGE - SUBLANES * (i - 1), :] = _down_a_segment(blk, entering)


def lru_fwd(lx, convw, convb, wa, wx, ba, bx, lam, name):
    T = lx.shape[0]
    tc = _tile(T, SCAN_CHUNK)
    m = tc // SEGS
    hb = tc // HALO_ROWS

    def body(x_ref, xh_ref, cw_ref, cb_ref, wa_ref, wx_ref, ba_ref, bx_ref, lam_ref,
             c_ref, r_ref, i_ref, h_ref, xe, src_a, src_b, dst_a, dst_h, cin_ref, carry_ref):
        i = pl.program_id(0)

        @pl.when(i == 0)
        def _():
            carry_ref[...] = jnp.zeros_like(carry_ref)

        _fill_conv_window(xe, x_ref, xh_ref, i == 0, tc)
        c = cb_ref[...] + cw_ref[0:1, :] * xe[0:tc, :]
        for k in range(1, CONV_WIDTH):
            c = c + cw_ref[k:k + 1, :] * xe[SUBLANES * k:SUBLANES * k + tc, :]
        c_ref[...] = c
        sp = _softplus_neg(lam_ref[...])
        for j in range(NB_LRU):
            ln = _lanes(j)
            cj = c[:, ln]
            cjb = cj.astype(bf16)
            r = _sig(_dot(cjb, wa_ref[j]) + ba_ref[:, ln])
            g = _sig(_dot(cjb, wx_ref[j]) + bx_ref[:, ln])
            r_ref[:, ln] = r
            i_ref[:, ln] = g
            log_a = -LRU_C * r * sp[:, ln]
            src_a[:, ln] = jnp.exp(log_a)
            src_b[:, ln] = jnp.sqrt(_one_minus_exp(2.0 * log_a)) * (g * cj)
        for j0 in range(0, NB_LRU, LRU_JB):
            def kstep(k, st):
                rows = _step_rows(k)
                out = []
                for q in range(LRU_JB):
                    ln = _lanes(j0 + q)
                    hh, ac = st[2 * q], st[2 * q + 1]
                    a = src_a[rows, ln]
                    hh = a * hh + src_b[rows, ln]
                    ac = a * ac
                    dst_h[rows, ln] = hh
                    dst_a[rows, ln] = ac
                    out += [hh, ac]
                return tuple(out)

            init = tuple(jnp.zeros((SEGS, LANES), f32) if q % 2 == 0 else jnp.ones((SEGS, LANES), f32)
                         for q in range(2 * LRU_JB))
            ends = lax.fori_loop(0, m, kstep, init)
            for q in range(LRU_JB):
                ln = _lanes(j0 + q)
                eh, ea = ends[2 * q], ends[2 * q + 1]
                cr = carry_ref[:, ln]
                rows_c = []
                for s in range(SEGS):
                    rows_c.append(cr)
                    cr = eh[s:s + 1, :] + ea[s:s + 1, :] * cr
                cin_ref[:, ln] = _stack_rows(rows_c)
                carry_ref[:, ln] = cr

        def fix(k, _):
            rows = _step_rows(k)
            h_ref[rows, :] = dst_h[rows, :] + dst_a[rows, :] * cin_ref[...]
            return 0

        lax.fori_loop(0, m, fix, 0)

    wide = lambda: _rows(tc, LRU_WIDTH)
    buf = lambda rows: pltpu.VMEM((rows, LRU_WIDTH), f32)
    return pl.pallas_call(
        body, name=name, grid=(T // tc,),
        in_specs=[wide(), pl.BlockSpec((HALO_ROWS, LRU_WIDTH), lambda i: (jnp.maximum(i * hb - 1, 0), 0)),
                  _full((CONV_WIDTH, LRU_WIDTH)), _full((1, LRU_WIDTH)),
                  _full((LRU_HEADS, LRU_HEAD_DIM, LRU_HEAD_DIM)), _full((LRU_HEADS, LRU_HEAD_DIM, LRU_HEAD_DIM)),
                  _full((1, LRU_WIDTH)), _full((1, LRU_WIDTH)), _full((1, LRU_WIDTH))],
        out_specs=[wide(), wide(), wide(), wide()],
        out_shape=[_S((T, LRU_WIDTH))] * 4,
        scratch_shapes=[buf(tc + EDGE), buf(tc), buf(tc), buf(tc), buf(tc), buf(SEGS), buf(1)],
        compiler_params=_cp("arbitrary"),
    )(lx, lx, convw, convb, wa, wx, ba, bx, lam)


def lru_bwd(gh, h, c, r, gi, lx, convw, wa, wx, lam, name):
    T = gh.shape[0]
    tc = _tile(T, SCAN_CHUNK)
    m = tc // SEGS
    nch = T // tc

    def body(gh_ref, h_ref, hh_ref, c_ref, r_ref, i_ref, x_ref, xh_ref, cw_ref, wa_ref, wx_ref, lam_ref,
             glx_ref, gwa_ref, gwx_ref, gba_ref, gbx_ref, glam_ref, gcb_ref, gcw_ref,
             src_a, src_m, dst_a, dst_m, mbuf, hbuf, xe, gce, cin_ref, gcc_ref, carry_ref):
        i = pl.program_id(0)

        @pl.when(i == 0)
        def _():
            carry_ref[...] = jnp.zeros_like(carry_ref)
            gcc_ref[...] = jnp.zeros_like(gcc_ref)
            for ref in (gwa_ref, gwx_ref, gba_ref, gbx_ref, glam_ref, gcb_ref, gcw_ref):
                ref[...] = jnp.zeros_like(ref)

        first = i == nch - 1
        last = slice(tc - SUBLANES, tc)
        hbuf[SUBLANES:, :] = h_ref[...]
        hbuf[0:SUBLANES, :] = _down_a_segment(h_ref[last, :], jnp.where(first, 0.0, hh_ref[SUBLANES - 1:SUBLANES, :]))
        _fill_conv_window(xe, x_ref, xh_ref, first, tc)
        lam_v = lam_ref[...]
        sp = _softplus_neg(lam_v)
        a_all = jnp.exp(-LRU_C * r_ref[...] * sp)
        src_a[...] = a_all
        src_m[...] = a_all * gh_ref[...]
        for j0 in range(0, NB_LRU, LRU_JB):
            def kstep(kk, st):
                rows = _step_rows(m - 1 - kk)
                out = []
                for q in range(LRU_JB):
                    ln = _lanes(j0 + q)
                    mu, ac = st[2 * q], st[2 * q + 1]
                    a = src_a[rows, ln]
                    mu = a * mu + src_m[rows, ln]
                    ac = a * ac
                    dst_m[rows, ln] = mu
                    dst_a[rows, ln] = ac
                    out += [mu, ac]
                return tuple(out)

            init = tuple(jnp.zeros((SEGS, LANES), f32) if q % 2 == 0 else jnp.ones((SEGS, LANES), f32)
                         for q in range(2 * LRU_JB))
            ends = lax.fori_loop(0, m, kstep, init)
            for q in range(LRU_JB):
                ln = _lanes(j0 + q)
                em, ea = ends[2 * q], ends[2 * q + 1]
                cr = carry_ref[:, ln]
                rows_c = [None] * SEGS
                for s in reversed(range(SEGS)):
                    rows_c[s] = cr
                    cr = em[s:s + 1, :] + ea[s:s + 1, :] * cr
                cin_ref[:, ln] = _stack_rows(rows_c)
                carry_ref[:, ln] = cr

        def fix(k, _):
            rows = _step_rows(k)
            mbuf[rows, :] = dst_m[rows, :] + dst_a[rows, :] * cin_ref[...]
            return 0

        lax.fori_loop(0, m, fix, 0)
        mbuf[tc:, :] = _up_a_segment(mbuf[0:SUBLANES, :], cin_ref[SUBLANES - 1:SUBLANES, :])
        sneg = _sig(-lam_v)
        for j in range(NB_LRU):
            ln = _lanes(j)
            lamt = gh_ref[:, ln] + mbuf[SUBLANES:, ln]
            rj, ij, cj = r_ref[:, ln], i_ref[:, ln], c_ref[:, ln]
            log_a = -LRU_C * rj * sp[:, ln]
            a = jnp.exp(log_a)
            mult = jnp.sqrt(_one_minus_exp(2.0 * log_a))
            g_a = lamt * hbuf[0:tc, ln]
            g_mult = lamt * ij * cj
            g_i = lamt * mult * cj
            g_c = lamt * mult * ij
            g_log_a = g_a * a - g_mult * a * a / mult
            glam_ref[:, ln] += jnp.sum(g_log_a * rj, axis=0, keepdims=True) * LRU_C * sneg[:, ln]
            g_ra = g_log_a * (-LRU_C) * sp[:, ln] * rj * (1.0 - rj)
            g_ia = g_i * ij * (1.0 - ij)
            gba_ref[:, ln] += jnp.sum(g_ra, axis=0, keepdims=True)
            gbx_ref[:, ln] += jnp.sum(g_ia, axis=0, keepdims=True)
            cjb, grb, gib = cj.astype(bf16), g_ra.astype(bf16), g_ia.astype(bf16)
            gwa_ref[j] += _dot_tn(cjb, grb)
            gwx_ref[j] += _dot_tn(cjb, gib)
            g_c = g_c + _dot_nt(grb, wa_ref[j]) + _dot_nt(gib, wx_ref[j])
            gce[0:tc, ln] = g_c
            gcb_ref[:, ln] += jnp.sum(g_c, axis=0, keepdims=True)
        for d in range(TAPS_BACK):
            blk = slice(SUBLANES * d, SUBLANES * (d + 1))
            gce[tc + SUBLANES * d:tc + SUBLANES * (d + 1), :] = _up_a_segment(gce[blk, :], gcc_ref[SUBLANES * d:SUBLANES * d + 1, :])
        gcc_ref[...] = gce[0:EDGE, :]
        gc = gce[0:tc, :]
        glx = cw_ref[CONV_WIDTH - 1:CONV_WIDTH, :] * gc
        gcw_ref[CONV_WIDTH - 1:CONV_WIDTH, :] += jnp.sum(gc * xe[EDGE:EDGE + tc, :], axis=0, keepdims=True)
        for k in range(CONV_WIDTH - 1):
            off = SUBLANES * (CONV_WIDTH - 1 - k)
            glx = glx + cw_ref[k:k + 1, :] * gce[off:off + tc, :]
            gcw_ref[k:k + 1, :] += jnp.sum(gc * xe[EDGE - off:EDGE - off + tc, :], axis=0, keepdims=True)
        glx_ref[...] = glx.astype(bf16)

    rev = lambda i: (nch - 1 - i, 0)
    halo = lambda rows: (lambda i: (jnp.maximum((nch - 1 - i) * (tc // rows) - 1, 0), 0))
    wide = lambda: pl.BlockSpec((tc, LRU_WIDTH), rev)
    vec = lambda: _full((1, LRU_WIDTH))
    hd = lambda: _full((LRU_HEADS, LRU_HEAD_DIM, LRU_HEAD_DIM))
    buf = lambda rows: pltpu.VMEM((rows, LRU_WIDTH), f32)
    return pl.pallas_call(
        body, name=name, grid=(nch,),
        in_specs=[wide(), wide(), pl.BlockSpec((SUBLANES, LRU_WIDTH), halo(SUBLANES)), wide(), wide(), wide(), wide(),
                  pl.BlockSpec((HALO_ROWS, LRU_WIDTH), halo(HALO_ROWS)), _full((CONV_WIDTH, LRU_WIDTH)), hd(), hd(), vec()],
        out_specs=[wide(), hd(), hd(), vec(), vec(), vec(), vec(), _full((CONV_WIDTH, LRU_WIDTH))],
        out_shape=[_S((T, LRU_WIDTH), bf16), _S((LRU_HEADS, LRU_HEAD_DIM, LRU_HEAD_DIM)),
                   _S((LRU_HEADS, LRU_HEAD_DIM, LRU_HEAD_DIM)), _S((1, LRU_WIDTH)), _S((1, LRU_WIDTH)),
                   _S((1, LRU_WIDTH)), _S((1, LRU_WIDTH)), _S((CONV_WIDTH, LRU_WIDTH))],
        scratch_shapes=[buf(tc), buf(tc), buf(tc), buf(tc), buf(tc + SUBLANES), buf(tc + SUBLANES), buf(tc + EDGE),
                        buf(tc + EDGE), buf(SEGS), buf(EDGE), buf(1)],
        compiler_params=_cp("arbitrary"),
    )(gh, h, h, c, r, gi, lx, lx, convw, wa, wx, lam)


def merge_fwd(h, lg, zs, gsl, x, p, wbl, wout, gpost, wple, wpg, name):
    T = x.shape[0]
    tm = _tile(T, TOKEN_TILE)

    def body(h_ref, lg_ref, zs_ref, gs_ref, gl_ref, x_ref, p_ref, wbl_ref, wout_ref, gp_ref, wple_ref, wpg_ref,
             zl_ref, mix_ref, q_ref, pe_ref, xo_ref):
        lg_v = lg_ref[...]
        yl = h_ref[...] * (lg_v * _sig(lg_v))
        zl = _dot(yl.astype(bf16), wbl_ref[...])
        merged = _sig(gs_ref[...]) * zs_ref[...] + _sig(gl_ref[...]) * zl
        mix = _dot(merged.astype(bf16), wout_ref[...])
        r2 = lax.rsqrt(jnp.mean(mix * mix, axis=-1, keepdims=True) + NORM_EPS)
        x1 = x_ref[...] + mix * r2 * gp_ref[...]
        q = _dot(x1.astype(bf16), wpg_ref[...])
        pe = _dot(p_ref[...].astype(bf16), wple_ref[...])
        zl_ref[...], mix_ref[...], q_ref[...], pe_ref[...] = zl, mix, q, pe
        xo_ref[...] = x1 + pe * _sig(q)

    dm = lambda: _rows(tm, D_MODEL)
    return pl.pallas_call(
        body, name=name, grid=(T // tm,),
        in_specs=[_rows(tm, LRU_WIDTH), _rows(tm, LRU_WIDTH), dm(), _rows(tm, D_MODEL, 0), _rows(tm, D_MODEL, 1), dm(),
                  _rows(tm, PLE_DIM), _resident((LRU_WIDTH, D_MODEL)), _resident((D_MODEL, D_MODEL)), _full((1, D_MODEL)),
                  _resident((PLE_DIM, D_MODEL)), _resident((D_MODEL, D_MODEL))],
        out_specs=[dm(), dm(), dm(), dm(), dm()],
        out_shape=[_S((T, D_MODEL))] * 5, compiler_params=_cp("parallel"),
    )(h, lg, zs, gsl, gsl, x, p, wbl, wout, gpost, wple, wpg)


def merge_bwd(gx2, q, pe, mix, x, zl, zs, gsl, h, lg, wpg, wout, wbl, gpost, name):
    T = x.shape[0]
    tm = _tile(T, TOKEN_TILE)

    def body(gx2_ref, q_ref, pe_ref, mix_ref, x_ref, zl_ref, zs_ref, gs_ref, gl_ref, h_ref, lg_ref,
             wpg_ref, wout_ref, wbl_ref, gp_ref,
             gres_ref, gpe_ref, gq_ref, x1_ref, gmix_ref, mrg_ref, gzl_ref, yl_ref, gzs_ref, ggsl_ref, gh_ref, glg_ref,
             ggp_ref):
        i = pl.program_id(0)
        gx2 = gx2_ref[...]
        sq = _sig(q_ref[...])
        pe = pe_ref[...]
        gpe_ref[...] = (gx2 * sq).astype(bf16)
        gq = (gx2 * pe * sq * (1.0 - sq)).astype(bf16)
        gq_ref[...] = gq
        mix = mix_ref[...]
        gp = gp_ref[...]
        r2 = lax.rsqrt(jnp.mean(mix * mix, axis=-1, keepdims=True) + NORM_EPS)
        nrm = mix * r2
        x1_ref[...] = (x_ref[...] + nrm * gp).astype(bf16)
        gx1 = gx2 + _dot_nt(gq, wpg_ref[...])
        gres_ref[...] = gx1
        part = jnp.sum(gx1 * nrm, axis=0, keepdims=True)

        @pl.when(i == 0)
        def _():
            ggp_ref[...] = part

        @pl.when(i > 0)
        def _():
            ggp_ref[...] += part

        gy = gx1 * gp
        gmix = (r2 * (gy - nrm * jnp.mean(gy * nrm, axis=-1, keepdims=True))).astype(bf16)
        gmix_ref[...] = gmix
        gmerged = _dot_nt(gmix, wout_ref[...])
        zs, zl = zs_ref[...], zl_ref[...]
        ss, sl = _sig(gs_ref[...]), _sig(gl_ref[...])
        mrg_ref[...] = (ss * zs + sl * zl).astype(bf16)
        gzs_ref[...] = (gmerged * ss).astype(bf16)
        gzl = (gmerged * sl).astype(bf16)
        gzl_ref[...] = gzl
        ggsl_ref[:, :D_MODEL] = (gmerged * zs * ss * (1.0 - ss)).astype(bf16)
        ggsl_ref[:, D_MODEL:] = (gmerged * zl * sl * (1.0 - sl)).astype(bf16)
        lg_v, hv = lg_ref[...], h_ref[...]
        slg = _sig(lg_v)
        silu = lg_v * slg
        yl_ref[...] = (hv * silu).astype(bf16)
        gyl = _dot_nt(gzl, wbl_ref[...])
        gh_ref[...] = gyl * silu
        glg_ref[...] = (gyl * hv * slg * (1.0 + lg_v * (1.0 - slg))).astype(bf16)

    dm = lambda: _rows(tm, D_MODEL)
    lw = lambda: _rows(tm, LRU_WIDTH)
    return pl.pallas_call(
        body, name=name, grid=(T // tm,),
        in_specs=[dm(), dm(), dm(), dm(), dm(), dm(), dm(), _rows(tm, D_MODEL, 0), _rows(tm, D_MODEL, 1), lw(), lw(),
                  _resident((D_MODEL, D_MODEL)), _resident((D_MODEL, D_MODEL)), _resident((LRU_WIDTH, D_MODEL)),
                  _full((1, D_MODEL))],
        out_specs=[dm(), dm(), dm(), dm(), dm(), dm(), dm(), lw(), dm(), _rows(tm, 2 * D_MODEL), lw(), lw(),
                   _full((1, D_MODEL))],
        out_shape=[_S((T, D_MODEL)), _S((T, D_MODEL), bf16), _S((T, D_MODEL), bf16), _S((T, D_MODEL), bf16),
                   _S((T, D_MODEL), bf16), _S((T, D_MODEL), bf16), _S((T, D_MODEL), bf16), _S((T, LRU_WIDTH), bf16),
                   _S((T, D_MODEL), bf16), _S((T, 2 * D_MODEL), bf16), _S((T, LRU_WIDTH)), _S((T, LRU_WIDTH), bf16),
                   _S((1, D_MODEL))],
        compiler_params=_cp("arbitrary"),
    )(gx2, q, pe, mix, x, zl, zs, gsl, gsl, h, lg, wpg, wout, wbl, gpost)


def loss_head(y, target, name):
    T = y.shape[0]
    tm = _tile(T, TOKEN_TILE)

    def body(y_ref, t_ref, l_ref, g_ref):
        i = pl.program_id(0)
        e = y_ref[...] - t_ref[...]
        g_ref[...] = e * (1.0 / D_MODEL)
        part = 0.5 * jnp.sum(jnp.sum(e * e, axis=-1, keepdims=True) * (1.0 / D_MODEL), axis=0, keepdims=True)

        @pl.when(i == 0)
        def _():
            l_ref[...] = part

        @pl.when(i > 0)
        def _():
            l_ref[...] += part

    return pl.pallas_call(
        body, name=name, grid=(T // tm,),
        in_specs=[_rows(tm, D_MODEL), _rows(tm, D_MODEL)], out_specs=[_full((1, 1)), _rows(tm, D_MODEL)],
        out_shape=[_S((1, 1)), _S((T, D_MODEL))],
        compiler_params=_cp("arbitrary"),
    )(y, target)


def _s5_operands(w, m, tag):
    b_re_t = jnp.transpose(w['s5_b_re'], (2, 0, 1))
    b_im_t = jnp.transpose(w['s5_b_im'], (2, 0, 1))
    ldt = w['s5_log_dt'][:, None]
    ab, pw, bb = s5_prep(w['s5_a_re'], w['s5_a_im'], ldt, b_re_t, b_im_t, m, "s5_prep" + tag)
    over_sublanes = lambda t: jnp.broadcast_to(t[..., None, :], t.shape[:-1] + (SEGS, S5_NS))
    ptab = pw.reshape(2, m, S5_NS)
    return dict(abar_b=over_sublanes(ab.reshape(2, S5_NS)), ptab_b=over_sublanes(ptab),
                ptab_rev_b=over_sublanes(ptab[:, ::-1, :]), bdb=_pack_bdb(bb).astype(bf16),
                cdb=_pack_cdb(w['s5_c_re'], w['s5_c_im']).astype(bf16), dvec=w['s5_d'][None, :],
                prep_in=(w['s5_a_re'], w['s5_a_im'], ldt, b_re_t, b_im_t))


def layer_fwd(x, p, w, tag):
    T = x.shape[0]
    m = _tile(T, SCAN_CHUNK) // SEGS
    s5 = _s5_operands(w, m, tag)
    h_bf = rms_fwd(x, w['g_pre'][None, :], "rms_fwd" + tag)
    win = w['w_in']
    usg = mm_nn(h_bf, win[:, :2 * S5_WIDTH], "proj_s5" + tag)
    lx = mm_nn(h_bf, win[:, 2 * S5_WIDTH:2 * S5_WIDTH + LRU_WIDTH], "proj_lx" + tag)
    lg = mm_nn(h_bf, win[:, 2 * S5_WIDTH + LRU_WIDTH:2 * S5_WIDTH + 2 * LRU_WIDTH], "proj_lg" + tag)
    gsl = mm_nn(h_bf, win[:, 2 * S5_WIDTH + 2 * LRU_WIDTH:], "proj_gate" + tag)
    ys, s_re, s_im, s_bf = s5_fwd(usg, s5['bdb'], s5['cdb'], s5['dvec'], s5['abar_b'], s5['ptab_b'], "s5_fwd" + tag)
    glu, zs = s5_post_fwd(ys, usg, w['w_glu'], w['w_bs'], "s5_post_fwd" + tag)
    wa, wx = w['lru_w_a'].astype(bf16), w['lru_w_x'].astype(bf16)
    c, r, gi, hs = lru_fwd(lx, w['conv_w'], w['conv_b'][None, :], wa, wx, w['lru_b_a'][None, :], w['lru_b_x'][None, :],
                           w['lru_lambda'][None, :], "lru_fwd" + tag)
    zl, mix, q, pe, x_out = merge_fwd(hs, lg, zs, gsl, x, p, w['w_bl'], w['w_out'], w['g_post'][None, :],
                                      w['w_ple'], w['w_ple_gate'], "merge_fwd" + tag)
    saved = dict(x=x, p=p, h_bf=h_bf, usg=usg, lx=lx, lg=lg, gsl=gsl, ys=ys, s_re=s_re, s_im=s_im, s_bf=s_bf, glu=glu,
                 zs=zs, c=c, r=r, gi=gi, hs=hs, zl=zl, mix=mix, q=q, pe=pe, s5=s5, wa=wa, wx=wx)
    return x_out, saved


def layer_bwd(gx_out, w, sv, tag):
    s5 = sv['s5']
    (gres, gpe, gq, x1_bf, gmix, merged, gzl, yl, gzs, ggsl, g_h, g_lg, g_gpost) = merge_bwd(
        gx_out, sv['q'], sv['pe'], sv['mix'], sv['x'], sv['zl'], sv['zs'], sv['gsl'], sv['hs'], sv['lg'],
        w['w_ple_gate'], w['w_out'], w['w_bl'], w['g_post'][None, :], "merge_bwd" + tag)
    g = {}
    g['w_ple'] = mm_tn(sv['p'], gpe, "gw_ple" + tag)
    g['w_ple_gate'] = mm_tn(x1_bf, gq, "gw_ple_gate" + tag)
    g['w_out'] = mm_tn(merged, gmix, "gw_out" + tag)
    g['w_bl'] = mm_tn(yl, gzl, "gw_bl" + tag)
    g['g_post'] = g_gpost[0]
    (g_lx, g_wa, g_wx, g_ba, g_bx, g_lam, g_cb, g_cw) = lru_bwd(
        g_h, sv['hs'], sv['c'], sv['r'], sv['gi'], sv['lx'], w['conv_w'], sv['wa'], sv['wx'],
        w['lru_lambda'][None, :], "lru_bwd" + tag)
    g['lru_w_a'], g['lru_w_x'] = g_wa, g_wx
    g['lru_b_a'], g['lru_b_x'], g['lru_lambda'], g['conv_b'], g['conv_w'] = g_ba[0], g_bx[0], g_lam[0], g_cb[0], g_cw
    y2, gglu, ge, g_ys, g_sg = s5_post_bwd(gzs, sv['glu'], sv['usg'], sv['ys'], w['w_bs'], w['w_glu'], "s5_post_bwd" + tag)
    g['w_bs'] = mm_tn(y2, gzs, "gw_bs" + tag)
    g['w_glu'] = mm_tn(ge, gglu, "gw_glu" + tag)
    g_u, g_ab, g_d, g_bdb, g_cdb = s5_bwd(g_ys, sv['usg'], sv['s_re'], sv['s_im'], sv['s_bf'], s5['bdb'], s5['cdb'],
                                          s5['dvec'], s5['abar_b'], s5['ptab_rev_b'], "s5_bwd" + tag)
    g['s5_d'] = g_d[0]
    g['s5_c_re'], g['s5_c_im'] = _unpack_cdb(g_cdb)
    g_are, g_aim, g_ldt, g_bre_t, g_bim_t = s5_prep_bwd(*s5['prep_in'], g_ab.reshape(2, S5_GROUPS, S5_STATE),
                                                       _unpack_bdb(g_bdb), "s5_prep_bwd" + tag)
    g['s5_a_re'], g['s5_a_im'], g['s5_log_dt'] = g_are, g_aim, g_ldt
    g['s5_b_re'] = jnp.transpose(g_bre_t, (1, 2, 0))
    g['s5_b_im'] = jnp.transpose(g_bim_t, (1, 2, 0))
    gproj = jnp.concatenate([g_u, g_sg, g_lx, g_lg, ggsl], axis=1)
    g['w_in'] = mm_tn(sv['h_bf'], gproj, "gw_in" + tag)
    gh = mm_nt(gproj, w['w_in'], "g_h" + tag)
    gx, g_gpre = rms_bwd(gh, sv['x'], w['g_pre'][None, :], gres, "rms_bwd" + tag)
    g['g_pre'] = g_gpre[0]
    return gx, g


def _as_2d(a):
    return a.reshape((-1, a.shape[-1])) if a.ndim > 1 else a.reshape((1, -1))


def _adamw_update(w, gv, m, v):
    nm = ADAM_B1 * m + (1.0 - ADAM_B1) * gv
    nv = ADAM_B2 * v + (1.0 - ADAM_B2) * (gv * gv)
    bc1 = 1.0 - ADAM_B1 ** ADAM_STEP
    bc2 = 1.0 - ADAM_B2 ** ADAM_STEP
    return -ADAM_LR * ((nm / bc1) / (jnp.sqrt(nv / bc2) + ADAM_EPS) + ADAM_WD * w), nm, nv


def adamw(w, g, m, v, name):
    shape = w.shape
    w2, g2, m2, v2 = _as_2d(w), _as_2d(g), _as_2d(m), _as_2d(v)
    R, C = w2.shape
    tr = _row_tile(R, C)

    def body(w_ref, g_ref, m_ref, v_ref, d_ref, nm_ref, nv_ref):
        d_ref[...], nm_ref[...], nv_ref[...] = _adamw_update(w_ref[...], g_ref[...], m_ref[...], v_ref[...])

    spec = lambda: pl.BlockSpec((tr, C), lambda i: (i, 0))
    d, nm, nv = pl.pallas_call(
        body, name=name, grid=(R // tr,), in_specs=[spec() for _ in range(4)], out_specs=[spec() for _ in range(3)],
        out_shape=[_S((R, C))] * 3, compiler_params=_cp("parallel"),
    )(w2, g2, m2, v2)
    return d.reshape(shape), nm.reshape(shape), nv.reshape(shape)


def adamw_reduce(w, a, theirs, m, v, chip, name):
    shape = w.shape
    w2, m2, v2 = _as_2d(w), _as_2d(m), _as_2d(v)
    R, C = w2.shape
    a3, t3 = a.reshape(4, R, C), theirs.reshape(3, R, C)
    tr = _row_tile(R, C)

    def body(chip_ref, w_ref, a_ref, t_ref, m_ref, v_ref, g_ref, d_ref, nm_ref, nv_ref):
        gv = ((a_ref[0] + t_ref[0].astype(f32)) + t_ref[1].astype(f32)) + t_ref[2].astype(f32)
        g_ref[...] = gv
        d_ref[...], nm_ref[...], nv_ref[...] = _adamw_update(w_ref[...], gv, m_ref[...], v_ref[...])

    spec = lambda: pl.BlockSpec((tr, C), lambda i, c: (i, 0))
    grid_spec = pltpu.PrefetchScalarGridSpec(
        num_scalar_prefetch=1, grid=(R // tr,),
        in_specs=[spec(), pl.BlockSpec((1, tr, C), lambda i, c: (c[0], i, 0)), pl.BlockSpec((3, tr, C), lambda i, c: (0, i, 0)),
                  spec(), spec()],
        out_specs=[spec() for _ in range(4)])
    g, d, nm, nv = pl.pallas_call(
        body, name=name, grid_spec=grid_spec, out_shape=[_S((R, C))] * 4, compiler_params=_cp("parallel"),
    )(chip, w2, a3, t3, m2, v2)
    return g.reshape(shape), d.reshape(shape), nm.reshape(shape), nv.reshape(shape)


MESH = pl.DeviceIdType.MESH
ANY = pl.BlockSpec(memory_space=pl.ANY)


def _place():
    return lax.axis_index("x"), lax.axis_index("y"), lax.axis_index("c")


def _other_chips(mx, my):
    return [(1 - mx, my), (mx, 1 - my), (1 - mx, 1 - my)]


def all_gather(shards, name):
    nb = len(shards)

    def body(*refs):
        x_refs, out_refs, send_sems, recv_sems, local_sems = refs[:nb], refs[nb:2 * nb], *refs[2 * nb:]
        mx, my, mc = _place()
        me, sibling = (mx, my, mc), (mx, my, 1 - mc)
        chips = _other_chips(mx, my)

        def copy(b, k, block, to, src=None):
            px, py, pc = block
            rows = out_refs[b].at[4 * px + 2 * py + pc]
            return pltpu.make_async_remote_copy(
                src_ref=rows if src is None else src, dst_ref=rows, send_sem=send_sems.at[b, k],
                recv_sem=recv_sems.at[b, k], device_id=to, device_id_type=MESH)

        first = []
        for b in range(nb):
            first += [copy(b, 1 + j, me, (*chip, mc), src=x_refs[b]) for j, chip in enumerate(chips)]
        for b in range(nb):
            first.append(copy(b, 0, me, sibling, src=x_refs[b]))
        for cp in first:
            cp.start()
        mine = [pltpu.make_async_copy(x_refs[b], out_refs[b].at[4 * mx + 2 * my + mc], local_sems.at[b]) for b in range(nb)]
        for cp in mine:
            cp.start()
        passed = []
        for j, chip in enumerate(chips):
            for b in range(nb):
                copy(b, 1 + j, (*chip, mc), me).wait_recv()
                passed.append(copy(b, 4 + j, (*chip, mc), sibling))
                passed[-1].start()
        for b in range(nb):
            copy(b, 0, sibling, me).wait_recv()
            for j, chip in enumerate(chips):
                copy(b, 4 + j, (*chip, 1 - mc), me).wait_recv()
        for cp in first + passed:
            cp.wait_send()
        for cp in mine:
            cp.wait()

    outs = pl.pallas_call(
        body, name=name, out_shape=[_S((N_DEV,) + s.shape, s.dtype) for s in shards], in_specs=[ANY] * nb,
        out_specs=[ANY] * nb,
        scratch_shapes=[pltpu.SemaphoreType.DMA((nb, 7)), pltpu.SemaphoreType.DMA((nb, 7)), pltpu.SemaphoreType.DMA((nb,))],
    )(*shards)
    return list(outs)


def exchange_sibling(gs, name):
    nb = len(gs)

    def body(*refs):
        g_refs, recv_refs, send_sems, recv_sems = refs[:nb], refs[nb:2 * nb], refs[2 * nb], refs[2 * nb + 1]
        mx, my, mc = _place()
        copies = [pltpu.make_async_remote_copy(
            src_ref=g_refs[b].at[2 * k + 1 - mc], dst_ref=recv_refs[b].at[k], send_sem=send_sems.at[b, k],
            recv_sem=recv_sems.at[b, k], device_id=(mx, my, 1 - mc), device_id_type=MESH)
            for b in range(nb) for k in range(4)]
        for cp in copies:
            cp.start()
        for cp in copies:
            cp.wait()

    outs = pl.pallas_call(
        body, name=name, out_shape=[_S((4,) + g.shape[1:], g.dtype) for g in gs], in_specs=[ANY] * nb,
        out_specs=[ANY] * nb,
        scratch_shapes=[pltpu.SemaphoreType.DMA((nb, 4)), pltpu.SemaphoreType.DMA((nb, 4))],
    )(*gs)
    return list(outs)


def exchange_chips(parts, name):
    nb = len(parts)

    def body(*refs):
        a_refs, recv_refs, send_sems, recv_sems = refs[:nb], refs[nb:2 * nb], refs[2 * nb], refs[2 * nb + 1]
        mx, my, mc = _place()
        copies = [pltpu.make_async_remote_copy(
            src_ref=a_refs[b].at[2 * px + py], dst_ref=recv_refs[b].at[j], send_sem=send_sems.at[b, j],
            recv_sem=recv_sems.at[b, j], device_id=(px, py, mc), device_id_type=MESH)
            for b in range(nb) for j, (px, py) in enumerate(_other_chips(mx, my))]
        for cp in copies:
            cp.start()
        for cp in copies:
            cp.wait()

    outs = pl.pallas_call(
        body, name=name, out_shape=[_S((3,) + a.shape[1:], a.dtype) for a in parts], in_specs=[ANY] * nb,
        out_specs=[ANY] * nb,
        scratch_shapes=[pltpu.SemaphoreType.DMA((nb, 3)), pltpu.SemaphoreType.DMA((nb, 3))],
    )(*parts)
    return list(outs)


def add_sibling(g, theirs, core, name, wire_dtype=f32):
    shp = theirs.shape
    C = shp[-1]
    R = math.prod(shp[1:-1])
    tr = _row_tile(R, C)
    narrow = wire_dtype != f32

    def body(core_ref, g_ref, t_ref, o_ref, *wire_ref):
        s = g_ref[...] + t_ref[...]
        o_ref[...] = s
        if narrow:
            wire_ref[0][...] = s.astype(wire_dtype)

    blk = lambda: pl.BlockSpec((1, tr, C), lambda k, i, c: (k, i, 0))
    grid_spec = pltpu.PrefetchScalarGridSpec(
        num_scalar_prefetch=1, grid=(4, R // tr),
        in_specs=[pl.BlockSpec((1, tr, C), lambda k, i, c: (2 * k + c[0], i, 0)), blk()],
        out_specs=[blk(), blk()] if narrow else [blk()])
    outs = pl.pallas_call(
        body, name=name, grid_spec=grid_spec,
        out_shape=[_S((4, R, C), f32)] + ([_S((4, R, C), wire_dtype)] if narrow else []),
        compiler_params=_cp("parallel", "parallel"),
    )(core, g.reshape(N_DEV, R, C), theirs.reshape(4, R, C))
    part = outs[0].reshape(shp)
    return part, (outs[1].reshape(shp) if narrow else part)


def add_chips(a, theirs, chip, name):
    _, R, C = a.shape
    tr = _row_tile(R, C)

    def body(chip_ref, a_ref, t_ref, out_ref):
        out_ref[...] = ((a_ref[0] + t_ref[0]) + t_ref[1]) + t_ref[2]

    grid_spec = pltpu.PrefetchScalarGridSpec(
        num_scalar_prefetch=1, grid=(R // tr,),
        in_specs=[pl.BlockSpec((1, tr, C), lambda i, c: (c[0], i, 0)), pl.BlockSpec((3, tr, C), lambda i, c: (0, i, 0))],
        out_specs=pl.BlockSpec((tr, C), lambda i, c: (i, 0)))
    return pl.pallas_call(
        body, name=name, grid_spec=grid_spec, out_shape=_S((R, C), a.dtype), compiler_params=_cp("parallel"),
    )(chip, a, theirs)


def _round_up(n, q):
    return (n + q - 1) // q * q


def _lane_rows(a):
    flat = a.reshape(-1)
    n = _round_up(flat.shape[0], SUBLANES * LANES)
    return jnp.pad(flat, (0, n - flat.shape[0])).reshape(-1, LANES)


def _full_to_shards(full, axis):
    shp = full.shape
    s = shp[axis] // N_DEV
    cut = full.reshape(shp[:axis] + (N_DEV, s) + shp[axis + 1:])
    return jnp.moveaxis(cut, axis, 0)


def _shards_to_full(parts, axis):
    shp = list(parts.shape[1:])
    shp[axis] *= N_DEV
    return jnp.moveaxis(parts, 0, axis).reshape(tuple(shp))


def kernel(x, p, g_pre, w_in, s5_a_re, s5_a_im, s5_log_dt, s5_b_re, s5_b_im, s5_c_re, s5_c_im, s5_d, w_glu, w_bs, conv_w, conv_b, lru_w_a, lru_b_a, lru_w_x, lru_b_x, lru_lambda, w_bl, w_out, g_post, w_ple, w_ple_gate, loss_target, m_g_pre, m_w_in, m_s5_a_re, m_s5_a_im, m_s5_log_dt, m_s5_b_re, m_s5_b_im, m_s5_c_re, m_s5_c_im, m_s5_d, m_w_glu, m_w_bs, m_conv_w, m_conv_b, m_lru_w_a, m_lru_b_a, m_lru_w_x, m_lru_b_x, m_lru_lambda, m_w_bl, m_w_out, m_g_post, m_w_ple, m_w_ple_gate, v_g_pre, v_w_in, v_s5_a_re, v_s5_a_im, v_s5_log_dt, v_s5_b_re, v_s5_b_im, v_s5_c_re, v_s5_c_im, v_s5_d, v_w_glu, v_w_bs, v_conv_w, v_conv_b, v_lru_w_a, v_lru_b_a, v_lru_w_x, v_lru_b_x, v_lru_lambda, v_w_bl, v_w_out, v_g_post, v_w_ple, v_w_ple_gate):
    given = dict(locals())
    W = {n: given[n] for n in WEIGHTS}
    M = {n: given["m_" + n] for n in WEIGHTS}
    V = {n: given["v_" + n] for n in WEIGHTS}
    xs, target = to_scan_order(x[0]), to_scan_order(loss_target[0])
    ps = [to_scan_order(p[i, 0]) for i in range(DEPTH)]

    mx, my, mc = _place()
    core = jnp.reshape(mc, (1,)).astype(jnp.int32)
    chip = jnp.reshape(2 * mx + my, (1,)).astype(jnp.int32)

    names = list(SHARDED)
    shards = [W[n].astype(bf16) if n in GATHER_BF16 else W[n] for n in names]
    gathered = all_gather(shards, "comm_gather_weights")
    full = {n: _shards_to_full(g, SHARDED[n]) for n, g in zip(names, gathered)}

    def layer_weights(i):
        return {n: (full[n][i] if n in SHARDED else W[n][i]) for n in WEIGHTS}

    act, saved = xs, []
    for i in range(DEPTH):
        act, sv = layer_fwd(act, ps[i], layer_weights(i), "_l%d" % i)
        saved.append(sv)
    loss_part, gact = loss_head(act, target, "loss_head")
    grads = [None] * DEPTH
    for i in reversed(range(DEPTH)):
        gact, grads[i] = layer_bwd(gact, layer_weights(i), saved[i], "_l%d" % i)
    loss = lax.psum(loss_part[0, 0], ("x", "y", "c"))
    gfull = {n: jnp.stack([grads[i][n].reshape(full[n].shape[1:] if n in SHARDED else W[n].shape[1:])
                           for i in range(DEPTH)]) for n in WEIGHTS}

    rep_rows = [_lane_rows(gfull[n]) for n in REPLICATED]
    n_rows = sum(r.shape[0] for r in rep_rows)
    pad_rows = _round_up(n_rows, N_DEV * SUBLANES) - n_rows
    rep_blocks = jnp.concatenate(rep_rows + [jnp.zeros((pad_rows, LANES), f32)]).reshape(N_DEV, -1, LANES)
    blocks = [_full_to_shards(gfull[n], SHARDED[n]) for n in names] + [rep_blocks]
    theirs = exchange_sibling(blocks, "comm_reduce_sibling")
    parts, wire = [], []
    for k, (b, t) in enumerate(zip(blocks, theirs)):
        part, sent = add_sibling(b, t, core, "reduce_add_sibling_%d" % k, bf16 if k < len(names) else f32)
        parts.append(part)
        wire.append(sent)
    others = exchange_chips(wire, "comm_reduce_chips")

    red, deltas, new_m, new_v = {}, {}, {}, {}
    for n, a, t in zip(names, parts, others):
        red[n], deltas[n], new_m[n], new_v[n] = adamw_reduce(W[n], a, t, M[n], V[n], chip, "adamw_" + n)
    piece = add_chips(parts[-1], others[-1], chip, "reduce_add_chips")
    rep_all = all_gather([piece], "comm_gather_replicated")[0].reshape(-1, LANES)
    off = 0
    for n, rows in zip(REPLICATED, rep_rows):
        k = math.prod(W[n].shape)
        red[n] = rep_all[off:off + rows.shape[0]].reshape(-1)[:k].reshape(W[n].shape)
        off += rows.shape[0]
        deltas[n], new_m[n], new_v[n] = adamw(W[n], red[n], M[n], V[n], "adamw_" + n)
    return (loss, from_scan_order(gact)[None], *[red[n] for n in WEIGHTS], *[deltas[n] for n in WEIGHTS],
            *[new_m[n] for n in WEIGHTS], *[new_v[n] for n in WEIGHTS])
```

```python
import math

import jax
import jax.numpy as jnp
from jax import lax
from jax.experimental import pallas as pl
from jax.experimental.pallas import tpu as pltpu

f32 = jnp.float32
bf16 = jnp.bfloat16

D_MODEL = 1024
DEPTH = 2
PLE_DIM = 256
NORM_EPS = 1e-6
S5_WIDTH = 512
S5_GROUP = 16
S5_GROUPS = 32
S5_STATE = 64
S5_NS = S5_GROUPS * S5_STATE
LRU_WIDTH = 1280
LRU_HEADS = 10
LRU_HEAD_DIM = 128
LRU_C = 8.0
CONV_WIDTH = 4
N_DEV = 8

ADAM_LR = 0.001
ADAM_B1 = 0.9
ADAM_B2 = 0.999
ADAM_EPS = 1e-08
ADAM_WD = 0.01
ADAM_STEP = 10

LANES = 128
SUBLANES = 8
SEGS = SUBLANES
SCAN_CHUNK = 256
TOKEN_TILE = 256
MM_TILE_M = 1024
MM_TILE_N = 1408
MM_TILE_K_ROWS = 1280
PAIR = 2 * SUBLANES
VMEM_LIMIT_BYTES = 56 * 1024 * 1024
ELEMENTWISE_BLOCK_BYTES = 1024 * 1024

WEIGHTS = ['g_pre', 'w_in', 's5_a_re', 's5_a_im', 's5_log_dt', 's5_b_re', 's5_b_im', 's5_c_re', 's5_c_im',
           's5_d', 'w_glu', 'w_bs', 'conv_w', 'conv_b', 'lru_w_a', 'lru_b_a', 'lru_w_x', 'lru_b_x',
           'lru_lambda', 'w_bl', 'w_out', 'g_post', 'w_ple', 'w_ple_gate']
SHARDED = {'w_in': 2, 'w_glu': 2, 'w_bs': 2, 'conv_w': 2, 'w_bl': 1, 'w_out': 1, 'w_ple': 2, 'w_ple_gate': 1}
GATHER_BF16 = ['w_in', 'w_glu', 'w_bs', 'w_bl', 'w_out', 'w_ple', 'w_ple_gate']
REPLICATED = [n for n in WEIGHTS if n not in SHARDED]


def _sig(x):
    return 0.5 * jnp.tanh(0.5 * x) + 0.5


def _gelu_parts(x):
    k = math.sqrt(2.0 / math.pi)
    t = jnp.tanh(k * (x + 0.044715 * x * x * x))
    return t, k


def _gelu(x):
    t, _ = _gelu_parts(x)
    return 0.5 * x * (1.0 + t)


def _gelu_grad(x):
    t, k = _gelu_parts(x)
    return 0.5 * (1.0 + t) + 0.5 * x * (1.0 - t * t) * k * (1.0 + 3.0 * 0.044715 * x * x)


def _one_minus_exp(z):
    series = -z * (1.0 + z * (0.5 + z * (1.0 / 6.0 + z * (1.0 / 24.0 + z * (1.0 / 120.0)))))
    return jnp.where(z > -0.05, series, 1.0 - jnp.exp(z))


def _softplus_neg(lam):
    return jnp.maximum(-lam, 0.0) + jnp.log(1.0 + jnp.exp(-jnp.abs(lam)))


def _dot(a, b):
    return jnp.dot(a, b, preferred_element_type=f32)


def _dot_nt(a, b):
    return lax.dot_general(a, b, (((1,), (1,)), ((), ())), preferred_element_type=f32)


def _dot_tn(a, b):
    return lax.dot_general(a, b, (((0,), (0,)), ((), ())), preferred_element_type=f32)


def _S(shape, dtype=f32):
    return jax.ShapeDtypeStruct(shape, dtype)


def _full(shape):
    nd = len(shape)
    return pl.BlockSpec(shape, lambda *_: (0,) * nd)


def _rows(tile, width, col=0):
    return pl.BlockSpec((tile, width), lambda i: (i, col))


def _cp(*semantics):
    return pltpu.CompilerParams(dimension_semantics=semantics or None, vmem_limit_bytes=VMEM_LIMIT_BYTES)


def _tile(n, want):
    t = min(n, want)
    assert n % t == 0, (n, want)
    return t


def _row_tile(R, C=LANES):
    cap = max(SUBLANES, min(R, ELEMENTWISE_BLOCK_BYTES // (4 * C)))
    for t in range(cap - cap % SUBLANES, 0, -SUBLANES):
        if R % t == 0:
            return t
    return R


def _lanes(j):
    return slice(LANES * j, LANES * (j + 1))


def _step_rows(k, n=SUBLANES):
    return pl.ds(pl.multiple_of(k * n, n), n)


def to_scan_order(a):
    T, C = a.shape
    tc = _tile(T, SCAN_CHUNK)
    return a.reshape(T // tc, SEGS, tc // SEGS, C).transpose(0, 2, 1, 3).reshape(T, C)


def from_scan_order(a):
    T, C = a.shape
    tc = _tile(T, SCAN_CHUNK)
    return a.reshape(T // tc, tc // SEGS, SEGS, C).transpose(0, 2, 1, 3).reshape(T, C)


def _col_tile(n, cap):
    if n <= cap:
        return n
    for t in range(cap - cap % LANES, 0, -LANES):
        if n % t == 0:
            return t
    return n


def _resident(shape):
    nd = len(shape)
    return pl.BlockSpec(shape, lambda *_: (0,) * nd, pipeline_mode=pl.Buffered(1))


def mm_nn(a, b, name, out_dtype=f32):
    M, K = a.shape
    N = b.shape[1]
    tm, tn = _tile(M, MM_TILE_M), _col_tile(N, MM_TILE_N)

    def body(a_ref, b_ref, o_ref):
        o_ref[...] = _dot(a_ref[...].astype(bf16), b_ref[...].astype(bf16)).astype(out_dtype)

    return pl.pallas_call(
        body, name=name, grid=(M // tm, N // tn),
        in_specs=[pl.BlockSpec((tm, K), lambda i, j: (i, 0)), pl.BlockSpec((K, tn), lambda i, j: (0, j))],
        out_specs=pl.BlockSpec((tm, tn), lambda i, j: (i, j)),
        out_shape=_S((M, N), out_dtype), compiler_params=_cp("parallel", "parallel"),
    )(a, b)


def mm_tn(a, b, name):
    M, K = a.shape
    N = b.shape[1]
    tm, tk, tn = _tile(M, MM_TILE_M), _col_tile(K, MM_TILE_K_ROWS), _col_tile(N, MM_TILE_N)

    def body(a_ref, b_ref, o_ref):
        m = pl.program_id(2)
        part = _dot_tn(a_ref[...].astype(bf16), b_ref[...].astype(bf16))

        @pl.when(m == 0)
        def _():
            o_ref[...] = part

        @pl.when(m > 0)
        def _():
            o_ref[...] += part

    return pl.pallas_call(
        body, name=name, grid=(K // tk, N // tn, M // tm),
        in_specs=[pl.BlockSpec((tm, tk), lambda i, j, m: (m, i)), pl.BlockSpec((tm, tn), lambda i, j, m: (m, j))],
        out_specs=pl.BlockSpec((tk, tn), lambda i, j, m: (i, j)),
        out_shape=_S((K, N), f32),
        compiler_params=_cp("parallel", "parallel", "arbitrary"),
    )(a, b)


def rms_fwd(x, g, name):
    T = x.shape[0]
    tm = _tile(T, TOKEN_TILE)

    def body(x_ref, g_ref, h_ref):
        xv = x_ref[...]
        r = lax.rsqrt(jnp.mean(xv * xv, axis=-1, keepdims=True) + NORM_EPS)
        h_ref[...] = (xv * r * g_ref[...]).astype(bf16)

    return pl.pallas_call(
        body, name=name, grid=(T // tm,),
        in_specs=[_rows(tm, D_MODEL), _full((1, D_MODEL))], out_specs=_rows(tm, D_MODEL),
        out_shape=_S((T, D_MODEL), bf16), compiler_params=_cp("parallel"),
    )(x, g)


def _s5_discretise(a_re, a_im, log_dt, b_re_t, b_im_t):
    dt = jnp.exp(log_dt)
    mag = jnp.exp(a_re * dt)
    ab_re = mag * jnp.cos(a_im * dt)
    ab_im = mag * jnp.sin(a_im * dt)
    den = a_re * a_re + a_im * a_im
    nr, ni = ab_re - 1.0, ab_im
    z_re = (nr * a_re + ni * a_im) / den
    z_im = (ni * a_re - nr * a_im) / den
    bb_re = z_re[None] * b_re_t - z_im[None] * b_im_t
    bb_im = z_re[None] * b_im_t + z_im[None] * b_re_t
    return ab_re, ab_im, bb_re, bb_im


def s5_prep(a_re, a_im, log_dt, b_re_t, b_im_t, m, name):
    G, N = a_re.shape

    def body(are_ref, aim_ref, ldt_ref, bre_ref, bim_ref, ab_ref, pw_ref, bb_ref):
        are, aim, ldt = are_ref[...], aim_ref[...], ldt_ref[...]
        ab_re, ab_im, bb_re, bb_im = _s5_discretise(are, aim, ldt, bre_ref[...], bim_ref[...])
        ab_ref[0], ab_ref[1] = ab_re, ab_im
        bb_ref[0], bb_ref[1] = bb_re, bb_im
        dt = jnp.exp(ldt)
        for k in range(m):
            mag = jnp.exp(are * dt * (k + 1.0))
            pw_ref[0, k] = mag * jnp.cos(aim * dt * (k + 1.0))
            pw_ref[1, k] = mag * jnp.sin(aim * dt * (k + 1.0))

    return pl.pallas_call(
        body, name=name,
        out_shape=[_S((2, G, N)), _S((2, m, G, N)), _S((2, S5_GROUP, G, N))], compiler_params=_cp(),
    )(a_re, a_im, log_dt, b_re_t, b_im_t)


def s5_prep_bwd(a_re, a_im, log_dt, b_re_t, b_im_t, g_ab, g_bb, name):
    G, N = a_re.shape

    def body(are_ref, aim_ref, ldt_ref, bre_ref, bim_ref, gab_ref, gbb_ref, o_are, o_aim, o_ldt, o_bre, o_bim):
        _, vjp = jax.vjp(_s5_discretise, are_ref[...], aim_ref[...], ldt_ref[...], bre_ref[...], bim_ref[...])
        g_are, g_aim, g_ldt, g_bre, g_bim = vjp((gab_ref[0], gab_ref[1], gbb_ref[0], gbb_ref[1]))
        o_are[...], o_aim[...], o_ldt[...], o_bre[...], o_bim[...] = g_are, g_aim, g_ldt, g_bre, g_bim

    return pl.pallas_call(
        body, name=name,
        out_shape=[_S((G, N)), _S((G, N)), _S((G, 1)), _S((S5_GROUP, G, N)), _S((S5_GROUP, G, N))],
        compiler_params=_cp(),
    )(a_re, a_im, log_dt, b_re_t, b_im_t, g_ab, g_bb)


NB_S5 = S5_NS // LANES
CB_S5 = S5_WIDTH // LANES
SB_PER_CB = NB_S5 // CB_S5
GRP_PER_SB = LANES // S5_STATE
S5_JB = 8


def _bdb_mask():
    j = jnp.arange(NB_S5)
    own_rows = (j[:, None] % SB_PER_CB == jnp.arange(SB_PER_CB)[None, :]).astype(f32)
    eye = jnp.eye(GRP_PER_SB, dtype=f32)
    return own_rows[:, :, None, None, None, None, None] * eye[None, None, :, None, None, :, None]


def _pack_bdb(bb):
    v = jnp.transpose(bb.reshape(2, S5_GROUP, NB_S5, GRP_PER_SB, S5_STATE), (2, 3, 1, 0, 4))
    full = v[:, None, :, :, :, None, :] * _bdb_mask()
    return full.reshape(NB_S5, LANES, 2 * LANES)


def _unpack_bdb(g_bdb):
    g7 = g_bdb.reshape(NB_S5, SB_PER_CB, GRP_PER_SB, S5_GROUP, 2, GRP_PER_SB, S5_STATE)
    v = jnp.sum(g7 * _bdb_mask(), axis=(1, 5))
    return jnp.transpose(v, (3, 2, 0, 1, 4)).reshape(2, S5_GROUP, S5_GROUPS, S5_STATE)


def _pack_cdb(c_re, c_im):
    gl = S5_GROUPS // CB_S5
    c2 = jnp.stack([c_re, -c_im]).reshape(2, CB_S5, gl, S5_GROUP, S5_STATE)
    eye = jnp.eye(gl, dtype=f32)
    full = jnp.transpose(c2, (1, 0, 2, 4, 3))[:, :, :, :, None, :] * eye[None, None, :, None, :, None]
    return full.reshape(CB_S5, 2 * SB_PER_CB * LANES, LANES)


def _unpack_cdb(g_cdb):
    gl = S5_GROUPS // CB_S5
    g6 = g_cdb.reshape(CB_S5, 2, gl, S5_STATE, gl, S5_GROUP)
    eye = jnp.eye(gl, dtype=f32)
    v = jnp.sum(g6 * eye[None, None, :, None, :, None], axis=4)
    v = jnp.transpose(v, (1, 0, 2, 4, 3)).reshape(2, S5_GROUPS, S5_GROUP, S5_STATE)
    return v[0], -v[1]


def _state_cat(ref, c):
    w = SB_PER_CB * LANES
    return jnp.concatenate([ref[:, w * c:w * (c + 1)], ref[:, S5_NS + w * c:S5_NS + w * (c + 1)]], axis=1)


def _state_pair(ref, j):
    return jnp.concatenate([ref[:, _lanes(j)], ref[:, S5_NS + LANES * j:S5_NS + LANES * (j + 1)]], axis=1)


def s5_fwd(usg, bdb, cdb, dvec, abar_b, ptab_b, name):
    T = usg.shape[0]
    tc = _tile(T, SCAN_CHUNK)
    m = tc // SEGS
    assert ptab_b.shape == (2, m, SEGS, S5_NS) and m % 2 == 0

    def body(u_ref, bdb_ref, cdb_ref, d_ref, a_ref, p_ref, ys_ref, sre_ref, sim_ref, sbf_ref,
             src_re, src_im, dst_re, dst_im, cin_ref, carry_ref):
        i = pl.program_id(0)

        @pl.when(i == 0)
        def _():
            carry_ref[...] = jnp.zeros_like(carry_ref)

        u = u_ref[...]
        ub = u.astype(bf16)
        for j in range(NB_S5):
            bu = _dot(ub[:, _lanes(j // SB_PER_CB)], bdb_ref[j])
            src_re[:, _lanes(j)] = bu[:, :LANES]
            src_im[:, _lanes(j)] = bu[:, LANES:]
        for j0 in range(0, NB_S5, S5_JB):
            def kstep(k, st):
                rows = _step_rows(k)
                out = []
                for q in range(S5_JB):
                    ln = _lanes(j0 + q)
                    sr, si = st[2 * q], st[2 * q + 1]
                    ar, ai = a_ref[0, :, ln], a_ref[1, :, ln]
                    nr = ar * sr - ai * si + src_re[rows, ln]
                    ni = ar * si + ai * sr + src_im[rows, ln]
                    dst_re[rows, ln] = nr
                    dst_im[rows, ln] = ni
                    out += [nr, ni]
                return tuple(out)

            ends = lax.fori_loop(0, m, kstep, tuple(jnp.zeros((SEGS, LANES), f32) for _ in range(2 * S5_JB)))
            for q in range(S5_JB):
                ln = _lanes(j0 + q)
                er, ei = ends[2 * q], ends[2 * q + 1]
                cr, ci = carry_ref[0, :, ln], carry_ref[1, :, ln]
                amr, ami = p_ref[0, m - 1, 0:1, ln], p_ref[1, m - 1, 0:1, ln]
                rows_r, rows_i = [], []
                for s in range(SEGS):
                    rows_r.append(cr)
                    rows_i.append(ci)
                    cr, ci = (er[s:s + 1, :] + amr * cr - ami * ci, ei[s:s + 1, :] + amr * ci + ami * cr)
                cin_ref[0, 0:SEGS, ln] = _stack_rows(rows_r)
                cin_ref[1, 0:SEGS, ln] = _stack_rows(rows_i)
                carry_ref[0, :, ln] = cr
                carry_ref[1, :, ln] = ci
        cin_ref[:, SEGS:, :] = cin_ref[:, 0:SEGS, :]

        def fix(k2, _):
            rows = _step_rows(k2, PAIR)
            pr = p_ref[0, pl.ds(2 * k2, 2)].reshape(PAIR, S5_NS)
            pi = p_ref[1, pl.ds(2 * k2, 2)].reshape(PAIR, S5_NS)
            cr, ci = cin_ref[0], cin_ref[1]
            sr = dst_re[rows, :] + pr * cr - pi * ci
            si = dst_im[rows, :] + pr * ci + pi * cr
            sre_ref[rows, :] = sr
            sim_ref[rows, :] = si
            sbf_ref[rows, 0:S5_NS] = sr.astype(bf16)
            sbf_ref[rows, S5_NS:] = si.astype(bf16)
            return 0

        lax.fori_loop(0, m // 2, fix, 0)
        for c in range(CB_S5):
            ys_ref[:, _lanes(c)] = _dot(_state_cat(sbf_ref, c), cdb_ref[c]) + d_ref[:, _lanes(c)] * u[:, _lanes(c)]

    st = lambda w: _rows(tc, w)
    return pl.pallas_call(
        body, name=name, grid=(T // tc,),
        in_specs=[_rows(tc, S5_WIDTH, 0), _resident(bdb.shape), _resident(cdb.shape), _full((1, S5_WIDTH)),
                  _resident((2, SEGS, S5_NS)), _resident((2, m, SEGS, S5_NS))],
        out_specs=[st(S5_WIDTH), st(S5_NS), st(S5_NS), st(2 * S5_NS)],
        out_shape=[_S((T, S5_WIDTH)), _S((T, S5_NS)), _S((T, S5_NS)), _S((T, 2 * S5_NS), bf16)],
        scratch_shapes=[pltpu.VMEM((tc, S5_NS), f32)] * 4 + [pltpu.VMEM((2, PAIR, S5_NS), f32),
                                                             pltpu.VMEM((2, 1, S5_NS), f32)],
        compiler_params=_cp("arbitrary"),
    )(usg, bdb, cdb, dvec, abar_b, ptab_b)


def s5_bwd(gys, usg, s_re, s_im, s_bf, bdb, cdb, dvec, abar_b, ptab_rev_b, name):
    T = gys.shape[0]
    tc = _tile(T, SCAN_CHUNK)
    m = tc // SEGS
    nch = T // tc
    hb = tc // SUBLANES

    def body(gy_ref, u_ref, sre_ref, sim_ref, hre_ref, him_ref, sbf_ref, bdb_ref, cdb_ref, d_ref, a_ref, p_ref,
             gu_ref, gab_ref, gd_ref, gbdb_ref, gcdb_ref,
             src_re, src_im, dst_re, dst_im, lam_ref, cin_ref, acc_ref, carry_ref):
        i = pl.program_id(0)

        @pl.when(i == 0)
        def _():
            carry_ref[...] = jnp.zeros_like(carry_ref)
            for ref in (gab_ref, gd_ref, gbdb_ref, gcdb_ref):
                ref[...] = jnp.zeros_like(ref)

        first = i == nch - 1
        gy = gy_ref[...]
        gyb = gy.astype(bf16)
        u = u_ref[...]
        ub = u.astype(bf16)
        w = SB_PER_CB * LANES
        for c in range(CB_S5):
            gs = _dot_nt(gyb[:, _lanes(c)], cdb_ref[c])
            src_re[:, w * c:w * (c + 1)] = gs[:, :w]
            src_im[:, w * c:w * (c + 1)] = gs[:, w:]
            gcdb_ref[c] += _dot_tn(_state_cat(sbf_ref, c), gyb[:, _lanes(c)])
        for j0 in range(0, NB_S5, S5_JB):
            def kstep(kk, st):
                rows = _step_rows(m - 1 - kk)
                out = []
                for q in range(S5_JB):
                    ln = _lanes(j0 + q)
                    lr, li = st[2 * q], st[2 * q + 1]
                    ar, ai = a_ref[0, :, ln], a_ref[1, :, ln]
                    nr = ar * lr + ai * li + src_re[rows, ln]
                    ni = ar * li - ai * lr + src_im[rows, ln]
                    dst_re[rows, ln] = nr
                    dst_im[rows, ln] = ni
                    out += [nr, ni]
                return tuple(out)

            ends = lax.fori_loop(0, m, kstep, tuple(jnp.zeros((SEGS, LANES), f32) for _ in range(2 * S5_JB)))
            for q in range(S5_JB):
                ln = _lanes(j0 + q)
                er, ei = ends[2 * q], ends[2 * q + 1]
                cr, ci = carry_ref[0, :, ln], carry_ref[1, :, ln]
                amr, ami = p_ref[0, 0, 0:1, ln], p_ref[1, 0, 0:1, ln]
                rows_r, rows_i = [None] * SEGS, [None] * SEGS
                for s in reversed(range(SEGS)):
                    rows_r[s], rows_i[s] = cr, ci
                    cr, ci = (er[s:s + 1, :] + amr * cr + ami * ci, ei[s:s + 1, :] + amr * ci - ami * cr)
                cin_ref[0, 0:SEGS, ln] = _stack_rows(rows_r)
                cin_ref[1, 0:SEGS, ln] = _stack_rows(rows_i)
                carry_ref[0, :, ln] = cr
                carry_ref[1, :, ln] = ci
        cin_ref[:, SEGS:, :] = cin_ref[:, 0:SEGS, :]
        acc_ref[...] = jnp.zeros_like(acc_ref)

        def fix_rows(rows, k2, prev_re, prev_im):
            pr = p_ref[0, pl.ds(2 * k2, 2)].reshape(PAIR, S5_NS)
            pi = p_ref[1, pl.ds(2 * k2, 2)].reshape(PAIR, S5_NS)
            cr, ci = cin_ref[0], cin_ref[1]
            lr = dst_re[rows, :] + pr * cr + pi * ci
            li = dst_im[rows, :] + pr * ci - pi * cr
            lam_ref[rows, 0:S5_NS] = lr.astype(bf16)
            lam_ref[rows, S5_NS:] = li.astype(bf16)
            acc_ref[0] += lr * prev_re + li * prev_im
            acc_ref[1] += li * prev_re - lr * prev_im

        last = slice(tc - SUBLANES, tc)
        wrap_re = _down_a_segment(sre_ref[last, :], jnp.where(first, 0.0, hre_ref[SUBLANES - 1:SUBLANES, :]))
        wrap_im = _down_a_segment(sim_ref[last, :], jnp.where(first, 0.0, him_ref[SUBLANES - 1:SUBLANES, :]))
        fix_rows(pl.ds(0, PAIR), 0, jnp.concatenate([wrap_re, sre_ref[0:SUBLANES, :]], axis=0),
                 jnp.concatenate([wrap_im, sim_ref[0:SUBLANES, :]], axis=0))

        def fix(k2, _):
            prev = pl.ds(pl.multiple_of(k2 * PAIR - SUBLANES, SUBLANES), PAIR)
            fix_rows(_step_rows(k2, PAIR), k2, sre_ref[prev, :], sim_ref[prev, :])
            return 0

        lax.fori_loop(1, m // 2, fix, 0)
        gab_ref[0] += jnp.sum(acc_ref[0], axis=0, keepdims=True)
        gab_ref[1] += jnp.sum(acc_ref[1], axis=0, keepdims=True)
        for c in range(CB_S5):
            x = gy[:, _lanes(c)] * d_ref[:, _lanes(c)]
            for j in range(SB_PER_CB * c, SB_PER_CB * (c + 1)):
                pair = _state_pair(lam_ref, j)
                x = x + _dot_nt(pair, bdb_ref[j])
                gbdb_ref[j] += _dot_tn(ub[:, _lanes(c)], pair)
            gu_ref[:, _lanes(c)] = x.astype(bf16)
        gd_ref[...] += jnp.sum(gy * u, axis=0, keepdims=True)

    rev = lambda i: (nch - 1 - i, 0)
    halo = lambda i: (jnp.maximum((nch - 1 - i) * hb - 1, 0), 0)
    blk = lambda wd: pl.BlockSpec((tc, wd), rev)
    return pl.pallas_call(
        body, name=name, grid=(nch,),
        in_specs=[blk(S5_WIDTH), blk(S5_WIDTH), blk(S5_NS), blk(S5_NS),
                  pl.BlockSpec((SUBLANES, S5_NS), halo), pl.BlockSpec((SUBLANES, S5_NS), halo), blk(2 * S5_NS),
                  _resident(bdb.shape), _resident(cdb.shape), _full((1, S5_WIDTH)),
                  _resident((2, SEGS, S5_NS)), _resident((2, m, SEGS, S5_NS))],
        out_specs=[blk(S5_WIDTH), _full((2, 1, S5_NS)), _full((1, S5_WIDTH)), _full(bdb.shape), _full(cdb.shape)],
        out_shape=[_S((T, S5_WIDTH), bf16), _S((2, 1, S5_NS)), _S((1, S5_WIDTH)), _S(bdb.shape), _S(cdb.shape)],
        scratch_shapes=[pltpu.VMEM((tc, S5_NS), f32)] * 4 + [
            pltpu.VMEM((tc, 2 * S5_NS), bf16), pltpu.VMEM((2, PAIR, S5_NS), f32), pltpu.VMEM((2, PAIR, S5_NS), f32),
            pltpu.VMEM((2, 1, S5_NS), f32)],
        compiler_params=_cp("arbitrary"),
    )(gys, usg, s_re, s_im, s_re, s_im, s_bf, bdb, cdb, dvec, abar_b, ptab_rev_b)


def s5_post_fwd(ys, usg, wglu, wbs, name):
    T = ys.shape[0]
    tm = _tile(T, TOKEN_TILE)

    def body(ys_ref, sg_ref, wglu_ref, wbs_ref, glu_ref, zs_ref):
        glu = _dot(_gelu(ys_ref[...]).astype(bf16), wglu_ref[...])
        sg = sg_ref[...]
        y2 = glu[:, :S5_WIDTH] * _sig(glu[:, S5_WIDTH:]) * (sg * _sig(sg))
        glu_ref[...] = glu
        zs_ref[...] = _dot(y2.astype(bf16), wbs_ref[...])

    return pl.pallas_call(
        body, name=name, grid=(T // tm,),
        in_specs=[_rows(tm, S5_WIDTH), _rows(tm, S5_WIDTH, 1), _resident((S5_WIDTH, 2 * S5_WIDTH)),
                  _resident((S5_WIDTH, D_MODEL))],
        out_specs=[_rows(tm, 2 * S5_WIDTH), _rows(tm, D_MODEL)],
        out_shape=[_S((T, 2 * S5_WIDTH)), _S((T, D_MODEL))], compiler_params=_cp("parallel"),
    )(ys, usg, wglu, wbs)


def _accumulate(ref, part, step):
    @pl.when(step == 0)
    def _():
        ref[...] = part

    @pl.when(step > 0)
    def _():
        ref[...] += part


def s5_post_bwd(gzs, glu, usg, ys, wbs, wglu, name):
    T = ys.shape[0]
    tm = _tile(T, TOKEN_TILE)

    def body(gzs_ref, glu_ref, sg_ref, ys_ref, wbs_ref, wglu_ref, gys_ref, gsg_ref, gwbs_ref, gwglu_ref):
        i = pl.program_id(0)
        glu = glu_ref[...]
        a, b = glu[:, :S5_WIDTH], glu[:, S5_WIDTH:]
        sg = sg_ref[...]
        ys = ys_ref[...]
        sb, ssg = _sig(b), _sig(sg)
        silu = sg * ssg
        _accumulate(gwbs_ref, _dot_tn((a * sb * silu).astype(bf16), gzs_ref[...]), i)
        gy2 = _dot_nt(gzs_ref[...], wbs_ref[...])
        g_a = gy2 * sb * silu
        g_b = gy2 * a * sb * (1.0 - sb) * silu
        gsg_ref[...] = (gy2 * a * sb * ssg * (1.0 + sg * (1.0 - ssg))).astype(bf16)
        gglu = jnp.concatenate([g_a, g_b], axis=1).astype(bf16)
        _accumulate(gwglu_ref, _dot_tn(_gelu(ys).astype(bf16), gglu), i)
        gys_ref[...] = _dot_nt(gglu, wglu_ref[...]) * _gelu_grad(ys)

    return pl.pallas_call(
        body, name=name, grid=(T // tm,),
        in_specs=[_rows(tm, D_MODEL), _rows(tm, 2 * S5_WIDTH), _rows(tm, S5_WIDTH, 1), _rows(tm, S5_WIDTH),
                  _resident((S5_WIDTH, D_MODEL)), _resident((S5_WIDTH, 2 * S5_WIDTH))],
        out_specs=[_rows(tm, S5_WIDTH), _rows(tm, S5_WIDTH), _full((S5_WIDTH, D_MODEL)), _full((S5_WIDTH, 2 * S5_WIDTH))],
        out_shape=[_S((T, S5_WIDTH)), _S((T, S5_WIDTH), bf16), _S((S5_WIDTH, D_MODEL)), _S((S5_WIDTH, 2 * S5_WIDTH))],
        compiler_params=_cp("arbitrary"),
    )(gzs, glu, usg, ys, wbs, wglu)


NB_LRU = LRU_WIDTH // LANES
LRU_JB = 5
TAPS_BACK = CONV_WIDTH - 1
EDGE = TAPS_BACK * SUBLANES
HALO_ROWS = 4 * SUBLANES


def _down_a_segment(blk, entering_row):
    sub = lax.broadcasted_iota(jnp.int32, blk.shape, 0)
    return jnp.where(sub == 0, entering_row, pltpu.roll(blk, 1, 0))


def _up_a_segment(blk, entering_row):
    sub = lax.broadcasted_iota(jnp.int32, blk.shape, 0)
    return jnp.where(sub == SUBLANES - 1, entering_row, pltpu.roll(blk, SUBLANES - 1, 0))


def _stack_rows(rows):
    sub = lax.broadcasted_iota(jnp.int32, (SUBLANES,) + rows[0].shape[1:], 0)
    out = jnp.broadcast_to(rows[0], sub.shape)
    for s in range(1, SUBLANES):
        out = jnp.where(sub == s, rows[s], out)
    return out


def _fill_conv_window(xe, x_ref, xh_ref, is_first, tc):
    xe[EDGE:, :] = x_ref[...]
    for i in range(1, TAPS_BACK + 1):
        row = HALO_ROWS - SUBLANES * i + SUBLANES - 1
        entering = jnp.where(is_first, 0.0, xh_ref[row:row + 1, :])
        blk = x_ref[tc - SUBLANES * i:tc - SUBLANES * (i - 1), :]
        xe[EDGE - SUBLANES * i:EDGE - SUBLANES * (i - 1), :] = _down_a_segment(blk, entering)


def lru_fwd(lx, convw, convb, wa, wx, ba, bx, lam, name):
    T = lx.shape[0]
    tc = _tile(T, SCAN_CHUNK)
    m = tc // SEGS
    hb = tc // HALO_ROWS

    def body(x_ref, xh_ref, cw_ref, cb_ref, wa_ref, wx_ref, ba_ref, bx_ref, lam_ref,
             c_ref, r_ref, i_ref, h_ref, xe, src_a, src_b, dst_a, dst_h, cin_ref, carry_ref):
        i = pl.program_id(0)

        @pl.when(i == 0)
        def _():
            carry_ref[...] = jnp.zeros_like(carry_ref)

        _fill_conv_window(xe, x_ref, xh_ref, i == 0, tc)
        c = cb_ref[...] + cw_ref[0:1, :] * xe[0:tc, :]
        for k in range(1, CONV_WIDTH):
            c = c + cw_ref[k:k + 1, :] * xe[SUBLANES * k:SUBLANES * k + tc, :]
        c_ref[...] = c
        sp = _softplus_neg(lam_ref[...])
        for j in range(NB_LRU):
            ln = _lanes(j)
            cj = c[:, ln]
            cjb = cj.astype(bf16)
            r = _sig(_dot(cjb, wa_ref[j]) + ba_ref[:, ln])
            g = _sig(_dot(cjb, wx_ref[j]) + bx_ref[:, ln])
            r_ref[:, ln] = r
            i_ref[:, ln] = g
            log_a = -LRU_C * r * sp[:, ln]
            src_a[:, ln] = jnp.exp(log_a)
            src_b[:, ln] = jnp.sqrt(_one_minus_exp(2.0 * log_a)) * (g * cj)
        for j0 in range(0, NB_LRU, LRU_JB):
            def kstep(k, st):
                rows = _step_rows(k)
                out = []
                for q in range(LRU_JB):
                    ln = _lanes(j0 + q)
                    hh, ac = st[2 * q], st[2 * q + 1]
                    a = src_a[rows, ln]
                    hh = a * hh + src_b[rows, ln]
                    ac = a * ac
                    dst_h[rows, ln] = hh
                    dst_a[rows, ln] = ac
                    out += [hh, ac]
                return tuple(out)

            init = tuple(jnp.zeros((SEGS, LANES), f32) if q % 2 == 0 else jnp.ones((SEGS, LANES), f32)
                         for q in range(2 * LRU_JB))
            ends = lax.fori_loop(0, m, kstep, init)
            for q in range(LRU_JB):
                ln = _lanes(j0 + q)
                eh, ea = ends[2 * q], ends[2 * q + 1]
                cr = carry_ref[:, ln]
                rows_c = []
                for s in range(SEGS):
                    rows_c.append(cr)
                    cr = eh[s:s + 1, :] + ea[s:s + 1, :] * cr
                cin_ref[:, ln] = _stack_rows(rows_c)
                carry_ref[:, ln] = cr

        def fix(k, _):
            rows = _step_rows(k)
            h_ref[rows, :] = dst_h[rows, :] + dst_a[rows, :] * cin_ref[...]
            return 0

        lax.fori_loop(0, m, fix, 0)

    wide = lambda: _rows(tc, LRU_WIDTH)
    buf = lambda rows: pltpu.VMEM((rows, LRU_WIDTH), f32)
    return pl.pallas_call(
        body, name=name, grid=(T // tc,),
        in_specs=[wide(), pl.BlockSpec((HALO_ROWS, LRU_WIDTH), lambda i: (jnp.maximum(i * hb - 1, 0), 0)),
                  _full((CONV_WIDTH, LRU_WIDTH)), _full((1, LRU_WIDTH)),
                  _full((LRU_HEADS, LRU_HEAD_DIM, LRU_HEAD_DIM)), _full((LRU_HEADS, LRU_HEAD_DIM, LRU_HEAD_DIM)),
                  _full((1, LRU_WIDTH)), _full((1, LRU_WIDTH)), _full((1, LRU_WIDTH))],
        out_specs=[wide(), wide(), wide(), wide()],
        out_shape=[_S((T, LRU_WIDTH))] * 4,
        scratch_shapes=[buf(tc + EDGE), buf(tc), buf(tc), buf(tc), buf(tc), buf(SEGS), buf(1)],
        compiler_params=_cp("arbitrary"),
    )(lx, lx, convw, convb, wa, wx, ba, bx, lam)


def lru_bwd(gh, h, c, r, gi, lx, convw, wa, wx, lam, name):
    T = gh.shape[0]
    tc = _tile(T, SCAN_CHUNK)
    m = tc // SEGS
    nch = T // tc

    def body(gh_ref, h_ref, hh_ref, c_ref, r_ref, i_ref, x_ref, xh_ref, cw_ref, wa_ref, wx_ref, lam_ref,
             glx_ref, gwa_ref, gwx_ref, gba_ref, gbx_ref, glam_ref, gcb_ref, gcw_ref,
             src_a, src_m, dst_a, dst_m, mbuf, hbuf, xe, gce, cin_ref, gcc_ref, carry_ref):
        i = pl.program_id(0)

        @pl.when(i == 0)
        def _():
            carry_ref[...] = jnp.zeros_like(carry_ref)
            gcc_ref[...] = jnp.zeros_like(gcc_ref)
            for ref in (gwa_ref, gwx_ref, gba_ref, gbx_ref, glam_ref, gcb_ref, gcw_ref):
                ref[...] = jnp.zeros_like(ref)

        first = i == nch - 1
        last = slice(tc - SUBLANES, tc)
        hbuf[SUBLANES:, :] = h_ref[...]
        hbuf[0:SUBLANES, :] = _down_a_segment(h_ref[last, :], jnp.where(first, 0.0, hh_ref[SUBLANES - 1:SUBLANES, :]))
        _fill_conv_window(xe, x_ref, xh_ref, first, tc)
        lam_v = lam_ref[...]
        sp = _softplus_neg(lam_v)
        a_all = jnp.exp(-LRU_C * r_ref[...] * sp)
        src_a[...] = a_all
        src_m[...] = a_all * gh_ref[...]
        for j0 in range(0, NB_LRU, LRU_JB):
            def kstep(kk, st):
                rows = _step_rows(m - 1 - kk)
                out = []
                for q in range(LRU_JB):
                    ln = _lanes(j0 + q)
                    mu, ac = st[2 * q], st[2 * q + 1]
                    a = src_a[rows, ln]
                    mu = a * mu + src_m[rows, ln]
                    ac = a * ac
                    dst_m[rows, ln] = mu
                    dst_a[rows, ln] = ac
                    out += [mu, ac]
                return tuple(out)

            init = tuple(jnp.zeros((SEGS, LANES), f32) if q % 2 == 0 else jnp.ones((SEGS, LANES), f32)
                         for q in range(2 * LRU_JB))
            ends = lax.fori_loop(0, m, kstep, init)
            for q in range(LRU_JB):
                ln = _lanes(j0 + q)
                em, ea = ends[2 * q], ends[2 * q + 1]
                cr = carry_ref[:, ln]
                rows_c = [None] * SEGS
                for s in reversed(range(SEGS)):
                    rows_c[s] = cr
                    cr = em[s:s + 1, :] + ea[s:s + 1, :] * cr
                cin_ref[:, ln] = _stack_rows(rows_c)
                carry_ref[:, ln] = cr

        def fix(k, _):
            rows = _step_rows(k)
            mbuf[rows, :] = dst_m[rows, :] + dst_a[rows, :] * cin_ref[...]
            return 0

        lax.fori_loop(0, m, fix, 0)
        mbuf[tc:, :] = _up_a_segment(mbuf[0:SUBLANES, :], cin_ref[SUBLANES - 1:SUBLANES, :])
        sneg = _sig(-lam_v)
        for j in range(NB_LRU):
            ln = _lanes(j)
            lamt = gh_ref[:, ln] + mbuf[SUBLANES:, ln]
            rj, ij, cj = r_ref[:, ln], i_ref[:, ln], c_ref[:, ln]
            log_a = -LRU_C * rj * sp[:, ln]
            a = jnp.exp(log_a)
            mult = jnp.sqrt(_one_minus_exp(2.0 * log_a))
            g_a = lamt * hbuf[0:tc, ln]
            g_mult = lamt * ij * cj
            g_i = lamt * mult * cj
            g_c = lamt * mult * ij
            g_log_a = g_a * a - g_mult * a * a / mult
            glam_ref[:, ln] += jnp.sum(g_log_a * rj, axis=0, keepdims=True) * LRU_C * sneg[:, ln]
            g_ra = g_log_a * (-LRU_C) * sp[:, ln] * rj * (1.0 - rj)
            g_ia = g_i * ij * (1.0 - ij)
            gba_ref[:, ln] += jnp.sum(g_ra, axis=0, keepdims=True)
            gbx_ref[:, ln] += jnp.sum(g_ia, axis=0, keepdims=True)
            cjb, grb, gib = cj.astype(bf16), g_ra.astype(bf16), g_ia.astype(bf16)
            gwa_ref[j] += _dot_tn(cjb, grb)
            gwx_ref[j] += _dot_tn(cjb, gib)
            g_c = g_c + _dot_nt(grb, wa_ref[j]) + _dot_nt(gib, wx_ref[j])
            gce[0:tc, ln] = g_c
            gcb_ref[:, ln] += jnp.sum(g_c, axis=0, keepdims=True)
        for d in range(TAPS_BACK):
            blk = slice(SUBLANES * d, SUBLANES * (d + 1))
            gce[tc + SUBLANES * d:tc + SUBLANES * (d + 1), :] = _up_a_segment(gce[blk, :], gcc_ref[SUBLANES * d:SUBLANES * d + 1, :])
        gcc_ref[...] = gce[0:EDGE, :]
        gc = gce[0:tc, :]
        glx = cw_ref[CONV_WIDTH - 1:CONV_WIDTH, :] * gc
        gcw_ref[CONV_WIDTH - 1:CONV_WIDTH, :] += jnp.sum(gc * xe[EDGE:EDGE + tc, :], axis=0, keepdims=True)
        for k in range(CONV_WIDTH - 1):
            off = SUBLANES * (CONV_WIDTH - 1 - k)
            glx = glx + cw_ref[k:k + 1, :] * gce[off:off + tc, :]
            gcw_ref[k:k + 1, :] += jnp.sum(gc * xe[EDGE - off:EDGE - off + tc, :], axis=0, keepdims=True)
        glx_ref[...] = glx.astype(bf16)

    rev = lambda i: (nch - 1 - i, 0)
    halo = lambda rows: (lambda i: (jnp.maximum((nch - 1 - i) * (tc // rows) - 1, 0), 0))
    wide = lambda: pl.BlockSpec((tc, LRU_WIDTH), rev)
    vec = lambda: _full((1, LRU_WIDTH))
    hd = lambda: _full((LRU_HEADS, LRU_HEAD_DIM, LRU_HEAD_DIM))
    buf = lambda rows: pltpu.VMEM((rows, LRU_WIDTH), f32)
    return pl.pallas_call(
        body, name=name, grid=(nch,),
        in_specs=[wide(), wide(), pl.BlockSpec((SUBLANES, LRU_WIDTH), halo(SUBLANES)), wide(), wide(), wide(), wide(),
                  pl.BlockSpec((HALO_ROWS, LRU_WIDTH), halo(HALO_ROWS)), _full((CONV_WIDTH, LRU_WIDTH)), hd(), hd(), vec()],
        out_specs=[wide(), hd(), hd(), vec(), vec(), vec(), vec(), _full((CONV_WIDTH, LRU_WIDTH))],
        out_shape=[_S((T, LRU_WIDTH), bf16), _S((LRU_HEADS, LRU_HEAD_DIM, LRU_HEAD_DIM)),
                   _S((LRU_HEADS, LRU_HEAD_DIM, LRU_HEAD_DIM)), _S((1, LRU_WIDTH)), _S((1, LRU_WIDTH)),
                   _S((1, LRU_WIDTH)), _S((1, LRU_WIDTH)), _S((CONV_WIDTH, LRU_WIDTH))],
        scratch_shapes=[buf(tc), buf(tc), buf(tc), buf(tc), buf(tc + SUBLANES), buf(tc + SUBLANES), buf(tc + EDGE),
                        buf(tc + EDGE), buf(SEGS), buf(EDGE), buf(1)],
        compiler_params=_cp("arbitrary"),
    )(gh, h, h, c, r, gi, lx, lx, convw, wa, wx, lam)


def merge_fwd(h, lg, zs, gsl, x, p, wbl, wout, gpost, wple, wpg, name):
    T = x.shape[0]
    tm = _tile(T, TOKEN_TILE)

    def body(h_ref, lg_ref, zs_ref, gs_ref, gl_ref, x_ref, p_ref, wbl_ref, wout_ref, gp_ref, wple_ref, wpg_ref,
             zl_ref, mix_ref, xo_ref):
        lg_v = lg_ref[...]
        yl = h_ref[...] * (lg_v * _sig(lg_v))
        zl = _dot(yl.astype(bf16), wbl_ref[...])
        merged = _sig(gs_ref[...]) * zs_ref[...] + _sig(gl_ref[...]) * zl
        mix = _dot(merged.astype(bf16), wout_ref[...])
        r2 = lax.rsqrt(jnp.mean(mix * mix, axis=-1, keepdims=True) + NORM_EPS)
        x1 = x_ref[...] + mix * r2 * gp_ref[...]
        q = _dot(x1.astype(bf16), wpg_ref[...])
        pe = _dot(p_ref[...].astype(bf16), wple_ref[...])
        zl_ref[...], mix_ref[...] = zl, mix
        xo_ref[...] = x1 + pe * _sig(q)

    dm = lambda: _rows(tm, D_MODEL)
    return pl.pallas_call(
        body, name=name, grid=(T // tm,),
        in_specs=[_rows(tm, LRU_WIDTH), _rows(tm, LRU_WIDTH), dm(), _rows(tm, D_MODEL, 0), _rows(tm, D_MODEL, 1), dm(),
                  _rows(tm, PLE_DIM), _resident((LRU_WIDTH, D_MODEL)), _resident((D_MODEL, D_MODEL)), _full((1, D_MODEL)),
                  _resident((PLE_DIM, D_MODEL)), _resident((D_MODEL, D_MODEL))],
        out_specs=[dm(), dm(), dm()],
        out_shape=[_S((T, D_MODEL))] * 3, compiler_params=_cp("parallel"),
    )(h, lg, zs, gsl, gsl, x, p, wbl, wout, gpost, wple, wpg)


def post_bwd(gx2, mix, x, p, wpg, wple, gpost, name):
    T = x.shape[0]
    tm = _tile(T, TOKEN_TILE)

    def body(gx2_ref, mix_ref, x_ref, p_ref, wpg_ref, wple_ref, gp_ref, gres_ref, gmix_ref, ggp_ref, gwpg_ref, gwple_ref):
        i = pl.program_id(0)
        gx2 = gx2_ref[...]
        mix = mix_ref[...]
        gp = gp_ref[...]
        r2 = lax.rsqrt(jnp.mean(mix * mix, axis=-1, keepdims=True) + NORM_EPS)
        nrm = mix * r2
        x1b = (x_ref[...] + nrm * gp).astype(bf16)
        pb = p_ref[...].astype(bf16)
        sq = _sig(_dot(x1b, wpg_ref[...]))
        pe = _dot(pb, wple_ref[...])
        gq = (gx2 * pe * sq * (1.0 - sq)).astype(bf16)
        _accumulate(gwple_ref, _dot_tn(pb, (gx2 * sq).astype(bf16)), i)
        _accumulate(gwpg_ref, _dot_tn(x1b, gq), i)
        gx1 = gx2 + _dot_nt(gq, wpg_ref[...])
        gres_ref[...] = gx1
        _accumulate(ggp_ref, jnp.sum(gx1 * nrm, axis=0, keepdims=True), i)
        gy = gx1 * gp
        gmix_ref[...] = (r2 * (gy - nrm * jnp.mean(gy * nrm, axis=-1, keepdims=True))).astype(bf16)

    dm = lambda: _rows(tm, D_MODEL)
    return pl.pallas_call(
        body, name=name, grid=(T // tm,),
        in_specs=[dm(), dm(), dm(), _rows(tm, PLE_DIM), _resident((D_MODEL, D_MODEL)), _resident((PLE_DIM, D_MODEL)),
                  _full((1, D_MODEL))],
        out_specs=[dm(), dm(), _full((1, D_MODEL)), _full((D_MODEL, D_MODEL)), _full((PLE_DIM, D_MODEL))],
        out_shape=[_S((T, D_MODEL)), _S((T, D_MODEL), bf16), _S((1, D_MODEL)), _S((D_MODEL, D_MODEL)),
                   _S((PLE_DIM, D_MODEL))],
        compiler_params=_cp("arbitrary"),
    )(gx2, mix, x, p, wpg, wple, gpost)


def gate_bwd(gmix, zl, zs, gsl, h, lg, wout, wbl, name):
    T = zl.shape[0]
    tm = _tile(T, TOKEN_TILE)

    def body(gmix_ref, zl_ref, zs_ref, gs_ref, gl_ref, h_ref, lg_ref, wout_ref, wbl_ref,
             gzs_ref, ggsl_ref, gh_ref, glg_ref, gwout_ref, gwbl_ref):
        i = pl.program_id(0)
        gmix = gmix_ref[...]
        gmerged = _dot_nt(gmix, wout_ref[...])
        zs, zl = zs_ref[...], zl_ref[...]
        ss, sl = _sig(gs_ref[...]), _sig(gl_ref[...])
        _accumulate(gwout_ref, _dot_tn((ss * zs + sl * zl).astype(bf16), gmix), i)
        gzs_ref[...] = (gmerged * ss).astype(bf16)
        gzl = (gmerged * sl).astype(bf16)
        ggsl_ref[:, :D_MODEL] = (gmerged * zs * ss * (1.0 - ss)).astype(bf16)
        ggsl_ref[:, D_MODEL:] = (gmerged * zl * sl * (1.0 - sl)).astype(bf16)
        lg_v, hv = lg_ref[...], h_ref[...]
        slg = _sig(lg_v)
        silu = lg_v * slg
        _accumulate(gwbl_ref, _dot_tn((hv * silu).astype(bf16), gzl), i)
        gyl = _dot_nt(gzl, wbl_ref[...])
        gh_ref[...] = gyl * silu
        glg_ref[...] = (gyl * hv * slg * (1.0 + lg_v * (1.0 - slg))).astype(bf16)

    dm = lambda: _rows(tm, D_MODEL)
    lw = lambda: _rows(tm, LRU_WIDTH)
    return pl.pallas_call(
        body, name=name, grid=(T // tm,),
        in_specs=[dm(), dm(), dm(), _rows(tm, D_MODEL, 0), _rows(tm, D_MODEL, 1), lw(), lw(),
                  _resident((D_MODEL, D_MODEL)), _resident((LRU_WIDTH, D_MODEL))],
        out_specs=[dm(), _rows(tm, 2 * D_MODEL), lw(), lw(), _full((D_MODEL, D_MODEL)), _full((LRU_WIDTH, D_MODEL))],
        out_shape=[_S((T, D_MODEL), bf16), _S((T, 2 * D_MODEL), bf16), _S((T, LRU_WIDTH)), _S((T, LRU_WIDTH), bf16),
                   _S((D_MODEL, D_MODEL)), _S((LRU_WIDTH, D_MODEL))],
        compiler_params=_cp("arbitrary"),
    )(gmix, zl, zs, gsl, gsl, h, lg, wout, wbl)


def in_proj_bwd(pieces, win, x, gres, g, name):
    T = x.shape[0]
    tm = _tile(T, MM_TILE_M // 2)
    widths = [pc.shape[1] for pc in pieces]
    offs = [sum(widths[:k]) for k in range(len(widths))]

    def body(*refs):
        pc_refs = refs[:len(widths)]
        w_ref, x_ref, gres_ref, g_ref, gx_ref, gg_ref = refs[len(widths):]
        i = pl.program_id(0)
        ghv = _dot_nt(pc_refs[0][...], w_ref[:, offs[0]:offs[0] + widths[0]])
        for k in range(1, len(widths)):
            ghv = ghv + _dot_nt(pc_refs[k][...], w_ref[:, offs[k]:offs[k] + widths[k]])
        xv = x_ref[...]
        r = lax.rsqrt(jnp.mean(xv * xv, axis=-1, keepdims=True) + NORM_EPS)
        nrm = xv * r
        gy = ghv * g_ref[...]
        gx_ref[...] = gres_ref[...] + r * (gy - nrm * jnp.mean(gy * nrm, axis=-1, keepdims=True))
        _accumulate(gg_ref, jnp.sum(ghv * nrm, axis=0, keepdims=True), i)

    dm = lambda: _rows(tm, D_MODEL)
    return pl.pallas_call(
        body, name=name, grid=(T // tm,),
        in_specs=[_rows(tm, wd) for wd in widths] + [_resident(win.shape), dm(), dm(), _full((1, D_MODEL))],
        out_specs=[dm(), _full((1, D_MODEL))],
        out_shape=[_S((T, D_MODEL)), _S((1, D_MODEL))], compiler_params=_cp("arbitrary"),
    )(*pieces, win, x, gres, g)


def loss_head(y, target, name):
    T = y.shape[0]
    tm = _tile(T, TOKEN_TILE)

    def body(y_ref, t_ref, l_ref, g_ref):
        i = pl.program_id(0)
        e = y_ref[...] - t_ref[...]
        g_ref[...] = e * (1.0 / D_MODEL)
        part = 0.5 * jnp.sum(jnp.sum(e * e, axis=-1, keepdims=True) * (1.0 / D_MODEL), axis=0, keepdims=True)

        @pl.when(i == 0)
        def _():
            l_ref[...] = part

        @pl.when(i > 0)
        def _():
            l_ref[...] += part

    return pl.pallas_call(
        body, name=name, grid=(T // tm,),
        in_specs=[_rows(tm, D_MODEL), _rows(tm, D_MODEL)], out_specs=[_full((1, 1)), _rows(tm, D_MODEL)],
        out_shape=[_S((1, 1)), _S((T, D_MODEL))],
        compiler_params=_cp("arbitrary"),
    )(y, target)


def _s5_operands(w, m, tag):
    b_re_t = jnp.transpose(w['s5_b_re'], (2, 0, 1))
    b_im_t = jnp.transpose(w['s5_b_im'], (2, 0, 1))
    ldt = w['s5_log_dt'][:, None]
    ab, pw, bb = s5_prep(w['s5_a_re'], w['s5_a_im'], ldt, b_re_t, b_im_t, m, "s5_prep" + tag)
    over_sublanes = lambda t: jnp.broadcast_to(t[..., None, :], t.shape[:-1] + (SEGS, S5_NS))
    ptab = pw.reshape(2, m, S5_NS)
    return dict(abar_b=over_sublanes(ab.reshape(2, S5_NS)), ptab_b=over_sublanes(ptab),
                ptab_rev_b=over_sublanes(ptab[:, ::-1, :]), bdb=_pack_bdb(bb).astype(bf16),
                cdb=_pack_cdb(w['s5_c_re'], w['s5_c_im']).astype(bf16), dvec=w['s5_d'][None, :],
                prep_in=(w['s5_a_re'], w['s5_a_im'], ldt, b_re_t, b_im_t))


def layer_fwd(x, p, w, tag):
    T = x.shape[0]
    m = _tile(T, SCAN_CHUNK) // SEGS
    s5 = _s5_operands(w, m, tag)
    h_bf = rms_fwd(x, w['g_pre'][None, :], "rms_fwd" + tag)
    win = w['w_in']
    usg = mm_nn(h_bf, win[:, :2 * S5_WIDTH], "proj_s5" + tag)
    lx = mm_nn(h_bf, win[:, 2 * S5_WIDTH:2 * S5_WIDTH + LRU_WIDTH], "proj_lx" + tag)
    lg = mm_nn(h_bf, win[:, 2 * S5_WIDTH + LRU_WIDTH:2 * S5_WIDTH + 2 * LRU_WIDTH], "proj_lg" + tag)
    gsl = mm_nn(h_bf, win[:, 2 * S5_WIDTH + 2 * LRU_WIDTH:], "proj_gate" + tag)
    ys, s_re, s_im, s_bf = s5_fwd(usg, s5['bdb'], s5['cdb'], s5['dvec'], s5['abar_b'], s5['ptab_b'], "s5_fwd" + tag)
    glu, zs = s5_post_fwd(ys, usg, w['w_glu'], w['w_bs'], "s5_post_fwd" + tag)
    wa, wx = w['lru_w_a'].astype(bf16), w['lru_w_x'].astype(bf16)
    c, r, gi, hs = lru_fwd(lx, w['conv_w'], w['conv_b'][None, :], wa, wx, w['lru_b_a'][None, :], w['lru_b_x'][None, :],
                           w['lru_lambda'][None, :], "lru_fwd" + tag)
    zl, mix, x_out = merge_fwd(hs, lg, zs, gsl, x, p, w['w_bl'], w['w_out'], w['g_post'][None, :],
                               w['w_ple'], w['w_ple_gate'], "merge_fwd" + tag)
    saved = dict(x=x, p=p, h_bf=h_bf, usg=usg, lx=lx, lg=lg, gsl=gsl, ys=ys, s_re=s_re, s_im=s_im, s_bf=s_bf, glu=glu,
                 zs=zs, c=c, r=r, gi=gi, hs=hs, zl=zl, mix=mix, s5=s5, wa=wa, wx=wx)
    return x_out, saved


def layer_bwd(gx_out, w, sv, tag):
    s5 = sv['s5']
    g = {}
    gres, gmix, g_gpost, g['w_ple_gate'], g['w_ple'] = post_bwd(
        gx_out, sv['mix'], sv['x'], sv['p'], w['w_ple_gate'], w['w_ple'], w['g_post'][None, :], "post_bwd" + tag)
    gzs, ggsl, g_h, g_lg, g['w_out'], g['w_bl'] = gate_bwd(
        gmix, sv['zl'], sv['zs'], sv['gsl'], sv['hs'], sv['lg'], w['w_out'], w['w_bl'], "gate_bwd" + tag)
    g['g_post'] = g_gpost[0]
    (g_lx, g_wa, g_wx, g_ba, g_bx, g_lam, g_cb, g_cw) = lru_bwd(
        g_h, sv['hs'], sv['c'], sv['r'], sv['gi'], sv['lx'], w['conv_w'], sv['wa'], sv['wx'],
        w['lru_lambda'][None, :], "lru_bwd" + tag)
    g['lru_w_a'], g['lru_w_x'] = g_wa, g_wx
    g['lru_b_a'], g['lru_b_x'], g['lru_lambda'], g['conv_b'], g['conv_w'] = g_ba[0], g_bx[0], g_lam[0], g_cb[0], g_cw
    g_ys, g_sg, g['w_bs'], g['w_glu'] = s5_post_bwd(gzs, sv['glu'], sv['usg'], sv['ys'], w['w_bs'], w['w_glu'],
                                                    "s5_post_bwd" + tag)
    g_u, g_ab, g_d, g_bdb, g_cdb = s5_bwd(g_ys, sv['usg'], sv['s_re'], sv['s_im'], sv['s_bf'], s5['bdb'], s5['cdb'],
                                          s5['dvec'], s5['abar_b'], s5['ptab_rev_b'], "s5_bwd" + tag)
    g['s5_d'] = g_d[0]
    g['s5_c_re'], g['s5_c_im'] = _unpack_cdb(g_cdb)
    g_are, g_aim, g_ldt, g_bre_t, g_bim_t = s5_prep_bwd(*s5['prep_in'], g_ab.reshape(2, S5_GROUPS, S5_STATE),
                                                       _unpack_bdb(g_bdb), "s5_prep_bwd" + tag)
    g['s5_a_re'], g['s5_a_im'], g['s5_log_dt'] = g_are, g_aim, g_ldt
    g['s5_b_re'] = jnp.transpose(g_bre_t, (1, 2, 0))
    g['s5_b_im'] = jnp.transpose(g_bim_t, (1, 2, 0))
    pieces = [g_u, g_sg, g_lx, g_lg, ggsl]
    g['w_in'] = jnp.concatenate([mm_tn(sv['h_bf'], pc, "gw_in%d%s" % (k, tag)) for k, pc in enumerate(pieces)], axis=1)
    gx, g_gpre = in_proj_bwd(pieces, w['w_in'], sv['x'], gres, w['g_pre'][None, :], "in_proj_bwd" + tag)
    g['g_pre'] = g_gpre[0]
    return gx, g


def _as_2d(a):
    return a.reshape((-1, a.shape[-1])) if a.ndim > 1 else a.reshape((1, -1))


def _adamw_update(w, gv, m, v):
    nm = ADAM_B1 * m + (1.0 - ADAM_B1) * gv
    nv = ADAM_B2 * v + (1.0 - ADAM_B2) * (gv * gv)
    bc1 = 1.0 - ADAM_B1 ** ADAM_STEP
    bc2 = 1.0 - ADAM_B2 ** ADAM_STEP
    return -ADAM_LR * ((nm / bc1) / (jnp.sqrt(nv / bc2) + ADAM_EPS) + ADAM_WD * w), nm, nv


def adamw(w, g, m, v, name):
    shape = w.shape
    w2, g2, m2, v2 = _as_2d(w), _as_2d(g), _as_2d(m), _as_2d(v)
    R, C = w2.shape
    tr = _row_tile(R, C)

    def body(w_ref, g_ref, m_ref, v_ref, d_ref, nm_ref, nv_ref):
        d_ref[...], nm_ref[...], nv_ref[...] = _adamw_update(w_ref[...], g_ref[...], m_ref[...], v_ref[...])

    spec = lambda: pl.BlockSpec((tr, C), lambda i: (i, 0))
    d, nm, nv = pl.pallas_call(
        body, name=name, grid=(R // tr,), in_specs=[spec() for _ in range(4)], out_specs=[spec() for _ in range(3)],
        out_shape=[_S((R, C))] * 3, compiler_params=_cp("parallel"),
    )(w2, g2, m2, v2)
    return d.reshape(shape), nm.reshape(shape), nv.reshape(shape)


def adamw_reduce(w, a, theirs, m, v, chip, name):
    shape = w.shape
    w2, m2, v2 = _as_2d(w), _as_2d(m), _as_2d(v)
    R, C = w2.shape
    a3, t3 = a.reshape(4, R, C), theirs.reshape(3, R, C)
    tr = _row_tile(R, C)

    def body(chip_ref, w_ref, a_ref, t_ref, m_ref, v_ref, g_ref, d_ref, nm_ref, nv_ref):
        gv = ((a_ref[0] + t_ref[0].astype(f32)) + t_ref[1].astype(f32)) + t_ref[2].astype(f32)
        g_ref[...] = gv
        d_ref[...], nm_ref[...], nv_ref[...] = _adamw_update(w_ref[...], gv, m_ref[...], v_ref[...])

    spec = lambda: pl.BlockSpec((tr, C), lambda i, c: (i, 0))
    grid_spec = pltpu.PrefetchScalarGridSpec(
        num_scalar_prefetch=1, grid=(R // tr,),
        in_specs=[spec(), pl.BlockSpec((1, tr, C), lambda i, c: (c[0], i, 0)), pl.BlockSpec((3, tr, C), lambda i, c: (0, i, 0)),
                  spec(), spec()],
        out_specs=[spec() for _ in range(4)])
    g, d, nm, nv = pl.pallas_call(
        body, name=name, grid_spec=grid_spec, out_shape=[_S((R, C))] * 4, compiler_params=_cp("parallel"),
    )(chip, w2, a3, t3, m2, v2)
    return g.reshape(shape), d.reshape(shape), nm.reshape(shape), nv.reshape(shape)


MESH = pl.DeviceIdType.MESH
ANY = pl.BlockSpec(memory_space=pl.ANY)


def _place():
    return lax.axis_index("x"), lax.axis_index("y"), lax.axis_index("c")


def _other_chips(mx, my):
    return [(1 - mx, my), (mx, 1 - my), (1 - mx, 1 - my)]


def all_gather(shards, name):
    nb = len(shards)

    def body(*refs):
        x_refs, out_refs, send_sems, recv_sems, local_sems = refs[:nb], refs[nb:2 * nb], *refs[2 * nb:]
        mx, my, mc = _place()
        me, sibling = (mx, my, mc), (mx, my, 1 - mc)
        chips = _other_chips(mx, my)

        def copy(b, k, block, to, src=None):
            px, py, pc = block
            rows = out_refs[b].at[4 * px + 2 * py + pc]
            return pltpu.make_async_remote_copy(
                src_ref=rows if src is None else src, dst_ref=rows, send_sem=send_sems.at[b, k],
                recv_sem=recv_sems.at[b, k], device_id=to, device_id_type=MESH)

        first = []
        for b in range(nb):
            first += [copy(b, 1 + j, me, (*chip, mc), src=x_refs[b]) for j, chip in enumerate(chips)]
        for b in range(nb):
            first.append(copy(b, 0, me, sibling, src=x_refs[b]))
        for cp in first:
            cp.start()
        mine = [pltpu.make_async_copy(x_refs[b], out_refs[b].at[4 * mx + 2 * my + mc], local_sems.at[b]) for b in range(nb)]
        for cp in mine:
            cp.start()
        passed = []
        for j, chip in enumerate(chips):
            for b in range(nb):
                copy(b, 1 + j, (*chip, mc), me).wait_recv()
                passed.append(copy(b, 4 + j, (*chip, mc), sibling))
                passed[-1].start()
        for b in range(nb):
            copy(b, 0, sibling, me).wait_recv()
            for j, chip in enumerate(chips):
                copy(b, 4 + j, (*chip, 1 - mc), me).wait_recv()
        for cp in first + passed:
            cp.wait_send()
        for cp in mine:
            cp.wait()

    outs = pl.pallas_call(
        body, name=name, out_shape=[_S((N_DEV,) + s.shape, s.dtype) for s in shards], in_specs=[ANY] * nb,
        out_specs=[ANY] * nb,
        scratch_shapes=[pltpu.SemaphoreType.DMA((nb, 7)), pltpu.SemaphoreType.DMA((nb, 7)), pltpu.SemaphoreType.DMA((nb,))],
    )(*shards)
    return list(outs)


def exchange_sibling(gs, name):
    nb = len(gs)

    def body(*refs):
        g_refs, recv_refs, send_sems, recv_sems = refs[:nb], refs[nb:2 * nb], refs[2 * nb], refs[2 * nb + 1]
        mx, my, mc = _place()
        copies = [pltpu.make_async_remote_copy(
            src_ref=g_refs[b].at[2 * k + 1 - mc], dst_ref=recv_refs[b].at[k], send_sem=send_sems.at[b, k],
            recv_sem=recv_sems.at[b, k], device_id=(mx, my, 1 - mc), device_id_type=MESH)
            for b in range(nb) for k in range(4)]
        for cp in copies:
            cp.start()
        for cp in copies:
            cp.wait()

    outs = pl.pallas_call(
        body, name=name, out_shape=[_S((4,) + g.shape[1:], g.dtype) for g in gs], in_specs=[ANY] * nb,
        out_specs=[ANY] * nb,
        scratch_shapes=[pltpu.SemaphoreType.DMA((nb, 4)), pltpu.SemaphoreType.DMA((nb, 4))],
    )(*gs)
    return list(outs)


def exchange_chips(parts, name):
    nb = len(parts)

    def body(*refs):
        a_refs, recv_refs, send_sems, recv_sems = refs[:nb], refs[nb:2 * nb], refs[2 * nb], refs[2 * nb + 1]
        mx, my, mc = _place()
        copies = [pltpu.make_async_remote_copy(
            src_ref=a_refs[b].at[2 * px + py], dst_ref=recv_refs[b].at[j], send_sem=send_sems.at[b, j],
            recv_sem=recv_sems.at[b, j], device_id=(px, py, mc), device_id_type=MESH)
            for b in range(nb) for j, (px, py) in enumerate(_other_chips(mx, my))]
        for cp in copies:
            cp.start()
        for cp in copies:
            cp.wait()

    outs = pl.pallas_call(
        body, name=name, out_shape=[_S((3,) + a.shape[1:], a.dtype) for a in parts], in_specs=[ANY] * nb,
        out_specs=[ANY] * nb,
        scratch_shapes=[pltpu.SemaphoreType.DMA((nb, 3)), pltpu.SemaphoreType.DMA((nb, 3))],
    )(*parts)
    return list(outs)


def add_sibling(g, theirs, core, name, wire_dtype=f32):
    shp = theirs.shape
    C = shp[-1]
    R = math.prod(shp[1:-1])
    tr = _row_tile(R, C)
    narrow = wire_dtype != f32

    def body(core_ref, g_ref, t_ref, o_ref, *wire_ref):
        s = g_ref[...] + t_ref[...]
        o_ref[...] = s
        if narrow:
            wire_ref[0][...] = s.astype(wire_dtype)

    blk = lambda: pl.BlockSpec((1, tr, C), lambda k, i, c: (k, i, 0))
    grid_spec = pltpu.PrefetchScalarGridSpec(
        num_scalar_prefetch=1, grid=(4, R // tr),
        in_specs=[pl.BlockSpec((1, tr, C), lambda k, i, c: (2 * k + c[0], i, 0)), blk()],
        out_specs=[blk(), blk()] if narrow else [blk()])
    outs = pl.pallas_call(
        body, name=name, grid_spec=grid_spec,
        out_shape=[_S((4, R, C), f32)] + ([_S((4, R, C), wire_dtype)] if narrow else []),
        compiler_params=_cp("parallel", "parallel"),
    )(core, g.reshape(N_DEV, R, C), theirs.reshape(4, R, C))
    part = outs[0].reshape(shp)
    return part, (outs[1].reshape(shp) if narrow else part)


def add_chips(a, theirs, chip, name):
    _, R, C = a.shape
    tr = _row_tile(R, C)

    def body(chip_ref, a_ref, t_ref, out_ref):
        out_ref[...] = ((a_ref[0] + t_ref[0]) + t_ref[1]) + t_ref[2]

    grid_spec = pltpu.PrefetchScalarGridSpec(
        num_scalar_prefetch=1, grid=(R // tr,),
        in_specs=[pl.BlockSpec((1, tr, C), lambda i, c: (c[0], i, 0)), pl.BlockSpec((3, tr, C), lambda i, c: (0, i, 0))],
        out_specs=pl.BlockSpec((tr, C), lambda i, c: (i, 0)))
    return pl.pallas_call(
        body, name=name, grid_spec=grid_spec, out_shape=_S((R, C), a.dtype), compiler_params=_cp("parallel"),
    )(chip, a, theirs)


def _round_up(n, q):
    return (n + q - 1) // q * q


def _lane_rows(a):
    flat = a.reshape(-1)
    n = _round_up(flat.shape[0], SUBLANES * LANES)
    return jnp.pad(flat, (0, n - flat.shape[0])).reshape(-1, LANES)


def _full_to_shards(full, axis):
    shp = full.shape
    s = shp[axis] // N_DEV
    cut = full.reshape(shp[:axis] + (N_DEV, s) + shp[axis + 1:])
    return jnp.moveaxis(cut, axis, 0)


def _shards_to_full(parts, axis):
    shp = list(parts.shape[1:])
    shp[axis] *= N_DEV
    return jnp.moveaxis(parts, 0, axis).reshape(tuple(shp))


def kernel(x, p, g_pre, w_in, s5_a_re, s5_a_im, s5_log_dt, s5_b_re, s5_b_im, s5_c_re, s5_c_im, s5_d, w_glu, w_bs, conv_w, conv_b, lru_w_a, lru_b_a, lru_w_x, lru_b_x, lru_lambda, w_bl, w_out, g_post, w_ple, w_ple_gate, loss_target, m_g_pre, m_w_in, m_s5_a_re, m_s5_a_im, m_s5_log_dt, m_s5_b_re, m_s5_b_im, m_s5_c_re, m_s5_c_im, m_s5_d, m_w_glu, m_w_bs, m_conv_w, m_conv_b, m_lru_w_a, m_lru_b_a, m_lru_w_x, m_lru_b_x, m_lru_lambda, m_w_bl, m_w_out, m_g_post, m_w_ple, m_w_ple_gate, v_g_pre, v_w_in, v_s5_a_re, v_s5_a_im, v_s5_log_dt, v_s5_b_re, v_s5_b_im, v_s5_c_re, v_s5_c_im, v_s5_d, v_w_glu, v_w_bs, v_conv_w, v_conv_b, v_lru_w_a, v_lru_b_a, v_lru_w_x, v_lru_b_x, v_lru_lambda, v_w_bl, v_w_out, v_g_post, v_w_ple, v_w_ple_gate):
    given = dict(locals())
    W = {n: given[n] for n in WEIGHTS}
    M = {n: given["m_" + n] for n in WEIGHTS}
    V = {n: given["v_" + n] for n in WEIGHTS}
    xs, target = to_scan_order(x[0]), to_scan_order(loss_target[0])
    ps = [to_scan_order(p[i, 0]) for i in range(DEPTH)]

    mx, my, mc = _place()
    core = jnp.reshape(mc, (1,)).astype(jnp.int32)
    chip = jnp.reshape(2 * mx + my, (1,)).astype(jnp.int32)

    names = list(SHARDED)
    shards = [W[n].astype(bf16) if n in GATHER_BF16 else W[n] for n in names]
    gathered = all_gather(shards, "comm_gather_weights")
    full = {n: _shards_to_full(g, SHARDED[n]) for n, g in zip(names, gathered)}

    def layer_weights(i):
        return {n: (full[n][i] if n in SHARDED else W[n][i]) for n in WEIGHTS}

    act, saved = xs, []
    for i in range(DEPTH):
        act, sv = layer_fwd(act, ps[i], layer_weights(i), "_l%d" % i)
        saved.append(sv)
    loss_part, gact = loss_head(act, target, "loss_head")
    grads = [None] * DEPTH
    for i in reversed(range(DEPTH)):
        gact, grads[i] = layer_bwd(gact, layer_weights(i), saved[i], "_l%d" % i)
    loss = lax.psum(loss_part[0, 0], ("x", "y", "c"))
    gfull = {n: jnp.stack([grads[i][n].reshape(full[n].shape[1:] if n in SHARDED else W[n].shape[1:])
                           for i in range(DEPTH)]) for n in WEIGHTS}

    rep_rows = [_lane_rows(gfull[n]) for n in REPLICATED]
    n_rows = sum(r.shape[0] for r in rep_rows)
    pad_rows = _round_up(n_rows, N_DEV * SUBLANES) - n_rows
    rep_blocks = jnp.concatenate(rep_rows + [jnp.zeros((pad_rows, LANES), f32)]).reshape(N_DEV, -1, LANES)
    blocks = [_full_to_shards(gfull[n], SHARDED[n]) for n in names] + [rep_blocks]
    theirs = exchange_sibling(blocks, "comm_reduce_sibling")
    parts, wire = [], []
    for k, (b, t) in enumerate(zip(blocks, theirs)):
        part, sent = add_sibling(b, t, core, "reduce_add_sibling_%d" % k, bf16 if k < len(names) else f32)
        parts.append(part)
        wire.append(sent)
    others = exchange_chips(wire, "comm_reduce_chips")

    red, deltas, new_m, new_v = {}, {}, {}, {}
    for n, a, t in zip(names, parts, others):
        red[n], deltas[n], new_m[n], new_v[n] = adamw_reduce(W[n], a, t, M[n], V[n], chip, "adamw_" + n)
    piece = add_chips(parts[-1], others[-1], chip, "reduce_add_chips")
    rep_all = all_gather([piece], "comm_gather_replicated")[0].reshape(-1, LANES)
    off = 0
    for n, rows in zip(REPLICATED, rep_rows):
        k = math.prod(W[n].shape)
        red[n] = rep_all[off:off + rows.shape[0]].reshape(-1)[:k].reshape(W[n].shape)
        off += rows.shape[0]
        deltas[n], new_m[n], new_v[n] = adamw(W[n], red[n], M[n], V[n], "adamw_" + n)
    return (loss, from_scan_order(gact)[None], *[red[n] for n in WEIGHTS], *[deltas[n] for n in WEIGHTS],
            *[new_m[n] for n in WEIGHTS], *[new_v[n] for n in WEIGHTS])
```

```python
import math

import jax
import jax.numpy as jnp
from jax import lax
from jax.experimental import pallas as pl
from jax.experimental.pallas import tpu as pltpu

f32 = jnp.float32
bf16 = jnp.bfloat16

D_MODEL = 1024
DEPTH = 2
PLE_DIM = 256
NORM_EPS = 1e-6
S5_WIDTH = 512
S5_GROUP = 16
S5_GROUPS = 32
S5_STATE = 64
S5_NS = S5_GROUPS * S5_STATE
LRU_WIDTH = 1280
LRU_HEADS = 10
LRU_HEAD_DIM = 128
LRU_C = 8.0
CONV_WIDTH = 4
N_DEV = 8

ADAM_LR = 0.001
ADAM_B1 = 0.9
ADAM_B2 = 0.999
ADAM_EPS = 1e-08
ADAM_WD = 0.01
ADAM_STEP = 10

LANES = 128
SUBLANES = 8
SEGS = SUBLANES
SCAN_CHUNK = 256
TOKEN_TILE = 256
MM_TILE_M = 1024
MM_TILE_N = 1408
MM_TILE_K_ROWS = 1280
PAIR = 2 * SUBLANES
VMEM_LIMIT_BYTES = 56 * 1024 * 1024
ELEMENTWISE_BLOCK_BYTES = 1024 * 1024

WEIGHTS = ['g_pre', 'w_in', 's5_a_re', 's5_a_im', 's5_log_dt', 's5_b_re', 's5_b_im', 's5_c_re', 's5_c_im',
           's5_d', 'w_glu', 'w_bs', 'conv_w', 'conv_b', 'lru_w_a', 'lru_b_a', 'lru_w_x', 'lru_b_x',
           'lru_lambda', 'w_bl', 'w_out', 'g_post', 'w_ple', 'w_ple_gate']
SHARDED = {'w_in': 2, 'w_glu': 2, 'w_bs': 2, 'conv_w': 2, 'w_bl': 1, 'w_out': 1, 'w_ple': 2, 'w_ple_gate': 1}
GATHER_BF16 = ['w_in', 'w_glu', 'w_bs', 'w_bl', 'w_out', 'w_ple', 'w_ple_gate']
REPLICATED = [n for n in WEIGHTS if n not in SHARDED]


def _sig(x):
    return 0.5 * jnp.tanh(0.5 * x) + 0.5


def _gelu_parts(x):
    k = math.sqrt(2.0 / math.pi)
    t = jnp.tanh(k * (x + 0.044715 * x * x * x))
    return t, k


def _gelu(x):
    t, _ = _gelu_parts(x)
    return 0.5 * x * (1.0 + t)


def _gelu_grad(x):
    t, k = _gelu_parts(x)
    return 0.5 * (1.0 + t) + 0.5 * x * (1.0 - t * t) * k * (1.0 + 3.0 * 0.044715 * x * x)


def _one_minus_sq(a, log_a):
    z = 2.0 * log_a
    series = -z * (1.0 + z * (0.5 + z * (1.0 / 6.0 + z * (1.0 / 24.0 + z * (1.0 / 120.0)))))
    return jnp.where(z > -0.05, series, 1.0 - a * a)


def _softplus_neg(lam):
    return jnp.maximum(-lam, 0.0) + jnp.log(1.0 + jnp.exp(-jnp.abs(lam)))


def _dot(a, b):
    return jnp.dot(a, b, preferred_element_type=f32)


def _dot_nt(a, b):
    return lax.dot_general(a, b, (((1,), (1,)), ((), ())), preferred_element_type=f32)


def _dot_tn(a, b):
    return lax.dot_general(a, b, (((0,), (0,)), ((), ())), preferred_element_type=f32)


def _S(shape, dtype=f32):
    return jax.ShapeDtypeStruct(shape, dtype)


def _full(shape):
    nd = len(shape)
    return pl.BlockSpec(shape, lambda *_: (0,) * nd)


def _rows(tile, width, col=0):
    return pl.BlockSpec((tile, width), lambda i: (i, col))


def _cp(*semantics):
    return pltpu.CompilerParams(dimension_semantics=semantics or None, vmem_limit_bytes=VMEM_LIMIT_BYTES)


def _tile(n, want):
    t = min(n, want)
    assert n % t == 0, (n, want)
    return t


def _row_tile(R, C=LANES):
    cap = max(SUBLANES, min(R, ELEMENTWISE_BLOCK_BYTES // (4 * C)))
    for t in range(cap - cap % SUBLANES, 0, -SUBLANES):
        if R % t == 0:
            return t
    return R


def _lanes(j):
    return slice(LANES * j, LANES * (j + 1))


def _step_rows(k, n=SUBLANES):
    return pl.ds(pl.multiple_of(k * n, n), n)


def to_scan_order(a):
    T, C = a.shape
    tc = _tile(T, SCAN_CHUNK)
    return a.reshape(T // tc, SEGS, tc // SEGS, C).transpose(0, 2, 1, 3).reshape(T, C)


def from_scan_order(a):
    T, C = a.shape
    tc = _tile(T, SCAN_CHUNK)
    return a.reshape(T // tc, tc // SEGS, SEGS, C).transpose(0, 2, 1, 3).reshape(T, C)


def _col_tile(n, cap):
    if n <= cap:
        return n
    for t in range(cap - cap % LANES, 0, -LANES):
        if n % t == 0:
            return t
    return n


def _resident(shape):
    nd = len(shape)
    return pl.BlockSpec(shape, lambda *_: (0,) * nd, pipeline_mode=pl.Buffered(1))


def mm_nn(a, b, name, out_dtype=f32):
    M, K = a.shape
    N = b.shape[1]
    tm, tn = _tile(M, MM_TILE_M), _col_tile(N, MM_TILE_N)

    def body(a_ref, b_ref, o_ref):
        o_ref[...] = _dot(a_ref[...].astype(bf16), b_ref[...].astype(bf16)).astype(out_dtype)

    return pl.pallas_call(
        body, name=name, grid=(M // tm, N // tn),
        in_specs=[pl.BlockSpec((tm, K), lambda i, j: (i, 0)), pl.BlockSpec((K, tn), lambda i, j: (0, j))],
        out_specs=pl.BlockSpec((tm, tn), lambda i, j: (i, j)),
        out_shape=_S((M, N), out_dtype), compiler_params=_cp("parallel", "parallel"),
    )(a, b)


def mm_tn(a, b, name):
    M, K = a.shape
    N = b.shape[1]
    tm, tk, tn = _tile(M, MM_TILE_M), _col_tile(K, MM_TILE_K_ROWS), _col_tile(N, MM_TILE_N)

    def body(a_ref, b_ref, o_ref):
        m = pl.program_id(2)
        part = _dot_tn(a_ref[...].astype(bf16), b_ref[...].astype(bf16))

        @pl.when(m == 0)
        def _():
            o_ref[...] = part

        @pl.when(m > 0)
        def _():
            o_ref[...] += part

    return pl.pallas_call(
        body, name=name, grid=(K // tk, N // tn, M // tm),
        in_specs=[pl.BlockSpec((tm, tk), lambda i, j, m: (m, i)), pl.BlockSpec((tm, tn), lambda i, j, m: (m, j))],
        out_specs=pl.BlockSpec((tk, tn), lambda i, j, m: (i, j)),
        out_shape=_S((K, N), f32),
        compiler_params=_cp("parallel", "parallel", "arbitrary"),
    )(a, b)


def rms_fwd(x, g, name):
    T = x.shape[0]
    tm = _tile(T, TOKEN_TILE)

    def body(x_ref, g_ref, h_ref):
        xv = x_ref[...]
        r = lax.rsqrt(jnp.mean(xv * xv, axis=-1, keepdims=True) + NORM_EPS)
        h_ref[...] = (xv * r * g_ref[...]).astype(bf16)

    return pl.pallas_call(
        body, name=name, grid=(T // tm,),
        in_specs=[_rows(tm, D_MODEL), _full((1, D_MODEL))], out_specs=_rows(tm, D_MODEL),
        out_shape=_S((T, D_MODEL), bf16), compiler_params=_cp("parallel"),
    )(x, g)


def _s5_discretise(a_re, a_im, log_dt, b_re_t, b_im_t):
    dt = jnp.exp(log_dt)
    mag = jnp.exp(a_re * dt)
    ab_re = mag * jnp.cos(a_im * dt)
    ab_im = mag * jnp.sin(a_im * dt)
    den = a_re * a_re + a_im * a_im
    nr, ni = ab_re - 1.0, ab_im
    z_re = (nr * a_re + ni * a_im) / den
    z_im = (ni * a_re - nr * a_im) / den
    bb_re = z_re[None] * b_re_t - z_im[None] * b_im_t
    bb_im = z_re[None] * b_im_t + z_im[None] * b_re_t
    return ab_re, ab_im, bb_re, bb_im


def s5_prep(a_re, a_im, log_dt, b_re_t, b_im_t, m, name):
    G, N = a_re.shape

    def body(are_ref, aim_ref, ldt_ref, bre_ref, bim_ref, ab_ref, pw_ref, bb_ref):
        are, aim, ldt = are_ref[...], aim_ref[...], ldt_ref[...]
        ab_re, ab_im, bb_re, bb_im = _s5_discretise(are, aim, ldt, bre_ref[...], bim_ref[...])
        ab_ref[0], ab_ref[1] = ab_re, ab_im
        bb_ref[0], bb_ref[1] = bb_re, bb_im
        dt = jnp.exp(ldt)
        for k in range(m):
            mag = jnp.exp(are * dt * (k + 1.0))
            pw_ref[0, k] = mag * jnp.cos(aim * dt * (k + 1.0))
            pw_ref[1, k] = mag * jnp.sin(aim * dt * (k + 1.0))

    return pl.pallas_call(
        body, name=name,
        out_shape=[_S((2, G, N)), _S((2, m, G, N)), _S((2, S5_GROUP, G, N))], compiler_params=_cp(),
    )(a_re, a_im, log_dt, b_re_t, b_im_t)


def s5_prep_bwd(a_re, a_im, log_dt, b_re_t, b_im_t, g_ab, g_bb, name):
    G, N = a_re.shape

    def body(are_ref, aim_ref, ldt_ref, bre_ref, bim_ref, gab_ref, gbb_ref, o_are, o_aim, o_ldt, o_bre, o_bim):
        _, vjp = jax.vjp(_s5_discretise, are_ref[...], aim_ref[...], ldt_ref[...], bre_ref[...], bim_ref[...])
        g_are, g_aim, g_ldt, g_bre, g_bim = vjp((gab_ref[0], gab_ref[1], gbb_ref[0], gbb_ref[1]))
        o_are[...], o_aim[...], o_ldt[...], o_bre[...], o_bim[...] = g_are, g_aim, g_ldt, g_bre, g_bim

    return pl.pallas_call(
        body, name=name,
        out_shape=[_S((G, N)), _S((G, N)), _S((G, 1)), _S((S5_GROUP, G, N)), _S((S5_GROUP, G, N))],
        compiler_params=_cp(),
    )(a_re, a_im, log_dt, b_re_t, b_im_t, g_ab, g_bb)


NB_S5 = S5_NS // LANES
CB_S5 = S5_WIDTH // LANES
SB_PER_CB = NB_S5 // CB_S5
GRP_PER_SB = LANES // S5_STATE
S5_JB = 8


def _bdb_mask():
    j = jnp.arange(NB_S5)
    own_rows = (j[:, None] % SB_PER_CB == jnp.arange(SB_PER_CB)[None, :]).astype(f32)
    eye = jnp.eye(GRP_PER_SB, dtype=f32)
    return own_rows[:, :, None, None, None, None, None] * eye[None, None, :, None, None, :, None]


def _pack_bdb(bb):
    v = jnp.transpose(bb.reshape(2, S5_GROUP, NB_S5, GRP_PER_SB, S5_STATE), (2, 3, 1, 0, 4))
    full = v[:, None, :, :, :, None, :] * _bdb_mask()
    return full.reshape(NB_S5, LANES, 2 * LANES)


def _unpack_bdb(g_bdb):
    g7 = g_bdb.reshape(NB_S5, SB_PER_CB, GRP_PER_SB, S5_GROUP, 2, GRP_PER_SB, S5_STATE)
    v = jnp.sum(g7 * _bdb_mask(), axis=(1, 5))
    return jnp.transpose(v, (3, 2, 0, 1, 4)).reshape(2, S5_GROUP, S5_GROUPS, S5_STATE)


def _pack_cdb(c_re, c_im):
    gl = S5_GROUPS // CB_S5
    c2 = jnp.stack([c_re, -c_im]).reshape(2, CB_S5, gl, S5_GROUP, S5_STATE)
    eye = jnp.eye(gl, dtype=f32)
    full = jnp.transpose(c2, (1, 0, 2, 4, 3))[:, :, :, :, None, :] * eye[None, None, :, None, :, None]
    return full.reshape(CB_S5, 2 * SB_PER_CB * LANES, LANES)


def _unpack_cdb(g_cdb):
    gl = S5_GROUPS // CB_S5
    g6 = g_cdb.reshape(CB_S5, 2, gl, S5_STATE, gl, S5_GROUP)
    eye = jnp.eye(gl, dtype=f32)
    v = jnp.sum(g6 * eye[None, None, :, None, :, None], axis=4)
    v = jnp.transpose(v, (1, 0, 2, 4, 3)).reshape(2, S5_GROUPS, S5_GROUP, S5_STATE)
    return v[0], -v[1]


def _state_cat(ref, c):
    w = SB_PER_CB * LANES
    return jnp.concatenate([ref[:, w * c:w * (c + 1)], ref[:, S5_NS + w * c:S5_NS + w * (c + 1)]], axis=1)


def _state_pair(ref, j):
    return jnp.concatenate([ref[:, _lanes(j)], ref[:, S5_NS + LANES * j:S5_NS + LANES * (j + 1)]], axis=1)


def s5_fwd(usg, bdb, cdb, dvec, abar_b, ptab_b, name):
    T = usg.shape[0]
    tc = _tile(T, SCAN_CHUNK)
    m = tc // SEGS
    assert ptab_b.shape == (2, m, SEGS, S5_NS) and m % 2 == 0

    def body(u_ref, bdb_ref, cdb_ref, d_ref, a_ref, p_ref, ys_ref, sre_ref, sim_ref, sbf_ref,
             src_re, src_im, dst_re, dst_im, cin_ref, carry_ref):
        i = pl.program_id(0)

        @pl.when(i == 0)
        def _():
            carry_ref[...] = jnp.zeros_like(carry_ref)

        u = u_ref[...]
        ub = u.astype(bf16)
        for j in range(NB_S5):
            bu = _dot(ub[:, _lanes(j // SB_PER_CB)], bdb_ref[j])
            src_re[:, _lanes(j)] = bu[:, :LANES]
            src_im[:, _lanes(j)] = bu[:, LANES:]
        for j0 in range(0, NB_S5, S5_JB):
            def kstep(k, st):
                rows = _step_rows(k)
                out = []
                for q in range(S5_JB):
                    ln = _lanes(j0 + q)
                    sr, si = st[2 * q], st[2 * q + 1]
                    ar, ai = a_ref[0, :, ln], a_ref[1, :, ln]
                    nr = ar * sr - ai * si + src_re[rows, ln]
                    ni = ar * si + ai * sr + src_im[rows, ln]
                    dst_re[rows, ln] = nr
                    dst_im[rows, ln] = ni
                    out += [nr, ni]
                return tuple(out)

            ends = lax.fori_loop(0, m, kstep, tuple(jnp.zeros((SEGS, LANES), f32) for _ in range(2 * S5_JB)))
            for q in range(S5_JB):
                ln = _lanes(j0 + q)
                er, ei = ends[2 * q], ends[2 * q + 1]
                cr, ci = carry_ref[0, :, ln], carry_ref[1, :, ln]
                amr, ami = p_ref[0, m - 1, 0:1, ln], p_ref[1, m - 1, 0:1, ln]
                rows_r, rows_i = [], []
                for s in range(SEGS):
                    rows_r.append(cr)
                    rows_i.append(ci)
                    cr, ci = (er[s:s + 1, :] + amr * cr - ami * ci, ei[s:s + 1, :] + amr * ci + ami * cr)
                cin_ref[0, 0:SEGS, ln] = _stack_rows(rows_r)
                cin_ref[1, 0:SEGS, ln] = _stack_rows(rows_i)
                carry_ref[0, :, ln] = cr
                carry_ref[1, :, ln] = ci
        cin_ref[:, SEGS:, :] = cin_ref[:, 0:SEGS, :]

        def fix(k2, _):
            rows = _step_rows(k2, PAIR)
            pr = p_ref[0, pl.ds(2 * k2, 2)].reshape(PAIR, S5_NS)
            pi = p_ref[1, pl.ds(2 * k2, 2)].reshape(PAIR, S5_NS)
            cr, ci = cin_ref[0], cin_ref[1]
            sr = dst_re[rows, :] + pr * cr - pi * ci
            si = dst_im[rows, :] + pr * ci + pi * cr
            sre_ref[rows, :] = sr
            sim_ref[rows, :] = si
            sbf_ref[rows, 0:S5_NS] = sr.astype(bf16)
            sbf_ref[rows, S5_NS:] = si.astype(bf16)
            return 0

        lax.fori_loop(0, m // 2, fix, 0)
        for c in range(CB_S5):
            ys_ref[:, _lanes(c)] = _dot(_state_cat(sbf_ref, c), cdb_ref[c]) + d_ref[:, _lanes(c)] * u[:, _lanes(c)]

    st = lambda w: _rows(tc, w)
    return pl.pallas_call(
        body, name=name, grid=(T // tc,),
        in_specs=[_rows(tc, S5_WIDTH, 0), _resident(bdb.shape), _resident(cdb.shape), _full((1, S5_WIDTH)),
                  _resident((2, SEGS, S5_NS)), _resident((2, m, SEGS, S5_NS))],
        out_specs=[st(S5_WIDTH), st(S5_NS), st(S5_NS), st(2 * S5_NS)],
        out_shape=[_S((T, S5_WIDTH)), _S((T, S5_NS)), _S((T, S5_NS)), _S((T, 2 * S5_NS), bf16)],
        scratch_shapes=[pltpu.VMEM((tc, S5_NS), f32)] * 4 + [pltpu.VMEM((2, PAIR, S5_NS), f32),
                                                             pltpu.VMEM((2, 1, S5_NS), f32)],
        compiler_params=_cp("arbitrary"),
    )(usg, bdb, cdb, dvec, abar_b, ptab_b)


def s5_bwd(gys, usg, s_re, s_im, s_bf, bdb, cdb, dvec, abar_b, ptab_rev_b, name):
    T = gys.shape[0]
    tc = _tile(T, SCAN_CHUNK)
    m = tc // SEGS
    nch = T // tc
    hb = tc // SUBLANES

    def body(gy_ref, u_ref, sre_ref, sim_ref, hre_ref, him_ref, sbf_ref, bdb_ref, cdb_ref, d_ref, a_ref, p_ref,
             gu_ref, gab_ref, gd_ref, gbdb_ref, gcdb_ref,
             src_re, src_im, dst_re, dst_im, lam_ref, cin_ref, acc_ref, carry_ref):
        i = pl.program_id(0)

        @pl.when(i == 0)
        def _():
            carry_ref[...] = jnp.zeros_like(carry_ref)
            for ref in (gab_ref, gd_ref, gbdb_ref, gcdb_ref):
                ref[...] = jnp.zeros_like(ref)

        first = i == nch - 1
        gy = gy_ref[...]
        gyb = gy.astype(bf16)
        u = u_ref[...]
        ub = u.astype(bf16)
        w = SB_PER_CB * LANES
        for c in range(CB_S5):
            gs = _dot_nt(gyb[:, _lanes(c)], cdb_ref[c])
            src_re[:, w * c:w * (c + 1)] = gs[:, :w]
            src_im[:, w * c:w * (c + 1)] = gs[:, w:]
            gcdb_ref[c] += _dot_tn(_state_cat(sbf_ref, c), gyb[:, _lanes(c)])
        for j0 in range(0, NB_S5, S5_JB):
            def kstep(kk, st):
                rows = _step_rows(m - 1 - kk)
                out = []
                for q in range(S5_JB):
                    ln = _lanes(j0 + q)
                    lr, li = st[2 * q], st[2 * q + 1]
                    ar, ai = a_ref[0, :, ln], a_ref[1, :, ln]
                    nr = ar * lr + ai * li + src_re[rows, ln]
                    ni = ar * li - ai * lr + src_im[rows, ln]
                    dst_re[rows, ln] = nr
                    dst_im[rows, ln] = ni
                    out += [nr, ni]
                return tuple(out)

            ends = lax.fori_loop(0, m, kstep, tuple(jnp.zeros((SEGS, LANES), f32) for _ in range(2 * S5_JB)))
            for q in range(S5_JB):
                ln = _lanes(j0 + q)
                er, ei = ends[2 * q], ends[2 * q + 1]
                cr, ci = carry_ref[0, :, ln], carry_ref[1, :, ln]
                amr, ami = p_ref[0, 0, 0:1, ln], p_ref[1, 0, 0:1, ln]
                rows_r, rows_i = [None] * SEGS, [None] * SEGS
                for s in reversed(range(SEGS)):
                    rows_r[s], rows_i[s] = cr, ci
                    cr, ci = (er[s:s + 1, :] + amr * cr + ami * ci, ei[s:s + 1, :] + amr * ci - ami * cr)
                cin_ref[0, 0:SEGS, ln] = _stack_rows(rows_r)
                cin_ref[1, 0:SEGS, ln] = _stack_rows(rows_i)
                carry_ref[0, :, ln] = cr
                carry_ref[1, :, ln] = ci
        cin_ref[:, SEGS:, :] = cin_ref[:, 0:SEGS, :]
        acc_ref[...] = jnp.zeros_like(acc_ref)

        def fix_rows(rows, k2, prev_re, prev_im):
            pr = p_ref[0, pl.ds(2 * k2, 2)].reshape(PAIR, S5_NS)
            pi = p_ref[1, pl.ds(2 * k2, 2)].reshape(PAIR, S5_NS)
            cr, ci = cin_ref[0], cin_ref[1]
            lr = dst_re[rows, :] + pr * cr + pi * ci
            li = dst_im[rows, :] + pr * ci - pi * cr
            lam_ref[rows, 0:S5_NS] = lr.astype(bf16)
            lam_ref[rows, S5_NS:] = li.astype(bf16)
            acc_ref[0] += lr * prev_re + li * prev_im
            acc_ref[1] += li * prev_re - lr * prev_im

        last = slice(tc - SUBLANES, tc)
        wrap_re = _down_a_segment(sre_ref[last, :], jnp.where(first, 0.0, hre_ref[SUBLANES - 1:SUBLANES, :]))
        wrap_im = _down_a_segment(sim_ref[last, :], jnp.where(first, 0.0, him_ref[SUBLANES - 1:SUBLANES, :]))
        fix_rows(pl.ds(0, PAIR), 0, jnp.concatenate([wrap_re, sre_ref[0:SUBLANES, :]], axis=0),
                 jnp.concatenate([wrap_im, sim_ref[0:SUBLANES, :]], axis=0))

        def fix(k2, _):
            prev = pl.ds(pl.multiple_of(k2 * PAIR - SUBLANES, SUBLANES), PAIR)
            fix_rows(_step_rows(k2, PAIR), k2, sre_ref[prev, :], sim_ref[prev, :])
            return 0

        lax.fori_loop(1, m // 2, fix, 0)
        gab_ref[0] += jnp.sum(acc_ref[0], axis=0, keepdims=True)
        gab_ref[1] += jnp.sum(acc_ref[1], axis=0, keepdims=True)
        for c in range(CB_S5):
            x = gy[:, _lanes(c)] * d_ref[:, _lanes(c)]
            for j in range(SB_PER_CB * c, SB_PER_CB * (c + 1)):
                pair = _state_pair(lam_ref, j)
                x = x + _dot_nt(pair, bdb_ref[j])
                gbdb_ref[j] += _dot_tn(ub[:, _lanes(c)], pair)
            gu_ref[:, _lanes(c)] = x.astype(bf16)
        gd_ref[...] += jnp.sum(gy * u, axis=0, keepdims=True)

    rev = lambda i: (nch - 1 - i, 0)
    halo = lambda i: (jnp.maximum((nch - 1 - i) * hb - 1, 0), 0)
    blk = lambda wd: pl.BlockSpec((tc, wd), rev)
    return pl.pallas_call(
        body, name=name, grid=(nch,),
        in_specs=[blk(S5_WIDTH), blk(S5_WIDTH), blk(S5_NS), blk(S5_NS),
                  pl.BlockSpec((SUBLANES, S5_NS), halo), pl.BlockSpec((SUBLANES, S5_NS), halo), blk(2 * S5_NS),
                  _resident(bdb.shape), _resident(cdb.shape), _full((1, S5_WIDTH)),
                  _resident((2, SEGS, S5_NS)), _resident((2, m, SEGS, S5_NS))],
        out_specs=[blk(S5_WIDTH), _full((2, 1, S5_NS)), _full((1, S5_WIDTH)), _full(bdb.shape), _full(cdb.shape)],
        out_shape=[_S((T, S5_WIDTH), bf16), _S((2, 1, S5_NS)), _S((1, S5_WIDTH)), _S(bdb.shape), _S(cdb.shape)],
        scratch_shapes=[pltpu.VMEM((tc, S5_NS), f32)] * 4 + [
            pltpu.VMEM((tc, 2 * S5_NS), bf16), pltpu.VMEM((2, PAIR, S5_NS), f32), pltpu.VMEM((2, PAIR, S5_NS), f32),
            pltpu.VMEM((2, 1, S5_NS), f32)],
        compiler_params=_cp("arbitrary"),
    )(gys, usg, s_re, s_im, s_re, s_im, s_bf, bdb, cdb, dvec, abar_b, ptab_rev_b)


def s5_post_fwd(ys, usg, wglu, wbs, name):
    T = ys.shape[0]
    tm = _tile(T, TOKEN_TILE)

    def body(ys_ref, sg_ref, wglu_ref, wbs_ref, glu_ref, zs_ref):
        glu = _dot(_gelu(ys_ref[...]).astype(bf16), wglu_ref[...])
        sg = sg_ref[...]
        y2 = glu[:, :S5_WIDTH] * _sig(glu[:, S5_WIDTH:]) * (sg * _sig(sg))
        glu_ref[...] = glu
        zs_ref[...] = _dot(y2.astype(bf16), wbs_ref[...])

    return pl.pallas_call(
        body, name=name, grid=(T // tm,),
        in_specs=[_rows(tm, S5_WIDTH), _rows(tm, S5_WIDTH, 1), _resident((S5_WIDTH, 2 * S5_WIDTH)),
                  _resident((S5_WIDTH, D_MODEL))],
        out_specs=[_rows(tm, 2 * S5_WIDTH), _rows(tm, D_MODEL)],
        out_shape=[_S((T, 2 * S5_WIDTH)), _S((T, D_MODEL))], compiler_params=_cp("parallel"),
    )(ys, usg, wglu, wbs)


def _accumulate(ref, part, step):
    @pl.when(step == 0)
    def _():
        ref[...] = part

    @pl.when(step > 0)
    def _():
        ref[...] += part


def s5_post_bwd(gzs, glu, usg, ys, wbs, wglu, name):
    T = ys.shape[0]
    tm = _tile(T, TOKEN_TILE)

    def body(gzs_ref, glu_ref, sg_ref, ys_ref, wbs_ref, wglu_ref, gys_ref, gsg_ref, gwbs_ref, gwglu_ref):
        i = pl.program_id(0)
        glu = glu_ref[...]
        a, b = glu[:, :S5_WIDTH], glu[:, S5_WIDTH:]
        sg = sg_ref[...]
        ys = ys_ref[...]
        sb, ssg = _sig(b), _sig(sg)
        silu = sg * ssg
        _accumulate(gwbs_ref, _dot_tn((a * sb * silu).astype(bf16), gzs_ref[...]), i)
        gy2 = _dot_nt(gzs_ref[...], wbs_ref[...])
        g_a = gy2 * sb * silu
        g_b = gy2 * a * sb * (1.0 - sb) * silu
        gsg_ref[...] = (gy2 * a * sb * ssg * (1.0 + sg * (1.0 - ssg))).astype(bf16)
        gglu = jnp.concatenate([g_a, g_b], axis=1).astype(bf16)
        _accumulate(gwglu_ref, _dot_tn(_gelu(ys).astype(bf16), gglu), i)
        gys_ref[...] = _dot_nt(gglu, wglu_ref[...]) * _gelu_grad(ys)

    return pl.pallas_call(
        body, name=name, grid=(T // tm,),
        in_specs=[_rows(tm, D_MODEL), _rows(tm, 2 * S5_WIDTH), _rows(tm, S5_WIDTH, 1), _rows(tm, S5_WIDTH),
                  _resident((S5_WIDTH, D_MODEL)), _resident((S5_WIDTH, 2 * S5_WIDTH))],
        out_specs=[_rows(tm, S5_WIDTH), _rows(tm, S5_WIDTH), _full((S5_WIDTH, D_MODEL)), _full((S5_WIDTH, 2 * S5_WIDTH))],
        out_shape=[_S((T, S5_WIDTH)), _S((T, S5_WIDTH), bf16), _S((S5_WIDTH, D_MODEL)), _S((S5_WIDTH, 2 * S5_WIDTH))],
        compiler_params=_cp("arbitrary"),
    )(gzs, glu, usg, ys, wbs, wglu)


NB_LRU = LRU_WIDTH // LANES
LRU_JB = 5
TAPS_BACK = CONV_WIDTH - 1
EDGE = TAPS_BACK * SUBLANES
HALO_ROWS = 4 * SUBLANES


def _down_a_segment(blk, entering_row):
    sub = lax.broadcasted_iota(jnp.int32, blk.shape, 0)
    return jnp.where(sub == 0, entering_row, pltpu.roll(blk, 1, 0))


def _up_a_segment(blk, entering_row):
    sub = lax.broadcasted_iota(jnp.int32, blk.shape, 0)
    return jnp.where(sub == SUBLANES - 1, entering_row, pltpu.roll(blk, SUBLANES - 1, 0))


def _stack_rows(rows):
    sub = lax.broadcasted_iota(jnp.int32, (SUBLANES,) + rows[0].shape[1:], 0)
    out = jnp.broadcast_to(rows[0], sub.shape)
    for s in range(1, SUBLANES):
        out = jnp.where(sub == s, rows[s], out)
    return out


def _fill_conv_window(xe, x_ref, xh_ref, is_first, tc):
    xe[EDGE:, :] = x_ref[...]
    for i in range(1, TAPS_BACK + 1):
        row = HALO_ROWS - SUBLANES * i + SUBLANES - 1
        entering = jnp.where(is_first, 0.0, xh_ref[row:row + 1, :])
        blk = x_ref[tc - SUBLANES * i:tc - SUBLANES * (i - 1), :]
        xe[EDGE - SUBLANES * i:EDGE - SUBLANES * (i - 1), :] = _down_a_segment(blk, entering)


def lru_fwd(lx, convw, convb, wa, wx, ba, bx, lam, name):
    T = lx.shape[0]
    tc = _tile(T, SCAN_CHUNK)
    m = tc // SEGS
    hb = tc // HALO_ROWS

    def body(x_ref, xh_ref, cw_ref, cb_ref, wa_ref, wx_ref, ba_ref, bx_ref, lam_ref,
             c_ref, r_ref, i_ref, h_ref, xe, src_a, src_b, dst_a, dst_h, cin_ref, carry_ref):
        i = pl.program_id(0)

        @pl.when(i == 0)
        def _():
            carry_ref[...] = jnp.zeros_like(carry_ref)

        _fill_conv_window(xe, x_ref, xh_ref, i == 0, tc)
        c = cb_ref[...] + cw_ref[0:1, :] * xe[0:tc, :]
        for k in range(1, CONV_WIDTH):
            c = c + cw_ref[k:k + 1, :] * xe[SUBLANES * k:SUBLANES * k + tc, :]
        c_ref[...] = c
        sp = _softplus_neg(lam_ref[...])
        for j in range(NB_LRU):
            ln = _lanes(j)
            cj = c[:, ln]
            cjb = cj.astype(bf16)
            r = _sig(_dot(cjb, wa_ref[j]) + ba_ref[:, ln])
            g = _sig(_dot(cjb, wx_ref[j]) + bx_ref[:, ln])
            r_ref[:, ln] = r
            i_ref[:, ln] = g
            log_a = -LRU_C * r * sp[:, ln]
            a = jnp.exp(log_a)
            src_a[:, ln] = a
            src_b[:, ln] = jnp.sqrt(_one_minus_sq(a, log_a)) * (g * cj)
        for j0 in range(0, NB_LRU, LRU_JB):
            def kstep(k, st):
                rows = _step_rows(k)
                out = []
                for q in range(LRU_JB):
                    ln = _lanes(j0 + q)
                    hh, ac = st[2 * q], st[2 * q + 1]
                    a = src_a[rows, ln]
                    hh = a * hh + src_b[rows, ln]
                    ac = a * ac
                    dst_h[rows, ln] = hh
                    dst_a[rows, ln] = ac
                    out += [hh, ac]
                return tuple(out)

            init = tuple(jnp.zeros((SEGS, LANES), f32) if q % 2 == 0 else jnp.ones((SEGS, LANES), f32)
                         for q in range(2 * LRU_JB))
            ends = lax.fori_loop(0, m, kstep, init)
            for q in range(LRU_JB):
                ln = _lanes(j0 + q)
                eh, ea = ends[2 * q], ends[2 * q + 1]
                cr = carry_ref[:, ln]
                rows_c = []
                for s in range(SEGS):
                    rows_c.append(cr)
                    cr = eh[s:s + 1, :] + ea[s:s + 1, :] * cr
                cin_ref[:, ln] = _stack_rows(rows_c)
                carry_ref[:, ln] = cr

        def fix(k, _):
            rows = _step_rows(k)
            h_ref[rows, :] = dst_h[rows, :] + dst_a[rows, :] * cin_ref[...]
            return 0

        lax.fori_loop(0, m, fix, 0)

    wide = lambda: _rows(tc, LRU_WIDTH)
    buf = lambda rows: pltpu.VMEM((rows, LRU_WIDTH), f32)
    return pl.pallas_call(
        body, name=name, grid=(T // tc,),
        in_specs=[wide(), pl.BlockSpec((HALO_ROWS, LRU_WIDTH), lambda i: (jnp.maximum(i * hb - 1, 0), 0)),
                  _full((CONV_WIDTH, LRU_WIDTH)), _full((1, LRU_WIDTH)),
                  _full((LRU_HEADS, LRU_HEAD_DIM, LRU_HEAD_DIM)), _full((LRU_HEADS, LRU_HEAD_DIM, LRU_HEAD_DIM)),
                  _full((1, LRU_WIDTH)), _full((1, LRU_WIDTH)), _full((1, LRU_WIDTH))],
        out_specs=[wide(), wide(), wide(), wide()],
        out_shape=[_S((T, LRU_WIDTH))] * 4,
        scratch_shapes=[buf(tc + EDGE), buf(tc), buf(tc), buf(tc), buf(tc), buf(SEGS), buf(1)],
        compiler_params=_cp("arbitrary"),
    )(lx, lx, convw, convb, wa, wx, ba, bx, lam)


def lru_bwd(gh, h, c, r, gi, lx, convw, wa, wx, lam, name):
    T = gh.shape[0]
    tc = _tile(T, SCAN_CHUNK)
    m = tc // SEGS
    nch = T // tc

    def body(gh_ref, h_ref, hh_ref, c_ref, r_ref, i_ref, x_ref, xh_ref, cw_ref, wa_ref, wx_ref, lam_ref,
             glx_ref, gwa_ref, gwx_ref, gba_ref, gbx_ref, glam_ref, gcb_ref, gcw_ref,
             src_a, src_m, dst_a, dst_m, mbuf, hbuf, xe, gce, cin_ref, gcc_ref, carry_ref):
        i = pl.program_id(0)

        @pl.when(i == 0)
        def _():
            carry_ref[...] = jnp.zeros_like(carry_ref)
            gcc_ref[...] = jnp.zeros_like(gcc_ref)
            for ref in (gwa_ref, gwx_ref, gba_ref, gbx_ref, glam_ref, gcb_ref, gcw_ref):
                ref[...] = jnp.zeros_like(ref)

        first = i == nch - 1
        last = slice(tc - SUBLANES, tc)
        hbuf[SUBLANES:, :] = h_ref[...]
        hbuf[0:SUBLANES, :] = _down_a_segment(h_ref[last, :], jnp.where(first, 0.0, hh_ref[SUBLANES - 1:SUBLANES, :]))
        _fill_conv_window(xe, x_ref, xh_ref, first, tc)
        lam_v = lam_ref[...]
        sp = _softplus_neg(lam_v)
        a_all = jnp.exp(-LRU_C * r_ref[...] * sp)
        src_a[...] = a_all
        src_m[...] = a_all * gh_ref[...]
        for j0 in range(0, NB_LRU, LRU_JB):
            def kstep(kk, st):
                rows = _step_rows(m - 1 - kk)
                out = []
                for q in range(LRU_JB):
                    ln = _lanes(j0 + q)
                    mu, ac = st[2 * q], st[2 * q + 1]
                    a = src_a[rows, ln]
                    mu = a * mu + src_m[rows, ln]
                    ac = a * ac
                    dst_m[rows, ln] = mu
                    dst_a[rows, ln] = ac
                    out += [mu, ac]
                return tuple(out)

            init = tuple(jnp.zeros((SEGS, LANES), f32) if q % 2 == 0 else jnp.ones((SEGS, LANES), f32)
                         for q in range(2 * LRU_JB))
            ends = lax.fori_loop(0, m, kstep, init)
            for q in range(LRU_JB):
                ln = _lanes(j0 + q)
                em, ea = ends[2 * q], ends[2 * q + 1]
                cr = carry_ref[:, ln]
                rows_c = [None] * SEGS
                for s in reversed(range(SEGS)):
                    rows_c[s] = cr
                    cr = em[s:s + 1, :] + ea[s:s + 1, :] * cr
                cin_ref[:, ln] = _stack_rows(rows_c)
                carry_ref[:, ln] = cr

        def fix(k, _):
            rows = _step_rows(k)
            mbuf[rows, :] = dst_m[rows, :] + dst_a[rows, :] * cin_ref[...]
            return 0

        lax.fori_loop(0, m, fix, 0)
        mbuf[tc:, :] = _up_a_segment(mbuf[0:SUBLANES, :], cin_ref[SUBLANES - 1:SUBLANES, :])
        sneg = _sig(-lam_v)
        for j in range(NB_LRU):
            ln = _lanes(j)
            lamt = gh_ref[:, ln] + mbuf[SUBLANES:, ln]
            rj, ij, cj = r_ref[:, ln], i_ref[:, ln], c_ref[:, ln]
            log_a = -LRU_C * rj * sp[:, ln]
            a = src_a[:, ln]
            om = _one_minus_sq(a, log_a)
            inv_mult = lax.rsqrt(om)
            mult = om * inv_mult
            g_a = lamt * hbuf[0:tc, ln]
            g_mult = lamt * ij * cj
            g_i = lamt * mult * cj
            g_c = lamt * mult * ij
            g_log_a = g_a * a - g_mult * a * a * inv_mult
            glam_ref[:, ln] += jnp.sum(g_log_a * rj, axis=0, keepdims=True) * LRU_C * sneg[:, ln]
            g_ra = g_log_a * (-LRU_C) * sp[:, ln] * rj * (1.0 - rj)
            g_ia = g_i * ij * (1.0 - ij)
            gba_ref[:, ln] += jnp.sum(g_ra, axis=0, keepdims=True)
            gbx_ref[:, ln] += jnp.sum(g_ia, axis=0, keepdims=True)
            cjb, grb, gib = cj.astype(bf16), g_ra.astype(bf16), g_ia.astype(bf16)
            gwa_ref[j] += _dot_tn(cjb, grb)
            gwx_ref[j] += _dot_tn(cjb, gib)
            g_c = g_c + _dot_nt(grb, wa_ref[j]) + _dot_nt(gib, wx_ref[j])
            gce[0:tc, ln] = g_c
            gcb_ref[:, ln] += jnp.sum(g_c, axis=0, keepdims=True)
        for d in range(TAPS_BACK):
            blk = slice(SUBLANES * d, SUBLANES * (d + 1))
            gce[tc + SUBLANES * d:tc + SUBLANES * (d + 1), :] = _up_a_segment(gce[blk, :], gcc_ref[SUBLANES * d:SUBLANES * d + 1, :])
        gcc_ref[...] = gce[0:EDGE, :]
        gc = gce[0:tc, :]
        glx = cw_ref[CONV_WIDTH - 1:CONV_WIDTH, :] * gc
        gcw_ref[CONV_WIDTH - 1:CONV_WIDTH, :] += jnp.sum(gc * xe[EDGE:EDGE + tc, :], axis=0, keepdims=True)
        for k in range(CONV_WIDTH - 1):
            off = SUBLANES * (CONV_WIDTH - 1 - k)
            glx = glx + cw_ref[k:k + 1, :] * gce[off:off + tc, :]
            gcw_ref[k:k + 1, :] += jnp.sum(gc * xe[EDGE - off:EDGE - off + tc, :], axis=0, keepdims=True)
        glx_ref[...] = glx.astype(bf16)

    rev = lambda i: (nch - 1 - i, 0)
    halo = lambda rows: (lambda i: (jnp.maximum((nch - 1 - i) * (tc // rows) - 1, 0), 0))
    wide = lambda: pl.BlockSpec((tc, LRU_WIDTH), rev)
    vec = lambda: _full((1, LRU_WIDTH))
    hd = lambda: _full((LRU_HEADS, LRU_HEAD_DIM, LRU_HEAD_DIM))
    buf = lambda rows: pltpu.VMEM((rows, LRU_WIDTH), f32)
    return pl.pallas_call(
        body, name=name, grid=(nch,),
        in_specs=[wide(), wide(), pl.BlockSpec((SUBLANES, LRU_WIDTH), halo(SUBLANES)), wide(), wide(), wide(), wide(),
                  pl.BlockSpec((HALO_ROWS, LRU_WIDTH), halo(HALO_ROWS)), _full((CONV_WIDTH, LRU_WIDTH)), hd(), hd(), vec()],
        out_specs=[wide(), hd(), hd(), vec(), vec(), vec(), vec(), _full((CONV_WIDTH, LRU_WIDTH))],
        out_shape=[_S((T, LRU_WIDTH), bf16), _S((LRU_HEADS, LRU_HEAD_DIM, LRU_HEAD_DIM)),
                   _S((LRU_HEADS, LRU_HEAD_DIM, LRU_HEAD_DIM)), _S((1, LRU_WIDTH)), _S((1, LRU_WIDTH)),
                   _S((1, LRU_WIDTH)), _S((1, LRU_WIDTH)), _S((CONV_WIDTH, LRU_WIDTH))],
        scratch_shapes=[buf(tc), buf(tc), buf(tc), buf(tc), buf(tc + SUBLANES), buf(tc + SUBLANES), buf(tc + EDGE),
                        buf(tc + EDGE), buf(SEGS), buf(EDGE), buf(1)],
        compiler_params=_cp("arbitrary"),
    )(gh, h, h, c, r, gi, lx, lx, convw, wa, wx, lam)


def merge_fwd(h, lg, zs, gsl, x, p, wbl, wout, gpost, wple, wpg, name):
    T = x.shape[0]
    tm = _tile(T, TOKEN_TILE)

    def body(h_ref, lg_ref, zs_ref, gs_ref, gl_ref, x_ref, p_ref, wbl_ref, wout_ref, gp_ref, wple_ref, wpg_ref,
             zl_ref, mix_ref, xo_ref):
        lg_v = lg_ref[...]
        yl = h_ref[...] * (lg_v * _sig(lg_v))
        zl = _dot(yl.astype(bf16), wbl_ref[...])
        merged = _sig(gs_ref[...]) * zs_ref[...] + _sig(gl_ref[...]) * zl
        mix = _dot(merged.astype(bf16), wout_ref[...])
        r2 = lax.rsqrt(jnp.mean(mix * mix, axis=-1, keepdims=True) + NORM_EPS)
        x1 = x_ref[...] + mix * r2 * gp_ref[...]
        q = _dot(x1.astype(bf16), wpg_ref[...])
        pe = _dot(p_ref[...].astype(bf16), wple_ref[...])
        zl_ref[...], mix_ref[...] = zl, mix
        xo_ref[...] = x1 + pe * _sig(q)

    dm = lambda: _rows(tm, D_MODEL)
    return pl.pallas_call(
        body, name=name, grid=(T // tm,),
        in_specs=[_rows(tm, LRU_WIDTH), _rows(tm, LRU_WIDTH), dm(), _rows(tm, D_MODEL, 0), _rows(tm, D_MODEL, 1), dm(),
                  _rows(tm, PLE_DIM), _resident((LRU_WIDTH, D_MODEL)), _resident((D_MODEL, D_MODEL)), _full((1, D_MODEL)),
                  _resident((PLE_DIM, D_MODEL)), _resident((D_MODEL, D_MODEL))],
        out_specs=[dm(), dm(), dm()],
        out_shape=[_S((T, D_MODEL))] * 3, compiler_params=_cp("parallel"),
    )(h, lg, zs, gsl, gsl, x, p, wbl, wout, gpost, wple, wpg)


def post_bwd(gx2, mix, x, p, wpg, wple, gpost, name):
    T = x.shape[0]
    tm = _tile(T, TOKEN_TILE)

    def body(gx2_ref, mix_ref, x_ref, p_ref, wpg_ref, wple_ref, gp_ref, gres_ref, gmix_ref, ggp_ref, gwpg_ref, gwple_ref):
        i = pl.program_id(0)
        gx2 = gx2_ref[...]
        mix = mix_ref[...]
        gp = gp_ref[...]
        r2 = lax.rsqrt(jnp.mean(mix * mix, axis=-1, keepdims=True) + NORM_EPS)
        nrm = mix * r2
        x1b = (x_ref[...] + nrm * gp).astype(bf16)
        pb = p_ref[...].astype(bf16)
        sq = _sig(_dot(x1b, wpg_ref[...]))
        pe = _dot(pb, wple_ref[...])
        gq = (gx2 * pe * sq * (1.0 - sq)).astype(bf16)
        _accumulate(gwple_ref, _dot_tn(pb, (gx2 * sq).astype(bf16)), i)
        _accumulate(gwpg_ref, _dot_tn(x1b, gq), i)
        gx1 = gx2 + _dot_nt(gq, wpg_ref[...])
        gres_ref[...] = gx1
        _accumulate(ggp_ref, jnp.sum(gx1 * nrm, axis=0, keepdims=True), i)
        gy = gx1 * gp
        gmix_ref[...] = (r2 * (gy - nrm * jnp.mean(gy * nrm, axis=-1, keepdims=True))).astype(bf16)

    dm = lambda: _rows(tm, D_MODEL)
    return pl.pallas_call(
        body, name=name, grid=(T // tm,),
        in_specs=[dm(), dm(), dm(), _rows(tm, PLE_DIM), _resident((D_MODEL, D_MODEL)), _resident((PLE_DIM, D_MODEL)),
                  _full((1, D_MODEL))],
        out_specs=[dm(), dm(), _full((1, D_MODEL)), _full((D_MODEL, D_MODEL)), _full((PLE_DIM, D_MODEL))],
        out_shape=[_S((T, D_MODEL)), _S((T, D_MODEL), bf16), _S((1, D_MODEL)), _S((D_MODEL, D_MODEL)),
                   _S((PLE_DIM, D_MODEL))],
        compiler_params=_cp("arbitrary"),
    )(gx2, mix, x, p, wpg, wple, gpost)


def gate_bwd(gmix, zl, zs, gsl, h, lg, wout, wbl, name):
    T = zl.shape[0]
    tm = _tile(T, TOKEN_TILE)

    def body(gmix_ref, zl_ref, zs_ref, gs_ref, gl_ref, h_ref, lg_ref, wout_ref, wbl_ref,
             gzs_ref, ggsl_ref, gh_ref, glg_ref, gwout_ref, gwbl_ref):
        i = pl.program_id(0)
        gmix = gmix_ref[...]
        gmerged = _dot_nt(gmix, wout_ref[...])
        zs, zl = zs_ref[...], zl_ref[...]
        ss, sl = _sig(gs_ref[...]), _sig(gl_ref[...])
        _accumulate(gwout_ref, _dot_tn((ss * zs + sl * zl).astype(bf16), gmix), i)
        gzs_ref[...] = (gmerged * ss).astype(bf16)
        gzl = (gmerged * sl).astype(bf16)
        ggsl_ref[:, :D_MODEL] = (gmerged * zs * ss * (1.0 - ss)).astype(bf16)
        ggsl_ref[:, D_MODEL:] = (gmerged * zl * sl * (1.0 - sl)).astype(bf16)
        lg_v, hv = lg_ref[...], h_ref[...]
        slg = _sig(lg_v)
        silu = lg_v * slg
        _accumulate(gwbl_ref, _dot_tn((hv * silu).astype(bf16), gzl), i)
        gyl = _dot_nt(gzl, wbl_ref[...])
        gh_ref[...] = gyl * silu
        glg_ref[...] = (gyl * hv * slg * (1.0 + lg_v * (1.0 - slg))).astype(bf16)

    dm = lambda: _rows(tm, D_MODEL)
    lw = lambda: _rows(tm, LRU_WIDTH)
    return pl.pallas_call(
        body, name=name, grid=(T // tm,),
        in_specs=[dm(), dm(), dm(), _rows(tm, D_MODEL, 0), _rows(tm, D_MODEL, 1), lw(), lw(),
                  _resident((D_MODEL, D_MODEL)), _resident((LRU_WIDTH, D_MODEL))],
        out_specs=[dm(), _rows(tm, 2 * D_MODEL), lw(), lw(), _full((D_MODEL, D_MODEL)), _full((LRU_WIDTH, D_MODEL))],
        out_shape=[_S((T, D_MODEL), bf16), _S((T, 2 * D_MODEL), bf16), _S((T, LRU_WIDTH)), _S((T, LRU_WIDTH), bf16),
                   _S((D_MODEL, D_MODEL)), _S((LRU_WIDTH, D_MODEL))],
        compiler_params=_cp("arbitrary"),
    )(gmix, zl, zs, gsl, gsl, h, lg, wout, wbl)


def in_proj_bwd(pieces, win, x, gres, g, name):
    T = x.shape[0]
    tm = _tile(T, MM_TILE_M // 2)
    widths = [pc.shape[1] for pc in pieces]
    offs = [sum(widths[:k]) for k in range(len(widths))]

    def body(*refs):
        pc_refs = refs[:len(widths)]
        w_ref, x_ref, gres_ref, g_ref, gx_ref, gg_ref = refs[len(widths):]
        i = pl.program_id(0)
        ghv = _dot_nt(pc_refs[0][...], w_ref[:, offs[0]:offs[0] + widths[0]])
        for k in range(1, len(widths)):
            ghv = ghv + _dot_nt(pc_refs[k][...], w_ref[:, offs[k]:offs[k] + widths[k]])
        xv = x_ref[...]
        r = lax.rsqrt(jnp.mean(xv * xv, axis=-1, keepdims=True) + NORM_EPS)
        nrm = xv * r
        gy = ghv * g_ref[...]
        gx_ref[...] = gres_ref[...] + r * (gy - nrm * jnp.mean(gy * nrm, axis=-1, keepdims=True))
        _accumulate(gg_ref, jnp.sum(ghv * nrm, axis=0, keepdims=True), i)

    dm = lambda: _rows(tm, D_MODEL)
    return pl.pallas_call(
        body, name=name, grid=(T // tm,),
        in_specs=[_rows(tm, wd) for wd in widths] + [_resident(win.shape), dm(), dm(), _full((1, D_MODEL))],
        out_specs=[dm(), _full((1, D_MODEL))],
        out_shape=[_S((T, D_MODEL)), _S((1, D_MODEL))], compiler_params=_cp("arbitrary"),
    )(*pieces, win, x, gres, g)


def loss_head(y, target, name):
    T = y.shape[0]
    tm = _tile(T, TOKEN_TILE)

    def body(y_ref, t_ref, l_ref, g_ref):
        i = pl.program_id(0)
        e = y_ref[...] - t_ref[...]
        g_ref[...] = e * (1.0 / D_MODEL)
        part = 0.5 * jnp.sum(jnp.sum(e * e, axis=-1, keepdims=True) * (1.0 / D_MODEL), axis=0, keepdims=True)

        @pl.when(i == 0)
        def _():
            l_ref[...] = part

        @pl.when(i > 0)
        def _():
            l_ref[...] += part

    return pl.pallas_call(
        body, name=name, grid=(T // tm,),
        in_specs=[_rows(tm, D_MODEL), _rows(tm, D_MODEL)], out_specs=[_full((1, 1)), _rows(tm, D_MODEL)],
        out_shape=[_S((1, 1)), _S((T, D_MODEL))],
        compiler_params=_cp("arbitrary"),
    )(y, target)


def _s5_operands(w, m, tag):
    b_re_t = jnp.transpose(w['s5_b_re'], (2, 0, 1))
    b_im_t = jnp.transpose(w['s5_b_im'], (2, 0, 1))
    ldt = w['s5_log_dt'][:, None]
    ab, pw, bb = s5_prep(w['s5_a_re'], w['s5_a_im'], ldt, b_re_t, b_im_t, m, "s5_prep" + tag)
    over_sublanes = lambda t: jnp.broadcast_to(t[..., None, :], t.shape[:-1] + (SEGS, S5_NS))
    ptab = pw.reshape(2, m, S5_NS)
    return dict(abar_b=over_sublanes(ab.reshape(2, S5_NS)), ptab_b=over_sublanes(ptab),
                ptab_rev_b=over_sublanes(ptab[:, ::-1, :]), bdb=_pack_bdb(bb).astype(bf16),
                cdb=_pack_cdb(w['s5_c_re'], w['s5_c_im']).astype(bf16), dvec=w['s5_d'][None, :],
                prep_in=(w['s5_a_re'], w['s5_a_im'], ldt, b_re_t, b_im_t))


def layer_fwd(x, p, w, tag):
    T = x.shape[0]
    m = _tile(T, SCAN_CHUNK) // SEGS
    s5 = _s5_operands(w, m, tag)
    h_bf = rms_fwd(x, w['g_pre'][None, :], "rms_fwd" + tag)
    win = w['w_in']
    usg = mm_nn(h_bf, win[:, :2 * S5_WIDTH], "proj_s5" + tag)
    lx = mm_nn(h_bf, win[:, 2 * S5_WIDTH:2 * S5_WIDTH + LRU_WIDTH], "proj_lx" + tag)
    lg = mm_nn(h_bf, win[:, 2 * S5_WIDTH + LRU_WIDTH:2 * S5_WIDTH + 2 * LRU_WIDTH], "proj_lg" + tag)
    gsl = mm_nn(h_bf, win[:, 2 * S5_WIDTH + 2 * LRU_WIDTH:], "proj_gate" + tag)
    ys, s_re, s_im, s_bf = s5_fwd(usg, s5['bdb'], s5['cdb'], s5['dvec'], s5['abar_b'], s5['ptab_b'], "s5_fwd" + tag)
    glu, zs = s5_post_fwd(ys, usg, w['w_glu'], w['w_bs'], "s5_post_fwd" + tag)
    wa, wx = w['lru_w_a'].astype(bf16), w['lru_w_x'].astype(bf16)
    c, r, gi, hs = lru_fwd(lx, w['conv_w'], w['conv_b'][None, :], wa, wx, w['lru_b_a'][None, :], w['lru_b_x'][None, :],
                           w['lru_lambda'][None, :], "lru_fwd" + tag)
    zl, mix, x_out = merge_fwd(hs, lg, zs, gsl, x, p, w['w_bl'], w['w_out'], w['g_post'][None, :],
                               w['w_ple'], w['w_ple_gate'], "merge_fwd" + tag)
    saved = dict(x=x, p=p, h_bf=h_bf, usg=usg, lx=lx, lg=lg, gsl=gsl, ys=ys, s_re=s_re, s_im=s_im, s_bf=s_bf, glu=glu,
                 zs=zs, c=c, r=r, gi=gi, hs=hs, zl=zl, mix=mix, s5=s5, wa=wa, wx=wx)
    return x_out, saved


def layer_bwd(gx_out, w, sv, tag):
    s5 = sv['s5']
    g = {}
    gres, gmix, g_gpost, g['w_ple_gate'], g['w_ple'] = post_bwd(
        gx_out, sv['mix'], sv['x'], sv['p'], w['w_ple_gate'], w['w_ple'], w['g_post'][None, :], "post_bwd" + tag)
    gzs, ggsl, g_h, g_lg, g['w_out'], g['w_bl'] = gate_bwd(
        gmix, sv['zl'], sv['zs'], sv['gsl'], sv['hs'], sv['lg'], w['w_out'], w['w_bl'], "gate_bwd" + tag)
    g['g_post'] = g_gpost[0]
    (g_lx, g_wa, g_wx, g_ba, g_bx, g_lam, g_cb, g_cw) = lru_bwd(
        g_h, sv['hs'], sv['c'], sv['r'], sv['gi'], sv['lx'], w['conv_w'], sv['wa'], sv['wx'],
        w['lru_lambda'][None, :], "lru_bwd" + tag)
    g['lru_w_a'], g['lru_w_x'] = g_wa, g_wx
    g['lru_b_a'], g['lru_b_x'], g['lru_lambda'], g['conv_b'], g['conv_w'] = g_ba[0], g_bx[0], g_lam[0], g_cb[0], g_cw
    g_ys, g_sg, g['w_bs'], g['w_glu'] = s5_post_bwd(gzs, sv['glu'], sv['usg'], sv['ys'], w['w_bs'], w['w_glu'],
                                                    "s5_post_bwd" + tag)
    g_u, g_ab, g_d, g_bdb, g_cdb = s5_bwd(g_ys, sv['usg'], sv['s_re'], sv['s_im'], sv['s_bf'], s5['bdb'], s5['cdb'],
                                          s5['dvec'], s5['abar_b'], s5['ptab_rev_b'], "s5_bwd" + tag)
    g['s5_d'] = g_d[0]
    g['s5_c_re'], g['s5_c_im'] = _unpack_cdb(g_cdb)
    g_are, g_aim, g_ldt, g_bre_t, g_bim_t = s5_prep_bwd(*s5['prep_in'], g_ab.reshape(2, S5_GROUPS, S5_STATE),
                                                       _unpack_bdb(g_bdb), "s5_prep_bwd" + tag)
    g['s5_a_re'], g['s5_a_im'], g['s5_log_dt'] = g_are, g_aim, g_ldt
    g['s5_b_re'] = jnp.transpose(g_bre_t, (1, 2, 0))
    g['s5_b_im'] = jnp.transpose(g_bim_t, (1, 2, 0))
    pieces = [g_u, g_sg, g_lx, g_lg, ggsl]
    g['w_in'] = jnp.concatenate([mm_tn(sv['h_bf'], pc, "gw_in%d%s" % (k, tag)) for k, pc in enumerate(pieces)], axis=1)
    gx, g_gpre = in_proj_bwd(pieces, w['w_in'], sv['x'], gres, w['g_pre'][None, :], "in_proj_bwd" + tag)
    g['g_pre'] = g_gpre[0]
    return gx, g


def _as_2d(a):
    return a.reshape((-1, a.shape[-1])) if a.ndim > 1 else a.reshape((1, -1))


def _adamw_update(w, gv, m, v):
    nm = ADAM_B1 * m + (1.0 - ADAM_B1) * gv
    nv = ADAM_B2 * v + (1.0 - ADAM_B2) * (gv * gv)
    bc1 = 1.0 - ADAM_B1 ** ADAM_STEP
    bc2 = 1.0 - ADAM_B2 ** ADAM_STEP
    return -ADAM_LR * ((nm / bc1) / (jnp.sqrt(nv / bc2) + ADAM_EPS) + ADAM_WD * w), nm, nv


def adamw(w, g, m, v, name):
    shape = w.shape
    w2, g2, m2, v2 = _as_2d(w), _as_2d(g), _as_2d(m), _as_2d(v)
    R, C = w2.shape
    tr = _row_tile(R, C)

    def body(w_ref, g_ref, m_ref, v_ref, d_ref, nm_ref, nv_ref):
        d_ref[...], nm_ref[...], nv_ref[...] = _adamw_update(w_ref[...], g_ref[...], m_ref[...], v_ref[...])

    spec = lambda: pl.BlockSpec((tr, C), lambda i: (i, 0))
    d, nm, nv = pl.pallas_call(
        body, name=name, grid=(R // tr,), in_specs=[spec() for _ in range(4)], out_specs=[spec() for _ in range(3)],
        out_shape=[_S((R, C))] * 3, compiler_params=_cp("parallel"),
    )(w2, g2, m2, v2)
    return d.reshape(shape), nm.reshape(shape), nv.reshape(shape)


def adamw_reduce(w, a, theirs, m, v, chip, name):
    shape = w.shape
    w2, m2, v2 = _as_2d(w), _as_2d(m), _as_2d(v)
    R, C = w2.shape
    a3, t3 = a.reshape(4, R, C), theirs.reshape(3, R, C)
    tr = _row_tile(R, C)

    def body(chip_ref, w_ref, a_ref, t_ref, m_ref, v_ref, g_ref, d_ref, nm_ref, nv_ref):
        gv = ((a_ref[0] + t_ref[0].astype(f32)) + t_ref[1].astype(f32)) + t_ref[2].astype(f32)
        g_ref[...] = gv
        d_ref[...], nm_ref[...], nv_ref[...] = _adamw_update(w_ref[...], gv, m_ref[...], v_ref[...])

    spec = lambda: pl.BlockSpec((tr, C), lambda i, c: (i, 0))
    grid_spec = pltpu.PrefetchScalarGridSpec(
        num_scalar_prefetch=1, grid=(R // tr,),
        in_specs=[spec(), pl.BlockSpec((1, tr, C), lambda i, c: (c[0], i, 0)), pl.BlockSpec((3, tr, C), lambda i, c: (0, i, 0)),
                  spec(), spec()],
        out_specs=[spec() for _ in range(4)])
    g, d, nm, nv = pl.pallas_call(
        body, name=name, grid_spec=grid_spec, out_shape=[_S((R, C))] * 4, compiler_params=_cp("parallel"),
    )(chip, w2, a3, t3, m2, v2)
    return g.reshape(shape), d.reshape(shape), nm.reshape(shape), nv.reshape(shape)


MESH = pl.DeviceIdType.MESH
ANY = pl.BlockSpec(memory_space=pl.ANY)


def _place():
    return lax.axis_index("x"), lax.axis_index("y"), lax.axis_index("c")


def _other_chips(mx, my):
    return [(1 - mx, my), (mx, 1 - my), (1 - mx, 1 - my)]


def all_gather(shards, name):
    nb = len(shards)
    OWN_SIB, OWN_X, OWN_Y, X_SIB, Y_SIB, RELAY_X, RELAY_Y, DIAG0_SIB, DIAG1_SIB = range(9)

    def body(*refs):
        x_refs, out_refs, send_sems, recv_sems, local_sems = refs[:nb], refs[nb:2 * nb], *refs[2 * nb:]
        mx, my, mc = _place()
        sibling, xn, yn = (mx, my, 1 - mc), (1 - mx, my, mc), (mx, 1 - my, mc)

        def block(b, px, py, pc, half=None):
            ref = out_refs[b].at[4 * px + 2 * py + pc]
            if half is None:
                return ref
            n = shards[b].shape[0] // 2
            return ref.at[pl.ds(half * n, n)]

        def copy(b, k, dst, to, src=None):
            return pltpu.make_async_remote_copy(
                src_ref=dst if src is None else src, dst_ref=dst, send_sem=send_sems.at[b, k],
                recv_sem=recv_sems.at[b, k], device_id=to, device_id_type=MESH)

        started = []

        def start(cp):
            cp.start()
            started.append(cp)

        for b in range(nb):
            start(copy(b, OWN_X, block(b, mx, my, mc), xn, src=x_refs[b]))
            start(copy(b, OWN_Y, block(b, mx, my, mc), yn, src=x_refs[b]))
        for b in range(nb):
            start(copy(b, OWN_SIB, block(b, mx, my, mc), sibling, src=x_refs[b]))
        mine = [pltpu.make_async_copy(x_refs[b], block(b, mx, my, mc), local_sems.at[b]) for b in range(nb)]
        for cp in mine:
            cp.start()
        for b in range(nb):
            copy(b, OWN_X, block(b, 1 - mx, my, mc), xn).wait_recv()
            start(copy(b, RELAY_X, block(b, 1 - mx, my, mc, 0), yn))
            start(copy(b, X_SIB, block(b, 1 - mx, my, mc), sibling))
        for b in range(nb):
            copy(b, OWN_Y, block(b, mx, 1 - my, mc), yn).wait_recv()
            start(copy(b, RELAY_Y, block(b, mx, 1 - my, mc, 1), xn))
            start(copy(b, Y_SIB, block(b, mx, 1 - my, mc), sibling))
        for b in range(nb):
            copy(b, RELAY_X, block(b, 1 - mx, 1 - my, mc, 0), yn).wait_recv()
            start(copy(b, DIAG0_SIB, block(b, 1 - mx, 1 - my, mc, 0), sibling))
            copy(b, RELAY_Y, block(b, 1 - mx, 1 - my, mc, 1), xn).wait_recv()
            start(copy(b, DIAG1_SIB, block(b, 1 - mx, 1 - my, mc, 1), sibling))
        for b in range(nb):
            copy(b, OWN_SIB, block(b, mx, my, 1 - mc), sibling).wait_recv()
            copy(b, X_SIB, block(b, 1 - mx, my, 1 - mc), sibling).wait_recv()
            copy(b, Y_SIB, block(b, mx, 1 - my, 1 - mc), sibling).wait_recv()
            copy(b, DIAG0_SIB, block(b, 1 - mx, 1 - my, 1 - mc, 0), sibling).wait_recv()
            copy(b, DIAG1_SIB, block(b, 1 - mx, 1 - my, 1 - mc, 1), sibling).wait_recv()
        for cp in started:
            cp.wait_send()
        for cp in mine:
            cp.wait()

    outs = pl.pallas_call(
        body, name=name, out_shape=[_S((N_DEV,) + s.shape, s.dtype) for s in shards], in_specs=[ANY] * nb,
        out_specs=[ANY] * nb,
        scratch_shapes=[pltpu.SemaphoreType.DMA((nb, 9)), pltpu.SemaphoreType.DMA((nb, 9)), pltpu.SemaphoreType.DMA((nb,))],
    )(*shards)
    return list(outs)


def exchange_sibling(gs, name):
    nb = len(gs)

    def body(*refs):
        g_refs, recv_refs, send_sems, recv_sems = refs[:nb], refs[nb:2 * nb], refs[2 * nb], refs[2 * nb + 1]
        mx, my, mc = _place()
        copies = [pltpu.make_async_remote_copy(
            src_ref=g_refs[b].at[2 * k + 1 - mc], dst_ref=recv_refs[b].at[k], send_sem=send_sems.at[b, k],
            recv_sem=recv_sems.at[b, k], device_id=(mx, my, 1 - mc), device_id_type=MESH)
            for b in range(nb) for k in range(4)]
        for cp in copies:
            cp.start()
        for cp in copies:
            cp.wait()

    outs = pl.pallas_call(
        body, name=name, out_shape=[_S((4,) + g.shape[1:], g.dtype) for g in gs], in_specs=[ANY] * nb,
        out_specs=[ANY] * nb,
        scratch_shapes=[pltpu.SemaphoreType.DMA((nb, 4)), pltpu.SemaphoreType.DMA((nb, 4))],
    )(*gs)
    return list(outs)


def exchange_chips(parts, name):
    nb = len(parts)

    def body(*refs):
        a_refs, recv_refs, send_sems, recv_sems = refs[:nb], refs[nb:2 * nb], refs[2 * nb], refs[2 * nb + 1]
        mx, my, mc = _place()
        copies = [pltpu.make_async_remote_copy(
            src_ref=a_refs[b].at[2 * px + py], dst_ref=recv_refs[b].at[j], send_sem=send_sems.at[b, j],
            recv_sem=recv_sems.at[b, j], device_id=(px, py, mc), device_id_type=MESH)
            for b in range(nb) for j, (px, py) in enumerate(_other_chips(mx, my))]
        for cp in copies:
            cp.start()
        for cp in copies:
            cp.wait()

    outs = pl.pallas_call(
        body, name=name, out_shape=[_S((3,) + a.shape[1:], a.dtype) for a in parts], in_specs=[ANY] * nb,
        out_specs=[ANY] * nb,
        scratch_shapes=[pltpu.SemaphoreType.DMA((nb, 3)), pltpu.SemaphoreType.DMA((nb, 3))],
    )(*parts)
    return list(outs)


def add_sibling(g, theirs, core, name, wire_dtype=f32):
    shp = theirs.shape
    C = shp[-1]
    R = math.prod(shp[1:-1])
    tr = _row_tile(R, C)
    narrow = wire_dtype != f32

    def body(core_ref, g_ref, t_ref, o_ref, *wire_ref):
        s = g_ref[...] + t_ref[...]
        o_ref[...] = s
        if narrow:
            wire_ref[0][...] = s.astype(wire_dtype)

    blk = lambda: pl.BlockSpec((1, tr, C), lambda k, i, c: (k, i, 0))
    grid_spec = pltpu.PrefetchScalarGridSpec(
        num_scalar_prefetch=1, grid=(4, R // tr),
        in_specs=[pl.BlockSpec((1, tr, C), lambda k, i, c: (2 * k + c[0], i, 0)), blk()],
        out_specs=[blk(), blk()] if narrow else [blk()])
    outs = pl.pallas_call(
        body, name=name, grid_spec=grid_spec,
        out_shape=[_S((4, R, C), f32)] + ([_S((4, R, C), wire_dtype)] if narrow else []),
        compiler_params=_cp("parallel", "parallel"),
    )(core, g.reshape(N_DEV, R, C), theirs.reshape(4, R, C))
    part = outs[0].reshape(shp)
    return part, (outs[1].reshape(shp) if narrow else part)


def add_chips(a, theirs, chip, name):
    _, R, C = a.shape
    tr = _row_tile(R, C)

    def body(chip_ref, a_ref, t_ref, out_ref):
        out_ref[...] = ((a_ref[0] + t_ref[0]) + t_ref[1]) + t_ref[2]

    grid_spec = pltpu.PrefetchScalarGridSpec(
        num_scalar_prefetch=1, grid=(R // tr,),
        in_specs=[pl.BlockSpec((1, tr, C), lambda i, c: (c[0], i, 0)), pl.BlockSpec((3, tr, C), lambda i, c: (0, i, 0))],
        out_specs=pl.BlockSpec((tr, C), lambda i, c: (i, 0)))
    return pl.pallas_call(
        body, name=name, grid_spec=grid_spec, out_shape=_S((R, C), a.dtype), compiler_params=_cp("parallel"),
    )(chip, a, theirs)


def _round_up(n, q):
    return (n + q - 1) // q * q


def _lane_rows(a):
    flat = a.reshape(-1)
    n = _round_up(flat.shape[0], SUBLANES * LANES)
    return jnp.pad(flat, (0, n - flat.shape[0])).reshape(-1, LANES)


def _full_to_shards(full, axis):
    shp = full.shape
    s = shp[axis] // N_DEV
    cut = full.reshape(shp[:axis] + (N_DEV, s) + shp[axis + 1:])
    return jnp.moveaxis(cut, axis, 0)


def _shards_to_full(parts, axis):
    shp = list(parts.shape[1:])
    shp[axis] *= N_DEV
    return jnp.moveaxis(parts, 0, axis).reshape(tuple(shp))


def kernel(x, p, g_pre, w_in, s5_a_re, s5_a_im, s5_log_dt, s5_b_re, s5_b_im, s5_c_re, s5_c_im, s5_d, w_glu, w_bs, conv_w, conv_b, lru_w_a, lru_b_a, lru_w_x, lru_b_x, lru_lambda, w_bl, w_out, g_post, w_ple, w_ple_gate, loss_target, m_g_pre, m_w_in, m_s5_a_re, m_s5_a_im, m_s5_log_dt, m_s5_b_re, m_s5_b_im, m_s5_c_re, m_s5_c_im, m_s5_d, m_w_glu, m_w_bs, m_conv_w, m_conv_b, m_lru_w_a, m_lru_b_a, m_lru_w_x, m_lru_b_x, m_lru_lambda, m_w_bl, m_w_out, m_g_post, m_w_ple, m_w_ple_gate, v_g_pre, v_w_in, v_s5_a_re, v_s5_a_im, v_s5_log_dt, v_s5_b_re, v_s5_b_im, v_s5_c_re, v_s5_c_im, v_s5_d, v_w_glu, v_w_bs, v_conv_w, v_conv_b, v_lru_w_a, v_lru_b_a, v_lru_w_x, v_lru_b_x, v_lru_lambda, v_w_bl, v_w_out, v_g_post, v_w_ple, v_w_ple_gate):
    given = dict(locals())
    W = {n: given[n] for n in WEIGHTS}
    M = {n: given["m_" + n] for n in WEIGHTS}
    V = {n: given["v_" + n] for n in WEIGHTS}
    xs, target = to_scan_order(x[0]), to_scan_order(loss_target[0])
    ps = [to_scan_order(p[i, 0]) for i in range(DEPTH)]

    mx, my, mc = _place()
    core = jnp.reshape(mc, (1,)).astype(jnp.int32)
    chip = jnp.reshape(2 * mx + my, (1,)).astype(jnp.int32)

    names = list(SHARDED)
    shards = [W[n].astype(bf16) if n in GATHER_BF16 else W[n] for n in names]
    gathered = all_gather(shards, "comm_gather_weights")
    full = {n: _shards_to_full(g, SHARDED[n]) for n, g in zip(names, gathered)}

    def layer_weights(i):
        return {n: (full[n][i] if n in SHARDED else W[n][i]) for n in WEIGHTS}

    act, saved = xs, []
    for i in range(DEPTH):
        act, sv = layer_fwd(act, ps[i], layer_weights(i), "_l%d" % i)
        saved.append(sv)
    loss_part, gact = loss_head(act, target, "loss_head")
    grads = [None] * DEPTH
    for i in reversed(range(DEPTH)):
        gact, grads[i] = layer_bwd(gact, layer_weights(i), saved[i], "_l%d" % i)
    loss = lax.psum(loss_part[0, 0], ("x", "y", "c"))
    gfull = {n: jnp.stack([grads[i][n].reshape(full[n].shape[1:] if n in SHARDED else W[n].shape[1:])
                           for i in range(DEPTH)]) for n in WEIGHTS}

    rep_rows = [_lane_rows(gfull[n]) for n in REPLICATED]
    n_rows = sum(r.shape[0] for r in rep_rows)
    pad_rows = _round_up(n_rows, N_DEV * SUBLANES) - n_rows
    rep_blocks = jnp.concatenate(rep_rows + [jnp.zeros((pad_rows, LANES), f32)]).reshape(N_DEV, -1, LANES)
    blocks = [_full_to_shards(gfull[n], SHARDED[n]) for n in names] + [rep_blocks]
    theirs = exchange_sibling(blocks, "comm_reduce_sibling")
    parts, wire = [], []
    for k, (b, t) in enumerate(zip(blocks, theirs)):
        part, sent = add_sibling(b, t, core, "reduce_add_sibling_%d" % k, bf16 if k < len(names) else f32)
        parts.append(part)
        wire.append(sent)
    others = exchange_chips(wire, "comm_reduce_chips")

    red, deltas, new_m, new_v = {}, {}, {}, {}
    for n, a, t in zip(names, parts, others):
        red[n], deltas[n], new_m[n], new_v[n] = adamw_reduce(W[n], a, t, M[n], V[n], chip, "adamw_" + n)
    piece = add_chips(parts[-1], others[-1], chip, "reduce_add_chips")
    rep_all = all_gather([piece], "comm_gather_replicated")[0].reshape(-1, LANES)
    off = 0
    for n, rows in zip(REPLICATED, rep_rows):
        k = math.prod(W[n].shape)
        red[n] = rep_all[off:off + rows.shape[0]].reshape(-1)[:k].reshape(W[n].shape)
        off += rows.shape[0]
        deltas[n], new_m[n], new_v[n] = adamw(W[n], red[n], M[n], V[n], "adamw_" + n)
    return (loss, from_scan_order(gact)[None], *[red[n] for n in WEIGHTS], *[deltas[n] for n in WEIGHTS],
            *[new_m[n] for n in WEIGHTS], *[new_v[n] for n in WEIGHTS])
```

```python
import math

import jax
import jax.numpy as jnp
from jax import lax
from jax.experimental import pallas as pl
from jax.experimental.pallas import tpu as pltpu

f32 = jnp.float32
bf16 = jnp.bfloat16

D_MODEL = 1024
DEPTH = 2
PLE_DIM = 256
NORM_EPS = 1e-6
S5_WIDTH = 512
S5_GROUP = 16
S5_GROUPS = 32
S5_STATE = 64
S5_NS = S5_GROUPS * S5_STATE
LRU_WIDTH = 1280
LRU_HEADS = 10
LRU_HEAD_DIM = 128
LRU_C = 8.0
CONV_WIDTH = 4
N_DEV = 8

ADAM_LR = 0.001
ADAM_B1 = 0.9
ADAM_B2 = 0.999
ADAM_EPS = 1e-08
ADAM_WD = 0.01
ADAM_STEP = 10

LANES = 128
SUBLANES = 8
SEGS = SUBLANES
SCAN_CHUNK = 256
TOKEN_TILE = 256
MM_TILE_M = 1024
MM_TILE_N = 1408
MM_TILE_K_ROWS = 1280
PAIR = 2 * SUBLANES
VMEM_LIMIT_BYTES = 56 * 1024 * 1024
ELEMENTWISE_BLOCK_BYTES = 1024 * 1024

WEIGHTS = ['g_pre', 'w_in', 's5_a_re', 's5_a_im', 's5_log_dt', 's5_b_re', 's5_b_im', 's5_c_re', 's5_c_im',
           's5_d', 'w_glu', 'w_bs', 'conv_w', 'conv_b', 'lru_w_a', 'lru_b_a', 'lru_w_x', 'lru_b_x',
           'lru_lambda', 'w_bl', 'w_out', 'g_post', 'w_ple', 'w_ple_gate']
SHARDED = {'w_in': 2, 'w_glu': 2, 'w_bs': 2, 'conv_w': 2, 'w_bl': 1, 'w_out': 1, 'w_ple': 2, 'w_ple_gate': 1}
GATHER_BF16 = ['w_in', 'w_glu', 'w_bs', 'w_bl', 'w_out', 'w_ple', 'w_ple_gate']
REPLICATED = [n for n in WEIGHTS if n not in SHARDED]


def _sig(x):
    return 0.5 * jnp.tanh(0.5 * x) + 0.5


def _gelu_parts(x):
    k = math.sqrt(2.0 / math.pi)
    t = jnp.tanh(k * (x + 0.044715 * x * x * x))
    return t, k


def _gelu(x):
    t, _ = _gelu_parts(x)
    return 0.5 * x * (1.0 + t)


def _gelu_grad(x):
    t, k = _gelu_parts(x)
    return 0.5 * (1.0 + t) + 0.5 * x * (1.0 - t * t) * k * (1.0 + 3.0 * 0.044715 * x * x)


def _one_minus_sq(a, log_a):
    z = 2.0 * log_a
    series = -z * (1.0 + z * (0.5 + z * (1.0 / 6.0 + z * (1.0 / 24.0 + z * (1.0 / 120.0)))))
    return jnp.where(z > -0.05, series, 1.0 - a * a)


def _softplus_neg(lam):
    return jnp.maximum(-lam, 0.0) + jnp.log(1.0 + jnp.exp(-jnp.abs(lam)))


def _dot(a, b):
    return jnp.dot(a, b, preferred_element_type=f32)


def _dot_nt(a, b):
    return lax.dot_general(a, b, (((1,), (1,)), ((), ())), preferred_element_type=f32)


def _dot_tn(a, b):
    return lax.dot_general(a, b, (((0,), (0,)), ((), ())), preferred_element_type=f32)


def _S(shape, dtype=f32):
    return jax.ShapeDtypeStruct(shape, dtype)


def _full(shape):
    nd = len(shape)
    return pl.BlockSpec(shape, lambda *_: (0,) * nd)


def _rows(tile, width, col=0):
    return pl.BlockSpec((tile, width), lambda i: (i, col))


def _cp(*semantics):
    return pltpu.CompilerParams(dimension_semantics=semantics or None, vmem_limit_bytes=VMEM_LIMIT_BYTES)


def _tile(n, want):
    t = min(n, want)
    assert n % t == 0, (n, want)
    return t


def _row_tile(R, C=LANES):
    cap = max(SUBLANES, min(R, ELEMENTWISE_BLOCK_BYTES // (4 * C)))
    for t in range(cap - cap % SUBLANES, 0, -SUBLANES):
        if R % t == 0:
            return t
    return R


def _lanes(j):
    return slice(LANES * j, LANES * (j + 1))


def _step_rows(k, n=SUBLANES):
    return pl.ds(pl.multiple_of(k * n, n), n)


def to_scan_order(a):
    T, C = a.shape
    tc = _tile(T, SCAN_CHUNK)
    return a.reshape(T // tc, SEGS, tc // SEGS, C).transpose(0, 2, 1, 3).reshape(T, C)


def from_scan_order(a):
    T, C = a.shape
    tc = _tile(T, SCAN_CHUNK)
    return a.reshape(T // tc, tc // SEGS, SEGS, C).transpose(0, 2, 1, 3).reshape(T, C)


def _col_tile(n, cap):
    if n <= cap:
        return n
    for t in range(cap - cap % LANES, 0, -LANES):
        if n % t == 0:
            return t
    return n


def _resident(shape):
    nd = len(shape)
    return pl.BlockSpec(shape, lambda *_: (0,) * nd, pipeline_mode=pl.Buffered(1))


def mm_nn(a, b, name, out_dtype=f32):
    M, K = a.shape
    N = b.shape[1]
    tm, tn = _tile(M, MM_TILE_M), _col_tile(N, MM_TILE_N)

    def body(a_ref, b_ref, o_ref):
        o_ref[...] = _dot(a_ref[...].astype(bf16), b_ref[...].astype(bf16)).astype(out_dtype)

    return pl.pallas_call(
        body, name=name, grid=(M // tm, N // tn),
        in_specs=[pl.BlockSpec((tm, K), lambda i, j: (i, 0)), pl.BlockSpec((K, tn), lambda i, j: (0, j))],
        out_specs=pl.BlockSpec((tm, tn), lambda i, j: (i, j)),
        out_shape=_S((M, N), out_dtype), compiler_params=_cp("parallel", "parallel"),
    )(a, b)


def mm_tn(a, b, name):
    M, K = a.shape
    N = b.shape[1]
    tm, tk, tn = _tile(M, MM_TILE_M), _col_tile(K, MM_TILE_K_ROWS), _col_tile(N, MM_TILE_N)

    def body(a_ref, b_ref, o_ref):
        m = pl.program_id(2)
        part = _dot_tn(a_ref[...].astype(bf16), b_ref[...].astype(bf16))

        @pl.when(m == 0)
        def _():
            o_ref[...] = part

        @pl.when(m > 0)
        def _():
            o_ref[...] += part

    return pl.pallas_call(
        body, name=name, grid=(K // tk, N // tn, M // tm),
        in_specs=[pl.BlockSpec((tm, tk), lambda i, j, m: (m, i)), pl.BlockSpec((tm, tn), lambda i, j, m: (m, j))],
        out_specs=pl.BlockSpec((tk, tn), lambda i, j, m: (i, j)),
        out_shape=_S((K, N), f32),
        compiler_params=_cp("parallel", "parallel", "arbitrary"),
    )(a, b)


def rms_fwd(x, g, name):
    T = x.shape[0]
    tm = _tile(T, TOKEN_TILE)

    def body(x_ref, g_ref, h_ref):
        xv = x_ref[...]
        r = lax.rsqrt(jnp.mean(xv * xv, axis=-1, keepdims=True) + NORM_EPS)
        h_ref[...] = (xv * r * g_ref[...]).astype(bf16)

    return pl.pallas_call(
        body, name=name, grid=(T // tm,),
        in_specs=[_rows(tm, D_MODEL), _full((1, D_MODEL))], out_specs=_rows(tm, D_MODEL),
        out_shape=_S((T, D_MODEL), bf16), compiler_params=_cp("parallel"),
    )(x, g)


def _s5_discretise(a_re, a_im, log_dt, b_re_t, b_im_t):
    dt = jnp.exp(log_dt)
    mag = jnp.exp(a_re * dt)
    ab_re = mag * jnp.cos(a_im * dt)
    ab_im = mag * jnp.sin(a_im * dt)
    den = a_re * a_re + a_im * a_im
    nr, ni = ab_re - 1.0, ab_im
    z_re = (nr * a_re + ni * a_im) / den
    z_im = (ni * a_re - nr * a_im) / den
    bb_re = z_re[None] * b_re_t - z_im[None] * b_im_t
    bb_im = z_re[None] * b_im_t + z_im[None] * b_re_t
    return ab_re, ab_im, bb_re, bb_im


def s5_prep(a_re, a_im, log_dt, b_re_t, b_im_t, m, name):
    G, N = a_re.shape

    def body(are_ref, aim_ref, ldt_ref, bre_ref, bim_ref, ab_ref, pw_ref, bb_ref):
        are, aim, ldt = are_ref[...], aim_ref[...], ldt_ref[...]
        ab_re, ab_im, bb_re, bb_im = _s5_discretise(are, aim, ldt, bre_ref[...], bim_ref[...])
        ab_ref[0], ab_ref[1] = ab_re, ab_im
        bb_ref[0], bb_ref[1] = bb_re, bb_im
        dt = jnp.exp(ldt)
        for k in range(m):
            mag = jnp.exp(are * dt * (k + 1.0))
            pw_ref[0, k] = mag * jnp.cos(aim * dt * (k + 1.0))
            pw_ref[1, k] = mag * jnp.sin(aim * dt * (k + 1.0))

    return pl.pallas_call(
        body, name=name,
        out_shape=[_S((2, G, N)), _S((2, m, G, N)), _S((2, S5_GROUP, G, N))], compiler_params=_cp(),
    )(a_re, a_im, log_dt, b_re_t, b_im_t)


def s5_prep_bwd(a_re, a_im, log_dt, b_re_t, b_im_t, g_ab, g_bb, name):
    G, N = a_re.shape

    def body(are_ref, aim_ref, ldt_ref, bre_ref, bim_ref, gab_ref, gbb_ref, o_are, o_aim, o_ldt, o_bre, o_bim):
        _, vjp = jax.vjp(_s5_discretise, are_ref[...], aim_ref[...], ldt_ref[...], bre_ref[...], bim_ref[...])
        g_are, g_aim, g_ldt, g_bre, g_bim = vjp((gab_ref[0], gab_ref[1], gbb_ref[0], gbb_ref[1]))
        o_are[...], o_aim[...], o_ldt[...], o_bre[...], o_bim[...] = g_are, g_aim, g_ldt, g_bre, g_bim

    return pl.pallas_call(
        body, name=name,
        out_shape=[_S((G, N)), _S((G, N)), _S((G, 1)), _S((S5_GROUP, G, N)), _S((S5_GROUP, G, N))],
        compiler_params=_cp(),
    )(a_re, a_im, log_dt, b_re_t, b_im_t, g_ab, g_bb)


NB_S5 = S5_NS // LANES
CB_S5 = S5_WIDTH // LANES
SB_PER_CB = NB_S5 // CB_S5
GRP_PER_SB = LANES // S5_STATE
S5_JB = 8


def _bdb_mask():
    j = jnp.arange(NB_S5)
    own_rows = (j[:, None] % SB_PER_CB == jnp.arange(SB_PER_CB)[None, :]).astype(f32)
    eye = jnp.eye(GRP_PER_SB, dtype=f32)
    return own_rows[:, :, None, None, None, None, None] * eye[None, None, :, None, None, :, None]


def _pack_bdb(bb):
    v = jnp.transpose(bb.reshape(2, S5_GROUP, NB_S5, GRP_PER_SB, S5_STATE), (2, 3, 1, 0, 4))
    full = v[:, None, :, :, :, None, :] * _bdb_mask()
    return full.reshape(NB_S5, LANES, 2 * LANES)


def _unpack_bdb(g_bdb):
    g7 = g_bdb.reshape(NB_S5, SB_PER_CB, GRP_PER_SB, S5_GROUP, 2, GRP_PER_SB, S5_STATE)
    v = jnp.sum(g7 * _bdb_mask(), axis=(1, 5))
    return jnp.transpose(v, (3, 2, 0, 1, 4)).reshape(2, S5_GROUP, S5_GROUPS, S5_STATE)


def _pack_cdb(c_re, c_im):
    gl = S5_GROUPS // CB_S5
    c2 = jnp.stack([c_re, -c_im]).reshape(2, CB_S5, gl, S5_GROUP, S5_STATE)
    eye = jnp.eye(gl, dtype=f32)
    full = jnp.transpose(c2, (1, 0, 2, 4, 3))[:, :, :, :, None, :] * eye[None, None, :, None, :, None]
    return full.reshape(CB_S5, 2 * SB_PER_CB * LANES, LANES)


def _unpack_cdb(g_cdb):
    gl = S5_GROUPS // CB_S5
    g6 = g_cdb.reshape(CB_S5, 2, gl, S5_STATE, gl, S5_GROUP)
    eye = jnp.eye(gl, dtype=f32)
    v = jnp.sum(g6 * eye[None, None, :, None, :, None], axis=4)
    v = jnp.transpose(v, (1, 0, 2, 4, 3)).reshape(2, S5_GROUPS, S5_GROUP, S5_STATE)
    return v[0], -v[1]


def _state_cat(ref, c):
    w = SB_PER_CB * LANES
    return jnp.concatenate([ref[:, w * c:w * (c + 1)], ref[:, S5_NS + w * c:S5_NS + w * (c + 1)]], axis=1)


def _state_pair(ref, j):
    return jnp.concatenate([ref[:, _lanes(j)], ref[:, S5_NS + LANES * j:S5_NS + LANES * (j + 1)]], axis=1)


def s5_fwd(usg, bdb, cdb, dvec, abar_b, ptab_b, name, gather=()):
    T = usg.shape[0]
    tc = _tile(T, SCAN_CHUNK)
    m = tc // SEGS
    nsteps = T // tc
    ng = len(gather)
    assert ptab_b.shape == (2, m, SEGS, S5_NS) and m % 2 == 0

    def body(*refs):
        u_ref, bdb_ref, cdb_ref, d_ref, a_ref, p_ref = refs[:6]
        ys_ref, sre_ref, sim_ref, sbf_ref = refs[6 + ng:10 + ng]
        src_re, src_im, dst_re, dst_im, cin_ref, carry_ref = refs[10 + 2 * ng:16 + 2 * ng]
        i = pl.program_id(0)
        if ng:
            phases = _gather_phases([s.shape for s in gather], refs[6:6 + ng], refs[10 + ng:10 + 2 * ng],
                                    *refs[16 + 2 * ng:])
            for phase, step in zip(phases, (0, nsteps // 2, (3 * nsteps) // 4, nsteps - 1)):
                pl.when(i == step)(phase)

        @pl.when(i == 0)
        def _():
            carry_ref[...] = jnp.zeros_like(carry_ref)

        u = u_ref[...]
        ub = u.astype(bf16)
        for j in range(NB_S5):
            bu = _dot(ub[:, _lanes(j // SB_PER_CB)], bdb_ref[j])
            src_re[:, _lanes(j)] = bu[:, :LANES]
            src_im[:, _lanes(j)] = bu[:, LANES:]
        for j0 in range(0, NB_S5, S5_JB):
            def kstep(k, st):
                rows = _step_rows(k)
                out = []
                for q in range(S5_JB):
                    ln = _lanes(j0 + q)
                    sr, si = st[2 * q], st[2 * q + 1]
                    ar, ai = a_ref[0, :, ln], a_ref[1, :, ln]
                    nr = ar * sr - ai * si + src_re[rows, ln]
                    ni = ar * si + ai * sr + src_im[rows, ln]
                    dst_re[rows, ln] = nr
                    dst_im[rows, ln] = ni
                    out += [nr, ni]
                return tuple(out)

            ends = lax.fori_loop(0, m, kstep, tuple(jnp.zeros((SEGS, LANES), f32) for _ in range(2 * S5_JB)))
            for q in range(S5_JB):
                ln = _lanes(j0 + q)
                er, ei = ends[2 * q], ends[2 * q + 1]
                cr, ci = carry_ref[0, :, ln], carry_ref[1, :, ln]
                amr, ami = p_ref[0, m - 1, 0:1, ln], p_ref[1, m - 1, 0:1, ln]
                rows_r, rows_i = [], []
                for s in range(SEGS):
                    rows_r.append(cr)
                    rows_i.append(ci)
                    cr, ci = (er[s:s + 1, :] + amr * cr - ami * ci, ei[s:s + 1, :] + amr * ci + ami * cr)
                cin_ref[0, 0:SEGS, ln] = _stack_rows(rows_r)
                cin_ref[1, 0:SEGS, ln] = _stack_rows(rows_i)
                carry_ref[0, :, ln] = cr
                carry_ref[1, :, ln] = ci
        cin_ref[:, SEGS:, :] = cin_ref[:, 0:SEGS, :]

        def fix(k2, _):
            rows = _step_rows(k2, PAIR)
            pr = p_ref[0, pl.ds(2 * k2, 2)].reshape(PAIR, S5_NS)
            pi = p_ref[1, pl.ds(2 * k2, 2)].reshape(PAIR, S5_NS)
            cr, ci = cin_ref[0], cin_ref[1]
            sr = dst_re[rows, :] + pr * cr - pi * ci
            si = dst_im[rows, :] + pr * ci + pi * cr
            sre_ref[rows, :] = sr
            sim_ref[rows, :] = si
            sbf_ref[rows, 0:S5_NS] = sr.astype(bf16)
            sbf_ref[rows, S5_NS:] = si.astype(bf16)
            return 0

        lax.fori_loop(0, m // 2, fix, 0)
        for c in range(CB_S5):
            ys_ref[:, _lanes(c)] = _dot(_state_cat(sbf_ref, c), cdb_ref[c]) + d_ref[:, _lanes(c)] * u[:, _lanes(c)]

    st = lambda w: _rows(tc, w)
    outs = pl.pallas_call(
        body, name=name, grid=(nsteps,),
        in_specs=[_rows(tc, S5_WIDTH, 0), _resident(bdb.shape), _resident(cdb.shape), _full((1, S5_WIDTH)),
                  _resident((2, SEGS, S5_NS)), _resident((2, m, SEGS, S5_NS))] + [ANY] * ng,
        out_specs=[st(S5_WIDTH), st(S5_NS), st(S5_NS), st(2 * S5_NS)] + [ANY] * ng,
        out_shape=[_S((T, S5_WIDTH)), _S((T, S5_NS)), _S((T, S5_NS)), _S((T, 2 * S5_NS), bf16)] + (
            _gather_out_shapes(gather) if ng else []),
        scratch_shapes=[pltpu.VMEM((tc, S5_NS), f32)] * 4 + [pltpu.VMEM((2, PAIR, S5_NS), f32),
                                                             pltpu.VMEM((2, 1, S5_NS), f32)] + (
            _gather_semaphores(ng) if ng else []),
        compiler_params=_cp("arbitrary"),
    )(usg, bdb, cdb, dvec, abar_b, ptab_b, *gather)
    return outs[:4], list(outs[4:])


def s5_bwd(gys, usg, s_re, s_im, s_bf, bdb, cdb, dvec, abar_b, ptab_rev_b, name):
    T = gys.shape[0]
    tc = _tile(T, SCAN_CHUNK)
    m = tc // SEGS
    nch = T // tc
    hb = tc // SUBLANES

    def body(gy_ref, u_ref, sre_ref, sim_ref, hre_ref, him_ref, sbf_ref, bdb_ref, cdb_ref, d_ref, a_ref, p_ref,
             gu_ref, gab_ref, gd_ref, gbdb_ref, gcdb_ref,
             src_re, src_im, dst_re, dst_im, lam_ref, cin_ref, acc_ref, carry_ref):
        i = pl.program_id(0)

        @pl.when(i == 0)
        def _():
            carry_ref[...] = jnp.zeros_like(carry_ref)
            for ref in (gab_ref, gd_ref, gbdb_ref, gcdb_ref):
                ref[...] = jnp.zeros_like(ref)

        first = i == nch - 1
        gy = gy_ref[...]
        gyb = gy.astype(bf16)
        u = u_ref[...]
        ub = u.astype(bf16)
        w = SB_PER_CB * LANES
        for c in range(CB_S5):
            gs = _dot_nt(gyb[:, _lanes(c)], cdb_ref[c])
            src_re[:, w * c:w * (c + 1)] = gs[:, :w]
            src_im[:, w * c:w * (c + 1)] = gs[:, w:]
            gcdb_ref[c] += _dot_tn(_state_cat(sbf_ref, c), gyb[:, _lanes(c)])
        for j0 in range(0, NB_S5, S5_JB):
            def kstep(kk, st):
                rows = _step_rows(m - 1 - kk)
                out = []
                for q in range(S5_JB):
                    ln = _lanes(j0 + q)
                    lr, li = st[2 * q], st[2 * q + 1]
                    ar, ai = a_ref[0, :, ln], a_ref[1, :, ln]
                    nr = ar * lr + ai * li + src_re[rows, ln]
                    ni = ar * li - ai * lr + src_im[rows, ln]
                    dst_re[rows, ln] = nr
                    dst_im[rows, ln] = ni
                    out += [nr, ni]
                return tuple(out)

            ends = lax.fori_loop(0, m, kstep, tuple(jnp.zeros((SEGS, LANES), f32) for _ in range(2 * S5_JB)))
            for q in range(S5_JB):
                ln = _lanes(j0 + q)
                er, ei = ends[2 * q], ends[2 * q + 1]
                cr, ci = carry_ref[0, :, ln], carry_ref[1, :, ln]
                amr, ami = p_ref[0, 0, 0:1, ln], p_ref[1, 0, 0:1, ln]
                rows_r, rows_i = [None] * SEGS, [None] * SEGS
                for s in reversed(range(SEGS)):
                    rows_r[s], rows_i[s] = cr, ci
                    cr, ci = (er[s:s + 1, :] + amr * cr + ami * ci, ei[s:s + 1, :] + amr * ci - ami * cr)
                cin_ref[0, 0:SEGS, ln] = _stack_rows(rows_r)
                cin_ref[1, 0:SEGS, ln] = _stack_rows(rows_i)
                carry_ref[0, :, ln] = cr
                carry_ref[1, :, ln] = ci
        cin_ref[:, SEGS:, :] = cin_ref[:, 0:SEGS, :]
        acc_ref[...] = jnp.zeros_like(acc_ref)

        def fix_rows(rows, k2, prev_re, prev_im):
            pr = p_ref[0, pl.ds(2 * k2, 2)].reshape(PAIR, S5_NS)
            pi = p_ref[1, pl.ds(2 * k2, 2)].reshape(PAIR, S5_NS)
            cr, ci = cin_ref[0], cin_ref[1]
            lr = dst_re[rows, :] + pr * cr + pi * ci
            li = dst_im[rows, :] + pr * ci - pi * cr
            lam_ref[rows, 0:S5_NS] = lr.astype(bf16)
            lam_ref[rows, S5_NS:] = li.astype(bf16)
            acc_ref[0] += lr * prev_re + li * prev_im
            acc_ref[1] += li * prev_re - lr * prev_im

        last = slice(tc - SUBLANES, tc)
        wrap_re = _down_a_segment(sre_ref[last, :], jnp.where(first, 0.0, hre_ref[SUBLANES - 1:SUBLANES, :]))
        wrap_im = _down_a_segment(sim_ref[last, :], jnp.where(first, 0.0, him_ref[SUBLANES - 1:SUBLANES, :]))
        fix_rows(pl.ds(0, PAIR), 0, jnp.concatenate([wrap_re, sre_ref[0:SUBLANES, :]], axis=0),
                 jnp.concatenate([wrap_im, sim_ref[0:SUBLANES, :]], axis=0))

        def fix(k2, _):
            prev = pl.ds(pl.multiple_of(k2 * PAIR - SUBLANES, SUBLANES), PAIR)
            fix_rows(_step_rows(k2, PAIR), k2, sre_ref[prev, :], sim_ref[prev, :])
            return 0

        lax.fori_loop(1, m // 2, fix, 0)
        gab_ref[0] += jnp.sum(acc_ref[0], axis=0, keepdims=True)
        gab_ref[1] += jnp.sum(acc_ref[1], axis=0, keepdims=True)
        for c in range(CB_S5):
            x = gy[:, _lanes(c)] * d_ref[:, _lanes(c)]
            for j in range(SB_PER_CB * c, SB_PER_CB * (c + 1)):
                pair = _state_pair(lam_ref, j)
                x = x + _dot_nt(pair, bdb_ref[j])
                gbdb_ref[j] += _dot_tn(ub[:, _lanes(c)], pair)
            gu_ref[:, _lanes(c)] = x.astype(bf16)
        gd_ref[...] += jnp.sum(gy * u, axis=0, keepdims=True)

    rev = lambda i: (nch - 1 - i, 0)
    halo = lambda i: (jnp.maximum((nch - 1 - i) * hb - 1, 0), 0)
    blk = lambda wd: pl.BlockSpec((tc, wd), rev)
    return pl.pallas_call(
        body, name=name, grid=(nch,),
        in_specs=[blk(S5_WIDTH), blk(S5_WIDTH), blk(S5_NS), blk(S5_NS),
                  pl.BlockSpec((SUBLANES, S5_NS), halo), pl.BlockSpec((SUBLANES, S5_NS), halo), blk(2 * S5_NS),
                  _resident(bdb.shape), _resident(cdb.shape), _full((1, S5_WIDTH)),
                  _resident((2, SEGS, S5_NS)), _resident((2, m, SEGS, S5_NS))],
        out_specs=[blk(S5_WIDTH), _full((2, 1, S5_NS)), _full((1, S5_WIDTH)), _full(bdb.shape), _full(cdb.shape)],
        out_shape=[_S((T, S5_WIDTH), bf16), _S((2, 1, S5_NS)), _S((1, S5_WIDTH)), _S(bdb.shape), _S(cdb.shape)],
        scratch_shapes=[pltpu.VMEM((tc, S5_NS), f32)] * 4 + [
            pltpu.VMEM((tc, 2 * S5_NS), bf16), pltpu.VMEM((2, PAIR, S5_NS), f32), pltpu.VMEM((2, PAIR, S5_NS), f32),
            pltpu.VMEM((2, 1, S5_NS), f32)],
        compiler_params=_cp("arbitrary"),
    )(gys, usg, s_re, s_im, s_re, s_im, s_bf, bdb, cdb, dvec, abar_b, ptab_rev_b)


def s5_post_fwd(ys, usg, wglu, wbs, name):
    T = ys.shape[0]
    tm = _tile(T, TOKEN_TILE)

    def body(ys_ref, sg_ref, wglu_ref, wbs_ref, glu_ref, zs_ref):
        glu = _dot(_gelu(ys_ref[...]).astype(bf16), wglu_ref[...])
        sg = sg_ref[...]
        y2 = glu[:, :S5_WIDTH] * _sig(glu[:, S5_WIDTH:]) * (sg * _sig(sg))
        glu_ref[...] = glu
        zs_ref[...] = _dot(y2.astype(bf16), wbs_ref[...])

    return pl.pallas_call(
        body, name=name, grid=(T // tm,),
        in_specs=[_rows(tm, S5_WIDTH), _rows(tm, S5_WIDTH, 1), _resident((S5_WIDTH, 2 * S5_WIDTH)),
                  _resident((S5_WIDTH, D_MODEL))],
        out_specs=[_rows(tm, 2 * S5_WIDTH), _rows(tm, D_MODEL)],
        out_shape=[_S((T, 2 * S5_WIDTH)), _S((T, D_MODEL))], compiler_params=_cp("parallel"),
    )(ys, usg, wglu, wbs)


def _accumulate(ref, part, step):
    @pl.when(step == 0)
    def _():
        ref[...] = part

    @pl.when(step > 0)
    def _():
        ref[...] += part


def s5_post_bwd(gzs, glu, usg, ys, wbs, wglu, name):
    T = ys.shape[0]
    tm = _tile(T, TOKEN_TILE)

    def body(gzs_ref, glu_ref, sg_ref, ys_ref, wbs_ref, wglu_ref, gys_ref, gsg_ref, gwbs_ref, gwglu_ref):
        i = pl.program_id(0)
        glu = glu_ref[...]
        a, b = glu[:, :S5_WIDTH], glu[:, S5_WIDTH:]
        sg = sg_ref[...]
        ys = ys_ref[...]
        sb, ssg = _sig(b), _sig(sg)
        silu = sg * ssg
        _accumulate(gwbs_ref, _dot_tn((a * sb * silu).astype(bf16), gzs_ref[...]), i)
        gy2 = _dot_nt(gzs_ref[...], wbs_ref[...])
        g_a = gy2 * sb * silu
        g_b = gy2 * a * sb * (1.0 - sb) * silu
        gsg_ref[...] = (gy2 * a * sb * ssg * (1.0 + sg * (1.0 - ssg))).astype(bf16)
        gglu = jnp.concatenate([g_a, g_b], axis=1).astype(bf16)
        _accumulate(gwglu_ref, _dot_tn(_gelu(ys).astype(bf16), gglu), i)
        gys_ref[...] = _dot_nt(gglu, wglu_ref[...]) * _gelu_grad(ys)

    return pl.pallas_call(
        body, name=name, grid=(T // tm,),
        in_specs=[_rows(tm, D_MODEL), _rows(tm, 2 * S5_WIDTH), _rows(tm, S5_WIDTH, 1), _rows(tm, S5_WIDTH),
                  _resident((S5_WIDTH, D_MODEL)), _resident((S5_WIDTH, 2 * S5_WIDTH))],
        out_specs=[_rows(tm, S5_WIDTH), _rows(tm, S5_WIDTH), _full((S5_WIDTH, D_MODEL)), _full((S5_WIDTH, 2 * S5_WIDTH))],
        out_shape=[_S((T, S5_WIDTH)), _S((T, S5_WIDTH), bf16), _S((S5_WIDTH, D_MODEL)), _S((S5_WIDTH, 2 * S5_WIDTH))],
        compiler_params=_cp("arbitrary"),
    )(gzs, glu, usg, ys, wbs, wglu)


NB_LRU = LRU_WIDTH // LANES
LRU_JB = 5
TAPS_BACK = CONV_WIDTH - 1
EDGE = TAPS_BACK * SUBLANES
HALO_ROWS = 4 * SUBLANES


def _down_a_segment(blk, entering_row):
    sub = lax.broadcasted_iota(jnp.int32, blk.shape, 0)
    return jnp.where(sub == 0, entering_row, pltpu.roll(blk, 1, 0))


def _up_a_segment(blk, entering_row):
    sub = lax.broadcasted_iota(jnp.int32, blk.shape, 0)
    return jnp.where(sub == SUBLANES - 1, entering_row, pltpu.roll(blk, SUBLANES - 1, 0))


def _stack_rows(rows):
    sub = lax.broadcasted_iota(jnp.int32, (SUBLANES,) + rows[0].shape[1:], 0)
    out = jnp.broadcast_to(rows[0], sub.shape)
    for s in range(1, SUBLANES):
        out = jnp.where(sub == s, rows[s], out)
    return out


def _fill_conv_window(xe, x_ref, xh_ref, is_first, tc):
    xe[EDGE:, :] = x_ref[...]
    for i in range(1, TAPS_BACK + 1):
        row = HALO_ROWS - SUBLANES * i + SUBLANES - 1
        entering = jnp.where(is_first, 0.0, xh_ref[row:row + 1, :])
        blk = x_ref[tc - SUBLANES * i:tc - SUBLANES * (i - 1), :]
        xe[EDGE - SUBLANES * i:EDGE - SUBLANES * (i - 1), :] = _down_a_segment(blk, entering)


def lru_fwd(lx, convw, convb, wa, wx, ba, bx, lam, name):
    T = lx.shape[0]
    tc = _tile(T, SCAN_CHUNK)
    m = tc // SEGS
    hb = tc // HALO_ROWS

    def body(x_ref, xh_ref, cw_ref, cb_ref, wa_ref, wx_ref, ba_ref, bx_ref, lam_ref,
             c_ref, r_ref, i_ref, h_ref, xe, src_a, src_b, dst_a, dst_h, cin_ref, carry_ref):
        i = pl.program_id(0)

        @pl.when(i == 0)
        def _():
            carry_ref[...] = jnp.zeros_like(carry_ref)

        _fill_conv_window(xe, x_ref, xh_ref, i == 0, tc)
        c = cb_ref[...] + cw_ref[0:1, :] * xe[0:tc, :]
        for k in range(1, CONV_WIDTH):
            c = c + cw_ref[k:k + 1, :] * xe[SUBLANES * k:SUBLANES * k + tc, :]
        c_ref[...] = c
        sp = _softplus_neg(lam_ref[...])
        for j in range(NB_LRU):
            ln = _lanes(j)
            cj = c[:, ln]
            cjb = cj.astype(bf16)
            r = _sig(_dot(cjb, wa_ref[j]) + ba_ref[:, ln])
            g = _sig(_dot(cjb, wx_ref[j]) + bx_ref[:, ln])
            r_ref[:, ln] = r
            i_ref[:, ln] = g
            log_a = -LRU_C * r * sp[:, ln]
            a = jnp.exp(log_a)
            src_a[:, ln] = a
            src_b[:, ln] = jnp.sqrt(_one_minus_sq(a, log_a)) * (g * cj)
        for j0 in range(0, NB_LRU, LRU_JB):
            def kstep(k, st):
                rows = _step_rows(k)
                out = []
                for q in range(LRU_JB):
                    ln = _lanes(j0 + q)
                    hh, ac = st[2 * q], st[2 * q + 1]
                    a = src_a[rows, ln]
                    hh = a * hh + src_b[rows, ln]
                    ac = a * ac
                    dst_h[rows, ln] = hh
                    dst_a[rows, ln] = ac
                    out += [hh, ac]
                return tuple(out)

            init = tuple(jnp.zeros((SEGS, LANES), f32) if q % 2 == 0 else jnp.ones((SEGS, LANES), f32)
                         for q in range(2 * LRU_JB))
            ends = lax.fori_loop(0, m, kstep, init)
            for q in range(LRU_JB):
                ln = _lanes(j0 + q)
                eh, ea = ends[2 * q], ends[2 * q + 1]
                cr = carry_ref[:, ln]
                rows_c = []
                for s in range(SEGS):
                    rows_c.append(cr)
                    cr = eh[s:s + 1, :] + ea[s:s + 1, :] * cr
                cin_ref[:, ln] = _stack_rows(rows_c)
                carry_ref[:, ln] = cr

        def fix(k, _):
            rows = _step_rows(k)
            h_ref[rows, :] = dst_h[rows, :] + dst_a[rows, :] * cin_ref[...]
            return 0

        lax.fori_loop(0, m, fix, 0)

    wide = lambda: _rows(tc, LRU_WIDTH)
    buf = lambda rows: pltpu.VMEM((rows, LRU_WIDTH), f32)
    return pl.pallas_call(
        body, name=name, grid=(T // tc,),
        in_specs=[wide(), pl.BlockSpec((HALO_ROWS, LRU_WIDTH), lambda i: (jnp.maximum(i * hb - 1, 0), 0)),
                  _full((CONV_WIDTH, LRU_WIDTH)), _full((1, LRU_WIDTH)),
                  _full((LRU_HEADS, LRU_HEAD_DIM, LRU_HEAD_DIM)), _full((LRU_HEADS, LRU_HEAD_DIM, LRU_HEAD_DIM)),
                  _full((1, LRU_WIDTH)), _full((1, LRU_WIDTH)), _full((1, LRU_WIDTH))],
        out_specs=[wide(), wide(), wide(), wide()],
        out_shape=[_S((T, LRU_WIDTH))] * 4,
        scratch_shapes=[buf(tc + EDGE), buf(tc), buf(tc), buf(tc), buf(tc), buf(SEGS), buf(1)],
        compiler_params=_cp("arbitrary"),
    )(lx, lx, convw, convb, wa, wx, ba, bx, lam)


def lru_bwd(gh, h, c, r, gi, lx, convw, wa, wx, lam, name):
    T = gh.shape[0]
    tc = _tile(T, SCAN_CHUNK)
    m = tc // SEGS
    nch = T // tc

    def body(gh_ref, h_ref, hh_ref, c_ref, r_ref, i_ref, x_ref, xh_ref, cw_ref, wa_ref, wx_ref, lam_ref,
             glx_ref, gwa_ref, gwx_ref, gba_ref, gbx_ref, glam_ref, gcb_ref, gcw_ref,
             src_a, src_m, dst_a, dst_m, mbuf, hbuf, xe, gce, cin_ref, gcc_ref, carry_ref):
        i = pl.program_id(0)

        @pl.when(i == 0)
        def _():
            carry_ref[...] = jnp.zeros_like(carry_ref)
            gcc_ref[...] = jnp.zeros_like(gcc_ref)
            for ref in (gwa_ref, gwx_ref, gba_ref, gbx_ref, glam_ref, gcb_ref, gcw_ref):
                ref[...] = jnp.zeros_like(ref)

        first = i == nch - 1
        last = slice(tc - SUBLANES, tc)
        hbuf[SUBLANES:, :] = h_ref[...]
        hbuf[0:SUBLANES, :] = _down_a_segment(h_ref[last, :], jnp.where(first, 0.0, hh_ref[SUBLANES - 1:SUBLANES, :]))
        _fill_conv_window(xe, x_ref, xh_ref, first, tc)
        lam_v = lam_ref[...]
        sp = _softplus_neg(lam_v)
        a_all = jnp.exp(-LRU_C * r_ref[...] * sp)
        src_a[...] = a_all
        src_m[...] = a_all * gh_ref[...]
        for j0 in range(0, NB_LRU, LRU_JB):
            def kstep(kk, st):
                rows = _step_rows(m - 1 - kk)
                out = []
                for q in range(LRU_JB):
                    ln = _lanes(j0 + q)
                    mu, ac = st[2 * q], st[2 * q + 1]
                    a = src_a[rows, ln]
                    mu = a * mu + src_m[rows, ln]
                    ac = a * ac
                    dst_m[rows, ln] = mu
                    dst_a[rows, ln] = ac
                    out += [mu, ac]
                return tuple(out)

            init = tuple(jnp.zeros((SEGS, LANES), f32) if q % 2 == 0 else jnp.ones((SEGS, LANES), f32)
                         for q in range(2 * LRU_JB))
            ends = lax.fori_loop(0, m, kstep, init)
            for q in range(LRU_JB):
                ln = _lanes(j0 + q)
                em, ea = ends[2 * q], ends[2 * q + 1]
                cr = carry_ref[:, ln]
                rows_c = [None] * SEGS
                for s in reversed(range(SEGS)):
                    rows_c[s] = cr
                    cr = em[s:s + 1, :] + ea[s:s + 1, :] * cr
                cin_ref[:, ln] = _stack_rows(rows_c)
                carry_ref[:, ln] = cr

        def fix(k, _):
            rows = _step_rows(k)
            mbuf[rows, :] = dst_m[rows, :] + dst_a[rows, :] * cin_ref[...]
            return 0

        lax.fori_loop(0, m, fix, 0)
        mbuf[tc:, :] = _up_a_segment(mbuf[0:SUBLANES, :], cin_ref[SUBLANES - 1:SUBLANES, :])
        sneg = _sig(-lam_v)
        for j in range(NB_LRU):
            ln = _lanes(j)
            lamt = gh_ref[:, ln] + mbuf[SUBLANES:, ln]
            rj, ij, cj = r_ref[:, ln], i_ref[:, ln], c_ref[:, ln]
            log_a = -LRU_C * rj * sp[:, ln]
            a = src_a[:, ln]
            om = _one_minus_sq(a, log_a)
            inv_mult = lax.rsqrt(om)
            mult = om * inv_mult
            g_a = lamt * hbuf[0:tc, ln]
            g_mult = lamt * ij * cj
            g_i = lamt * mult * cj
            g_c = lamt * mult * ij
            g_log_a = g_a * a - g_mult * a * a * inv_mult
            glam_ref[:, ln] += jnp.sum(g_log_a * rj, axis=0, keepdims=True) * LRU_C * sneg[:, ln]
            g_ra = g_log_a * (-LRU_C) * sp[:, ln] * rj * (1.0 - rj)
            g_ia = g_i * ij * (1.0 - ij)
            gba_ref[:, ln] += jnp.sum(g_ra, axis=0, keepdims=True)
            gbx_ref[:, ln] += jnp.sum(g_ia, axis=0, keepdims=True)
            cjb, grb, gib = cj.astype(bf16), g_ra.astype(bf16), g_ia.astype(bf16)
            gwa_ref[j] += _dot_tn(cjb, grb)
            gwx_ref[j] += _dot_tn(cjb, gib)
            g_c = g_c + _dot_nt(grb, wa_ref[j]) + _dot_nt(gib, wx_ref[j])
            gce[0:tc, ln] = g_c
            gcb_ref[:, ln] += jnp.sum(g_c, axis=0, keepdims=True)
        for d in range(TAPS_BACK):
            blk = slice(SUBLANES * d, SUBLANES * (d + 1))
            gce[tc + SUBLANES * d:tc + SUBLANES * (d + 1), :] = _up_a_segment(gce[blk, :], gcc_ref[SUBLANES * d:SUBLANES * d + 1, :])
        gcc_ref[...] = gce[0:EDGE, :]
        gc = gce[0:tc, :]
        glx = cw_ref[CONV_WIDTH - 1:CONV_WIDTH, :] * gc
        gcw_ref[CONV_WIDTH - 1:CONV_WIDTH, :] += jnp.sum(gc * xe[EDGE:EDGE + tc, :], axis=0, keepdims=True)
        for k in range(CONV_WIDTH - 1):
            off = SUBLANES * (CONV_WIDTH - 1 - k)
            glx = glx + cw_ref[k:k + 1, :] * gce[off:off + tc, :]
            gcw_ref[k:k + 1, :] += jnp.sum(gc * xe[EDGE - off:EDGE - off + tc, :], axis=0, keepdims=True)
        glx_ref[...] = glx.astype(bf16)

    rev = lambda i: (nch - 1 - i, 0)
    halo = lambda rows: (lambda i: (jnp.maximum((nch - 1 - i) * (tc // rows) - 1, 0), 0))
    wide = lambda: pl.BlockSpec((tc, LRU_WIDTH), rev)
    vec = lambda: _full((1, LRU_WIDTH))
    hd = lambda: _full((LRU_HEADS, LRU_HEAD_DIM, LRU_HEAD_DIM))
    buf = lambda rows: pltpu.VMEM((rows, LRU_WIDTH), f32)
    return pl.pallas_call(
        body, name=name, grid=(nch,),
        in_specs=[wide(), wide(), pl.BlockSpec((SUBLANES, LRU_WIDTH), halo(SUBLANES)), wide(), wide(), wide(), wide(),
                  pl.BlockSpec((HALO_ROWS, LRU_WIDTH), halo(HALO_ROWS)), _full((CONV_WIDTH, LRU_WIDTH)), hd(), hd(), vec()],
        out_specs=[wide(), hd(), hd(), vec(), vec(), vec(), vec(), _full((CONV_WIDTH, LRU_WIDTH))],
        out_shape=[_S((T, LRU_WIDTH), bf16), _S((LRU_HEADS, LRU_HEAD_DIM, LRU_HEAD_DIM)),
                   _S((LRU_HEADS, LRU_HEAD_DIM, LRU_HEAD_DIM)), _S((1, LRU_WIDTH)), _S((1, LRU_WIDTH)),
                   _S((1, LRU_WIDTH)), _S((1, LRU_WIDTH)), _S((CONV_WIDTH, LRU_WIDTH))],
        scratch_shapes=[buf(tc), buf(tc), buf(tc), buf(tc), buf(tc + SUBLANES), buf(tc + SUBLANES), buf(tc + EDGE),
                        buf(tc + EDGE), buf(SEGS), buf(EDGE), buf(1)],
        compiler_params=_cp("arbitrary"),
    )(gh, h, h, c, r, gi, lx, lx, convw, wa, wx, lam)


def merge_fwd(h, lg, zs, gsl, x, p, wbl, wout, gpost, wple, wpg, name):
    T = x.shape[0]
    tm = _tile(T, TOKEN_TILE)

    def body(h_ref, lg_ref, zs_ref, gs_ref, gl_ref, x_ref, p_ref, wbl_ref, wout_ref, gp_ref, wple_ref, wpg_ref,
             zl_ref, mix_ref, xo_ref):
        lg_v = lg_ref[...]
        yl = h_ref[...] * (lg_v * _sig(lg_v))
        zl = _dot(yl.astype(bf16), wbl_ref[...])
        merged = _sig(gs_ref[...]) * zs_ref[...] + _sig(gl_ref[...]) * zl
        mix = _dot(merged.astype(bf16), wout_ref[...])
        r2 = lax.rsqrt(jnp.mean(mix * mix, axis=-1, keepdims=True) + NORM_EPS)
        x1 = x_ref[...] + mix * r2 * gp_ref[...]
        q = _dot(x1.astype(bf16), wpg_ref[...])
        pe = _dot(p_ref[...].astype(bf16), wple_ref[...])
        zl_ref[...], mix_ref[...] = zl, mix
        xo_ref[...] = x1 + pe * _sig(q)

    dm = lambda: _rows(tm, D_MODEL)
    return pl.pallas_call(
        body, name=name, grid=(T // tm,),
        in_specs=[_rows(tm, LRU_WIDTH), _rows(tm, LRU_WIDTH), dm(), _rows(tm, D_MODEL, 0), _rows(tm, D_MODEL, 1), dm(),
                  _rows(tm, PLE_DIM), _resident((LRU_WIDTH, D_MODEL)), _resident((D_MODEL, D_MODEL)), _full((1, D_MODEL)),
                  _resident((PLE_DIM, D_MODEL)), _resident((D_MODEL, D_MODEL))],
        out_specs=[dm(), dm(), dm()],
        out_shape=[_S((T, D_MODEL))] * 3, compiler_params=_cp("parallel"),
    )(h, lg, zs, gsl, gsl, x, p, wbl, wout, gpost, wple, wpg)


def post_bwd(gx2, mix, x, p, wpg, wple, gpost, name):
    T = x.shape[0]
    tm = _tile(T, TOKEN_TILE)

    def body(gx2_ref, mix_ref, x_ref, p_ref, wpg_ref, wple_ref, gp_ref, gres_ref, gmix_ref, ggp_ref, gwpg_ref, gwple_ref):
        i = pl.program_id(0)
        gx2 = gx2_ref[...]
        mix = mix_ref[...]
        gp = gp_ref[...]
        r2 = lax.rsqrt(jnp.mean(mix * mix, axis=-1, keepdims=True) + NORM_EPS)
        nrm = mix * r2
        x1b = (x_ref[...] + nrm * gp).astype(bf16)
        pb = p_ref[...].astype(bf16)
        sq = _sig(_dot(x1b, wpg_ref[...]))
        pe = _dot(pb, wple_ref[...])
        gq = (gx2 * pe * sq * (1.0 - sq)).astype(bf16)
        _accumulate(gwple_ref, _dot_tn(pb, (gx2 * sq).astype(bf16)), i)
        _accumulate(gwpg_ref, _dot_tn(x1b, gq), i)
        gx1 = gx2 + _dot_nt(gq, wpg_ref[...])
        gres_ref[...] = gx1
        _accumulate(ggp_ref, jnp.sum(gx1 * nrm, axis=0, keepdims=True), i)
        gy = gx1 * gp
        gmix_ref[...] = (r2 * (gy - nrm * jnp.mean(gy * nrm, axis=-1, keepdims=True))).astype(bf16)

    dm = lambda: _rows(tm, D_MODEL)
    return pl.pallas_call(
        body, name=name, grid=(T // tm,),
        in_specs=[dm(), dm(), dm(), _rows(tm, PLE_DIM), _resident((D_MODEL, D_MODEL)), _resident((PLE_DIM, D_MODEL)),
                  _full((1, D_MODEL))],
        out_specs=[dm(), dm(), _full((1, D_MODEL)), _full((D_MODEL, D_MODEL)), _full((PLE_DIM, D_MODEL))],
        out_shape=[_S((T, D_MODEL)), _S((T, D_MODEL), bf16), _S((1, D_MODEL)), _S((D_MODEL, D_MODEL)),
                   _S((PLE_DIM, D_MODEL))],
        compiler_params=_cp("arbitrary"),
    )(gx2, mix, x, p, wpg, wple, gpost)


def gate_bwd(gmix, zl, zs, gsl, h, lg, wout, wbl, name):
    T = zl.shape[0]
    tm = _tile(T, TOKEN_TILE)

    def body(gmix_ref, zl_ref, zs_ref, gs_ref, gl_ref, h_ref, lg_ref, wout_ref, wbl_ref,
             gzs_ref, ggsl_ref, gh_ref, glg_ref, gwout_ref, gwbl_ref):
        i = pl.program_id(0)
        gmix = gmix_ref[...]
        gmerged = _dot_nt(gmix, wout_ref[...])
        zs, zl = zs_ref[...], zl_ref[...]
        ss, sl = _sig(gs_ref[...]), _sig(gl_ref[...])
        _accumulate(gwout_ref, _dot_tn((ss * zs + sl * zl).astype(bf16), gmix), i)
        gzs_ref[...] = (gmerged * ss).astype(bf16)
        gzl = (gmerged * sl).astype(bf16)
        ggsl_ref[:, :D_MODEL] = (gmerged * zs * ss * (1.0 - ss)).astype(bf16)
        ggsl_ref[:, D_MODEL:] = (gmerged * zl * sl * (1.0 - sl)).astype(bf16)
        lg_v, hv = lg_ref[...], h_ref[...]
        slg = _sig(lg_v)
        silu = lg_v * slg
        _accumulate(gwbl_ref, _dot_tn((hv * silu).astype(bf16), gzl), i)
        gyl = _dot_nt(gzl, wbl_ref[...])
        gh_ref[...] = gyl * silu
        glg_ref[...] = (gyl * hv * slg * (1.0 + lg_v * (1.0 - slg))).astype(bf16)

    dm = lambda: _rows(tm, D_MODEL)
    lw = lambda: _rows(tm, LRU_WIDTH)
    return pl.pallas_call(
        body, name=name, grid=(T // tm,),
        in_specs=[dm(), dm(), dm(), _rows(tm, D_MODEL, 0), _rows(tm, D_MODEL, 1), lw(), lw(),
                  _resident((D_MODEL, D_MODEL)), _resident((LRU_WIDTH, D_MODEL))],
        out_specs=[dm(), _rows(tm, 2 * D_MODEL), lw(), lw(), _full((D_MODEL, D_MODEL)), _full((LRU_WIDTH, D_MODEL))],
        out_shape=[_S((T, D_MODEL), bf16), _S((T, 2 * D_MODEL), bf16), _S((T, LRU_WIDTH)), _S((T, LRU_WIDTH), bf16),
                   _S((D_MODEL, D_MODEL)), _S((LRU_WIDTH, D_MODEL))],
        compiler_params=_cp("arbitrary"),
    )(gmix, zl, zs, gsl, gsl, h, lg, wout, wbl)


def in_proj_bwd(pieces, win, x, gres, g, name):
    T = x.shape[0]
    tm = _tile(T, MM_TILE_M // 2)
    widths = [pc.shape[1] for pc in pieces]
    offs = [sum(widths[:k]) for k in range(len(widths))]

    def body(*refs):
        pc_refs = refs[:len(widths)]
        w_ref, x_ref, gres_ref, g_ref, gx_ref, gg_ref = refs[len(widths):]
        i = pl.program_id(0)
        ghv = _dot_nt(pc_refs[0][...], w_ref[:, offs[0]:offs[0] + widths[0]])
        for k in range(1, len(widths)):
            ghv = ghv + _dot_nt(pc_refs[k][...], w_ref[:, offs[k]:offs[k] + widths[k]])
        xv = x_ref[...]
        r = lax.rsqrt(jnp.mean(xv * xv, axis=-1, keepdims=True) + NORM_EPS)
        nrm = xv * r
        gy = ghv * g_ref[...]
        gx_ref[...] = gres_ref[...] + r * (gy - nrm * jnp.mean(gy * nrm, axis=-1, keepdims=True))
        _accumulate(gg_ref, jnp.sum(ghv * nrm, axis=0, keepdims=True), i)

    dm = lambda: _rows(tm, D_MODEL)
    return pl.pallas_call(
        body, name=name, grid=(T // tm,),
        in_specs=[_rows(tm, wd) for wd in widths] + [_resident(win.shape), dm(), dm(), _full((1, D_MODEL))],
        out_specs=[dm(), _full((1, D_MODEL))],
        out_shape=[_S((T, D_MODEL)), _S((1, D_MODEL))], compiler_params=_cp("arbitrary"),
    )(*pieces, win, x, gres, g)


def loss_head(y, target, name):
    T = y.shape[0]
    tm = _tile(T, TOKEN_TILE)

    def body(y_ref, t_ref, l_ref, g_ref):
        i = pl.program_id(0)
        e = y_ref[...] - t_ref[...]
        g_ref[...] = e * (1.0 / D_MODEL)
        part = 0.5 * jnp.sum(jnp.sum(e * e, axis=-1, keepdims=True) * (1.0 / D_MODEL), axis=0, keepdims=True)

        @pl.when(i == 0)
        def _():
            l_ref[...] = part

        @pl.when(i > 0)
        def _():
            l_ref[...] += part

    return pl.pallas_call(
        body, name=name, grid=(T // tm,),
        in_specs=[_rows(tm, D_MODEL), _rows(tm, D_MODEL)], out_specs=[_full((1, 1)), _rows(tm, D_MODEL)],
        out_shape=[_S((1, 1)), _S((T, D_MODEL))],
        compiler_params=_cp("arbitrary"),
    )(y, target)


def _s5_operands(w, m, tag):
    b_re_t = jnp.transpose(w['s5_b_re'], (2, 0, 1))
    b_im_t = jnp.transpose(w['s5_b_im'], (2, 0, 1))
    ldt = w['s5_log_dt'][:, None]
    ab, pw, bb = s5_prep(w['s5_a_re'], w['s5_a_im'], ldt, b_re_t, b_im_t, m, "s5_prep" + tag)
    over_sublanes = lambda t: jnp.broadcast_to(t[..., None, :], t.shape[:-1] + (SEGS, S5_NS))
    ptab = pw.reshape(2, m, S5_NS)
    return dict(abar_b=over_sublanes(ab.reshape(2, S5_NS)), ptab_b=over_sublanes(ptab),
                ptab_rev_b=over_sublanes(ptab[:, ::-1, :]), bdb=_pack_bdb(bb).astype(bf16),
                cdb=_pack_cdb(w['s5_c_re'], w['s5_c_im']).astype(bf16), dvec=w['s5_d'][None, :],
                prep_in=(w['s5_a_re'], w['s5_a_im'], ldt, b_re_t, b_im_t))


def layer_fwd(x, p, w, tag, gather=()):
    T = x.shape[0]
    m = _tile(T, SCAN_CHUNK) // SEGS
    s5 = _s5_operands(w, m, tag)
    h_bf = rms_fwd(x, w['g_pre'][None, :], "rms_fwd" + tag)
    win = w['w_in']
    usg = mm_nn(h_bf, win[:, :2 * S5_WIDTH], "proj_s5" + tag)
    lx = mm_nn(h_bf, win[:, 2 * S5_WIDTH:2 * S5_WIDTH + LRU_WIDTH], "proj_lx" + tag)
    lg = mm_nn(h_bf, win[:, 2 * S5_WIDTH + LRU_WIDTH:2 * S5_WIDTH + 2 * LRU_WIDTH], "proj_lg" + tag)
    gsl = mm_nn(h_bf, win[:, 2 * S5_WIDTH + 2 * LRU_WIDTH:], "proj_gate" + tag)
    (ys, s_re, s_im, s_bf), gathered = s5_fwd(usg, s5['bdb'], s5['cdb'], s5['dvec'], s5['abar_b'], s5['ptab_b'],
                                              "s5_fwd" + tag, gather)
    glu, zs = s5_post_fwd(ys, usg, w['w_glu'], w['w_bs'], "s5_post_fwd" + tag)
    wa, wx = w['lru_w_a'].astype(bf16), w['lru_w_x'].astype(bf16)
    c, r, gi, hs = lru_fwd(lx, w['conv_w'], w['conv_b'][None, :], wa, wx, w['lru_b_a'][None, :], w['lru_b_x'][None, :],
                           w['lru_lambda'][None, :], "lru_fwd" + tag)
    zl, mix, x_out = merge_fwd(hs, lg, zs, gsl, x, p, w['w_bl'], w['w_out'], w['g_post'][None, :],
                               w['w_ple'], w['w_ple_gate'], "merge_fwd" + tag)
    saved = dict(x=x, p=p, h_bf=h_bf, usg=usg, lx=lx, lg=lg, gsl=gsl, ys=ys, s_re=s_re, s_im=s_im, s_bf=s_bf, glu=glu,
                 zs=zs, c=c, r=r, gi=gi, hs=hs, zl=zl, mix=mix, s5=s5, wa=wa, wx=wx)
    return x_out, saved, gathered


def layer_bwd(gx_out, w, sv, tag):
    s5 = sv['s5']
    g = {}
    gres, gmix, g_gpost, g['w_ple_gate'], g['w_ple'] = post_bwd(
        gx_out, sv['mix'], sv['x'], sv['p'], w['w_ple_gate'], w['w_ple'], w['g_post'][None, :], "post_bwd" + tag)
    gzs, ggsl, g_h, g_lg, g['w_out'], g['w_bl'] = gate_bwd(
        gmix, sv['zl'], sv['zs'], sv['gsl'], sv['hs'], sv['lg'], w['w_out'], w['w_bl'], "gate_bwd" + tag)
    g['g_post'] = g_gpost[0]
    (g_lx, g_wa, g_wx, g_ba, g_bx, g_lam, g_cb, g_cw) = lru_bwd(
        g_h, sv['hs'], sv['c'], sv['r'], sv['gi'], sv['lx'], w['conv_w'], sv['wa'], sv['wx'],
        w['lru_lambda'][None, :], "lru_bwd" + tag)
    g['lru_w_a'], g['lru_w_x'] = g_wa, g_wx
    g['lru_b_a'], g['lru_b_x'], g['lru_lambda'], g['conv_b'], g['conv_w'] = g_ba[0], g_bx[0], g_lam[0], g_cb[0], g_cw
    g_ys, g_sg, g['w_bs'], g['w_glu'] = s5_post_bwd(gzs, sv['glu'], sv['usg'], sv['ys'], w['w_bs'], w['w_glu'],
                                                    "s5_post_bwd" + tag)
    g_u, g_ab, g_d, g_bdb, g_cdb = s5_bwd(g_ys, sv['usg'], sv['s_re'], sv['s_im'], sv['s_bf'], s5['bdb'], s5['cdb'],
                                          s5['dvec'], s5['abar_b'], s5['ptab_rev_b'], "s5_bwd" + tag)
    g['s5_d'] = g_d[0]
    g['s5_c_re'], g['s5_c_im'] = _unpack_cdb(g_cdb)
    g_are, g_aim, g_ldt, g_bre_t, g_bim_t = s5_prep_bwd(*s5['prep_in'], g_ab.reshape(2, S5_GROUPS, S5_STATE),
                                                       _unpack_bdb(g_bdb), "s5_prep_bwd" + tag)
    g['s5_a_re'], g['s5_a_im'], g['s5_log_dt'] = g_are, g_aim, g_ldt
    g['s5_b_re'] = jnp.transpose(g_bre_t, (1, 2, 0))
    g['s5_b_im'] = jnp.transpose(g_bim_t, (1, 2, 0))
    pieces = [g_u, g_sg, g_lx, g_lg, ggsl]
    g['w_in'] = jnp.concatenate([mm_tn(sv['h_bf'], pc, "gw_in%d%s" % (k, tag)) for k, pc in enumerate(pieces)], axis=1)
    gx, g_gpre = in_proj_bwd(pieces, w['w_in'], sv['x'], gres, w['g_pre'][None, :], "in_proj_bwd" + tag)
    g['g_pre'] = g_gpre[0]
    return gx, g


def _as_2d(a):
    return a.reshape((-1, a.shape[-1])) if a.ndim > 1 else a.reshape((1, -1))


def _adamw_update(w, gv, m, v):
    nm = ADAM_B1 * m + (1.0 - ADAM_B1) * gv
    nv = ADAM_B2 * v + (1.0 - ADAM_B2) * (gv * gv)
    bc1 = 1.0 - ADAM_B1 ** ADAM_STEP
    bc2 = 1.0 - ADAM_B2 ** ADAM_STEP
    return -ADAM_LR * ((nm / bc1) / (jnp.sqrt(nv / bc2) + ADAM_EPS) + ADAM_WD * w), nm, nv


def adamw(w, g, m, v, name):
    shape = w.shape
    w2, g2, m2, v2 = _as_2d(w), _as_2d(g), _as_2d(m), _as_2d(v)
    R, C = w2.shape
    tr = _row_tile(R, C)

    def body(w_ref, g_ref, m_ref, v_ref, d_ref, nm_ref, nv_ref):
        d_ref[...], nm_ref[...], nv_ref[...] = _adamw_update(w_ref[...], g_ref[...], m_ref[...], v_ref[...])

    spec = lambda: pl.BlockSpec((tr, C), lambda i: (i, 0))
    d, nm, nv = pl.pallas_call(
        body, name=name, grid=(R // tr,), in_specs=[spec() for _ in range(4)], out_specs=[spec() for _ in range(3)],
        out_shape=[_S((R, C))] * 3, compiler_params=_cp("parallel"),
    )(w2, g2, m2, v2)
    return d.reshape(shape), nm.reshape(shape), nv.reshape(shape)


def adamw_reduce(w, a, theirs, m, v, chip, name):
    shape = w.shape
    w2, m2, v2 = _as_2d(w), _as_2d(m), _as_2d(v)
    R, C = w2.shape
    a3, t3 = a.reshape(4, R, C), theirs.reshape(3, R, C)
    tr = _row_tile(R, C)

    def body(chip_ref, w_ref, a_ref, t_ref, m_ref, v_ref, g_ref, d_ref, nm_ref, nv_ref):
        gv = ((a_ref[0] + t_ref[0].astype(f32)) + t_ref[1].astype(f32)) + t_ref[2].astype(f32)
        g_ref[...] = gv
        d_ref[...], nm_ref[...], nv_ref[...] = _adamw_update(w_ref[...], gv, m_ref[...], v_ref[...])

    spec = lambda: pl.BlockSpec((tr, C), lambda i, c: (i, 0))
    grid_spec = pltpu.PrefetchScalarGridSpec(
        num_scalar_prefetch=1, grid=(R // tr,),
        in_specs=[spec(), pl.BlockSpec((1, tr, C), lambda i, c: (c[0], i, 0)), pl.BlockSpec((3, tr, C), lambda i, c: (0, i, 0)),
                  spec(), spec()],
        out_specs=[spec() for _ in range(4)])
    g, d, nm, nv = pl.pallas_call(
        body, name=name, grid_spec=grid_spec, out_shape=[_S((R, C))] * 4, compiler_params=_cp("parallel"),
    )(chip, w2, a3, t3, m2, v2)
    return g.reshape(shape), d.reshape(shape), nm.reshape(shape), nv.reshape(shape)


MESH = pl.DeviceIdType.MESH
ANY = pl.BlockSpec(memory_space=pl.ANY)


def _place():
    return lax.axis_index("x"), lax.axis_index("y"), lax.axis_index("c")


def _other_chips(mx, my):
    return [(1 - mx, my), (mx, 1 - my), (1 - mx, 1 - my)]


def all_gather(shards, name):
    nb = len(shards)

    def body(*refs):
        phases = _gather_phases([s.shape for s in shards], refs[:nb], refs[nb:2 * nb], *refs[2 * nb:])
        for phase in phases:
            phase()

    outs = pl.pallas_call(
        body, name=name, out_shape=_gather_out_shapes(shards), in_specs=[ANY] * nb, out_specs=[ANY] * nb,
        scratch_shapes=_gather_semaphores(nb),
    )(*shards)
    return list(outs)


GATHER_COPIES = 9
OWN_SIB, OWN_X, OWN_Y, X_SIB, Y_SIB, RELAY_X, RELAY_Y, DIAG0_SIB, DIAG1_SIB = range(GATHER_COPIES)


def _gather_out_shapes(shards):
    return [_S((N_DEV,) + s.shape, s.dtype) for s in shards]


def _gather_semaphores(nb):
    return [pltpu.SemaphoreType.DMA((nb, GATHER_COPIES)), pltpu.SemaphoreType.DMA((nb, GATHER_COPIES)),
            pltpu.SemaphoreType.DMA((nb,))]


def _gather_phases(shapes, x_refs, out_refs, send_sems, recv_sems, local_sems):
    nb = len(shapes)
    mx, my, mc = _place()
    sibling, xn, yn = (mx, my, 1 - mc), (1 - mx, my, mc), (mx, 1 - my, mc)

    def block(b, px, py, pc, half=None):
        ref = out_refs[b].at[4 * px + 2 * py + pc]
        if half is None:
            return ref
        n = shapes[b][0] // 2
        return ref.at[pl.ds(half * n, n)]

    def copy(b, k, dst, to, src=None):
        return pltpu.make_async_remote_copy(
            src_ref=dst if src is None else src, dst_ref=dst, send_sem=send_sems.at[b, k],
            recv_sem=recv_sems.at[b, k], device_id=to, device_id_type=MESH)

    def send(b, k):
        if k in (OWN_X, OWN_Y, OWN_SIB):
            return copy(b, k, block(b, mx, my, mc), {OWN_X: xn, OWN_Y: yn, OWN_SIB: sibling}[k], src=x_refs[b])
        what, to = {RELAY_X: ((1 - mx, my, mc, 0), yn), RELAY_Y: ((mx, 1 - my, mc, 1), xn),
                    X_SIB: ((1 - mx, my, mc), sibling), Y_SIB: ((mx, 1 - my, mc), sibling),
                    DIAG0_SIB: ((1 - mx, 1 - my, mc, 0), sibling), DIAG1_SIB: ((1 - mx, 1 - my, mc, 1), sibling)}[k]
        return copy(b, k, block(b, *what), to)

    def local(b):
        return pltpu.make_async_copy(x_refs[b], block(b, mx, my, mc), local_sems.at[b])

    def send_own():
        for k in (OWN_X, OWN_Y, OWN_SIB):
            for b in range(nb):
                send(b, k).start()
        for b in range(nb):
            local(b).start()

    def relay_neighbours():
        for b in range(nb):
            copy(b, OWN_X, block(b, 1 - mx, my, mc), xn).wait_recv()
            send(b, RELAY_X).start()
            send(b, X_SIB).start()
        for b in range(nb):
            copy(b, OWN_Y, block(b, mx, 1 - my, mc), yn).wait_recv()
            send(b, RELAY_Y).start()
            send(b, Y_SIB).start()

    def hand_on_diagonal():
        for b in range(nb):
            copy(b, RELAY_X, block(b, 1 - mx, 1 - my, mc, 0), yn).wait_recv()
            send(b, DIAG0_SIB).start()
            copy(b, RELAY_Y, block(b, 1 - mx, 1 - my, mc, 1), xn).wait_recv()
            send(b, DIAG1_SIB).start()

    def finish():
        for b in range(nb):
            copy(b, OWN_SIB, block(b, mx, my, 1 - mc), sibling).wait_recv()
            copy(b, X_SIB, block(b, 1 - mx, my, 1 - mc), sibling).wait_recv()
            copy(b, Y_SIB, block(b, mx, 1 - my, 1 - mc), sibling).wait_recv()
            copy(b, DIAG0_SIB, block(b, 1 - mx, 1 - my, 1 - mc, 0), sibling).wait_recv()
            copy(b, DIAG1_SIB, block(b, 1 - mx, 1 - my, 1 - mc, 1), sibling).wait_recv()
        for b in range(nb):
            for k in range(GATHER_COPIES):
                send(b, k).wait_send()
            local(b).wait()

    return send_own, relay_neighbours, hand_on_diagonal, finish


def exchange_sibling(gs, name):
    nb = len(gs)

    def body(*refs):
        g_refs, recv_refs, send_sems, recv_sems = refs[:nb], refs[nb:2 * nb], refs[2 * nb], refs[2 * nb + 1]
        mx, my, mc = _place()
        copies = [pltpu.make_async_remote_copy(
            src_ref=g_refs[b].at[2 * k + 1 - mc], dst_ref=recv_refs[b].at[k], send_sem=send_sems.at[b, k],
            recv_sem=recv_sems.at[b, k], device_id=(mx, my, 1 - mc), device_id_type=MESH)
            for b in range(nb) for k in range(4)]
        for cp in copies:
            cp.start()
        for cp in copies:
            cp.wait()

    outs = pl.pallas_call(
        body, name=name, out_shape=[_S((4,) + g.shape[1:], g.dtype) for g in gs], in_specs=[ANY] * nb,
        out_specs=[ANY] * nb,
        scratch_shapes=[pltpu.SemaphoreType.DMA((nb, 4)), pltpu.SemaphoreType.DMA((nb, 4))],
    )(*gs)
    return list(outs)


def exchange_chips(parts, name):
    nb = len(parts)

    def body(*refs):
        a_refs, recv_refs, send_sems, recv_sems = refs[:nb], refs[nb:2 * nb], refs[2 * nb], refs[2 * nb + 1]
        mx, my, mc = _place()
        copies = [pltpu.make_async_remote_copy(
            src_ref=a_refs[b].at[2 * px + py], dst_ref=recv_refs[b].at[j], send_sem=send_sems.at[b, j],
            recv_sem=recv_sems.at[b, j], device_id=(px, py, mc), device_id_type=MESH)
            for b in range(nb) for j, (px, py) in enumerate(_other_chips(mx, my))]
        for cp in copies:
            cp.start()
        for cp in copies:
            cp.wait()

    outs = pl.pallas_call(
        body, name=name, out_shape=[_S((3,) + a.shape[1:], a.dtype) for a in parts], in_specs=[ANY] * nb,
        out_specs=[ANY] * nb,
        scratch_shapes=[pltpu.SemaphoreType.DMA((nb, 3)), pltpu.SemaphoreType.DMA((nb, 3))],
    )(*parts)
    return list(outs)


def add_sibling(g, theirs, core, name, wire_dtype=f32):
    shp = theirs.shape
    C = shp[-1]
    R = math.prod(shp[1:-1])
    tr = _row_tile(R, C)
    narrow = wire_dtype != f32

    def body(core_ref, g_ref, t_ref, o_ref, *wire_ref):
        s = g_ref[...] + t_ref[...]
        o_ref[...] = s
        if narrow:
            wire_ref[0][...] = s.astype(wire_dtype)

    blk = lambda: pl.BlockSpec((1, tr, C), lambda k, i, c: (k, i, 0))
    grid_spec = pltpu.PrefetchScalarGridSpec(
        num_scalar_prefetch=1, grid=(4, R // tr),
        in_specs=[pl.BlockSpec((1, tr, C), lambda k, i, c: (2 * k + c[0], i, 0)), blk()],
        out_specs=[blk(), blk()] if narrow else [blk()])
    outs = pl.pallas_call(
        body, name=name, grid_spec=grid_spec,
        out_shape=[_S((4, R, C), f32)] + ([_S((4, R, C), wire_dtype)] if narrow else []),
        compiler_params=_cp("parallel", "parallel"),
    )(core, g.reshape(N_DEV, R, C), theirs.reshape(4, R, C))
    part = outs[0].reshape(shp)
    return part, (outs[1].reshape(shp) if narrow else part)


def add_chips(a, theirs, chip, name):
    _, R, C = a.shape
    tr = _row_tile(R, C)

    def body(chip_ref, a_ref, t_ref, out_ref):
        out_ref[...] = ((a_ref[0] + t_ref[0]) + t_ref[1]) + t_ref[2]

    grid_spec = pltpu.PrefetchScalarGridSpec(
        num_scalar_prefetch=1, grid=(R // tr,),
        in_specs=[pl.BlockSpec((1, tr, C), lambda i, c: (c[0], i, 0)), pl.BlockSpec((3, tr, C), lambda i, c: (0, i, 0))],
        out_specs=pl.BlockSpec((tr, C), lambda i, c: (i, 0)))
    return pl.pallas_call(
        body, name=name, grid_spec=grid_spec, out_shape=_S((R, C), a.dtype), compiler_params=_cp("parallel"),
    )(chip, a, theirs)


def _round_up(n, q):
    return (n + q - 1) // q * q


def _lane_rows(a):
    flat = a.reshape(-1)
    n = _round_up(flat.shape[0], SUBLANES * LANES)
    return jnp.pad(flat, (0, n - flat.shape[0])).reshape(-1, LANES)


def _full_to_shards(full, axis):
    shp = full.shape
    s = shp[axis] // N_DEV
    cut = full.reshape(shp[:axis] + (N_DEV, s) + shp[axis + 1:])
    return jnp.moveaxis(cut, axis, 0)


def _shards_to_full(parts, axis):
    shp = list(parts.shape[1:])
    shp[axis] *= N_DEV
    return jnp.moveaxis(parts, 0, axis).reshape(tuple(shp))


def kernel(x, p, g_pre, w_in, s5_a_re, s5_a_im, s5_log_dt, s5_b_re, s5_b_im, s5_c_re, s5_c_im, s5_d, w_glu, w_bs, conv_w, conv_b, lru_w_a, lru_b_a, lru_w_x, lru_b_x, lru_lambda, w_bl, w_out, g_post, w_ple, w_ple_gate, loss_target, m_g_pre, m_w_in, m_s5_a_re, m_s5_a_im, m_s5_log_dt, m_s5_b_re, m_s5_b_im, m_s5_c_re, m_s5_c_im, m_s5_d, m_w_glu, m_w_bs, m_conv_w, m_conv_b, m_lru_w_a, m_lru_b_a, m_lru_w_x, m_lru_b_x, m_lru_lambda, m_w_bl, m_w_out, m_g_post, m_w_ple, m_w_ple_gate, v_g_pre, v_w_in, v_s5_a_re, v_s5_a_im, v_s5_log_dt, v_s5_b_re, v_s5_b_im, v_s5_c_re, v_s5_c_im, v_s5_d, v_w_glu, v_w_bs, v_conv_w, v_conv_b, v_lru_w_a, v_lru_b_a, v_lru_w_x, v_lru_b_x, v_lru_lambda, v_w_bl, v_w_out, v_g_post, v_w_ple, v_w_ple_gate):
    given = dict(locals())
    W = {n: given[n] for n in WEIGHTS}
    M = {n: given["m_" + n] for n in WEIGHTS}
    V = {n: given["v_" + n] for n in WEIGHTS}
    xs, target = to_scan_order(x[0]), to_scan_order(loss_target[0])
    ps = [to_scan_order(p[i, 0]) for i in range(DEPTH)]

    mx, my, mc = _place()
    core = jnp.reshape(mc, (1,)).astype(jnp.int32)
    chip = jnp.reshape(2 * mx + my, (1,)).astype(jnp.int32)

    names = list(SHARDED)
    conv_rows = PAIR - CONV_WIDTH

    def layer_shards(i):
        return [W[n][i].astype(bf16) if n in GATHER_BF16 else jnp.pad(W[n][i], ((0, conv_rows), (0, 0))) for n in names]

    def layer_weights(i, gathered):
        full = {n: _shards_to_full(g if n in GATHER_BF16 else g[:, :CONV_WIDTH], SHARDED[n] - 1)
                for n, g in zip(names, gathered)}
        return {n: (full[n] if n in SHARDED else W[n][i]) for n in WEIGHTS}

    act, saved, weights = xs, [], []
    gathered = all_gather(layer_shards(0), "comm_gather_weights")
    for i in range(DEPTH):
        weights.append(layer_weights(i, gathered))
        act, sv, gathered = layer_fwd(act, ps[i], weights[i], "_l%d" % i, layer_shards(i + 1) if i + 1 < DEPTH else ())
        saved.append(sv)
    loss_part, gact = loss_head(act, target, "loss_head")
    grads = [None] * DEPTH
    for i in reversed(range(DEPTH)):
        gact, grads[i] = layer_bwd(gact, weights[i], saved[i], "_l%d" % i)
    loss = lax.psum(loss_part[0, 0], ("x", "y", "c"))
    gfull = {n: jnp.stack([grads[i][n].reshape(weights[i][n].shape) for i in range(DEPTH)]) for n in WEIGHTS}

    rep_rows = [_lane_rows(gfull[n]) for n in REPLICATED]
    n_rows = sum(r.shape[0] for r in rep_rows)
    pad_rows = _round_up(n_rows, N_DEV * SUBLANES) - n_rows
    rep_blocks = jnp.concatenate(rep_rows + [jnp.zeros((pad_rows, LANES), f32)]).reshape(N_DEV, -1, LANES)
    blocks = [_full_to_shards(gfull[n], SHARDED[n]) for n in names] + [rep_blocks]
    theirs = exchange_sibling(blocks, "comm_reduce_sibling")
    parts, wire = [], []
    for k, (b, t) in enumerate(zip(blocks, theirs)):
        part, sent = add_sibling(b, t, core, "reduce_add_sibling_%d" % k, bf16 if k < len(names) else f32)
        parts.append(part)
        wire.append(sent)
    others = exchange_chips(wire, "comm_reduce_chips")

    red, deltas, new_m, new_v = {}, {}, {}, {}
    for n, a, t in zip(names, parts, others):
        red[n], deltas[n], new_m[n], new_v[n] = adamw_reduce(W[n], a, t, M[n], V[n], chip, "adamw_" + n)
    piece = add_chips(parts[-1], others[-1], chip, "reduce_add_chips")
    rep_all = all_gather([piece], "comm_gather_replicated")[0].reshape(-1, LANES)
    off = 0
    for n, rows in zip(REPLICATED, rep_rows):
        k = math.prod(W[n].shape)
        red[n] = rep_all[off:off + rows.shape[0]].reshape(-1)[:k].reshape(W[n].shape)
        off += rows.shape[0]
        deltas[n], new_m[n], new_v[n] = adamw(W[n], red[n], M[n], V[n], "adamw_" + n)
    return (loss, from_scan_order(gact)[None], *[red[n] for n in WEIGHTS], *[deltas[n] for n in WEIGHTS],
            *[new_m[n] for n in WEIGHTS], *[new_v[n] for n in WEIGHTS])
```

```python
import math

import jax
import jax.numpy as jnp
from jax import lax
from jax.experimental import pallas as pl
from jax.experimental.pallas import tpu as pltpu

f32 = jnp.float32
bf16 = jnp.bfloat16

D_MODEL = 1024
DEPTH = 2
PLE_DIM = 256
NORM_EPS = 1e-6
S5_WIDTH = 512
S5_GROUP = 16
S5_GROUPS = 32
S5_STATE = 64
S5_NS = S5_GROUPS * S5_STATE
LRU_WIDTH = 1280
LRU_HEADS = 10
LRU_HEAD_DIM = 128
LRU_C = 8.0
CONV_WIDTH = 4
N_DEV = 8

ADAM_LR = 0.001
ADAM_B1 = 0.9
ADAM_B2 = 0.999
ADAM_EPS = 1e-08
ADAM_WD = 0.01
ADAM_STEP = 10

LANES = 128
SUBLANES = 8
SEGS = SUBLANES
SCAN_CHUNK = 256
TOKEN_TILE = 256
MM_TILE_M = 1024
MM_TILE_N = 1408
MM_TILE_K_ROWS = 1280
PAIR = 2 * SUBLANES
VMEM_LIMIT_BYTES = 56 * 1024 * 1024
ELEMENTWISE_BLOCK_BYTES = 1024 * 1024

WEIGHTS = ['g_pre', 'w_in', 's5_a_re', 's5_a_im', 's5_log_dt', 's5_b_re', 's5_b_im', 's5_c_re', 's5_c_im',
           's5_d', 'w_glu', 'w_bs', 'conv_w', 'conv_b', 'lru_w_a', 'lru_b_a', 'lru_w_x', 'lru_b_x',
           'lru_lambda', 'w_bl', 'w_out', 'g_post', 'w_ple', 'w_ple_gate']
SHARDED = {'w_in': 2, 'w_glu': 2, 'w_bs': 2, 'conv_w': 2, 'w_bl': 1, 'w_out': 1, 'w_ple': 2, 'w_ple_gate': 1}
GATHER_BF16 = ['w_in', 'w_glu', 'w_bs', 'w_bl', 'w_out', 'w_ple', 'w_ple_gate']
REPLICATED = [n for n in WEIGHTS if n not in SHARDED]


def _sig(x):
    return 0.5 * jnp.tanh(0.5 * x) + 0.5


def _gelu_parts(x):
    k = math.sqrt(2.0 / math.pi)
    t = jnp.tanh(k * (x + 0.044715 * x * x * x))
    return t, k


def _gelu(x):
    t, _ = _gelu_parts(x)
    return 0.5 * x * (1.0 + t)


def _gelu_grad(x):
    t, k = _gelu_parts(x)
    return 0.5 * (1.0 + t) + 0.5 * x * (1.0 - t * t) * k * (1.0 + 3.0 * 0.044715 * x * x)


def _one_minus_sq(a, log_a):
    z = 2.0 * log_a
    series = -z * (1.0 + z * (0.5 + z * (1.0 / 6.0 + z * (1.0 / 24.0 + z * (1.0 / 120.0)))))
    return jnp.where(z > -0.05, series, 1.0 - a * a)


def _softplus_neg(lam):
    return jnp.maximum(-lam, 0.0) + jnp.log(1.0 + jnp.exp(-jnp.abs(lam)))


def _dot(a, b):
    return jnp.dot(a, b, preferred_element_type=f32)


def _dot_nt(a, b):
    return lax.dot_general(a, b, (((1,), (1,)), ((), ())), preferred_element_type=f32)


def _dot_tn(a, b):
    return lax.dot_general(a, b, (((0,), (0,)), ((), ())), preferred_element_type=f32)


def _S(shape, dtype=f32):
    return jax.ShapeDtypeStruct(shape, dtype)


def _full(shape):
    nd = len(shape)
    return pl.BlockSpec(shape, lambda *_: (0,) * nd)


def _rows(tile, width, col=0):
    return pl.BlockSpec((tile, width), lambda i: (i, col))


def _cp(*semantics):
    return pltpu.CompilerParams(dimension_semantics=semantics or None, vmem_limit_bytes=VMEM_LIMIT_BYTES)


def _tile(n, want):
    t = min(n, want)
    assert n % t == 0, (n, want)
    return t


def _row_tile(R, C=LANES):
    cap = max(SUBLANES, min(R, ELEMENTWISE_BLOCK_BYTES // (4 * C)))
    for t in range(cap - cap % SUBLANES, 0, -SUBLANES):
        if R % t == 0:
            return t
    return R


def _lanes(j):
    return slice(LANES * j, LANES * (j + 1))


def _step_rows(k, n=SUBLANES):
    return pl.ds(pl.multiple_of(k * n, n), n)


def to_scan_order(a):
    T, C = a.shape
    tc = _tile(T, SCAN_CHUNK)
    return a.reshape(T // tc, SEGS, tc // SEGS, C).transpose(0, 2, 1, 3).reshape(T, C)


def from_scan_order(a):
    T, C = a.shape
    tc = _tile(T, SCAN_CHUNK)
    return a.reshape(T // tc, tc // SEGS, SEGS, C).transpose(0, 2, 1, 3).reshape(T, C)


def _col_tile(n, cap):
    if n <= cap:
        return n
    for t in range(cap - cap % LANES, 0, -LANES):
        if n % t == 0:
            return t
    return n


def _resident(shape):
    nd = len(shape)
    return pl.BlockSpec(shape, lambda *_: (0,) * nd, pipeline_mode=pl.Buffered(1))


def mm_nn(a, b, name, out_dtype=f32):
    M, K = a.shape
    N = b.shape[1]
    tm, tn = _tile(M, MM_TILE_M), _col_tile(N, MM_TILE_N)

    def body(a_ref, b_ref, o_ref):
        o_ref[...] = _dot(a_ref[...].astype(bf16), b_ref[...].astype(bf16)).astype(out_dtype)

    return pl.pallas_call(
        body, name=name, grid=(M // tm, N // tn),
        in_specs=[pl.BlockSpec((tm, K), lambda i, j: (i, 0)), pl.BlockSpec((K, tn), lambda i, j: (0, j))],
        out_specs=pl.BlockSpec((tm, tn), lambda i, j: (i, j)),
        out_shape=_S((M, N), out_dtype), compiler_params=_cp("parallel", "parallel"),
    )(a, b)


def mm_tn(a, b, name):
    M, K = a.shape
    N = b.shape[1]
    tm, tk, tn = _tile(M, MM_TILE_M), _col_tile(K, MM_TILE_K_ROWS), _col_tile(N, MM_TILE_N)

    def body(a_ref, b_ref, o_ref):
        m = pl.program_id(2)
        part = _dot_tn(a_ref[...].astype(bf16), b_ref[...].astype(bf16))

        @pl.when(m == 0)
        def _():
            o_ref[...] = part

        @pl.when(m > 0)
        def _():
            o_ref[...] += part

    return pl.pallas_call(
        body, name=name, grid=(K // tk, N // tn, M // tm),
        in_specs=[pl.BlockSpec((tm, tk), lambda i, j, m: (m, i)), pl.BlockSpec((tm, tn), lambda i, j, m: (m, j))],
        out_specs=pl.BlockSpec((tk, tn), lambda i, j, m: (i, j)),
        out_shape=_S((K, N), f32),
        compiler_params=_cp("parallel", "parallel", "arbitrary"),
    )(a, b)


def rms_fwd(x, g, name):
    T = x.shape[0]
    tm = _tile(T, TOKEN_TILE)

    def body(x_ref, g_ref, h_ref):
        xv = x_ref[...]
        r = lax.rsqrt(jnp.mean(xv * xv, axis=-1, keepdims=True) + NORM_EPS)
        h_ref[...] = (xv * r * g_ref[...]).astype(bf16)

    return pl.pallas_call(
        body, name=name, grid=(T // tm,),
        in_specs=[_rows(tm, D_MODEL), _full((1, D_MODEL))], out_specs=_rows(tm, D_MODEL),
        out_shape=_S((T, D_MODEL), bf16), compiler_params=_cp("parallel"),
    )(x, g)


def _s5_discretise(a_re, a_im, log_dt, b_re_t, b_im_t):
    dt = jnp.exp(log_dt)
    mag = jnp.exp(a_re * dt)
    ab_re = mag * jnp.cos(a_im * dt)
    ab_im = mag * jnp.sin(a_im * dt)
    den = a_re * a_re + a_im * a_im
    nr, ni = ab_re - 1.0, ab_im
    z_re = (nr * a_re + ni * a_im) / den
    z_im = (ni * a_re - nr * a_im) / den
    bb_re = z_re[None] * b_re_t - z_im[None] * b_im_t
    bb_im = z_re[None] * b_im_t + z_im[None] * b_re_t
    return ab_re, ab_im, bb_re, bb_im


def s5_prep(a_re, a_im, log_dt, b_re_t, b_im_t, m, name):
    G, N = a_re.shape

    def body(are_ref, aim_ref, ldt_ref, bre_ref, bim_ref, ab_ref, pw_ref, bb_ref):
        are, aim, ldt = are_ref[...], aim_ref[...], ldt_ref[...]
        ab_re, ab_im, bb_re, bb_im = _s5_discretise(are, aim, ldt, bre_ref[...], bim_ref[...])
        ab_ref[0], ab_ref[1] = ab_re, ab_im
        bb_ref[0], bb_ref[1] = bb_re, bb_im
        dt = jnp.exp(ldt)
        for k in range(m):
            mag = jnp.exp(are * dt * (k + 1.0))
            pw_ref[0, k] = mag * jnp.cos(aim * dt * (k + 1.0))
            pw_ref[1, k] = mag * jnp.sin(aim * dt * (k + 1.0))

    return pl.pallas_call(
        body, name=name,
        out_shape=[_S((2, G, N)), _S((2, m, G, N)), _S((2, S5_GROUP, G, N))], compiler_params=_cp(),
    )(a_re, a_im, log_dt, b_re_t, b_im_t)


def s5_prep_bwd(a_re, a_im, log_dt, b_re_t, b_im_t, g_ab, g_bb, name):
    G, N = a_re.shape

    def body(are_ref, aim_ref, ldt_ref, bre_ref, bim_ref, gab_ref, gbb_ref, o_are, o_aim, o_ldt, o_bre, o_bim):
        _, vjp = jax.vjp(_s5_discretise, are_ref[...], aim_ref[...], ldt_ref[...], bre_ref[...], bim_ref[...])
        g_are, g_aim, g_ldt, g_bre, g_bim = vjp((gab_ref[0], gab_ref[1], gbb_ref[0], gbb_ref[1]))
        o_are[...], o_aim[...], o_ldt[...], o_bre[...], o_bim[...] = g_are, g_aim, g_ldt, g_bre, g_bim

    return pl.pallas_call(
        body, name=name,
        out_shape=[_S((G, N)), _S((G, N)), _S((G, 1)), _S((S5_GROUP, G, N)), _S((S5_GROUP, G, N))],
        compiler_params=_cp(),
    )(a_re, a_im, log_dt, b_re_t, b_im_t, g_ab, g_bb)


NB_S5 = S5_NS // LANES
CB_S5 = S5_WIDTH // LANES
SB_PER_CB = NB_S5 // CB_S5
GRP_PER_SB = LANES // S5_STATE
S5_JB = 8


def _bdb_mask():
    j = jnp.arange(NB_S5)
    own_rows = (j[:, None] % SB_PER_CB == jnp.arange(SB_PER_CB)[None, :]).astype(f32)
    eye = jnp.eye(GRP_PER_SB, dtype=f32)
    return own_rows[:, :, None, None, None, None, None] * eye[None, None, :, None, None, :, None]


def _pack_bdb(bb):
    v = jnp.transpose(bb.reshape(2, S5_GROUP, NB_S5, GRP_PER_SB, S5_STATE), (2, 3, 1, 0, 4))
    full = v[:, None, :, :, :, None, :] * _bdb_mask()
    return full.reshape(NB_S5, LANES, 2 * LANES)


def _unpack_bdb(g_bdb):
    g7 = g_bdb.reshape(NB_S5, SB_PER_CB, GRP_PER_SB, S5_GROUP, 2, GRP_PER_SB, S5_STATE)
    v = jnp.sum(g7 * _bdb_mask(), axis=(1, 5))
    return jnp.transpose(v, (3, 2, 0, 1, 4)).reshape(2, S5_GROUP, S5_GROUPS, S5_STATE)


def _pack_cdb(c_re, c_im):
    gl = S5_GROUPS // CB_S5
    c2 = jnp.stack([c_re, -c_im]).reshape(2, CB_S5, gl, S5_GROUP, S5_STATE)
    eye = jnp.eye(gl, dtype=f32)
    full = jnp.transpose(c2, (1, 0, 2, 4, 3))[:, :, :, :, None, :] * eye[None, None, :, None, :, None]
    return full.reshape(CB_S5, 2 * SB_PER_CB * LANES, LANES)


def _unpack_cdb(g_cdb):
    gl = S5_GROUPS // CB_S5
    g6 = g_cdb.reshape(CB_S5, 2, gl, S5_STATE, gl, S5_GROUP)
    eye = jnp.eye(gl, dtype=f32)
    v = jnp.sum(g6 * eye[None, None, :, None, :, None], axis=4)
    v = jnp.transpose(v, (1, 0, 2, 4, 3)).reshape(2, S5_GROUPS, S5_GROUP, S5_STATE)
    return v[0], -v[1]


def _state_cat(ref, c):
    w = SB_PER_CB * LANES
    return jnp.concatenate([ref[:, w * c:w * (c + 1)], ref[:, S5_NS + w * c:S5_NS + w * (c + 1)]], axis=1)


def _state_pair(ref, j):
    return jnp.concatenate([ref[:, _lanes(j)], ref[:, S5_NS + LANES * j:S5_NS + LANES * (j + 1)]], axis=1)


def s5_fwd(usg, bdb, cdb, dvec, abar_b, ptab_b, name, gather=()):
    T = usg.shape[0]
    tc = _tile(T, SCAN_CHUNK)
    m = tc // SEGS
    nsteps = T // tc
    ng = len(gather)
    assert ptab_b.shape == (2, m, SEGS, S5_NS) and m % 2 == 0

    def body(*refs):
        u_ref, bdb_ref, cdb_ref, d_ref, a_ref, p_ref = refs[:6]
        ys_ref, sre_ref, sim_ref, sbf_ref = refs[6 + ng:10 + ng]
        src_re, src_im, dst_re, dst_im, cin_ref, carry_ref = refs[10 + 2 * ng:16 + 2 * ng]
        i = pl.program_id(0)
        if ng:
            phases = _gather_phases([s.shape for s in gather], refs[6:6 + ng], refs[10 + ng:10 + 2 * ng],
                                    *refs[16 + 2 * ng:])
            for phase, step in zip(phases, (0, nsteps // 2, (3 * nsteps) // 4, nsteps - 1)):
                pl.when(i == step)(phase)

        @pl.when(i == 0)
        def _():
            carry_ref[...] = jnp.zeros_like(carry_ref)

        u = u_ref[...]
        ub = u.astype(bf16)
        for j in range(NB_S5):
            bu = _dot(ub[:, _lanes(j // SB_PER_CB)], bdb_ref[j])
            src_re[:, _lanes(j)] = bu[:, :LANES]
            src_im[:, _lanes(j)] = bu[:, LANES:]
        for j0 in range(0, NB_S5, S5_JB):
            def kstep(k, st):
                rows = _step_rows(k)
                out = []
                for q in range(S5_JB):
                    ln = _lanes(j0 + q)
                    sr, si = st[2 * q], st[2 * q + 1]
                    ar, ai = a_ref[0, :, ln], a_ref[1, :, ln]
                    nr = ar * sr - ai * si + src_re[rows, ln]
                    ni = ar * si + ai * sr + src_im[rows, ln]
                    dst_re[rows, ln] = nr
                    dst_im[rows, ln] = ni
                    out += [nr, ni]
                return tuple(out)

            ends = lax.fori_loop(0, m, kstep, tuple(jnp.zeros((SEGS, LANES), f32) for _ in range(2 * S5_JB)))
            for q in range(S5_JB):
                ln = _lanes(j0 + q)
                er, ei = ends[2 * q], ends[2 * q + 1]
                cr, ci = carry_ref[0, :, ln], carry_ref[1, :, ln]
                amr, ami = p_ref[0, m - 1, 0:1, ln], p_ref[1, m - 1, 0:1, ln]
                rows_r, rows_i = [], []
                for s in range(SEGS):
                    rows_r.append(cr)
                    rows_i.append(ci)
                    cr, ci = (er[s:s + 1, :] + amr * cr - ami * ci, ei[s:s + 1, :] + amr * ci + ami * cr)
                cin_ref[0, 0:SEGS, ln] = _stack_rows(rows_r)
                cin_ref[1, 0:SEGS, ln] = _stack_rows(rows_i)
                carry_ref[0, :, ln] = cr
                carry_ref[1, :, ln] = ci
        cin_ref[:, SEGS:, :] = cin_ref[:, 0:SEGS, :]

        def fix(k2, _):
            rows = _step_rows(k2, PAIR)
            pr = p_ref[0, pl.ds(2 * k2, 2)].reshape(PAIR, S5_NS)
            pi = p_ref[1, pl.ds(2 * k2, 2)].reshape(PAIR, S5_NS)
            cr, ci = cin_ref[0], cin_ref[1]
            sr = dst_re[rows, :] + pr * cr - pi * ci
            si = dst_im[rows, :] + pr * ci + pi * cr
            sre_ref[rows, :] = sr
            sim_ref[rows, :] = si
            sbf_ref[rows, 0:S5_NS] = sr.astype(bf16)
            sbf_ref[rows, S5_NS:] = si.astype(bf16)
            return 0

        lax.fori_loop(0, m // 2, fix, 0)
        for c in range(CB_S5):
            ys_ref[:, _lanes(c)] = _dot(_state_cat(sbf_ref, c), cdb_ref[c]) + d_ref[:, _lanes(c)] * u[:, _lanes(c)]

    st = lambda w: _rows(tc, w)
    outs = pl.pallas_call(
        body, name=name, grid=(nsteps,),
        in_specs=[_rows(tc, S5_WIDTH, 0), _resident(bdb.shape), _resident(cdb.shape), _full((1, S5_WIDTH)),
                  _resident((2, SEGS, S5_NS)), _resident((2, m, SEGS, S5_NS))] + [ANY] * ng,
        out_specs=[st(S5_WIDTH), st(S5_NS), st(S5_NS), st(2 * S5_NS)] + [ANY] * ng,
        out_shape=[_S((T, S5_WIDTH)), _S((T, S5_NS)), _S((T, S5_NS)), _S((T, 2 * S5_NS), bf16)] + (
            _gather_out_shapes(gather) if ng else []),
        scratch_shapes=[pltpu.VMEM((tc, S5_NS), f32)] * 4 + [pltpu.VMEM((2, PAIR, S5_NS), f32),
                                                             pltpu.VMEM((2, 1, S5_NS), f32)] + (
            _gather_semaphores(ng) if ng else []),
        compiler_params=_cp("arbitrary"),
    )(usg, bdb, cdb, dvec, abar_b, ptab_b, *gather)
    return outs[:4], list(outs[4:])


def s5_bwd(gys, usg, s_re, s_im, s_bf, bdb, cdb, dvec, abar_b, ptab_rev_b, name):
    T = gys.shape[0]
    tc = _tile(T, SCAN_CHUNK)
    m = tc // SEGS
    nch = T // tc
    hb = tc // SUBLANES

    def body(gy_ref, u_ref, sre_ref, sim_ref, hre_ref, him_ref, sbf_ref, bdb_ref, cdb_ref, d_ref, a_ref, p_ref,
             gu_ref, gab_ref, gd_ref, gbdb_ref, gcdb_ref,
             src_re, src_im, dst_re, dst_im, lam_ref, cin_ref, acc_ref, carry_ref):
        i = pl.program_id(0)

        @pl.when(i == 0)
        def _():
            carry_ref[...] = jnp.zeros_like(carry_ref)
            for ref in (gab_ref, gd_ref, gbdb_ref, gcdb_ref):
                ref[...] = jnp.zeros_like(ref)

        first = i == nch - 1
        gy = gy_ref[...]
        gyb = gy.astype(bf16)
        u = u_ref[...]
        ub = u.astype(bf16)
        w = SB_PER_CB * LANES
        for c in range(CB_S5):
            gs = _dot_nt(gyb[:, _lanes(c)], cdb_ref[c])
            src_re[:, w * c:w * (c + 1)] = gs[:, :w]
            src_im[:, w * c:w * (c + 1)] = gs[:, w:]
            gcdb_ref[c] += _dot_tn(_state_cat(sbf_ref, c), gyb[:, _lanes(c)])
        for j0 in range(0, NB_S5, S5_JB):
            def kstep(kk, st):
                rows = _step_rows(m - 1 - kk)
                out = []
                for q in range(S5_JB):
                    ln = _lanes(j0 + q)
                    lr, li = st[2 * q], st[2 * q + 1]
                    ar, ai = a_ref[0, :, ln], a_ref[1, :, ln]
                    nr = ar * lr + ai * li + src_re[rows, ln]
                    ni = ar * li - ai * lr + src_im[rows, ln]
                    dst_re[rows, ln] = nr
                    dst_im[rows, ln] = ni
                    out += [nr, ni]
                return tuple(out)

            ends = lax.fori_loop(0, m, kstep, tuple(jnp.zeros((SEGS, LANES), f32) for _ in range(2 * S5_JB)))
            for q in range(S5_JB):
                ln = _lanes(j0 + q)
                er, ei = ends[2 * q], ends[2 * q + 1]
                cr, ci = carry_ref[0, :, ln], carry_ref[1, :, ln]
                amr, ami = p_ref[0, 0, 0:1, ln], p_ref[1, 0, 0:1, ln]
                rows_r, rows_i = [None] * SEGS, [None] * SEGS
                for s in reversed(range(SEGS)):
                    rows_r[s], rows_i[s] = cr, ci
                    cr, ci = (er[s:s + 1, :] + amr * cr + ami * ci, ei[s:s + 1, :] + amr * ci - ami * cr)
                cin_ref[0, 0:SEGS, ln] = _stack_rows(rows_r)
                cin_ref[1, 0:SEGS, ln] = _stack_rows(rows_i)
                carry_ref[0, :, ln] = cr
                carry_ref[1, :, ln] = ci
        cin_ref[:, SEGS:, :] = cin_ref[:, 0:SEGS, :]
        acc_ref[...] = jnp.zeros_like(acc_ref)

        def fix_rows(rows, k2, prev_re, prev_im):
            pr = p_ref[0, pl.ds(2 * k2, 2)].reshape(PAIR, S5_NS)
            pi = p_ref[1, pl.ds(2 * k2, 2)].reshape(PAIR, S5_NS)
            cr, ci = cin_ref[0], cin_ref[1]
            lr = dst_re[rows, :] + pr * cr + pi * ci
            li = dst_im[rows, :] + pr * ci - pi * cr
            lam_ref[rows, 0:S5_NS] = lr.astype(bf16)
            lam_ref[rows, S5_NS:] = li.astype(bf16)
            acc_ref[0] += lr * prev_re + li * prev_im
            acc_ref[1] += li * prev_re - lr * prev_im

        last = slice(tc - SUBLANES, tc)
        wrap_re = _down_a_segment(sre_ref[last, :], jnp.where(first, 0.0, hre_ref[SUBLANES - 1:SUBLANES, :]))
        wrap_im = _down_a_segment(sim_ref[last, :], jnp.where(first, 0.0, him_ref[SUBLANES - 1:SUBLANES, :]))
        fix_rows(pl.ds(0, PAIR), 0, jnp.concatenate([wrap_re, sre_ref[0:SUBLANES, :]], axis=0),
                 jnp.concatenate([wrap_im, sim_ref[0:SUBLANES, :]], axis=0))

        def fix(k2, _):
            prev = pl.ds(pl.multiple_of(k2 * PAIR - SUBLANES, SUBLANES), PAIR)
            fix_rows(_step_rows(k2, PAIR), k2, sre_ref[prev, :], sim_ref[prev, :])
            return 0

        lax.fori_loop(1, m // 2, fix, 0)
        gab_ref[0] += jnp.sum(acc_ref[0], axis=0, keepdims=True)
        gab_ref[1] += jnp.sum(acc_ref[1], axis=0, keepdims=True)
        for c in range(CB_S5):
            x = gy[:, _lanes(c)] * d_ref[:, _lanes(c)]
            for j in range(SB_PER_CB * c, SB_PER_CB * (c + 1)):
                pair = _state_pair(lam_ref, j)
                x = x + _dot_nt(pair, bdb_ref[j])
                gbdb_ref[j] += _dot_tn(ub[:, _lanes(c)], pair)
            gu_ref[:, _lanes(c)] = x.astype(bf16)
        gd_ref[...] += jnp.sum(gy * u, axis=0, keepdims=True)

    rev = lambda i: (nch - 1 - i, 0)
    halo = lambda i: (jnp.maximum((nch - 1 - i) * hb - 1, 0), 0)
    blk = lambda wd: pl.BlockSpec((tc, wd), rev)
    return pl.pallas_call(
        body, name=name, grid=(nch,),
        in_specs=[blk(S5_WIDTH), blk(S5_WIDTH), blk(S5_NS), blk(S5_NS),
                  pl.BlockSpec((SUBLANES, S5_NS), halo), pl.BlockSpec((SUBLANES, S5_NS), halo), blk(2 * S5_NS),
                  _resident(bdb.shape), _resident(cdb.shape), _full((1, S5_WIDTH)),
                  _resident((2, SEGS, S5_NS)), _resident((2, m, SEGS, S5_NS))],
        out_specs=[blk(S5_WIDTH), _full((2, 1, S5_NS)), _full((1, S5_WIDTH)), _full(bdb.shape), _full(cdb.shape)],
        out_shape=[_S((T, S5_WIDTH), bf16), _S((2, 1, S5_NS)), _S((1, S5_WIDTH)), _S(bdb.shape), _S(cdb.shape)],
        scratch_shapes=[pltpu.VMEM((tc, S5_NS), f32)] * 4 + [
            pltpu.VMEM((tc, 2 * S5_NS), bf16), pltpu.VMEM((2, PAIR, S5_NS), f32), pltpu.VMEM((2, PAIR, S5_NS), f32),
            pltpu.VMEM((2, 1, S5_NS), f32)],
        compiler_params=_cp("arbitrary"),
    )(gys, usg, s_re, s_im, s_re, s_im, s_bf, bdb, cdb, dvec, abar_b, ptab_rev_b)


def s5_post_fwd(ys, usg, wglu, wbs, name):
    T = ys.shape[0]
    tm = _tile(T, TOKEN_TILE)

    def body(ys_ref, sg_ref, wglu_ref, wbs_ref, glu_ref, zs_ref):
        glu = _dot(_gelu(ys_ref[...]).astype(bf16), wglu_ref[...])
        sg = sg_ref[...]
        y2 = glu[:, :S5_WIDTH] * _sig(glu[:, S5_WIDTH:]) * (sg * _sig(sg))
        glu_ref[...] = glu
        zs_ref[...] = _dot(y2.astype(bf16), wbs_ref[...])

    return pl.pallas_call(
        body, name=name, grid=(T // tm,),
        in_specs=[_rows(tm, S5_WIDTH), _rows(tm, S5_WIDTH, 1), _resident((S5_WIDTH, 2 * S5_WIDTH)),
                  _resident((S5_WIDTH, D_MODEL))],
        out_specs=[_rows(tm, 2 * S5_WIDTH), _rows(tm, D_MODEL)],
        out_shape=[_S((T, 2 * S5_WIDTH)), _S((T, D_MODEL))], compiler_params=_cp("parallel"),
    )(ys, usg, wglu, wbs)


def _accumulate(ref, part, step):
    @pl.when(step == 0)
    def _():
        ref[...] = part

    @pl.when(step > 0)
    def _():
        ref[...] += part


def s5_post_bwd(gzs, glu, usg, ys, wbs, wglu, name):
    T = ys.shape[0]
    tm = _tile(T, TOKEN_TILE)

    def body(gzs_ref, glu_ref, sg_ref, ys_ref, wbs_ref, wglu_ref, gys_ref, gsg_ref, gwbs_ref, gwglu_ref):
        i = pl.program_id(0)
        glu = glu_ref[...]
        a, b = glu[:, :S5_WIDTH], glu[:, S5_WIDTH:]
        sg = sg_ref[...]
        ys = ys_ref[...]
        sb, ssg = _sig(b), _sig(sg)
        silu = sg * ssg
        _accumulate(gwbs_ref, _dot_tn((a * sb * silu).astype(bf16), gzs_ref[...]), i)
        gy2 = _dot_nt(gzs_ref[...], wbs_ref[...])
        g_a = gy2 * sb * silu
        g_b = gy2 * a * sb * (1.0 - sb) * silu
        gsg_ref[...] = (gy2 * a * sb * ssg * (1.0 + sg * (1.0 - ssg))).astype(bf16)
        gglu = jnp.concatenate([g_a, g_b], axis=1).astype(bf16)
        _accumulate(gwglu_ref, _dot_tn(_gelu(ys).astype(bf16), gglu), i)
        gys_ref[...] = _dot_nt(gglu, wglu_ref[...]) * _gelu_grad(ys)

    return pl.pallas_call(
        body, name=name, grid=(T // tm,),
        in_specs=[_rows(tm, D_MODEL), _rows(tm, 2 * S5_WIDTH), _rows(tm, S5_WIDTH, 1), _rows(tm, S5_WIDTH),
                  _resident((S5_WIDTH, D_MODEL)), _resident((S5_WIDTH, 2 * S5_WIDTH))],
        out_specs=[_rows(tm, S5_WIDTH), _rows(tm, S5_WIDTH), _full((S5_WIDTH, D_MODEL)), _full((S5_WIDTH, 2 * S5_WIDTH))],
        out_shape=[_S((T, S5_WIDTH)), _S((T, S5_WIDTH), bf16), _S((S5_WIDTH, D_MODEL)), _S((S5_WIDTH, 2 * S5_WIDTH))],
        compiler_params=_cp("arbitrary"),
    )(gzs, glu, usg, ys, wbs, wglu)


NB_LRU = LRU_WIDTH // LANES
LRU_JB = 5
TAPS_BACK = CONV_WIDTH - 1
EDGE = TAPS_BACK * SUBLANES
HALO_ROWS = 4 * SUBLANES


def _down_a_segment(blk, entering_row):
    sub = lax.broadcasted_iota(jnp.int32, blk.shape, 0)
    return jnp.where(sub == 0, entering_row, pltpu.roll(blk, 1, 0))


def _up_a_segment(blk, entering_row):
    sub = lax.broadcasted_iota(jnp.int32, blk.shape, 0)
    return jnp.where(sub == SUBLANES - 1, entering_row, pltpu.roll(blk, SUBLANES - 1, 0))


def _stack_rows(rows):
    sub = lax.broadcasted_iota(jnp.int32, (SUBLANES,) + rows[0].shape[1:], 0)
    out = jnp.broadcast_to(rows[0], sub.shape)
    for s in range(1, SUBLANES):
        out = jnp.where(sub == s, rows[s], out)
    return out


def _fill_conv_window(xe, x_ref, xh_ref, is_first, tc):
    xe[EDGE:, :] = x_ref[...]
    for i in range(1, TAPS_BACK + 1):
        row = HALO_ROWS - SUBLANES * i + SUBLANES - 1
        entering = jnp.where(is_first, 0.0, xh_ref[row:row + 1, :])
        blk = x_ref[tc - SUBLANES * i:tc - SUBLANES * (i - 1), :]
        xe[EDGE - SUBLANES * i:EDGE - SUBLANES * (i - 1), :] = _down_a_segment(blk, entering)


def lru_fwd(lx, convw, convb, wa, wx, ba, bx, lam, name):
    T = lx.shape[0]
    tc = _tile(T, SCAN_CHUNK)
    m = tc // SEGS
    hb = tc // HALO_ROWS

    def body(x_ref, xh_ref, cw_ref, cb_ref, wa_ref, wx_ref, ba_ref, bx_ref, lam_ref,
             c_ref, r_ref, i_ref, h_ref, xe, src_a, src_b, dst_a, dst_h, cin_ref, carry_ref):
        i = pl.program_id(0)

        @pl.when(i == 0)
        def _():
            carry_ref[...] = jnp.zeros_like(carry_ref)

        _fill_conv_window(xe, x_ref, xh_ref, i == 0, tc)
        c = cb_ref[...] + cw_ref[0:1, :] * xe[0:tc, :]
        for k in range(1, CONV_WIDTH):
            c = c + cw_ref[k:k + 1, :] * xe[SUBLANES * k:SUBLANES * k + tc, :]
        c_ref[...] = c
        sp = _softplus_neg(lam_ref[...])
        for j in range(NB_LRU):
            ln = _lanes(j)
            cj = c[:, ln]
            cjb = cj.astype(bf16)
            r = _sig(_dot(cjb, wa_ref[j]) + ba_ref[:, ln])
            g = _sig(_dot(cjb, wx_ref[j]) + bx_ref[:, ln])
            r_ref[:, ln] = r
            i_ref[:, ln] = g
            log_a = -LRU_C * r * sp[:, ln]
            a = jnp.exp(log_a)
            src_a[:, ln] = a
            src_b[:, ln] = jnp.sqrt(_one_minus_sq(a, log_a)) * (g * cj)
        for j0 in range(0, NB_LRU, LRU_JB):
            def kstep(k, st):
                rows = _step_rows(k)
                out = []
                for q in range(LRU_JB):
                    ln = _lanes(j0 + q)
                    hh, ac = st[2 * q], st[2 * q + 1]
                    a = src_a[rows, ln]
                    hh = a * hh + src_b[rows, ln]
                    ac = a * ac
                    dst_h[rows, ln] = hh
                    dst_a[rows, ln] = ac
                    out += [hh, ac]
                return tuple(out)

            init = tuple(jnp.zeros((SEGS, LANES), f32) if q % 2 == 0 else jnp.ones((SEGS, LANES), f32)
                         for q in range(2 * LRU_JB))
            ends = lax.fori_loop(0, m, kstep, init)
            for q in range(LRU_JB):
                ln = _lanes(j0 + q)
                eh, ea = ends[2 * q], ends[2 * q + 1]
                cr = carry_ref[:, ln]
                rows_c = []
                for s in range(SEGS):
                    rows_c.append(cr)
                    cr = eh[s:s + 1, :] + ea[s:s + 1, :] * cr
                cin_ref[:, ln] = _stack_rows(rows_c)
                carry_ref[:, ln] = cr

        def fix(k, _):
            rows = _step_rows(k)
            h_ref[rows, :] = dst_h[rows, :] + dst_a[rows, :] * cin_ref[...]
            return 0

        lax.fori_loop(0, m, fix, 0)

    wide = lambda: _rows(tc, LRU_WIDTH)
    buf = lambda rows: pltpu.VMEM((rows, LRU_WIDTH), f32)
    return pl.pallas_call(
        body, name=name, grid=(T // tc,),
        in_specs=[wide(), pl.BlockSpec((HALO_ROWS, LRU_WIDTH), lambda i: (jnp.maximum(i * hb - 1, 0), 0)),
                  _full((CONV_WIDTH, LRU_WIDTH)), _full((1, LRU_WIDTH)),
                  _full((LRU_HEADS, LRU_HEAD_DIM, LRU_HEAD_DIM)), _full((LRU_HEADS, LRU_HEAD_DIM, LRU_HEAD_DIM)),
                  _full((1, LRU_WIDTH)), _full((1, LRU_WIDTH)), _full((1, LRU_WIDTH))],
        out_specs=[wide(), wide(), wide(), wide()],
        out_shape=[_S((T, LRU_WIDTH))] * 4,
        scratch_shapes=[buf(tc + EDGE), buf(tc), buf(tc), buf(tc), buf(tc), buf(SEGS), buf(1)],
        compiler_params=_cp("arbitrary"),
    )(lx, lx, convw, convb, wa, wx, ba, bx, lam)


def lru_bwd(gh, h, c, r, gi, lx, convw, wa, wx, lam, name, exchange=()):
    T = gh.shape[0]
    tc = _tile(T, SCAN_CHUNK)
    m = tc // SEGS
    nch = T // tc
    ne = len(exchange)

    def body(*refs):
        gh_ref, h_ref, hh_ref, c_ref, r_ref, i_ref, x_ref, xh_ref, cw_ref, wa_ref, wx_ref, lam_ref = refs[:12]
        glx_ref, gwa_ref, gwx_ref, gba_ref, gbx_ref, glam_ref, gcb_ref, gcw_ref = refs[12 + ne:20 + ne]
        src_a, src_m, dst_a, dst_m, mbuf, hbuf, xe, gce, cin_ref, gcc_ref, carry_ref = refs[20 + 2 * ne:31 + 2 * ne]
        i = pl.program_id(0)
        if ne:
            start, finish = _chips_phases(refs[12:12 + ne], refs[20 + ne:20 + 2 * ne], *refs[31 + 2 * ne:])
            pl.when(i == 0)(start)
            pl.when(i == nch - 1)(finish)

        @pl.when(i == 0)
        def _():
            carry_ref[...] = jnp.zeros_like(carry_ref)
            gcc_ref[...] = jnp.zeros_like(gcc_ref)
            for ref in (gwa_ref, gwx_ref, gba_ref, gbx_ref, glam_ref, gcb_ref, gcw_ref):
                ref[...] = jnp.zeros_like(ref)

        first = i == nch - 1
        last = slice(tc - SUBLANES, tc)
        hbuf[SUBLANES:, :] = h_ref[...]
        hbuf[0:SUBLANES, :] = _down_a_segment(h_ref[last, :], jnp.where(first, 0.0, hh_ref[SUBLANES - 1:SUBLANES, :]))
        _fill_conv_window(xe, x_ref, xh_ref, first, tc)
        lam_v = lam_ref[...]
        sp = _softplus_neg(lam_v)
        a_all = jnp.exp(-LRU_C * r_ref[...] * sp)
        src_a[...] = a_all
        src_m[...] = a_all * gh_ref[...]
        for j0 in range(0, NB_LRU, LRU_JB):
            def kstep(kk, st):
                rows = _step_rows(m - 1 - kk)
                out = []
                for q in range(LRU_JB):
                    ln = _lanes(j0 + q)
                    mu, ac = st[2 * q], st[2 * q + 1]
                    a = src_a[rows, ln]
                    mu = a * mu + src_m[rows, ln]
                    ac = a * ac
                    dst_m[rows, ln] = mu
                    dst_a[rows, ln] = ac
                    out += [mu, ac]
                return tuple(out)

            init = tuple(jnp.zeros((SEGS, LANES), f32) if q % 2 == 0 else jnp.ones((SEGS, LANES), f32)
                         for q in range(2 * LRU_JB))
            ends = lax.fori_loop(0, m, kstep, init)
            for q in range(LRU_JB):
                ln = _lanes(j0 + q)
                em, ea = ends[2 * q], ends[2 * q + 1]
                cr = carry_ref[:, ln]
                rows_c = [None] * SEGS
                for s in reversed(range(SEGS)):
                    rows_c[s] = cr
                    cr = em[s:s + 1, :] + ea[s:s + 1, :] * cr
                cin_ref[:, ln] = _stack_rows(rows_c)
                carry_ref[:, ln] = cr

        def fix(k, _):
            rows = _step_rows(k)
            mbuf[rows, :] = dst_m[rows, :] + dst_a[rows, :] * cin_ref[...]
            return 0

        lax.fori_loop(0, m, fix, 0)
        mbuf[tc:, :] = _up_a_segment(mbuf[0:SUBLANES, :], cin_ref[SUBLANES - 1:SUBLANES, :])
        sneg = _sig(-lam_v)
        for j in range(NB_LRU):
            ln = _lanes(j)
            lamt = gh_ref[:, ln] + mbuf[SUBLANES:, ln]
            rj, ij, cj = r_ref[:, ln], i_ref[:, ln], c_ref[:, ln]
            log_a = -LRU_C * rj * sp[:, ln]
            a = src_a[:, ln]
            om = _one_minus_sq(a, log_a)
            inv_mult = lax.rsqrt(om)
            mult = om * inv_mult
            g_a = lamt * hbuf[0:tc, ln]
            g_mult = lamt * ij * cj
            g_i = lamt * mult * cj
            g_c = lamt * mult * ij
            g_log_a = g_a * a - g_mult * a * a * inv_mult
            glam_ref[:, ln] += jnp.sum(g_log_a * rj, axis=0, keepdims=True) * LRU_C * sneg[:, ln]
            g_ra = g_log_a * (-LRU_C) * sp[:, ln] * rj * (1.0 - rj)
            g_ia = g_i * ij * (1.0 - ij)
            gba_ref[:, ln] += jnp.sum(g_ra, axis=0, keepdims=True)
            gbx_ref[:, ln] += jnp.sum(g_ia, axis=0, keepdims=True)
            cjb, grb, gib = cj.astype(bf16), g_ra.astype(bf16), g_ia.astype(bf16)
            gwa_ref[j] += _dot_tn(cjb, grb)
            gwx_ref[j] += _dot_tn(cjb, gib)
            g_c = g_c + _dot_nt(grb, wa_ref[j]) + _dot_nt(gib, wx_ref[j])
            gce[0:tc, ln] = g_c
            gcb_ref[:, ln] += jnp.sum(g_c, axis=0, keepdims=True)
        for d in range(TAPS_BACK):
            blk = slice(SUBLANES * d, SUBLANES * (d + 1))
            gce[tc + SUBLANES * d:tc + SUBLANES * (d + 1), :] = _up_a_segment(gce[blk, :], gcc_ref[SUBLANES * d:SUBLANES * d + 1, :])
        gcc_ref[...] = gce[0:EDGE, :]
        gc = gce[0:tc, :]
        glx = cw_ref[CONV_WIDTH - 1:CONV_WIDTH, :] * gc
        gcw_ref[CONV_WIDTH - 1:CONV_WIDTH, :] += jnp.sum(gc * xe[EDGE:EDGE + tc, :], axis=0, keepdims=True)
        for k in range(CONV_WIDTH - 1):
            off = SUBLANES * (CONV_WIDTH - 1 - k)
            glx = glx + cw_ref[k:k + 1, :] * gce[off:off + tc, :]
            gcw_ref[k:k + 1, :] += jnp.sum(gc * xe[EDGE - off:EDGE - off + tc, :], axis=0, keepdims=True)
        glx_ref[...] = glx.astype(bf16)

    rev = lambda i: (nch - 1 - i, 0)
    halo = lambda rows: (lambda i: (jnp.maximum((nch - 1 - i) * (tc // rows) - 1, 0), 0))
    wide = lambda: pl.BlockSpec((tc, LRU_WIDTH), rev)
    vec = lambda: _full((1, LRU_WIDTH))
    hd = lambda: _full((LRU_HEADS, LRU_HEAD_DIM, LRU_HEAD_DIM))
    buf = lambda rows: pltpu.VMEM((rows, LRU_WIDTH), f32)
    outs = pl.pallas_call(
        body, name=name, grid=(nch,),
        in_specs=[wide(), wide(), pl.BlockSpec((SUBLANES, LRU_WIDTH), halo(SUBLANES)), wide(), wide(), wide(), wide(),
                  pl.BlockSpec((HALO_ROWS, LRU_WIDTH), halo(HALO_ROWS)), _full((CONV_WIDTH, LRU_WIDTH)), hd(), hd(), vec()]
        + [ANY] * ne,
        out_specs=[wide(), hd(), hd(), vec(), vec(), vec(), vec(), _full((CONV_WIDTH, LRU_WIDTH))] + [ANY] * ne,
        out_shape=[_S((T, LRU_WIDTH), bf16), _S((LRU_HEADS, LRU_HEAD_DIM, LRU_HEAD_DIM)),
                   _S((LRU_HEADS, LRU_HEAD_DIM, LRU_HEAD_DIM)), _S((1, LRU_WIDTH)), _S((1, LRU_WIDTH)),
                   _S((1, LRU_WIDTH)), _S((1, LRU_WIDTH)), _S((CONV_WIDTH, LRU_WIDTH))] + (
            _chips_out_shapes(exchange) if ne else []),
        scratch_shapes=[buf(tc), buf(tc), buf(tc), buf(tc), buf(tc + SUBLANES), buf(tc + SUBLANES), buf(tc + EDGE),
                        buf(tc + EDGE), buf(SEGS), buf(EDGE), buf(1)] + (_chips_semaphores(ne) if ne else []),
        compiler_params=_cp("arbitrary"),
    )(gh, h, h, c, r, gi, lx, lx, convw, wa, wx, lam, *exchange)
    return outs[:8], list(outs[8:])


def merge_fwd(h, lg, zs, gsl, x, p, wbl, wout, gpost, wple, wpg, name):
    T = x.shape[0]
    tm = _tile(T, TOKEN_TILE)

    def body(h_ref, lg_ref, zs_ref, gs_ref, gl_ref, x_ref, p_ref, wbl_ref, wout_ref, gp_ref, wple_ref, wpg_ref,
             zl_ref, mix_ref, xo_ref):
        lg_v = lg_ref[...]
        yl = h_ref[...] * (lg_v * _sig(lg_v))
        zl = _dot(yl.astype(bf16), wbl_ref[...])
        merged = _sig(gs_ref[...]) * zs_ref[...] + _sig(gl_ref[...]) * zl
        mix = _dot(merged.astype(bf16), wout_ref[...])
        r2 = lax.rsqrt(jnp.mean(mix * mix, axis=-1, keepdims=True) + NORM_EPS)
        x1 = x_ref[...] + mix * r2 * gp_ref[...]
        q = _dot(x1.astype(bf16), wpg_ref[...])
        pe = _dot(p_ref[...].astype(bf16), wple_ref[...])
        zl_ref[...], mix_ref[...] = zl, mix
        xo_ref[...] = x1 + pe * _sig(q)

    dm = lambda: _rows(tm, D_MODEL)
    return pl.pallas_call(
        body, name=name, grid=(T // tm,),
        in_specs=[_rows(tm, LRU_WIDTH), _rows(tm, LRU_WIDTH), dm(), _rows(tm, D_MODEL, 0), _rows(tm, D_MODEL, 1), dm(),
                  _rows(tm, PLE_DIM), _resident((LRU_WIDTH, D_MODEL)), _resident((D_MODEL, D_MODEL)), _full((1, D_MODEL)),
                  _resident((PLE_DIM, D_MODEL)), _resident((D_MODEL, D_MODEL))],
        out_specs=[dm(), dm(), dm()],
        out_shape=[_S((T, D_MODEL))] * 3, compiler_params=_cp("parallel"),
    )(h, lg, zs, gsl, gsl, x, p, wbl, wout, gpost, wple, wpg)


def post_bwd(gx2, mix, x, p, wpg, wple, gpost, name):
    T = x.shape[0]
    tm = _tile(T, TOKEN_TILE)

    def body(gx2_ref, mix_ref, x_ref, p_ref, wpg_ref, wple_ref, gp_ref, gres_ref, gmix_ref, ggp_ref, gwpg_ref, gwple_ref):
        i = pl.program_id(0)
        gx2 = gx2_ref[...]
        mix = mix_ref[...]
        gp = gp_ref[...]
        r2 = lax.rsqrt(jnp.mean(mix * mix, axis=-1, keepdims=True) + NORM_EPS)
        nrm = mix * r2
        x1b = (x_ref[...] + nrm * gp).astype(bf16)
        pb = p_ref[...].astype(bf16)
        sq = _sig(_dot(x1b, wpg_ref[...]))
        pe = _dot(pb, wple_ref[...])
        gq = (gx2 * pe * sq * (1.0 - sq)).astype(bf16)
        _accumulate(gwple_ref, _dot_tn(pb, (gx2 * sq).astype(bf16)), i)
        _accumulate(gwpg_ref, _dot_tn(x1b, gq), i)
        gx1 = gx2 + _dot_nt(gq, wpg_ref[...])
        gres_ref[...] = gx1
        _accumulate(ggp_ref, jnp.sum(gx1 * nrm, axis=0, keepdims=True), i)
        gy = gx1 * gp
        gmix_ref[...] = (r2 * (gy - nrm * jnp.mean(gy * nrm, axis=-1, keepdims=True))).astype(bf16)

    dm = lambda: _rows(tm, D_MODEL)
    return pl.pallas_call(
        body, name=name, grid=(T // tm,),
        in_specs=[dm(), dm(), dm(), _rows(tm, PLE_DIM), _resident((D_MODEL, D_MODEL)), _resident((PLE_DIM, D_MODEL)),
                  _full((1, D_MODEL))],
        out_specs=[dm(), dm(), _full((1, D_MODEL)), _full((D_MODEL, D_MODEL)), _full((PLE_DIM, D_MODEL))],
        out_shape=[_S((T, D_MODEL)), _S((T, D_MODEL), bf16), _S((1, D_MODEL)), _S((D_MODEL, D_MODEL)),
                   _S((PLE_DIM, D_MODEL))],
        compiler_params=_cp("arbitrary"),
    )(gx2, mix, x, p, wpg, wple, gpost)


def gate_bwd(gmix, zl, zs, gsl, h, lg, wout, wbl, name):
    T = zl.shape[0]
    tm = _tile(T, TOKEN_TILE)

    def body(gmix_ref, zl_ref, zs_ref, gs_ref, gl_ref, h_ref, lg_ref, wout_ref, wbl_ref,
             gzs_ref, ggsl_ref, gh_ref, glg_ref, gwout_ref, gwbl_ref):
        i = pl.program_id(0)
        gmix = gmix_ref[...]
        gmerged = _dot_nt(gmix, wout_ref[...])
        zs, zl = zs_ref[...], zl_ref[...]
        ss, sl = _sig(gs_ref[...]), _sig(gl_ref[...])
        _accumulate(gwout_ref, _dot_tn((ss * zs + sl * zl).astype(bf16), gmix), i)
        gzs_ref[...] = (gmerged * ss).astype(bf16)
        gzl = (gmerged * sl).astype(bf16)
        ggsl_ref[:, :D_MODEL] = (gmerged * zs * ss * (1.0 - ss)).astype(bf16)
        ggsl_ref[:, D_MODEL:] = (gmerged * zl * sl * (1.0 - sl)).astype(bf16)
        lg_v, hv = lg_ref[...], h_ref[...]
        slg = _sig(lg_v)
        silu = lg_v * slg
        _accumulate(gwbl_ref, _dot_tn((hv * silu).astype(bf16), gzl), i)
        gyl = _dot_nt(gzl, wbl_ref[...])
        gh_ref[...] = gyl * silu
        glg_ref[...] = (gyl * hv * slg * (1.0 + lg_v * (1.0 - slg))).astype(bf16)

    dm = lambda: _rows(tm, D_MODEL)
    lw = lambda: _rows(tm, LRU_WIDTH)
    return pl.pallas_call(
        body, name=name, grid=(T // tm,),
        in_specs=[dm(), dm(), dm(), _rows(tm, D_MODEL, 0), _rows(tm, D_MODEL, 1), lw(), lw(),
                  _resident((D_MODEL, D_MODEL)), _resident((LRU_WIDTH, D_MODEL))],
        out_specs=[dm(), _rows(tm, 2 * D_MODEL), lw(), lw(), _full((D_MODEL, D_MODEL)), _full((LRU_WIDTH, D_MODEL))],
        out_shape=[_S((T, D_MODEL), bf16), _S((T, 2 * D_MODEL), bf16), _S((T, LRU_WIDTH)), _S((T, LRU_WIDTH), bf16),
                   _S((D_MODEL, D_MODEL)), _S((LRU_WIDTH, D_MODEL))],
        compiler_params=_cp("arbitrary"),
    )(gmix, zl, zs, gsl, gsl, h, lg, wout, wbl)


def in_proj_bwd(pieces, win, x, gres, g, name):
    T = x.shape[0]
    tm = _tile(T, MM_TILE_M // 2)
    widths = [pc.shape[1] for pc in pieces]
    offs = [sum(widths[:k]) for k in range(len(widths))]

    def body(*refs):
        pc_refs = refs[:len(widths)]
        w_ref, x_ref, gres_ref, g_ref, gx_ref, gg_ref = refs[len(widths):]
        i = pl.program_id(0)
        ghv = _dot_nt(pc_refs[0][...], w_ref[:, offs[0]:offs[0] + widths[0]])
        for k in range(1, len(widths)):
            ghv = ghv + _dot_nt(pc_refs[k][...], w_ref[:, offs[k]:offs[k] + widths[k]])
        xv = x_ref[...]
        r = lax.rsqrt(jnp.mean(xv * xv, axis=-1, keepdims=True) + NORM_EPS)
        nrm = xv * r
        gy = ghv * g_ref[...]
        gx_ref[...] = gres_ref[...] + r * (gy - nrm * jnp.mean(gy * nrm, axis=-1, keepdims=True))
        _accumulate(gg_ref, jnp.sum(ghv * nrm, axis=0, keepdims=True), i)

    dm = lambda: _rows(tm, D_MODEL)
    return pl.pallas_call(
        body, name=name, grid=(T // tm,),
        in_specs=[_rows(tm, wd) for wd in widths] + [_resident(win.shape), dm(), dm(), _full((1, D_MODEL))],
        out_specs=[dm(), _full((1, D_MODEL))],
        out_shape=[_S((T, D_MODEL)), _S((1, D_MODEL))], compiler_params=_cp("arbitrary"),
    )(*pieces, win, x, gres, g)


def loss_head(y, target, name):
    T = y.shape[0]
    tm = _tile(T, TOKEN_TILE)

    def body(y_ref, t_ref, l_ref, g_ref):
        i = pl.program_id(0)
        e = y_ref[...] - t_ref[...]
        g_ref[...] = e * (1.0 / D_MODEL)
        part = 0.5 * jnp.sum(jnp.sum(e * e, axis=-1, keepdims=True) * (1.0 / D_MODEL), axis=0, keepdims=True)

        @pl.when(i == 0)
        def _():
            l_ref[...] = part

        @pl.when(i > 0)
        def _():
            l_ref[...] += part

    return pl.pallas_call(
        body, name=name, grid=(T // tm,),
        in_specs=[_rows(tm, D_MODEL), _rows(tm, D_MODEL)], out_specs=[_full((1, 1)), _rows(tm, D_MODEL)],
        out_shape=[_S((1, 1)), _S((T, D_MODEL))],
        compiler_params=_cp("arbitrary"),
    )(y, target)


def _s5_operands(w, m, tag):
    b_re_t = jnp.transpose(w['s5_b_re'], (2, 0, 1))
    b_im_t = jnp.transpose(w['s5_b_im'], (2, 0, 1))
    ldt = w['s5_log_dt'][:, None]
    ab, pw, bb = s5_prep(w['s5_a_re'], w['s5_a_im'], ldt, b_re_t, b_im_t, m, "s5_prep" + tag)
    over_sublanes = lambda t: jnp.broadcast_to(t[..., None, :], t.shape[:-1] + (SEGS, S5_NS))
    ptab = pw.reshape(2, m, S5_NS)
    return dict(abar_b=over_sublanes(ab.reshape(2, S5_NS)), ptab_b=over_sublanes(ptab),
                ptab_rev_b=over_sublanes(ptab[:, ::-1, :]), bdb=_pack_bdb(bb).astype(bf16),
                cdb=_pack_cdb(w['s5_c_re'], w['s5_c_im']).astype(bf16), dvec=w['s5_d'][None, :],
                prep_in=(w['s5_a_re'], w['s5_a_im'], ldt, b_re_t, b_im_t))


def layer_fwd(x, p, w, tag, gather=()):
    T = x.shape[0]
    m = _tile(T, SCAN_CHUNK) // SEGS
    s5 = _s5_operands(w, m, tag)
    h_bf = rms_fwd(x, w['g_pre'][None, :], "rms_fwd" + tag)
    win = w['w_in']
    usg = mm_nn(h_bf, win[:, :2 * S5_WIDTH], "proj_s5" + tag)
    lx = mm_nn(h_bf, win[:, 2 * S5_WIDTH:2 * S5_WIDTH + LRU_WIDTH], "proj_lx" + tag)
    lg = mm_nn(h_bf, win[:, 2 * S5_WIDTH + LRU_WIDTH:2 * S5_WIDTH + 2 * LRU_WIDTH], "proj_lg" + tag)
    gsl = mm_nn(h_bf, win[:, 2 * S5_WIDTH + 2 * LRU_WIDTH:], "proj_gate" + tag)
    (ys, s_re, s_im, s_bf), gathered = s5_fwd(usg, s5['bdb'], s5['cdb'], s5['dvec'], s5['abar_b'], s5['ptab_b'],
                                              "s5_fwd" + tag, gather)
    glu, zs = s5_post_fwd(ys, usg, w['w_glu'], w['w_bs'], "s5_post_fwd" + tag)
    wa, wx = w['lru_w_a'].astype(bf16), w['lru_w_x'].astype(bf16)
    c, r, gi, hs = lru_fwd(lx, w['conv_w'], w['conv_b'][None, :], wa, wx, w['lru_b_a'][None, :], w['lru_b_x'][None, :],
                           w['lru_lambda'][None, :], "lru_fwd" + tag)
    zl, mix, x_out = merge_fwd(hs, lg, zs, gsl, x, p, w['w_bl'], w['w_out'], w['g_post'][None, :],
                               w['w_ple'], w['w_ple_gate'], "merge_fwd" + tag)
    saved = dict(x=x, p=p, h_bf=h_bf, usg=usg, lx=lx, lg=lg, gsl=gsl, ys=ys, s_re=s_re, s_im=s_im, s_bf=s_bf, glu=glu,
                 zs=zs, c=c, r=r, gi=gi, hs=hs, zl=zl, mix=mix, s5=s5, wa=wa, wx=wx)
    return x_out, saved, gathered


def layer_bwd(gx_out, w, sv, tag, exchange=()):
    s5 = sv['s5']
    g = {}
    gres, gmix, g_gpost, g['w_ple_gate'], g['w_ple'] = post_bwd(
        gx_out, sv['mix'], sv['x'], sv['p'], w['w_ple_gate'], w['w_ple'], w['g_post'][None, :], "post_bwd" + tag)
    gzs, ggsl, g_h, g_lg, g['w_out'], g['w_bl'] = gate_bwd(
        gmix, sv['zl'], sv['zs'], sv['gsl'], sv['hs'], sv['lg'], w['w_out'], w['w_bl'], "gate_bwd" + tag)
    g['g_post'] = g_gpost[0]
    (g_lx, g_wa, g_wx, g_ba, g_bx, g_lam, g_cb, g_cw), exchanged = lru_bwd(
        g_h, sv['hs'], sv['c'], sv['r'], sv['gi'], sv['lx'], w['conv_w'], sv['wa'], sv['wx'],
        w['lru_lambda'][None, :], "lru_bwd" + tag, exchange)
    g['lru_w_a'], g['lru_w_x'] = g_wa, g_wx
    g['lru_b_a'], g['lru_b_x'], g['lru_lambda'], g['conv_b'], g['conv_w'] = g_ba[0], g_bx[0], g_lam[0], g_cb[0], g_cw
    g_ys, g_sg, g['w_bs'], g['w_glu'] = s5_post_bwd(gzs, sv['glu'], sv['usg'], sv['ys'], w['w_bs'], w['w_glu'],
                                                    "s5_post_bwd" + tag)
    g_u, g_ab, g_d, g_bdb, g_cdb = s5_bwd(g_ys, sv['usg'], sv['s_re'], sv['s_im'], sv['s_bf'], s5['bdb'], s5['cdb'],
                                          s5['dvec'], s5['abar_b'], s5['ptab_rev_b'], "s5_bwd" + tag)
    g['s5_d'] = g_d[0]
    g['s5_c_re'], g['s5_c_im'] = _unpack_cdb(g_cdb)
    g_are, g_aim, g_ldt, g_bre_t, g_bim_t = s5_prep_bwd(*s5['prep_in'], g_ab.reshape(2, S5_GROUPS, S5_STATE),
                                                       _unpack_bdb(g_bdb), "s5_prep_bwd" + tag)
    g['s5_a_re'], g['s5_a_im'], g['s5_log_dt'] = g_are, g_aim, g_ldt
    g['s5_b_re'] = jnp.transpose(g_bre_t, (1, 2, 0))
    g['s5_b_im'] = jnp.transpose(g_bim_t, (1, 2, 0))
    pieces = [g_u, g_sg, g_lx, g_lg, ggsl]
    g['w_in'] = jnp.concatenate([mm_tn(sv['h_bf'], pc, "gw_in%d%s" % (k, tag)) for k, pc in enumerate(pieces)], axis=1)
    gx, g_gpre = in_proj_bwd(pieces, w['w_in'], sv['x'], gres, w['g_pre'][None, :], "in_proj_bwd" + tag)
    g['g_pre'] = g_gpre[0]
    return gx, g, exchanged


def _as_2d(a):
    return a.reshape((-1, a.shape[-1])) if a.ndim > 1 else a.reshape((1, -1))


def _adamw_update(w, gv, m, v):
    nm = ADAM_B1 * m + (1.0 - ADAM_B1) * gv
    nv = ADAM_B2 * v + (1.0 - ADAM_B2) * (gv * gv)
    bc1 = 1.0 - ADAM_B1 ** ADAM_STEP
    bc2 = 1.0 - ADAM_B2 ** ADAM_STEP
    return -ADAM_LR * ((nm / bc1) / (jnp.sqrt(nv / bc2) + ADAM_EPS) + ADAM_WD * w), nm, nv


def adamw(w, g, m, v, name):
    shape = w.shape
    w2, g2, m2, v2 = _as_2d(w), _as_2d(g), _as_2d(m), _as_2d(v)
    R, C = w2.shape
    tr = _row_tile(R, C)

    def body(w_ref, g_ref, m_ref, v_ref, d_ref, nm_ref, nv_ref):
        d_ref[...], nm_ref[...], nv_ref[...] = _adamw_update(w_ref[...], g_ref[...], m_ref[...], v_ref[...])

    spec = lambda: pl.BlockSpec((tr, C), lambda i: (i, 0))
    d, nm, nv = pl.pallas_call(
        body, name=name, grid=(R // tr,), in_specs=[spec() for _ in range(4)], out_specs=[spec() for _ in range(3)],
        out_shape=[_S((R, C))] * 3, compiler_params=_cp("parallel"),
    )(w2, g2, m2, v2)
    return d.reshape(shape), nm.reshape(shape), nv.reshape(shape)


def adamw_reduce(w, parts, theirs, m, v, chip, name):
    shape = w.shape
    C = shape[-1]
    R = math.prod(shape[1:-1])
    w3, m3, v3 = w.reshape(DEPTH, R, C), m.reshape(DEPTH, R, C), v.reshape(DEPTH, R, C)
    tr = _row_tile(R, C)

    def body(chip_ref, w_ref, *refs):
        layer_refs, (m_ref, v_ref, g_ref, d_ref, nm_ref, nv_ref) = refs[:2 * DEPTH], refs[2 * DEPTH:]
        layer = pl.program_id(0)
        for l in range(DEPTH):
            @pl.when(layer == l)
            def _():
                a_ref, t_ref = layer_refs[2 * l], layer_refs[2 * l + 1]
                gv = ((a_ref[0] + t_ref[0].astype(f32)) + t_ref[1].astype(f32)) + t_ref[2].astype(f32)
                g_ref[0] = gv
                d_ref[0], nm_ref[0], nv_ref[0] = _adamw_update(w_ref[0], gv, m_ref[0], v_ref[0])

    spec = lambda: pl.BlockSpec((1, tr, C), lambda l, i, c: (l, i, 0))
    rows_of = lambda l: (lambda ll, i, c: jnp.where(ll == l, i, 0))
    layer_specs = []
    for l in range(DEPTH):
        layer_specs.append(pl.BlockSpec((1, tr, C), lambda ll, i, c, r=rows_of(l): (c[0], r(ll, i, c), 0)))
        layer_specs.append(pl.BlockSpec((3, tr, C), lambda ll, i, c, r=rows_of(l): (0, r(ll, i, c), 0)))
    grid_spec = pltpu.PrefetchScalarGridSpec(
        num_scalar_prefetch=1, grid=(DEPTH, R // tr),
        in_specs=[spec()] + layer_specs + [spec(), spec()], out_specs=[spec() for _ in range(4)])
    operands = [x.reshape(x.shape[0], R, C) for l in range(DEPTH) for x in (parts[l], theirs[l])]
    g, d, nm, nv = pl.pallas_call(
        body, name=name, grid_spec=grid_spec, out_shape=[_S((DEPTH, R, C))] * 4,
        compiler_params=_cp("arbitrary", "arbitrary"),
    )(chip, w3, *operands, m3, v3)
    return g.reshape(shape), d.reshape(shape), nm.reshape(shape), nv.reshape(shape)


MESH = pl.DeviceIdType.MESH
ANY = pl.BlockSpec(memory_space=pl.ANY)


def _place():
    return lax.axis_index("x"), lax.axis_index("y"), lax.axis_index("c")


def _other_chips(mx, my):
    return [(1 - mx, my), (mx, 1 - my), (1 - mx, 1 - my)]


def all_gather(shards, name):
    nb = len(shards)

    def body(*refs):
        phases = _gather_phases([s.shape for s in shards], refs[:nb], refs[nb:2 * nb], *refs[2 * nb:])
        for phase in phases:
            phase()

    outs = pl.pallas_call(
        body, name=name, out_shape=_gather_out_shapes(shards), in_specs=[ANY] * nb, out_specs=[ANY] * nb,
        scratch_shapes=_gather_semaphores(nb),
    )(*shards)
    return list(outs)


GATHER_COPIES = 9
OWN_SIB, OWN_X, OWN_Y, X_SIB, Y_SIB, RELAY_X, RELAY_Y, DIAG0_SIB, DIAG1_SIB = range(GATHER_COPIES)


def _gather_out_shapes(shards):
    return [_S((N_DEV,) + s.shape, s.dtype) for s in shards]


def _gather_semaphores(nb):
    return [pltpu.SemaphoreType.DMA((nb, GATHER_COPIES)), pltpu.SemaphoreType.DMA((nb, GATHER_COPIES)),
            pltpu.SemaphoreType.DMA((nb,))]


def _gather_phases(shapes, x_refs, out_refs, send_sems, recv_sems, local_sems):
    nb = len(shapes)
    mx, my, mc = _place()
    sibling, xn, yn = (mx, my, 1 - mc), (1 - mx, my, mc), (mx, 1 - my, mc)

    def block(b, px, py, pc, half=None):
        ref = out_refs[b].at[4 * px + 2 * py + pc]
        if half is None:
            return ref
        n = shapes[b][0] // 2
        return ref.at[pl.ds(half * n, n)]

    def copy(b, k, dst, to, src=None):
        return pltpu.make_async_remote_copy(
            src_ref=dst if src is None else src, dst_ref=dst, send_sem=send_sems.at[b, k],
            recv_sem=recv_sems.at[b, k], device_id=to, device_id_type=MESH)

    def send(b, k):
        if k in (OWN_X, OWN_Y, OWN_SIB):
            return copy(b, k, block(b, mx, my, mc), {OWN_X: xn, OWN_Y: yn, OWN_SIB: sibling}[k], src=x_refs[b])
        what, to = {RELAY_X: ((1 - mx, my, mc, 0), yn), RELAY_Y: ((mx, 1 - my, mc, 1), xn),
                    X_SIB: ((1 - mx, my, mc), sibling), Y_SIB: ((mx, 1 - my, mc), sibling),
                    DIAG0_SIB: ((1 - mx, 1 - my, mc, 0), sibling), DIAG1_SIB: ((1 - mx, 1 - my, mc, 1), sibling)}[k]
        return copy(b, k, block(b, *what), to)

    def local(b):
        return pltpu.make_async_copy(x_refs[b], block(b, mx, my, mc), local_sems.at[b])

    def send_own():
        for k in (OWN_X, OWN_Y, OWN_SIB):
            for b in range(nb):
                send(b, k).start()
        for b in range(nb):
            local(b).start()

    def relay_neighbours():
        for b in range(nb):
            copy(b, OWN_X, block(b, 1 - mx, my, mc), xn).wait_recv()
            send(b, RELAY_X).start()
            send(b, X_SIB).start()
        for b in range(nb):
            copy(b, OWN_Y, block(b, mx, 1 - my, mc), yn).wait_recv()
            send(b, RELAY_Y).start()
            send(b, Y_SIB).start()

    def hand_on_diagonal():
        for b in range(nb):
            copy(b, RELAY_X, block(b, 1 - mx, 1 - my, mc, 0), yn).wait_recv()
            send(b, DIAG0_SIB).start()
            copy(b, RELAY_Y, block(b, 1 - mx, 1 - my, mc, 1), xn).wait_recv()
            send(b, DIAG1_SIB).start()

    def finish():
        for b in range(nb):
            copy(b, OWN_SIB, block(b, mx, my, 1 - mc), sibling).wait_recv()
            copy(b, X_SIB, block(b, 1 - mx, my, 1 - mc), sibling).wait_recv()
            copy(b, Y_SIB, block(b, mx, 1 - my, 1 - mc), sibling).wait_recv()
            copy(b, DIAG0_SIB, block(b, 1 - mx, 1 - my, 1 - mc, 0), sibling).wait_recv()
            copy(b, DIAG1_SIB, block(b, 1 - mx, 1 - my, 1 - mc, 1), sibling).wait_recv()
        for b in range(nb):
            for k in range(GATHER_COPIES):
                send(b, k).wait_send()
            local(b).wait()

    return send_own, relay_neighbours, hand_on_diagonal, finish


def exchange_sibling(gs, name):
    nb = len(gs)

    def body(*refs):
        g_refs, recv_refs, send_sems, recv_sems = refs[:nb], refs[nb:2 * nb], refs[2 * nb], refs[2 * nb + 1]
        mx, my, mc = _place()
        copies = [pltpu.make_async_remote_copy(
            src_ref=g_refs[b].at[2 * k + 1 - mc], dst_ref=recv_refs[b].at[k], send_sem=send_sems.at[b, k],
            recv_sem=recv_sems.at[b, k], device_id=(mx, my, 1 - mc), device_id_type=MESH)
            for b in range(nb) for k in range(4)]
        for cp in copies:
            cp.start()
        for cp in copies:
            cp.wait()

    outs = pl.pallas_call(
        body, name=name, out_shape=[_S((4,) + g.shape[1:], g.dtype) for g in gs], in_specs=[ANY] * nb,
        out_specs=[ANY] * nb,
        scratch_shapes=[pltpu.SemaphoreType.DMA((nb, 4)), pltpu.SemaphoreType.DMA((nb, 4))],
    )(*gs)
    return list(outs)


def exchange_chips(parts, name):
    nb = len(parts)

    def body(*refs):
        start, finish = _chips_phases(refs[:nb], refs[nb:2 * nb], refs[2 * nb], refs[2 * nb + 1])
        start()
        finish()

    outs = pl.pallas_call(
        body, name=name, out_shape=_chips_out_shapes(parts), in_specs=[ANY] * nb, out_specs=[ANY] * nb,
        scratch_shapes=_chips_semaphores(nb),
    )(*parts)
    return list(outs)


def _chips_out_shapes(parts):
    return [_S((3,) + a.shape[1:], a.dtype) for a in parts]


def _chips_semaphores(nb):
    return [pltpu.SemaphoreType.DMA((nb, 3)), pltpu.SemaphoreType.DMA((nb, 3))]


def _chips_phases(a_refs, recv_refs, send_sems, recv_sems):
    mx, my, mc = _place()

    def copies():
        return [pltpu.make_async_remote_copy(
            src_ref=a_refs[b].at[2 * px + py], dst_ref=recv_refs[b].at[j], send_sem=send_sems.at[b, j],
            recv_sem=recv_sems.at[b, j], device_id=(px, py, mc), device_id_type=MESH)
            for b in range(len(a_refs)) for j, (px, py) in enumerate(_other_chips(mx, my))]

    def start():
        for cp in copies():
            cp.start()

    def finish():
        for cp in copies():
            cp.wait()

    return start, finish


def add_sibling(g, theirs, core, name, wire_dtype=f32):
    shp = theirs.shape
    C = shp[-1]
    R = math.prod(shp[1:-1])
    tr = _row_tile(R, C)
    narrow = wire_dtype != f32

    def body(core_ref, g_ref, t_ref, o_ref, *wire_ref):
        s = g_ref[...] + t_ref[...]
        o_ref[...] = s
        if narrow:
            wire_ref[0][...] = s.astype(wire_dtype)

    blk = lambda: pl.BlockSpec((1, tr, C), lambda k, i, c: (k, i, 0))
    grid_spec = pltpu.PrefetchScalarGridSpec(
        num_scalar_prefetch=1, grid=(4, R // tr),
        in_specs=[pl.BlockSpec((1, tr, C), lambda k, i, c: (2 * k + c[0], i, 0)), blk()],
        out_specs=[blk(), blk()] if narrow else [blk()])
    outs = pl.pallas_call(
        body, name=name, grid_spec=grid_spec,
        out_shape=[_S((4, R, C), f32)] + ([_S((4, R, C), wire_dtype)] if narrow else []),
        compiler_params=_cp("parallel", "parallel"),
    )(core, g.reshape(N_DEV, R, C), theirs.reshape(4, R, C))
    part = outs[0].reshape(shp)
    return part, (outs[1].reshape(shp) if narrow else part)


def add_chips(a, theirs, chip, name):
    _, R, C = a.shape
    tr = _row_tile(R, C)

    def body(chip_ref, a_ref, t_ref, out_ref):
        out_ref[...] = ((a_ref[0] + t_ref[0]) + t_ref[1]) + t_ref[2]

    grid_spec = pltpu.PrefetchScalarGridSpec(
        num_scalar_prefetch=1, grid=(R // tr,),
        in_specs=[pl.BlockSpec((1, tr, C), lambda i, c: (c[0], i, 0)), pl.BlockSpec((3, tr, C), lambda i, c: (0, i, 0))],
        out_specs=pl.BlockSpec((tr, C), lambda i, c: (i, 0)))
    return pl.pallas_call(
        body, name=name, grid_spec=grid_spec, out_shape=_S((R, C), a.dtype), compiler_params=_cp("parallel"),
    )(chip, a, theirs)


def _round_up(n, q):
    return (n + q - 1) // q * q


def _lane_rows(a):
    flat = a.reshape(-1)
    n = _round_up(flat.shape[0], SUBLANES * LANES)
    return jnp.pad(flat, (0, n - flat.shape[0])).reshape(-1, LANES)


def _full_to_shards(full, axis):
    shp = full.shape
    s = shp[axis] // N_DEV
    cut = full.reshape(shp[:axis] + (N_DEV, s) + shp[axis + 1:])
    return jnp.moveaxis(cut, axis, 0)


def _shards_to_full(parts, axis):
    shp = list(parts.shape[1:])
    shp[axis] *= N_DEV
    return jnp.moveaxis(parts, 0, axis).reshape(tuple(shp))


def kernel(x, p, g_pre, w_in, s5_a_re, s5_a_im, s5_log_dt, s5_b_re, s5_b_im, s5_c_re, s5_c_im, s5_d, w_glu, w_bs, conv_w, conv_b, lru_w_a, lru_b_a, lru_w_x, lru_b_x, lru_lambda, w_bl, w_out, g_post, w_ple, w_ple_gate, loss_target, m_g_pre, m_w_in, m_s5_a_re, m_s5_a_im, m_s5_log_dt, m_s5_b_re, m_s5_b_im, m_s5_c_re, m_s5_c_im, m_s5_d, m_w_glu, m_w_bs, m_conv_w, m_conv_b, m_lru_w_a, m_lru_b_a, m_lru_w_x, m_lru_b_x, m_lru_lambda, m_w_bl, m_w_out, m_g_post, m_w_ple, m_w_ple_gate, v_g_pre, v_w_in, v_s5_a_re, v_s5_a_im, v_s5_log_dt, v_s5_b_re, v_s5_b_im, v_s5_c_re, v_s5_c_im, v_s5_d, v_w_glu, v_w_bs, v_conv_w, v_conv_b, v_lru_w_a, v_lru_b_a, v_lru_w_x, v_lru_b_x, v_lru_lambda, v_w_bl, v_w_out, v_g_post, v_w_ple, v_w_ple_gate):
    given = dict(locals())
    W = {n: given[n] for n in WEIGHTS}
    M = {n: given["m_" + n] for n in WEIGHTS}
    V = {n: given["v_" + n] for n in WEIGHTS}
    xs, target = to_scan_order(x[0]), to_scan_order(loss_target[0])
    ps = [to_scan_order(p[i, 0]) for i in range(DEPTH)]

    mx, my, mc = _place()
    core = jnp.reshape(mc, (1,)).astype(jnp.int32)
    chip = jnp.reshape(2 * mx + my, (1,)).astype(jnp.int32)

    names = list(SHARDED)
    conv_rows = PAIR - CONV_WIDTH

    def layer_shards(i):
        return [W[n][i].astype(bf16) if n in GATHER_BF16 else jnp.pad(W[n][i], ((0, conv_rows), (0, 0))) for n in names]

    def layer_weights(i, gathered):
        full = {n: _shards_to_full(g if n in GATHER_BF16 else g[:, :CONV_WIDTH], SHARDED[n] - 1)
                for n, g in zip(names, gathered)}
        return {n: (full[n] if n in SHARDED else W[n][i]) for n in WEIGHTS}

    act, saved, weights = xs, [], []
    gathered = all_gather(layer_shards(0), "comm_gather_weights")
    for i in range(DEPTH):
        weights.append(layer_weights(i, gathered))
        act, sv, gathered = layer_fwd(act, ps[i], weights[i], "_l%d" % i, layer_shards(i + 1) if i + 1 < DEPTH else ())
        saved.append(sv)
    loss_part, gact = loss_head(act, target, "loss_head")
    loss = lax.psum(loss_part[0, 0], ("x", "y", "c"))

    def sibling_sums(i, g):
        rep_rows = [_lane_rows(g[n].reshape(W[n].shape[1:])) for n in REPLICATED]
        n_rows = sum(r.shape[0] for r in rep_rows)
        pad_rows = _round_up(n_rows, N_DEV * SUBLANES) - n_rows
        rep_blocks = jnp.concatenate(rep_rows + [jnp.zeros((pad_rows, LANES), f32)]).reshape(N_DEV, -1, LANES)
        blocks = [_full_to_shards(g[n].reshape(weights[i][n].shape), SHARDED[n] - 1) for n in names] + [rep_blocks]
        theirs = exchange_sibling(blocks, "comm_reduce_sibling_l%d" % i)
        parts, wire = [], []
        for k, (b, t) in enumerate(zip(blocks, theirs)):
            part, sent = add_sibling(b, t, core, "reduce_add_sibling_%d_l%d" % (k, i), bf16 if k < len(names) else f32)
            parts.append(part)
            wire.append(sent)
        return parts, wire, [r.shape[0] for r in rep_rows]

    parts, others, rep_sizes, wire = [None] * DEPTH, [None] * DEPTH, None, ()
    for i in reversed(range(DEPTH)):
        gact, g, exchanged = layer_bwd(gact, weights[i], saved[i], "_l%d" % i, wire)
        if i + 1 < DEPTH:
            others[i + 1] = exchanged
        parts[i], wire, rep_sizes = sibling_sums(i, g)
    others[0] = exchange_chips(wire, "comm_reduce_chips")

    red, deltas, new_m, new_v = {}, {}, {}, {}
    for k, n in enumerate(names):
        red[n], deltas[n], new_m[n], new_v[n] = adamw_reduce(
            W[n], [parts[i][k] for i in range(DEPTH)], [others[i][k] for i in range(DEPTH)], M[n], V[n], chip, "adamw_" + n)
    pieces = [add_chips(parts[i][-1], others[i][-1], chip, "reduce_add_chips_l%d" % i) for i in range(DEPTH)]
    rep_all = [r.reshape(-1, LANES) for r in all_gather(pieces, "comm_gather_replicated")]
    off = 0
    for n, rows in zip(REPLICATED, rep_sizes):
        k = math.prod(W[n].shape[1:])
        red[n] = jnp.stack([rep_all[i][off:off + rows].reshape(-1)[:k] for i in range(DEPTH)]).reshape(W[n].shape)
        off += rows
        deltas[n], new_m[n], new_v[n] = adamw(W[n], red[n], M[n], V[n], "adamw_" + n)
    return (loss, from_scan_order(gact)[None], *[red[n] for n in WEIGHTS], *[deltas[n] for n in WEIGHTS],
            *[new_m[n] for n in WEIGHTS], *[new_v[n] for n in WEIGHTS])
```

```python
import math

import jax
import jax.numpy as jnp
from jax import lax
from jax.experimental import pallas as pl
from jax.experimental.pallas import tpu as pltpu

f32 = jnp.float32
bf16 = jnp.bfloat16

D_MODEL = 1024
DEPTH = 2
PLE_DIM = 256
NORM_EPS = 1e-6
S5_WIDTH = 512
S5_GROUP = 16
S5_GROUPS = 32
S5_STATE = 64
S5_NS = S5_GROUPS * S5_STATE
LRU_WIDTH = 1280
LRU_HEADS = 10
LRU_HEAD_DIM = 128
LRU_C = 8.0
CONV_WIDTH = 4
N_DEV = 8

ADAM_LR = 0.001
ADAM_B1 = 0.9
ADAM_B2 = 0.999
ADAM_EPS = 1e-08
ADAM_WD = 0.01
ADAM_STEP = 10

LANES = 128
SUBLANES = 8
SEGS = SUBLANES
SCAN_CHUNK = 256
TOKEN_TILE = 256
TOKEN_TILE_LIGHT = 512
MM_TILE_M = 1024
MM_TILE_N = 1408
MM_TILE_K_ROWS = 1280
PAIR = 2 * SUBLANES
VMEM_LIMIT_BYTES = 56 * 1024 * 1024
ELEMENTWISE_BLOCK_BYTES = 1024 * 1024

WEIGHTS = ['g_pre', 'w_in', 's5_a_re', 's5_a_im', 's5_log_dt', 's5_b_re', 's5_b_im', 's5_c_re', 's5_c_im',
           's5_d', 'w_glu', 'w_bs', 'conv_w', 'conv_b', 'lru_w_a', 'lru_b_a', 'lru_w_x', 'lru_b_x',
           'lru_lambda', 'w_bl', 'w_out', 'g_post', 'w_ple', 'w_ple_gate']
SHARDED = {'w_in': 2, 'w_glu': 2, 'w_bs': 2, 'conv_w': 2, 'w_bl': 1, 'w_out': 1, 'w_ple': 2, 'w_ple_gate': 1}
GATHER_BF16 = ['w_in', 'w_glu', 'w_bs', 'w_bl', 'w_out', 'w_ple', 'w_ple_gate']
REPLICATED = [n for n in WEIGHTS if n not in SHARDED]


def _sig(x):
    return 0.5 * jnp.tanh(0.5 * x) + 0.5


def _gelu_parts(x):
    k = math.sqrt(2.0 / math.pi)
    t = jnp.tanh(k * (x + 0.044715 * x * x * x))
    return t, k


def _gelu(x):
    t, _ = _gelu_parts(x)
    return 0.5 * x * (1.0 + t)


def _gelu_grad(x):
    t, k = _gelu_parts(x)
    return 0.5 * (1.0 + t) + 0.5 * x * (1.0 - t * t) * k * (1.0 + 3.0 * 0.044715 * x * x)


def _one_minus_sq(a, log_a):
    z = 2.0 * log_a
    series = -z * (1.0 + z * (0.5 + z * (1.0 / 6.0 + z * (1.0 / 24.0 + z * (1.0 / 120.0)))))
    return jnp.where(z > -0.05, series, 1.0 - a * a)


def _softplus_neg(lam):
    return jnp.maximum(-lam, 0.0) + jnp.log(1.0 + jnp.exp(-jnp.abs(lam)))


def _dot(a, b):
    return jnp.dot(a, b, preferred_element_type=f32)


def _dot_nt(a, b):
    return lax.dot_general(a, b, (((1,), (1,)), ((), ())), preferred_element_type=f32)


def _dot_tn(a, b):
    return lax.dot_general(a, b, (((0,), (0,)), ((), ())), preferred_element_type=f32)


def _S(shape, dtype=f32):
    return jax.ShapeDtypeStruct(shape, dtype)


def _full(shape):
    nd = len(shape)
    return pl.BlockSpec(shape, lambda *_: (0,) * nd)


def _rows(tile, width, col=0):
    return pl.BlockSpec((tile, width), lambda i: (i, col))


def _cp(*semantics):
    return pltpu.CompilerParams(dimension_semantics=semantics or None, vmem_limit_bytes=VMEM_LIMIT_BYTES)


def _tile(n, want):
    t = min(n, want)
    assert n % t == 0, (n, want)
    return t


def _row_tile(R, C=LANES):
    cap = max(SUBLANES, min(R, ELEMENTWISE_BLOCK_BYTES // (4 * C)))
    for t in range(cap - cap % SUBLANES, 0, -SUBLANES):
        if R % t == 0:
            return t
    return R


def _lanes(j):
    return slice(LANES * j, LANES * (j + 1))


def _step_rows(k, n=SUBLANES):
    return pl.ds(pl.multiple_of(k * n, n), n)


def to_scan_order(a):
    T, C = a.shape
    tc = _tile(T, SCAN_CHUNK)
    return a.reshape(T // tc, SEGS, tc // SEGS, C).transpose(0, 2, 1, 3).reshape(T, C)


def from_scan_order(a):
    T, C = a.shape
    tc = _tile(T, SCAN_CHUNK)
    return a.reshape(T // tc, tc // SEGS, SEGS, C).transpose(0, 2, 1, 3).reshape(T, C)


def _col_tile(n, cap):
    if n <= cap:
        return n
    for t in range(cap - cap % LANES, 0, -LANES):
        if n % t == 0:
            return t
    return n


def _resident(shape):
    nd = len(shape)
    return pl.BlockSpec(shape, lambda *_: (0,) * nd, pipeline_mode=pl.Buffered(1))


def mm_nn(a, b, name, out_dtype=f32):
    M, K = a.shape
    N = b.shape[1]
    tm, tn = _tile(M, MM_TILE_M), _col_tile(N, MM_TILE_N)

    def body(a_ref, b_ref, o_ref):
        o_ref[...] = _dot(a_ref[...].astype(bf16), b_ref[...].astype(bf16)).astype(out_dtype)

    return pl.pallas_call(
        body, name=name, grid=(M // tm, N // tn),
        in_specs=[pl.BlockSpec((tm, K), lambda i, j: (i, 0)), pl.BlockSpec((K, tn), lambda i, j: (0, j))],
        out_specs=pl.BlockSpec((tm, tn), lambda i, j: (i, j)),
        out_shape=_S((M, N), out_dtype), compiler_params=_cp("parallel", "parallel"),
    )(a, b)


def mm_tn(a, b, name):
    M, K = a.shape
    N = b.shape[1]
    tm, tk, tn = _tile(M, MM_TILE_M), _col_tile(K, MM_TILE_K_ROWS), _col_tile(N, MM_TILE_N)

    def body(a_ref, b_ref, o_ref):
        m = pl.program_id(2)
        part = _dot_tn(a_ref[...].astype(bf16), b_ref[...].astype(bf16))

        @pl.when(m == 0)
        def _():
            o_ref[...] = part

        @pl.when(m > 0)
        def _():
            o_ref[...] += part

    return pl.pallas_call(
        body, name=name, grid=(K // tk, N // tn, M // tm),
        in_specs=[pl.BlockSpec((tm, tk), lambda i, j, m: (m, i)), pl.BlockSpec((tm, tn), lambda i, j, m: (m, j))],
        out_specs=pl.BlockSpec((tk, tn), lambda i, j, m: (i, j)),
        out_shape=_S((K, N), f32),
        compiler_params=_cp("parallel", "parallel", "arbitrary"),
    )(a, b)


def rms_fwd(x, g, name):
    T = x.shape[0]
    tm = _tile(T, TOKEN_TILE_LIGHT)

    def body(x_ref, g_ref, h_ref):
        xv = x_ref[...]
        r = lax.rsqrt(jnp.mean(xv * xv, axis=-1, keepdims=True) + NORM_EPS)
        h_ref[...] = (xv * r * g_ref[...]).astype(bf16)

    return pl.pallas_call(
        body, name=name, grid=(T // tm,),
        in_specs=[_rows(tm, D_MODEL), _full((1, D_MODEL))], out_specs=_rows(tm, D_MODEL),
        out_shape=_S((T, D_MODEL), bf16), compiler_params=_cp("parallel"),
    )(x, g)


def _s5_discretise(a_re, a_im, log_dt, b_re_t, b_im_t):
    dt = jnp.exp(log_dt)
    mag = jnp.exp(a_re * dt)
    ab_re = mag * jnp.cos(a_im * dt)
    ab_im = mag * jnp.sin(a_im * dt)
    den = a_re * a_re + a_im * a_im
    nr, ni = ab_re - 1.0, ab_im
    z_re = (nr * a_re + ni * a_im) / den
    z_im = (ni * a_re - nr * a_im) / den
    bb_re = z_re[None] * b_re_t - z_im[None] * b_im_t
    bb_im = z_re[None] * b_im_t + z_im[None] * b_re_t
    return ab_re, ab_im, bb_re, bb_im


def s5_prep(a_re, a_im, log_dt, b_re_t, b_im_t, m, name):
    G, N = a_re.shape

    def body(are_ref, aim_ref, ldt_ref, bre_ref, bim_ref, ab_ref, pw_ref, bb_ref):
        are, aim, ldt = are_ref[...], aim_ref[...], ldt_ref[...]
        ab_re, ab_im, bb_re, bb_im = _s5_discretise(are, aim, ldt, bre_ref[...], bim_ref[...])
        ab_ref[0], ab_ref[1] = ab_re, ab_im
        bb_ref[0], bb_ref[1] = bb_re, bb_im
        dt = jnp.exp(ldt)
        for k in range(m):
            mag = jnp.exp(are * dt * (k + 1.0))
            pw_ref[0, k] = mag * jnp.cos(aim * dt * (k + 1.0))
            pw_ref[1, k] = mag * jnp.sin(aim * dt * (k + 1.0))

    return pl.pallas_call(
        body, name=name,
        out_shape=[_S((2, G, N)), _S((2, m, G, N)), _S((2, S5_GROUP, G, N))], compiler_params=_cp(),
    )(a_re, a_im, log_dt, b_re_t, b_im_t)


def s5_prep_bwd(a_re, a_im, log_dt, b_re_t, b_im_t, g_ab, g_bb, name):
    G, N = a_re.shape

    def body(are_ref, aim_ref, ldt_ref, bre_ref, bim_ref, gab_ref, gbb_ref, o_are, o_aim, o_ldt, o_bre, o_bim):
        _, vjp = jax.vjp(_s5_discretise, are_ref[...], aim_ref[...], ldt_ref[...], bre_ref[...], bim_ref[...])
        g_are, g_aim, g_ldt, g_bre, g_bim = vjp((gab_ref[0], gab_ref[1], gbb_ref[0], gbb_ref[1]))
        o_are[...], o_aim[...], o_ldt[...], o_bre[...], o_bim[...] = g_are, g_aim, g_ldt, g_bre, g_bim

    return pl.pallas_call(
        body, name=name,
        out_shape=[_S((G, N)), _S((G, N)), _S((G, 1)), _S((S5_GROUP, G, N)), _S((S5_GROUP, G, N))],
        compiler_params=_cp(),
    )(a_re, a_im, log_dt, b_re_t, b_im_t, g_ab, g_bb)


NB_S5 = S5_NS // LANES
CB_S5 = S5_WIDTH // LANES
SB_PER_CB = NB_S5 // CB_S5
GRP_PER_SB = LANES // S5_STATE
S5_JB = 8


def _bdb_mask():
    j = jnp.arange(NB_S5)
    own_rows = (j[:, None] % SB_PER_CB == jnp.arange(SB_PER_CB)[None, :]).astype(f32)
    eye = jnp.eye(GRP_PER_SB, dtype=f32)
    return own_rows[:, :, None, None, None, None, None] * eye[None, None, :, None, None, :, None]


def _pack_bdb(bb):
    v = jnp.transpose(bb.reshape(2, S5_GROUP, NB_S5, GRP_PER_SB, S5_STATE), (2, 3, 1, 0, 4))
    full = v[:, None, :, :, :, None, :] * _bdb_mask()
    return full.reshape(NB_S5, LANES, 2 * LANES)


def _unpack_bdb(g_bdb):
    g7 = g_bdb.reshape(NB_S5, SB_PER_CB, GRP_PER_SB, S5_GROUP, 2, GRP_PER_SB, S5_STATE)
    v = jnp.sum(g7 * _bdb_mask(), axis=(1, 5))
    return jnp.transpose(v, (3, 2, 0, 1, 4)).reshape(2, S5_GROUP, S5_GROUPS, S5_STATE)


def _pack_cdb(c_re, c_im):
    gl = S5_GROUPS // CB_S5
    c2 = jnp.stack([c_re, -c_im]).reshape(2, CB_S5, gl, S5_GROUP, S5_STATE)
    eye = jnp.eye(gl, dtype=f32)
    full = jnp.transpose(c2, (1, 0, 2, 4, 3))[:, :, :, :, None, :] * eye[None, None, :, None, :, None]
    return full.reshape(CB_S5, 2 * SB_PER_CB * LANES, LANES)


def _unpack_cdb(g_cdb):
    gl = S5_GROUPS // CB_S5
    g6 = g_cdb.reshape(CB_S5, 2, gl, S5_STATE, gl, S5_GROUP)
    eye = jnp.eye(gl, dtype=f32)
    v = jnp.sum(g6 * eye[None, None, :, None, :, None], axis=4)
    v = jnp.transpose(v, (1, 0, 2, 4, 3)).reshape(2, S5_GROUPS, S5_GROUP, S5_STATE)
    return v[0], -v[1]


def _state_cat(ref, c):
    w = SB_PER_CB * LANES
    return jnp.concatenate([ref[:, w * c:w * (c + 1)], ref[:, S5_NS + w * c:S5_NS + w * (c + 1)]], axis=1)


def _state_pair(ref, j):
    return jnp.concatenate([ref[:, _lanes(j)], ref[:, S5_NS + LANES * j:S5_NS + LANES * (j + 1)]], axis=1)


def s5_fwd(usg, bdb, cdb, dvec, abar_b, ptab_b, name, gather=()):
    T = usg.shape[0]
    tc = _tile(T, SCAN_CHUNK)
    m = tc // SEGS
    nsteps = T // tc
    ng = len(gather)
    assert ptab_b.shape == (2, m, SEGS, S5_NS) and m % 2 == 0

    def body(*refs):
        u_ref, bdb_ref, cdb_ref, d_ref, a_ref, p_ref = refs[:6]
        ys_ref, sre_ref, sim_ref, sbf_ref = refs[6 + ng:10 + ng]
        src_re, src_im, dst_re, dst_im, cin_ref, carry_ref = refs[10 + 2 * ng:16 + 2 * ng]
        i = pl.program_id(0)
        if ng:
            phases = _gather_phases([s.shape for s in gather], refs[6:6 + ng], refs[10 + ng:10 + 2 * ng],
                                    *refs[16 + 2 * ng:])
            for phase, step in zip(phases, (0, nsteps // 2, (3 * nsteps) // 4, nsteps - 1)):
                pl.when(i == step)(phase)

        @pl.when(i == 0)
        def _():
            carry_ref[...] = jnp.zeros_like(carry_ref)

        u = u_ref[...]
        ub = u.astype(bf16)
        for j in range(NB_S5):
            bu = _dot(ub[:, _lanes(j // SB_PER_CB)], bdb_ref[j])
            src_re[:, _lanes(j)] = bu[:, :LANES]
            src_im[:, _lanes(j)] = bu[:, LANES:]
        for j0 in range(0, NB_S5, S5_JB):
            def kstep(k, st):
                rows = _step_rows(k)
                out = []
                for q in range(S5_JB):
                    ln = _lanes(j0 + q)
                    sr, si = st[2 * q], st[2 * q + 1]
                    ar, ai = a_ref[0, :, ln], a_ref[1, :, ln]
                    nr = ar * sr - ai * si + src_re[rows, ln]
                    ni = ar * si + ai * sr + src_im[rows, ln]
                    dst_re[rows, ln] = nr
                    dst_im[rows, ln] = ni
                    out += [nr, ni]
                return tuple(out)

            ends = lax.fori_loop(0, m, kstep, tuple(jnp.zeros((SEGS, LANES), f32) for _ in range(2 * S5_JB)))
            for q in range(S5_JB):
                ln = _lanes(j0 + q)
                er, ei = ends[2 * q], ends[2 * q + 1]
                cr, ci = carry_ref[0, :, ln], carry_ref[1, :, ln]
                amr, ami = p_ref[0, m - 1, 0:1, ln], p_ref[1, m - 1, 0:1, ln]
                rows_r, rows_i = [], []
                for s in range(SEGS):
                    rows_r.append(cr)
                    rows_i.append(ci)
                    cr, ci = (er[s:s + 1, :] + amr * cr - ami * ci, ei[s:s + 1, :] + amr * ci + ami * cr)
                cin_ref[0, 0:SEGS, ln] = _stack_rows(rows_r)
                cin_ref[1, 0:SEGS, ln] = _stack_rows(rows_i)
                carry_ref[0, :, ln] = cr
                carry_ref[1, :, ln] = ci
        cin_ref[:, SEGS:, :] = cin_ref[:, 0:SEGS, :]

        def fix(k2, _):
            rows = _step_rows(k2, PAIR)
            pr = p_ref[0, pl.ds(2 * k2, 2)].reshape(PAIR, S5_NS)
            pi = p_ref[1, pl.ds(2 * k2, 2)].reshape(PAIR, S5_NS)
            cr, ci = cin_ref[0], cin_ref[1]
            sr = dst_re[rows, :] + pr * cr - pi * ci
            si = dst_im[rows, :] + pr * ci + pi * cr
            sre_ref[rows, :] = sr
            sim_ref[rows, :] = si
            sbf_ref[rows, 0:S5_NS] = sr.astype(bf16)
            sbf_ref[rows, S5_NS:] = si.astype(bf16)
            return 0

        lax.fori_loop(0, m // 2, fix, 0)
        for c in range(CB_S5):
            ys_ref[:, _lanes(c)] = _dot(_state_cat(sbf_ref, c), cdb_ref[c]) + d_ref[:, _lanes(c)] * u[:, _lanes(c)]

    st = lambda w: _rows(tc, w)
    outs = pl.pallas_call(
        body, name=name, grid=(nsteps,),
        in_specs=[_rows(tc, S5_WIDTH, 0), _resident(bdb.shape), _resident(cdb.shape), _full((1, S5_WIDTH)),
                  _resident((2, SEGS, S5_NS)), _resident((2, m, SEGS, S5_NS))] + [ANY] * ng,
        out_specs=[st(S5_WIDTH), st(S5_NS), st(S5_NS), st(2 * S5_NS)] + [ANY] * ng,
        out_shape=[_S((T, S5_WIDTH)), _S((T, S5_NS)), _S((T, S5_NS)), _S((T, 2 * S5_NS), bf16)] + (
            _gather_out_shapes(gather) if ng else []),
        scratch_shapes=[pltpu.VMEM((tc, S5_NS), f32)] * 4 + [pltpu.VMEM((2, PAIR, S5_NS), f32),
                                                             pltpu.VMEM((2, 1, S5_NS), f32)] + (
            _gather_semaphores(ng) if ng else []),
        compiler_params=_cp("arbitrary"),
    )(usg, bdb, cdb, dvec, abar_b, ptab_b, *gather)
    return outs[:4], list(outs[4:])


def s5_bwd(gys, usg, s_re, s_im, s_bf, bdb, cdb, dvec, abar_b, ptab_rev_b, name):
    T = gys.shape[0]
    tc = _tile(T, SCAN_CHUNK)
    m = tc // SEGS
    nch = T // tc
    hb = tc // SUBLANES

    def body(gy_ref, u_ref, sre_ref, sim_ref, hre_ref, him_ref, sbf_ref, bdb_ref, cdb_ref, d_ref, a_ref, p_ref,
             gu_ref, gab_ref, gd_ref, gbdb_ref, gcdb_ref,
             src_re, src_im, dst_re, dst_im, lam_ref, cin_ref, acc_ref, carry_ref):
        i = pl.program_id(0)

        @pl.when(i == 0)
        def _():
            carry_ref[...] = jnp.zeros_like(carry_ref)
            for ref in (gab_ref, gd_ref, gbdb_ref, gcdb_ref):
                ref[...] = jnp.zeros_like(ref)

        first = i == nch - 1
        gy = gy_ref[...]
        gyb = gy.astype(bf16)
        u = u_ref[...]
        ub = u.astype(bf16)
        w = SB_PER_CB * LANES
        for c in range(CB_S5):
            gs = _dot_nt(gyb[:, _lanes(c)], cdb_ref[c])
            src_re[:, w * c:w * (c + 1)] = gs[:, :w]
            src_im[:, w * c:w * (c + 1)] = gs[:, w:]
            gcdb_ref[c] += _dot_tn(_state_cat(sbf_ref, c), gyb[:, _lanes(c)])
        for j0 in range(0, NB_S5, S5_JB):
            def kstep(kk, st):
                rows = _step_rows(m - 1 - kk)
                out = []
                for q in range(S5_JB):
                    ln = _lanes(j0 + q)
                    lr, li = st[2 * q], st[2 * q + 1]
                    ar, ai = a_ref[0, :, ln], a_ref[1, :, ln]
                    nr = ar * lr + ai * li + src_re[rows, ln]
                    ni = ar * li - ai * lr + src_im[rows, ln]
                    dst_re[rows, ln] = nr
                    dst_im[rows, ln] = ni
                    out += [nr, ni]
                return tuple(out)

            ends = lax.fori_loop(0, m, kstep, tuple(jnp.zeros((SEGS, LANES), f32) for _ in range(2 * S5_JB)))
            for q in range(S5_JB):
                ln = _lanes(j0 + q)
                er, ei = ends[2 * q], ends[2 * q + 1]
                cr, ci = carry_ref[0, :, ln], carry_ref[1, :, ln]
                amr, ami = p_ref[0, 0, 0:1, ln], p_ref[1, 0, 0:1, ln]
                rows_r, rows_i = [None] * SEGS, [None] * SEGS
                for s in reversed(range(SEGS)):
                    rows_r[s], rows_i[s] = cr, ci
                    cr, ci = (er[s:s + 1, :] + amr * cr + ami * ci, ei[s:s + 1, :] + amr * ci - ami * cr)
                cin_ref[0, 0:SEGS, ln] = _stack_rows(rows_r)
                cin_ref[1, 0:SEGS, ln] = _stack_rows(rows_i)
                carry_ref[0, :, ln] = cr
                carry_ref[1, :, ln] = ci
        cin_ref[:, SEGS:, :] = cin_ref[:, 0:SEGS, :]
        acc_ref[...] = jnp.zeros_like(acc_ref)

        def fix_rows(rows, k2, prev_re, prev_im):
            pr = p_ref[0, pl.ds(2 * k2, 2)].reshape(PAIR, S5_NS)
            pi = p_ref[1, pl.ds(2 * k2, 2)].reshape(PAIR, S5_NS)
            cr, ci = cin_ref[0], cin_ref[1]
            lr = dst_re[rows, :] + pr * cr + pi * ci
            li = dst_im[rows, :] + pr * ci - pi * cr
            lam_ref[rows, 0:S5_NS] = lr.astype(bf16)
            lam_ref[rows, S5_NS:] = li.astype(bf16)
            acc_ref[0] += lr * prev_re + li * prev_im
            acc_ref[1] += li * prev_re - lr * prev_im

        last = slice(tc - SUBLANES, tc)
        wrap_re = _down_a_segment(sre_ref[last, :], jnp.where(first, 0.0, hre_ref[SUBLANES - 1:SUBLANES, :]))
        wrap_im = _down_a_segment(sim_ref[last, :], jnp.where(first, 0.0, him_ref[SUBLANES - 1:SUBLANES, :]))
        fix_rows(pl.ds(0, PAIR), 0, jnp.concatenate([wrap_re, sre_ref[0:SUBLANES, :]], axis=0),
                 jnp.concatenate([wrap_im, sim_ref[0:SUBLANES, :]], axis=0))

        def fix(k2, _):
            prev = pl.ds(pl.multiple_of(k2 * PAIR - SUBLANES, SUBLANES), PAIR)
            fix_rows(_step_rows(k2, PAIR), k2, sre_ref[prev, :], sim_ref[prev, :])
            return 0

        lax.fori_loop(1, m // 2, fix, 0)
        gab_ref[0] += jnp.sum(acc_ref[0], axis=0, keepdims=True)
        gab_ref[1] += jnp.sum(acc_ref[1], axis=0, keepdims=True)
        for c in range(CB_S5):
            x = gy[:, _lanes(c)] * d_ref[:, _lanes(c)]
            for j in range(SB_PER_CB * c, SB_PER_CB * (c + 1)):
                pair = _state_pair(lam_ref, j)
                x = x + _dot_nt(pair, bdb_ref[j])
                gbdb_ref[j] += _dot_tn(ub[:, _lanes(c)], pair)
            gu_ref[:, _lanes(c)] = x.astype(bf16)
        gd_ref[...] += jnp.sum(gy * u, axis=0, keepdims=True)

    rev = lambda i: (nch - 1 - i, 0)
    halo = lambda i: (jnp.maximum((nch - 1 - i) * hb - 1, 0), 0)
    blk = lambda wd: pl.BlockSpec((tc, wd), rev)
    return pl.pallas_call(
        body, name=name, grid=(nch,),
        in_specs=[blk(S5_WIDTH), blk(S5_WIDTH), blk(S5_NS), blk(S5_NS),
                  pl.BlockSpec((SUBLANES, S5_NS), halo), pl.BlockSpec((SUBLANES, S5_NS), halo), blk(2 * S5_NS),
                  _resident(bdb.shape), _resident(cdb.shape), _full((1, S5_WIDTH)),
                  _resident((2, SEGS, S5_NS)), _resident((2, m, SEGS, S5_NS))],
        out_specs=[blk(S5_WIDTH), _full((2, 1, S5_NS)), _full((1, S5_WIDTH)), _full(bdb.shape), _full(cdb.shape)],
        out_shape=[_S((T, S5_WIDTH), bf16), _S((2, 1, S5_NS)), _S((1, S5_WIDTH)), _S(bdb.shape), _S(cdb.shape)],
        scratch_shapes=[pltpu.VMEM((tc, S5_NS), f32)] * 4 + [
            pltpu.VMEM((tc, 2 * S5_NS), bf16), pltpu.VMEM((2, PAIR, S5_NS), f32), pltpu.VMEM((2, PAIR, S5_NS), f32),
            pltpu.VMEM((2, 1, S5_NS), f32)],
        compiler_params=_cp("arbitrary"),
    )(gys, usg, s_re, s_im, s_re, s_im, s_bf, bdb, cdb, dvec, abar_b, ptab_rev_b)


def s5_post_fwd(ys, usg, wglu, wbs, name):
    T = ys.shape[0]
    tm = _tile(T, TOKEN_TILE_LIGHT)

    def body(ys_ref, sg_ref, wglu_ref, wbs_ref, glu_ref, zs_ref):
        glu = _dot(_gelu(ys_ref[...]).astype(bf16), wglu_ref[...])
        sg = sg_ref[...]
        y2 = glu[:, :S5_WIDTH] * _sig(glu[:, S5_WIDTH:]) * (sg * _sig(sg))
        glu_ref[...] = glu
        zs_ref[...] = _dot(y2.astype(bf16), wbs_ref[...])

    return pl.pallas_call(
        body, name=name, grid=(T // tm,),
        in_specs=[_rows(tm, S5_WIDTH), _rows(tm, S5_WIDTH, 1), _resident((S5_WIDTH, 2 * S5_WIDTH)),
                  _resident((S5_WIDTH, D_MODEL))],
        out_specs=[_rows(tm, 2 * S5_WIDTH), _rows(tm, D_MODEL)],
        out_shape=[_S((T, 2 * S5_WIDTH)), _S((T, D_MODEL))], compiler_params=_cp("parallel"),
    )(ys, usg, wglu, wbs)


def _accumulate(ref, part, step):
    @pl.when(step == 0)
    def _():
        ref[...] = part

    @pl.when(step > 0)
    def _():
        ref[...] += part


def s5_post_bwd(gzs, glu, usg, ys, wbs, wglu, name):
    T = ys.shape[0]
    tm = _tile(T, TOKEN_TILE_LIGHT)

    def body(gzs_ref, glu_ref, sg_ref, ys_ref, wbs_ref, wglu_ref, gys_ref, gsg_ref, gwbs_ref, gwglu_ref):
        i = pl.program_id(0)
        glu = glu_ref[...]
        a, b = glu[:, :S5_WIDTH], glu[:, S5_WIDTH:]
        sg = sg_ref[...]
        ys = ys_ref[...]
        sb, ssg = _sig(b), _sig(sg)
        silu = sg * ssg
        _accumulate(gwbs_ref, _dot_tn((a * sb * silu).astype(bf16), gzs_ref[...]), i)
        gy2 = _dot_nt(gzs_ref[...], wbs_ref[...])
        g_a = gy2 * sb * silu
        g_b = gy2 * a * sb * (1.0 - sb) * silu
        gsg_ref[...] = (gy2 * a * sb * ssg * (1.0 + sg * (1.0 - ssg))).astype(bf16)
        gglu = jnp.concatenate([g_a, g_b], axis=1).astype(bf16)
        _accumulate(gwglu_ref, _dot_tn(_gelu(ys).astype(bf16), gglu), i)
        gys_ref[...] = _dot_nt(gglu, wglu_ref[...]) * _gelu_grad(ys)

    return pl.pallas_call(
        body, name=name, grid=(T // tm,),
        in_specs=[_rows(tm, D_MODEL), _rows(tm, 2 * S5_WIDTH), _rows(tm, S5_WIDTH, 1), _rows(tm, S5_WIDTH),
                  _resident((S5_WIDTH, D_MODEL)), _resident((S5_WIDTH, 2 * S5_WIDTH))],
        out_specs=[_rows(tm, S5_WIDTH), _rows(tm, S5_WIDTH), _full((S5_WIDTH, D_MODEL)), _full((S5_WIDTH, 2 * S5_WIDTH))],
        out_shape=[_S((T, S5_WIDTH)), _S((T, S5_WIDTH), bf16), _S((S5_WIDTH, D_MODEL)), _S((S5_WIDTH, 2 * S5_WIDTH))],
        compiler_params=_cp("arbitrary"),
    )(gzs, glu, usg, ys, wbs, wglu)


NB_LRU = LRU_WIDTH // LANES
LRU_JB = 5
TAPS_BACK = CONV_WIDTH - 1
EDGE = TAPS_BACK * SUBLANES
HALO_ROWS = 4 * SUBLANES


def _down_a_segment(blk, entering_row):
    sub = lax.broadcasted_iota(jnp.int32, blk.shape, 0)
    return jnp.where(sub == 0, entering_row, pltpu.roll(blk, 1, 0))


def _up_a_segment(blk, entering_row):
    sub = lax.broadcasted_iota(jnp.int32, blk.shape, 0)
    return jnp.where(sub == SUBLANES - 1, entering_row, pltpu.roll(blk, SUBLANES - 1, 0))


def _stack_rows(rows):
    sub = lax.broadcasted_iota(jnp.int32, (SUBLANES,) + rows[0].shape[1:], 0)
    out = jnp.broadcast_to(rows[0], sub.shape)
    for s in range(1, SUBLANES):
        out = jnp.where(sub == s, rows[s], out)
    return out


def _fill_conv_window(xe, x_ref, xh_ref, is_first, tc):
    xe[EDGE:, :] = x_ref[...]
    for i in range(1, TAPS_BACK + 1):
        row = HALO_ROWS - SUBLANES * i + SUBLANES - 1
        entering = jnp.where(is_first, 0.0, xh_ref[row:row + 1, :])
        blk = x_ref[tc - SUBLANES * i:tc - SUBLANES * (i - 1), :]
        xe[EDGE - SUBLANES * i:EDGE - SUBLANES * (i - 1), :] = _down_a_segment(blk, entering)


def lru_fwd(lx, convw, convb, wa, wx, ba, bx, lam, name):
    T = lx.shape[0]
    tc = _tile(T, SCAN_CHUNK)
    m = tc // SEGS
    hb = tc // HALO_ROWS

    def body(x_ref, xh_ref, cw_ref, cb_ref, wa_ref, wx_ref, ba_ref, bx_ref, lam_ref,
             c_ref, r_ref, i_ref, h_ref, xe, src_a, src_b, dst_a, dst_h, cin_ref, carry_ref):
        i = pl.program_id(0)

        @pl.when(i == 0)
        def _():
            carry_ref[...] = jnp.zeros_like(carry_ref)

        _fill_conv_window(xe, x_ref, xh_ref, i == 0, tc)
        c = cb_ref[...] + cw_ref[0:1, :] * xe[0:tc, :]
        for k in range(1, CONV_WIDTH):
            c = c + cw_ref[k:k + 1, :] * xe[SUBLANES * k:SUBLANES * k + tc, :]
        c_ref[...] = c
        sp = _softplus_neg(lam_ref[...])
        for j in range(NB_LRU):
            ln = _lanes(j)
            cj = c[:, ln]
            cjb = cj.astype(bf16)
            r = _sig(_dot(cjb, wa_ref[j]) + ba_ref[:, ln])
            g = _sig(_dot(cjb, wx_ref[j]) + bx_ref[:, ln])
            r_ref[:, ln] = r
            i_ref[:, ln] = g
            log_a = -LRU_C * r * sp[:, ln]
            a = jnp.exp(log_a)
            src_a[:, ln] = a
            src_b[:, ln] = jnp.sqrt(_one_minus_sq(a, log_a)) * (g * cj)
        for j0 in range(0, NB_LRU, LRU_JB):
            def kstep(k, st):
                rows = _step_rows(k)
                out = []
                for q in range(LRU_JB):
                    ln = _lanes(j0 + q)
                    hh, ac = st[2 * q], st[2 * q + 1]
                    a = src_a[rows, ln]
                    hh = a * hh + src_b[rows, ln]
                    ac = a * ac
                    dst_h[rows, ln] = hh
                    dst_a[rows, ln] = ac
                    out += [hh, ac]
                return tuple(out)

            init = tuple(jnp.zeros((SEGS, LANES), f32) if q % 2 == 0 else jnp.ones((SEGS, LANES), f32)
                         for q in range(2 * LRU_JB))
            ends = lax.fori_loop(0, m, kstep, init)
            for q in range(LRU_JB):
                ln = _lanes(j0 + q)
                eh, ea = ends[2 * q], ends[2 * q + 1]
                cr = carry_ref[:, ln]
                rows_c = []
                for s in range(SEGS):
                    rows_c.append(cr)
                    cr = eh[s:s + 1, :] + ea[s:s + 1, :] * cr
                cin_ref[:, ln] = _stack_rows(rows_c)
                carry_ref[:, ln] = cr

        def fix(k, _):
            rows = _step_rows(k)
            h_ref[rows, :] = dst_h[rows, :] + dst_a[rows, :] * cin_ref[...]
            return 0

        lax.fori_loop(0, m, fix, 0)

    wide = lambda: _rows(tc, LRU_WIDTH)
    buf = lambda rows: pltpu.VMEM((rows, LRU_WIDTH), f32)
    return pl.pallas_call(
        body, name=name, grid=(T // tc,),
        in_specs=[wide(), pl.BlockSpec((HALO_ROWS, LRU_WIDTH), lambda i: (jnp.maximum(i * hb - 1, 0), 0)),
                  _full((CONV_WIDTH, LRU_WIDTH)), _full((1, LRU_WIDTH)),
                  _full((LRU_HEADS, LRU_HEAD_DIM, LRU_HEAD_DIM)), _full((LRU_HEADS, LRU_HEAD_DIM, LRU_HEAD_DIM)),
                  _full((1, LRU_WIDTH)), _full((1, LRU_WIDTH)), _full((1, LRU_WIDTH))],
        out_specs=[wide(), wide(), wide(), wide()],
        out_shape=[_S((T, LRU_WIDTH))] * 4,
        scratch_shapes=[buf(tc + EDGE), buf(tc), buf(tc), buf(tc), buf(tc), buf(SEGS), buf(1)],
        compiler_params=_cp("arbitrary"),
    )(lx, lx, convw, convb, wa, wx, ba, bx, lam)


def lru_bwd(gh, h, c, r, gi, lx, convw, wa, wx, lam, name, exchange=()):
    T = gh.shape[0]
    tc = _tile(T, SCAN_CHUNK)
    m = tc // SEGS
    nch = T // tc
    ne = len(exchange)

    def body(*refs):
        gh_ref, h_ref, hh_ref, c_ref, r_ref, i_ref, x_ref, xh_ref, cw_ref, wa_ref, wx_ref, lam_ref = refs[:12]
        glx_ref, gwa_ref, gwx_ref, gba_ref, gbx_ref, glam_ref, gcb_ref, gcw_ref = refs[12 + ne:20 + ne]
        src_a, src_m, dst_a, dst_m, mbuf, hbuf, xe, gce, cin_ref, gcc_ref, carry_ref = refs[20 + 2 * ne:31 + 2 * ne]
        i = pl.program_id(0)
        if ne:
            start, finish = _chips_phases(refs[12:12 + ne], refs[20 + ne:20 + 2 * ne], *refs[31 + 2 * ne:])
            pl.when(i == 0)(start)
            pl.when(i == nch - 1)(finish)

        @pl.when(i == 0)
        def _():
            carry_ref[...] = jnp.zeros_like(carry_ref)
            gcc_ref[...] = jnp.zeros_like(gcc_ref)
            for ref in (gwa_ref, gwx_ref, gba_ref, gbx_ref, glam_ref, gcb_ref, gcw_ref):
                ref[...] = jnp.zeros_like(ref)

        first = i == nch - 1
        last = slice(tc - SUBLANES, tc)
        hbuf[SUBLANES:, :] = h_ref[...]
        hbuf[0:SUBLANES, :] = _down_a_segment(h_ref[last, :], jnp.where(first, 0.0, hh_ref[SUBLANES - 1:SUBLANES, :]))
        _fill_conv_window(xe, x_ref, xh_ref, first, tc)
        lam_v = lam_ref[...]
        sp = _softplus_neg(lam_v)
        a_all = jnp.exp(-LRU_C * r_ref[...] * sp)
        src_a[...] = a_all
        src_m[...] = a_all * gh_ref[...]
        for j0 in range(0, NB_LRU, LRU_JB):
            def kstep(kk, st):
                rows = _step_rows(m - 1 - kk)
                out = []
                for q in range(LRU_JB):
                    ln = _lanes(j0 + q)
                    mu, ac = st[2 * q], st[2 * q + 1]
                    a = src_a[rows, ln]
                    mu = a * mu + src_m[rows, ln]
                    ac = a * ac
                    dst_m[rows, ln] = mu
                    dst_a[rows, ln] = ac
                    out += [mu, ac]
                return tuple(out)

            init = tuple(jnp.zeros((SEGS, LANES), f32) if q % 2 == 0 else jnp.ones((SEGS, LANES), f32)
                         for q in range(2 * LRU_JB))
            ends = lax.fori_loop(0, m, kstep, init)
            for q in range(LRU_JB):
                ln = _lanes(j0 + q)
                em, ea = ends[2 * q], ends[2 * q + 1]
                cr = carry_ref[:, ln]
                rows_c = [None] * SEGS
                for s in reversed(range(SEGS)):
                    rows_c[s] = cr
                    cr = em[s:s + 1, :] + ea[s:s + 1, :] * cr
                cin_ref[:, ln] = _stack_rows(rows_c)
                carry_ref[:, ln] = cr

        def fix(k, _):
            rows = _step_rows(k)
            mbuf[rows, :] = dst_m[rows, :] + dst_a[rows, :] * cin_ref[...]
            return 0

        lax.fori_loop(0, m, fix, 0)
        mbuf[tc:, :] = _up_a_segment(mbuf[0:SUBLANES, :], cin_ref[SUBLANES - 1:SUBLANES, :])
        sneg = _sig(-lam_v)
        for j in range(NB_LRU):
            ln = _lanes(j)
            lamt = gh_ref[:, ln] + mbuf[SUBLANES:, ln]
            rj, ij, cj = r_ref[:, ln], i_ref[:, ln], c_ref[:, ln]
            log_a = -LRU_C * rj * sp[:, ln]
            a = src_a[:, ln]
            om = _one_minus_sq(a, log_a)
            inv_mult = lax.rsqrt(om)
            mult = om * inv_mult
            g_a = lamt * hbuf[0:tc, ln]
            g_mult = lamt * ij * cj
            g_i = lamt * mult * cj
            g_c = lamt * mult * ij
            g_log_a = g_a * a - g_mult * a * a * inv_mult
            glam_ref[:, ln] += jnp.sum(g_log_a * rj, axis=0, keepdims=True) * LRU_C * sneg[:, ln]
            g_ra = g_log_a * (-LRU_C) * sp[:, ln] * rj * (1.0 - rj)
            g_ia = g_i * ij * (1.0 - ij)
            gba_ref[:, ln] += jnp.sum(g_ra, axis=0, keepdims=True)
            gbx_ref[:, ln] += jnp.sum(g_ia, axis=0, keepdims=True)
            cjb, grb, gib = cj.astype(bf16), g_ra.astype(bf16), g_ia.astype(bf16)
            gwa_ref[j] += _dot_tn(cjb, grb)
            gwx_ref[j] += _dot_tn(cjb, gib)
            g_c = g_c + _dot_nt(grb, wa_ref[j]) + _dot_nt(gib, wx_ref[j])
            gce[0:tc, ln] = g_c
            gcb_ref[:, ln] += jnp.sum(g_c, axis=0, keepdims=True)
        for d in range(TAPS_BACK):
            blk = slice(SUBLANES * d, SUBLANES * (d + 1))
            gce[tc + SUBLANES * d:tc + SUBLANES * (d + 1), :] = _up_a_segment(gce[blk, :], gcc_ref[SUBLANES * d:SUBLANES * d + 1, :])
        gcc_ref[...] = gce[0:EDGE, :]
        gc = gce[0:tc, :]
        glx = cw_ref[CONV_WIDTH - 1:CONV_WIDTH, :] * gc
        gcw_ref[CONV_WIDTH - 1:CONV_WIDTH, :] += jnp.sum(gc * xe[EDGE:EDGE + tc, :], axis=0, keepdims=True)
        for k in range(CONV_WIDTH - 1):
            off = SUBLANES * (CONV_WIDTH - 1 - k)
            glx = glx + cw_ref[k:k + 1, :] * gce[off:off + tc, :]
            gcw_ref[k:k + 1, :] += jnp.sum(gc * xe[EDGE - off:EDGE - off + tc, :], axis=0, keepdims=True)
        glx_ref[...] = glx.astype(bf16)

    rev = lambda i: (nch - 1 - i, 0)
    halo = lambda rows: (lambda i: (jnp.maximum((nch - 1 - i) * (tc // rows) - 1, 0), 0))
    wide = lambda: pl.BlockSpec((tc, LRU_WIDTH), rev)
    vec = lambda: _full((1, LRU_WIDTH))
    hd = lambda: _full((LRU_HEADS, LRU_HEAD_DIM, LRU_HEAD_DIM))
    buf = lambda rows: pltpu.VMEM((rows, LRU_WIDTH), f32)
    outs = pl.pallas_call(
        body, name=name, grid=(nch,),
        in_specs=[wide(), wide(), pl.BlockSpec((SUBLANES, LRU_WIDTH), halo(SUBLANES)), wide(), wide(), wide(), wide(),
                  pl.BlockSpec((HALO_ROWS, LRU_WIDTH), halo(HALO_ROWS)), _full((CONV_WIDTH, LRU_WIDTH)), hd(), hd(), vec()]
        + [ANY] * ne,
        out_specs=[wide(), hd(), hd(), vec(), vec(), vec(), vec(), _full((CONV_WIDTH, LRU_WIDTH))] + [ANY] * ne,
        out_shape=[_S((T, LRU_WIDTH), bf16), _S((LRU_HEADS, LRU_HEAD_DIM, LRU_HEAD_DIM)),
                   _S((LRU_HEADS, LRU_HEAD_DIM, LRU_HEAD_DIM)), _S((1, LRU_WIDTH)), _S((1, LRU_WIDTH)),
                   _S((1, LRU_WIDTH)), _S((1, LRU_WIDTH)), _S((CONV_WIDTH, LRU_WIDTH))] + (
            _chips_out_shapes(exchange) if ne else []),
        scratch_shapes=[buf(tc), buf(tc), buf(tc), buf(tc), buf(tc + SUBLANES), buf(tc + SUBLANES), buf(tc + EDGE),
                        buf(tc + EDGE), buf(SEGS), buf(EDGE), buf(1)] + (_chips_semaphores(ne) if ne else []),
        compiler_params=_cp("arbitrary"),
    )(gh, h, h, c, r, gi, lx, lx, convw, wa, wx, lam, *exchange)
    return outs[:8], list(outs[8:])


def merge_fwd(h, lg, zs, gsl, x, p, wbl, wout, gpost, wple, wpg, name):
    T = x.shape[0]
    tm = _tile(T, TOKEN_TILE)

    def body(h_ref, lg_ref, zs_ref, gs_ref, gl_ref, x_ref, p_ref, wbl_ref, wout_ref, gp_ref, wple_ref, wpg_ref,
             zl_ref, mix_ref, xo_ref):
        lg_v = lg_ref[...]
        yl = h_ref[...] * (lg_v * _sig(lg_v))
        zl = _dot(yl.astype(bf16), wbl_ref[...])
        merged = _sig(gs_ref[...]) * zs_ref[...] + _sig(gl_ref[...]) * zl
        mix = _dot(merged.astype(bf16), wout_ref[...])
        r2 = lax.rsqrt(jnp.mean(mix * mix, axis=-1, keepdims=True) + NORM_EPS)
        x1 = x_ref[...] + mix * r2 * gp_ref[...]
        q = _dot(x1.astype(bf16), wpg_ref[...])
        pe = _dot(p_ref[...].astype(bf16), wple_ref[...])
        zl_ref[...], mix_ref[...] = zl, mix
        xo_ref[...] = x1 + pe * _sig(q)

    dm = lambda: _rows(tm, D_MODEL)
    return pl.pallas_call(
        body, name=name, grid=(T // tm,),
        in_specs=[_rows(tm, LRU_WIDTH), _rows(tm, LRU_WIDTH), dm(), _rows(tm, D_MODEL, 0), _rows(tm, D_MODEL, 1), dm(),
                  _rows(tm, PLE_DIM), _resident((LRU_WIDTH, D_MODEL)), _resident((D_MODEL, D_MODEL)), _full((1, D_MODEL)),
                  _resident((PLE_DIM, D_MODEL)), _resident((D_MODEL, D_MODEL))],
        out_specs=[dm(), dm(), dm()],
        out_shape=[_S((T, D_MODEL))] * 3, compiler_params=_cp("parallel"),
    )(h, lg, zs, gsl, gsl, x, p, wbl, wout, gpost, wple, wpg)


def post_bwd(gx2, mix, x, p, wpg, wple, gpost, name):
    T = x.shape[0]
    tm = _tile(T, TOKEN_TILE_LIGHT)

    def body(gx2_ref, mix_ref, x_ref, p_ref, wpg_ref, wple_ref, gp_ref, gres_ref, gmix_ref, ggp_ref, gwpg_ref, gwple_ref):
        i = pl.program_id(0)
        gx2 = gx2_ref[...]
        mix = mix_ref[...]
        gp = gp_ref[...]
        r2 = lax.rsqrt(jnp.mean(mix * mix, axis=-1, keepdims=True) + NORM_EPS)
        nrm = mix * r2
        x1b = (x_ref[...] + nrm * gp).astype(bf16)
        pb = p_ref[...].astype(bf16)
        sq = _sig(_dot(x1b, wpg_ref[...]))
        pe = _dot(pb, wple_ref[...])
        gq = (gx2 * pe * sq * (1.0 - sq)).astype(bf16)
        _accumulate(gwple_ref, _dot_tn(pb, (gx2 * sq).astype(bf16)), i)
        _accumulate(gwpg_ref, _dot_tn(x1b, gq), i)
        gx1 = gx2 + _dot_nt(gq, wpg_ref[...])
        gres_ref[...] = gx1
        _accumulate(ggp_ref, jnp.sum(gx1 * nrm, axis=0, keepdims=True), i)
        gy = gx1 * gp
        gmix_ref[...] = (r2 * (gy - nrm * jnp.mean(gy * nrm, axis=-1, keepdims=True))).astype(bf16)

    dm = lambda: _rows(tm, D_MODEL)
    return pl.pallas_call(
        body, name=name, grid=(T // tm,),
        in_specs=[dm(), dm(), dm(), _rows(tm, PLE_DIM), _resident((D_MODEL, D_MODEL)), _resident((PLE_DIM, D_MODEL)),
                  _full((1, D_MODEL))],
        out_specs=[dm(), dm(), _full((1, D_MODEL)), _full((D_MODEL, D_MODEL)), _full((PLE_DIM, D_MODEL))],
        out_shape=[_S((T, D_MODEL)), _S((T, D_MODEL), bf16), _S((1, D_MODEL)), _S((D_MODEL, D_MODEL)),
                   _S((PLE_DIM, D_MODEL))],
        compiler_params=_cp("arbitrary"),
    )(gx2, mix, x, p, wpg, wple, gpost)


def gate_bwd(gmix, zl, zs, gsl, h, lg, wout, wbl, name):
    T = zl.shape[0]
    tm = _tile(T, TOKEN_TILE)

    def body(gmix_ref, zl_ref, zs_ref, gs_ref, gl_ref, h_ref, lg_ref, wout_ref, wbl_ref,
             gzs_ref, ggsl_ref, gh_ref, glg_ref, gwout_ref, gwbl_ref):
        i = pl.program_id(0)
        gmix = gmix_ref[...]
        gmerged = _dot_nt(gmix, wout_ref[...])
        zs, zl = zs_ref[...], zl_ref[...]
        ss, sl = _sig(gs_ref[...]), _sig(gl_ref[...])
        _accumulate(gwout_ref, _dot_tn((ss * zs + sl * zl).astype(bf16), gmix), i)
        gzs_ref[...] = (gmerged * ss).astype(bf16)
        gzl = (gmerged * sl).astype(bf16)
        ggsl_ref[:, :D_MODEL] = (gmerged * zs * ss * (1.0 - ss)).astype(bf16)
        ggsl_ref[:, D_MODEL:] = (gmerged * zl * sl * (1.0 - sl)).astype(bf16)
        lg_v, hv = lg_ref[...], h_ref[...]
        slg = _sig(lg_v)
        silu = lg_v * slg
        _accumulate(gwbl_ref, _dot_tn((hv * silu).astype(bf16), gzl), i)
        gyl = _dot_nt(gzl, wbl_ref[...])
        gh_ref[...] = gyl * silu
        glg_ref[...] = (gyl * hv * slg * (1.0 + lg_v * (1.0 - slg))).astype(bf16)

    dm = lambda: _rows(tm, D_MODEL)
    lw = lambda: _rows(tm, LRU_WIDTH)
    return pl.pallas_call(
        body, name=name, grid=(T // tm,),
        in_specs=[dm(), dm(), dm(), _rows(tm, D_MODEL, 0), _rows(tm, D_MODEL, 1), lw(), lw(),
                  _resident((D_MODEL, D_MODEL)), _resident((LRU_WIDTH, D_MODEL))],
        out_specs=[dm(), _rows(tm, 2 * D_MODEL), lw(), lw(), _full((D_MODEL, D_MODEL)), _full((LRU_WIDTH, D_MODEL))],
        out_shape=[_S((T, D_MODEL), bf16), _S((T, 2 * D_MODEL), bf16), _S((T, LRU_WIDTH)), _S((T, LRU_WIDTH), bf16),
                   _S((D_MODEL, D_MODEL)), _S((LRU_WIDTH, D_MODEL))],
        compiler_params=_cp("arbitrary"),
    )(gmix, zl, zs, gsl, gsl, h, lg, wout, wbl)


def in_proj_bwd(pieces, win, x, gres, g, name):
    T = x.shape[0]
    tm = _tile(T, MM_TILE_M // 2)
    widths = [pc.shape[1] for pc in pieces]
    offs = [sum(widths[:k]) for k in range(len(widths))]

    def body(*refs):
        pc_refs = refs[:len(widths)]
        w_ref, x_ref, gres_ref, g_ref, gx_ref, gg_ref = refs[len(widths):]
        i = pl.program_id(0)
        ghv = _dot_nt(pc_refs[0][...], w_ref[:, offs[0]:offs[0] + widths[0]])
        for k in range(1, len(widths)):
            ghv = ghv + _dot_nt(pc_refs[k][...], w_ref[:, offs[k]:offs[k] + widths[k]])
        xv = x_ref[...]
        r = lax.rsqrt(jnp.mean(xv * xv, axis=-1, keepdims=True) + NORM_EPS)
        nrm = xv * r
        gy = ghv * g_ref[...]
        gx_ref[...] = gres_ref[...] + r * (gy - nrm * jnp.mean(gy * nrm, axis=-1, keepdims=True))
        _accumulate(gg_ref, jnp.sum(ghv * nrm, axis=0, keepdims=True), i)

    dm = lambda: _rows(tm, D_MODEL)
    return pl.pallas_call(
        body, name=name, grid=(T // tm,),
        in_specs=[_rows(tm, wd) for wd in widths] + [_resident(win.shape), dm(), dm(), _full((1, D_MODEL))],
        out_specs=[dm(), _full((1, D_MODEL))],
        out_shape=[_S((T, D_MODEL)), _S((1, D_MODEL))], compiler_params=_cp("arbitrary"),
    )(*pieces, win, x, gres, g)


def loss_head(y, target, name):
    T = y.shape[0]
    tm = _tile(T, TOKEN_TILE_LIGHT)

    def body(y_ref, t_ref, l_ref, g_ref):
        i = pl.program_id(0)
        e = y_ref[...] - t_ref[...]
        g_ref[...] = e * (1.0 / D_MODEL)
        part = 0.5 * jnp.sum(jnp.sum(e * e, axis=-1, keepdims=True) * (1.0 / D_MODEL), axis=0, keepdims=True)

        @pl.when(i == 0)
        def _():
            l_ref[...] = part

        @pl.when(i > 0)
        def _():
            l_ref[...] += part

    return pl.pallas_call(
        body, name=name, grid=(T // tm,),
        in_specs=[_rows(tm, D_MODEL), _rows(tm, D_MODEL)], out_specs=[_full((1, 1)), _rows(tm, D_MODEL)],
        out_shape=[_S((1, 1)), _S((T, D_MODEL))],
        compiler_params=_cp("arbitrary"),
    )(y, target)


def _s5_operands(w, m, tag):
    b_re_t = jnp.transpose(w['s5_b_re'], (2, 0, 1))
    b_im_t = jnp.transpose(w['s5_b_im'], (2, 0, 1))
    ldt = w['s5_log_dt'][:, None]
    ab, pw, bb = s5_prep(w['s5_a_re'], w['s5_a_im'], ldt, b_re_t, b_im_t, m, "s5_prep" + tag)
    over_sublanes = lambda t: jnp.broadcast_to(t[..., None, :], t.shape[:-1] + (SEGS, S5_NS))
    ptab = pw.reshape(2, m, S5_NS)
    return dict(abar_b=over_sublanes(ab.reshape(2, S5_NS)), ptab_b=over_sublanes(ptab),
                ptab_rev_b=over_sublanes(ptab[:, ::-1, :]), bdb=_pack_bdb(bb).astype(bf16),
                cdb=_pack_cdb(w['s5_c_re'], w['s5_c_im']).astype(bf16), dvec=w['s5_d'][None, :],
                prep_in=(w['s5_a_re'], w['s5_a_im'], ldt, b_re_t, b_im_t))


def layer_fwd(x, p, w, tag, gather=()):
    T = x.shape[0]
    m = _tile(T, SCAN_CHUNK) // SEGS
    s5 = _s5_operands(w, m, tag)
    h_bf = rms_fwd(x, w['g_pre'][None, :], "rms_fwd" + tag)
    win = w['w_in']
    usg = mm_nn(h_bf, win[:, :2 * S5_WIDTH], "proj_s5" + tag)
    lx = mm_nn(h_bf, win[:, 2 * S5_WIDTH:2 * S5_WIDTH + LRU_WIDTH], "proj_lx" + tag)
    lg = mm_nn(h_bf, win[:, 2 * S5_WIDTH + LRU_WIDTH:2 * S5_WIDTH + 2 * LRU_WIDTH], "proj_lg" + tag)
    gsl = mm_nn(h_bf, win[:, 2 * S5_WIDTH + 2 * LRU_WIDTH:], "proj_gate" + tag)
    (ys, s_re, s_im, s_bf), gathered = s5_fwd(usg, s5['bdb'], s5['cdb'], s5['dvec'], s5['abar_b'], s5['ptab_b'],
                                              "s5_fwd" + tag, gather)
    glu, zs = s5_post_fwd(ys, usg, w['w_glu'], w['w_bs'], "s5_post_fwd" + tag)
    wa, wx = w['lru_w_a'].astype(bf16), w['lru_w_x'].astype(bf16)
    c, r, gi, hs = lru_fwd(lx, w['conv_w'], w['conv_b'][None, :], wa, wx, w['lru_b_a'][None, :], w['lru_b_x'][None, :],
                           w['lru_lambda'][None, :], "lru_fwd" + tag)
    zl, mix, x_out = merge_fwd(hs, lg, zs, gsl, x, p, w['w_bl'], w['w_out'], w['g_post'][None, :],
                               w['w_ple'], w['w_ple_gate'], "merge_fwd" + tag)
    saved = dict(x=x, p=p, h_bf=h_bf, usg=usg, lx=lx, lg=lg, gsl=gsl, ys=ys, s_re=s_re, s_im=s_im, s_bf=s_bf, glu=glu,
                 zs=zs, c=c, r=r, gi=gi, hs=hs, zl=zl, mix=mix, s5=s5, wa=wa, wx=wx)
    return x_out, saved, gathered


def layer_bwd(gx_out, w, sv, tag, exchange=()):
    s5 = sv['s5']
    g = {}
    gres, gmix, g_gpost, g['w_ple_gate'], g['w_ple'] = post_bwd(
        gx_out, sv['mix'], sv['x'], sv['p'], w['w_ple_gate'], w['w_ple'], w['g_post'][None, :], "post_bwd" + tag)
    gzs, ggsl, g_h, g_lg, g['w_out'], g['w_bl'] = gate_bwd(
        gmix, sv['zl'], sv['zs'], sv['gsl'], sv['hs'], sv['lg'], w['w_out'], w['w_bl'], "gate_bwd" + tag)
    g['g_post'] = g_gpost[0]
    (g_lx, g_wa, g_wx, g_ba, g_bx, g_lam, g_cb, g_cw), exchanged = lru_bwd(
        g_h, sv['hs'], sv['c'], sv['r'], sv['gi'], sv['lx'], w['conv_w'], sv['wa'], sv['wx'],
        w['lru_lambda'][None, :], "lru_bwd" + tag, exchange)
    g['lru_w_a'], g['lru_w_x'] = g_wa, g_wx
    g['lru_b_a'], g['lru_b_x'], g['lru_lambda'], g['conv_b'], g['conv_w'] = g_ba[0], g_bx[0], g_lam[0], g_cb[0], g_cw
    g_ys, g_sg, g['w_bs'], g['w_glu'] = s5_post_bwd(gzs, sv['glu'], sv['usg'], sv['ys'], w['w_bs'], w['w_glu'],
                                                    "s5_post_bwd" + tag)
    g_u, g_ab, g_d, g_bdb, g_cdb = s5_bwd(g_ys, sv['usg'], sv['s_re'], sv['s_im'], sv['s_bf'], s5['bdb'], s5['cdb'],
                                          s5['dvec'], s5['abar_b'], s5['ptab_rev_b'], "s5_bwd" + tag)
    g['s5_d'] = g_d[0]
    g['s5_c_re'], g['s5_c_im'] = _unpack_cdb(g_cdb)
    g_are, g_aim, g_ldt, g_bre_t, g_bim_t = s5_prep_bwd(*s5['prep_in'], g_ab.reshape(2, S5_GROUPS, S5_STATE),
                                                       _unpack_bdb(g_bdb), "s5_prep_bwd" + tag)
    g['s5_a_re'], g['s5_a_im'], g['s5_log_dt'] = g_are, g_aim, g_ldt
    g['s5_b_re'] = jnp.transpose(g_bre_t, (1, 2, 0))
    g['s5_b_im'] = jnp.transpose(g_bim_t, (1, 2, 0))
    pieces = [g_u, g_sg, g_lx, g_lg, ggsl]
    g['w_in'] = jnp.concatenate([mm_tn(sv['h_bf'], pc, "gw_in%d%s" % (k, tag)) for k, pc in enumerate(pieces)], axis=1)
    gx, g_gpre = in_proj_bwd(pieces, w['w_in'], sv['x'], gres, w['g_pre'][None, :], "in_proj_bwd" + tag)
    g['g_pre'] = g_gpre[0]
    return gx, g, exchanged


def _as_2d(a):
    return a.reshape((-1, a.shape[-1])) if a.ndim > 1 else a.reshape((1, -1))


def _adamw_update(w, gv, m, v):
    nm = ADAM_B1 * m + (1.0 - ADAM_B1) * gv
    nv = ADAM_B2 * v + (1.0 - ADAM_B2) * (gv * gv)
    bc1 = 1.0 - ADAM_B1 ** ADAM_STEP
    bc2 = 1.0 - ADAM_B2 ** ADAM_STEP
    return -ADAM_LR * ((nm / bc1) / (jnp.sqrt(nv / bc2) + ADAM_EPS) + ADAM_WD * w), nm, nv


def adamw(w, g, m, v, name):
    shape = w.shape
    w2, g2, m2, v2 = _as_2d(w), _as_2d(g), _as_2d(m), _as_2d(v)
    R, C = w2.shape
    tr = _row_tile(R, C)

    def body(w_ref, g_ref, m_ref, v_ref, d_ref, nm_ref, nv_ref):
        d_ref[...], nm_ref[...], nv_ref[...] = _adamw_update(w_ref[...], g_ref[...], m_ref[...], v_ref[...])

    spec = lambda: pl.BlockSpec((tr, C), lambda i: (i, 0))
    d, nm, nv = pl.pallas_call(
        body, name=name, grid=(R // tr,), in_specs=[spec() for _ in range(4)], out_specs=[spec() for _ in range(3)],
        out_shape=[_S((R, C))] * 3, compiler_params=_cp("parallel"),
    )(w2, g2, m2, v2)
    return d.reshape(shape), nm.reshape(shape), nv.reshape(shape)


def adamw_reduce(w, parts, theirs, m, v, chip, name):
    shape = w.shape
    C = shape[-1]
    R = math.prod(shape[1:-1])
    w3, m3, v3 = w.reshape(DEPTH, R, C), m.reshape(DEPTH, R, C), v.reshape(DEPTH, R, C)
    tr = _row_tile(R, C)

    def body(chip_ref, w_ref, *refs):
        layer_refs, (m_ref, v_ref, g_ref, d_ref, nm_ref, nv_ref) = refs[:2 * DEPTH], refs[2 * DEPTH:]
        layer = pl.program_id(0)
        for l in range(DEPTH):
            @pl.when(layer == l)
            def _():
                a_ref, t_ref = layer_refs[2 * l], layer_refs[2 * l + 1]
                gv = ((a_ref[0] + t_ref[0].astype(f32)) + t_ref[1].astype(f32)) + t_ref[2].astype(f32)
                g_ref[0] = gv
                d_ref[0], nm_ref[0], nv_ref[0] = _adamw_update(w_ref[0], gv, m_ref[0], v_ref[0])

    spec = lambda: pl.BlockSpec((1, tr, C), lambda l, i, c: (l, i, 0))
    rows_of = lambda l: (lambda ll, i, c: jnp.where(ll == l, i, 0))
    layer_specs = []
    for l in range(DEPTH):
        layer_specs.append(pl.BlockSpec((1, tr, C), lambda ll, i, c, r=rows_of(l): (c[0], r(ll, i, c), 0)))
        layer_specs.append(pl.BlockSpec((3, tr, C), lambda ll, i, c, r=rows_of(l): (0, r(ll, i, c), 0)))
    grid_spec = pltpu.PrefetchScalarGridSpec(
        num_scalar_prefetch=1, grid=(DEPTH, R // tr),
        in_specs=[spec()] + layer_specs + [spec(), spec()], out_specs=[spec() for _ in range(4)])
    operands = [x.reshape(x.shape[0], R, C) for l in range(DEPTH) for x in (parts[l], theirs[l])]
    g, d, nm, nv = pl.pallas_call(
        body, name=name, grid_spec=grid_spec, out_shape=[_S((DEPTH, R, C))] * 4,
        compiler_params=_cp("arbitrary", "arbitrary"),
    )(chip, w3, *operands, m3, v3)
    return g.reshape(shape), d.reshape(shape), nm.reshape(shape), nv.reshape(shape)


MESH = pl.DeviceIdType.MESH
ANY = pl.BlockSpec(memory_space=pl.ANY)


def _place():
    return lax.axis_index("x"), lax.axis_index("y"), lax.axis_index("c")


def _other_chips(mx, my):
    return [(1 - mx, my), (mx, 1 - my), (1 - mx, 1 - my)]


def all_gather(shards, name):
    nb = len(shards)

    def body(*refs):
        phases = _gather_phases([s.shape for s in shards], refs[:nb], refs[nb:2 * nb], *refs[2 * nb:])
        for phase in phases:
            phase()

    outs = pl.pallas_call(
        body, name=name, out_shape=_gather_out_shapes(shards), in_specs=[ANY] * nb, out_specs=[ANY] * nb,
        scratch_shapes=_gather_semaphores(nb),
    )(*shards)
    return list(outs)


GATHER_COPIES = 9
OWN_SIB, OWN_X, OWN_Y, X_SIB, Y_SIB, RELAY_X, RELAY_Y, DIAG0_SIB, DIAG1_SIB = range(GATHER_COPIES)


def _gather_out_shapes(shards):
    return [_S((N_DEV,) + s.shape, s.dtype) for s in shards]


def _gather_semaphores(nb):
    return [pltpu.SemaphoreType.DMA((nb, GATHER_COPIES)), pltpu.SemaphoreType.DMA((nb, GATHER_COPIES)),
            pltpu.SemaphoreType.DMA((nb,))]


def _gather_phases(shapes, x_refs, out_refs, send_sems, recv_sems, local_sems):
    nb = len(shapes)
    mx, my, mc = _place()
    sibling, xn, yn = (mx, my, 1 - mc), (1 - mx, my, mc), (mx, 1 - my, mc)

    def block(b, px, py, pc, half=None):
        ref = out_refs[b].at[4 * px + 2 * py + pc]
        if half is None:
            return ref
        n = shapes[b][0] // 2
        return ref.at[pl.ds(half * n, n)]

    def copy(b, k, dst, to, src=None):
        return pltpu.make_async_remote_copy(
            src_ref=dst if src is None else src, dst_ref=dst, send_sem=send_sems.at[b, k],
            recv_sem=recv_sems.at[b, k], device_id=to, device_id_type=MESH)

    def send(b, k):
        if k in (OWN_X, OWN_Y, OWN_SIB):
            return copy(b, k, block(b, mx, my, mc), {OWN_X: xn, OWN_Y: yn, OWN_SIB: sibling}[k], src=x_refs[b])
        what, to = {RELAY_X: ((1 - mx, my, mc, 0), yn), RELAY_Y: ((mx, 1 - my, mc, 1), xn),
                    X_SIB: ((1 - mx, my, mc), sibling), Y_SIB: ((mx, 1 - my, mc), sibling),
                    DIAG0_SIB: ((1 - mx, 1 - my, mc, 0), sibling), DIAG1_SIB: ((1 - mx, 1 - my, mc, 1), sibling)}[k]
        return copy(b, k, block(b, *what), to)

    def local(b):
        return pltpu.make_async_copy(x_refs[b], block(b, mx, my, mc), local_sems.at[b])

    def send_own():
        for k in (OWN_X, OWN_Y, OWN_SIB):
            for b in range(nb):
                send(b, k).start()
        for b in range(nb):
            local(b).start()

    def relay_neighbours():
        for b in range(nb):
            copy(b, OWN_X, block(b, 1 - mx, my, mc), xn).wait_recv()
            send(b, RELAY_X).start()
            send(b, X_SIB).start()
        for b in range(nb):
            copy(b, OWN_Y, block(b, mx, 1 - my, mc), yn).wait_recv()
            send(b, RELAY_Y).start()
            send(b, Y_SIB).start()

    def hand_on_diagonal():
        for b in range(nb):
            copy(b, RELAY_X, block(b, 1 - mx, 1 - my, mc, 0), yn).wait_recv()
            send(b, DIAG0_SIB).start()
            copy(b, RELAY_Y, block(b, 1 - mx, 1 - my, mc, 1), xn).wait_recv()
            send(b, DIAG1_SIB).start()

    def finish():
        for b in range(nb):
            copy(b, OWN_SIB, block(b, mx, my, 1 - mc), sibling).wait_recv()
            copy(b, X_SIB, block(b, 1 - mx, my, 1 - mc), sibling).wait_recv()
            copy(b, Y_SIB, block(b, mx, 1 - my, 1 - mc), sibling).wait_recv()
            copy(b, DIAG0_SIB, block(b, 1 - mx, 1 - my, 1 - mc, 0), sibling).wait_recv()
            copy(b, DIAG1_SIB, block(b, 1 - mx, 1 - my, 1 - mc, 1), sibling).wait_recv()
        for b in range(nb):
            for k in range(GATHER_COPIES):
                send(b, k).wait_send()
            local(b).wait()

    return send_own, relay_neighbours, hand_on_diagonal, finish


def exchange_sibling(gs, name):
    nb = len(gs)

    def body(*refs):
        g_refs, recv_refs, send_sems, recv_sems = refs[:nb], refs[nb:2 * nb], refs[2 * nb], refs[2 * nb + 1]
        mx, my, mc = _place()
        copies = [pltpu.make_async_remote_copy(
            src_ref=g_refs[b].at[2 * k + 1 - mc], dst_ref=recv_refs[b].at[k], send_sem=send_sems.at[b, k],
            recv_sem=recv_sems.at[b, k], device_id=(mx, my, 1 - mc), device_id_type=MESH)
            for b in range(nb) for k in range(4)]
        for cp in copies:
            cp.start()
        for cp in copies:
            cp.wait()

    outs = pl.pallas_call(
        body, name=name, out_shape=[_S((4,) + g.shape[1:], g.dtype) for g in gs], in_specs=[ANY] * nb,
        out_specs=[ANY] * nb,
        scratch_shapes=[pltpu.SemaphoreType.DMA((nb, 4)), pltpu.SemaphoreType.DMA((nb, 4))],
    )(*gs)
    return list(outs)


def exchange_chips(parts, name):
    nb = len(parts)

    def body(*refs):
        start, finish = _chips_phases(refs[:nb], refs[nb:2 * nb], refs[2 * nb], refs[2 * nb + 1])
        start()
        finish()

    outs = pl.pallas_call(
        body, name=name, out_shape=_chips_out_shapes(parts), in_specs=[ANY] * nb, out_specs=[ANY] * nb,
        scratch_shapes=_chips_semaphores(nb),
    )(*parts)
    return list(outs)


def _chips_out_shapes(parts):
    return [_S((3,) + a.shape[1:], a.dtype) for a in parts]


def _chips_semaphores(nb):
    return [pltpu.SemaphoreType.DMA((nb, 3)), pltpu.SemaphoreType.DMA((nb, 3))]


def _chips_phases(a_refs, recv_refs, send_sems, recv_sems):
    mx, my, mc = _place()

    def copies():
        return [pltpu.make_async_remote_copy(
            src_ref=a_refs[b].at[2 * px + py], dst_ref=recv_refs[b].at[j], send_sem=send_sems.at[b, j],
            recv_sem=recv_sems.at[b, j], device_id=(px, py, mc), device_id_type=MESH)
            for b in range(len(a_refs)) for j, (px, py) in enumerate(_other_chips(mx, my))]

    def start():
        for cp in copies():
            cp.start()

    def finish():
        for cp in copies():
            cp.wait()

    return start, finish


def add_sibling(g, theirs, core, name, wire_dtype=f32):
    shp = theirs.shape
    C = shp[-1]
    R = math.prod(shp[1:-1])
    tr = _row_tile(R, C)
    narrow = wire_dtype != f32

    def body(core_ref, g_ref, t_ref, o_ref, *wire_ref):
        s = g_ref[...] + t_ref[...]
        o_ref[...] = s
        if narrow:
            wire_ref[0][...] = s.astype(wire_dtype)

    blk = lambda: pl.BlockSpec((1, tr, C), lambda k, i, c: (k, i, 0))
    grid_spec = pltpu.PrefetchScalarGridSpec(
        num_scalar_prefetch=1, grid=(4, R // tr),
        in_specs=[pl.BlockSpec((1, tr, C), lambda k, i, c: (2 * k + c[0], i, 0)), blk()],
        out_specs=[blk(), blk()] if narrow else [blk()])
    outs = pl.pallas_call(
        body, name=name, grid_spec=grid_spec,
        out_shape=[_S((4, R, C), f32)] + ([_S((4, R, C), wire_dtype)] if narrow else []),
        compiler_params=_cp("parallel", "parallel"),
    )(core, g.reshape(N_DEV, R, C), theirs.reshape(4, R, C))
    part = outs[0].reshape(shp)
    return part, (outs[1].reshape(shp) if narrow else part)


def add_chips(a, theirs, chip, name):
    _, R, C = a.shape
    tr = _row_tile(R, C)

    def body(chip_ref, a_ref, t_ref, out_ref):
        out_ref[...] = ((a_ref[0] + t_ref[0]) + t_ref[1]) + t_ref[2]

    grid_spec = pltpu.PrefetchScalarGridSpec(
        num_scalar_prefetch=1, grid=(R // tr,),
        in_specs=[pl.BlockSpec((1, tr, C), lambda i, c: (c[0], i, 0)), pl.BlockSpec((3, tr, C), lambda i, c: (0, i, 0))],
        out_specs=pl.BlockSpec((tr, C), lambda i, c: (i, 0)))
    return pl.pallas_call(
        body, name=name, grid_spec=grid_spec, out_shape=_S((R, C), a.dtype), compiler_params=_cp("parallel"),
    )(chip, a, theirs)


def _round_up(n, q):
    return (n + q - 1) // q * q


def _lane_rows(a):
    flat = a.reshape(-1)
    n = _round_up(flat.shape[0], SUBLANES * LANES)
    return jnp.pad(flat, (0, n - flat.shape[0])).reshape(-1, LANES)


def _full_to_shards(full, axis):
    shp = full.shape
    s = shp[axis] // N_DEV
    cut = full.reshape(shp[:axis] + (N_DEV, s) + shp[axis + 1:])
    return jnp.moveaxis(cut, axis, 0)


def _shards_to_full(parts, axis):
    shp = list(parts.shape[1:])
    shp[axis] *= N_DEV
    return jnp.moveaxis(parts, 0, axis).reshape(tuple(shp))


def kernel(x, p, g_pre, w_in, s5_a_re, s5_a_im, s5_log_dt, s5_b_re, s5_b_im, s5_c_re, s5_c_im, s5_d, w_glu, w_bs, conv_w, conv_b, lru_w_a, lru_b_a, lru_w_x, lru_b_x, lru_lambda, w_bl, w_out, g_post, w_ple, w_ple_gate, loss_target, m_g_pre, m_w_in, m_s5_a_re, m_s5_a_im, m_s5_log_dt, m_s5_b_re, m_s5_b_im, m_s5_c_re, m_s5_c_im, m_s5_d, m_w_glu, m_w_bs, m_conv_w, m_conv_b, m_lru_w_a, m_lru_b_a, m_lru_w_x, m_lru_b_x, m_lru_lambda, m_w_bl, m_w_out, m_g_post, m_w_ple, m_w_ple_gate, v_g_pre, v_w_in, v_s5_a_re, v_s5_a_im, v_s5_log_dt, v_s5_b_re, v_s5_b_im, v_s5_c_re, v_s5_c_im, v_s5_d, v_w_glu, v_w_bs, v_conv_w, v_conv_b, v_lru_w_a, v_lru_b_a, v_lru_w_x, v_lru_b_x, v_lru_lambda, v_w_bl, v_w_out, v_g_post, v_w_ple, v_w_ple_gate):
    given = dict(locals())
    W = {n: given[n] for n in WEIGHTS}
    M = {n: given["m_" + n] for n in WEIGHTS}
    V = {n: given["v_" + n] for n in WEIGHTS}
    xs, target = to_scan_order(x[0]), to_scan_order(loss_target[0])
    ps = [to_scan_order(p[i, 0]) for i in range(DEPTH)]

    mx, my, mc = _place()
    core = jnp.reshape(mc, (1,)).astype(jnp.int32)
    chip = jnp.reshape(2 * mx + my, (1,)).astype(jnp.int32)

    names = list(SHARDED)
    conv_rows = PAIR - CONV_WIDTH

    def layer_shards(i):
        return [W[n][i].astype(bf16) if n in GATHER_BF16 else jnp.pad(W[n][i], ((0, conv_rows), (0, 0))) for n in names]

    def layer_weights(i, gathered):
        full = {n: _shards_to_full(g if n in GATHER_BF16 else g[:, :CONV_WIDTH], SHARDED[n] - 1)
                for n, g in zip(names, gathered)}
        return {n: (full[n] if n in SHARDED else W[n][i]) for n in WEIGHTS}

    act, saved, weights = xs, [], []
    gathered = all_gather(layer_shards(0), "comm_gather_weights")
    for i in range(DEPTH):
        weights.append(layer_weights(i, gathered))
        act, sv, gathered = layer_fwd(act, ps[i], weights[i], "_l%d" % i, layer_shards(i + 1) if i + 1 < DEPTH else ())
        saved.append(sv)
    loss_part, gact = loss_head(act, target, "loss_head")
    loss = lax.psum(loss_part[0, 0], ("x", "y", "c"))

    def sibling_sums(i, g):
        rep_rows = [_lane_rows(g[n].reshape(W[n].shape[1:])) for n in REPLICATED]
        n_rows = sum(r.shape[0] for r in rep_rows)
        pad_rows = _round_up(n_rows, N_DEV * SUBLANES) - n_rows
        rep_blocks = jnp.concatenate(rep_rows + [jnp.zeros((pad_rows, LANES), f32)]).reshape(N_DEV, -1, LANES)
        blocks = [_full_to_shards(g[n].reshape(weights[i][n].shape), SHARDED[n] - 1) for n in names] + [rep_blocks]
        theirs = exchange_sibling(blocks, "comm_reduce_sibling_l%d" % i)
        parts, wire = [], []
        for k, (b, t) in enumerate(zip(blocks, theirs)):
            part, sent = add_sibling(b, t, core, "reduce_add_sibling_%d_l%d" % (k, i), bf16 if k < len(names) else f32)
            parts.append(part)
            wire.append(sent)
        return parts, wire, [r.shape[0] for r in rep_rows]

    parts, others, rep_sizes, wire = [None] * DEPTH, [None] * DEPTH, None, ()
    for i in reversed(range(DEPTH)):
        gact, g, exchanged = layer_bwd(gact, weights[i], saved[i], "_l%d" % i, wire)
        if i + 1 < DEPTH:
            others[i + 1] = exchanged
        parts[i], wire, rep_sizes = sibling_sums(i, g)
    others[0] = exchange_chips(wire, "comm_reduce_chips")

    red, deltas, new_m, new_v = {}, {}, {}, {}
    for k, n in enumerate(names):
        red[n], deltas[n], new_m[n], new_v[n] = adamw_reduce(
            W[n], [parts[i][k] for i in range(DEPTH)], [others[i][k] for i in range(DEPTH)], M[n], V[n], chip, "adamw_" + n)
    pieces = [add_chips(parts[i][-1], others[i][-1], chip, "reduce_add_chips_l%d" % i) for i in range(DEPTH)]
    rep_all = [r.reshape(-1, LANES) for r in all_gather(pieces, "comm_gather_replicated")]
    off = 0
    for n, rows in zip(REPLICATED, rep_sizes):
        k = math.prod(W[n].shape[1:])
        red[n] = jnp.stack([rep_all[i][off:off + rows].reshape(-1)[:k] for i in range(DEPTH)]).reshape(W[n].shape)
        off += rows
        deltas[n], new_m[n], new_v[n] = adamw(W[n], red[n], M[n], V[n], "adamw_" + n)
    return (loss, from_scan_order(gact)[None], *[red[n] for n in WEIGHTS], *[deltas[n] for n in WEIGHTS],
            *[new_m[n] for n in WEIGHTS], *[new_v[n] for n in WEIGHTS])
```

```python
import math

import jax
import jax.numpy as jnp
from jax import lax
from jax.experimental import pallas as pl
from jax.experimental.pallas import tpu as pltpu

f32 = jnp.float32
bf16 = jnp.bfloat16

D_MODEL = 1024
DEPTH = 2
PLE_DIM = 256
NORM_EPS = 1e-6
S5_WIDTH = 512
S5_GROUP = 16
S5_GROUPS = 32
S5_STATE = 64
S5_NS = S5_GROUPS * S5_STATE
LRU_WIDTH = 1280
LRU_HEADS = 10
LRU_HEAD_DIM = 128
LRU_C = 8.0
CONV_WIDTH = 4
N_DEV = 8

ADAM_LR = 0.001
ADAM_B1 = 0.9
ADAM_B2 = 0.999
ADAM_EPS = 1e-08
ADAM_WD = 0.01
ADAM_STEP = 10

LANES = 128
SUBLANES = 8
SEGS = SUBLANES
SCAN_CHUNK = 256
TOKEN_TILE = 256
TOKEN_TILE_LIGHT = 512
MM_TILE_M = 1024
MM_TILE_N = 1408
MM_TILE_K_ROWS = 1280
PAIR = 2 * SUBLANES
VMEM_LIMIT_BYTES = 56 * 1024 * 1024
ELEMENTWISE_BLOCK_BYTES = 1024 * 1024

WEIGHTS = ['g_pre', 'w_in', 's5_a_re', 's5_a_im', 's5_log_dt', 's5_b_re', 's5_b_im', 's5_c_re', 's5_c_im',
           's5_d', 'w_glu', 'w_bs', 'conv_w', 'conv_b', 'lru_w_a', 'lru_b_a', 'lru_w_x', 'lru_b_x',
           'lru_lambda', 'w_bl', 'w_out', 'g_post', 'w_ple', 'w_ple_gate']
SHARDED = {'w_in': 2, 'w_glu': 2, 'w_bs': 2, 'conv_w': 2, 'w_bl': 1, 'w_out': 1, 'w_ple': 2, 'w_ple_gate': 1}
GATHER_BF16 = ['w_in', 'w_glu', 'w_bs', 'w_bl', 'w_out', 'w_ple', 'w_ple_gate']
REPLICATED = [n for n in WEIGHTS if n not in SHARDED]


def _sig(x):
    return 0.5 * jnp.tanh(0.5 * x) + 0.5


def _gelu_parts(x):
    k = math.sqrt(2.0 / math.pi)
    t = jnp.tanh(k * (x + 0.044715 * x * x * x))
    return t, k


def _gelu(x):
    t, _ = _gelu_parts(x)
    return 0.5 * x * (1.0 + t)


def _gelu_grad(x):
    t, k = _gelu_parts(x)
    return 0.5 * (1.0 + t) + 0.5 * x * (1.0 - t * t) * k * (1.0 + 3.0 * 0.044715 * x * x)


def _one_minus_sq(a, log_a):
    z = 2.0 * log_a
    series = -z * (1.0 + z * (0.5 + z * (1.0 / 6.0 + z * (1.0 / 24.0 + z * (1.0 / 120.0)))))
    return jnp.where(z > -0.05, series, 1.0 - a * a)


def _softplus_neg(lam):
    return jnp.maximum(-lam, 0.0) + jnp.log(1.0 + jnp.exp(-jnp.abs(lam)))


def _dot(a, b):
    return jnp.dot(a, b, preferred_element_type=f32)


def _dot_nt(a, b):
    return lax.dot_general(a, b, (((1,), (1,)), ((), ())), preferred_element_type=f32)


def _dot_tn(a, b):
    return lax.dot_general(a, b, (((0,), (0,)), ((), ())), preferred_element_type=f32)


def _S(shape, dtype=f32):
    return jax.ShapeDtypeStruct(shape, dtype)


def _full(shape):
    nd = len(shape)
    return pl.BlockSpec(shape, lambda *_: (0,) * nd)


def _rows(tile, width, col=0):
    return pl.BlockSpec((tile, width), lambda i: (i, col))


def _cp(*semantics):
    return pltpu.CompilerParams(dimension_semantics=semantics or None, vmem_limit_bytes=VMEM_LIMIT_BYTES)


def _tile(n, want):
    t = min(n, want)
    assert n % t == 0, (n, want)
    return t


def _row_tile(R, C=LANES):
    cap = max(SUBLANES, min(R, ELEMENTWISE_BLOCK_BYTES // (4 * C)))
    for t in range(cap - cap % SUBLANES, 0, -SUBLANES):
        if R % t == 0:
            return t
    return R


def _lanes(j):
    return slice(LANES * j, LANES * (j + 1))


def _step_rows(k, n=SUBLANES):
    return pl.ds(pl.multiple_of(k * n, n), n)


def to_scan_order(a):
    T, C = a.shape
    tc = _tile(T, SCAN_CHUNK)
    return a.reshape(T // tc, SEGS, tc // SEGS, C).transpose(0, 2, 1, 3).reshape(T, C)


def from_scan_order(a):
    T, C = a.shape
    tc = _tile(T, SCAN_CHUNK)
    return a.reshape(T // tc, tc // SEGS, SEGS, C).transpose(0, 2, 1, 3).reshape(T, C)


def _col_tile(n, cap):
    if n <= cap:
        return n
    for t in range(cap - cap % LANES, 0, -LANES):
        if n % t == 0:
            return t
    return n


def _resident(shape):
    nd = len(shape)
    return pl.BlockSpec(shape, lambda *_: (0,) * nd, pipeline_mode=pl.Buffered(1))


def mm_nn(a, b, name, out_dtype=f32):
    M, K = a.shape
    N = b.shape[1]
    tm, tn = _tile(M, MM_TILE_M), _col_tile(N, MM_TILE_N)

    def body(a_ref, b_ref, o_ref):
        o_ref[...] = _dot(a_ref[...].astype(bf16), b_ref[...].astype(bf16)).astype(out_dtype)

    return pl.pallas_call(
        body, name=name, grid=(M // tm, N // tn),
        in_specs=[pl.BlockSpec((tm, K), lambda i, j: (i, 0)), pl.BlockSpec((K, tn), lambda i, j: (0, j))],
        out_specs=pl.BlockSpec((tm, tn), lambda i, j: (i, j)),
        out_shape=_S((M, N), out_dtype), compiler_params=_cp("parallel", "parallel"),
    )(a, b)


def mm_tn(a, b, name):
    M, K = a.shape
    N = b.shape[1]
    tm, tk, tn = _tile(M, MM_TILE_M), _col_tile(K, MM_TILE_K_ROWS), _col_tile(N, MM_TILE_N)

    def body(a_ref, b_ref, o_ref):
        m = pl.program_id(2)
        part = _dot_tn(a_ref[...].astype(bf16), b_ref[...].astype(bf16))

        @pl.when(m == 0)
        def _():
            o_ref[...] = part

        @pl.when(m > 0)
        def _():
            o_ref[...] += part

    return pl.pallas_call(
        body, name=name, grid=(K // tk, N // tn, M // tm),
        in_specs=[pl.BlockSpec((tm, tk), lambda i, j, m: (m, i)), pl.BlockSpec((tm, tn), lambda i, j, m: (m, j))],
        out_specs=pl.BlockSpec((tk, tn), lambda i, j, m: (i, j)),
        out_shape=_S((K, N), f32),
        compiler_params=_cp("parallel", "parallel", "arbitrary"),
    )(a, b)


def rms_fwd(x, g, name):
    T = x.shape[0]
    tm = _tile(T, TOKEN_TILE_LIGHT)

    def body(x_ref, g_ref, h_ref):
        xv = x_ref[...]
        r = lax.rsqrt(jnp.mean(xv * xv, axis=-1, keepdims=True) + NORM_EPS)
        h_ref[...] = (xv * r * g_ref[...]).astype(bf16)

    return pl.pallas_call(
        body, name=name, grid=(T // tm,),
        in_specs=[_rows(tm, D_MODEL), _full((1, D_MODEL))], out_specs=_rows(tm, D_MODEL),
        out_shape=_S((T, D_MODEL), bf16), compiler_params=_cp("parallel"),
    )(x, g)


def _s5_discretise(a_re, a_im, log_dt, b_re_t, b_im_t):
    dt = jnp.exp(log_dt)
    mag = jnp.exp(a_re * dt)
    ab_re = mag * jnp.cos(a_im * dt)
    ab_im = mag * jnp.sin(a_im * dt)
    den = a_re * a_re + a_im * a_im
    nr, ni = ab_re - 1.0, ab_im
    z_re = (nr * a_re + ni * a_im) / den
    z_im = (ni * a_re - nr * a_im) / den
    bb_re = z_re[None] * b_re_t - z_im[None] * b_im_t
    bb_im = z_re[None] * b_im_t + z_im[None] * b_re_t
    return ab_re, ab_im, bb_re, bb_im


def s5_prep(a_re, a_im, log_dt, b_re_t, b_im_t, m, name):
    G, N = a_re.shape

    def body(are_ref, aim_ref, ldt_ref, bre_ref, bim_ref, ab_ref, pw_ref, bb_ref):
        are, aim, ldt = are_ref[...], aim_ref[...], ldt_ref[...]
        ab_re, ab_im, bb_re, bb_im = _s5_discretise(are, aim, ldt, bre_ref[...], bim_ref[...])
        ab_ref[0], ab_ref[1] = ab_re, ab_im
        bb_ref[0], bb_ref[1] = bb_re, bb_im
        dt = jnp.exp(ldt)
        for k in range(m):
            mag = jnp.exp(are * dt * (k + 1.0))
            pw_ref[0, k] = mag * jnp.cos(aim * dt * (k + 1.0))
            pw_ref[1, k] = mag * jnp.sin(aim * dt * (k + 1.0))

    return pl.pallas_call(
        body, name=name,
        out_shape=[_S((2, G, N)), _S((2, m, G, N)), _S((2, S5_GROUP, G, N))], compiler_params=_cp(),
    )(a_re, a_im, log_dt, b_re_t, b_im_t)


def s5_prep_bwd(a_re, a_im, log_dt, b_re_t, b_im_t, g_ab, g_bb, name):
    G, N = a_re.shape

    def body(are_ref, aim_ref, ldt_ref, bre_ref, bim_ref, gab_ref, gbb_ref, o_are, o_aim, o_ldt, o_bre, o_bim):
        _, vjp = jax.vjp(_s5_discretise, are_ref[...], aim_ref[...], ldt_ref[...], bre_ref[...], bim_ref[...])
        g_are, g_aim, g_ldt, g_bre, g_bim = vjp((gab_ref[0], gab_ref[1], gbb_ref[0], gbb_ref[1]))
        o_are[...], o_aim[...], o_ldt[...], o_bre[...], o_bim[...] = g_are, g_aim, g_ldt, g_bre, g_bim

    return pl.pallas_call(
        body, name=name,
        out_shape=[_S((G, N)), _S((G, N)), _S((G, 1)), _S((S5_GROUP, G, N)), _S((S5_GROUP, G, N))],
        compiler_params=_cp(),
    )(a_re, a_im, log_dt, b_re_t, b_im_t, g_ab, g_bb)


NB_S5 = S5_NS // LANES
CB_S5 = S5_WIDTH // LANES
SB_PER_CB = NB_S5 // CB_S5
GRP_PER_SB = LANES // S5_STATE
S5_JB = 8


def _bdb_mask():
    j = jnp.arange(NB_S5)
    own_rows = (j[:, None] % SB_PER_CB == jnp.arange(SB_PER_CB)[None, :]).astype(f32)
    eye = jnp.eye(GRP_PER_SB, dtype=f32)
    return own_rows[:, :, None, None, None, None, None] * eye[None, None, :, None, None, :, None]


def _pack_bdb(bb):
    v = jnp.transpose(bb.reshape(2, S5_GROUP, NB_S5, GRP_PER_SB, S5_STATE), (2, 3, 1, 0, 4))
    full = v[:, None, :, :, :, None, :] * _bdb_mask()
    return full.reshape(NB_S5, LANES, 2 * LANES)


def _unpack_bdb(g_bdb):
    g7 = g_bdb.reshape(NB_S5, SB_PER_CB, GRP_PER_SB, S5_GROUP, 2, GRP_PER_SB, S5_STATE)
    v = jnp.sum(g7 * _bdb_mask(), axis=(1, 5))
    return jnp.transpose(v, (3, 2, 0, 1, 4)).reshape(2, S5_GROUP, S5_GROUPS, S5_STATE)


def _pack_cdb(c_re, c_im):
    gl = S5_GROUPS // CB_S5
    c2 = jnp.stack([c_re, -c_im]).reshape(2, CB_S5, gl, S5_GROUP, S5_STATE)
    eye = jnp.eye(gl, dtype=f32)
    full = jnp.transpose(c2, (1, 0, 2, 4, 3))[:, :, :, :, None, :] * eye[None, None, :, None, :, None]
    return full.reshape(CB_S5, 2 * SB_PER_CB * LANES, LANES)


def _unpack_cdb(g_cdb):
    gl = S5_GROUPS // CB_S5
    g6 = g_cdb.reshape(CB_S5, 2, gl, S5_STATE, gl, S5_GROUP)
    eye = jnp.eye(gl, dtype=f32)
    v = jnp.sum(g6 * eye[None, None, :, None, :, None], axis=4)
    v = jnp.transpose(v, (1, 0, 2, 4, 3)).reshape(2, S5_GROUPS, S5_GROUP, S5_STATE)
    return v[0], -v[1]


def _state_cat(ref, c):
    w = SB_PER_CB * LANES
    return jnp.concatenate([ref[:, w * c:w * (c + 1)], ref[:, S5_NS + w * c:S5_NS + w * (c + 1)]], axis=1)


def _state_pair(ref, j):
    return jnp.concatenate([ref[:, _lanes(j)], ref[:, S5_NS + LANES * j:S5_NS + LANES * (j + 1)]], axis=1)


def s5_fwd(usg, bdb, cdb, dvec, abar_b, ptab_b, name, gather=()):
    T = usg.shape[0]
    tc = _tile(T, SCAN_CHUNK)
    m = tc // SEGS
    nsteps = T // tc
    ng = len(gather)
    assert ptab_b.shape == (2, m, SEGS, S5_NS) and m % 2 == 0

    def body(*refs):
        u_ref, bdb_ref, cdb_ref, d_ref, a_ref, p_ref = refs[:6]
        ys_ref, sre_ref, sim_ref, sbf_ref = refs[6 + ng:10 + ng]
        src_re, src_im, dst_re, dst_im, cin_ref, carry_ref = refs[10 + 2 * ng:16 + 2 * ng]
        i = pl.program_id(0)
        if ng:
            phases = _gather_phases([s.shape for s in gather], refs[6:6 + ng], refs[10 + ng:10 + 2 * ng],
                                    *refs[16 + 2 * ng:])
            for phase, step in zip(phases, (0, nsteps // 2, (3 * nsteps) // 4, nsteps - 1)):
                pl.when(i == step)(phase)

        @pl.when(i == 0)
        def _():
            carry_ref[...] = jnp.zeros_like(carry_ref)

        u = u_ref[...]
        ub = u.astype(bf16)
        for j in range(NB_S5):
            bu = _dot(ub[:, _lanes(j // SB_PER_CB)], bdb_ref[j])
            src_re[:, _lanes(j)] = bu[:, :LANES]
            src_im[:, _lanes(j)] = bu[:, LANES:]
        for j0 in range(0, NB_S5, S5_JB):
            def kstep(k, st):
                rows = _step_rows(k)
                out = []
                for q in range(S5_JB):
                    ln = _lanes(j0 + q)
                    sr, si = st[2 * q], st[2 * q + 1]
                    ar, ai = a_ref[0, :, ln], a_ref[1, :, ln]
                    nr = ar * sr - ai * si + src_re[rows, ln]
                    ni = ar * si + ai * sr + src_im[rows, ln]
                    dst_re[rows, ln] = nr
                    dst_im[rows, ln] = ni
                    out += [nr, ni]
                return tuple(out)

            ends = lax.fori_loop(0, m, kstep, tuple(jnp.zeros((SEGS, LANES), f32) for _ in range(2 * S5_JB)))
            for q in range(S5_JB):
                ln = _lanes(j0 + q)
                er, ei = ends[2 * q], ends[2 * q + 1]
                cr, ci = carry_ref[0, :, ln], carry_ref[1, :, ln]
                amr, ami = p_ref[0, m - 1, 0:1, ln], p_ref[1, m - 1, 0:1, ln]
                rows_r, rows_i = [], []
                for s in range(SEGS):
                    rows_r.append(cr)
                    rows_i.append(ci)
                    cr, ci = (er[s:s + 1, :] + amr * cr - ami * ci, ei[s:s + 1, :] + amr * ci + ami * cr)
                cin_ref[0, 0:SEGS, ln] = _stack_rows(rows_r)
                cin_ref[1, 0:SEGS, ln] = _stack_rows(rows_i)
                carry_ref[0, :, ln] = cr
                carry_ref[1, :, ln] = ci
        cin_ref[:, SEGS:, :] = cin_ref[:, 0:SEGS, :]

        def fix(k2, _):
            rows = _step_rows(k2, PAIR)
            pr = p_ref[0, pl.ds(2 * k2, 2)].reshape(PAIR, S5_NS)
            pi = p_ref[1, pl.ds(2 * k2, 2)].reshape(PAIR, S5_NS)
            cr, ci = cin_ref[0], cin_ref[1]
            sr = dst_re[rows, :] + pr * cr - pi * ci
            si = dst_im[rows, :] + pr * ci + pi * cr
            sre_ref[rows, :] = sr
            sim_ref[rows, :] = si
            sbf_ref[rows, 0:S5_NS] = sr.astype(bf16)
            sbf_ref[rows, S5_NS:] = si.astype(bf16)
            return 0

        lax.fori_loop(0, m // 2, fix, 0)
        for c in range(CB_S5):
            ys_ref[:, _lanes(c)] = _dot(_state_cat(sbf_ref, c), cdb_ref[c]) + d_ref[:, _lanes(c)] * u[:, _lanes(c)]

    st = lambda w: _rows(tc, w)
    outs = pl.pallas_call(
        body, name=name, grid=(nsteps,),
        in_specs=[_rows(tc, S5_WIDTH, 0), _resident(bdb.shape), _resident(cdb.shape), _full((1, S5_WIDTH)),
                  _resident((2, SEGS, S5_NS)), _resident((2, m, SEGS, S5_NS))] + [ANY] * ng,
        out_specs=[st(S5_WIDTH), st(S5_NS), st(S5_NS), st(2 * S5_NS)] + [ANY] * ng,
        out_shape=[_S((T, S5_WIDTH)), _S((T, S5_NS)), _S((T, S5_NS)), _S((T, 2 * S5_NS), bf16)] + (
            _gather_out_shapes(gather) if ng else []),
        scratch_shapes=[pltpu.VMEM((tc, S5_NS), f32)] * 4 + [pltpu.VMEM((2, PAIR, S5_NS), f32),
                                                             pltpu.VMEM((2, 1, S5_NS), f32)] + (
            _gather_semaphores(ng) if ng else []),
        compiler_params=_cp("arbitrary"),
    )(usg, bdb, cdb, dvec, abar_b, ptab_b, *gather)
    return outs[:4], list(outs[4:])


def s5_bwd(gys, usg, s_re, s_im, s_bf, bdb, cdb, dvec, abar_b, ptab_rev_b, name):
    T = gys.shape[0]
    tc = _tile(T, SCAN_CHUNK)
    m = tc // SEGS
    nch = T // tc
    hb = tc // SUBLANES

    def body(gy_ref, u_ref, sre_ref, sim_ref, hre_ref, him_ref, sbf_ref, bdb_ref, cdb_ref, d_ref, a_ref, p_ref,
             gu_ref, gab_ref, gd_ref, gbdb_ref, gcdb_ref,
             src_re, src_im, dst_re, dst_im, lam_ref, cin_ref, acc_ref, carry_ref):
        i = pl.program_id(0)

        @pl.when(i == 0)
        def _():
            carry_ref[...] = jnp.zeros_like(carry_ref)
            for ref in (gab_ref, gd_ref, gbdb_ref, gcdb_ref):
                ref[...] = jnp.zeros_like(ref)

        first = i == nch - 1
        gy = gy_ref[...]
        gyb = gy.astype(bf16)
        u = u_ref[...]
        ub = u.astype(bf16)
        w = SB_PER_CB * LANES
        for c in range(CB_S5):
            gs = _dot_nt(gyb[:, _lanes(c)], cdb_ref[c])
            src_re[:, w * c:w * (c + 1)] = gs[:, :w]
            src_im[:, w * c:w * (c + 1)] = gs[:, w:]
            gcdb_ref[c] += _dot_tn(_state_cat(sbf_ref, c), gyb[:, _lanes(c)])
        for j0 in range(0, NB_S5, S5_JB):
            def kstep(kk, st):
                rows = _step_rows(m - 1 - kk)
                out = []
                for q in range(S5_JB):
                    ln = _lanes(j0 + q)
                    lr, li = st[2 * q], st[2 * q + 1]
                    ar, ai = a_ref[0, :, ln], a_ref[1, :, ln]
                    nr = ar * lr + ai * li + src_re[rows, ln]
                    ni = ar * li - ai * lr + src_im[rows, ln]
                    dst_re[rows, ln] = nr
                    dst_im[rows, ln] = ni
                    out += [nr, ni]
                return tuple(out)

            ends = lax.fori_loop(0, m, kstep, tuple(jnp.zeros((SEGS, LANES), f32) for _ in range(2 * S5_JB)))
            for q in range(S5_JB):
                ln = _lanes(j0 + q)
                er, ei = ends[2 * q], ends[2 * q + 1]
                cr, ci = carry_ref[0, :, ln], carry_ref[1, :, ln]
                amr, ami = p_ref[0, 0, 0:1, ln], p_ref[1, 0, 0:1, ln]
                rows_r, rows_i = [None] * SEGS, [None] * SEGS
                for s in reversed(range(SEGS)):
                    rows_r[s], rows_i[s] = cr, ci
                    cr, ci = (er[s:s + 1, :] + amr * cr + ami * ci, ei[s:s + 1, :] + amr * ci - ami * cr)
                cin_ref[0, 0:SEGS, ln] = _stack_rows(rows_r)
                cin_ref[1, 0:SEGS, ln] = _stack_rows(rows_i)
                carry_ref[0, :, ln] = cr
                carry_ref[1, :, ln] = ci
        cin_ref[:, SEGS:, :] = cin_ref[:, 0:SEGS, :]
        acc_ref[...] = jnp.zeros_like(acc_ref)

        def fix_rows(rows, k2, prev_re, prev_im):
            pr = p_ref[0, pl.ds(2 * k2, 2)].reshape(PAIR, S5_NS)
            pi = p_ref[1, pl.ds(2 * k2, 2)].reshape(PAIR, S5_NS)
            cr, ci = cin_ref[0], cin_ref[1]
            lr = dst_re[rows, :] + pr * cr + pi * ci
            li = dst_im[rows, :] + pr * ci - pi * cr
            lam_ref[rows, 0:S5_NS] = lr.astype(bf16)
            lam_ref[rows, S5_NS:] = li.astype(bf16)
            acc_ref[0] += lr * prev_re + li * prev_im
            acc_ref[1] += li * prev_re - lr * prev_im

        last = slice(tc - SUBLANES, tc)
        wrap_re = _down_a_segment(sre_ref[last, :], jnp.where(first, 0.0, hre_ref[SUBLANES - 1:SUBLANES, :]))
        wrap_im = _down_a_segment(sim_ref[last, :], jnp.where(first, 0.0, him_ref[SUBLANES - 1:SUBLANES, :]))
        fix_rows(pl.ds(0, PAIR), 0, jnp.concatenate([wrap_re, sre_ref[0:SUBLANES, :]], axis=0),
                 jnp.concatenate([wrap_im, sim_ref[0:SUBLANES, :]], axis=0))

        def fix(k2, _):
            prev = pl.ds(pl.multiple_of(k2 * PAIR - SUBLANES, SUBLANES), PAIR)
            fix_rows(_step_rows(k2, PAIR), k2, sre_ref[prev, :], sim_ref[prev, :])
            return 0

        lax.fori_loop(1, m // 2, fix, 0)
        gab_ref[0] += jnp.sum(acc_ref[0], axis=0, keepdims=True)
        gab_ref[1] += jnp.sum(acc_ref[1], axis=0, keepdims=True)
        for c in range(CB_S5):
            x = gy[:, _lanes(c)] * d_ref[:, _lanes(c)]
            for j in range(SB_PER_CB * c, SB_PER_CB * (c + 1)):
                pair = _state_pair(lam_ref, j)
                x = x + _dot_nt(pair, bdb_ref[j])
                gbdb_ref[j] += _dot_tn(ub[:, _lanes(c)], pair)
            gu_ref[:, _lanes(c)] = x.astype(bf16)
        gd_ref[...] += jnp.sum(gy * u, axis=0, keepdims=True)

    rev = lambda i: (nch - 1 - i, 0)
    halo = lambda i: (jnp.maximum((nch - 1 - i) * hb - 1, 0), 0)
    blk = lambda wd: pl.BlockSpec((tc, wd), rev)
    return pl.pallas_call(
        body, name=name, grid=(nch,),
        in_specs=[blk(S5_WIDTH), blk(S5_WIDTH), blk(S5_NS), blk(S5_NS),
                  pl.BlockSpec((SUBLANES, S5_NS), halo), pl.BlockSpec((SUBLANES, S5_NS), halo), blk(2 * S5_NS),
                  _resident(bdb.shape), _resident(cdb.shape), _full((1, S5_WIDTH)),
                  _resident((2, SEGS, S5_NS)), _resident((2, m, SEGS, S5_NS))],
        out_specs=[blk(S5_WIDTH), _full((2, 1, S5_NS)), _full((1, S5_WIDTH)), _full(bdb.shape), _full(cdb.shape)],
        out_shape=[_S((T, S5_WIDTH), bf16), _S((2, 1, S5_NS)), _S((1, S5_WIDTH)), _S(bdb.shape), _S(cdb.shape)],
        scratch_shapes=[pltpu.VMEM((tc, S5_NS), f32)] * 4 + [
            pltpu.VMEM((tc, 2 * S5_NS), bf16), pltpu.VMEM((2, PAIR, S5_NS), f32), pltpu.VMEM((2, PAIR, S5_NS), f32),
            pltpu.VMEM((2, 1, S5_NS), f32)],
        compiler_params=_cp("arbitrary"),
    )(gys, usg, s_re, s_im, s_re, s_im, s_bf, bdb, cdb, dvec, abar_b, ptab_rev_b)


def s5_post_fwd(ys, usg, wglu, wbs, name):
    T = ys.shape[0]
    tm = _tile(T, TOKEN_TILE_LIGHT)

    def body(ys_ref, sg_ref, wglu_ref, wbs_ref, glu_ref, zs_ref):
        glu = _dot(_gelu(ys_ref[...]).astype(bf16), wglu_ref[...])
        sg = sg_ref[...]
        y2 = glu[:, :S5_WIDTH] * _sig(glu[:, S5_WIDTH:]) * (sg * _sig(sg))
        glu_ref[...] = glu
        zs_ref[...] = _dot(y2.astype(bf16), wbs_ref[...])

    return pl.pallas_call(
        body, name=name, grid=(T // tm,),
        in_specs=[_rows(tm, S5_WIDTH), _rows(tm, S5_WIDTH, 1), _resident((S5_WIDTH, 2 * S5_WIDTH)),
                  _resident((S5_WIDTH, D_MODEL))],
        out_specs=[_rows(tm, 2 * S5_WIDTH), _rows(tm, D_MODEL)],
        out_shape=[_S((T, 2 * S5_WIDTH)), _S((T, D_MODEL))], compiler_params=_cp("parallel"),
    )(ys, usg, wglu, wbs)


def _accumulate(ref, part, step):
    @pl.when(step == 0)
    def _():
        ref[...] = part

    @pl.when(step > 0)
    def _():
        ref[...] += part


def s5_post_bwd(gzs, glu, usg, ys, wbs, wglu, name):
    T = ys.shape[0]
    tm = _tile(T, TOKEN_TILE_LIGHT)

    def body(gzs_ref, glu_ref, sg_ref, ys_ref, wbs_ref, wglu_ref, gys_ref, gsg_ref, gwbs_ref, gwglu_ref):
        i = pl.program_id(0)
        glu = glu_ref[...]
        a, b = glu[:, :S5_WIDTH], glu[:, S5_WIDTH:]
        sg = sg_ref[...]
        ys = ys_ref[...]
        sb, ssg = _sig(b), _sig(sg)
        silu = sg * ssg
        _accumulate(gwbs_ref, _dot_tn((a * sb * silu).astype(bf16), gzs_ref[...]), i)
        gy2 = _dot_nt(gzs_ref[...], wbs_ref[...])
        g_a = gy2 * sb * silu
        g_b = gy2 * a * sb * (1.0 - sb) * silu
        gsg_ref[...] = (gy2 * a * sb * ssg * (1.0 + sg * (1.0 - ssg))).astype(bf16)
        gglu = jnp.concatenate([g_a, g_b], axis=1).astype(bf16)
        _accumulate(gwglu_ref, _dot_tn(_gelu(ys).astype(bf16), gglu), i)
        gys_ref[...] = _dot_nt(gglu, wglu_ref[...]) * _gelu_grad(ys)

    return pl.pallas_call(
        body, name=name, grid=(T // tm,),
        in_specs=[_rows(tm, D_MODEL), _rows(tm, 2 * S5_WIDTH), _rows(tm, S5_WIDTH, 1), _rows(tm, S5_WIDTH),
                  _resident((S5_WIDTH, D_MODEL)), _resident((S5_WIDTH, 2 * S5_WIDTH))],
        out_specs=[_rows(tm, S5_WIDTH), _rows(tm, S5_WIDTH), _full((S5_WIDTH, D_MODEL)), _full((S5_WIDTH, 2 * S5_WIDTH))],
        out_shape=[_S((T, S5_WIDTH)), _S((T, S5_WIDTH), bf16), _S((S5_WIDTH, D_MODEL)), _S((S5_WIDTH, 2 * S5_WIDTH))],
        compiler_params=_cp("arbitrary"),
    )(gzs, glu, usg, ys, wbs, wglu)


NB_LRU = LRU_WIDTH // LANES
LRU_JB = 5
TAPS_BACK = CONV_WIDTH - 1
EDGE = TAPS_BACK * SUBLANES
HALO_ROWS = 4 * SUBLANES


def _down_a_segment(blk, entering_row):
    sub = lax.broadcasted_iota(jnp.int32, blk.shape, 0)
    return jnp.where(sub == 0, entering_row, pltpu.roll(blk, 1, 0))


def _up_a_segment(blk, entering_row):
    sub = lax.broadcasted_iota(jnp.int32, blk.shape, 0)
    return jnp.where(sub == SUBLANES - 1, entering_row, pltpu.roll(blk, SUBLANES - 1, 0))


def _stack_rows(rows):
    sub = lax.broadcasted_iota(jnp.int32, (SUBLANES,) + rows[0].shape[1:], 0)
    out = jnp.broadcast_to(rows[0], sub.shape)
    for s in range(1, SUBLANES):
        out = jnp.where(sub == s, rows[s], out)
    return out


def _fill_conv_window(xe, x_ref, xh_ref, is_first, tc):
    xe[EDGE:, :] = x_ref[...]
    for i in range(1, TAPS_BACK + 1):
        row = HALO_ROWS - SUBLANES * i + SUBLANES - 1
        entering = jnp.where(is_first, 0.0, xh_ref[row:row + 1, :])
        blk = x_ref[tc - SUBLANES * i:tc - SUBLANES * (i - 1), :]
        xe[EDGE - SUBLANES * i:EDGE - SUBLANES * (i - 1), :] = _down_a_segment(blk, entering)


def lru_fwd(lx, convw, convb, wa, wx, ba, bx, lam, name):
    T = lx.shape[0]
    tc = _tile(T, SCAN_CHUNK)
    m = tc // SEGS
    hb = tc // HALO_ROWS

    def body(x_ref, xh_ref, cw_ref, cb_ref, wa_ref, wx_ref, ba_ref, bx_ref, lam_ref,
             c_ref, r_ref, i_ref, h_ref, xe, src_a, src_b, dst_a, dst_h, cin_ref, carry_ref):
        i = pl.program_id(0)

        @pl.when(i == 0)
        def _():
            carry_ref[...] = jnp.zeros_like(carry_ref)

        _fill_conv_window(xe, x_ref, xh_ref, i == 0, tc)
        c = cb_ref[...] + cw_ref[0:1, :] * xe[0:tc, :]
        for k in range(1, CONV_WIDTH):
            c = c + cw_ref[k:k + 1, :] * xe[SUBLANES * k:SUBLANES * k + tc, :]
        c_ref[...] = c
        sp = _softplus_neg(lam_ref[...])
        for j in range(NB_LRU):
            ln = _lanes(j)
            cj = c[:, ln]
            cjb = cj.astype(bf16)
            r = _sig(_dot(cjb, wa_ref[j]) + ba_ref[:, ln])
            g = _sig(_dot(cjb, wx_ref[j]) + bx_ref[:, ln])
            r_ref[:, ln] = r
            i_ref[:, ln] = g
            log_a = -LRU_C * r * sp[:, ln]
            a = jnp.exp(log_a)
            src_a[:, ln] = a
            src_b[:, ln] = jnp.sqrt(_one_minus_sq(a, log_a)) * (g * cj)
        for j0 in range(0, NB_LRU, LRU_JB):
            def kstep(k, st):
                rows = _step_rows(k)
                out = []
                for q in range(LRU_JB):
                    ln = _lanes(j0 + q)
                    hh, ac = st[2 * q], st[2 * q + 1]
                    a = src_a[rows, ln]
                    hh = a * hh + src_b[rows, ln]
                    ac = a * ac
                    dst_h[rows, ln] = hh
                    dst_a[rows, ln] = ac
                    out += [hh, ac]
                return tuple(out)

            init = tuple(jnp.zeros((SEGS, LANES), f32) if q % 2 == 0 else jnp.ones((SEGS, LANES), f32)
                         for q in range(2 * LRU_JB))
            ends = lax.fori_loop(0, m, kstep, init)
            for q in range(LRU_JB):
                ln = _lanes(j0 + q)
                eh, ea = ends[2 * q], ends[2 * q + 1]
                cr = carry_ref[:, ln]
                rows_c = []
                for s in range(SEGS):
                    rows_c.append(cr)
                    cr = eh[s:s + 1, :] + ea[s:s + 1, :] * cr
                cin_ref[:, ln] = _stack_rows(rows_c)
                carry_ref[:, ln] = cr

        def fix(k, _):
            rows = _step_rows(k)
            h_ref[rows, :] = dst_h[rows, :] + dst_a[rows, :] * cin_ref[...]
            return 0

        lax.fori_loop(0, m, fix, 0)

    wide = lambda: _rows(tc, LRU_WIDTH)
    buf = lambda rows: pltpu.VMEM((rows, LRU_WIDTH), f32)
    return pl.pallas_call(
        body, name=name, grid=(T // tc,),
        in_specs=[wide(), pl.BlockSpec((HALO_ROWS, LRU_WIDTH), lambda i: (jnp.maximum(i * hb - 1, 0), 0)),
                  _full((CONV_WIDTH, LRU_WIDTH)), _full((1, LRU_WIDTH)),
                  _full((LRU_HEADS, LRU_HEAD_DIM, LRU_HEAD_DIM)), _full((LRU_HEADS, LRU_HEAD_DIM, LRU_HEAD_DIM)),
                  _full((1, LRU_WIDTH)), _full((1, LRU_WIDTH)), _full((1, LRU_WIDTH))],
        out_specs=[wide(), wide(), wide(), wide()],
        out_shape=[_S((T, LRU_WIDTH))] * 4,
        scratch_shapes=[buf(tc + EDGE), buf(tc), buf(tc), buf(tc), buf(tc), buf(SEGS), buf(1)],
        compiler_params=_cp("arbitrary"),
    )(lx, lx, convw, convb, wa, wx, ba, bx, lam)


def lru_bwd(gh, h, c, r, gi, lx, convw, wa, wx, lam, name, exchange=()):
    T = gh.shape[0]
    tc = _tile(T, SCAN_CHUNK)
    m = tc // SEGS
    nch = T // tc
    ne = len(exchange)

    def body(*refs):
        gh_ref, h_ref, hh_ref, c_ref, r_ref, i_ref, x_ref, xh_ref, cw_ref, wa_ref, wx_ref, lam_ref = refs[:12]
        glx_ref, gwa_ref, gwx_ref, gba_ref, gbx_ref, glam_ref, gcb_ref, gcw_ref = refs[12 + ne:20 + ne]
        src_a, src_m, dst_a, dst_m, mbuf, hbuf, xe, gce, cin_ref, gcc_ref, carry_ref = refs[20 + 2 * ne:31 + 2 * ne]
        i = pl.program_id(0)
        if ne:
            start, finish = _chips_phases(refs[12:12 + ne], refs[20 + ne:20 + 2 * ne], *refs[31 + 2 * ne:])
            pl.when(i == 0)(start)
            pl.when(i == nch - 1)(finish)

        @pl.when(i == 0)
        def _():
            carry_ref[...] = jnp.zeros_like(carry_ref)
            gcc_ref[...] = jnp.zeros_like(gcc_ref)
            for ref in (gwa_ref, gwx_ref, gba_ref, gbx_ref, glam_ref, gcb_ref, gcw_ref):
                ref[...] = jnp.zeros_like(ref)

        first = i == nch - 1
        last = slice(tc - SUBLANES, tc)
        hbuf[SUBLANES:, :] = h_ref[...]
        hbuf[0:SUBLANES, :] = _down_a_segment(h_ref[last, :], jnp.where(first, 0.0, hh_ref[SUBLANES - 1:SUBLANES, :]))
        _fill_conv_window(xe, x_ref, xh_ref, first, tc)
        lam_v = lam_ref[...]
        sp = _softplus_neg(lam_v)
        a_all = jnp.exp(-LRU_C * r_ref[...] * sp)
        src_a[...] = a_all
        src_m[...] = a_all * gh_ref[...]
        for j0 in range(0, NB_LRU, LRU_JB):
            def kstep(kk, st):
                rows = _step_rows(m - 1 - kk)
                out = []
                for q in range(LRU_JB):
                    ln = _lanes(j0 + q)
                    mu, ac = st[2 * q], st[2 * q + 1]
                    a = src_a[rows, ln]
                    mu = a * mu + src_m[rows, ln]
                    ac = a * ac
                    dst_m[rows, ln] = mu
                    dst_a[rows, ln] = ac
                    out += [mu, ac]
                return tuple(out)

            init = tuple(jnp.zeros((SEGS, LANES), f32) if q % 2 == 0 else jnp.ones((SEGS, LANES), f32)
                         for q in range(2 * LRU_JB))
            ends = lax.fori_loop(0, m, kstep, init)
            for q in range(LRU_JB):
                ln = _lanes(j0 + q)
                em, ea = ends[2 * q], ends[2 * q + 1]
                cr = carry_ref[:, ln]
                rows_c = [None] * SEGS
                for s in reversed(range(SEGS)):
                    rows_c[s] = cr
                    cr = em[s:s + 1, :] + ea[s:s + 1, :] * cr
                cin_ref[:, ln] = _stack_rows(rows_c)
                carry_ref[:, ln] = cr

        def fix(k, _):
            rows = _step_rows(k)
            mbuf[rows, :] = dst_m[rows, :] + dst_a[rows, :] * cin_ref[...]
            return 0

        lax.fori_loop(0, m, fix, 0)
        mbuf[tc:, :] = _up_a_segment(mbuf[0:SUBLANES, :], cin_ref[SUBLANES - 1:SUBLANES, :])
        sneg = _sig(-lam_v)
        for j in range(NB_LRU):
            ln = _lanes(j)
            lamt = gh_ref[:, ln] + mbuf[SUBLANES:, ln]
            rj, ij, cj = r_ref[:, ln], i_ref[:, ln], c_ref[:, ln]
            log_a = -LRU_C * rj * sp[:, ln]
            a = src_a[:, ln]
            om = _one_minus_sq(a, log_a)
            inv_mult = lax.rsqrt(om)
            mult = om * inv_mult
            g_a = lamt * hbuf[0:tc, ln]
            g_mult = lamt * ij * cj
            g_i = lamt * mult * cj
            g_c = lamt * mult * ij
            g_log_a = g_a * a - g_mult * a * a * inv_mult
            glam_ref[:, ln] += jnp.sum(g_log_a * rj, axis=0, keepdims=True) * LRU_C * sneg[:, ln]
            g_ra = g_log_a * (-LRU_C) * sp[:, ln] * rj * (1.0 - rj)
            g_ia = g_i * ij * (1.0 - ij)
            gba_ref[:, ln] += jnp.sum(g_ra, axis=0, keepdims=True)
            gbx_ref[:, ln] += jnp.sum(g_ia, axis=0, keepdims=True)
            cjb, grb, gib = cj.astype(bf16), g_ra.astype(bf16), g_ia.astype(bf16)
            gwa_ref[j] += _dot_tn(cjb, grb)
            gwx_ref[j] += _dot_tn(cjb, gib)
            g_c = g_c + _dot_nt(grb, wa_ref[j]) + _dot_nt(gib, wx_ref[j])
            gce[0:tc, ln] = g_c
            gcb_ref[:, ln] += jnp.sum(g_c, axis=0, keepdims=True)
        for d in range(TAPS_BACK):
            blk = slice(SUBLANES * d, SUBLANES * (d + 1))
            gce[tc + SUBLANES * d:tc + SUBLANES * (d + 1), :] = _up_a_segment(gce[blk, :], gcc_ref[SUBLANES * d:SUBLANES * d + 1, :])
        gcc_ref[...] = gce[0:EDGE, :]
        gc = gce[0:tc, :]
        glx = cw_ref[CONV_WIDTH - 1:CONV_WIDTH, :] * gc
        gcw_ref[CONV_WIDTH - 1:CONV_WIDTH, :] += jnp.sum(gc * xe[EDGE:EDGE + tc, :], axis=0, keepdims=True)
        for k in range(CONV_WIDTH - 1):
            off = SUBLANES * (CONV_WIDTH - 1 - k)
            glx = glx + cw_ref[k:k + 1, :] * gce[off:off + tc, :]
            gcw_ref[k:k + 1, :] += jnp.sum(gc * xe[EDGE - off:EDGE - off + tc, :], axis=0, keepdims=True)
        glx_ref[...] = glx.astype(bf16)

    rev = lambda i: (nch - 1 - i, 0)
    halo = lambda rows: (lambda i: (jnp.maximum((nch - 1 - i) * (tc // rows) - 1, 0), 0))
    wide = lambda: pl.BlockSpec((tc, LRU_WIDTH), rev)
    vec = lambda: _full((1, LRU_WIDTH))
    hd = lambda: _full((LRU_HEADS, LRU_HEAD_DIM, LRU_HEAD_DIM))
    buf = lambda rows: pltpu.VMEM((rows, LRU_WIDTH), f32)
    outs = pl.pallas_call(
        body, name=name, grid=(nch,),
        in_specs=[wide(), wide(), pl.BlockSpec((SUBLANES, LRU_WIDTH), halo(SUBLANES)), wide(), wide(), wide(), wide(),
                  pl.BlockSpec((HALO_ROWS, LRU_WIDTH), halo(HALO_ROWS)), _full((CONV_WIDTH, LRU_WIDTH)), hd(), hd(), vec()]
        + [ANY] * ne,
        out_specs=[wide(), hd(), hd(), vec(), vec(), vec(), vec(), _full((CONV_WIDTH, LRU_WIDTH))] + [ANY] * ne,
        out_shape=[_S((T, LRU_WIDTH), bf16), _S((LRU_HEADS, LRU_HEAD_DIM, LRU_HEAD_DIM)),
                   _S((LRU_HEADS, LRU_HEAD_DIM, LRU_HEAD_DIM)), _S((1, LRU_WIDTH)), _S((1, LRU_WIDTH)),
                   _S((1, LRU_WIDTH)), _S((1, LRU_WIDTH)), _S((CONV_WIDTH, LRU_WIDTH))] + (
            _chips_out_shapes(exchange) if ne else []),
        scratch_shapes=[buf(tc), buf(tc), buf(tc), buf(tc), buf(tc + SUBLANES), buf(tc + SUBLANES), buf(tc + EDGE),
                        buf(tc + EDGE), buf(SEGS), buf(EDGE), buf(1)] + (_chips_semaphores(ne) if ne else []),
        compiler_params=_cp("arbitrary"),
    )(gh, h, h, c, r, gi, lx, lx, convw, wa, wx, lam, *exchange)
    return outs[:8], list(outs[8:])


def merge_fwd(h, lg, zs, gsl, x, p, wbl, wout, gpost, wple, wpg, name):
    T = x.shape[0]
    tm = _tile(T, TOKEN_TILE)

    def body(h_ref, lg_ref, zs_ref, gs_ref, gl_ref, x_ref, p_ref, wbl_ref, wout_ref, gp_ref, wple_ref, wpg_ref,
             zl_ref, mix_ref, xo_ref):
        lg_v = lg_ref[...]
        yl = h_ref[...] * (lg_v * _sig(lg_v))
        zl = _dot(yl.astype(bf16), wbl_ref[...])
        merged = _sig(gs_ref[...]) * zs_ref[...] + _sig(gl_ref[...]) * zl
        mix = _dot(merged.astype(bf16), wout_ref[...])
        r2 = lax.rsqrt(jnp.mean(mix * mix, axis=-1, keepdims=True) + NORM_EPS)
        x1 = x_ref[...] + mix * r2 * gp_ref[...]
        q = _dot(x1.astype(bf16), wpg_ref[...])
        pe = _dot(p_ref[...].astype(bf16), wple_ref[...])
        zl_ref[...], mix_ref[...] = zl, mix
        xo_ref[...] = x1 + pe * _sig(q)

    dm = lambda: _rows(tm, D_MODEL)
    return pl.pallas_call(
        body, name=name, grid=(T // tm,),
        in_specs=[_rows(tm, LRU_WIDTH), _rows(tm, LRU_WIDTH), dm(), _rows(tm, D_MODEL, 0), _rows(tm, D_MODEL, 1), dm(),
                  _rows(tm, PLE_DIM), _resident((LRU_WIDTH, D_MODEL)), _resident((D_MODEL, D_MODEL)), _full((1, D_MODEL)),
                  _resident((PLE_DIM, D_MODEL)), _resident((D_MODEL, D_MODEL))],
        out_specs=[dm(), dm(), dm()],
        out_shape=[_S((T, D_MODEL))] * 3, compiler_params=_cp("parallel"),
    )(h, lg, zs, gsl, gsl, x, p, wbl, wout, gpost, wple, wpg)


def post_bwd(gx2, mix, x, p, wpg, wple, gpost, name):
    T = x.shape[0]
    tm = _tile(T, TOKEN_TILE_LIGHT)

    def body(gx2_ref, mix_ref, x_ref, p_ref, wpg_ref, wple_ref, gp_ref, gres_ref, gmix_ref, ggp_ref, gwpg_ref, gwple_ref):
        i = pl.program_id(0)
        gx2 = gx2_ref[...]
        mix = mix_ref[...]
        gp = gp_ref[...]
        r2 = lax.rsqrt(jnp.mean(mix * mix, axis=-1, keepdims=True) + NORM_EPS)
        nrm = mix * r2
        x1b = (x_ref[...] + nrm * gp).astype(bf16)
        pb = p_ref[...].astype(bf16)
        sq = _sig(_dot(x1b, wpg_ref[...]))
        pe = _dot(pb, wple_ref[...])
        gq = (gx2 * pe * sq * (1.0 - sq)).astype(bf16)
        _accumulate(gwple_ref, _dot_tn(pb, (gx2 * sq).astype(bf16)), i)
        _accumulate(gwpg_ref, _dot_tn(x1b, gq), i)
        gx1 = gx2 + _dot_nt(gq, wpg_ref[...])
        gres_ref[...] = gx1
        _accumulate(ggp_ref, jnp.sum(gx1 * nrm, axis=0, keepdims=True), i)
        gy = gx1 * gp
        gmix_ref[...] = (r2 * (gy - nrm * jnp.mean(gy * nrm, axis=-1, keepdims=True))).astype(bf16)

    dm = lambda: _rows(tm, D_MODEL)
    return pl.pallas_call(
        body, name=name, grid=(T // tm,),
        in_specs=[dm(), dm(), dm(), _rows(tm, PLE_DIM), _resident((D_MODEL, D_MODEL)), _resident((PLE_DIM, D_MODEL)),
                  _full((1, D_MODEL))],
        out_specs=[dm(), dm(), _full((1, D_MODEL)), _full((D_MODEL, D_MODEL)), _full((PLE_DIM, D_MODEL))],
        out_shape=[_S((T, D_MODEL)), _S((T, D_MODEL), bf16), _S((1, D_MODEL)), _S((D_MODEL, D_MODEL)),
                   _S((PLE_DIM, D_MODEL))],
        compiler_params=_cp("arbitrary"),
    )(gx2, mix, x, p, wpg, wple, gpost)


def gate_bwd(gmix, zl, zs, gsl, wout, name):
    T = zl.shape[0]
    tm = _tile(T, TOKEN_TILE_LIGHT)

    def body(gmix_ref, zl_ref, zs_ref, gs_ref, gl_ref, wout_ref, gzs_ref, gzl_ref, ggsl_ref, gwout_ref):
        i = pl.program_id(0)
        gmix = gmix_ref[...]
        gmerged = _dot_nt(gmix, wout_ref[...])
        zs, zl = zs_ref[...], zl_ref[...]
        ss, sl = _sig(gs_ref[...]), _sig(gl_ref[...])
        _accumulate(gwout_ref, _dot_tn((ss * zs + sl * zl).astype(bf16), gmix), i)
        gzs_ref[...] = (gmerged * ss).astype(bf16)
        gzl_ref[...] = (gmerged * sl).astype(bf16)
        ggsl_ref[:, :D_MODEL] = (gmerged * zs * ss * (1.0 - ss)).astype(bf16)
        ggsl_ref[:, D_MODEL:] = (gmerged * zl * sl * (1.0 - sl)).astype(bf16)

    dm = lambda: _rows(tm, D_MODEL)
    return pl.pallas_call(
        body, name=name, grid=(T // tm,),
        in_specs=[dm(), dm(), dm(), _rows(tm, D_MODEL, 0), _rows(tm, D_MODEL, 1), _resident((D_MODEL, D_MODEL))],
        out_specs=[dm(), dm(), _rows(tm, 2 * D_MODEL), _full((D_MODEL, D_MODEL))],
        out_shape=[_S((T, D_MODEL), bf16), _S((T, D_MODEL), bf16), _S((T, 2 * D_MODEL), bf16), _S((D_MODEL, D_MODEL))],
        compiler_params=_cp("arbitrary"),
    )(gmix, zl, zs, gsl, gsl, wout)


def lru_out_bwd(gzl, h, lg, wbl, name):
    T = h.shape[0]
    tm = _tile(T, TOKEN_TILE_LIGHT)

    def body(gzl_ref, h_ref, lg_ref, wbl_ref, gh_ref, glg_ref, gwbl_ref):
        i = pl.program_id(0)
        gzl = gzl_ref[...]
        lg_v, hv = lg_ref[...], h_ref[...]
        slg = _sig(lg_v)
        silu = lg_v * slg
        _accumulate(gwbl_ref, _dot_tn((hv * silu).astype(bf16), gzl), i)
        gyl = _dot_nt(gzl, wbl_ref[...])
        gh_ref[...] = gyl * silu
        glg_ref[...] = (gyl * hv * slg * (1.0 + lg_v * (1.0 - slg))).astype(bf16)

    lw = lambda: _rows(tm, LRU_WIDTH)
    return pl.pallas_call(
        body, name=name, grid=(T // tm,),
        in_specs=[_rows(tm, D_MODEL), lw(), lw(), _resident((LRU_WIDTH, D_MODEL))],
        out_specs=[lw(), lw(), _full((LRU_WIDTH, D_MODEL))],
        out_shape=[_S((T, LRU_WIDTH)), _S((T, LRU_WIDTH), bf16), _S((LRU_WIDTH, D_MODEL))],
        compiler_params=_cp("arbitrary"),
    )(gzl, h, lg, wbl)


def in_proj_bwd(pieces, win, x, gres, g, name):
    T = x.shape[0]
    tm = _tile(T, MM_TILE_M // 2)
    widths = [pc.shape[1] for pc in pieces]
    offs = [sum(widths[:k]) for k in range(len(widths))]

    def body(*refs):
        pc_refs = refs[:len(widths)]
        w_ref, x_ref, gres_ref, g_ref, gx_ref, gg_ref = refs[len(widths):]
        i = pl.program_id(0)
        ghv = _dot_nt(pc_refs[0][...], w_ref[:, offs[0]:offs[0] + widths[0]])
        for k in range(1, len(widths)):
            ghv = ghv + _dot_nt(pc_refs[k][...], w_ref[:, offs[k]:offs[k] + widths[k]])
        xv = x_ref[...]
        r = lax.rsqrt(jnp.mean(xv * xv, axis=-1, keepdims=True) + NORM_EPS)
        nrm = xv * r
        gy = ghv * g_ref[...]
        gx_ref[...] = gres_ref[...] + r * (gy - nrm * jnp.mean(gy * nrm, axis=-1, keepdims=True))
        _accumulate(gg_ref, jnp.sum(ghv * nrm, axis=0, keepdims=True), i)

    dm = lambda: _rows(tm, D_MODEL)
    return pl.pallas_call(
        body, name=name, grid=(T // tm,),
        in_specs=[_rows(tm, wd) for wd in widths] + [_resident(win.shape), dm(), dm(), _full((1, D_MODEL))],
        out_specs=[dm(), _full((1, D_MODEL))],
        out_shape=[_S((T, D_MODEL)), _S((1, D_MODEL))], compiler_params=_cp("arbitrary"),
    )(*pieces, win, x, gres, g)


def loss_head(y, target, name):
    T = y.shape[0]
    tm = _tile(T, TOKEN_TILE_LIGHT)

    def body(y_ref, t_ref, l_ref, g_ref):
        i = pl.program_id(0)
        e = y_ref[...] - t_ref[...]
        g_ref[...] = e * (1.0 / D_MODEL)
        part = 0.5 * jnp.sum(jnp.sum(e * e, axis=-1, keepdims=True) * (1.0 / D_MODEL), axis=0, keepdims=True)

        @pl.when(i == 0)
        def _():
            l_ref[...] = part

        @pl.when(i > 0)
        def _():
            l_ref[...] += part

    return pl.pallas_call(
        body, name=name, grid=(T // tm,),
        in_specs=[_rows(tm, D_MODEL), _rows(tm, D_MODEL)], out_specs=[_full((1, 1)), _rows(tm, D_MODEL)],
        out_shape=[_S((1, 1)), _S((T, D_MODEL))],
        compiler_params=_cp("arbitrary"),
    )(y, target)


def _s5_operands(w, m, tag):
    b_re_t = jnp.transpose(w['s5_b_re'], (2, 0, 1))
    b_im_t = jnp.transpose(w['s5_b_im'], (2, 0, 1))
    ldt = w['s5_log_dt'][:, None]
    ab, pw, bb = s5_prep(w['s5_a_re'], w['s5_a_im'], ldt, b_re_t, b_im_t, m, "s5_prep" + tag)
    over_sublanes = lambda t: jnp.broadcast_to(t[..., None, :], t.shape[:-1] + (SEGS, S5_NS))
    ptab = pw.reshape(2, m, S5_NS)
    return dict(abar_b=over_sublanes(ab.reshape(2, S5_NS)), ptab_b=over_sublanes(ptab),
                ptab_rev_b=over_sublanes(ptab[:, ::-1, :]), bdb=_pack_bdb(bb).astype(bf16),
                cdb=_pack_cdb(w['s5_c_re'], w['s5_c_im']).astype(bf16), dvec=w['s5_d'][None, :],
                prep_in=(w['s5_a_re'], w['s5_a_im'], ldt, b_re_t, b_im_t))


def layer_fwd(x, p, w, tag, gather=()):
    T = x.shape[0]
    m = _tile(T, SCAN_CHUNK) // SEGS
    s5 = _s5_operands(w, m, tag)
    h_bf = rms_fwd(x, w['g_pre'][None, :], "rms_fwd" + tag)
    win = w['w_in']
    usg = mm_nn(h_bf, win[:, :2 * S5_WIDTH], "proj_s5" + tag)
    lx = mm_nn(h_bf, win[:, 2 * S5_WIDTH:2 * S5_WIDTH + LRU_WIDTH], "proj_lx" + tag)
    lg = mm_nn(h_bf, win[:, 2 * S5_WIDTH + LRU_WIDTH:2 * S5_WIDTH + 2 * LRU_WIDTH], "proj_lg" + tag)
    gsl = mm_nn(h_bf, win[:, 2 * S5_WIDTH + 2 * LRU_WIDTH:], "proj_gate" + tag)
    (ys, s_re, s_im, s_bf), gathered = s5_fwd(usg, s5['bdb'], s5['cdb'], s5['dvec'], s5['abar_b'], s5['ptab_b'],
                                              "s5_fwd" + tag, gather)
    glu, zs = s5_post_fwd(ys, usg, w['w_glu'], w['w_bs'], "s5_post_fwd" + tag)
    wa, wx = w['lru_w_a'].astype(bf16), w['lru_w_x'].astype(bf16)
    c, r, gi, hs = lru_fwd(lx, w['conv_w'], w['conv_b'][None, :], wa, wx, w['lru_b_a'][None, :], w['lru_b_x'][None, :],
                           w['lru_lambda'][None, :], "lru_fwd" + tag)
    zl, mix, x_out = merge_fwd(hs, lg, zs, gsl, x, p, w['w_bl'], w['w_out'], w['g_post'][None, :],
                               w['w_ple'], w['w_ple_gate'], "merge_fwd" + tag)
    saved = dict(x=x, p=p, h_bf=h_bf, usg=usg, lx=lx, lg=lg, gsl=gsl, ys=ys, s_re=s_re, s_im=s_im, s_bf=s_bf, glu=glu,
                 zs=zs, c=c, r=r, gi=gi, hs=hs, zl=zl, mix=mix, s5=s5, wa=wa, wx=wx)
    return x_out, saved, gathered


def layer_bwd(gx_out, w, sv, tag, exchange=()):
    s5 = sv['s5']
    g = {}
    gres, gmix, g_gpost, g['w_ple_gate'], g['w_ple'] = post_bwd(
        gx_out, sv['mix'], sv['x'], sv['p'], w['w_ple_gate'], w['w_ple'], w['g_post'][None, :], "post_bwd" + tag)
    gzs, gzl, ggsl, g['w_out'] = gate_bwd(gmix, sv['zl'], sv['zs'], sv['gsl'], w['w_out'], "gate_bwd" + tag)
    g_h, g_lg, g['w_bl'] = lru_out_bwd(gzl, sv['hs'], sv['lg'], w['w_bl'], "lru_out_bwd" + tag)
    g['g_post'] = g_gpost[0]
    (g_lx, g_wa, g_wx, g_ba, g_bx, g_lam, g_cb, g_cw), exchanged = lru_bwd(
        g_h, sv['hs'], sv['c'], sv['r'], sv['gi'], sv['lx'], w['conv_w'], sv['wa'], sv['wx'],
        w['lru_lambda'][None, :], "lru_bwd" + tag, exchange)
    g['lru_w_a'], g['lru_w_x'] = g_wa, g_wx
    g['lru_b_a'], g['lru_b_x'], g['lru_lambda'], g['conv_b'], g['conv_w'] = g_ba[0], g_bx[0], g_lam[0], g_cb[0], g_cw
    g_ys, g_sg, g['w_bs'], g['w_glu'] = s5_post_bwd(gzs, sv['glu'], sv['usg'], sv['ys'], w['w_bs'], w['w_glu'],
                                                    "s5_post_bwd" + tag)
    g_u, g_ab, g_d, g_bdb, g_cdb = s5_bwd(g_ys, sv['usg'], sv['s_re'], sv['s_im'], sv['s_bf'], s5['bdb'], s5['cdb'],
                                          s5['dvec'], s5['abar_b'], s5['ptab_rev_b'], "s5_bwd" + tag)
    g['s5_d'] = g_d[0]
    g['s5_c_re'], g['s5_c_im'] = _unpack_cdb(g_cdb)
    g_are, g_aim, g_ldt, g_bre_t, g_bim_t = s5_prep_bwd(*s5['prep_in'], g_ab.reshape(2, S5_GROUPS, S5_STATE),
                                                       _unpack_bdb(g_bdb), "s5_prep_bwd" + tag)
    g['s5_a_re'], g['s5_a_im'], g['s5_log_dt'] = g_are, g_aim, g_ldt
    g['s5_b_re'] = jnp.transpose(g_bre_t, (1, 2, 0))
    g['s5_b_im'] = jnp.transpose(g_bim_t, (1, 2, 0))
    pieces = [g_u, g_sg, g_lx, g_lg, ggsl]
    g['w_in'] = jnp.concatenate([mm_tn(sv['h_bf'], pc, "gw_in%d%s" % (k, tag)) for k, pc in enumerate(pieces)], axis=1)
    gx, g_gpre = in_proj_bwd(pieces, w['w_in'], sv['x'], gres, w['g_pre'][None, :], "in_proj_bwd" + tag)
    g['g_pre'] = g_gpre[0]
    return gx, g, exchanged


def _as_2d(a):
    return a.reshape((-1, a.shape[-1])) if a.ndim > 1 else a.reshape((1, -1))


def _adamw_update(w, gv, m, v):
    nm = ADAM_B1 * m + (1.0 - ADAM_B1) * gv
    nv = ADAM_B2 * v + (1.0 - ADAM_B2) * (gv * gv)
    bc1 = 1.0 - ADAM_B1 ** ADAM_STEP
    bc2 = 1.0 - ADAM_B2 ** ADAM_STEP
    return -ADAM_LR * ((nm / bc1) / (jnp.sqrt(nv / bc2) + ADAM_EPS) + ADAM_WD * w), nm, nv


def adamw(w, g, m, v, name):
    shape = w.shape
    w2, g2, m2, v2 = _as_2d(w), _as_2d(g), _as_2d(m), _as_2d(v)
    R, C = w2.shape
    tr = _row_tile(R, C)

    def body(w_ref, g_ref, m_ref, v_ref, d_ref, nm_ref, nv_ref):
        d_ref[...], nm_ref[...], nv_ref[...] = _adamw_update(w_ref[...], g_ref[...], m_ref[...], v_ref[...])

    spec = lambda: pl.BlockSpec((tr, C), lambda i: (i, 0))
    d, nm, nv = pl.pallas_call(
        body, name=name, grid=(R // tr,), in_specs=[spec() for _ in range(4)], out_specs=[spec() for _ in range(3)],
        out_shape=[_S((R, C))] * 3, compiler_params=_cp("parallel"),
    )(w2, g2, m2, v2)
    return d.reshape(shape), nm.reshape(shape), nv.reshape(shape)


def adamw_reduce(w, parts, theirs, m, v, chip, name):
    shape = w.shape
    C = shape[-1]
    R = math.prod(shape[1:-1])
    w3, m3, v3 = w.reshape(DEPTH, R, C), m.reshape(DEPTH, R, C), v.reshape(DEPTH, R, C)
    tr = _row_tile(R, C)

    def body(chip_ref, w_ref, *refs):
        layer_refs, (m_ref, v_ref, g_ref, d_ref, nm_ref, nv_ref) = refs[:2 * DEPTH], refs[2 * DEPTH:]
        layer = pl.program_id(0)
        for l in range(DEPTH):
            @pl.when(layer == l)
            def _():
                a_ref, t_ref = layer_refs[2 * l], layer_refs[2 * l + 1]
                gv = ((a_ref[0] + t_ref[0].astype(f32)) + t_ref[1].astype(f32)) + t_ref[2].astype(f32)
                g_ref[0] = gv
                d_ref[0], nm_ref[0], nv_ref[0] = _adamw_update(w_ref[0], gv, m_ref[0], v_ref[0])

    spec = lambda: pl.BlockSpec((1, tr, C), lambda l, i, c: (l, i, 0))
    rows_of = lambda l: (lambda ll, i, c: jnp.where(ll == l, i, 0))
    layer_specs = []
    for l in range(DEPTH):
        layer_specs.append(pl.BlockSpec((1, tr, C), lambda ll, i, c, r=rows_of(l): (c[0], r(ll, i, c), 0)))
        layer_specs.append(pl.BlockSpec((3, tr, C), lambda ll, i, c, r=rows_of(l): (0, r(ll, i, c), 0)))
    grid_spec = pltpu.PrefetchScalarGridSpec(
        num_scalar_prefetch=1, grid=(DEPTH, R // tr),
        in_specs=[spec()] + layer_specs + [spec(), spec()], out_specs=[spec() for _ in range(4)])
    operands = [x.reshape(x.shape[0], R, C) for l in range(DEPTH) for x in (parts[l], theirs[l])]
    g, d, nm, nv = pl.pallas_call(
        body, name=name, grid_spec=grid_spec, out_shape=[_S((DEPTH, R, C))] * 4,
        compiler_params=_cp("arbitrary", "arbitrary"),
    )(chip, w3, *operands, m3, v3)
    return g.reshape(shape), d.reshape(shape), nm.reshape(shape), nv.reshape(shape)


MESH = pl.DeviceIdType.MESH
ANY = pl.BlockSpec(memory_space=pl.ANY)


def _place():
    return lax.axis_index("x"), lax.axis_index("y"), lax.axis_index("c")


def _other_chips(mx, my):
    return [(1 - mx, my), (mx, 1 - my), (1 - mx, 1 - my)]


def all_gather(shards, name):
    nb = len(shards)

    def body(*refs):
        phases = _gather_phases([s.shape for s in shards], refs[:nb], refs[nb:2 * nb], *refs[2 * nb:])
        for phase in phases:
            phase()

    outs = pl.pallas_call(
        body, name=name, out_shape=_gather_out_shapes(shards), in_specs=[ANY] * nb, out_specs=[ANY] * nb,
        scratch_shapes=_gather_semaphores(nb),
    )(*shards)
    return list(outs)


GATHER_COPIES = 9
OWN_SIB, OWN_X, OWN_Y, X_SIB, Y_SIB, RELAY_X, RELAY_Y, DIAG0_SIB, DIAG1_SIB = range(GATHER_COPIES)


def _gather_out_shapes(shards):
    return [_S((N_DEV,) + s.shape, s.dtype) for s in shards]


def _gather_semaphores(nb):
    return [pltpu.SemaphoreType.DMA((nb, GATHER_COPIES)), pltpu.SemaphoreType.DMA((nb, GATHER_COPIES)),
            pltpu.SemaphoreType.DMA((nb,))]


def _gather_phases(shapes, x_refs, out_refs, send_sems, recv_sems, local_sems):
    nb = len(shapes)
    mx, my, mc = _place()
    sibling, xn, yn = (mx, my, 1 - mc), (1 - mx, my, mc), (mx, 1 - my, mc)

    def block(b, px, py, pc, half=None):
        ref = out_refs[b].at[4 * px + 2 * py + pc]
        if half is None:
            return ref
        n = shapes[b][0] // 2
        return ref.at[pl.ds(half * n, n)]

    def copy(b, k, dst, to, src=None):
        return pltpu.make_async_remote_copy(
            src_ref=dst if src is None else src, dst_ref=dst, send_sem=send_sems.at[b, k],
            recv_sem=recv_sems.at[b, k], device_id=to, device_id_type=MESH)

    def send(b, k):
        if k in (OWN_X, OWN_Y, OWN_SIB):
            return copy(b, k, block(b, mx, my, mc), {OWN_X: xn, OWN_Y: yn, OWN_SIB: sibling}[k], src=x_refs[b])
        what, to = {RELAY_X: ((1 - mx, my, mc, 0), yn), RELAY_Y: ((mx, 1 - my, mc, 1), xn),
                    X_SIB: ((1 - mx, my, mc), sibling), Y_SIB: ((mx, 1 - my, mc), sibling),
                    DIAG0_SIB: ((1 - mx, 1 - my, mc, 0), sibling), DIAG1_SIB: ((1 - mx, 1 - my, mc, 1), sibling)}[k]
        return copy(b, k, block(b, *what), to)

    def local(b):
        return pltpu.make_async_copy(x_refs[b], block(b, mx, my, mc), local_sems.at[b])

    def send_own():
        for k in (OWN_X, OWN_Y, OWN_SIB):
            for b in range(nb):
                send(b, k).start()
        for b in range(nb):
            local(b).start()

    def relay_neighbours():
        for b in range(nb):
            copy(b, OWN_X, block(b, 1 - mx, my, mc), xn).wait_recv()
            send(b, RELAY_X).start()
            send(b, X_SIB).start()
        for b in range(nb):
            copy(b, OWN_Y, block(b, mx, 1 - my, mc), yn).wait_recv()
            send(b, RELAY_Y).start()
            send(b, Y_SIB).start()

    def hand_on_diagonal():
        for b in range(nb):
            copy(b, RELAY_X, block(b, 1 - mx, 1 - my, mc, 0), yn).wait_recv()
            send(b, DIAG0_SIB).start()
            copy(b, RELAY_Y, block(b, 1 - mx, 1 - my, mc, 1), xn).wait_recv()
            send(b, DIAG1_SIB).start()

    def finish():
        for b in range(nb):
            copy(b, OWN_SIB, block(b, mx, my, 1 - mc), sibling).wait_recv()
            copy(b, X_SIB, block(b, 1 - mx, my, 1 - mc), sibling).wait_recv()
            copy(b, Y_SIB, block(b, mx, 1 - my, 1 - mc), sibling).wait_recv()
            copy(b, DIAG0_SIB, block(b, 1 - mx, 1 - my, 1 - mc, 0), sibling).wait_recv()
            copy(b, DIAG1_SIB, block(b, 1 - mx, 1 - my, 1 - mc, 1), sibling).wait_recv()
        for b in range(nb):
            for k in range(GATHER_COPIES):
                send(b, k).wait_send()
            local(b).wait()

    return send_own, relay_neighbours, hand_on_diagonal, finish


def exchange_sibling(gs, name):
    nb = len(gs)

    def body(*refs):
        g_refs, recv_refs, send_sems, recv_sems = refs[:nb], refs[nb:2 * nb], refs[2 * nb], refs[2 * nb + 1]
        mx, my, mc = _place()
        copies = [pltpu.make_async_remote_copy(
            src_ref=g_refs[b].at[2 * k + 1 - mc], dst_ref=recv_refs[b].at[k], send_sem=send_sems.at[b, k],
            recv_sem=recv_sems.at[b, k], device_id=(mx, my, 1 - mc), device_id_type=MESH)
            for b in range(nb) for k in range(4)]
        for cp in copies:
            cp.start()
        for cp in copies:
            cp.wait()

    outs = pl.pallas_call(
        body, name=name, out_shape=[_S((4,) + g.shape[1:], g.dtype) for g in gs], in_specs=[ANY] * nb,
        out_specs=[ANY] * nb,
        scratch_shapes=[pltpu.SemaphoreType.DMA((nb, 4)), pltpu.SemaphoreType.DMA((nb, 4))],
    )(*gs)
    return list(outs)


def exchange_chips(parts, name):
    nb = len(parts)

    def body(*refs):
        start, finish = _chips_phases(refs[:nb], refs[nb:2 * nb], refs[2 * nb], refs[2 * nb + 1])
        start()
        finish()

    outs = pl.pallas_call(
        body, name=name, out_shape=_chips_out_shapes(parts), in_specs=[ANY] * nb, out_specs=[ANY] * nb,
        scratch_shapes=_chips_semaphores(nb),
    )(*parts)
    return list(outs)


def _chips_out_shapes(parts):
    return [_S((3,) + a.shape[1:], a.dtype) for a in parts]


def _chips_semaphores(nb):
    return [pltpu.SemaphoreType.DMA((nb, 3)), pltpu.SemaphoreType.DMA((nb, 3))]


def _chips_phases(a_refs, recv_refs, send_sems, recv_sems):
    mx, my, mc = _place()

    def copies():
        return [pltpu.make_async_remote_copy(
            src_ref=a_refs[b].at[2 * px + py], dst_ref=recv_refs[b].at[j], send_sem=send_sems.at[b, j],
            recv_sem=recv_sems.at[b, j], device_id=(px, py, mc), device_id_type=MESH)
            for b in range(len(a_refs)) for j, (px, py) in enumerate(_other_chips(mx, my))]

    def start():
        for cp in copies():
            cp.start()

    def finish():
        for cp in copies():
            cp.wait()

    return start, finish


def add_sibling(g, theirs, core, name, wire_dtype=f32):
    shp = theirs.shape
    C = shp[-1]
    R = math.prod(shp[1:-1])
    tr = _row_tile(R, C)
    narrow = wire_dtype != f32

    def body(core_ref, g_ref, t_ref, o_ref, *wire_ref):
        s = g_ref[...] + t_ref[...]
        o_ref[...] = s
        if narrow:
            wire_ref[0][...] = s.astype(wire_dtype)

    blk = lambda: pl.BlockSpec((1, tr, C), lambda k, i, c: (k, i, 0))
    grid_spec = pltpu.PrefetchScalarGridSpec(
        num_scalar_prefetch=1, grid=(4, R // tr),
        in_specs=[pl.BlockSpec((1, tr, C), lambda k, i, c: (2 * k + c[0], i, 0)), blk()],
        out_specs=[blk(), blk()] if narrow else [blk()])
    outs = pl.pallas_call(
        body, name=name, grid_spec=grid_spec,
        out_shape=[_S((4, R, C), f32)] + ([_S((4, R, C), wire_dtype)] if narrow else []),
        compiler_params=_cp("parallel", "parallel"),
    )(core, g.reshape(N_DEV, R, C), theirs.reshape(4, R, C))
    part = outs[0].reshape(shp)
    return part, (outs[1].reshape(shp) if narrow else part)


def add_chips(a, theirs, chip, name):
    _, R, C = a.shape
    tr = _row_tile(R, C)

    def body(chip_ref, a_ref, t_ref, out_ref):
        out_ref[...] = ((a_ref[0] + t_ref[0]) + t_ref[1]) + t_ref[2]

    grid_spec = pltpu.PrefetchScalarGridSpec(
        num_scalar_prefetch=1, grid=(R // tr,),
        in_specs=[pl.BlockSpec((1, tr, C), lambda i, c: (c[0], i, 0)), pl.BlockSpec((3, tr, C), lambda i, c: (0, i, 0))],
        out_specs=pl.BlockSpec((tr, C), lambda i, c: (i, 0)))
    return pl.pallas_call(
        body, name=name, grid_spec=grid_spec, out_shape=_S((R, C), a.dtype), compiler_params=_cp("parallel"),
    )(chip, a, theirs)


def _round_up(n, q):
    return (n + q - 1) // q * q


def _lane_rows(a):
    flat = a.reshape(-1)
    n = _round_up(flat.shape[0], SUBLANES * LANES)
    return jnp.pad(flat, (0, n - flat.shape[0])).reshape(-1, LANES)


def _full_to_shards(full, axis):
    shp = full.shape
    s = shp[axis] // N_DEV
    cut = full.reshape(shp[:axis] + (N_DEV, s) + shp[axis + 1:])
    return jnp.moveaxis(cut, axis, 0)


def _shards_to_full(parts, axis):
    shp = list(parts.shape[1:])
    shp[axis] *= N_DEV
    return jnp.moveaxis(parts, 0, axis).reshape(tuple(shp))


def kernel(x, p, g_pre, w_in, s5_a_re, s5_a_im, s5_log_dt, s5_b_re, s5_b_im, s5_c_re, s5_c_im, s5_d, w_glu, w_bs, conv_w, conv_b, lru_w_a, lru_b_a, lru_w_x, lru_b_x, lru_lambda, w_bl, w_out, g_post, w_ple, w_ple_gate, loss_target, m_g_pre, m_w_in, m_s5_a_re, m_s5_a_im, m_s5_log_dt, m_s5_b_re, m_s5_b_im, m_s5_c_re, m_s5_c_im, m_s5_d, m_w_glu, m_w_bs, m_conv_w, m_conv_b, m_lru_w_a, m_lru_b_a, m_lru_w_x, m_lru_b_x, m_lru_lambda, m_w_bl, m_w_out, m_g_post, m_w_ple, m_w_ple_gate, v_g_pre, v_w_in, v_s5_a_re, v_s5_a_im, v_s5_log_dt, v_s5_b_re, v_s5_b_im, v_s5_c_re, v_s5_c_im, v_s5_d, v_w_glu, v_w_bs, v_conv_w, v_conv_b, v_lru_w_a, v_lru_b_a, v_lru_w_x, v_lru_b_x, v_lru_lambda, v_w_bl, v_w_out, v_g_post, v_w_ple, v_w_ple_gate):
    given = dict(locals())
    W = {n: given[n] for n in WEIGHTS}
    M = {n: given["m_" + n] for n in WEIGHTS}
    V = {n: given["v_" + n] for n in WEIGHTS}
    xs, target = to_scan_order(x[0]), to_scan_order(loss_target[0])
    ps = [to_scan_order(p[i, 0]) for i in range(DEPTH)]

    mx, my, mc = _place()
    core = jnp.reshape(mc, (1,)).astype(jnp.int32)
    chip = jnp.reshape(2 * mx + my, (1,)).astype(jnp.int32)

    names = list(SHARDED)
    conv_rows = PAIR - CONV_WIDTH

    def layer_shards(i):
        return [W[n][i].astype(bf16) if n in GATHER_BF16 else jnp.pad(W[n][i], ((0, conv_rows), (0, 0))) for n in names]

    def layer_weights(i, gathered):
        full = {n: _shards_to_full(g if n in GATHER_BF16 else g[:, :CONV_WIDTH], SHARDED[n] - 1)
                for n, g in zip(names, gathered)}
        return {n: (full[n] if n in SHARDED else W[n][i]) for n in WEIGHTS}

    act, saved, weights = xs, [], []
    gathered = all_gather(layer_shards(0), "comm_gather_weights")
    for i in range(DEPTH):
        weights.append(layer_weights(i, gathered))
        act, sv, gathered = layer_fwd(act, ps[i], weights[i], "_l%d" % i, layer_shards(i + 1) if i + 1 < DEPTH else ())
        saved.append(sv)
    loss_part, gact = loss_head(act, target, "loss_head")
    loss = lax.psum(loss_part[0, 0], ("x", "y", "c"))

    def sibling_sums(i, g):
        rep_rows = [_lane_rows(g[n].reshape(W[n].shape[1:])) for n in REPLICATED]
        n_rows = sum(r.shape[0] for r in rep_rows)
        pad_rows = _round_up(n_rows, N_DEV * SUBLANES) - n_rows
        rep_blocks = jnp.concatenate(rep_rows + [jnp.zeros((pad_rows, LANES), f32)]).reshape(N_DEV, -1, LANES)
        blocks = [_full_to_shards(g[n].reshape(weights[i][n].shape), SHARDED[n] - 1) for n in names] + [rep_blocks]
        theirs = exchange_sibling(blocks, "comm_reduce_sibling_l%d" % i)
        parts, wire = [], []
        for k, (b, t) in enumerate(zip(blocks, theirs)):
            part, sent = add_sibling(b, t, core, "reduce_add_sibling_%d_l%d" % (k, i), bf16 if k < len(names) else f32)
            parts.append(part)
            wire.append(sent)
        return parts, wire, [r.shape[0] for r in rep_rows]

    parts, others, rep_sizes, wire = [None] * DEPTH, [None] * DEPTH, None, ()
    for i in reversed(range(DEPTH)):
        gact, g, exchanged = layer_bwd(gact, weights[i], saved[i], "_l%d" % i, wire)
        if i + 1 < DEPTH:
            others[i + 1] = exchanged
        parts[i], wire, rep_sizes = sibling_sums(i, g)
    others[0] = exchange_chips(wire, "comm_reduce_chips")

    red, deltas, new_m, new_v = {}, {}, {}, {}
    for k, n in enumerate(names):
        red[n], deltas[n], new_m[n], new_v[n] = adamw_reduce(
            W[n], [parts[i][k] for i in range(DEPTH)], [others[i][k] for i in range(DEPTH)], M[n], V[n], chip, "adamw_" + n)
    pieces = [add_chips(parts[i][-1], others[i][-1], chip, "reduce_add_chips_l%d" % i) for i in range(DEPTH)]
    rep_all = [r.reshape(-1, LANES) for r in all_gather(pieces, "comm_gather_replicated")]
    off = 0
    for n, rows in zip(REPLICATED, rep_sizes):
        k = math.prod(W[n].shape[1:])
        red[n] = jnp.stack([rep_all[i][off:off + rows].reshape(-1)[:k] for i in range(DEPTH)]).reshape(W[n].shape)
        off += rows
        deltas[n], new_m[n], new_v[n] = adamw(W[n], red[n], M[n], V[n], "adamw_" + n)
    return (loss, from_scan_order(gact)[None], *[red[n] for n in WEIGHTS], *[deltas[n] for n in WEIGHTS],
            *[new_m[n] for n in WEIGHTS], *[new_v[n] for n in WEIGHTS])
```

```python
import math

import jax
import jax.numpy as jnp
from jax import lax
from jax.experimental import pallas as pl
from jax.experimental.pallas import tpu as pltpu

f32 = jnp.float32
bf16 = jnp.bfloat16

D_MODEL = 1024
DEPTH = 2
PLE_DIM = 256
NORM_EPS = 1e-6
S5_WIDTH = 512
S5_GROUP = 16
S5_GROUPS = 32
S5_STATE = 64
S5_NS = S5_GROUPS * S5_STATE
LRU_WIDTH = 1280
LRU_HEADS = 10
LRU_HEAD_DIM = 128
LRU_C = 8.0
CONV_WIDTH = 4
N_DEV = 8

ADAM_LR = 0.001
ADAM_B1 = 0.9
ADAM_B2 = 0.999
ADAM_EPS = 1e-08
ADAM_WD = 0.01
ADAM_STEP = 10

LANES = 128
SUBLANES = 8
SEGS = SUBLANES
SCAN_CHUNK = 256
TOKEN_TILE = 256
TOKEN_TILE_LIGHT = 512
MM_TILE_M = 1024
MM_TILE_N = 1408
MM_TILE_K_ROWS = 1280
PAIR = 2 * SUBLANES
VMEM_LIMIT_BYTES = 56 * 1024 * 1024
ELEMENTWISE_BLOCK_BYTES = 1024 * 1024

WEIGHTS = ['g_pre', 'w_in', 's5_a_re', 's5_a_im', 's5_log_dt', 's5_b_re', 's5_b_im', 's5_c_re', 's5_c_im',
           's5_d', 'w_glu', 'w_bs', 'conv_w', 'conv_b', 'lru_w_a', 'lru_b_a', 'lru_w_x', 'lru_b_x',
           'lru_lambda', 'w_bl', 'w_out', 'g_post', 'w_ple', 'w_ple_gate']
SHARDED = {'w_in': 2, 'w_glu': 2, 'w_bs': 2, 'conv_w': 2, 'w_bl': 1, 'w_out': 1, 'w_ple': 2, 'w_ple_gate': 1}
GATHER_BF16 = ['w_in', 'w_glu', 'w_bs', 'w_bl', 'w_out', 'w_ple', 'w_ple_gate']
REPLICATED = [n for n in WEIGHTS if n not in SHARDED]


def _sig(x):
    return 0.5 * jnp.tanh(0.5 * x) + 0.5


def _gelu_parts(x):
    k = math.sqrt(2.0 / math.pi)
    t = jnp.tanh(k * (x + 0.044715 * x * x * x))
    return t, k


def _gelu(x):
    t, _ = _gelu_parts(x)
    return 0.5 * x * (1.0 + t)


def _gelu_grad(x):
    t, k = _gelu_parts(x)
    return 0.5 * (1.0 + t) + 0.5 * x * (1.0 - t * t) * k * (1.0 + 3.0 * 0.044715 * x * x)


def _one_minus_sq(a, log_a):
    z = 2.0 * log_a
    series = -z * (1.0 + z * (0.5 + z * (1.0 / 6.0 + z * (1.0 / 24.0 + z * (1.0 / 120.0)))))
    return jnp.where(z > -0.05, series, 1.0 - a * a)


def _softplus_neg(lam):
    return jnp.maximum(-lam, 0.0) + jnp.log(1.0 + jnp.exp(-jnp.abs(lam)))


def _dot(a, b):
    return jnp.dot(a, b, preferred_element_type=f32)


def _dot_nt(a, b):
    return lax.dot_general(a, b, (((1,), (1,)), ((), ())), preferred_element_type=f32)


def _dot_tn(a, b):
    return lax.dot_general(a, b, (((0,), (0,)), ((), ())), preferred_element_type=f32)


def _S(shape, dtype=f32):
    return jax.ShapeDtypeStruct(shape, dtype)


def _full(shape):
    nd = len(shape)
    return pl.BlockSpec(shape, lambda *_: (0,) * nd)


def _rows(tile, width, col=0):
    return pl.BlockSpec((tile, width), lambda i: (i, col))


def _cp(*semantics):
    return pltpu.CompilerParams(dimension_semantics=semantics or None, vmem_limit_bytes=VMEM_LIMIT_BYTES)


def _tile(n, want):
    t = min(n, want)
    assert n % t == 0, (n, want)
    return t


def _row_tile(R, C=LANES):
    cap = max(SUBLANES, min(R, ELEMENTWISE_BLOCK_BYTES // (4 * C)))
    for t in range(cap - cap % SUBLANES, 0, -SUBLANES):
        if R % t == 0:
            return t
    return R


def _lanes(j):
    return slice(LANES * j, LANES * (j + 1))


def _step_rows(k, n=SUBLANES):
    return pl.ds(pl.multiple_of(k * n, n), n)


def to_scan_order(a):
    T, C = a.shape
    tc = _tile(T, SCAN_CHUNK)
    return a.reshape(T // tc, SEGS, tc // SEGS, C).transpose(0, 2, 1, 3).reshape(T, C)


def from_scan_order(a):
    T, C = a.shape
    tc = _tile(T, SCAN_CHUNK)
    return a.reshape(T // tc, tc // SEGS, SEGS, C).transpose(0, 2, 1, 3).reshape(T, C)


def _col_tile(n, cap):
    if n <= cap:
        return n
    for t in range(cap - cap % LANES, 0, -LANES):
        if n % t == 0:
            return t
    return n


def _resident(shape):
    nd = len(shape)
    return pl.BlockSpec(shape, lambda *_: (0,) * nd, pipeline_mode=pl.Buffered(1))


def mm_nn(a, b, name, out_dtype=f32):
    M, K = a.shape
    N = b.shape[1]
    tm, tn = _tile(M, MM_TILE_M), _col_tile(N, MM_TILE_N)

    def body(a_ref, b_ref, o_ref):
        o_ref[...] = _dot(a_ref[...].astype(bf16), b_ref[...].astype(bf16)).astype(out_dtype)

    return pl.pallas_call(
        body, name=name, grid=(M // tm, N // tn),
        in_specs=[pl.BlockSpec((tm, K), lambda i, j: (i, 0)), pl.BlockSpec((K, tn), lambda i, j: (0, j))],
        out_specs=pl.BlockSpec((tm, tn), lambda i, j: (i, j)),
        out_shape=_S((M, N), out_dtype), compiler_params=_cp("parallel", "parallel"),
    )(a, b)


def mm_tn(a, b, name):
    M, K = a.shape
    N = b.shape[1]
    tm, tk, tn = _tile(M, MM_TILE_M), _col_tile(K, MM_TILE_K_ROWS), _col_tile(N, MM_TILE_N)

    def body(a_ref, b_ref, o_ref):
        m = pl.program_id(2)
        part = _dot_tn(a_ref[...].astype(bf16), b_ref[...].astype(bf16))

        @pl.when(m == 0)
        def _():
            o_ref[...] = part

        @pl.when(m > 0)
        def _():
            o_ref[...] += part

    return pl.pallas_call(
        body, name=name, grid=(K // tk, N // tn, M // tm),
        in_specs=[pl.BlockSpec((tm, tk), lambda i, j, m: (m, i)), pl.BlockSpec((tm, tn), lambda i, j, m: (m, j))],
        out_specs=pl.BlockSpec((tk, tn), lambda i, j, m: (i, j)),
        out_shape=_S((K, N), f32),
        compiler_params=_cp("parallel", "parallel", "arbitrary"),
    )(a, b)


def rms_fwd(x, g, name):
    T = x.shape[0]
    tm = _tile(T, TOKEN_TILE_LIGHT)

    def body(x_ref, g_ref, h_ref):
        xv = x_ref[...]
        r = lax.rsqrt(jnp.mean(xv * xv, axis=-1, keepdims=True) + NORM_EPS)
        h_ref[...] = (xv * r * g_ref[...]).astype(bf16)

    return pl.pallas_call(
        body, name=name, grid=(T // tm,),
        in_specs=[_rows(tm, D_MODEL), _full((1, D_MODEL))], out_specs=_rows(tm, D_MODEL),
        out_shape=_S((T, D_MODEL), bf16), compiler_params=_cp("parallel"),
    )(x, g)


def _s5_discretise(a_re, a_im, log_dt, b_re_t, b_im_t):
    dt = jnp.exp(log_dt)
    mag = jnp.exp(a_re * dt)
    ab_re = mag * jnp.cos(a_im * dt)
    ab_im = mag * jnp.sin(a_im * dt)
    den = a_re * a_re + a_im * a_im
    nr, ni = ab_re - 1.0, ab_im
    z_re = (nr * a_re + ni * a_im) / den
    z_im = (ni * a_re - nr * a_im) / den
    bb_re = z_re[None] * b_re_t - z_im[None] * b_im_t
    bb_im = z_re[None] * b_im_t + z_im[None] * b_re_t
    return ab_re, ab_im, bb_re, bb_im


def s5_prep(a_re, a_im, log_dt, b_re_t, b_im_t, m, name):
    G, N = a_re.shape

    def body(are_ref, aim_ref, ldt_ref, bre_ref, bim_ref, ab_ref, pw_ref, bb_ref):
        are, aim, ldt = are_ref[...], aim_ref[...], ldt_ref[...]
        ab_re, ab_im, bb_re, bb_im = _s5_discretise(are, aim, ldt, bre_ref[...], bim_ref[...])
        ab_ref[0], ab_ref[1] = ab_re, ab_im
        bb_ref[0], bb_ref[1] = bb_re, bb_im
        dt = jnp.exp(ldt)
        for k in range(m):
            mag = jnp.exp(are * dt * (k + 1.0))
            pw_ref[0, k] = mag * jnp.cos(aim * dt * (k + 1.0))
            pw_ref[1, k] = mag * jnp.sin(aim * dt * (k + 1.0))

    return pl.pallas_call(
        body, name=name,
        out_shape=[_S((2, G, N)), _S((2, m, G, N)), _S((2, S5_GROUP, G, N))], compiler_params=_cp(),
    )(a_re, a_im, log_dt, b_re_t, b_im_t)


def s5_prep_bwd(a_re, a_im, log_dt, b_re_t, b_im_t, g_ab, g_bb, name):
    G, N = a_re.shape

    def body(are_ref, aim_ref, ldt_ref, bre_ref, bim_ref, gab_ref, gbb_ref, o_are, o_aim, o_ldt, o_bre, o_bim):
        _, vjp = jax.vjp(_s5_discretise, are_ref[...], aim_ref[...], ldt_ref[...], bre_ref[...], bim_ref[...])
        g_are, g_aim, g_ldt, g_bre, g_bim = vjp((gab_ref[0], gab_ref[1], gbb_ref[0], gbb_ref[1]))
        o_are[...], o_aim[...], o_ldt[...], o_bre[...], o_bim[...] = g_are, g_aim, g_ldt, g_bre, g_bim

    return pl.pallas_call(
        body, name=name,
        out_shape=[_S((G, N)), _S((G, N)), _S((G, 1)), _S((S5_GROUP, G, N)), _S((S5_GROUP, G, N))],
        compiler_params=_cp(),
    )(a_re, a_im, log_dt, b_re_t, b_im_t, g_ab, g_bb)


NB_S5 = S5_NS // LANES
CB_S5 = S5_WIDTH // LANES
SB_PER_CB = NB_S5 // CB_S5
GRP_PER_SB = LANES // S5_STATE
S5_JB = 8


def _bdb_mask():
    j = jnp.arange(NB_S5)
    own_rows = (j[:, None] % SB_PER_CB == jnp.arange(SB_PER_CB)[None, :]).astype(f32)
    eye = jnp.eye(GRP_PER_SB, dtype=f32)
    return own_rows[:, :, None, None, None, None, None] * eye[None, None, :, None, None, :, None]


def _pack_bdb(bb):
    v = jnp.transpose(bb.reshape(2, S5_GROUP, NB_S5, GRP_PER_SB, S5_STATE), (2, 3, 1, 0, 4))
    full = v[:, None, :, :, :, None, :] * _bdb_mask()
    return full.reshape(NB_S5, LANES, 2 * LANES)


def _unpack_bdb(g_bdb):
    g7 = g_bdb.reshape(NB_S5, SB_PER_CB, GRP_PER_SB, S5_GROUP, 2, GRP_PER_SB, S5_STATE)
    v = jnp.sum(g7 * _bdb_mask(), axis=(1, 5))
    return jnp.transpose(v, (3, 2, 0, 1, 4)).reshape(2, S5_GROUP, S5_GROUPS, S5_STATE)


def _pack_cdb(c_re, c_im):
    gl = S5_GROUPS // CB_S5
    c2 = jnp.stack([c_re, -c_im]).reshape(2, CB_S5, gl, S5_GROUP, S5_STATE)
    eye = jnp.eye(gl, dtype=f32)
    full = jnp.transpose(c2, (1, 0, 2, 4, 3))[:, :, :, :, None, :] * eye[None, None, :, None, :, None]
    return full.reshape(CB_S5, 2 * SB_PER_CB * LANES, LANES)


def _unpack_cdb(g_cdb):
    gl = S5_GROUPS // CB_S5
    g6 = g_cdb.reshape(CB_S5, 2, gl, S5_STATE, gl, S5_GROUP)
    eye = jnp.eye(gl, dtype=f32)
    v = jnp.sum(g6 * eye[None, None, :, None, :, None], axis=4)
    v = jnp.transpose(v, (1, 0, 2, 4, 3)).reshape(2, S5_GROUPS, S5_GROUP, S5_STATE)
    return v[0], -v[1]


def _state_cat(ref, c):
    w = SB_PER_CB * LANES
    return jnp.concatenate([ref[:, w * c:w * (c + 1)], ref[:, S5_NS + w * c:S5_NS + w * (c + 1)]], axis=1)


def _state_pair(ref, j):
    return jnp.concatenate([ref[:, _lanes(j)], ref[:, S5_NS + LANES * j:S5_NS + LANES * (j + 1)]], axis=1)


def s5_fwd(usg, bdb, cdb, dvec, abar_b, ptab_b, name, gather=()):
    T = usg.shape[0]
    tc = _tile(T, SCAN_CHUNK)
    m = tc // SEGS
    nsteps = T // tc
    ng = len(gather)
    assert ptab_b.shape == (2, m, SEGS, S5_NS) and m % 2 == 0

    def body(*refs):
        u_ref, bdb_ref, cdb_ref, d_ref, a_ref, p_ref = refs[:6]
        ys_ref, sre_ref, sim_ref, sbf_ref = refs[6 + ng:10 + ng]
        src_re, src_im, dst_re, dst_im, cin_ref, carry_ref = refs[10 + 2 * ng:16 + 2 * ng]
        i = pl.program_id(0)
        if ng:
            phases = _gather_phases([s.shape for s in gather], refs[6:6 + ng], refs[10 + ng:10 + 2 * ng],
                                    *refs[16 + 2 * ng:])
            for phase, step in zip(phases, (0, nsteps // 2, (3 * nsteps) // 4, nsteps - 1)):
                pl.when(i == step)(phase)

        @pl.when(i == 0)
        def _():
            carry_ref[...] = jnp.zeros_like(carry_ref)

        u = u_ref[...]
        ub = u.astype(bf16)
        for j in range(NB_S5):
            bu = _dot(ub[:, _lanes(j // SB_PER_CB)], bdb_ref[j])
            src_re[:, _lanes(j)] = bu[:, :LANES]
            src_im[:, _lanes(j)] = bu[:, LANES:]
        for j0 in range(0, NB_S5, S5_JB):
            def kstep(k, st):
                rows = _step_rows(k)
                out = []
                for q in range(S5_JB):
                    ln = _lanes(j0 + q)
                    sr, si = st[2 * q], st[2 * q + 1]
                    ar, ai = a_ref[0, :, ln], a_ref[1, :, ln]
                    nr = ar * sr - ai * si + src_re[rows, ln]
                    ni = ar * si + ai * sr + src_im[rows, ln]
                    dst_re[rows, ln] = nr
                    dst_im[rows, ln] = ni
                    out += [nr, ni]
                return tuple(out)

            ends = lax.fori_loop(0, m, kstep, tuple(jnp.zeros((SEGS, LANES), f32) for _ in range(2 * S5_JB)))
            for q in range(S5_JB):
                ln = _lanes(j0 + q)
                er, ei = ends[2 * q], ends[2 * q + 1]
                cr, ci = carry_ref[0, :, ln], carry_ref[1, :, ln]
                amr, ami = p_ref[0, m - 1, 0:1, ln], p_ref[1, m - 1, 0:1, ln]
                rows_r, rows_i = [], []
                for s in range(SEGS):
                    rows_r.append(cr)
                    rows_i.append(ci)
                    cr, ci = (er[s:s + 1, :] + amr * cr - ami * ci, ei[s:s + 1, :] + amr * ci + ami * cr)
                cin_ref[0, 0:SEGS, ln] = _stack_rows(rows_r)
                cin_ref[1, 0:SEGS, ln] = _stack_rows(rows_i)
                carry_ref[0, :, ln] = cr
                carry_ref[1, :, ln] = ci
        cin_ref[:, SEGS:, :] = cin_ref[:, 0:SEGS, :]

        def fix(k2, _):
            rows = _step_rows(k2, PAIR)
            pr = p_ref[0, pl.ds(2 * k2, 2)].reshape(PAIR, S5_NS)
            pi = p_ref[1, pl.ds(2 * k2, 2)].reshape(PAIR, S5_NS)
            cr, ci = cin_ref[0], cin_ref[1]
            sr = dst_re[rows, :] + pr * cr - pi * ci
            si = dst_im[rows, :] + pr * ci + pi * cr
            sre_ref[rows, :] = sr
            sim_ref[rows, :] = si
            sbf_ref[rows, 0:S5_NS] = sr.astype(bf16)
            sbf_ref[rows, S5_NS:] = si.astype(bf16)
            return 0

        lax.fori_loop(0, m // 2, fix, 0)
        for c in range(CB_S5):
            ys_ref[:, _lanes(c)] = _dot(_state_cat(sbf_ref, c), cdb_ref[c]) + d_ref[:, _lanes(c)] * u[:, _lanes(c)]

    st = lambda w: _rows(tc, w)
    outs = pl.pallas_call(
        body, name=name, grid=(nsteps,),
        in_specs=[_rows(tc, S5_WIDTH, 0), _resident(bdb.shape), _resident(cdb.shape), _full((1, S5_WIDTH)),
                  _resident((2, SEGS, S5_NS)), _resident((2, m, SEGS, S5_NS))] + [ANY] * ng,
        out_specs=[st(S5_WIDTH), st(S5_NS), st(S5_NS), st(2 * S5_NS)] + [ANY] * ng,
        out_shape=[_S((T, S5_WIDTH)), _S((T, S5_NS)), _S((T, S5_NS)), _S((T, 2 * S5_NS), bf16)] + (
            _gather_out_shapes(gather) if ng else []),
        scratch_shapes=[pltpu.VMEM((tc, S5_NS), f32)] * 4 + [pltpu.VMEM((2, PAIR, S5_NS), f32),
                                                             pltpu.VMEM((2, 1, S5_NS), f32)] + (
            _gather_semaphores(ng) if ng else []),
        compiler_params=_cp("arbitrary"),
    )(usg, bdb, cdb, dvec, abar_b, ptab_b, *gather)
    return outs[:4], list(outs[4:])


def s5_bwd(gys, usg, s_re, s_im, s_bf, bdb, cdb, dvec, abar_b, ptab_rev_b, name):
    T = gys.shape[0]
    tc = _tile(T, SCAN_CHUNK)
    m = tc // SEGS
    nch = T // tc
    hb = tc // SUBLANES

    def body(gy_ref, u_ref, sre_ref, sim_ref, hre_ref, him_ref, sbf_ref, bdb_ref, cdb_ref, d_ref, a_ref, p_ref,
             gu_ref, gab_ref, gd_ref, gbdb_ref, gcdb_ref,
             src_re, src_im, dst_re, dst_im, lam_ref, cin_ref, acc_ref, carry_ref):
        i = pl.program_id(0)

        @pl.when(i == 0)
        def _():
            carry_ref[...] = jnp.zeros_like(carry_ref)
            for ref in (gab_ref, gd_ref, gbdb_ref, gcdb_ref):
                ref[...] = jnp.zeros_like(ref)

        first = i == nch - 1
        gy = gy_ref[...]
        gyb = gy.astype(bf16)
        u = u_ref[...]
        ub = u.astype(bf16)
        w = SB_PER_CB * LANES
        for c in range(CB_S5):
            gs = _dot_nt(gyb[:, _lanes(c)], cdb_ref[c])
            src_re[:, w * c:w * (c + 1)] = gs[:, :w]
            src_im[:, w * c:w * (c + 1)] = gs[:, w:]
            gcdb_ref[c] += _dot_tn(_state_cat(sbf_ref, c), gyb[:, _lanes(c)])
        for j0 in range(0, NB_S5, S5_JB):
            def kstep(kk, st):
                rows = _step_rows(m - 1 - kk)
                out = []
                for q in range(S5_JB):
                    ln = _lanes(j0 + q)
                    lr, li = st[2 * q], st[2 * q + 1]
                    ar, ai = a_ref[0, :, ln], a_ref[1, :, ln]
                    nr = ar * lr + ai * li + src_re[rows, ln]
                    ni = ar * li - ai * lr + src_im[rows, ln]
                    dst_re[rows, ln] = nr
                    dst_im[rows, ln] = ni
                    out += [nr, ni]
                return tuple(out)

            ends = lax.fori_loop(0, m, kstep, tuple(jnp.zeros((SEGS, LANES), f32) for _ in range(2 * S5_JB)))
            for q in range(S5_JB):
                ln = _lanes(j0 + q)
                er, ei = ends[2 * q], ends[2 * q + 1]
                cr, ci = carry_ref[0, :, ln], carry_ref[1, :, ln]
                amr, ami = p_ref[0, 0, 0:1, ln], p_ref[1, 0, 0:1, ln]
                rows_r, rows_i = [None] * SEGS, [None] * SEGS
                for s in reversed(range(SEGS)):
                    rows_r[s], rows_i[s] = cr, ci
                    cr, ci = (er[s:s + 1, :] + amr * cr + ami * ci, ei[s:s + 1, :] + amr * ci - ami * cr)
                cin_ref[0, 0:SEGS, ln] = _stack_rows(rows_r)
                cin_ref[1, 0:SEGS, ln] = _stack_rows(rows_i)
                carry_ref[0, :, ln] = cr
                carry_ref[1, :, ln] = ci
        cin_ref[:, SEGS:, :] = cin_ref[:, 0:SEGS, :]
        acc_ref[...] = jnp.zeros_like(acc_ref)

        def fix_rows(rows, k2, prev_re, prev_im):
            pr = p_ref[0, pl.ds(2 * k2, 2)].reshape(PAIR, S5_NS)
            pi = p_ref[1, pl.ds(2 * k2, 2)].reshape(PAIR, S5_NS)
            cr, ci = cin_ref[0], cin_ref[1]
            lr = dst_re[rows, :] + pr * cr + pi * ci
            li = dst_im[rows, :] + pr * ci - pi * cr
            lam_ref[rows, 0:S5_NS] = lr.astype(bf16)
            lam_ref[rows, S5_NS:] = li.astype(bf16)
            acc_ref[0] += lr * prev_re + li * prev_im
            acc_ref[1] += li * prev_re - lr * prev_im

        last = slice(tc - SUBLANES, tc)
        wrap_re = _down_a_segment(sre_ref[last, :], jnp.where(first, 0.0, hre_ref[SUBLANES - 1:SUBLANES, :]))
        wrap_im = _down_a_segment(sim_ref[last, :], jnp.where(first, 0.0, him_ref[SUBLANES - 1:SUBLANES, :]))
        fix_rows(pl.ds(0, PAIR), 0, jnp.concatenate([wrap_re, sre_ref[0:SUBLANES, :]], axis=0),
                 jnp.concatenate([wrap_im, sim_ref[0:SUBLANES, :]], axis=0))

        def fix(k2, _):
            prev = pl.ds(pl.multiple_of(k2 * PAIR - SUBLANES, SUBLANES), PAIR)
            fix_rows(_step_rows(k2, PAIR), k2, sre_ref[prev, :], sim_ref[prev, :])
            return 0

        lax.fori_loop(1, m // 2, fix, 0)
        gab_ref[0] += jnp.sum(acc_ref[0], axis=0, keepdims=True)
        gab_ref[1] += jnp.sum(acc_ref[1], axis=0, keepdims=True)
        for c in range(CB_S5):
            x = gy[:, _lanes(c)] * d_ref[:, _lanes(c)]
            for j in range(SB_PER_CB * c, SB_PER_CB * (c + 1)):
                pair = _state_pair(lam_ref, j)
                x = x + _dot_nt(pair, bdb_ref[j])
                gbdb_ref[j] += _dot_tn(ub[:, _lanes(c)], pair)
            gu_ref[:, _lanes(c)] = x.astype(bf16)
        gd_ref[...] += jnp.sum(gy * u, axis=0, keepdims=True)

    rev = lambda i: (nch - 1 - i, 0)
    halo = lambda i: (jnp.maximum((nch - 1 - i) * hb - 1, 0), 0)
    blk = lambda wd: pl.BlockSpec((tc, wd), rev)
    return pl.pallas_call(
        body, name=name, grid=(nch,),
        in_specs=[blk(S5_WIDTH), blk(S5_WIDTH), blk(S5_NS), blk(S5_NS),
                  pl.BlockSpec((SUBLANES, S5_NS), halo), pl.BlockSpec((SUBLANES, S5_NS), halo), blk(2 * S5_NS),
                  _resident(bdb.shape), _resident(cdb.shape), _full((1, S5_WIDTH)),
                  _resident((2, SEGS, S5_NS)), _resident((2, m, SEGS, S5_NS))],
        out_specs=[blk(S5_WIDTH), _full((2, 1, S5_NS)), _full((1, S5_WIDTH)), _full(bdb.shape), _full(cdb.shape)],
        out_shape=[_S((T, S5_WIDTH), bf16), _S((2, 1, S5_NS)), _S((1, S5_WIDTH)), _S(bdb.shape), _S(cdb.shape)],
        scratch_shapes=[pltpu.VMEM((tc, S5_NS), f32)] * 4 + [
            pltpu.VMEM((tc, 2 * S5_NS), bf16), pltpu.VMEM((2, PAIR, S5_NS), f32), pltpu.VMEM((2, PAIR, S5_NS), f32),
            pltpu.VMEM((2, 1, S5_NS), f32)],
        compiler_params=_cp("arbitrary"),
    )(gys, usg, s_re, s_im, s_re, s_im, s_bf, bdb, cdb, dvec, abar_b, ptab_rev_b)


def s5_post_fwd(ys, usg, wglu, wbs, name):
    T = ys.shape[0]
    tm = _tile(T, TOKEN_TILE_LIGHT)

    def body(ys_ref, sg_ref, wglu_ref, wbs_ref, glu_ref, zs_ref):
        glu = _dot(_gelu(ys_ref[...]).astype(bf16), wglu_ref[...])
        sg = sg_ref[...]
        y2 = glu[:, :S5_WIDTH] * _sig(glu[:, S5_WIDTH:]) * (sg * _sig(sg))
        glu_ref[...] = glu
        zs_ref[...] = _dot(y2.astype(bf16), wbs_ref[...])

    return pl.pallas_call(
        body, name=name, grid=(T // tm,),
        in_specs=[_rows(tm, S5_WIDTH), _rows(tm, S5_WIDTH, 1), _resident((S5_WIDTH, 2 * S5_WIDTH)),
                  _resident((S5_WIDTH, D_MODEL))],
        out_specs=[_rows(tm, 2 * S5_WIDTH), _rows(tm, D_MODEL)],
        out_shape=[_S((T, 2 * S5_WIDTH)), _S((T, D_MODEL))], compiler_params=_cp("parallel"),
    )(ys, usg, wglu, wbs)


def _accumulate(ref, part, step):
    @pl.when(step == 0)
    def _():
        ref[...] = part

    @pl.when(step > 0)
    def _():
        ref[...] += part


def s5_post_bwd(gzs, glu, usg, ys, wbs, wglu, name):
    T = ys.shape[0]
    tm = _tile(T, TOKEN_TILE_LIGHT)

    def body(gzs_ref, glu_ref, sg_ref, ys_ref, wbs_ref, wglu_ref, gys_ref, gsg_ref, gwbs_ref, gwglu_ref):
        i = pl.program_id(0)
        glu = glu_ref[...]
        a, b = glu[:, :S5_WIDTH], glu[:, S5_WIDTH:]
        sg = sg_ref[...]
        ys = ys_ref[...]
        sb, ssg = _sig(b), _sig(sg)
        silu = sg * ssg
        _accumulate(gwbs_ref, _dot_tn((a * sb * silu).astype(bf16), gzs_ref[...]), i)
        gy2 = _dot_nt(gzs_ref[...], wbs_ref[...])
        g_a = gy2 * sb * silu
        g_b = gy2 * a * sb * (1.0 - sb) * silu
        gsg_ref[...] = (gy2 * a * sb * ssg * (1.0 + sg * (1.0 - ssg))).astype(bf16)
        gglu = jnp.concatenate([g_a, g_b], axis=1).astype(bf16)
        _accumulate(gwglu_ref, _dot_tn(_gelu(ys).astype(bf16), gglu), i)
        gys_ref[...] = _dot_nt(gglu, wglu_ref[...]) * _gelu_grad(ys)

    return pl.pallas_call(
        body, name=name, grid=(T // tm,),
        in_specs=[_rows(tm, D_MODEL), _rows(tm, 2 * S5_WIDTH), _rows(tm, S5_WIDTH, 1), _rows(tm, S5_WIDTH),
                  _resident((S5_WIDTH, D_MODEL)), _resident((S5_WIDTH, 2 * S5_WIDTH))],
        out_specs=[_rows(tm, S5_WIDTH), _rows(tm, S5_WIDTH), _full((S5_WIDTH, D_MODEL)), _full((S5_WIDTH, 2 * S5_WIDTH))],
        out_shape=[_S((T, S5_WIDTH)), _S((T, S5_WIDTH), bf16), _S((S5_WIDTH, D_MODEL)), _S((S5_WIDTH, 2 * S5_WIDTH))],
        compiler_params=_cp("arbitrary"),
    )(gzs, glu, usg, ys, wbs, wglu)


NB_LRU = LRU_WIDTH // LANES
LRU_JB = 5
TAPS_BACK = CONV_WIDTH - 1
EDGE = TAPS_BACK * SUBLANES
HALO_ROWS = 4 * SUBLANES


def _down_a_segment(blk, entering_row):
    sub = lax.broadcasted_iota(jnp.int32, blk.shape, 0)
    return jnp.where(sub == 0, entering_row, pltpu.roll(blk, 1, 0))


def _up_a_segment(blk, entering_row):
    sub = lax.broadcasted_iota(jnp.int32, blk.shape, 0)
    return jnp.where(sub == SUBLANES - 1, entering_row, pltpu.roll(blk, SUBLANES - 1, 0))


def _stack_rows(rows):
    sub = lax.broadcasted_iota(jnp.int32, (SUBLANES,) + rows[0].shape[1:], 0)
    out = jnp.broadcast_to(rows[0], sub.shape)
    for s in range(1, SUBLANES):
        out = jnp.where(sub == s, rows[s], out)
    return out


def _fill_conv_window(xe, x_ref, xh_ref, is_first, tc):
    xe[EDGE:, :] = x_ref[...]
    for i in range(1, TAPS_BACK + 1):
        row = HALO_ROWS - SUBLANES * i + SUBLANES - 1
        entering = jnp.where(is_first, 0.0, xh_ref[row:row + 1, :])
        blk = x_ref[tc - SUBLANES * i:tc - SUBLANES * (i - 1), :]
        xe[EDGE - SUBLANES * i:EDGE - SUBLANES * (i - 1), :] = _down_a_segment(blk, entering)


def lru_fwd(lx, convw, convb, wa, wx, ba, bx, lam, name):
    T = lx.shape[0]
    tc = _tile(T, SCAN_CHUNK)
    m = tc // SEGS
    hb = tc // HALO_ROWS

    def body(x_ref, xh_ref, cw_ref, cb_ref, wa_ref, wx_ref, ba_ref, bx_ref, lam_ref,
             c_ref, r_ref, i_ref, om_ref, h_ref, xe, src_a, src_b, dst_a, dst_h, cin_ref, carry_ref):
        i = pl.program_id(0)

        @pl.when(i == 0)
        def _():
            carry_ref[...] = jnp.zeros_like(carry_ref)

        _fill_conv_window(xe, x_ref, xh_ref, i == 0, tc)
        c = cb_ref[...] + cw_ref[0:1, :] * xe[0:tc, :]
        for k in range(1, CONV_WIDTH):
            c = c + cw_ref[k:k + 1, :] * xe[SUBLANES * k:SUBLANES * k + tc, :]
        c_ref[...] = c
        sp = _softplus_neg(lam_ref[...])
        for j in range(NB_LRU):
            ln = _lanes(j)
            cj = c[:, ln]
            cjb = cj.astype(bf16)
            r = _sig(_dot(cjb, wa_ref[j]) + ba_ref[:, ln])
            g = _sig(_dot(cjb, wx_ref[j]) + bx_ref[:, ln])
            r_ref[:, ln] = r
            i_ref[:, ln] = g
            log_a = -LRU_C * r * sp[:, ln]
            a = jnp.exp(log_a)
            src_a[:, ln] = a
            om = _one_minus_sq(a, log_a)
            om_ref[:, ln] = om
            src_b[:, ln] = jnp.sqrt(om) * (g * cj)
        for j0 in range(0, NB_LRU, LRU_JB):
            def kstep(k, st):
                rows = _step_rows(k)
                out = []
                for q in range(LRU_JB):
                    ln = _lanes(j0 + q)
                    hh, ac = st[2 * q], st[2 * q + 1]
                    a = src_a[rows, ln]
                    hh = a * hh + src_b[rows, ln]
                    ac = a * ac
                    dst_h[rows, ln] = hh
                    dst_a[rows, ln] = ac
                    out += [hh, ac]
                return tuple(out)

            init = tuple(jnp.zeros((SEGS, LANES), f32) if q % 2 == 0 else jnp.ones((SEGS, LANES), f32)
                         for q in range(2 * LRU_JB))
            ends = lax.fori_loop(0, m, kstep, init)
            for q in range(LRU_JB):
                ln = _lanes(j0 + q)
                eh, ea = ends[2 * q], ends[2 * q + 1]
                cr = carry_ref[:, ln]
                rows_c = []
                for s in range(SEGS):
                    rows_c.append(cr)
                    cr = eh[s:s + 1, :] + ea[s:s + 1, :] * cr
                cin_ref[:, ln] = _stack_rows(rows_c)
                carry_ref[:, ln] = cr

        def fix(k, _):
            rows = _step_rows(k)
            h_ref[rows, :] = dst_h[rows, :] + dst_a[rows, :] * cin_ref[...]
            return 0

        lax.fori_loop(0, m, fix, 0)

    wide = lambda: _rows(tc, LRU_WIDTH)
    buf = lambda rows: pltpu.VMEM((rows, LRU_WIDTH), f32)
    return pl.pallas_call(
        body, name=name, grid=(T // tc,),
        in_specs=[wide(), pl.BlockSpec((HALO_ROWS, LRU_WIDTH), lambda i: (jnp.maximum(i * hb - 1, 0), 0)),
                  _full((CONV_WIDTH, LRU_WIDTH)), _full((1, LRU_WIDTH)),
                  _full((LRU_HEADS, LRU_HEAD_DIM, LRU_HEAD_DIM)), _full((LRU_HEADS, LRU_HEAD_DIM, LRU_HEAD_DIM)),
                  _full((1, LRU_WIDTH)), _full((1, LRU_WIDTH)), _full((1, LRU_WIDTH))],
        out_specs=[wide(), wide(), wide(), wide(), wide()],
        out_shape=[_S((T, LRU_WIDTH))] * 5,
        scratch_shapes=[buf(tc + EDGE), buf(tc), buf(tc), buf(tc), buf(tc), buf(SEGS), buf(1)],
        compiler_params=_cp("arbitrary"),
    )(lx, lx, convw, convb, wa, wx, ba, bx, lam)


def lru_bwd(gh, h, c, r, gi, om, lx, convw, wa, wx, lam, name, exchange=()):
    T = gh.shape[0]
    tc = _tile(T, SCAN_CHUNK)
    m = tc // SEGS
    nch = T // tc
    ne = len(exchange)
    NI = 13

    def body(*refs):
        (gh_ref, h_ref, hh_ref, c_ref, r_ref, i_ref, om_ref, x_ref, xh_ref, cw_ref, wa_ref, wx_ref, lam_ref) = refs[:NI]
        glx_ref, gwa_ref, gwx_ref, gba_ref, gbx_ref, glam_ref, gcb_ref, gcw_ref = refs[NI + ne:NI + 8 + ne]
        (src_a, src_m, dst_a, dst_m, mbuf, hbuf, xe, gce, cin_ref, gcc_ref, carry_ref) = refs[NI + 8 + 2 * ne:NI + 19 + 2 * ne]
        i = pl.program_id(0)
        if ne:
            start, finish = _chips_phases(refs[NI:NI + ne], refs[NI + 8 + ne:NI + 8 + 2 * ne], *refs[NI + 19 + 2 * ne:])
            pl.when(i == 0)(start)
            pl.when(i == nch - 1)(finish)

        @pl.when(i == 0)
        def _():
            carry_ref[...] = jnp.zeros_like(carry_ref)
            gcc_ref[...] = jnp.zeros_like(gcc_ref)
            for ref in (gwa_ref, gwx_ref, gba_ref, gbx_ref, glam_ref, gcb_ref, gcw_ref):
                ref[...] = jnp.zeros_like(ref)

        first = i == nch - 1
        last = slice(tc - SUBLANES, tc)
        hbuf[SUBLANES:, :] = h_ref[...]
        hbuf[0:SUBLANES, :] = _down_a_segment(h_ref[last, :], jnp.where(first, 0.0, hh_ref[SUBLANES - 1:SUBLANES, :]))
        _fill_conv_window(xe, x_ref, xh_ref, first, tc)
        lam_v = lam_ref[...]
        sp = _softplus_neg(lam_v)
        a_all = jnp.exp(-LRU_C * r_ref[...] * sp)
        src_a[...] = a_all
        src_m[...] = a_all * gh_ref[...]
        for j0 in range(0, NB_LRU, LRU_JB):
            def kstep(kk, st):
                rows = _step_rows(m - 1 - kk)
                out = []
                for q in range(LRU_JB):
                    ln = _lanes(j0 + q)
                    mu, ac = st[2 * q], st[2 * q + 1]
                    a = src_a[rows, ln]
                    mu = a * mu + src_m[rows, ln]
                    ac = a * ac
                    dst_m[rows, ln] = mu
                    dst_a[rows, ln] = ac
                    out += [mu, ac]
                return tuple(out)

            init = tuple(jnp.zeros((SEGS, LANES), f32) if q % 2 == 0 else jnp.ones((SEGS, LANES), f32)
                         for q in range(2 * LRU_JB))
            ends = lax.fori_loop(0, m, kstep, init)
            for q in range(LRU_JB):
                ln = _lanes(j0 + q)
                em, ea = ends[2 * q], ends[2 * q + 1]
                cr = carry_ref[:, ln]
                rows_c = [None] * SEGS
                for s in reversed(range(SEGS)):
                    rows_c[s] = cr
                    cr = em[s:s + 1, :] + ea[s:s + 1, :] * cr
                cin_ref[:, ln] = _stack_rows(rows_c)
                carry_ref[:, ln] = cr

        def fix(k, _):
            rows = _step_rows(k)
            mbuf[rows, :] = dst_m[rows, :] + dst_a[rows, :] * cin_ref[...]
            return 0

        lax.fori_loop(0, m, fix, 0)
        mbuf[tc:, :] = _up_a_segment(mbuf[0:SUBLANES, :], cin_ref[SUBLANES - 1:SUBLANES, :])
        sneg = _sig(-lam_v)
        for j in range(NB_LRU):
            ln = _lanes(j)
            lamt = gh_ref[:, ln] + mbuf[SUBLANES:, ln]
            rj, ij, cj = r_ref[:, ln], i_ref[:, ln], c_ref[:, ln]
            a = src_a[:, ln]
            om = om_ref[:, ln]
            inv_mult = lax.rsqrt(om)
            mult = om * inv_mult
            g_a = lamt * hbuf[0:tc, ln]
            g_mult = lamt * ij * cj
            g_i = lamt * mult * cj
            g_c = lamt * mult * ij
            g_log_a = g_a * a - g_mult * a * a * inv_mult
            glam_ref[:, ln] += jnp.sum(g_log_a * rj, axis=0, keepdims=True) * LRU_C * sneg[:, ln]
            g_ra = g_log_a * (-LRU_C) * sp[:, ln] * rj * (1.0 - rj)
            g_ia = g_i * ij * (1.0 - ij)
            gba_ref[:, ln] += jnp.sum(g_ra, axis=0, keepdims=True)
            gbx_ref[:, ln] += jnp.sum(g_ia, axis=0, keepdims=True)
            cjb, grb, gib = cj.astype(bf16), g_ra.astype(bf16), g_ia.astype(bf16)
            gwa_ref[j] += _dot_tn(cjb, grb)
            gwx_ref[j] += _dot_tn(cjb, gib)
            g_c = g_c + _dot_nt(grb, wa_ref[j]) + _dot_nt(gib, wx_ref[j])
            gce[0:tc, ln] = g_c
            gcb_ref[:, ln] += jnp.sum(g_c, axis=0, keepdims=True)
        for d in range(TAPS_BACK):
            blk = slice(SUBLANES * d, SUBLANES * (d + 1))
            gce[tc + SUBLANES * d:tc + SUBLANES * (d + 1), :] = _up_a_segment(gce[blk, :], gcc_ref[SUBLANES * d:SUBLANES * d + 1, :])
        gcc_ref[...] = gce[0:EDGE, :]
        gc = gce[0:tc, :]
        glx = cw_ref[CONV_WIDTH - 1:CONV_WIDTH, :] * gc
        gcw_ref[CONV_WIDTH - 1:CONV_WIDTH, :] += jnp.sum(gc * xe[EDGE:EDGE + tc, :], axis=0, keepdims=True)
        for k in range(CONV_WIDTH - 1):
            off = SUBLANES * (CONV_WIDTH - 1 - k)
            glx = glx + cw_ref[k:k + 1, :] * gce[off:off + tc, :]
            gcw_ref[k:k + 1, :] += jnp.sum(gc * xe[EDGE - off:EDGE - off + tc, :], axis=0, keepdims=True)
        glx_ref[...] = glx.astype(bf16)

    rev = lambda i: (nch - 1 - i, 0)
    halo = lambda rows: (lambda i: (jnp.maximum((nch - 1 - i) * (tc // rows) - 1, 0), 0))
    wide = lambda: pl.BlockSpec((tc, LRU_WIDTH), rev)
    vec = lambda: _full((1, LRU_WIDTH))
    hd = lambda: _full((LRU_HEADS, LRU_HEAD_DIM, LRU_HEAD_DIM))
    buf = lambda rows: pltpu.VMEM((rows, LRU_WIDTH), f32)
    outs = pl.pallas_call(
        body, name=name, grid=(nch,),
        in_specs=[wide(), wide(), pl.BlockSpec((SUBLANES, LRU_WIDTH), halo(SUBLANES)), wide(), wide(), wide(), wide(), wide(),
                  pl.BlockSpec((HALO_ROWS, LRU_WIDTH), halo(HALO_ROWS)), _full((CONV_WIDTH, LRU_WIDTH)), hd(), hd(), vec()]
        + [ANY] * ne,
        out_specs=[wide(), hd(), hd(), vec(), vec(), vec(), vec(), _full((CONV_WIDTH, LRU_WIDTH))] + [ANY] * ne,
        out_shape=[_S((T, LRU_WIDTH), bf16), _S((LRU_HEADS, LRU_HEAD_DIM, LRU_HEAD_DIM)),
                   _S((LRU_HEADS, LRU_HEAD_DIM, LRU_HEAD_DIM)), _S((1, LRU_WIDTH)), _S((1, LRU_WIDTH)),
                   _S((1, LRU_WIDTH)), _S((1, LRU_WIDTH)), _S((CONV_WIDTH, LRU_WIDTH))] + (
            _chips_out_shapes(exchange) if ne else []),
        scratch_shapes=[buf(tc), buf(tc), buf(tc), buf(tc), buf(tc + SUBLANES), buf(tc + SUBLANES), buf(tc + EDGE),
                        buf(tc + EDGE), buf(SEGS), buf(EDGE), buf(1)] + (_chips_semaphores(ne) if ne else []),
        compiler_params=_cp("arbitrary"),
    )(gh, h, h, c, r, gi, om, lx, lx, convw, wa, wx, lam, *exchange)
    return outs[:8], list(outs[8:])


def merge_fwd(h, lg, zs, gsl, x, p, wbl, wout, gpost, wple, wpg, name):
    T = x.shape[0]
    tm = _tile(T, TOKEN_TILE)

    def body(h_ref, lg_ref, zs_ref, gs_ref, gl_ref, x_ref, p_ref, wbl_ref, wout_ref, gp_ref, wple_ref, wpg_ref,
             zl_ref, mix_ref, xo_ref):
        lg_v = lg_ref[...]
        yl = h_ref[...] * (lg_v * _sig(lg_v))
        zl = _dot(yl.astype(bf16), wbl_ref[...])
        merged = _sig(gs_ref[...]) * zs_ref[...] + _sig(gl_ref[...]) * zl
        mix = _dot(merged.astype(bf16), wout_ref[...])
        r2 = lax.rsqrt(jnp.mean(mix * mix, axis=-1, keepdims=True) + NORM_EPS)
        x1 = x_ref[...] + mix * r2 * gp_ref[...]
        q = _dot(x1.astype(bf16), wpg_ref[...])
        pe = _dot(p_ref[...].astype(bf16), wple_ref[...])
        zl_ref[...], mix_ref[...] = zl, mix
        xo_ref[...] = x1 + pe * _sig(q)

    dm = lambda: _rows(tm, D_MODEL)
    return pl.pallas_call(
        body, name=name, grid=(T // tm,),
        in_specs=[_rows(tm, LRU_WIDTH), _rows(tm, LRU_WIDTH), dm(), _rows(tm, D_MODEL, 0), _rows(tm, D_MODEL, 1), dm(),
                  _rows(tm, PLE_DIM), _resident((LRU_WIDTH, D_MODEL)), _resident((D_MODEL, D_MODEL)), _full((1, D_MODEL)),
                  _resident((PLE_DIM, D_MODEL)), _resident((D_MODEL, D_MODEL))],
        out_specs=[dm(), dm(), dm()],
        out_shape=[_S((T, D_MODEL))] * 3, compiler_params=_cp("parallel"),
    )(h, lg, zs, gsl, gsl, x, p, wbl, wout, gpost, wple, wpg)


def post_bwd(gx2, mix, x, p, wpg, wple, gpost, name):
    T = x.shape[0]
    tm = _tile(T, TOKEN_TILE_LIGHT)

    def body(gx2_ref, mix_ref, x_ref, p_ref, wpg_ref, wple_ref, gp_ref, gres_ref, gmix_ref, ggp_ref, gwpg_ref, gwple_ref):
        i = pl.program_id(0)
        gx2 = gx2_ref[...]
        mix = mix_ref[...]
        gp = gp_ref[...]
        r2 = lax.rsqrt(jnp.mean(mix * mix, axis=-1, keepdims=True) + NORM_EPS)
        nrm = mix * r2
        x1b = (x_ref[...] + nrm * gp).astype(bf16)
        pb = p_ref[...].astype(bf16)
        sq = _sig(_dot(x1b, wpg_ref[...]))
        pe = _dot(pb, wple_ref[...])
        gq = (gx2 * pe * sq * (1.0 - sq)).astype(bf16)
        _accumulate(gwple_ref, _dot_tn(pb, (gx2 * sq).astype(bf16)), i)
        _accumulate(gwpg_ref, _dot_tn(x1b, gq), i)
        gx1 = gx2 + _dot_nt(gq, wpg_ref[...])
        gres_ref[...] = gx1
        _accumulate(ggp_ref, jnp.sum(gx1 * nrm, axis=0, keepdims=True), i)
        gy = gx1 * gp
        gmix_ref[...] = (r2 * (gy - nrm * jnp.mean(gy * nrm, axis=-1, keepdims=True))).astype(bf16)

    dm = lambda: _rows(tm, D_MODEL)
    return pl.pallas_call(
        body, name=name, grid=(T // tm,),
        in_specs=[dm(), dm(), dm(), _rows(tm, PLE_DIM), _resident((D_MODEL, D_MODEL)), _resident((PLE_DIM, D_MODEL)),
                  _full((1, D_MODEL))],
        out_specs=[dm(), dm(), _full((1, D_MODEL)), _full((D_MODEL, D_MODEL)), _full((PLE_DIM, D_MODEL))],
        out_shape=[_S((T, D_MODEL)), _S((T, D_MODEL), bf16), _S((1, D_MODEL)), _S((D_MODEL, D_MODEL)),
                   _S((PLE_DIM, D_MODEL))],
        compiler_params=_cp("arbitrary"),
    )(gx2, mix, x, p, wpg, wple, gpost)


def gate_bwd(gmix, zl, zs, gsl, wout, name):
    T = zl.shape[0]
    tm = _tile(T, TOKEN_TILE_LIGHT)

    def body(gmix_ref, zl_ref, zs_ref, gs_ref, gl_ref, wout_ref, gzs_ref, gzl_ref, ggsl_ref, gwout_ref):
        i = pl.program_id(0)
        gmix = gmix_ref[...]
        gmerged = _dot_nt(gmix, wout_ref[...])
        zs, zl = zs_ref[...], zl_ref[...]
        ss, sl = _sig(gs_ref[...]), _sig(gl_ref[...])
        _accumulate(gwout_ref, _dot_tn((ss * zs + sl * zl).astype(bf16), gmix), i)
        gzs_ref[...] = (gmerged * ss).astype(bf16)
        gzl_ref[...] = (gmerged * sl).astype(bf16)
        ggsl_ref[:, :D_MODEL] = (gmerged * zs * ss * (1.0 - ss)).astype(bf16)
        ggsl_ref[:, D_MODEL:] = (gmerged * zl * sl * (1.0 - sl)).astype(bf16)

    dm = lambda: _rows(tm, D_MODEL)
    return pl.pallas_call(
        body, name=name, grid=(T // tm,),
        in_specs=[dm(), dm(), dm(), _rows(tm, D_MODEL, 0), _rows(tm, D_MODEL, 1), _resident((D_MODEL, D_MODEL))],
        out_specs=[dm(), dm(), _rows(tm, 2 * D_MODEL), _full((D_MODEL, D_MODEL))],
        out_shape=[_S((T, D_MODEL), bf16), _S((T, D_MODEL), bf16), _S((T, 2 * D_MODEL), bf16), _S((D_MODEL, D_MODEL))],
        compiler_params=_cp("arbitrary"),
    )(gmix, zl, zs, gsl, gsl, wout)


def lru_out_bwd(gzl, h, lg, wbl, name):
    T = h.shape[0]
    tm = _tile(T, TOKEN_TILE_LIGHT)

    def body(gzl_ref, h_ref, lg_ref, wbl_ref, gh_ref, glg_ref, gwbl_ref):
        i = pl.program_id(0)
        gzl = gzl_ref[...]
        lg_v, hv = lg_ref[...], h_ref[...]
        slg = _sig(lg_v)
        silu = lg_v * slg
        _accumulate(gwbl_ref, _dot_tn((hv * silu).astype(bf16), gzl), i)
        gyl = _dot_nt(gzl, wbl_ref[...])
        gh_ref[...] = gyl * silu
        glg_ref[...] = (gyl * hv * slg * (1.0 + lg_v * (1.0 - slg))).astype(bf16)

    lw = lambda: _rows(tm, LRU_WIDTH)
    return pl.pallas_call(
        body, name=name, grid=(T // tm,),
        in_specs=[_rows(tm, D_MODEL), lw(), lw(), _resident((LRU_WIDTH, D_MODEL))],
        out_specs=[lw(), lw(), _full((LRU_WIDTH, D_MODEL))],
        out_shape=[_S((T, LRU_WIDTH)), _S((T, LRU_WIDTH), bf16), _S((LRU_WIDTH, D_MODEL))],
        compiler_params=_cp("arbitrary"),
    )(gzl, h, lg, wbl)


def in_proj_bwd(pieces, win, x, gres, g, name):
    T = x.shape[0]
    tm = _tile(T, MM_TILE_M // 2)
    widths = [pc.shape[1] for pc in pieces]
    offs = [sum(widths[:k]) for k in range(len(widths))]

    def body(*refs):
        pc_refs = refs[:len(widths)]
        w_ref, x_ref, gres_ref, g_ref, gx_ref, gg_ref = refs[len(widths):]
        i = pl.program_id(0)
        ghv = _dot_nt(pc_refs[0][...], w_ref[:, offs[0]:offs[0] + widths[0]])
        for k in range(1, len(widths)):
            ghv = ghv + _dot_nt(pc_refs[k][...], w_ref[:, offs[k]:offs[k] + widths[k]])
        xv = x_ref[...]
        r = lax.rsqrt(jnp.mean(xv * xv, axis=-1, keepdims=True) + NORM_EPS)
        nrm = xv * r
        gy = ghv * g_ref[...]
        gx_ref[...] = gres_ref[...] + r * (gy - nrm * jnp.mean(gy * nrm, axis=-1, keepdims=True))
        _accumulate(gg_ref, jnp.sum(ghv * nrm, axis=0, keepdims=True), i)

    dm = lambda: _rows(tm, D_MODEL)
    return pl.pallas_call(
        body, name=name, grid=(T // tm,),
        in_specs=[_rows(tm, wd) for wd in widths] + [_resident(win.shape), dm(), dm(), _full((1, D_MODEL))],
        out_specs=[dm(), _full((1, D_MODEL))],
        out_shape=[_S((T, D_MODEL)), _S((1, D_MODEL))], compiler_params=_cp("arbitrary"),
    )(*pieces, win, x, gres, g)


def loss_head(y, target, name):
    T = y.shape[0]
    tm = _tile(T, TOKEN_TILE_LIGHT)

    def body(y_ref, t_ref, l_ref, g_ref):
        i = pl.program_id(0)
        e = y_ref[...] - t_ref[...]
        g_ref[...] = e * (1.0 / D_MODEL)
        part = 0.5 * jnp.sum(jnp.sum(e * e, axis=-1, keepdims=True) * (1.0 / D_MODEL), axis=0, keepdims=True)

        @pl.when(i == 0)
        def _():
            l_ref[...] = part

        @pl.when(i > 0)
        def _():
            l_ref[...] += part

    return pl.pallas_call(
        body, name=name, grid=(T // tm,),
        in_specs=[_rows(tm, D_MODEL), _rows(tm, D_MODEL)], out_specs=[_full((1, 1)), _rows(tm, D_MODEL)],
        out_shape=[_S((1, 1)), _S((T, D_MODEL))],
        compiler_params=_cp("arbitrary"),
    )(y, target)


def _s5_operands(w, m, tag):
    b_re_t = jnp.transpose(w['s5_b_re'], (2, 0, 1))
    b_im_t = jnp.transpose(w['s5_b_im'], (2, 0, 1))
    ldt = w['s5_log_dt'][:, None]
    ab, pw, bb = s5_prep(w['s5_a_re'], w['s5_a_im'], ldt, b_re_t, b_im_t, m, "s5_prep" + tag)
    over_sublanes = lambda t: jnp.broadcast_to(t[..., None, :], t.shape[:-1] + (SEGS, S5_NS))
    ptab = pw.reshape(2, m, S5_NS)
    return dict(abar_b=over_sublanes(ab.reshape(2, S5_NS)), ptab_b=over_sublanes(ptab),
                ptab_rev_b=over_sublanes(ptab[:, ::-1, :]), bdb=_pack_bdb(bb).astype(bf16),
                cdb=_pack_cdb(w['s5_c_re'], w['s5_c_im']).astype(bf16), dvec=w['s5_d'][None, :],
                prep_in=(w['s5_a_re'], w['s5_a_im'], ldt, b_re_t, b_im_t))


def layer_fwd(x, p, w, tag, gather=()):
    T = x.shape[0]
    m = _tile(T, SCAN_CHUNK) // SEGS
    s5 = _s5_operands(w, m, tag)
    h_bf = rms_fwd(x, w['g_pre'][None, :], "rms_fwd" + tag)
    win = w['w_in']
    usg = mm_nn(h_bf, win[:, :2 * S5_WIDTH], "proj_s5" + tag)
    lx = mm_nn(h_bf, win[:, 2 * S5_WIDTH:2 * S5_WIDTH + LRU_WIDTH], "proj_lx" + tag)
    lg = mm_nn(h_bf, win[:, 2 * S5_WIDTH + LRU_WIDTH:2 * S5_WIDTH + 2 * LRU_WIDTH], "proj_lg" + tag)
    gsl = mm_nn(h_bf, win[:, 2 * S5_WIDTH + 2 * LRU_WIDTH:], "proj_gate" + tag)
    (ys, s_re, s_im, s_bf), gathered = s5_fwd(usg, s5['bdb'], s5['cdb'], s5['dvec'], s5['abar_b'], s5['ptab_b'],
                                              "s5_fwd" + tag, gather)
    glu, zs = s5_post_fwd(ys, usg, w['w_glu'], w['w_bs'], "s5_post_fwd" + tag)
    wa, wx = w['lru_w_a'].astype(bf16), w['lru_w_x'].astype(bf16)
    c, r, gi, om, hs = lru_fwd(lx, w['conv_w'], w['conv_b'][None, :], wa, wx, w['lru_b_a'][None, :],
                               w['lru_b_x'][None, :], w['lru_lambda'][None, :], "lru_fwd" + tag)
    zl, mix, x_out = merge_fwd(hs, lg, zs, gsl, x, p, w['w_bl'], w['w_out'], w['g_post'][None, :],
                               w['w_ple'], w['w_ple_gate'], "merge_fwd" + tag)
    saved = dict(x=x, p=p, h_bf=h_bf, usg=usg, lx=lx, lg=lg, gsl=gsl, ys=ys, s_re=s_re, s_im=s_im, s_bf=s_bf, glu=glu,
                 zs=zs, c=c, r=r, gi=gi, om=om, hs=hs, zl=zl, mix=mix, s5=s5, wa=wa, wx=wx)
    return x_out, saved, gathered


def layer_bwd(gx_out, w, sv, tag, exchange=()):
    s5 = sv['s5']
    g = {}
    gres, gmix, g_gpost, g['w_ple_gate'], g['w_ple'] = post_bwd(
        gx_out, sv['mix'], sv['x'], sv['p'], w['w_ple_gate'], w['w_ple'], w['g_post'][None, :], "post_bwd" + tag)
    gzs, gzl, ggsl, g['w_out'] = gate_bwd(gmix, sv['zl'], sv['zs'], sv['gsl'], w['w_out'], "gate_bwd" + tag)
    g_h, g_lg, g['w_bl'] = lru_out_bwd(gzl, sv['hs'], sv['lg'], w['w_bl'], "lru_out_bwd" + tag)
    g['g_post'] = g_gpost[0]
    (g_lx, g_wa, g_wx, g_ba, g_bx, g_lam, g_cb, g_cw), exchanged = lru_bwd(
        g_h, sv['hs'], sv['c'], sv['r'], sv['gi'], sv['om'], sv['lx'], w['conv_w'], sv['wa'], sv['wx'],
        w['lru_lambda'][None, :], "lru_bwd" + tag, exchange)
    g['lru_w_a'], g['lru_w_x'] = g_wa, g_wx
    g['lru_b_a'], g['lru_b_x'], g['lru_lambda'], g['conv_b'], g['conv_w'] = g_ba[0], g_bx[0], g_lam[0], g_cb[0], g_cw
    g_ys, g_sg, g['w_bs'], g['w_glu'] = s5_post_bwd(gzs, sv['glu'], sv['usg'], sv['ys'], w['w_bs'], w['w_glu'],
                                                    "s5_post_bwd" + tag)
    g_u, g_ab, g_d, g_bdb, g_cdb = s5_bwd(g_ys, sv['usg'], sv['s_re'], sv['s_im'], sv['s_bf'], s5['bdb'], s5['cdb'],
                                          s5['dvec'], s5['abar_b'], s5['ptab_rev_b'], "s5_bwd" + tag)
    g['s5_d'] = g_d[0]
    g['s5_c_re'], g['s5_c_im'] = _unpack_cdb(g_cdb)
    g_are, g_aim, g_ldt, g_bre_t, g_bim_t = s5_prep_bwd(*s5['prep_in'], g_ab.reshape(2, S5_GROUPS, S5_STATE),
                                                       _unpack_bdb(g_bdb), "s5_prep_bwd" + tag)
    g['s5_a_re'], g['s5_a_im'], g['s5_log_dt'] = g_are, g_aim, g_ldt
    g['s5_b_re'] = jnp.transpose(g_bre_t, (1, 2, 0))
    g['s5_b_im'] = jnp.transpose(g_bim_t, (1, 2, 0))
    pieces = [g_u, g_sg, g_lx, g_lg, ggsl]
    g['w_in'] = jnp.concatenate([mm_tn(sv['h_bf'], pc, "gw_in%d%s" % (k, tag)) for k, pc in enumerate(pieces)], axis=1)
    gx, g_gpre = in_proj_bwd(pieces, w['w_in'], sv['x'], gres, w['g_pre'][None, :], "in_proj_bwd" + tag)
    g['g_pre'] = g_gpre[0]
    return gx, g, exchanged


def _as_2d(a):
    return a.reshape((-1, a.shape[-1])) if a.ndim > 1 else a.reshape((1, -1))


def _adamw_update(w, gv, m, v):
    nm = ADAM_B1 * m + (1.0 - ADAM_B1) * gv
    nv = ADAM_B2 * v + (1.0 - ADAM_B2) * (gv * gv)
    bc1 = 1.0 - ADAM_B1 ** ADAM_STEP
    bc2 = 1.0 - ADAM_B2 ** ADAM_STEP
    return -ADAM_LR * ((nm / bc1) / (jnp.sqrt(nv / bc2) + ADAM_EPS) + ADAM_WD * w), nm, nv


def adamw(w, g, m, v, name):
    shape = w.shape
    w2, g2, m2, v2 = _as_2d(w), _as_2d(g), _as_2d(m), _as_2d(v)
    R, C = w2.shape
    tr = _row_tile(R, C)

    def body(w_ref, g_ref, m_ref, v_ref, d_ref, nm_ref, nv_ref):
        d_ref[...], nm_ref[...], nv_ref[...] = _adamw_update(w_ref[...], g_ref[...], m_ref[...], v_ref[...])

    spec = lambda: pl.BlockSpec((tr, C), lambda i: (i, 0))
    d, nm, nv = pl.pallas_call(
        body, name=name, grid=(R // tr,), in_specs=[spec() for _ in range(4)], out_specs=[spec() for _ in range(3)],
        out_shape=[_S((R, C))] * 3, compiler_params=_cp("parallel"),
    )(w2, g2, m2, v2)
    return d.reshape(shape), nm.reshape(shape), nv.reshape(shape)


def adamw_reduce(w, parts, theirs, m, v, chip, name):
    shape = w.shape
    C = shape[-1]
    R = math.prod(shape[1:-1])
    w3, m3, v3 = w.reshape(DEPTH, R, C), m.reshape(DEPTH, R, C), v.reshape(DEPTH, R, C)
    tr = _row_tile(R, C)

    def body(chip_ref, w_ref, *refs):
        layer_refs, (m_ref, v_ref, g_ref, d_ref, nm_ref, nv_ref) = refs[:2 * DEPTH], refs[2 * DEPTH:]
        layer = pl.program_id(0)
        for l in range(DEPTH):
            @pl.when(layer == l)
            def _():
                a_ref, t_ref = layer_refs[2 * l], layer_refs[2 * l + 1]
                gv = ((a_ref[0] + t_ref[0].astype(f32)) + t_ref[1].astype(f32)) + t_ref[2].astype(f32)
                g_ref[0] = gv
                d_ref[0], nm_ref[0], nv_ref[0] = _adamw_update(w_ref[0], gv, m_ref[0], v_ref[0])

    spec = lambda: pl.BlockSpec((1, tr, C), lambda l, i, c: (l, i, 0))
    rows_of = lambda l: (lambda ll, i, c: jnp.where(ll == l, i, 0))
    layer_specs = []
    for l in range(DEPTH):
        layer_specs.append(pl.BlockSpec((1, tr, C), lambda ll, i, c, r=rows_of(l): (c[0], r(ll, i, c), 0)))
        layer_specs.append(pl.BlockSpec((3, tr, C), lambda ll, i, c, r=rows_of(l): (0, r(ll, i, c), 0)))
    grid_spec = pltpu.PrefetchScalarGridSpec(
        num_scalar_prefetch=1, grid=(DEPTH, R // tr),
        in_specs=[spec()] + layer_specs + [spec(), spec()], out_specs=[spec() for _ in range(4)])
    operands = [x.reshape(x.shape[0], R, C) for l in range(DEPTH) for x in (parts[l], theirs[l])]
    g, d, nm, nv = pl.pallas_call(
        body, name=name, grid_spec=grid_spec, out_shape=[_S((DEPTH, R, C))] * 4,
        compiler_params=_cp("arbitrary", "arbitrary"),
    )(chip, w3, *operands, m3, v3)
    return g.reshape(shape), d.reshape(shape), nm.reshape(shape), nv.reshape(shape)


MESH = pl.DeviceIdType.MESH
ANY = pl.BlockSpec(memory_space=pl.ANY)


def _place():
    return lax.axis_index("x"), lax.axis_index("y"), lax.axis_index("c")


def _other_chips(mx, my):
    return [(1 - mx, my), (mx, 1 - my), (1 - mx, 1 - my)]


def all_gather(shards, name):
    nb = len(shards)

    def body(*refs):
        phases = _gather_phases([s.shape for s in shards], refs[:nb], refs[nb:2 * nb], *refs[2 * nb:])
        for phase in phases:
            phase()

    outs = pl.pallas_call(
        body, name=name, out_shape=_gather_out_shapes(shards), in_specs=[ANY] * nb, out_specs=[ANY] * nb,
        scratch_shapes=_gather_semaphores(nb),
    )(*shards)
    return list(outs)


GATHER_COPIES = 9
OWN_SIB, OWN_X, OWN_Y, X_SIB, Y_SIB, RELAY_X, RELAY_Y, DIAG0_SIB, DIAG1_SIB = range(GATHER_COPIES)


def _gather_out_shapes(shards):
    return [_S((N_DEV,) + s.shape, s.dtype) for s in shards]


def _gather_semaphores(nb):
    return [pltpu.SemaphoreType.DMA((nb, GATHER_COPIES)), pltpu.SemaphoreType.DMA((nb, GATHER_COPIES)),
            pltpu.SemaphoreType.DMA((nb,))]


def _gather_phases(shapes, x_refs, out_refs, send_sems, recv_sems, local_sems):
    nb = len(shapes)
    mx, my, mc = _place()
    sibling, xn, yn = (mx, my, 1 - mc), (1 - mx, my, mc), (mx, 1 - my, mc)

    def block(b, px, py, pc, half=None):
        ref = out_refs[b].at[4 * px + 2 * py + pc]
        if half is None:
            return ref
        n = shapes[b][0] // 2
        return ref.at[pl.ds(half * n, n)]

    def copy(b, k, dst, to, src=None):
        return pltpu.make_async_remote_copy(
            src_ref=dst if src is None else src, dst_ref=dst, send_sem=send_sems.at[b, k],
            recv_sem=recv_sems.at[b, k], device_id=to, device_id_type=MESH)

    def send(b, k):
        if k in (OWN_X, OWN_Y, OWN_SIB):
            return copy(b, k, block(b, mx, my, mc), {OWN_X: xn, OWN_Y: yn, OWN_SIB: sibling}[k], src=x_refs[b])
        what, to = {RELAY_X: ((1 - mx, my, mc, 0), yn), RELAY_Y: ((mx, 1 - my, mc, 1), xn),
                    X_SIB: ((1 - mx, my, mc), sibling), Y_SIB: ((mx, 1 - my, mc), sibling),
                    DIAG0_SIB: ((1 - mx, 1 - my, mc, 0), sibling), DIAG1_SIB: ((1 - mx, 1 - my, mc, 1), sibling)}[k]
        return copy(b, k, block(b, *what), to)

    def local(b):
        return pltpu.make_async_copy(x_refs[b], block(b, mx, my, mc), local_sems.at[b])

    def send_own():
        for k in (OWN_X, OWN_Y, OWN_SIB):
            for b in range(nb):
                send(b, k).start()
        for b in range(nb):
            local(b).start()

    def relay_neighbours():
        for b in range(nb):
            copy(b, OWN_X, block(b, 1 - mx, my, mc), xn).wait_recv()
            send(b, RELAY_X).start()
            send(b, X_SIB).start()
        for b in range(nb):
            copy(b, OWN_Y, block(b, mx, 1 - my, mc), yn).wait_recv()
            send(b, RELAY_Y).start()
            send(b, Y_SIB).start()

    def hand_on_diagonal():
        for b in range(nb):
            copy(b, RELAY_X, block(b, 1 - mx, 1 - my, mc, 0), yn).wait_recv()
            send(b, DIAG0_SIB).start()
            copy(b, RELAY_Y, block(b, 1 - mx, 1 - my, mc, 1), xn).wait_recv()
            send(b, DIAG1_SIB).start()

    def finish():
        for b in range(nb):
            copy(b, OWN_SIB, block(b, mx, my, 1 - mc), sibling).wait_recv()
            copy(b, X_SIB, block(b, 1 - mx, my, 1 - mc), sibling).wait_recv()
            copy(b, Y_SIB, block(b, mx, 1 - my, 1 - mc), sibling).wait_recv()
            copy(b, DIAG0_SIB, block(b, 1 - mx, 1 - my, 1 - mc, 0), sibling).wait_recv()
            copy(b, DIAG1_SIB, block(b, 1 - mx, 1 - my, 1 - mc, 1), sibling).wait_recv()
        for b in range(nb):
            for k in range(GATHER_COPIES):
                send(b, k).wait_send()
            local(b).wait()

    return send_own, relay_neighbours, hand_on_diagonal, finish


def exchange_sibling(gs, name):
    nb = len(gs)

    def body(*refs):
        g_refs, recv_refs, send_sems, recv_sems = refs[:nb], refs[nb:2 * nb], refs[2 * nb], refs[2 * nb + 1]
        mx, my, mc = _place()
        copies = [pltpu.make_async_remote_copy(
            src_ref=g_refs[b].at[2 * k + 1 - mc], dst_ref=recv_refs[b].at[k], send_sem=send_sems.at[b, k],
            recv_sem=recv_sems.at[b, k], device_id=(mx, my, 1 - mc), device_id_type=MESH)
            for b in range(nb) for k in range(4)]
        for cp in copies:
            cp.start()
        for cp in copies:
            cp.wait()

    outs = pl.pallas_call(
        body, name=name, out_shape=[_S((4,) + g.shape[1:], g.dtype) for g in gs], in_specs=[ANY] * nb,
        out_specs=[ANY] * nb,
        scratch_shapes=[pltpu.SemaphoreType.DMA((nb, 4)), pltpu.SemaphoreType.DMA((nb, 4))],
    )(*gs)
    return list(outs)


def exchange_chips(parts, name):
    nb = len(parts)

    def body(*refs):
        start, finish = _chips_phases(refs[:nb], refs[nb:2 * nb], refs[2 * nb], refs[2 * nb + 1])
        start()
        finish()

    outs = pl.pallas_call(
        body, name=name, out_shape=_chips_out_shapes(parts), in_specs=[ANY] * nb, out_specs=[ANY] * nb,
        scratch_shapes=_chips_semaphores(nb),
    )(*parts)
    return list(outs)


def _chips_out_shapes(parts):
    return [_S((3,) + a.shape[1:], a.dtype) for a in parts]


def _chips_semaphores(nb):
    return [pltpu.SemaphoreType.DMA((nb, 3)), pltpu.SemaphoreType.DMA((nb, 3))]


def _chips_phases(a_refs, recv_refs, send_sems, recv_sems):
    mx, my, mc = _place()

    def copies():
        return [pltpu.make_async_remote_copy(
            src_ref=a_refs[b].at[2 * px + py], dst_ref=recv_refs[b].at[j], send_sem=send_sems.at[b, j],
            recv_sem=recv_sems.at[b, j], device_id=(px, py, mc), device_id_type=MESH)
            for b in range(len(a_refs)) for j, (px, py) in enumerate(_other_chips(mx, my))]

    def start():
        for cp in copies():
            cp.start()

    def finish():
        for cp in copies():
            cp.wait()

    return start, finish


def add_sibling(g, theirs, core, name, wire_dtype=f32):
    shp = theirs.shape
    C = shp[-1]
    R = math.prod(shp[1:-1])
    tr = _row_tile(R, C)
    narrow = wire_dtype != f32

    def body(core_ref, g_ref, t_ref, o_ref, *wire_ref):
        s = g_ref[...] + t_ref[...]
        o_ref[...] = s
        if narrow:
            wire_ref[0][...] = s.astype(wire_dtype)

    blk = lambda: pl.BlockSpec((1, tr, C), lambda k, i, c: (k, i, 0))
    grid_spec = pltpu.PrefetchScalarGridSpec(
        num_scalar_prefetch=1, grid=(4, R // tr),
        in_specs=[pl.BlockSpec((1, tr, C), lambda k, i, c: (2 * k + c[0], i, 0)), blk()],
        out_specs=[blk(), blk()] if narrow else [blk()])
    outs = pl.pallas_call(
        body, name=name, grid_spec=grid_spec,
        out_shape=[_S((4, R, C), f32)] + ([_S((4, R, C), wire_dtype)] if narrow else []),
        compiler_params=_cp("parallel", "parallel"),
    )(core, g.reshape(N_DEV, R, C), theirs.reshape(4, R, C))
    part = outs[0].reshape(shp)
    return part, (outs[1].reshape(shp) if narrow else part)


def add_chips(a, theirs, chip, name):
    _, R, C = a.shape
    tr = _row_tile(R, C)

    def body(chip_ref, a_ref, t_ref, out_ref):
        out_ref[...] = ((a_ref[0] + t_ref[0]) + t_ref[1]) + t_ref[2]

    grid_spec = pltpu.PrefetchScalarGridSpec(
        num_scalar_prefetch=1, grid=(R // tr,),
        in_specs=[pl.BlockSpec((1, tr, C), lambda i, c: (c[0], i, 0)), pl.BlockSpec((3, tr, C), lambda i, c: (0, i, 0))],
        out_specs=pl.BlockSpec((tr, C), lambda i, c: (i, 0)))
    return pl.pallas_call(
        body, name=name, grid_spec=grid_spec, out_shape=_S((R, C), a.dtype), compiler_params=_cp("parallel"),
    )(chip, a, theirs)


def _round_up(n, q):
    return (n + q - 1) // q * q


def _lane_rows(a):
    flat = a.reshape(-1)
    n = _round_up(flat.shape[0], SUBLANES * LANES)
    return jnp.pad(flat, (0, n - flat.shape[0])).reshape(-1, LANES)


def _full_to_shards(full, axis):
    shp = full.shape
    s = shp[axis] // N_DEV
    cut = full.reshape(shp[:axis] + (N_DEV, s) + shp[axis + 1:])
    return jnp.moveaxis(cut, axis, 0)


def _shards_to_full(parts, axis):
    shp = list(parts.shape[1:])
    shp[axis] *= N_DEV
    return jnp.moveaxis(parts, 0, axis).reshape(tuple(shp))


def kernel(x, p, g_pre, w_in, s5_a_re, s5_a_im, s5_log_dt, s5_b_re, s5_b_im, s5_c_re, s5_c_im, s5_d, w_glu, w_bs, conv_w, conv_b, lru_w_a, lru_b_a, lru_w_x, lru_b_x, lru_lambda, w_bl, w_out, g_post, w_ple, w_ple_gate, loss_target, m_g_pre, m_w_in, m_s5_a_re, m_s5_a_im, m_s5_log_dt, m_s5_b_re, m_s5_b_im, m_s5_c_re, m_s5_c_im, m_s5_d, m_w_glu, m_w_bs, m_conv_w, m_conv_b, m_lru_w_a, m_lru_b_a, m_lru_w_x, m_lru_b_x, m_lru_lambda, m_w_bl, m_w_out, m_g_post, m_w_ple, m_w_ple_gate, v_g_pre, v_w_in, v_s5_a_re, v_s5_a_im, v_s5_log_dt, v_s5_b_re, v_s5_b_im, v_s5_c_re, v_s5_c_im, v_s5_d, v_w_glu, v_w_bs, v_conv_w, v_conv_b, v_lru_w_a, v_lru_b_a, v_lru_w_x, v_lru_b_x, v_lru_lambda, v_w_bl, v_w_out, v_g_post, v_w_ple, v_w_ple_gate):
    given = dict(locals())
    W = {n: given[n] for n in WEIGHTS}
    M = {n: given["m_" + n] for n in WEIGHTS}
    V = {n: given["v_" + n] for n in WEIGHTS}
    xs, target = to_scan_order(x[0]), to_scan_order(loss_target[0])
    ps = [to_scan_order(p[i, 0]) for i in range(DEPTH)]

    mx, my, mc = _place()
    core = jnp.reshape(mc, (1,)).astype(jnp.int32)
    chip = jnp.reshape(2 * mx + my, (1,)).astype(jnp.int32)

    names = list(SHARDED)
    conv_rows = PAIR - CONV_WIDTH

    def layer_shards(i):
        return [W[n][i].astype(bf16) if n in GATHER_BF16 else jnp.pad(W[n][i], ((0, conv_rows), (0, 0))) for n in names]

    def layer_weights(i, gathered):
        full = {n: _shards_to_full(g if n in GATHER_BF16 else g[:, :CONV_WIDTH], SHARDED[n] - 1)
                for n, g in zip(names, gathered)}
        return {n: (full[n] if n in SHARDED else W[n][i]) for n in WEIGHTS}

    act, saved, weights = xs, [], []
    gathered = all_gather(layer_shards(0), "comm_gather_weights")
    for i in range(DEPTH):
        weights.append(layer_weights(i, gathered))
        act, sv, gathered = layer_fwd(act, ps[i], weights[i], "_l%d" % i, layer_shards(i + 1) if i + 1 < DEPTH else ())
        saved.append(sv)
    loss_part, gact = loss_head(act, target, "loss_head")
    loss = lax.psum(loss_part[0, 0], ("x", "y", "c"))

    def sibling_sums(i, g):
        rep_rows = [_lane_rows(g[n].reshape(W[n].shape[1:])) for n in REPLICATED]
        n_rows = sum(r.shape[0] for r in rep_rows)
        pad_rows = _round_up(n_rows, N_DEV * SUBLANES) - n_rows
        rep_blocks = jnp.concatenate(rep_rows + [jnp.zeros((pad_rows, LANES), f32)]).reshape(N_DEV, -1, LANES)
        blocks = [_full_to_shards(g[n].reshape(weights[i][n].shape), SHARDED[n] - 1) for n in names] + [rep_blocks]
        theirs = exchange_sibling(blocks, "comm_reduce_sibling_l%d" % i)
        parts, wire = [], []
        for k, (b, t) in enumerate(zip(blocks, theirs)):
            part, sent = add_sibling(b, t, core, "reduce_add_sibling_%d_l%d" % (k, i), bf16 if k < len(names) else f32)
            parts.append(part)
            wire.append(sent)
        return parts, wire, [r.shape[0] for r in rep_rows]

    parts, others, rep_sizes, wire = [None] * DEPTH, [None] * DEPTH, None, ()
    for i in reversed(range(DEPTH)):
        gact, g, exchanged = layer_bwd(gact, weights[i], saved[i], "_l%d" % i, wire)
        if i + 1 < DEPTH:
            others[i + 1] = exchanged
        parts[i], wire, rep_sizes = sibling_sums(i, g)
    others[0] = exchange_chips(wire, "comm_reduce_chips")

    red, deltas, new_m, new_v = {}, {}, {}, {}
    for k, n in enumerate(names):
        red[n], deltas[n], new_m[n], new_v[n] = adamw_reduce(
            W[n], [parts[i][k] for i in range(DEPTH)], [others[i][k] for i in range(DEPTH)], M[n], V[n], chip, "adamw_" + n)
    pieces = [add_chips(parts[i][-1], others[i][-1], chip, "reduce_add_chips_l%d" % i) for i in range(DEPTH)]
    rep_all = [r.reshape(-1, LANES) for r in all_gather(pieces, "comm_gather_replicated")]
    off = 0
    for n, rows in zip(REPLICATED, rep_sizes):
        k = math.prod(W[n].shape[1:])
        red[n] = jnp.stack([rep_all[i][off:off + rows].reshape(-1)[:k] for i in range(DEPTH)]).reshape(W[n].shape)
        off += rows
        deltas[n], new_m[n], new_v[n] = adamw(W[n], red[n], M[n], V[n], "adamw_" + n)
    return (loss, from_scan_order(gact)[None], *[red[n] for n in WEIGHTS], *[deltas[n] for n in WEIGHTS],
            *[new_m[n] for n in WEIGHTS], *[new_v[n] for n in WEIGHTS])
```

```python
import math

import jax
import jax.numpy as jnp
from jax import lax
from jax.experimental import pallas as pl
from jax.experimental.pallas import tpu as pltpu

f32 = jnp.float32
bf16 = jnp.bfloat16

D_MODEL = 1024
DEPTH = 2
PLE_DIM = 256
NORM_EPS = 1e-6
S5_WIDTH = 512
S5_GROUP = 16
S5_GROUPS = 32
S5_STATE = 64
S5_NS = S5_GROUPS * S5_STATE
LRU_WIDTH = 1280
LRU_HEADS = 10
LRU_HEAD_DIM = 128
LRU_C = 8.0
CONV_WIDTH = 4
N_DEV = 8

ADAM_LR = 0.001
ADAM_B1 = 0.9
ADAM_B2 = 0.999
ADAM_EPS = 1e-08
ADAM_WD = 0.01
ADAM_STEP = 10

LANES = 128
SUBLANES = 8
SEGS = SUBLANES
SCAN_CHUNK = 256
TOKEN_TILE = 256
TOKEN_TILE_LIGHT = 512
MM_TILE_M = 1024
PAIR = 2 * SUBLANES
VMEM_LIMIT_BYTES = 56 * 1024 * 1024
ELEMENTWISE_BLOCK_BYTES = 1024 * 1024

WEIGHTS = ['g_pre', 'w_in', 's5_a_re', 's5_a_im', 's5_log_dt', 's5_b_re', 's5_b_im', 's5_c_re', 's5_c_im',
           's5_d', 'w_glu', 'w_bs', 'conv_w', 'conv_b', 'lru_w_a', 'lru_b_a', 'lru_w_x', 'lru_b_x',
           'lru_lambda', 'w_bl', 'w_out', 'g_post', 'w_ple', 'w_ple_gate']
SHARDED = {'w_in': 2, 'w_glu': 2, 'w_bs': 2, 'conv_w': 2, 'w_bl': 1, 'w_out': 1, 'w_ple': 2, 'w_ple_gate': 1}
GATHER_BF16 = ['w_in', 'w_glu', 'w_bs', 'w_bl', 'w_out', 'w_ple', 'w_ple_gate']
REPLICATED = [n for n in WEIGHTS if n not in SHARDED]


def _sig(x):
    return 0.5 * jnp.tanh(0.5 * x) + 0.5


def _gelu_parts(x):
    k = math.sqrt(2.0 / math.pi)
    t = jnp.tanh(k * (x + 0.044715 * x * x * x))
    return t, k


def _gelu(x):
    t, _ = _gelu_parts(x)
    return 0.5 * x * (1.0 + t)


def _gelu_grad(x):
    t, k = _gelu_parts(x)
    return 0.5 * (1.0 + t) + 0.5 * x * (1.0 - t * t) * k * (1.0 + 3.0 * 0.044715 * x * x)


def _one_minus_sq(a, log_a):
    z = 2.0 * log_a
    series = -z * (1.0 + z * (0.5 + z * (1.0 / 6.0 + z * (1.0 / 24.0 + z * (1.0 / 120.0)))))
    return jnp.where(z > -0.05, series, 1.0 - a * a)


def _softplus_neg(lam):
    return jnp.maximum(-lam, 0.0) + jnp.log(1.0 + jnp.exp(-jnp.abs(lam)))


def _dot(a, b):
    return jnp.dot(a, b, preferred_element_type=f32)


def _dot_nt(a, b):
    return lax.dot_general(a, b, (((1,), (1,)), ((), ())), preferred_element_type=f32)


def _dot_tn(a, b):
    return lax.dot_general(a, b, (((0,), (0,)), ((), ())), preferred_element_type=f32)


def _S(shape, dtype=f32):
    return jax.ShapeDtypeStruct(shape, dtype)


def _full(shape):
    nd = len(shape)
    return pl.BlockSpec(shape, lambda *_: (0,) * nd)


def _rows(tile, width, col=0):
    return pl.BlockSpec((tile, width), lambda i: (i, col))


def _cp(*semantics):
    return pltpu.CompilerParams(dimension_semantics=semantics or None, vmem_limit_bytes=VMEM_LIMIT_BYTES)


def _tile(n, want):
    t = min(n, want)
    assert n % t == 0, (n, want)
    return t


def _row_tile(R, C=LANES):
    cap = max(SUBLANES, min(R, ELEMENTWISE_BLOCK_BYTES // (4 * C)))
    for t in range(cap - cap % SUBLANES, 0, -SUBLANES):
        if R % t == 0:
            return t
    return R


def _lanes(j):
    return slice(LANES * j, LANES * (j + 1))


def _step_rows(k, n=SUBLANES):
    return pl.ds(pl.multiple_of(k * n, n), n)


def to_scan_order(a):
    T, C = a.shape
    tc = _tile(T, SCAN_CHUNK)
    return a.reshape(T // tc, SEGS, tc // SEGS, C).transpose(0, 2, 1, 3).reshape(T, C)


def from_scan_order(a):
    T, C = a.shape
    tc = _tile(T, SCAN_CHUNK)
    return a.reshape(T // tc, tc // SEGS, SEGS, C).transpose(0, 2, 1, 3).reshape(T, C)


def _resident(shape):
    nd = len(shape)
    return pl.BlockSpec(shape, lambda *_: (0,) * nd, pipeline_mode=pl.Buffered(1))


def mm_tn(a, bs, name):
    M, K = a.shape
    tm = _tile(M, MM_TILE_M)
    nb = len(bs)

    def body(a_ref, *refs):
        m = pl.program_id(0)
        av = a_ref[...]
        for b_ref, o_ref in zip(refs[:nb], refs[nb:]):
            _accumulate(o_ref, _dot_tn(av, b_ref[...]), m)

    outs = pl.pallas_call(
        body, name=name, grid=(M // tm,),
        in_specs=[_rows(tm, K)] + [_rows(tm, b.shape[1]) for b in bs],
        out_specs=[_full((K, b.shape[1])) for b in bs],
        out_shape=[_S((K, b.shape[1])) for b in bs],
        compiler_params=_cp("arbitrary"),
    )(a, *bs)
    return list(outs)


IN_PROJ_WIDTHS = (2 * S5_WIDTH, LRU_WIDTH, LRU_WIDTH, 2 * D_MODEL)


def in_proj_fwd(x, g, win, name):
    T = x.shape[0]
    tm = _tile(T, MM_TILE_M // 2)
    offs = [sum(IN_PROJ_WIDTHS[:k]) for k in range(len(IN_PROJ_WIDTHS))]

    def body(x_ref, g_ref, w_ref, h_ref, *out_refs):
        xv = x_ref[...]
        r = lax.rsqrt(jnp.mean(xv * xv, axis=-1, keepdims=True) + NORM_EPS)
        h = (xv * r * g_ref[...]).astype(bf16)
        h_ref[...] = h
        for o_ref, off, wd in zip(out_refs, offs, IN_PROJ_WIDTHS):
            o_ref[...] = _dot(h, w_ref[:, off:off + wd])

    return pl.pallas_call(
        body, name=name, grid=(T // tm,),
        in_specs=[_rows(tm, D_MODEL), _full((1, D_MODEL)), _resident(win.shape)],
        out_specs=[_rows(tm, D_MODEL)] + [_rows(tm, wd) for wd in IN_PROJ_WIDTHS],
        out_shape=[_S((T, D_MODEL), bf16)] + [_S((T, wd)) for wd in IN_PROJ_WIDTHS],
        compiler_params=_cp("parallel"),
    )(x, g, win)


def _s5_discretise(a_re, a_im, log_dt, b_re_t, b_im_t):
    dt = jnp.exp(log_dt)
    mag = jnp.exp(a_re * dt)
    ab_re = mag * jnp.cos(a_im * dt)
    ab_im = mag * jnp.sin(a_im * dt)
    den = a_re * a_re + a_im * a_im
    nr, ni = ab_re - 1.0, ab_im
    z_re = (nr * a_re + ni * a_im) / den
    z_im = (ni * a_re - nr * a_im) / den
    bb_re = z_re[None] * b_re_t - z_im[None] * b_im_t
    bb_im = z_re[None] * b_im_t + z_im[None] * b_re_t
    return ab_re, ab_im, bb_re, bb_im


def s5_prep(a_re, a_im, log_dt, b_re_t, b_im_t, m, name):
    G, N = a_re.shape

    def body(are_ref, aim_ref, ldt_ref, bre_ref, bim_ref, ab_ref, pw_ref, bb_ref):
        are, aim, ldt = are_ref[...], aim_ref[...], ldt_ref[...]
        ab_re, ab_im, bb_re, bb_im = _s5_discretise(are, aim, ldt, bre_ref[...], bim_ref[...])
        ab_ref[0], ab_ref[1] = ab_re, ab_im
        bb_ref[0], bb_ref[1] = bb_re, bb_im
        dt = jnp.exp(ldt)
        for k in range(m):
            mag = jnp.exp(are * dt * (k + 1.0))
            pw_ref[0, k] = mag * jnp.cos(aim * dt * (k + 1.0))
            pw_ref[1, k] = mag * jnp.sin(aim * dt * (k + 1.0))

    return pl.pallas_call(
        body, name=name,
        out_shape=[_S((2, G, N)), _S((2, m, G, N)), _S((2, S5_GROUP, G, N))], compiler_params=_cp(),
    )(a_re, a_im, log_dt, b_re_t, b_im_t)


def s5_prep_bwd(a_re, a_im, log_dt, b_re_t, b_im_t, g_ab, g_bb, name):
    G, N = a_re.shape

    def body(are_ref, aim_ref, ldt_ref, bre_ref, bim_ref, gab_ref, gbb_ref, o_are, o_aim, o_ldt, o_bre, o_bim):
        _, vjp = jax.vjp(_s5_discretise, are_ref[...], aim_ref[...], ldt_ref[...], bre_ref[...], bim_ref[...])
        g_are, g_aim, g_ldt, g_bre, g_bim = vjp((gab_ref[0], gab_ref[1], gbb_ref[0], gbb_ref[1]))
        o_are[...], o_aim[...], o_ldt[...], o_bre[...], o_bim[...] = g_are, g_aim, g_ldt, g_bre, g_bim

    return pl.pallas_call(
        body, name=name,
        out_shape=[_S((G, N)), _S((G, N)), _S((G, 1)), _S((S5_GROUP, G, N)), _S((S5_GROUP, G, N))],
        compiler_params=_cp(),
    )(a_re, a_im, log_dt, b_re_t, b_im_t, g_ab, g_bb)


NB_S5 = S5_NS // LANES
CB_S5 = S5_WIDTH // LANES
SB_PER_CB = NB_S5 // CB_S5
GRP_PER_SB = LANES // S5_STATE
S5_JB = 8


def _bdb_mask():
    j = jnp.arange(NB_S5)
    own_rows = (j[:, None] % SB_PER_CB == jnp.arange(SB_PER_CB)[None, :]).astype(f32)
    eye = jnp.eye(GRP_PER_SB, dtype=f32)
    return own_rows[:, :, None, None, None, None, None] * eye[None, None, :, None, None, :, None]


def _pack_bdb(bb):
    v = jnp.transpose(bb.reshape(2, S5_GROUP, NB_S5, GRP_PER_SB, S5_STATE), (2, 3, 1, 0, 4))
    full = v[:, None, :, :, :, None, :] * _bdb_mask()
    return full.reshape(NB_S5, LANES, 2 * LANES)


def _unpack_bdb(g_bdb):
    g7 = g_bdb.reshape(NB_S5, SB_PER_CB, GRP_PER_SB, S5_GROUP, 2, GRP_PER_SB, S5_STATE)
    v = jnp.sum(g7 * _bdb_mask(), axis=(1, 5))
    return jnp.transpose(v, (3, 2, 0, 1, 4)).reshape(2, S5_GROUP, S5_GROUPS, S5_STATE)


def _pack_cdb(c_re, c_im):
    gl = S5_GROUPS // CB_S5
    c2 = jnp.stack([c_re, -c_im]).reshape(2, CB_S5, gl, S5_GROUP, S5_STATE)
    eye = jnp.eye(gl, dtype=f32)
    full = jnp.transpose(c2, (1, 0, 2, 4, 3))[:, :, :, :, None, :] * eye[None, None, :, None, :, None]
    return full.reshape(CB_S5, 2 * SB_PER_CB * LANES, LANES)


def _unpack_cdb(g_cdb):
    gl = S5_GROUPS // CB_S5
    g6 = g_cdb.reshape(CB_S5, 2, gl, S5_STATE, gl, S5_GROUP)
    eye = jnp.eye(gl, dtype=f32)
    v = jnp.sum(g6 * eye[None, None, :, None, :, None], axis=4)
    v = jnp.transpose(v, (1, 0, 2, 4, 3)).reshape(2, S5_GROUPS, S5_GROUP, S5_STATE)
    return v[0], -v[1]


def _state_cat(ref, c):
    w = SB_PER_CB * LANES
    return jnp.concatenate([ref[:, w * c:w * (c + 1)], ref[:, S5_NS + w * c:S5_NS + w * (c + 1)]], axis=1)


def _state_pair(ref, j):
    return jnp.concatenate([ref[:, _lanes(j)], ref[:, S5_NS + LANES * j:S5_NS + LANES * (j + 1)]], axis=1)


def s5_fwd(usg, bdb, cdb, dvec, abar_b, ptab_b, name, gather=()):
    T = usg.shape[0]
    tc = _tile(T, SCAN_CHUNK)
    m = tc // SEGS
    nsteps = T // tc
    ng = len(gather)
    assert ptab_b.shape == (2, m, SEGS, S5_NS) and m % 2 == 0

    def body(*refs):
        u_ref, bdb_ref, cdb_ref, d_ref, a_ref, p_ref = refs[:6]
        ys_ref, sre_ref, sim_ref, sbf_ref = refs[6 + ng:10 + ng]
        src_re, src_im, dst_re, dst_im, cin_ref, carry_ref = refs[10 + 2 * ng:16 + 2 * ng]
        i = pl.program_id(0)
        if ng:
            phases = _gather_phases([s.shape for s in gather], refs[6:6 + ng], refs[10 + ng:10 + 2 * ng],
                                    *refs[16 + 2 * ng:])
            for phase, step in zip(phases, (0, nsteps // 2, (3 * nsteps) // 4, nsteps - 1)):
                pl.when(i == step)(phase)

        @pl.when(i == 0)
        def _():
            carry_ref[...] = jnp.zeros_like(carry_ref)

        u = u_ref[...]
        ub = u.astype(bf16)
        for j in range(NB_S5):
            bu = _dot(ub[:, _lanes(j // SB_PER_CB)], bdb_ref[j])
            src_re[:, _lanes(j)] = bu[:, :LANES]
            src_im[:, _lanes(j)] = bu[:, LANES:]
        for j0 in range(0, NB_S5, S5_JB):
            def kstep(k, st):
                rows = _step_rows(k)
                out = []
                for q in range(S5_JB):
                    ln = _lanes(j0 + q)
                    sr, si = st[2 * q], st[2 * q + 1]
                    ar, ai = a_ref[0, :, ln], a_ref[1, :, ln]
                    nr = ar * sr - ai * si + src_re[rows, ln]
                    ni = ar * si + ai * sr + src_im[rows, ln]
                    dst_re[rows, ln] = nr
                    dst_im[rows, ln] = ni
                    out += [nr, ni]
                return tuple(out)

            ends = lax.fori_loop(0, m, kstep, tuple(jnp.zeros((SEGS, LANES), f32) for _ in range(2 * S5_JB)))
            for q in range(S5_JB):
                ln = _lanes(j0 + q)
                er, ei = ends[2 * q], ends[2 * q + 1]
                cr, ci = carry_ref[0, :, ln], carry_ref[1, :, ln]
                amr, ami = p_ref[0, m - 1, 0:1, ln], p_ref[1, m - 1, 0:1, ln]
                rows_r, rows_i = [], []
                for s in range(SEGS):
                    rows_r.append(cr)
                    rows_i.append(ci)
                    cr, ci = (er[s:s + 1, :] + amr * cr - ami * ci, ei[s:s + 1, :] + amr * ci + ami * cr)
                cin_ref[0, 0:SEGS, ln] = _stack_rows(rows_r)
                cin_ref[1, 0:SEGS, ln] = _stack_rows(rows_i)
                carry_ref[0, :, ln] = cr
                carry_ref[1, :, ln] = ci
        cin_ref[:, SEGS:, :] = cin_ref[:, 0:SEGS, :]

        def fix(k2, _):
            rows = _step_rows(k2, PAIR)
            pr = p_ref[0, pl.ds(2 * k2, 2)].reshape(PAIR, S5_NS)
            pi = p_ref[1, pl.ds(2 * k2, 2)].reshape(PAIR, S5_NS)
            cr, ci = cin_ref[0], cin_ref[1]
            sr = dst_re[rows, :] + pr * cr - pi * ci
            si = dst_im[rows, :] + pr * ci + pi * cr
            sre_ref[rows, :] = sr
            sim_ref[rows, :] = si
            sbf_ref[rows, 0:S5_NS] = sr.astype(bf16)
            sbf_ref[rows, S5_NS:] = si.astype(bf16)
            return 0

        lax.fori_loop(0, m // 2, fix, 0)
        for c in range(CB_S5):
            ys_ref[:, _lanes(c)] = _dot(_state_cat(sbf_ref, c), cdb_ref[c]) + d_ref[:, _lanes(c)] * u[:, _lanes(c)]

    st = lambda w: _rows(tc, w)
    outs = pl.pallas_call(
        body, name=name, grid=(nsteps,),
        in_specs=[_rows(tc, S5_WIDTH, 0), _resident(bdb.shape), _resident(cdb.shape), _full((1, S5_WIDTH)),
                  _resident((2, SEGS, S5_NS)), _resident((2, m, SEGS, S5_NS))] + [ANY] * ng,
        out_specs=[st(S5_WIDTH), st(S5_NS), st(S5_NS), st(2 * S5_NS)] + [ANY] * ng,
        out_shape=[_S((T, S5_WIDTH)), _S((T, S5_NS)), _S((T, S5_NS)), _S((T, 2 * S5_NS), bf16)] + (
            _gather_out_shapes(gather) if ng else []),
        scratch_shapes=[pltpu.VMEM((tc, S5_NS), f32)] * 4 + [pltpu.VMEM((2, PAIR, S5_NS), f32),
                                                             pltpu.VMEM((2, 1, S5_NS), f32)] + (
            _gather_semaphores(ng) if ng else []),
        compiler_params=_cp("arbitrary"),
    )(usg, bdb, cdb, dvec, abar_b, ptab_b, *gather)
    return outs[:4], list(outs[4:])


def s5_bwd(gys, usg, s_re, s_im, s_bf, bdb, cdb, dvec, abar_b, ptab_rev_b, name):
    T = gys.shape[0]
    tc = _tile(T, SCAN_CHUNK)
    m = tc // SEGS
    nch = T // tc
    hb = tc // SUBLANES

    def body(gy_ref, u_ref, sre_ref, sim_ref, hre_ref, him_ref, sbf_ref, bdb_ref, cdb_ref, d_ref, a_ref, p_ref,
             gu_ref, gab_ref, gd_ref, gbdb_ref, gcdb_ref,
             src_re, src_im, dst_re, dst_im, lam_ref, cin_ref, acc_ref, carry_ref):
        i = pl.program_id(0)

        @pl.when(i == 0)
        def _():
            carry_ref[...] = jnp.zeros_like(carry_ref)
            for ref in (gab_ref, gd_ref, gbdb_ref, gcdb_ref):
                ref[...] = jnp.zeros_like(ref)

        first = i == nch - 1
        gy = gy_ref[...]
        gyb = gy.astype(bf16)
        u = u_ref[...]
        ub = u.astype(bf16)
        w = SB_PER_CB * LANES
        for c in range(CB_S5):
            gs = _dot_nt(gyb[:, _lanes(c)], cdb_ref[c])
            src_re[:, w * c:w * (c + 1)] = gs[:, :w]
            src_im[:, w * c:w * (c + 1)] = gs[:, w:]
            gcdb_ref[c] += _dot_tn(_state_cat(sbf_ref, c), gyb[:, _lanes(c)])
        for j0 in range(0, NB_S5, S5_JB):
            def kstep(kk, st):
                rows = _step_rows(m - 1 - kk)
                out = []
                for q in range(S5_JB):
                    ln = _lanes(j0 + q)
                    lr, li = st[2 * q], st[2 * q + 1]
                    ar, ai = a_ref[0, :, ln], a_ref[1, :, ln]
                    nr = ar * lr + ai * li + src_re[rows, ln]
                    ni = ar * li - ai * lr + src_im[rows, ln]
                    dst_re[rows, ln] = nr
                    dst_im[rows, ln] = ni
                    out += [nr, ni]
                return tuple(out)

            ends = lax.fori_loop(0, m, kstep, tuple(jnp.zeros((SEGS, LANES), f32) for _ in range(2 * S5_JB)))
            for q in range(S5_JB):
                ln = _lanes(j0 + q)
                er, ei = ends[2 * q], ends[2 * q + 1]
                cr, ci = carry_ref[0, :, ln], carry_ref[1, :, ln]
                amr, ami = p_ref[0, 0, 0:1, ln], p_ref[1, 0, 0:1, ln]
                rows_r, rows_i = [None] * SEGS, [None] * SEGS
                for s in reversed(range(SEGS)):
                    rows_r[s], rows_i[s] = cr, ci
                    cr, ci = (er[s:s + 1, :] + amr * cr + ami * ci, ei[s:s + 1, :] + amr * ci - ami * cr)
                cin_ref[0, 0:SEGS, ln] = _stack_rows(rows_r)
                cin_ref[1, 0:SEGS, ln] = _stack_rows(rows_i)
                carry_ref[0, :, ln] = cr
                carry_ref[1, :, ln] = ci
        cin_ref[:, SEGS:, :] = cin_ref[:, 0:SEGS, :]
        acc_ref[...] = jnp.zeros_like(acc_ref)

        def fix_rows(rows, k2, prev_re, prev_im):
            pr = p_ref[0, pl.ds(2 * k2, 2)].reshape(PAIR, S5_NS)
            pi = p_ref[1, pl.ds(2 * k2, 2)].reshape(PAIR, S5_NS)
            cr, ci = cin_ref[0], cin_ref[1]
            lr = dst_re[rows, :] + pr * cr + pi * ci
            li = dst_im[rows, :] + pr * ci - pi * cr
            lam_ref[rows, 0:S5_NS] = lr.astype(bf16)
            lam_ref[rows, S5_NS:] = li.astype(bf16)
            acc_ref[0] += lr * prev_re + li * prev_im
            acc_ref[1] += li * prev_re - lr * prev_im

        last = slice(tc - SUBLANES, tc)
        wrap_re = _down_a_segment(sre_ref[last, :], jnp.where(first, 0.0, hre_ref[SUBLANES - 1:SUBLANES, :]))
        wrap_im = _down_a_segment(sim_ref[last, :], jnp.where(first, 0.0, him_ref[SUBLANES - 1:SUBLANES, :]))
        fix_rows(pl.ds(0, PAIR), 0, jnp.concatenate([wrap_re, sre_ref[0:SUBLANES, :]], axis=0),
                 jnp.concatenate([wrap_im, sim_ref[0:SUBLANES, :]], axis=0))

        def fix(k2, _):
            prev = pl.ds(pl.multiple_of(k2 * PAIR - SUBLANES, SUBLANES), PAIR)
            fix_rows(_step_rows(k2, PAIR), k2, sre_ref[prev, :], sim_ref[prev, :])
            return 0

        lax.fori_loop(1, m // 2, fix, 0)
        gab_ref[0] += jnp.sum(acc_ref[0], axis=0, keepdims=True)
        gab_ref[1] += jnp.sum(acc_ref[1], axis=0, keepdims=True)
        for c in range(CB_S5):
            x = gy[:, _lanes(c)] * d_ref[:, _lanes(c)]
            for j in range(SB_PER_CB * c, SB_PER_CB * (c + 1)):
                pair = _state_pair(lam_ref, j)
                x = x + _dot_nt(pair, bdb_ref[j])
                gbdb_ref[j] += _dot_tn(ub[:, _lanes(c)], pair)
            gu_ref[:, _lanes(c)] = x.astype(bf16)
        gd_ref[...] += jnp.sum(gy * u, axis=0, keepdims=True)

    rev = lambda i: (nch - 1 - i, 0)
    halo = lambda i: (jnp.maximum((nch - 1 - i) * hb - 1, 0), 0)
    blk = lambda wd: pl.BlockSpec((tc, wd), rev)
    return pl.pallas_call(
        body, name=name, grid=(nch,),
        in_specs=[blk(S5_WIDTH), blk(S5_WIDTH), blk(S5_NS), blk(S5_NS),
                  pl.BlockSpec((SUBLANES, S5_NS), halo), pl.BlockSpec((SUBLANES, S5_NS), halo), blk(2 * S5_NS),
                  _resident(bdb.shape), _resident(cdb.shape), _full((1, S5_WIDTH)),
                  _resident((2, SEGS, S5_NS)), _resident((2, m, SEGS, S5_NS))],
        out_specs=[blk(S5_WIDTH), _full((2, 1, S5_NS)), _full((1, S5_WIDTH)), _full(bdb.shape), _full(cdb.shape)],
        out_shape=[_S((T, S5_WIDTH), bf16), _S((2, 1, S5_NS)), _S((1, S5_WIDTH)), _S(bdb.shape), _S(cdb.shape)],
        scratch_shapes=[pltpu.VMEM((tc, S5_NS), f32)] * 4 + [
            pltpu.VMEM((tc, 2 * S5_NS), bf16), pltpu.VMEM((2, PAIR, S5_NS), f32), pltpu.VMEM((2, PAIR, S5_NS), f32),
            pltpu.VMEM((2, 1, S5_NS), f32)],
        compiler_params=_cp("arbitrary"),
    )(gys, usg, s_re, s_im, s_re, s_im, s_bf, bdb, cdb, dvec, abar_b, ptab_rev_b)


def s5_post_fwd(ys, usg, wglu, wbs, name):
    T = ys.shape[0]
    tm = _tile(T, TOKEN_TILE_LIGHT)

    def body(ys_ref, sg_ref, wglu_ref, wbs_ref, glu_ref, zs_ref):
        glu = _dot(_gelu(ys_ref[...]).astype(bf16), wglu_ref[...])
        sg = sg_ref[...]
        y2 = glu[:, :S5_WIDTH] * _sig(glu[:, S5_WIDTH:]) * (sg * _sig(sg))
        glu_ref[...] = glu
        zs_ref[...] = _dot(y2.astype(bf16), wbs_ref[...])

    return pl.pallas_call(
        body, name=name, grid=(T // tm,),
        in_specs=[_rows(tm, S5_WIDTH), _rows(tm, S5_WIDTH, 1), _resident((S5_WIDTH, 2 * S5_WIDTH)),
                  _resident((S5_WIDTH, D_MODEL))],
        out_specs=[_rows(tm, 2 * S5_WIDTH), _rows(tm, D_MODEL)],
        out_shape=[_S((T, 2 * S5_WIDTH)), _S((T, D_MODEL))], compiler_params=_cp("parallel"),
    )(ys, usg, wglu, wbs)


def _accumulate(ref, part, step):
    @pl.when(step == 0)
    def _():
        ref[...] = part

    @pl.when(step > 0)
    def _():
        ref[...] += part


def s5_post_bwd(gzs, glu, usg, ys, wbs, wglu, name):
    T = ys.shape[0]
    tm = _tile(T, TOKEN_TILE_LIGHT)

    def body(gzs_ref, glu_ref, sg_ref, ys_ref, wbs_ref, wglu_ref, gys_ref, gsg_ref, gwbs_ref, gwglu_ref):
        i = pl.program_id(0)
        glu = glu_ref[...]
        a, b = glu[:, :S5_WIDTH], glu[:, S5_WIDTH:]
        sg = sg_ref[...]
        ys = ys_ref[...]
        sb, ssg = _sig(b), _sig(sg)
        silu = sg * ssg
        _accumulate(gwbs_ref, _dot_tn((a * sb * silu).astype(bf16), gzs_ref[...]), i)
        gy2 = _dot_nt(gzs_ref[...], wbs_ref[...])
        g_a = gy2 * sb * silu
        g_b = gy2 * a * sb * (1.0 - sb) * silu
        gsg_ref[...] = (gy2 * a * sb * ssg * (1.0 + sg * (1.0 - ssg))).astype(bf16)
        gglu = jnp.concatenate([g_a, g_b], axis=1).astype(bf16)
        _accumulate(gwglu_ref, _dot_tn(_gelu(ys).astype(bf16), gglu), i)
        gys_ref[...] = _dot_nt(gglu, wglu_ref[...]) * _gelu_grad(ys)

    return pl.pallas_call(
        body, name=name, grid=(T // tm,),
        in_specs=[_rows(tm, D_MODEL), _rows(tm, 2 * S5_WIDTH), _rows(tm, S5_WIDTH, 1), _rows(tm, S5_WIDTH),
                  _resident((S5_WIDTH, D_MODEL)), _resident((S5_WIDTH, 2 * S5_WIDTH))],
        out_specs=[_rows(tm, S5_WIDTH), _rows(tm, S5_WIDTH), _full((S5_WIDTH, D_MODEL)), _full((S5_WIDTH, 2 * S5_WIDTH))],
        out_shape=[_S((T, S5_WIDTH)), _S((T, S5_WIDTH), bf16), _S((S5_WIDTH, D_MODEL)), _S((S5_WIDTH, 2 * S5_WIDTH))],
        compiler_params=_cp("arbitrary"),
    )(gzs, glu, usg, ys, wbs, wglu)


NB_LRU = LRU_WIDTH // LANES
LRU_JB = 5
TAPS_BACK = CONV_WIDTH - 1
EDGE = TAPS_BACK * SUBLANES
HALO_ROWS = 4 * SUBLANES


def _down_a_segment(blk, entering_row):
    sub = lax.broadcasted_iota(jnp.int32, blk.shape, 0)
    return jnp.where(sub == 0, entering_row, pltpu.roll(blk, 1, 0))


def _up_a_segment(blk, entering_row):
    sub = lax.broadcasted_iota(jnp.int32, blk.shape, 0)
    return jnp.where(sub == SUBLANES - 1, entering_row, pltpu.roll(blk, SUBLANES - 1, 0))


def _stack_rows(rows):
    sub = lax.broadcasted_iota(jnp.int32, (SUBLANES,) + rows[0].shape[1:], 0)
    out = jnp.broadcast_to(rows[0], sub.shape)
    for s in range(1, SUBLANES):
        out = jnp.where(sub == s, rows[s], out)
    return out


def _fill_conv_window(xe, x_ref, xh_ref, is_first, tc):
    xe[EDGE:, :] = x_ref[...]
    for i in range(1, TAPS_BACK + 1):
        row = HALO_ROWS - SUBLANES * i + SUBLANES - 1
        entering = jnp.where(is_first, 0.0, xh_ref[row:row + 1, :])
        blk = x_ref[tc - SUBLANES * i:tc - SUBLANES * (i - 1), :]
        xe[EDGE - SUBLANES * i:EDGE - SUBLANES * (i - 1), :] = _down_a_segment(blk, entering)


def lru_fwd(lx, convw, convb, wa, wx, ba, bx, lam, name):
    T = lx.shape[0]
    tc = _tile(T, SCAN_CHUNK)
    m = tc // SEGS
    hb = tc // HALO_ROWS

    def body(x_ref, xh_ref, cw_ref, cb_ref, wa_ref, wx_ref, ba_ref, bx_ref, lam_ref,
             c_ref, r_ref, i_ref, om_ref, h_ref, xe, src_a, src_b, dst_a, dst_h, cin_ref, carry_ref):
        i = pl.program_id(0)

        @pl.when(i == 0)
        def _():
            carry_ref[...] = jnp.zeros_like(carry_ref)

        _fill_conv_window(xe, x_ref, xh_ref, i == 0, tc)
        c = cb_ref[...] + cw_ref[0:1, :] * xe[0:tc, :]
        for k in range(1, CONV_WIDTH):
            c = c + cw_ref[k:k + 1, :] * xe[SUBLANES * k:SUBLANES * k + tc, :]
        c_ref[...] = c
        sp = _softplus_neg(lam_ref[...])
        for j in range(NB_LRU):
            ln = _lanes(j)
            cj = c[:, ln]
            cjb = cj.astype(bf16)
            r = _sig(_dot(cjb, wa_ref[j]) + ba_ref[:, ln])
            g = _sig(_dot(cjb, wx_ref[j]) + bx_ref[:, ln])
            r_ref[:, ln] = r
            i_ref[:, ln] = g
            log_a = -LRU_C * r * sp[:, ln]
            a = jnp.exp(log_a)
            src_a[:, ln] = a
            om = _one_minus_sq(a, log_a)
            om_ref[:, ln] = om
            src_b[:, ln] = jnp.sqrt(om) * (g * cj)
        for j0 in range(0, NB_LRU, LRU_JB):
            def kstep(k, st):
                rows = _step_rows(k)
                out = []
                for q in range(LRU_JB):
                    ln = _lanes(j0 + q)
                    hh, ac = st[2 * q], st[2 * q + 1]
                    a = src_a[rows, ln]
                    hh = a * hh + src_b[rows, ln]
                    ac = a * ac
                    dst_h[rows, ln] = hh
                    dst_a[rows, ln] = ac
                    out += [hh, ac]
                return tuple(out)

            init = tuple(jnp.zeros((SEGS, LANES), f32) if q % 2 == 0 else jnp.ones((SEGS, LANES), f32)
                         for q in range(2 * LRU_JB))
            ends = lax.fori_loop(0, m, kstep, init)
            for q in range(LRU_JB):
                ln = _lanes(j0 + q)
                eh, ea = ends[2 * q], ends[2 * q + 1]
                cr = carry_ref[:, ln]
                rows_c = []
                for s in range(SEGS):
                    rows_c.append(cr)
                    cr = eh[s:s + 1, :] + ea[s:s + 1, :] * cr
                cin_ref[:, ln] = _stack_rows(rows_c)
                carry_ref[:, ln] = cr

        def fix(k, _):
            rows = _step_rows(k)
            h_ref[rows, :] = dst_h[rows, :] + dst_a[rows, :] * cin_ref[...]
            return 0

        lax.fori_loop(0, m, fix, 0)

    wide = lambda: _rows(tc, LRU_WIDTH)
    buf = lambda rows: pltpu.VMEM((rows, LRU_WIDTH), f32)
    return pl.pallas_call(
        body, name=name, grid=(T // tc,),
        in_specs=[wide(), pl.BlockSpec((HALO_ROWS, LRU_WIDTH), lambda i: (jnp.maximum(i * hb - 1, 0), 0)),
                  _full((CONV_WIDTH, LRU_WIDTH)), _full((1, LRU_WIDTH)),
                  _full((LRU_HEADS, LRU_HEAD_DIM, LRU_HEAD_DIM)), _full((LRU_HEADS, LRU_HEAD_DIM, LRU_HEAD_DIM)),
                  _full((1, LRU_WIDTH)), _full((1, LRU_WIDTH)), _full((1, LRU_WIDTH))],
        out_specs=[wide(), wide(), wide(), wide(), wide()],
        out_shape=[_S((T, LRU_WIDTH))] * 5,
        scratch_shapes=[buf(tc + EDGE), buf(tc), buf(tc), buf(tc), buf(tc), buf(SEGS), buf(1)],
        compiler_params=_cp("arbitrary"),
    )(lx, lx, convw, convb, wa, wx, ba, bx, lam)


def lru_bwd(gh, h, c, r, gi, om, lx, convw, wa, wx, lam, name, exchange=()):
    T = gh.shape[0]
    tc = _tile(T, SCAN_CHUNK)
    m = tc // SEGS
    nch = T // tc
    ne = len(exchange)
    NI = 13

    def body(*refs):
        (gh_ref, h_ref, hh_ref, c_ref, r_ref, i_ref, om_ref, x_ref, xh_ref, cw_ref, wa_ref, wx_ref, lam_ref) = refs[:NI]
        glx_ref, gwa_ref, gwx_ref, gba_ref, gbx_ref, glam_ref, gcb_ref, gcw_ref = refs[NI + ne:NI + 8 + ne]
        (src_a, src_m, dst_a, dst_m, mbuf, hbuf, xe, gce, cin_ref, gcc_ref, carry_ref) = refs[NI + 8 + 2 * ne:NI + 19 + 2 * ne]
        i = pl.program_id(0)
        if ne:
            start, finish = _chips_phases(refs[NI:NI + ne], refs[NI + 8 + ne:NI + 8 + 2 * ne], *refs[NI + 19 + 2 * ne:])
            pl.when(i == 0)(start)
            pl.when(i == nch - 1)(finish)

        @pl.when(i == 0)
        def _():
            carry_ref[...] = jnp.zeros_like(carry_ref)
            gcc_ref[...] = jnp.zeros_like(gcc_ref)
            for ref in (gwa_ref, gwx_ref, gba_ref, gbx_ref, glam_ref, gcb_ref, gcw_ref):
                ref[...] = jnp.zeros_like(ref)

        first = i == nch - 1
        last = slice(tc - SUBLANES, tc)
        hbuf[SUBLANES:, :] = h_ref[...]
        hbuf[0:SUBLANES, :] = _down_a_segment(h_ref[last, :], jnp.where(first, 0.0, hh_ref[SUBLANES - 1:SUBLANES, :]))
        _fill_conv_window(xe, x_ref, xh_ref, first, tc)
        lam_v = lam_ref[...]
        sp = _softplus_neg(lam_v)
        a_all = jnp.exp(-LRU_C * r_ref[...] * sp)
        src_a[...] = a_all
        src_m[...] = a_all * gh_ref[...]
        for j0 in range(0, NB_LRU, LRU_JB):
            def kstep(kk, st):
                rows = _step_rows(m - 1 - kk)
                out = []
                for q in range(LRU_JB):
                    ln = _lanes(j0 + q)
                    mu, ac = st[2 * q], st[2 * q + 1]
                    a = src_a[rows, ln]
                    mu = a * mu + src_m[rows, ln]
                    ac = a * ac
                    dst_m[rows, ln] = mu
                    dst_a[rows, ln] = ac
                    out += [mu, ac]
                return tuple(out)

            init = tuple(jnp.zeros((SEGS, LANES), f32) if q % 2 == 0 else jnp.ones((SEGS, LANES), f32)
                         for q in range(2 * LRU_JB))
            ends = lax.fori_loop(0, m, kstep, init)
            for q in range(LRU_JB):
                ln = _lanes(j0 + q)
                em, ea = ends[2 * q], ends[2 * q + 1]
                cr = carry_ref[:, ln]
                rows_c = [None] * SEGS
                for s in reversed(range(SEGS)):
                    rows_c[s] = cr
                    cr = em[s:s + 1, :] + ea[s:s + 1, :] * cr
                cin_ref[:, ln] = _stack_rows(rows_c)
                carry_ref[:, ln] = cr

        def fix(k, _):
            rows = _step_rows(k)
            mbuf[rows, :] = dst_m[rows, :] + dst_a[rows, :] * cin_ref[...]
            return 0

        lax.fori_loop(0, m, fix, 0)
        mbuf[tc:, :] = _up_a_segment(mbuf[0:SUBLANES, :], cin_ref[SUBLANES - 1:SUBLANES, :])
        sneg = _sig(-lam_v)
        for j in range(NB_LRU):
            ln = _lanes(j)
            lamt = gh_ref[:, ln] + mbuf[SUBLANES:, ln]
            rj, ij, cj = r_ref[:, ln], i_ref[:, ln], c_ref[:, ln]
            a = src_a[:, ln]
            om = om_ref[:, ln]
            inv_mult = lax.rsqrt(om)
            mult = om * inv_mult
            g_a = lamt * hbuf[0:tc, ln]
            g_mult = lamt * ij * cj
            g_i = lamt * mult * cj
            g_c = lamt * mult * ij
            g_log_a = g_a * a - g_mult * a * a * inv_mult
            glam_ref[:, ln] += jnp.sum(g_log_a * rj, axis=0, keepdims=True) * LRU_C * sneg[:, ln]
            g_ra = g_log_a * (-LRU_C) * sp[:, ln] * rj * (1.0 - rj)
            g_ia = g_i * ij * (1.0 - ij)
            gba_ref[:, ln] += jnp.sum(g_ra, axis=0, keepdims=True)
            gbx_ref[:, ln] += jnp.sum(g_ia, axis=0, keepdims=True)
            cjb, grb, gib = cj.astype(bf16), g_ra.astype(bf16), g_ia.astype(bf16)
            gwa_ref[j] += _dot_tn(cjb, grb)
            gwx_ref[j] += _dot_tn(cjb, gib)
            g_c = g_c + _dot_nt(grb, wa_ref[j]) + _dot_nt(gib, wx_ref[j])
            gce[0:tc, ln] = g_c
            gcb_ref[:, ln] += jnp.sum(g_c, axis=0, keepdims=True)
        for d in range(TAPS_BACK):
            blk = slice(SUBLANES * d, SUBLANES * (d + 1))
            gce[tc + SUBLANES * d:tc + SUBLANES * (d + 1), :] = _up_a_segment(gce[blk, :], gcc_ref[SUBLANES * d:SUBLANES * d + 1, :])
        gcc_ref[...] = gce[0:EDGE, :]
        gc = gce[0:tc, :]
        glx = cw_ref[CONV_WIDTH - 1:CONV_WIDTH, :] * gc
        gcw_ref[CONV_WIDTH - 1:CONV_WIDTH, :] += jnp.sum(gc * xe[EDGE:EDGE + tc, :], axis=0, keepdims=True)
        for k in range(CONV_WIDTH - 1):
            off = SUBLANES * (CONV_WIDTH - 1 - k)
            glx = glx + cw_ref[k:k + 1, :] * gce[off:off + tc, :]
            gcw_ref[k:k + 1, :] += jnp.sum(gc * xe[EDGE - off:EDGE - off + tc, :], axis=0, keepdims=True)
        glx_ref[...] = glx.astype(bf16)

    rev = lambda i: (nch - 1 - i, 0)
    halo = lambda rows: (lambda i: (jnp.maximum((nch - 1 - i) * (tc // rows) - 1, 0), 0))
    wide = lambda: pl.BlockSpec((tc, LRU_WIDTH), rev)
    vec = lambda: _full((1, LRU_WIDTH))
    hd = lambda: _full((LRU_HEADS, LRU_HEAD_DIM, LRU_HEAD_DIM))
    buf = lambda rows: pltpu.VMEM((rows, LRU_WIDTH), f32)
    outs = pl.pallas_call(
        body, name=name, grid=(nch,),
        in_specs=[wide(), wide(), pl.BlockSpec((SUBLANES, LRU_WIDTH), halo(SUBLANES)), wide(), wide(), wide(), wide(), wide(),
                  pl.BlockSpec((HALO_ROWS, LRU_WIDTH), halo(HALO_ROWS)), _full((CONV_WIDTH, LRU_WIDTH)), hd(), hd(), vec()]
        + [ANY] * ne,
        out_specs=[wide(), hd(), hd(), vec(), vec(), vec(), vec(), _full((CONV_WIDTH, LRU_WIDTH))] + [ANY] * ne,
        out_shape=[_S((T, LRU_WIDTH), bf16), _S((LRU_HEADS, LRU_HEAD_DIM, LRU_HEAD_DIM)),
                   _S((LRU_HEADS, LRU_HEAD_DIM, LRU_HEAD_DIM)), _S((1, LRU_WIDTH)), _S((1, LRU_WIDTH)),
                   _S((1, LRU_WIDTH)), _S((1, LRU_WIDTH)), _S((CONV_WIDTH, LRU_WIDTH))] + (
            _chips_out_shapes(exchange) if ne else []),
        scratch_shapes=[buf(tc), buf(tc), buf(tc), buf(tc), buf(tc + SUBLANES), buf(tc + SUBLANES), buf(tc + EDGE),
                        buf(tc + EDGE), buf(SEGS), buf(EDGE), buf(1)] + (_chips_semaphores(ne) if ne else []),
        compiler_params=_cp("arbitrary"),
    )(gh, h, h, c, r, gi, om, lx, lx, convw, wa, wx, lam, *exchange)
    return outs[:8], list(outs[8:])


def merge_fwd(h, lg, zs, gsl, x, p, wbl, wout, gpost, wple, wpg, name):
    T = x.shape[0]
    tm = _tile(T, TOKEN_TILE)

    def body(h_ref, lg_ref, zs_ref, gs_ref, gl_ref, x_ref, p_ref, wbl_ref, wout_ref, gp_ref, wple_ref, wpg_ref,
             zl_ref, mix_ref, xo_ref):
        lg_v = lg_ref[...]
        yl = h_ref[...] * (lg_v * _sig(lg_v))
        zl = _dot(yl.astype(bf16), wbl_ref[...])
        merged = _sig(gs_ref[...]) * zs_ref[...] + _sig(gl_ref[...]) * zl
        mix = _dot(merged.astype(bf16), wout_ref[...])
        r2 = lax.rsqrt(jnp.mean(mix * mix, axis=-1, keepdims=True) + NORM_EPS)
        x1 = x_ref[...] + mix * r2 * gp_ref[...]
        q = _dot(x1.astype(bf16), wpg_ref[...])
        pe = _dot(p_ref[...].astype(bf16), wple_ref[...])
        zl_ref[...], mix_ref[...] = zl, mix
        xo_ref[...] = x1 + pe * _sig(q)

    dm = lambda: _rows(tm, D_MODEL)
    return pl.pallas_call(
        body, name=name, grid=(T // tm,),
        in_specs=[_rows(tm, LRU_WIDTH), _rows(tm, LRU_WIDTH), dm(), _rows(tm, D_MODEL, 0), _rows(tm, D_MODEL, 1), dm(),
                  _rows(tm, PLE_DIM), _resident((LRU_WIDTH, D_MODEL)), _resident((D_MODEL, D_MODEL)), _full((1, D_MODEL)),
                  _resident((PLE_DIM, D_MODEL)), _resident((D_MODEL, D_MODEL))],
        out_specs=[dm(), dm(), dm()],
        out_shape=[_S((T, D_MODEL))] * 3, compiler_params=_cp("parallel"),
    )(h, lg, zs, gsl, gsl, x, p, wbl, wout, gpost, wple, wpg)


def post_bwd(gx2, mix, x, p, wpg, wple, gpost, name):
    T = x.shape[0]
    tm = _tile(T, TOKEN_TILE_LIGHT)

    def body(gx2_ref, mix_ref, x_ref, p_ref, wpg_ref, wple_ref, gp_ref, gres_ref, gmix_ref, ggp_ref, gwpg_ref, gwple_ref):
        i = pl.program_id(0)
        gx2 = gx2_ref[...]
        mix = mix_ref[...]
        gp = gp_ref[...]
        r2 = lax.rsqrt(jnp.mean(mix * mix, axis=-1, keepdims=True) + NORM_EPS)
        nrm = mix * r2
        x1b = (x_ref[...] + nrm * gp).astype(bf16)
        pb = p_ref[...].astype(bf16)
        sq = _sig(_dot(x1b, wpg_ref[...]))
        pe = _dot(pb, wple_ref[...])
        gq = (gx2 * pe * sq * (1.0 - sq)).astype(bf16)
        _accumulate(gwple_ref, _dot_tn(pb, (gx2 * sq).astype(bf16)), i)
        _accumulate(gwpg_ref, _dot_tn(x1b, gq), i)
        gx1 = gx2 + _dot_nt(gq, wpg_ref[...])
        gres_ref[...] = gx1
        _accumulate(ggp_ref, jnp.sum(gx1 * nrm, axis=0, keepdims=True), i)
        gy = gx1 * gp
        gmix_ref[...] = (r2 * (gy - nrm * jnp.mean(gy * nrm, axis=-1, keepdims=True))).astype(bf16)

    dm = lambda: _rows(tm, D_MODEL)
    return pl.pallas_call(
        body, name=name, grid=(T // tm,),
        in_specs=[dm(), dm(), dm(), _rows(tm, PLE_DIM), _resident((D_MODEL, D_MODEL)), _resident((PLE_DIM, D_MODEL)),
                  _full((1, D_MODEL))],
        out_specs=[dm(), dm(), _full((1, D_MODEL)), _full((D_MODEL, D_MODEL)), _full((PLE_DIM, D_MODEL))],
        out_shape=[_S((T, D_MODEL)), _S((T, D_MODEL), bf16), _S((1, D_MODEL)), _S((D_MODEL, D_MODEL)),
                   _S((PLE_DIM, D_MODEL))],
        compiler_params=_cp("arbitrary"),
    )(gx2, mix, x, p, wpg, wple, gpost)


def gate_bwd(gmix, zl, zs, gsl, wout, name):
    T = zl.shape[0]
    tm = _tile(T, TOKEN_TILE_LIGHT)

    def body(gmix_ref, zl_ref, zs_ref, gs_ref, gl_ref, wout_ref, gzs_ref, gzl_ref, ggsl_ref, gwout_ref):
        i = pl.program_id(0)
        gmix = gmix_ref[...]
        gmerged = _dot_nt(gmix, wout_ref[...])
        zs, zl = zs_ref[...], zl_ref[...]
        ss, sl = _sig(gs_ref[...]), _sig(gl_ref[...])
        _accumulate(gwout_ref, _dot_tn((ss * zs + sl * zl).astype(bf16), gmix), i)
        gzs_ref[...] = (gmerged * ss).astype(bf16)
        gzl_ref[...] = (gmerged * sl).astype(bf16)
        ggsl_ref[:, :D_MODEL] = (gmerged * zs * ss * (1.0 - ss)).astype(bf16)
        ggsl_ref[:, D_MODEL:] = (gmerged * zl * sl * (1.0 - sl)).astype(bf16)

    dm = lambda: _rows(tm, D_MODEL)
    return pl.pallas_call(
        body, name=name, grid=(T // tm,),
        in_specs=[dm(), dm(), dm(), _rows(tm, D_MODEL, 0), _rows(tm, D_MODEL, 1), _resident((D_MODEL, D_MODEL))],
        out_specs=[dm(), dm(), _rows(tm, 2 * D_MODEL), _full((D_MODEL, D_MODEL))],
        out_shape=[_S((T, D_MODEL), bf16), _S((T, D_MODEL), bf16), _S((T, 2 * D_MODEL), bf16), _S((D_MODEL, D_MODEL))],
        compiler_params=_cp("arbitrary"),
    )(gmix, zl, zs, gsl, gsl, wout)


def lru_out_bwd(gzl, h, lg, wbl, name):
    T = h.shape[0]
    tm = _tile(T, TOKEN_TILE_LIGHT)

    def body(gzl_ref, h_ref, lg_ref, wbl_ref, gh_ref, glg_ref, gwbl_ref):
        i = pl.program_id(0)
        gzl = gzl_ref[...]
        lg_v, hv = lg_ref[...], h_ref[...]
        slg = _sig(lg_v)
        silu = lg_v * slg
        _accumulate(gwbl_ref, _dot_tn((hv * silu).astype(bf16), gzl), i)
        gyl = _dot_nt(gzl, wbl_ref[...])
        gh_ref[...] = gyl * silu
        glg_ref[...] = (gyl * hv * slg * (1.0 + lg_v * (1.0 - slg))).astype(bf16)

    lw = lambda: _rows(tm, LRU_WIDTH)
    return pl.pallas_call(
        body, name=name, grid=(T // tm,),
        in_specs=[_rows(tm, D_MODEL), lw(), lw(), _resident((LRU_WIDTH, D_MODEL))],
        out_specs=[lw(), lw(), _full((LRU_WIDTH, D_MODEL))],
        out_shape=[_S((T, LRU_WIDTH)), _S((T, LRU_WIDTH), bf16), _S((LRU_WIDTH, D_MODEL))],
        compiler_params=_cp("arbitrary"),
    )(gzl, h, lg, wbl)


def in_proj_bwd(pieces, win, x, gres, g, name):
    T = x.shape[0]
    tm = _tile(T, MM_TILE_M // 2)
    widths = [pc.shape[1] for pc in pieces]
    offs = [sum(widths[:k]) for k in range(len(widths))]

    def body(*refs):
        pc_refs = refs[:len(widths)]
        w_ref, x_ref, gres_ref, g_ref, gx_ref, gg_ref = refs[len(widths):]
        i = pl.program_id(0)
        ghv = _dot_nt(pc_refs[0][...], w_ref[:, offs[0]:offs[0] + widths[0]])
        for k in range(1, len(widths)):
            ghv = ghv + _dot_nt(pc_refs[k][...], w_ref[:, offs[k]:offs[k] + widths[k]])
        xv = x_ref[...]
        r = lax.rsqrt(jnp.mean(xv * xv, axis=-1, keepdims=True) + NORM_EPS)
        nrm = xv * r
        gy = ghv * g_ref[...]
        gx_ref[...] = gres_ref[...] + r * (gy - nrm * jnp.mean(gy * nrm, axis=-1, keepdims=True))
        _accumulate(gg_ref, jnp.sum(ghv * nrm, axis=0, keepdims=True), i)

    dm = lambda: _rows(tm, D_MODEL)
    return pl.pallas_call(
        body, name=name, grid=(T // tm,),
        in_specs=[_rows(tm, wd) for wd in widths] + [_resident(win.shape), dm(), dm(), _full((1, D_MODEL))],
        out_specs=[dm(), _full((1, D_MODEL))],
        out_shape=[_S((T, D_MODEL)), _S((1, D_MODEL))], compiler_params=_cp("arbitrary"),
    )(*pieces, win, x, gres, g)


def loss_head(y, target, name):
    T = y.shape[0]
    tm = _tile(T, TOKEN_TILE_LIGHT)

    def body(y_ref, t_ref, l_ref, g_ref):
        i = pl.program_id(0)
        e = y_ref[...] - t_ref[...]
        g_ref[...] = e * (1.0 / D_MODEL)
        part = 0.5 * jnp.sum(jnp.sum(e * e, axis=-1, keepdims=True) * (1.0 / D_MODEL), axis=0, keepdims=True)

        @pl.when(i == 0)
        def _():
            l_ref[...] = part

        @pl.when(i > 0)
        def _():
            l_ref[...] += part

    return pl.pallas_call(
        body, name=name, grid=(T // tm,),
        in_specs=[_rows(tm, D_MODEL), _rows(tm, D_MODEL)], out_specs=[_full((1, 1)), _rows(tm, D_MODEL)],
        out_shape=[_S((1, 1)), _S((T, D_MODEL))],
        compiler_params=_cp("arbitrary"),
    )(y, target)


def _s5_operands(w, m, tag):
    b_re_t = jnp.transpose(w['s5_b_re'], (2, 0, 1))
    b_im_t = jnp.transpose(w['s5_b_im'], (2, 0, 1))
    ldt = w['s5_log_dt'][:, None]
    ab, pw, bb = s5_prep(w['s5_a_re'], w['s5_a_im'], ldt, b_re_t, b_im_t, m, "s5_prep" + tag)
    over_sublanes = lambda t: jnp.broadcast_to(t[..., None, :], t.shape[:-1] + (SEGS, S5_NS))
    ptab = pw.reshape(2, m, S5_NS)
    return dict(abar_b=over_sublanes(ab.reshape(2, S5_NS)), ptab_b=over_sublanes(ptab),
                ptab_rev_b=over_sublanes(ptab[:, ::-1, :]), bdb=_pack_bdb(bb).astype(bf16),
                cdb=_pack_cdb(w['s5_c_re'], w['s5_c_im']).astype(bf16), dvec=w['s5_d'][None, :],
                prep_in=(w['s5_a_re'], w['s5_a_im'], ldt, b_re_t, b_im_t))


def layer_fwd(x, p, w, tag, gather=()):
    T = x.shape[0]
    m = _tile(T, SCAN_CHUNK) // SEGS
    s5 = _s5_operands(w, m, tag)
    h_bf, usg, lx, lg, gsl = in_proj_fwd(x, w['g_pre'][None, :], w['w_in'], "in_proj_fwd" + tag)
    (ys, s_re, s_im, s_bf), gathered = s5_fwd(usg, s5['bdb'], s5['cdb'], s5['dvec'], s5['abar_b'], s5['ptab_b'],
                                              "s5_fwd" + tag, gather)
    glu, zs = s5_post_fwd(ys, usg, w['w_glu'], w['w_bs'], "s5_post_fwd" + tag)
    wa, wx = w['lru_w_a'].astype(bf16), w['lru_w_x'].astype(bf16)
    c, r, gi, om, hs = lru_fwd(lx, w['conv_w'], w['conv_b'][None, :], wa, wx, w['lru_b_a'][None, :],
                               w['lru_b_x'][None, :], w['lru_lambda'][None, :], "lru_fwd" + tag)
    zl, mix, x_out = merge_fwd(hs, lg, zs, gsl, x, p, w['w_bl'], w['w_out'], w['g_post'][None, :],
                               w['w_ple'], w['w_ple_gate'], "merge_fwd" + tag)
    saved = dict(x=x, p=p, h_bf=h_bf, usg=usg, lx=lx, lg=lg, gsl=gsl, ys=ys, s_re=s_re, s_im=s_im, s_bf=s_bf, glu=glu,
                 zs=zs, c=c, r=r, gi=gi, om=om, hs=hs, zl=zl, mix=mix, s5=s5, wa=wa, wx=wx)
    return x_out, saved, gathered


def layer_bwd(gx_out, w, sv, tag, exchange=()):
    s5 = sv['s5']
    g = {}
    gres, gmix, g_gpost, g['w_ple_gate'], g['w_ple'] = post_bwd(
        gx_out, sv['mix'], sv['x'], sv['p'], w['w_ple_gate'], w['w_ple'], w['g_post'][None, :], "post_bwd" + tag)
    gzs, gzl, ggsl, g['w_out'] = gate_bwd(gmix, sv['zl'], sv['zs'], sv['gsl'], w['w_out'], "gate_bwd" + tag)
    g_h, g_lg, g['w_bl'] = lru_out_bwd(gzl, sv['hs'], sv['lg'], w['w_bl'], "lru_out_bwd" + tag)
    g['g_post'] = g_gpost[0]
    (g_lx, g_wa, g_wx, g_ba, g_bx, g_lam, g_cb, g_cw), exchanged = lru_bwd(
        g_h, sv['hs'], sv['c'], sv['r'], sv['gi'], sv['om'], sv['lx'], w['conv_w'], sv['wa'], sv['wx'],
        w['lru_lambda'][None, :], "lru_bwd" + tag, exchange)
    g['lru_w_a'], g['lru_w_x'] = g_wa, g_wx
    g['lru_b_a'], g['lru_b_x'], g['lru_lambda'], g['conv_b'], g['conv_w'] = g_ba[0], g_bx[0], g_lam[0], g_cb[0], g_cw
    g_ys, g_sg, g['w_bs'], g['w_glu'] = s5_post_bwd(gzs, sv['glu'], sv['usg'], sv['ys'], w['w_bs'], w['w_glu'],
                                                    "s5_post_bwd" + tag)
    g_u, g_ab, g_d, g_bdb, g_cdb = s5_bwd(g_ys, sv['usg'], sv['s_re'], sv['s_im'], sv['s_bf'], s5['bdb'], s5['cdb'],
                                          s5['dvec'], s5['abar_b'], s5['ptab_rev_b'], "s5_bwd" + tag)
    g['s5_d'] = g_d[0]
    g['s5_c_re'], g['s5_c_im'] = _unpack_cdb(g_cdb)
    g_are, g_aim, g_ldt, g_bre_t, g_bim_t = s5_prep_bwd(*s5['prep_in'], g_ab.reshape(2, S5_GROUPS, S5_STATE),
                                                       _unpack_bdb(g_bdb), "s5_prep_bwd" + tag)
    g['s5_a_re'], g['s5_a_im'], g['s5_log_dt'] = g_are, g_aim, g_ldt
    g['s5_b_re'] = jnp.transpose(g_bre_t, (1, 2, 0))
    g['s5_b_im'] = jnp.transpose(g_bim_t, (1, 2, 0))
    pieces = [g_u, g_sg, g_lx, g_lg, ggsl]
    g['w_in'] = jnp.concatenate(mm_tn(sv['h_bf'], pieces[:3], "gw_in_a" + tag) + mm_tn(sv['h_bf'], pieces[3:], "gw_in_b" + tag),
                                axis=1)
    gx, g_gpre = in_proj_bwd(pieces, w['w_in'], sv['x'], gres, w['g_pre'][None, :], "in_proj_bwd" + tag)
    g['g_pre'] = g_gpre[0]
    return gx, g, exchanged


def _as_2d(a):
    return a.reshape((-1, a.shape[-1])) if a.ndim > 1 else a.reshape((1, -1))


def _adamw_update(w, gv, m, v):
    nm = ADAM_B1 * m + (1.0 - ADAM_B1) * gv
    nv = ADAM_B2 * v + (1.0 - ADAM_B2) * (gv * gv)
    bc1 = 1.0 - ADAM_B1 ** ADAM_STEP
    bc2 = 1.0 - ADAM_B2 ** ADAM_STEP
    return -ADAM_LR * ((nm / bc1) / (jnp.sqrt(nv / bc2) + ADAM_EPS) + ADAM_WD * w), nm, nv


def adamw(w, g, m, v, name):
    shape = w.shape
    w2, g2, m2, v2 = _as_2d(w), _as_2d(g), _as_2d(m), _as_2d(v)
    R, C = w2.shape
    tr = _row_tile(R, C)

    def body(w_ref, g_ref, m_ref, v_ref, d_ref, nm_ref, nv_ref):
        d_ref[...], nm_ref[...], nv_ref[...] = _adamw_update(w_ref[...], g_ref[...], m_ref[...], v_ref[...])

    spec = lambda: pl.BlockSpec((tr, C), lambda i: (i, 0))
    d, nm, nv = pl.pallas_call(
        body, name=name, grid=(R // tr,), in_specs=[spec() for _ in range(4)], out_specs=[spec() for _ in range(3)],
        out_shape=[_S((R, C))] * 3, compiler_params=_cp("parallel"),
    )(w2, g2, m2, v2)
    return d.reshape(shape), nm.reshape(shape), nv.reshape(shape)


def adamw_reduce(w, parts, theirs, m, v, chip, name):
    shape = w.shape
    C = shape[-1]
    R = math.prod(shape[1:-1])
    w3, m3, v3 = w.reshape(DEPTH, R, C), m.reshape(DEPTH, R, C), v.reshape(DEPTH, R, C)
    tr = _row_tile(R, C)

    def body(chip_ref, w_ref, *refs):
        layer_refs, (m_ref, v_ref, g_ref, d_ref, nm_ref, nv_ref) = refs[:2 * DEPTH], refs[2 * DEPTH:]
        layer = pl.program_id(0)
        for l in range(DEPTH):
            @pl.when(layer == l)
            def _():
                a_ref, t_ref = layer_refs[2 * l], layer_refs[2 * l + 1]
                gv = ((a_ref[0] + t_ref[0].astype(f32)) + t_ref[1].astype(f32)) + t_ref[2].astype(f32)
                g_ref[0] = gv
                d_ref[0], nm_ref[0], nv_ref[0] = _adamw_update(w_ref[0], gv, m_ref[0], v_ref[0])

    spec = lambda: pl.BlockSpec((1, tr, C), lambda l, i, c: (l, i, 0))
    rows_of = lambda l: (lambda ll, i, c: jnp.where(ll == l, i, 0))
    layer_specs = []
    for l in range(DEPTH):
        layer_specs.append(pl.BlockSpec((1, tr, C), lambda ll, i, c, r=rows_of(l): (c[0], r(ll, i, c), 0)))
        layer_specs.append(pl.BlockSpec((3, tr, C), lambda ll, i, c, r=rows_of(l): (0, r(ll, i, c), 0)))
    grid_spec = pltpu.PrefetchScalarGridSpec(
        num_scalar_prefetch=1, grid=(DEPTH, R // tr),
        in_specs=[spec()] + layer_specs + [spec(), spec()], out_specs=[spec() for _ in range(4)])
    operands = [x.reshape(x.shape[0], R, C) for l in range(DEPTH) for x in (parts[l], theirs[l])]
    g, d, nm, nv = pl.pallas_call(
        body, name=name, grid_spec=grid_spec, out_shape=[_S((DEPTH, R, C))] * 4,
        compiler_params=_cp("arbitrary", "arbitrary"),
    )(chip, w3, *operands, m3, v3)
    return g.reshape(shape), d.reshape(shape), nm.reshape(shape), nv.reshape(shape)


MESH = pl.DeviceIdType.MESH
ANY = pl.BlockSpec(memory_space=pl.ANY)


def _place():
    return lax.axis_index("x"), lax.axis_index("y"), lax.axis_index("c")


def _other_chips(mx, my):
    return [(1 - mx, my), (mx, 1 - my), (1 - mx, 1 - my)]


def all_gather(shards, name):
    nb = len(shards)

    def body(*refs):
        phases = _gather_phases([s.shape for s in shards], refs[:nb], refs[nb:2 * nb], *refs[2 * nb:])
        for phase in phases:
            phase()

    outs = pl.pallas_call(
        body, name=name, out_shape=_gather_out_shapes(shards), in_specs=[ANY] * nb, out_specs=[ANY] * nb,
        scratch_shapes=_gather_semaphores(nb),
    )(*shards)
    return list(outs)


GATHER_COPIES = 9
OWN_SIB, OWN_X, OWN_Y, X_SIB, Y_SIB, RELAY_X, RELAY_Y, DIAG0_SIB, DIAG1_SIB = range(GATHER_COPIES)


def _gather_out_shapes(shards):
    return [_S((N_DEV,) + s.shape, s.dtype) for s in shards]


def _gather_semaphores(nb):
    return [pltpu.SemaphoreType.DMA((nb, GATHER_COPIES)), pltpu.SemaphoreType.DMA((nb, GATHER_COPIES)),
            pltpu.SemaphoreType.DMA((nb,))]


def _gather_phases(shapes, x_refs, out_refs, send_sems, recv_sems, local_sems):
    nb = len(shapes)
    mx, my, mc = _place()
    sibling, xn, yn = (mx, my, 1 - mc), (1 - mx, my, mc), (mx, 1 - my, mc)

    def block(b, px, py, pc, half=None):
        ref = out_refs[b].at[4 * px + 2 * py + pc]
        if half is None:
            return ref
        n = shapes[b][0] // 2
        return ref.at[pl.ds(half * n, n)]

    def copy(b, k, dst, to, src=None):
        return pltpu.make_async_remote_copy(
            src_ref=dst if src is None else src, dst_ref=dst, send_sem=send_sems.at[b, k],
            recv_sem=recv_sems.at[b, k], device_id=to, device_id_type=MESH)

    def send(b, k):
        if k in (OWN_X, OWN_Y, OWN_SIB):
            return copy(b, k, block(b, mx, my, mc), {OWN_X: xn, OWN_Y: yn, OWN_SIB: sibling}[k], src=x_refs[b])
        what, to = {RELAY_X: ((1 - mx, my, mc, 0), yn), RELAY_Y: ((mx, 1 - my, mc, 1), xn),
                    X_SIB: ((1 - mx, my, mc), sibling), Y_SIB: ((mx, 1 - my, mc), sibling),
                    DIAG0_SIB: ((1 - mx, 1 - my, mc, 0), sibling), DIAG1_SIB: ((1 - mx, 1 - my, mc, 1), sibling)}[k]
        return copy(b, k, block(b, *what), to)

    def local(b):
        return pltpu.make_async_copy(x_refs[b], block(b, mx, my, mc), local_sems.at[b])

    def send_own():
        for k in (OWN_X, OWN_Y, OWN_SIB):
            for b in range(nb):
                send(b, k).start()
        for b in range(nb):
            local(b).start()

    def relay_neighbours():
        for b in range(nb):
            copy(b, OWN_X, block(b, 1 - mx, my, mc), xn).wait_recv()
            send(b, RELAY_X).start()
            send(b, X_SIB).start()
        for b in range(nb):
            copy(b, OWN_Y, block(b, mx, 1 - my, mc), yn).wait_recv()
            send(b, RELAY_Y).start()
            send(b, Y_SIB).start()

    def hand_on_diagonal():
        for b in range(nb):
            copy(b, RELAY_X, block(b, 1 - mx, 1 - my, mc, 0), yn).wait_recv()
            send(b, DIAG0_SIB).start()
            copy(b, RELAY_Y, block(b, 1 - mx, 1 - my, mc, 1), xn).wait_recv()
            send(b, DIAG1_SIB).start()

    def finish():
        for b in range(nb):
            copy(b, OWN_SIB, block(b, mx, my, 1 - mc), sibling).wait_recv()
            copy(b, X_SIB, block(b, 1 - mx, my, 1 - mc), sibling).wait_recv()
            copy(b, Y_SIB, block(b, mx, 1 - my, 1 - mc), sibling).wait_recv()
            copy(b, DIAG0_SIB, block(b, 1 - mx, 1 - my, 1 - mc, 0), sibling).wait_recv()
            copy(b, DIAG1_SIB, block(b, 1 - mx, 1 - my, 1 - mc, 1), sibling).wait_recv()
        for b in range(nb):
            for k in range(GATHER_COPIES):
                send(b, k).wait_send()
            local(b).wait()

    return send_own, relay_neighbours, hand_on_diagonal, finish


def exchange_sibling(gs, name):
    nb = len(gs)

    def body(*refs):
        g_refs, recv_refs, send_sems, recv_sems = refs[:nb], refs[nb:2 * nb], refs[2 * nb], refs[2 * nb + 1]
        mx, my, mc = _place()
        copies = [pltpu.make_async_remote_copy(
            src_ref=g_refs[b].at[2 * k + 1 - mc], dst_ref=recv_refs[b].at[k], send_sem=send_sems.at[b, k],
            recv_sem=recv_sems.at[b, k], device_id=(mx, my, 1 - mc), device_id_type=MESH)
            for b in range(nb) for k in range(4)]
        for cp in copies:
            cp.start()
        for cp in copies:
            cp.wait()

    outs = pl.pallas_call(
        body, name=name, out_shape=[_S((4,) + g.shape[1:], g.dtype) for g in gs], in_specs=[ANY] * nb,
        out_specs=[ANY] * nb,
        scratch_shapes=[pltpu.SemaphoreType.DMA((nb, 4)), pltpu.SemaphoreType.DMA((nb, 4))],
    )(*gs)
    return list(outs)


def exchange_chips(parts, name):
    nb = len(parts)

    def body(*refs):
        start, finish = _chips_phases(refs[:nb], refs[nb:2 * nb], refs[2 * nb], refs[2 * nb + 1])
        start()
        finish()

    outs = pl.pallas_call(
        body, name=name, out_shape=_chips_out_shapes(parts), in_specs=[ANY] * nb, out_specs=[ANY] * nb,
        scratch_shapes=_chips_semaphores(nb),
    )(*parts)
    return list(outs)


def _chips_out_shapes(parts):
    return [_S((3,) + a.shape[1:], a.dtype) for a in parts]


def _chips_semaphores(nb):
    return [pltpu.SemaphoreType.DMA((nb, 3)), pltpu.SemaphoreType.DMA((nb, 3))]


def _chips_phases(a_refs, recv_refs, send_sems, recv_sems):
    mx, my, mc = _place()

    def copies():
        return [pltpu.make_async_remote_copy(
            src_ref=a_refs[b].at[2 * px + py], dst_ref=recv_refs[b].at[j], send_sem=send_sems.at[b, j],
            recv_sem=recv_sems.at[b, j], device_id=(px, py, mc), device_id_type=MESH)
            for b in range(len(a_refs)) for j, (px, py) in enumerate(_other_chips(mx, my))]

    def start():
        for cp in copies():
            cp.start()

    def finish():
        for cp in copies():
            cp.wait()

    return start, finish


def add_sibling(g, theirs, core, name, wire_dtype=f32):
    shp = theirs.shape
    C = shp[-1]
    R = math.prod(shp[1:-1])
    tr = _row_tile(R, C)
    narrow = wire_dtype != f32

    def body(core_ref, g_ref, t_ref, o_ref, *wire_ref):
        s = g_ref[...] + t_ref[...]
        o_ref[...] = s
        if narrow:
            wire_ref[0][...] = s.astype(wire_dtype)

    blk = lambda: pl.BlockSpec((1, tr, C), lambda k, i, c: (k, i, 0))
    grid_spec = pltpu.PrefetchScalarGridSpec(
        num_scalar_prefetch=1, grid=(4, R // tr),
        in_specs=[pl.BlockSpec((1, tr, C), lambda k, i, c: (2 * k + c[0], i, 0)), blk()],
        out_specs=[blk(), blk()] if narrow else [blk()])
    outs = pl.pallas_call(
        body, name=name, grid_spec=grid_spec,
        out_shape=[_S((4, R, C), f32)] + ([_S((4, R, C), wire_dtype)] if narrow else []),
        compiler_params=_cp("parallel", "parallel"),
    )(core, g.reshape(N_DEV, R, C), theirs.reshape(4, R, C))
    part = outs[0].reshape(shp)
    return part, (outs[1].reshape(shp) if narrow else part)


def add_chips(a, theirs, chip, name):
    _, R, C = a.shape
    tr = _row_tile(R, C)

    def body(chip_ref, a_ref, t_ref, out_ref):
        out_ref[...] = ((a_ref[0] + t_ref[0]) + t_ref[1]) + t_ref[2]

    grid_spec = pltpu.PrefetchScalarGridSpec(
        num_scalar_prefetch=1, grid=(R // tr,),
        in_specs=[pl.BlockSpec((1, tr, C), lambda i, c: (c[0], i, 0)), pl.BlockSpec((3, tr, C), lambda i, c: (0, i, 0))],
        out_specs=pl.BlockSpec((tr, C), lambda i, c: (i, 0)))
    return pl.pallas_call(
        body, name=name, grid_spec=grid_spec, out_shape=_S((R, C), a.dtype), compiler_params=_cp("parallel"),
    )(chip, a, theirs)


def _round_up(n, q):
    return (n + q - 1) // q * q


def _lane_rows(a):
    flat = a.reshape(-1)
    n = _round_up(flat.shape[0], SUBLANES * LANES)
    return jnp.pad(flat, (0, n - flat.shape[0])).reshape(-1, LANES)


def _full_to_shards(full, axis):
    shp = full.shape
    s = shp[axis] // N_DEV
    cut = full.reshape(shp[:axis] + (N_DEV, s) + shp[axis + 1:])
    return jnp.moveaxis(cut, axis, 0)


def _shards_to_full(parts, axis):
    shp = list(parts.shape[1:])
    shp[axis] *= N_DEV
    return jnp.moveaxis(parts, 0, axis).reshape(tuple(shp))


def kernel(x, p, g_pre, w_in, s5_a_re, s5_a_im, s5_log_dt, s5_b_re, s5_b_im, s5_c_re, s5_c_im, s5_d, w_glu, w_bs, conv_w, conv_b, lru_w_a, lru_b_a, lru_w_x, lru_b_x, lru_lambda, w_bl, w_out, g_post, w_ple, w_ple_gate, loss_target, m_g_pre, m_w_in, m_s5_a_re, m_s5_a_im, m_s5_log_dt, m_s5_b_re, m_s5_b_im, m_s5_c_re, m_s5_c_im, m_s5_d, m_w_glu, m_w_bs, m_conv_w, m_conv_b, m_lru_w_a, m_lru_b_a, m_lru_w_x, m_lru_b_x, m_lru_lambda, m_w_bl, m_w_out, m_g_post, m_w_ple, m_w_ple_gate, v_g_pre, v_w_in, v_s5_a_re, v_s5_a_im, v_s5_log_dt, v_s5_b_re, v_s5_b_im, v_s5_c_re, v_s5_c_im, v_s5_d, v_w_glu, v_w_bs, v_conv_w, v_conv_b, v_lru_w_a, v_lru_b_a, v_lru_w_x, v_lru_b_x, v_lru_lambda, v_w_bl, v_w_out, v_g_post, v_w_ple, v_w_ple_gate):
    given = dict(locals())
    W = {n: given[n] for n in WEIGHTS}
    M = {n: given["m_" + n] for n in WEIGHTS}
    V = {n: given["v_" + n] for n in WEIGHTS}
    xs, target = to_scan_order(x[0]), to_scan_order(loss_target[0])
    ps = [to_scan_order(p[i, 0]) for i in range(DEPTH)]

    mx, my, mc = _place()
    core = jnp.reshape(mc, (1,)).astype(jnp.int32)
    chip = jnp.reshape(2 * mx + my, (1,)).astype(jnp.int32)

    names = list(SHARDED)
    conv_rows = PAIR - CONV_WIDTH

    def layer_shards(i):
        return [W[n][i].astype(bf16) if n in GATHER_BF16 else jnp.pad(W[n][i], ((0, conv_rows), (0, 0))) for n in names]

    def layer_weights(i, gathered):
        full = {n: _shards_to_full(g if n in GATHER_BF16 else g[:, :CONV_WIDTH], SHARDED[n] - 1)
                for n, g in zip(names, gathered)}
        return {n: (full[n] if n in SHARDED else W[n][i]) for n in WEIGHTS}

    act, saved, weights = xs, [], []
    gathered = all_gather(layer_shards(0), "comm_gather_weights")
    for i in range(DEPTH):
        weights.append(layer_weights(i, gathered))
        act, sv, gathered = layer_fwd(act, ps[i], weights[i], "_l%d" % i, layer_shards(i + 1) if i + 1 < DEPTH else ())
        saved.append(sv)
    loss_part, gact = loss_head(act, target, "loss_head")
    loss = lax.psum(loss_part[0, 0], ("x", "y", "c"))

    def sibling_sums(i, g):
        rep_rows = [_lane_rows(g[n].reshape(W[n].shape[1:])) for n in REPLICATED]
        n_rows = sum(r.shape[0] for r in rep_rows)
        pad_rows = _round_up(n_rows, N_DEV * SUBLANES) - n_rows
        rep_blocks = jnp.concatenate(rep_rows + [jnp.zeros((pad_rows, LANES), f32)]).reshape(N_DEV, -1, LANES)
        blocks = [_full_to_shards(g[n].reshape(weights[i][n].shape), SHARDED[n] - 1) for n in names] + [rep_blocks]
        theirs = exchange_sibling(blocks, "comm_reduce_sibling_l%d" % i)
        parts, wire = [], []
        for k, (b, t) in enumerate(zip(blocks, theirs)):
            part, sent = add_sibling(b, t, core, "reduce_add_sibling_%d_l%d" % (k, i), bf16 if k < len(names) else f32)
            parts.append(part)
            wire.append(sent)
        return parts, wire, [r.shape[0] for r in rep_rows]

    parts, others, rep_sizes, wire = [None] * DEPTH, [None] * DEPTH, None, ()
    for i in reversed(range(DEPTH)):
        gact, g, exchanged = layer_bwd(gact, weights[i], saved[i], "_l%d" % i, wire)
        if i + 1 < DEPTH:
            others[i + 1] = exchanged
        parts[i], wire, rep_sizes = sibling_sums(i, g)
    others[0] = exchange_chips(wire, "comm_reduce_chips")

    red, deltas, new_m, new_v = {}, {}, {}, {}
    for k, n in enumerate(names):
        red[n], deltas[n], new_m[n], new_v[n] = adamw_reduce(
            W[n], [parts[i][k] for i in range(DEPTH)], [others[i][k] for i in range(DEPTH)], M[n], V[n], chip, "adamw_" + n)
    pieces = [add_chips(parts[i][-1], others[i][-1], chip, "reduce_add_chips_l%d" % i) for i in range(DEPTH)]
    rep_all = [r.reshape(-1, LANES) for r in all_gather(pieces, "comm_gather_replicated")]
    off = 0
    for n, rows in zip(REPLICATED, rep_sizes):
        k = math.prod(W[n].shape[1:])
        red[n] = jnp.stack([rep_all[i][off:off + rows].reshape(-1)[:k] for i in range(DEPTH)]).reshape(W[n].shape)
        off += rows
        deltas[n], new_m[n], new_v[n] = adamw(W[n], red[n], M[n], V[n], "adamw_" + n)
    return (loss, from_scan_order(gact)[None], *[red[n] for n in WEIGHTS], *[deltas[n] for n in WEIGHTS],
            *[new_m[n] for n in WEIGHTS], *[new_v[n] for n in WEIGHTS])
```

```python
import math

import jax
import jax.numpy as jnp
from jax import lax
from jax.experimental import pallas as pl
from jax.experimental.pallas import tpu as pltpu

f32 = jnp.float32
bf16 = jnp.bfloat16

D_MODEL = 1024
DEPTH = 2
PLE_DIM = 256
NORM_EPS = 1e-6
S5_WIDTH = 512
S5_GROUP = 16
S5_GROUPS = 32
S5_STATE = 64
S5_NS = S5_GROUPS * S5_STATE
LRU_WIDTH = 1280
LRU_HEADS = 10
LRU_HEAD_DIM = 128
LRU_C = 8.0
CONV_WIDTH = 4
N_DEV = 8

ADAM_LR = 0.001
ADAM_B1 = 0.9
ADAM_B2 = 0.999
ADAM_EPS = 1e-08
ADAM_WD = 0.01
ADAM_STEP = 10

LANES = 128
SUBLANES = 8
SEGS = SUBLANES
SCAN_CHUNK = 256
TOKEN_TILE = 256
TOKEN_TILE_LIGHT = 512
MM_TILE_M = 1024
PAIR = 2 * SUBLANES
VMEM_LIMIT_BYTES = 56 * 1024 * 1024
ELEMENTWISE_BLOCK_BYTES = 1024 * 1024

WEIGHTS = ['g_pre', 'w_in', 's5_a_re', 's5_a_im', 's5_log_dt', 's5_b_re', 's5_b_im', 's5_c_re', 's5_c_im',
           's5_d', 'w_glu', 'w_bs', 'conv_w', 'conv_b', 'lru_w_a', 'lru_b_a', 'lru_w_x', 'lru_b_x',
           'lru_lambda', 'w_bl', 'w_out', 'g_post', 'w_ple', 'w_ple_gate']
SHARDED = {'w_in': 2, 'w_glu': 2, 'w_bs': 2, 'conv_w': 2, 'w_bl': 1, 'w_out': 1, 'w_ple': 2, 'w_ple_gate': 1}
GATHER_BF16 = ['w_in', 'w_glu', 'w_bs', 'w_bl', 'w_out', 'w_ple', 'w_ple_gate']
REPLICATED = [n for n in WEIGHTS if n not in SHARDED]


def _sig(x):
    return 0.5 * jnp.tanh(0.5 * x) + 0.5


def _gelu_parts(x):
    k = math.sqrt(2.0 / math.pi)
    t = jnp.tanh(k * (x + 0.044715 * x * x * x))
    return t, k


def _gelu(x):
    t, _ = _gelu_parts(x)
    return 0.5 * x * (1.0 + t)


def _gelu_grad(x):
    t, k = _gelu_parts(x)
    return 0.5 * (1.0 + t) + 0.5 * x * (1.0 - t * t) * k * (1.0 + 3.0 * 0.044715 * x * x)


def _one_minus_sq(a, log_a):
    z = 2.0 * log_a
    series = -z * (1.0 + z * (0.5 + z * (1.0 / 6.0 + z * (1.0 / 24.0 + z * (1.0 / 120.0)))))
    return jnp.where(z > -0.05, series, 1.0 - a * a)


def _softplus_neg(lam):
    return jnp.maximum(-lam, 0.0) + jnp.log(1.0 + jnp.exp(-jnp.abs(lam)))


def _dot(a, b):
    return jnp.dot(a, b, preferred_element_type=f32)


def _dot_nt(a, b):
    return lax.dot_general(a, b, (((1,), (1,)), ((), ())), preferred_element_type=f32)


def _dot_tn(a, b):
    return lax.dot_general(a, b, (((0,), (0,)), ((), ())), preferred_element_type=f32)


def _S(shape, dtype=f32):
    return jax.ShapeDtypeStruct(shape, dtype)


def _full(shape):
    nd = len(shape)
    return pl.BlockSpec(shape, lambda *_: (0,) * nd)


def _rows(tile, width, col=0):
    return pl.BlockSpec((tile, width), lambda i: (i, col))


def _cp(*semantics):
    return pltpu.CompilerParams(dimension_semantics=semantics or None, vmem_limit_bytes=VMEM_LIMIT_BYTES)


def _tile(n, want):
    t = min(n, want)
    assert n % t == 0, (n, want)
    return t


def _row_tile(R, C=LANES):
    cap = max(SUBLANES, min(R, ELEMENTWISE_BLOCK_BYTES // (4 * C)))
    for t in range(cap - cap % SUBLANES, 0, -SUBLANES):
        if R % t == 0:
            return t
    return R


def _lanes(j):
    return slice(LANES * j, LANES * (j + 1))


def _step_rows(k, n=SUBLANES):
    return pl.ds(pl.multiple_of(k * n, n), n)


def to_scan_order(a):
    T, C = a.shape
    tc = _tile(T, SCAN_CHUNK)
    return a.reshape(T // tc, SEGS, tc // SEGS, C).transpose(0, 2, 1, 3).reshape(T, C)


def from_scan_order(a):
    T, C = a.shape
    tc = _tile(T, SCAN_CHUNK)
    return a.reshape(T // tc, tc // SEGS, SEGS, C).transpose(0, 2, 1, 3).reshape(T, C)


def _resident(shape):
    nd = len(shape)
    return pl.BlockSpec(shape, lambda *_: (0,) * nd, pipeline_mode=pl.Buffered(1))


def mm_tn(a, bs, name):
    M, K = a.shape
    tm = _tile(M, MM_TILE_M)
    nb = len(bs)

    def body(a_ref, *refs):
        m = pl.program_id(0)
        av = a_ref[...]
        for b_ref, o_ref in zip(refs[:nb], refs[nb:]):
            _accumulate(o_ref, _dot_tn(av, b_ref[...]), m)

    outs = pl.pallas_call(
        body, name=name, grid=(M // tm,),
        in_specs=[_rows(tm, K)] + [_rows(tm, b.shape[1]) for b in bs],
        out_specs=[_full((K, b.shape[1])) for b in bs],
        out_shape=[_S((K, b.shape[1])) for b in bs],
        compiler_params=_cp("arbitrary"),
    )(a, *bs)
    return list(outs)


IN_PROJ_WIDTHS = (2 * S5_WIDTH, LRU_WIDTH, LRU_WIDTH, 2 * D_MODEL)


def in_proj_fwd(x, g, win, name):
    T = x.shape[0]
    tm = _tile(T, MM_TILE_M // 2)
    offs = [sum(IN_PROJ_WIDTHS[:k]) for k in range(len(IN_PROJ_WIDTHS))]

    def body(x_ref, g_ref, w_ref, h_ref, *out_refs):
        xv = x_ref[...]
        r = lax.rsqrt(jnp.mean(xv * xv, axis=-1, keepdims=True) + NORM_EPS)
        h = (xv * r * g_ref[...]).astype(bf16)
        h_ref[...] = h
        for o_ref, off, wd in zip(out_refs, offs, IN_PROJ_WIDTHS):
            o_ref[...] = _dot(h, w_ref[:, off:off + wd])

    return pl.pallas_call(
        body, name=name, grid=(T // tm,),
        in_specs=[_rows(tm, D_MODEL), _full((1, D_MODEL)), _resident(win.shape)],
        out_specs=[_rows(tm, D_MODEL)] + [_rows(tm, wd) for wd in IN_PROJ_WIDTHS],
        out_shape=[_S((T, D_MODEL), bf16)] + [_S((T, wd)) for wd in IN_PROJ_WIDTHS],
        compiler_params=_cp("parallel"),
    )(x, g, win)


def _s5_discretise(a_re, a_im, log_dt, b_re_t, b_im_t):
    dt = jnp.exp(log_dt)
    mag = jnp.exp(a_re * dt)
    ab_re = mag * jnp.cos(a_im * dt)
    ab_im = mag * jnp.sin(a_im * dt)
    den = a_re * a_re + a_im * a_im
    nr, ni = ab_re - 1.0, ab_im
    z_re = (nr * a_re + ni * a_im) / den
    z_im = (ni * a_re - nr * a_im) / den
    bb_re = z_re[None] * b_re_t - z_im[None] * b_im_t
    bb_im = z_re[None] * b_im_t + z_im[None] * b_re_t
    return ab_re, ab_im, bb_re, bb_im


def s5_prep(a_re, a_im, log_dt, b_re_t, b_im_t, m, name):
    G, N = a_re.shape

    def body(are_ref, aim_ref, ldt_ref, bre_ref, bim_ref, ab_ref, pw_ref, bb_ref):
        are, aim, ldt = are_ref[...], aim_ref[...], ldt_ref[...]
        ab_re, ab_im, bb_re, bb_im = _s5_discretise(are, aim, ldt, bre_ref[...], bim_ref[...])
        ab_ref[0], ab_ref[1] = ab_re, ab_im
        bb_ref[0], bb_ref[1] = bb_re, bb_im
        dt = jnp.exp(ldt)
        for k in range(m):
            mag = jnp.exp(are * dt * (k + 1.0))
            pw_ref[0, k] = mag * jnp.cos(aim * dt * (k + 1.0))
            pw_ref[1, k] = mag * jnp.sin(aim * dt * (k + 1.0))

    return pl.pallas_call(
        body, name=name,
        out_shape=[_S((2, G, N)), _S((2, m, G, N)), _S((2, S5_GROUP, G, N))], compiler_params=_cp(),
    )(a_re, a_im, log_dt, b_re_t, b_im_t)


def s5_prep_bwd(a_re, a_im, log_dt, b_re_t, b_im_t, g_ab, g_bb, name):
    G, N = a_re.shape

    def body(are_ref, aim_ref, ldt_ref, bre_ref, bim_ref, gab_ref, gbb_ref, o_are, o_aim, o_ldt, o_bre, o_bim):
        _, vjp = jax.vjp(_s5_discretise, are_ref[...], aim_ref[...], ldt_ref[...], bre_ref[...], bim_ref[...])
        g_are, g_aim, g_ldt, g_bre, g_bim = vjp((gab_ref[0], gab_ref[1], gbb_ref[0], gbb_ref[1]))
        o_are[...], o_aim[...], o_ldt[...], o_bre[...], o_bim[...] = g_are, g_aim, g_ldt, g_bre, g_bim

    return pl.pallas_call(
        body, name=name,
        out_shape=[_S((G, N)), _S((G, N)), _S((G, 1)), _S((S5_GROUP, G, N)), _S((S5_GROUP, G, N))],
        compiler_params=_cp(),
    )(a_re, a_im, log_dt, b_re_t, b_im_t, g_ab, g_bb)


NB_S5 = S5_NS // LANES
CB_S5 = S5_WIDTH // LANES
SB_PER_CB = NB_S5 // CB_S5
GRP_PER_SB = LANES // S5_STATE
S5_JB = 8


def _bdb_mask():
    j = jnp.arange(NB_S5)
    own_rows = (j[:, None] % SB_PER_CB == jnp.arange(SB_PER_CB)[None, :]).astype(f32)
    eye = jnp.eye(GRP_PER_SB, dtype=f32)
    return own_rows[:, :, None, None, None, None, None] * eye[None, None, :, None, None, :, None]


def _pack_bdb(bb):
    v = jnp.transpose(bb.reshape(2, S5_GROUP, NB_S5, GRP_PER_SB, S5_STATE), (2, 3, 1, 0, 4))
    full = v[:, None, :, :, :, None, :] * _bdb_mask()
    return full.reshape(NB_S5, LANES, 2 * LANES)


def _unpack_bdb(g_bdb):
    g7 = g_bdb.reshape(NB_S5, SB_PER_CB, GRP_PER_SB, S5_GROUP, 2, GRP_PER_SB, S5_STATE)
    v = jnp.sum(g7 * _bdb_mask(), axis=(1, 5))
    return jnp.transpose(v, (3, 2, 0, 1, 4)).reshape(2, S5_GROUP, S5_GROUPS, S5_STATE)


def _pack_cdb(c_re, c_im):
    gl = S5_GROUPS // CB_S5
    c2 = jnp.stack([c_re, -c_im]).reshape(2, CB_S5, gl, S5_GROUP, S5_STATE)
    eye = jnp.eye(gl, dtype=f32)
    full = jnp.transpose(c2, (1, 0, 2, 4, 3))[:, :, :, :, None, :] * eye[None, None, :, None, :, None]
    return full.reshape(CB_S5, 2 * SB_PER_CB * LANES, LANES)


def _unpack_cdb(g_cdb):
    gl = S5_GROUPS // CB_S5
    g6 = g_cdb.reshape(CB_S5, 2, gl, S5_STATE, gl, S5_GROUP)
    eye = jnp.eye(gl, dtype=f32)
    v = jnp.sum(g6 * eye[None, None, :, None, :, None], axis=4)
    v = jnp.transpose(v, (1, 0, 2, 4, 3)).reshape(2, S5_GROUPS, S5_GROUP, S5_STATE)
    return v[0], -v[1]


def _state_cat(ref, c):
    w = SB_PER_CB * LANES
    return jnp.concatenate([ref[:, w * c:w * (c + 1)], ref[:, S5_NS + w * c:S5_NS + w * (c + 1)]], axis=1)


def _state_pair(ref, j):
    return jnp.concatenate([ref[:, _lanes(j)], ref[:, S5_NS + LANES * j:S5_NS + LANES * (j + 1)]], axis=1)


def s5_fwd(usg, bdb, cdb, dvec, abar_b, ptab_b, name, gather=()):
    T = usg.shape[0]
    tc = _tile(T, SCAN_CHUNK)
    m = tc // SEGS
    nsteps = T // tc
    ng = len(gather)
    assert ptab_b.shape == (2, m, SEGS, S5_NS) and m % 2 == 0

    def body(*refs):
        u_ref, bdb_ref, cdb_ref, d_ref, a_ref, p_ref = refs[:6]
        ys_ref, sre_ref, sim_ref, sbf_ref = refs[6 + ng:10 + ng]
        src_re, src_im, dst_re, dst_im, cin_ref, carry_ref = refs[10 + 2 * ng:16 + 2 * ng]
        i = pl.program_id(0)
        if ng:
            phases = _gather_phases([s.shape for s in gather], refs[6:6 + ng], refs[10 + ng:10 + 2 * ng],
                                    *refs[16 + 2 * ng:])
            for phase, step in zip(phases, (0, nsteps // 2, (3 * nsteps) // 4, nsteps - 1)):
                pl.when(i == step)(phase)

        @pl.when(i == 0)
        def _():
            carry_ref[...] = jnp.zeros_like(carry_ref)

        u = u_ref[...]
        ub = u.astype(bf16)
        for j in range(NB_S5):
            bu = _dot(ub[:, _lanes(j // SB_PER_CB)], bdb_ref[j])
            src_re[:, _lanes(j)] = bu[:, :LANES]
            src_im[:, _lanes(j)] = bu[:, LANES:]
        for j0 in range(0, NB_S5, S5_JB):
            def kstep(k, st):
                rows = _step_rows(k)
                out = []
                for q in range(S5_JB):
                    ln = _lanes(j0 + q)
                    sr, si = st[2 * q], st[2 * q + 1]
                    ar, ai = a_ref[0, :, ln], a_ref[1, :, ln]
                    nr = ar * sr - ai * si + src_re[rows, ln]
                    ni = ar * si + ai * sr + src_im[rows, ln]
                    dst_re[rows, ln] = nr
                    dst_im[rows, ln] = ni
                    out += [nr, ni]
                return tuple(out)

            ends = lax.fori_loop(0, m, kstep, tuple(jnp.zeros((SEGS, LANES), f32) for _ in range(2 * S5_JB)))
            for q in range(S5_JB):
                ln = _lanes(j0 + q)
                er, ei = ends[2 * q], ends[2 * q + 1]
                cr, ci = carry_ref[0, :, ln], carry_ref[1, :, ln]
                amr, ami = p_ref[0, m - 1, 0:1, ln], p_ref[1, m - 1, 0:1, ln]
                rows_r, rows_i = [], []
                for s in range(SEGS):
                    rows_r.append(cr)
                    rows_i.append(ci)
                    cr, ci = (er[s:s + 1, :] + amr * cr - ami * ci, ei[s:s + 1, :] + amr * ci + ami * cr)
                cin_ref[0, 0:SEGS, ln] = _stack_rows(rows_r)
                cin_ref[1, 0:SEGS, ln] = _stack_rows(rows_i)
                carry_ref[0, :, ln] = cr
                carry_ref[1, :, ln] = ci
        cin_ref[:, SEGS:, :] = cin_ref[:, 0:SEGS, :]

        def fix(k2, _):
            rows = _step_rows(k2, PAIR)
            pr = p_ref[0, pl.ds(2 * k2, 2)].reshape(PAIR, S5_NS)
            pi = p_ref[1, pl.ds(2 * k2, 2)].reshape(PAIR, S5_NS)
            cr, ci = cin_ref[0], cin_ref[1]
            sr = dst_re[rows, :] + pr * cr - pi * ci
            si = dst_im[rows, :] + pr * ci + pi * cr
            sre_ref[rows, :] = sr
            sim_ref[rows, :] = si
            sbf_ref[rows, 0:S5_NS] = sr.astype(bf16)
            sbf_ref[rows, S5_NS:] = si.astype(bf16)
            return 0

        lax.fori_loop(0, m // 2, fix, 0)
        for c in range(CB_S5):
            ys_ref[:, _lanes(c)] = _dot(_state_cat(sbf_ref, c), cdb_ref[c]) + d_ref[:, _lanes(c)] * u[:, _lanes(c)]

    st = lambda w: _rows(tc, w)
    outs = pl.pallas_call(
        body, name=name, grid=(nsteps,),
        in_specs=[_rows(tc, S5_WIDTH, 0), _resident(bdb.shape), _resident(cdb.shape), _full((1, S5_WIDTH)),
                  _resident((2, SEGS, S5_NS)), _resident((2, m, SEGS, S5_NS))] + [ANY] * ng,
        out_specs=[st(S5_WIDTH), st(S5_NS), st(S5_NS), st(2 * S5_NS)] + [ANY] * ng,
        out_shape=[_S((T, S5_WIDTH)), _S((T, S5_NS)), _S((T, S5_NS)), _S((T, 2 * S5_NS), bf16)] + (
            _gather_out_shapes(gather) if ng else []),
        scratch_shapes=[pltpu.VMEM((tc, S5_NS), f32)] * 4 + [pltpu.VMEM((2, PAIR, S5_NS), f32),
                                                             pltpu.VMEM((2, 1, S5_NS), f32)] + (
            _gather_semaphores(ng) if ng else []),
        compiler_params=_cp("arbitrary"),
    )(usg, bdb, cdb, dvec, abar_b, ptab_b, *gather)
    return outs[:4], list(outs[4:])


def s5_bwd(gys, usg, s_re, s_im, s_bf, bdb, cdb, dvec, abar_b, ptab_rev_b, name):
    T = gys.shape[0]
    tc = _tile(T, SCAN_CHUNK)
    m = tc // SEGS
    nch = T // tc
    hb = tc // SUBLANES

    def body(gy_ref, u_ref, sre_ref, sim_ref, hre_ref, him_ref, sbf_ref, bdb_ref, cdb_ref, d_ref, a_ref, p_ref,
             gu_ref, gab_ref, gd_ref, gbdb_ref, gcdb_ref,
             src_re, src_im, dst_re, dst_im, lam_ref, cin_ref, acc_ref, carry_ref):
        i = pl.program_id(0)

        @pl.when(i == 0)
        def _():
            carry_ref[...] = jnp.zeros_like(carry_ref)
            for ref in (gab_ref, gd_ref, gbdb_ref, gcdb_ref):
                ref[...] = jnp.zeros_like(ref)

        first = i == nch - 1
        gy = gy_ref[...]
        gyb = gy.astype(bf16)
        u = u_ref[...]
        ub = u.astype(bf16)
        w = SB_PER_CB * LANES
        for c in range(CB_S5):
            gs = _dot_nt(gyb[:, _lanes(c)], cdb_ref[c])
            src_re[:, w * c:w * (c + 1)] = gs[:, :w]
            src_im[:, w * c:w * (c + 1)] = gs[:, w:]
            gcdb_ref[c] += _dot_tn(_state_cat(sbf_ref, c), gyb[:, _lanes(c)])
        for j0 in range(0, NB_S5, S5_JB):
            def kstep(kk, st):
                rows = _step_rows(m - 1 - kk)
                out = []
                for q in range(S5_JB):
                    ln = _lanes(j0 + q)
                    lr, li = st[2 * q], st[2 * q + 1]
                    ar, ai = a_ref[0, :, ln], a_ref[1, :, ln]
                    nr = ar * lr + ai * li + src_re[rows, ln]
                    ni = ar * li - ai * lr + src_im[rows, ln]
                    dst_re[rows, ln] = nr
                    dst_im[rows, ln] = ni
                    out += [nr, ni]
                return tuple(out)

            ends = lax.fori_loop(0, m, kstep, tuple(jnp.zeros((SEGS, LANES), f32) for _ in range(2 * S5_JB)))
            for q in range(S5_JB):
                ln = _lanes(j0 + q)
                er, ei = ends[2 * q], ends[2 * q + 1]
                cr, ci = carry_ref[0, :, ln], carry_ref[1, :, ln]
                amr, ami = p_ref[0, 0, 0:1, ln], p_ref[1, 0, 0:1, ln]
                rows_r, rows_i = [None] * SEGS, [None] * SEGS
                for s in reversed(range(SEGS)):
                    rows_r[s], rows_i[s] = cr, ci
                    cr, ci = (er[s:s + 1, :] + amr * cr + ami * ci, ei[s:s + 1, :] + amr * ci - ami * cr)
                cin_ref[0, 0:SEGS, ln] = _stack_rows(rows_r)
                cin_ref[1, 0:SEGS, ln] = _stack_rows(rows_i)
                carry_ref[0, :, ln] = cr
                carry_ref[1, :, ln] = ci
        cin_ref[:, SEGS:, :] = cin_ref[:, 0:SEGS, :]
        acc_ref[...] = jnp.zeros_like(acc_ref)

        def fix_rows(rows, k2, prev_re, prev_im):
            pr = p_ref[0, pl.ds(2 * k2, 2)].reshape(PAIR, S5_NS)
            pi = p_ref[1, pl.ds(2 * k2, 2)].reshape(PAIR, S5_NS)
            cr, ci = cin_ref[0], cin_ref[1]
            lr = dst_re[rows, :] + pr * cr + pi * ci
            li = dst_im[rows, :] + pr * ci - pi * cr
            lam_ref[rows, 0:S5_NS] = lr.astype(bf16)
            lam_ref[rows, S5_NS:] = li.astype(bf16)
            acc_ref[0] += lr * prev_re + li * prev_im
            acc_ref[1] += li * prev_re - lr * prev_im

        last = slice(tc - SUBLANES, tc)
        wrap_re = _down_a_segment(sre_ref[last, :], jnp.where(first, 0.0, hre_ref[SUBLANES - 1:SUBLANES, :]))
        wrap_im = _down_a_segment(sim_ref[last, :], jnp.where(first, 0.0, him_ref[SUBLANES - 1:SUBLANES, :]))
        fix_rows(pl.ds(0, PAIR), 0, jnp.concatenate([wrap_re, sre_ref[0:SUBLANES, :]], axis=0),
                 jnp.concatenate([wrap_im, sim_ref[0:SUBLANES, :]], axis=0))

        def fix(k2, _):
            prev = pl.ds(pl.multiple_of(k2 * PAIR - SUBLANES, SUBLANES), PAIR)
            fix_rows(_step_rows(k2, PAIR), k2, sre_ref[prev, :], sim_ref[prev, :])
            return 0

        lax.fori_loop(1, m // 2, fix, 0)
        gab_ref[0] += jnp.sum(acc_ref[0], axis=0, keepdims=True)
        gab_ref[1] += jnp.sum(acc_ref[1], axis=0, keepdims=True)
        for c in range(CB_S5):
            x = gy[:, _lanes(c)] * d_ref[:, _lanes(c)]
            for j in range(SB_PER_CB * c, SB_PER_CB * (c + 1)):
                pair = _state_pair(lam_ref, j)
                x = x + _dot_nt(pair, bdb_ref[j])
                gbdb_ref[j] += _dot_tn(ub[:, _lanes(c)], pair)
            gu_ref[:, _lanes(c)] = x.astype(bf16)
        gd_ref[...] += jnp.sum(gy * u, axis=0, keepdims=True)

    rev = lambda i: (nch - 1 - i, 0)
    halo = lambda i: (jnp.maximum((nch - 1 - i) * hb - 1, 0), 0)
    blk = lambda wd: pl.BlockSpec((tc, wd), rev)
    return pl.pallas_call(
        body, name=name, grid=(nch,),
        in_specs=[blk(S5_WIDTH), blk(S5_WIDTH), blk(S5_NS), blk(S5_NS),
                  pl.BlockSpec((SUBLANES, S5_NS), halo), pl.BlockSpec((SUBLANES, S5_NS), halo), blk(2 * S5_NS),
                  _resident(bdb.shape), _resident(cdb.shape), _full((1, S5_WIDTH)),
                  _resident((2, SEGS, S5_NS)), _resident((2, m, SEGS, S5_NS))],
        out_specs=[blk(S5_WIDTH), _full((2, 1, S5_NS)), _full((1, S5_WIDTH)), _full(bdb.shape), _full(cdb.shape)],
        out_shape=[_S((T, S5_WIDTH), bf16), _S((2, 1, S5_NS)), _S((1, S5_WIDTH)), _S(bdb.shape), _S(cdb.shape)],
        scratch_shapes=[pltpu.VMEM((tc, S5_NS), f32)] * 4 + [
            pltpu.VMEM((tc, 2 * S5_NS), bf16), pltpu.VMEM((2, PAIR, S5_NS), f32), pltpu.VMEM((2, PAIR, S5_NS), f32),
            pltpu.VMEM((2, 1, S5_NS), f32)],
        compiler_params=_cp("arbitrary"),
    )(gys, usg, s_re, s_im, s_re, s_im, s_bf, bdb, cdb, dvec, abar_b, ptab_rev_b)


def s5_post_fwd(ys, usg, wglu, wbs, name):
    T = ys.shape[0]
    tm = _tile(T, TOKEN_TILE_LIGHT)

    def body(ys_ref, sg_ref, wglu_ref, wbs_ref, glu_ref, zs_ref):
        glu = _dot(_gelu(ys_ref[...]).astype(bf16), wglu_ref[...])
        sg = sg_ref[...]
        y2 = glu[:, :S5_WIDTH] * _sig(glu[:, S5_WIDTH:]) * (sg * _sig(sg))
        glu_ref[...] = glu
        zs_ref[...] = _dot(y2.astype(bf16), wbs_ref[...])

    return pl.pallas_call(
        body, name=name, grid=(T // tm,),
        in_specs=[_rows(tm, S5_WIDTH), _rows(tm, S5_WIDTH, 1), _resident((S5_WIDTH, 2 * S5_WIDTH)),
                  _resident((S5_WIDTH, D_MODEL))],
        out_specs=[_rows(tm, 2 * S5_WIDTH), _rows(tm, D_MODEL)],
        out_shape=[_S((T, 2 * S5_WIDTH)), _S((T, D_MODEL))], compiler_params=_cp("parallel"),
    )(ys, usg, wglu, wbs)


def _accumulate(ref, part, step):
    @pl.when(step == 0)
    def _():
        ref[...] = part

    @pl.when(step > 0)
    def _():
        ref[...] += part


def s5_post_bwd(gzs, glu, usg, ys, wbs, wglu, name):
    T = ys.shape[0]
    tm = _tile(T, TOKEN_TILE_LIGHT)

    def body(gzs_ref, glu_ref, sg_ref, ys_ref, wbs_ref, wglu_ref, gys_ref, gsg_ref, gwbs_ref, gwglu_ref):
        i = pl.program_id(0)
        glu = glu_ref[...]
        a, b = glu[:, :S5_WIDTH], glu[:, S5_WIDTH:]
        sg = sg_ref[...]
        ys = ys_ref[...]
        sb, ssg = _sig(b), _sig(sg)
        silu = sg * ssg
        _accumulate(gwbs_ref, _dot_tn((a * sb * silu).astype(bf16), gzs_ref[...]), i)
        gy2 = _dot_nt(gzs_ref[...], wbs_ref[...])
        g_a = gy2 * sb * silu
        g_b = gy2 * a * sb * (1.0 - sb) * silu
        gsg_ref[...] = (gy2 * a * sb * ssg * (1.0 + sg * (1.0 - ssg))).astype(bf16)
        gglu = jnp.concatenate([g_a, g_b], axis=1).astype(bf16)
        _accumulate(gwglu_ref, _dot_tn(_gelu(ys).astype(bf16), gglu), i)
        gys_ref[...] = _dot_nt(gglu, wglu_ref[...]) * _gelu_grad(ys)

    return pl.pallas_call(
        body, name=name, grid=(T // tm,),
        in_specs=[_rows(tm, D_MODEL), _rows(tm, 2 * S5_WIDTH), _rows(tm, S5_WIDTH, 1), _rows(tm, S5_WIDTH),
                  _resident((S5_WIDTH, D_MODEL)), _resident((S5_WIDTH, 2 * S5_WIDTH))],
        out_specs=[_rows(tm, S5_WIDTH), _rows(tm, S5_WIDTH), _full((S5_WIDTH, D_MODEL)), _full((S5_WIDTH, 2 * S5_WIDTH))],
        out_shape=[_S((T, S5_WIDTH)), _S((T, S5_WIDTH), bf16), _S((S5_WIDTH, D_MODEL)), _S((S5_WIDTH, 2 * S5_WIDTH))],
        compiler_params=_cp("arbitrary"),
    )(gzs, glu, usg, ys, wbs, wglu)


NB_LRU = LRU_WIDTH // LANES
LRU_JB = 5
TAPS_BACK = CONV_WIDTH - 1
EDGE = TAPS_BACK * SUBLANES
HALO_ROWS = 4 * SUBLANES


def _down_a_segment(blk, entering_row):
    sub = lax.broadcasted_iota(jnp.int32, blk.shape, 0)
    return jnp.where(sub == 0, entering_row, pltpu.roll(blk, 1, 0))


def _up_a_segment(blk, entering_row):
    sub = lax.broadcasted_iota(jnp.int32, blk.shape, 0)
    return jnp.where(sub == SUBLANES - 1, entering_row, pltpu.roll(blk, SUBLANES - 1, 0))


def _stack_rows(rows):
    sub = lax.broadcasted_iota(jnp.int32, (SUBLANES,) + rows[0].shape[1:], 0)
    out = jnp.broadcast_to(rows[0], sub.shape)
    for s in range(1, SUBLANES):
        out = jnp.where(sub == s, rows[s], out)
    return out


def _fill_conv_window(xe, x_ref, xh_ref, is_first, tc):
    xe[EDGE:, :] = x_ref[...]
    for i in range(1, TAPS_BACK + 1):
        row = HALO_ROWS - SUBLANES * i + SUBLANES - 1
        entering = jnp.where(is_first, 0.0, xh_ref[row:row + 1, :])
        blk = x_ref[tc - SUBLANES * i:tc - SUBLANES * (i - 1), :]
        xe[EDGE - SUBLANES * i:EDGE - SUBLANES * (i - 1), :] = _down_a_segment(blk, entering)


def lru_fwd(lx, convw, convb, wa, wx, ba, bx, lam, name):
    T = lx.shape[0]
    tc = _tile(T, SCAN_CHUNK)
    m = tc // SEGS
    hb = tc // HALO_ROWS

    def body(x_ref, xh_ref, cw_ref, cb_ref, wa_ref, wx_ref, ba_ref, bx_ref, lam_ref,
             c_ref, r_ref, i_ref, om_ref, h_ref, xe, src_a, src_b, dst_a, dst_h, cin_ref, carry_ref):
        i = pl.program_id(0)

        @pl.when(i == 0)
        def _():
            carry_ref[...] = jnp.zeros_like(carry_ref)

        _fill_conv_window(xe, x_ref, xh_ref, i == 0, tc)
        c = cb_ref[...] + cw_ref[0:1, :] * xe[0:tc, :]
        for k in range(1, CONV_WIDTH):
            c = c + cw_ref[k:k + 1, :] * xe[SUBLANES * k:SUBLANES * k + tc, :]
        c_ref[...] = c
        sp = _softplus_neg(lam_ref[...])
        for j in range(NB_LRU):
            ln = _lanes(j)
            cj = c[:, ln]
            cjb = cj.astype(bf16)
            r = _sig(_dot(cjb, wa_ref[j]) + ba_ref[:, ln])
            g = _sig(_dot(cjb, wx_ref[j]) + bx_ref[:, ln])
            r_ref[:, ln] = r
            i_ref[:, ln] = g
            log_a = -LRU_C * r * sp[:, ln]
            a = jnp.exp(log_a)
            src_a[:, ln] = a
            om = _one_minus_sq(a, log_a)
            om_ref[:, ln] = om
            src_b[:, ln] = jnp.sqrt(om) * (g * cj)
        for j0 in range(0, NB_LRU, LRU_JB):
            def kstep(k, st):
                rows = _step_rows(k)
                out = []
                for q in range(LRU_JB):
                    ln = _lanes(j0 + q)
                    hh, ac = st[2 * q], st[2 * q + 1]
                    a = src_a[rows, ln]
                    hh = a * hh + src_b[rows, ln]
                    ac = a * ac
                    dst_h[rows, ln] = hh
                    dst_a[rows, ln] = ac
                    out += [hh, ac]
                return tuple(out)

            init = tuple(jnp.zeros((SEGS, LANES), f32) if q % 2 == 0 else jnp.ones((SEGS, LANES), f32)
                         for q in range(2 * LRU_JB))
            ends = lax.fori_loop(0, m, kstep, init)
            for q in range(LRU_JB):
                ln = _lanes(j0 + q)
                eh, ea = ends[2 * q], ends[2 * q + 1]
                cr = carry_ref[:, ln]
                rows_c = []
                for s in range(SEGS):
                    rows_c.append(cr)
                    cr = eh[s:s + 1, :] + ea[s:s + 1, :] * cr
                cin_ref[:, ln] = _stack_rows(rows_c)
                carry_ref[:, ln] = cr

        def fix(k, _):
            rows = _step_rows(k)
            h_ref[rows, :] = dst_h[rows, :] + dst_a[rows, :] * cin_ref[...]
            return 0

        lax.fori_loop(0, m, fix, 0)

    wide = lambda: _rows(tc, LRU_WIDTH)
    buf = lambda rows: pltpu.VMEM((rows, LRU_WIDTH), f32)
    return pl.pallas_call(
        body, name=name, grid=(T // tc,),
        in_specs=[wide(), pl.BlockSpec((HALO_ROWS, LRU_WIDTH), lambda i: (jnp.maximum(i * hb - 1, 0), 0)),
                  _full((CONV_WIDTH, LRU_WIDTH)), _full((1, LRU_WIDTH)),
                  _full((LRU_HEADS, LRU_HEAD_DIM, LRU_HEAD_DIM)), _full((LRU_HEADS, LRU_HEAD_DIM, LRU_HEAD_DIM)),
                  _full((1, LRU_WIDTH)), _full((1, LRU_WIDTH)), _full((1, LRU_WIDTH))],
        out_specs=[wide(), wide(), wide(), wide(), wide()],
        out_shape=[_S((T, LRU_WIDTH))] * 5,
        scratch_shapes=[buf(tc + EDGE), buf(tc), buf(tc), buf(tc), buf(tc), buf(SEGS), buf(1)],
        compiler_params=_cp("arbitrary"),
    )(lx, lx, convw, convb, wa, wx, ba, bx, lam)


def lru_bwd(gh, h, c, r, gi, om, lx, convw, wa, wx, lam, name, exchange=()):
    T = gh.shape[0]
    tc = _tile(T, SCAN_CHUNK)
    m = tc // SEGS
    nch = T // tc
    ne = len(exchange)
    NI = 13

    def body(*refs):
        (gh_ref, h_ref, hh_ref, c_ref, r_ref, i_ref, om_ref, x_ref, xh_ref, cw_ref, wa_ref, wx_ref, lam_ref) = refs[:NI]
        glx_ref, gwa_ref, gwx_ref, gba_ref, gbx_ref, glam_ref, gcb_ref, gcw_ref = refs[NI + ne:NI + 8 + ne]
        (src_a, src_m, dst_a, dst_m, mbuf, hbuf, xe, gce, cin_ref, gcc_ref, carry_ref) = refs[NI + 8 + 2 * ne:NI + 19 + 2 * ne]
        i = pl.program_id(0)
        if ne:
            start, finish = _chips_phases(refs[NI:NI + ne], refs[NI + 8 + ne:NI + 8 + 2 * ne], *refs[NI + 19 + 2 * ne:])
            pl.when(i == 0)(start)
            pl.when(i == nch - 1)(finish)

        @pl.when(i == 0)
        def _():
            carry_ref[...] = jnp.zeros_like(carry_ref)
            gcc_ref[...] = jnp.zeros_like(gcc_ref)
            for ref in (gwa_ref, gwx_ref, gba_ref, gbx_ref, glam_ref, gcb_ref, gcw_ref):
                ref[...] = jnp.zeros_like(ref)

        first = i == nch - 1
        last = slice(tc - SUBLANES, tc)
        hbuf[SUBLANES:, :] = h_ref[...]
        hbuf[0:SUBLANES, :] = _down_a_segment(h_ref[last, :], jnp.where(first, 0.0, hh_ref[SUBLANES - 1:SUBLANES, :]))
        _fill_conv_window(xe, x_ref, xh_ref, first, tc)
        lam_v = lam_ref[...]
        sp = _softplus_neg(lam_v)
        a_all = jnp.exp(-LRU_C * r_ref[...] * sp)
        src_a[...] = a_all
        src_m[...] = a_all * gh_ref[...]
        for j0 in range(0, NB_LRU, LRU_JB):
            def kstep(kk, st):
                rows = _step_rows(m - 1 - kk)
                out = []
                for q in range(LRU_JB):
                    ln = _lanes(j0 + q)
                    mu, ac = st[2 * q], st[2 * q + 1]
                    a = src_a[rows, ln]
                    mu = a * mu + src_m[rows, ln]
                    ac = a * ac
                    dst_m[rows, ln] = mu
                    dst_a[rows, ln] = ac
                    out += [mu, ac]
                return tuple(out)

            init = tuple(jnp.zeros((SEGS, LANES), f32) if q % 2 == 0 else jnp.ones((SEGS, LANES), f32)
                         for q in range(2 * LRU_JB))
            ends = lax.fori_loop(0, m, kstep, init)
            for q in range(LRU_JB):
                ln = _lanes(j0 + q)
                em, ea = ends[2 * q], ends[2 * q + 1]
                cr = carry_ref[:, ln]
                rows_c = [None] * SEGS
                for s in reversed(range(SEGS)):
                    rows_c[s] = cr
                    cr = em[s:s + 1, :] + ea[s:s + 1, :] * cr
                cin_ref[:, ln] = _stack_rows(rows_c)
                carry_ref[:, ln] = cr

        def fix(k, _):
            rows = _step_rows(k)
            mbuf[rows, :] = dst_m[rows, :] + dst_a[rows, :] * cin_ref[...]
            return 0

        lax.fori_loop(0, m, fix, 0)
        mbuf[tc:, :] = _up_a_segment(mbuf[0:SUBLANES, :], cin_ref[SUBLANES - 1:SUBLANES, :])
        sneg = _sig(-lam_v)
        for j in range(NB_LRU):
            ln = _lanes(j)
            lamt = gh_ref[:, ln] + mbuf[SUBLANES:, ln]
            rj, ij, cj = r_ref[:, ln], i_ref[:, ln], c_ref[:, ln]
            a = src_a[:, ln]
            om = om_ref[:, ln]
            inv_mult = lax.rsqrt(om)
            mult = om * inv_mult
            g_a = lamt * hbuf[0:tc, ln]
            g_mult = lamt * ij * cj
            g_i = lamt * mult * cj
            g_c = lamt * mult * ij
            g_log_a = g_a * a - g_mult * a * a * inv_mult
            glam_ref[:, ln] += jnp.sum(g_log_a * rj, axis=0, keepdims=True) * LRU_C * sneg[:, ln]
            g_ra = g_log_a * (-LRU_C) * sp[:, ln] * rj * (1.0 - rj)
            g_ia = g_i * ij * (1.0 - ij)
            gba_ref[:, ln] += jnp.sum(g_ra, axis=0, keepdims=True)
            gbx_ref[:, ln] += jnp.sum(g_ia, axis=0, keepdims=True)
            cjb, grb, gib = cj.astype(bf16), g_ra.astype(bf16), g_ia.astype(bf16)
            gwa_ref[j] += _dot_tn(cjb, grb)
            gwx_ref[j] += _dot_tn(cjb, gib)
            g_c = g_c + _dot_nt(grb, wa_ref[j]) + _dot_nt(gib, wx_ref[j])
            gce[0:tc, ln] = g_c
            gcb_ref[:, ln] += jnp.sum(g_c, axis=0, keepdims=True)
        for d in range(TAPS_BACK):
            blk = slice(SUBLANES * d, SUBLANES * (d + 1))
            gce[tc + SUBLANES * d:tc + SUBLANES * (d + 1), :] = _up_a_segment(gce[blk, :], gcc_ref[SUBLANES * d:SUBLANES * d + 1, :])
        gcc_ref[...] = gce[0:EDGE, :]
        gc = gce[0:tc, :]
        glx = cw_ref[CONV_WIDTH - 1:CONV_WIDTH, :] * gc
        gcw_ref[CONV_WIDTH - 1:CONV_WIDTH, :] += jnp.sum(gc * xe[EDGE:EDGE + tc, :], axis=0, keepdims=True)
        for k in range(CONV_WIDTH - 1):
            off = SUBLANES * (CONV_WIDTH - 1 - k)
            glx = glx + cw_ref[k:k + 1, :] * gce[off:off + tc, :]
            gcw_ref[k:k + 1, :] += jnp.sum(gc * xe[EDGE - off:EDGE - off + tc, :], axis=0, keepdims=True)
        glx_ref[...] = glx.astype(bf16)

    rev = lambda i: (nch - 1 - i, 0)
    halo = lambda rows: (lambda i: (jnp.maximum((nch - 1 - i) * (tc // rows) - 1, 0), 0))
    wide = lambda: pl.BlockSpec((tc, LRU_WIDTH), rev)
    vec = lambda: _full((1, LRU_WIDTH))
    hd = lambda: _full((LRU_HEADS, LRU_HEAD_DIM, LRU_HEAD_DIM))
    buf = lambda rows: pltpu.VMEM((rows, LRU_WIDTH), f32)
    outs = pl.pallas_call(
        body, name=name, grid=(nch,),
        in_specs=[wide(), wide(), pl.BlockSpec((SUBLANES, LRU_WIDTH), halo(SUBLANES)), wide(), wide(), wide(), wide(), wide(),
                  pl.BlockSpec((HALO_ROWS, LRU_WIDTH), halo(HALO_ROWS)), _full((CONV_WIDTH, LRU_WIDTH)), hd(), hd(), vec()]
        + [ANY] * ne,
        out_specs=[wide(), hd(), hd(), vec(), vec(), vec(), vec(), _full((CONV_WIDTH, LRU_WIDTH))] + [ANY] * ne,
        out_shape=[_S((T, LRU_WIDTH), bf16), _S((LRU_HEADS, LRU_HEAD_DIM, LRU_HEAD_DIM)),
                   _S((LRU_HEADS, LRU_HEAD_DIM, LRU_HEAD_DIM)), _S((1, LRU_WIDTH)), _S((1, LRU_WIDTH)),
                   _S((1, LRU_WIDTH)), _S((1, LRU_WIDTH)), _S((CONV_WIDTH, LRU_WIDTH))] + (
            _chips_out_shapes(exchange) if ne else []),
        scratch_shapes=[buf(tc), buf(tc), buf(tc), buf(tc), buf(tc + SUBLANES), buf(tc + SUBLANES), buf(tc + EDGE),
                        buf(tc + EDGE), buf(SEGS), buf(EDGE), buf(1)] + (_chips_semaphores(ne) if ne else []),
        compiler_params=_cp("arbitrary"),
    )(gh, h, h, c, r, gi, om, lx, lx, convw, wa, wx, lam, *exchange)
    return outs[:8], list(outs[8:])


def merge_fwd(h, lg, zs, gsl, x, p, wbl, wout, gpost, wple, wpg, name, target=None):
    T = x.shape[0]
    tm = _tile(T, TOKEN_TILE)
    head = target is not None

    def body(h_ref, lg_ref, zs_ref, gs_ref, gl_ref, x_ref, p_ref, wbl_ref, wout_ref, gp_ref, wple_ref, wpg_ref, *refs):
        zl_ref, mix_ref, xo_ref = refs[head:head + 3]
        lg_v = lg_ref[...]
        yl = h_ref[...] * (lg_v * _sig(lg_v))
        zl = _dot(yl.astype(bf16), wbl_ref[...])
        merged = _sig(gs_ref[...]) * zs_ref[...] + _sig(gl_ref[...]) * zl
        mix = _dot(merged.astype(bf16), wout_ref[...])
        r2 = lax.rsqrt(jnp.mean(mix * mix, axis=-1, keepdims=True) + NORM_EPS)
        x1 = x_ref[...] + mix * r2 * gp_ref[...]
        q = _dot(x1.astype(bf16), wpg_ref[...])
        pe = _dot(p_ref[...].astype(bf16), wple_ref[...])
        zl_ref[...], mix_ref[...] = zl, mix
        x2 = x1 + pe * _sig(q)
        if head:
            e = x2 - refs[0][...]
            xo_ref[...] = e * (1.0 / D_MODEL)
            _accumulate(refs[4], 0.5 * jnp.sum(jnp.sum(e * e, axis=-1, keepdims=True) * (1.0 / D_MODEL), axis=0,
                                               keepdims=True), pl.program_id(0))
        else:
            xo_ref[...] = x2

    dm = lambda: _rows(tm, D_MODEL)
    return pl.pallas_call(
        body, name=name, grid=(T // tm,),
        in_specs=[_rows(tm, LRU_WIDTH), _rows(tm, LRU_WIDTH), dm(), _rows(tm, D_MODEL, 0), _rows(tm, D_MODEL, 1), dm(),
                  _rows(tm, PLE_DIM), _resident((LRU_WIDTH, D_MODEL)), _resident((D_MODEL, D_MODEL)), _full((1, D_MODEL)),
                  _resident((PLE_DIM, D_MODEL)), _resident((D_MODEL, D_MODEL))] + ([dm()] if head else []),
        out_specs=[dm(), dm(), dm()] + ([_full((1, 1))] if head else []),
        out_shape=[_S((T, D_MODEL))] * 3 + ([_S((1, 1))] if head else []),
        compiler_params=_cp("arbitrary" if head else "parallel"),
    )(h, lg, zs, gsl, gsl, x, p, wbl, wout, gpost, wple, wpg, *([target] if head else []))


def post_bwd(gx2, mix, x, p, wpg, wple, gpost, name):
    T = x.shape[0]
    tm = _tile(T, TOKEN_TILE_LIGHT)

    def body(gx2_ref, mix_ref, x_ref, p_ref, wpg_ref, wple_ref, gp_ref, gres_ref, gmix_ref, ggp_ref, gwpg_ref, gwple_ref):
        i = pl.program_id(0)
        gx2 = gx2_ref[...]
        mix = mix_ref[...]
        gp = gp_ref[...]
        r2 = lax.rsqrt(jnp.mean(mix * mix, axis=-1, keepdims=True) + NORM_EPS)
        nrm = mix * r2
        x1b = (x_ref[...] + nrm * gp).astype(bf16)
        pb = p_ref[...].astype(bf16)
        sq = _sig(_dot(x1b, wpg_ref[...]))
        pe = _dot(pb, wple_ref[...])
        gq = (gx2 * pe * sq * (1.0 - sq)).astype(bf16)
        _accumulate(gwple_ref, _dot_tn(pb, (gx2 * sq).astype(bf16)), i)
        _accumulate(gwpg_ref, _dot_tn(x1b, gq), i)
        gx1 = gx2 + _dot_nt(gq, wpg_ref[...])
        gres_ref[...] = gx1
        _accumulate(ggp_ref, jnp.sum(gx1 * nrm, axis=0, keepdims=True), i)
        gy = gx1 * gp
        gmix_ref[...] = (r2 * (gy - nrm * jnp.mean(gy * nrm, axis=-1, keepdims=True))).astype(bf16)

    dm = lambda: _rows(tm, D_MODEL)
    return pl.pallas_call(
        body, name=name, grid=(T // tm,),
        in_specs=[dm(), dm(), dm(), _rows(tm, PLE_DIM), _resident((D_MODEL, D_MODEL)), _resident((PLE_DIM, D_MODEL)),
                  _full((1, D_MODEL))],
        out_specs=[dm(), dm(), _full((1, D_MODEL)), _full((D_MODEL, D_MODEL)), _full((PLE_DIM, D_MODEL))],
        out_shape=[_S((T, D_MODEL)), _S((T, D_MODEL), bf16), _S((1, D_MODEL)), _S((D_MODEL, D_MODEL)),
                   _S((PLE_DIM, D_MODEL))],
        compiler_params=_cp("arbitrary"),
    )(gx2, mix, x, p, wpg, wple, gpost)


def gate_bwd(gmix, zl, zs, gsl, wout, name):
    T = zl.shape[0]
    tm = _tile(T, TOKEN_TILE_LIGHT)

    def body(gmix_ref, zl_ref, zs_ref, gs_ref, gl_ref, wout_ref, gzs_ref, gzl_ref, ggsl_ref, gwout_ref):
        i = pl.program_id(0)
        gmix = gmix_ref[...]
        gmerged = _dot_nt(gmix, wout_ref[...])
        zs, zl = zs_ref[...], zl_ref[...]
        ss, sl = _sig(gs_ref[...]), _sig(gl_ref[...])
        _accumulate(gwout_ref, _dot_tn((ss * zs + sl * zl).astype(bf16), gmix), i)
        gzs_ref[...] = (gmerged * ss).astype(bf16)
        gzl_ref[...] = (gmerged * sl).astype(bf16)
        ggsl_ref[:, :D_MODEL] = (gmerged * zs * ss * (1.0 - ss)).astype(bf16)
        ggsl_ref[:, D_MODEL:] = (gmerged * zl * sl * (1.0 - sl)).astype(bf16)

    dm = lambda: _rows(tm, D_MODEL)
    return pl.pallas_call(
        body, name=name, grid=(T // tm,),
        in_specs=[dm(), dm(), dm(), _rows(tm, D_MODEL, 0), _rows(tm, D_MODEL, 1), _resident((D_MODEL, D_MODEL))],
        out_specs=[dm(), dm(), _rows(tm, 2 * D_MODEL), _full((D_MODEL, D_MODEL))],
        out_shape=[_S((T, D_MODEL), bf16), _S((T, D_MODEL), bf16), _S((T, 2 * D_MODEL), bf16), _S((D_MODEL, D_MODEL))],
        compiler_params=_cp("arbitrary"),
    )(gmix, zl, zs, gsl, gsl, wout)


def lru_out_bwd(gzl, h, lg, wbl, name):
    T = h.shape[0]
    tm = _tile(T, TOKEN_TILE_LIGHT)

    def body(gzl_ref, h_ref, lg_ref, wbl_ref, gh_ref, glg_ref, gwbl_ref):
        i = pl.program_id(0)
        gzl = gzl_ref[...]
        lg_v, hv = lg_ref[...], h_ref[...]
        slg = _sig(lg_v)
        silu = lg_v * slg
        _accumulate(gwbl_ref, _dot_tn((hv * silu).astype(bf16), gzl), i)
        gyl = _dot_nt(gzl, wbl_ref[...])
        gh_ref[...] = gyl * silu
        glg_ref[...] = (gyl * hv * slg * (1.0 + lg_v * (1.0 - slg))).astype(bf16)

    lw = lambda: _rows(tm, LRU_WIDTH)
    return pl.pallas_call(
        body, name=name, grid=(T // tm,),
        in_specs=[_rows(tm, D_MODEL), lw(), lw(), _resident((LRU_WIDTH, D_MODEL))],
        out_specs=[lw(), lw(), _full((LRU_WIDTH, D_MODEL))],
        out_shape=[_S((T, LRU_WIDTH)), _S((T, LRU_WIDTH), bf16), _S((LRU_WIDTH, D_MODEL))],
        compiler_params=_cp("arbitrary"),
    )(gzl, h, lg, wbl)


def in_proj_bwd(pieces, win, x, gres, g, name):
    T = x.shape[0]
    tm = _tile(T, MM_TILE_M // 2)
    widths = [pc.shape[1] for pc in pieces]
    offs = [sum(widths[:k]) for k in range(len(widths))]

    def body(*refs):
        pc_refs = refs[:len(widths)]
        w_ref, x_ref, gres_ref, g_ref, gx_ref, gg_ref = refs[len(widths):]
        i = pl.program_id(0)
        ghv = _dot_nt(pc_refs[0][...], w_ref[:, offs[0]:offs[0] + widths[0]])
        for k in range(1, len(widths)):
            ghv = ghv + _dot_nt(pc_refs[k][...], w_ref[:, offs[k]:offs[k] + widths[k]])
        xv = x_ref[...]
        r = lax.rsqrt(jnp.mean(xv * xv, axis=-1, keepdims=True) + NORM_EPS)
        nrm = xv * r
        gy = ghv * g_ref[...]
        gx_ref[...] = gres_ref[...] + r * (gy - nrm * jnp.mean(gy * nrm, axis=-1, keepdims=True))
        _accumulate(gg_ref, jnp.sum(ghv * nrm, axis=0, keepdims=True), i)

    dm = lambda: _rows(tm, D_MODEL)
    return pl.pallas_call(
        body, name=name, grid=(T // tm,),
        in_specs=[_rows(tm, wd) for wd in widths] + [_resident(win.shape), dm(), dm(), _full((1, D_MODEL))],
        out_specs=[dm(), _full((1, D_MODEL))],
        out_shape=[_S((T, D_MODEL)), _S((1, D_MODEL))], compiler_params=_cp("arbitrary"),
    )(*pieces, win, x, gres, g)


def _s5_operands(w, m, tag):
    b_re_t = jnp.transpose(w['s5_b_re'], (2, 0, 1))
    b_im_t = jnp.transpose(w['s5_b_im'], (2, 0, 1))
    ldt = w['s5_log_dt'][:, None]
    ab, pw, bb = s5_prep(w['s5_a_re'], w['s5_a_im'], ldt, b_re_t, b_im_t, m, "s5_prep" + tag)
    over_sublanes = lambda t: jnp.broadcast_to(t[..., None, :], t.shape[:-1] + (SEGS, S5_NS))
    ptab = pw.reshape(2, m, S5_NS)
    return dict(abar_b=over_sublanes(ab.reshape(2, S5_NS)), ptab_b=over_sublanes(ptab),
                ptab_rev_b=over_sublanes(ptab[:, ::-1, :]), bdb=_pack_bdb(bb).astype(bf16),
                cdb=_pack_cdb(w['s5_c_re'], w['s5_c_im']).astype(bf16), dvec=w['s5_d'][None, :],
                prep_in=(w['s5_a_re'], w['s5_a_im'], ldt, b_re_t, b_im_t))


def layer_fwd(x, p, w, tag, gather=(), target=None):
    T = x.shape[0]
    m = _tile(T, SCAN_CHUNK) // SEGS
    s5 = _s5_operands(w, m, tag)
    h_bf, usg, lx, lg, gsl = in_proj_fwd(x, w['g_pre'][None, :], w['w_in'], "in_proj_fwd" + tag)
    (ys, s_re, s_im, s_bf), gathered = s5_fwd(usg, s5['bdb'], s5['cdb'], s5['dvec'], s5['abar_b'], s5['ptab_b'],
                                              "s5_fwd" + tag, gather)
    glu, zs = s5_post_fwd(ys, usg, w['w_glu'], w['w_bs'], "s5_post_fwd" + tag)
    wa, wx = w['lru_w_a'].astype(bf16), w['lru_w_x'].astype(bf16)
    c, r, gi, om, hs = lru_fwd(lx, w['conv_w'], w['conv_b'][None, :], wa, wx, w['lru_b_a'][None, :],
                               w['lru_b_x'][None, :], w['lru_lambda'][None, :], "lru_fwd" + tag)
    zl, mix, x_out, *loss_part = merge_fwd(hs, lg, zs, gsl, x, p, w['w_bl'], w['w_out'], w['g_post'][None, :],
                                           w['w_ple'], w['w_ple_gate'], "merge_fwd" + tag, target)
    saved = dict(x=x, p=p, h_bf=h_bf, usg=usg, lx=lx, lg=lg, gsl=gsl, ys=ys, s_re=s_re, s_im=s_im, s_bf=s_bf, glu=glu,
                 zs=zs, c=c, r=r, gi=gi, om=om, hs=hs, zl=zl, mix=mix, s5=s5, wa=wa, wx=wx)
    return x_out, saved, gathered, (loss_part[0] if loss_part else None)


def layer_bwd(gx_out, w, sv, tag, exchange=()):
    s5 = sv['s5']
    g = {}
    gres, gmix, g_gpost, g['w_ple_gate'], g['w_ple'] = post_bwd(
        gx_out, sv['mix'], sv['x'], sv['p'], w['w_ple_gate'], w['w_ple'], w['g_post'][None, :], "post_bwd" + tag)
    gzs, gzl, ggsl, g['w_out'] = gate_bwd(gmix, sv['zl'], sv['zs'], sv['gsl'], w['w_out'], "gate_bwd" + tag)
    g_h, g_lg, g['w_bl'] = lru_out_bwd(gzl, sv['hs'], sv['lg'], w['w_bl'], "lru_out_bwd" + tag)
    g['g_post'] = g_gpost[0]
    (g_lx, g_wa, g_wx, g_ba, g_bx, g_lam, g_cb, g_cw), exchanged = lru_bwd(
        g_h, sv['hs'], sv['c'], sv['r'], sv['gi'], sv['om'], sv['lx'], w['conv_w'], sv['wa'], sv['wx'],
        w['lru_lambda'][None, :], "lru_bwd" + tag, exchange)
    g['lru_w_a'], g['lru_w_x'] = g_wa, g_wx
    g['lru_b_a'], g['lru_b_x'], g['lru_lambda'], g['conv_b'], g['conv_w'] = g_ba[0], g_bx[0], g_lam[0], g_cb[0], g_cw
    g_ys, g_sg, g['w_bs'], g['w_glu'] = s5_post_bwd(gzs, sv['glu'], sv['usg'], sv['ys'], w['w_bs'], w['w_glu'],
                                                    "s5_post_bwd" + tag)
    g_u, g_ab, g_d, g_bdb, g_cdb = s5_bwd(g_ys, sv['usg'], sv['s_re'], sv['s_im'], sv['s_bf'], s5['bdb'], s5['cdb'],
                                          s5['dvec'], s5['abar_b'], s5['ptab_rev_b'], "s5_bwd" + tag)
    g['s5_d'] = g_d[0]
    g['s5_c_re'], g['s5_c_im'] = _unpack_cdb(g_cdb)
    g_are, g_aim, g_ldt, g_bre_t, g_bim_t = s5_prep_bwd(*s5['prep_in'], g_ab.reshape(2, S5_GROUPS, S5_STATE),
                                                       _unpack_bdb(g_bdb), "s5_prep_bwd" + tag)
    g['s5_a_re'], g['s5_a_im'], g['s5_log_dt'] = g_are, g_aim, g_ldt
    g['s5_b_re'] = jnp.transpose(g_bre_t, (1, 2, 0))
    g['s5_b_im'] = jnp.transpose(g_bim_t, (1, 2, 0))
    pieces = [g_u, g_sg, g_lx, g_lg, ggsl]
    g['w_in'] = jnp.concatenate(mm_tn(sv['h_bf'], pieces[:3], "gw_in_a" + tag) + mm_tn(sv['h_bf'], pieces[3:], "gw_in_b" + tag),
                                axis=1)
    gx, g_gpre = in_proj_bwd(pieces, w['w_in'], sv['x'], gres, w['g_pre'][None, :], "in_proj_bwd" + tag)
    g['g_pre'] = g_gpre[0]
    return gx, g, exchanged


def _as_2d(a):
    return a.reshape((-1, a.shape[-1])) if a.ndim > 1 else a.reshape((1, -1))


def _adamw_update(w, gv, m, v):
    nm = ADAM_B1 * m + (1.0 - ADAM_B1) * gv
    nv = ADAM_B2 * v + (1.0 - ADAM_B2) * (gv * gv)
    bc1 = 1.0 - ADAM_B1 ** ADAM_STEP
    bc2 = 1.0 - ADAM_B2 ** ADAM_STEP
    return -ADAM_LR * ((nm / bc1) / (jnp.sqrt(nv / bc2) + ADAM_EPS) + ADAM_WD * w), nm, nv


def adamw(w, g, m, v, name):
    shape = w.shape
    w2, g2, m2, v2 = _as_2d(w), _as_2d(g), _as_2d(m), _as_2d(v)
    R, C = w2.shape
    tr = _row_tile(R, C)

    def body(w_ref, g_ref, m_ref, v_ref, d_ref, nm_ref, nv_ref):
        d_ref[...], nm_ref[...], nv_ref[...] = _adamw_update(w_ref[...], g_ref[...], m_ref[...], v_ref[...])

    spec = lambda: pl.BlockSpec((tr, C), lambda i: (i, 0))
    d, nm, nv = pl.pallas_call(
        body, name=name, grid=(R // tr,), in_specs=[spec() for _ in range(4)], out_specs=[spec() for _ in range(3)],
        out_shape=[_S((R, C))] * 3, compiler_params=_cp("parallel"),
    )(w2, g2, m2, v2)
    return d.reshape(shape), nm.reshape(shape), nv.reshape(shape)


def adamw_reduce(w, parts, theirs, m, v, chip, name):
    shape = w.shape
    C = shape[-1]
    R = math.prod(shape[1:-1])
    w3, m3, v3 = w.reshape(DEPTH, R, C), m.reshape(DEPTH, R, C), v.reshape(DEPTH, R, C)
    tr = _row_tile(R, C)

    def body(chip_ref, w_ref, *refs):
        layer_refs, (m_ref, v_ref, g_ref, d_ref, nm_ref, nv_ref) = refs[:2 * DEPTH], refs[2 * DEPTH:]
        layer = pl.program_id(0)
        for l in range(DEPTH):
            @pl.when(layer == l)
            def _():
                a_ref, t_ref = layer_refs[2 * l], layer_refs[2 * l + 1]
                gv = ((a_ref[0] + t_ref[0].astype(f32)) + t_ref[1].astype(f32)) + t_ref[2].astype(f32)
                g_ref[0] = gv
                d_ref[0], nm_ref[0], nv_ref[0] = _adamw_update(w_ref[0], gv, m_ref[0], v_ref[0])

    spec = lambda: pl.BlockSpec((1, tr, C), lambda l, i, c: (l, i, 0))
    rows_of = lambda l: (lambda ll, i, c: jnp.where(ll == l, i, 0))
    layer_specs = []
    for l in range(DEPTH):
        layer_specs.append(pl.BlockSpec((1, tr, C), lambda ll, i, c, r=rows_of(l): (c[0], r(ll, i, c), 0)))
        layer_specs.append(pl.BlockSpec((3, tr, C), lambda ll, i, c, r=rows_of(l): (0, r(ll, i, c), 0)))
    grid_spec = pltpu.PrefetchScalarGridSpec(
        num_scalar_prefetch=1, grid=(DEPTH, R // tr),
        in_specs=[spec()] + layer_specs + [spec(), spec()], out_specs=[spec() for _ in range(4)])
    operands = [x.reshape(x.shape[0], R, C) for l in range(DEPTH) for x in (parts[l], theirs[l])]
    g, d, nm, nv = pl.pallas_call(
        body, name=name, grid_spec=grid_spec, out_shape=[_S((DEPTH, R, C))] * 4,
        compiler_params=_cp("arbitrary", "arbitrary"),
    )(chip, w3, *operands, m3, v3)
    return g.reshape(shape), d.reshape(shape), nm.reshape(shape), nv.reshape(shape)


MESH = pl.DeviceIdType.MESH
ANY = pl.BlockSpec(memory_space=pl.ANY)


def _place():
    return lax.axis_index("x"), lax.axis_index("y"), lax.axis_index("c")


def _other_chips(mx, my):
    return [(1 - mx, my), (mx, 1 - my), (1 - mx, 1 - my)]


def all_gather(shards, name):
    nb = len(shards)

    def body(*refs):
        phases = _gather_phases([s.shape for s in shards], refs[:nb], refs[nb:2 * nb], *refs[2 * nb:])
        for phase in phases:
            phase()

    outs = pl.pallas_call(
        body, name=name, out_shape=_gather_out_shapes(shards), in_specs=[ANY] * nb, out_specs=[ANY] * nb,
        scratch_shapes=_gather_semaphores(nb),
    )(*shards)
    return list(outs)


GATHER_COPIES = 9
OWN_SIB, OWN_X, OWN_Y, X_SIB, Y_SIB, RELAY_X, RELAY_Y, DIAG0_SIB, DIAG1_SIB = range(GATHER_COPIES)


def _gather_out_shapes(shards):
    return [_S((N_DEV,) + s.shape, s.dtype) for s in shards]


def _gather_semaphores(nb):
    return [pltpu.SemaphoreType.DMA((nb, GATHER_COPIES)), pltpu.SemaphoreType.DMA((nb, GATHER_COPIES)),
            pltpu.SemaphoreType.DMA((nb,))]


def _gather_phases(shapes, x_refs, out_refs, send_sems, recv_sems, local_sems):
    nb = len(shapes)
    mx, my, mc = _place()
    sibling, xn, yn = (mx, my, 1 - mc), (1 - mx, my, mc), (mx, 1 - my, mc)

    def block(b, px, py, pc, half=None):
        ref = out_refs[b].at[4 * px + 2 * py + pc]
        if half is None:
            return ref
        n = shapes[b][0] // 2
        return ref.at[pl.ds(half * n, n)]

    def copy(b, k, dst, to, src=None):
        return pltpu.make_async_remote_copy(
            src_ref=dst if src is None else src, dst_ref=dst, send_sem=send_sems.at[b, k],
            recv_sem=recv_sems.at[b, k], device_id=to, device_id_type=MESH)

    def send(b, k):
        if k in (OWN_X, OWN_Y, OWN_SIB):
            return copy(b, k, block(b, mx, my, mc), {OWN_X: xn, OWN_Y: yn, OWN_SIB: sibling}[k], src=x_refs[b])
        what, to = {RELAY_X: ((1 - mx, my, mc, 0), yn), RELAY_Y: ((mx, 1 - my, mc, 1), xn),
                    X_SIB: ((1 - mx, my, mc), sibling), Y_SIB: ((mx, 1 - my, mc), sibling),
                    DIAG0_SIB: ((1 - mx, 1 - my, mc, 0), sibling), DIAG1_SIB: ((1 - mx, 1 - my, mc, 1), sibling)}[k]
        return copy(b, k, block(b, *what), to)

    def local(b):
        return pltpu.make_async_copy(x_refs[b], block(b, mx, my, mc), local_sems.at[b])

    def send_own():
        for k in (OWN_X, OWN_Y, OWN_SIB):
            for b in range(nb):
                send(b, k).start()
        for b in range(nb):
            local(b).start()

    def relay_neighbours():
        for b in range(nb):
            copy(b, OWN_X, block(b, 1 - mx, my, mc), xn).wait_recv()
            send(b, RELAY_X).start()
            send(b, X_SIB).start()
        for b in range(nb):
            copy(b, OWN_Y, block(b, mx, 1 - my, mc), yn).wait_recv()
            send(b, RELAY_Y).start()
            send(b, Y_SIB).start()

    def hand_on_diagonal():
        for b in range(nb):
            copy(b, RELAY_X, block(b, 1 - mx, 1 - my, mc, 0), yn).wait_recv()
            send(b, DIAG0_SIB).start()
            copy(b, RELAY_Y, block(b, 1 - mx, 1 - my, mc, 1), xn).wait_recv()
            send(b, DIAG1_SIB).start()

    def finish():
        for b in range(nb):
            copy(b, OWN_SIB, block(b, mx, my, 1 - mc), sibling).wait_recv()
            copy(b, X_SIB, block(b, 1 - mx, my, 1 - mc), sibling).wait_recv()
            copy(b, Y_SIB, block(b, mx, 1 - my, 1 - mc), sibling).wait_recv()
            copy(b, DIAG0_SIB, block(b, 1 - mx, 1 - my, 1 - mc, 0), sibling).wait_recv()
            copy(b, DIAG1_SIB, block(b, 1 - mx, 1 - my, 1 - mc, 1), sibling).wait_recv()
        for b in range(nb):
            for k in range(GATHER_COPIES):
                send(b, k).wait_send()
            local(b).wait()

    return send_own, relay_neighbours, hand_on_diagonal, finish


def exchange_sibling(gs, name):
    nb = len(gs)

    def body(*refs):
        g_refs, recv_refs, send_sems, recv_sems = refs[:nb], refs[nb:2 * nb], refs[2 * nb], refs[2 * nb + 1]
        mx, my, mc = _place()
        copies = [pltpu.make_async_remote_copy(
            src_ref=g_refs[b].at[2 * k + 1 - mc], dst_ref=recv_refs[b].at[k], send_sem=send_sems.at[b, k],
            recv_sem=recv_sems.at[b, k], device_id=(mx, my, 1 - mc), device_id_type=MESH)
            for b in range(nb) for k in range(4)]
        for cp in copies:
            cp.start()
        for cp in copies:
            cp.wait()

    outs = pl.pallas_call(
        body, name=name, out_shape=[_S((4,) + g.shape[1:], g.dtype) for g in gs], in_specs=[ANY] * nb,
        out_specs=[ANY] * nb,
        scratch_shapes=[pltpu.SemaphoreType.DMA((nb, 4)), pltpu.SemaphoreType.DMA((nb, 4))],
    )(*gs)
    return list(outs)


def exchange_chips(parts, name):
    nb = len(parts)

    def body(*refs):
        start, finish = _chips_phases(refs[:nb], refs[nb:2 * nb], refs[2 * nb], refs[2 * nb + 1])
        start()
        finish()

    outs = pl.pallas_call(
        body, name=name, out_shape=_chips_out_shapes(parts), in_specs=[ANY] * nb, out_specs=[ANY] * nb,
        scratch_shapes=_chips_semaphores(nb),
    )(*parts)
    return list(outs)


def _chips_out_shapes(parts):
    return [_S((3,) + a.shape[1:], a.dtype) for a in parts]


def _chips_semaphores(nb):
    return [pltpu.SemaphoreType.DMA((nb, 3)), pltpu.SemaphoreType.DMA((nb, 3))]


def _chips_phases(a_refs, recv_refs, send_sems, recv_sems):
    mx, my, mc = _place()

    def copies():
        return [pltpu.make_async_remote_copy(
            src_ref=a_refs[b].at[2 * px + py], dst_ref=recv_refs[b].at[j], send_sem=send_sems.at[b, j],
            recv_sem=recv_sems.at[b, j], device_id=(px, py, mc), device_id_type=MESH)
            for b in range(len(a_refs)) for j, (px, py) in enumerate(_other_chips(mx, my))]

    def start():
        for cp in copies():
            cp.start()

    def finish():
        for cp in copies():
            cp.wait()

    return start, finish


def add_sibling(g, theirs, core, name, wire_dtype=f32):
    shp = theirs.shape
    C = shp[-1]
    R = math.prod(shp[1:-1])
    tr = _row_tile(R, C)
    narrow = wire_dtype != f32

    def body(core_ref, g_ref, t_ref, o_ref, *wire_ref):
        s = g_ref[...] + t_ref[...]
        o_ref[...] = s
        if narrow:
            wire_ref[0][...] = s.astype(wire_dtype)

    blk = lambda: pl.BlockSpec((1, tr, C), lambda k, i, c: (k, i, 0))
    grid_spec = pltpu.PrefetchScalarGridSpec(
        num_scalar_prefetch=1, grid=(4, R // tr),
        in_specs=[pl.BlockSpec((1, tr, C), lambda k, i, c: (2 * k + c[0], i, 0)), blk()],
        out_specs=[blk(), blk()] if narrow else [blk()])
    outs = pl.pallas_call(
        body, name=name, grid_spec=grid_spec,
        out_shape=[_S((4, R, C), f32)] + ([_S((4, R, C), wire_dtype)] if narrow else []),
        compiler_params=_cp("parallel", "parallel"),
    )(core, g.reshape(N_DEV, R, C), theirs.reshape(4, R, C))
    part = outs[0].reshape(shp)
    return part, (outs[1].reshape(shp) if narrow else part)


def add_chips(a, theirs, chip, name):
    _, R, C = a.shape
    tr = _row_tile(R, C)

    def body(chip_ref, a_ref, t_ref, out_ref):
        out_ref[...] = ((a_ref[0] + t_ref[0]) + t_ref[1]) + t_ref[2]

    grid_spec = pltpu.PrefetchScalarGridSpec(
        num_scalar_prefetch=1, grid=(R // tr,),
        in_specs=[pl.BlockSpec((1, tr, C), lambda i, c: (c[0], i, 0)), pl.BlockSpec((3, tr, C), lambda i, c: (0, i, 0))],
        out_specs=pl.BlockSpec((tr, C), lambda i, c: (i, 0)))
    return pl.pallas_call(
        body, name=name, grid_spec=grid_spec, out_shape=_S((R, C), a.dtype), compiler_params=_cp("parallel"),
    )(chip, a, theirs)


def _round_up(n, q):
    return (n + q - 1) // q * q


def _lane_rows(a):
    flat = a.reshape(-1)
    n = _round_up(flat.shape[0], SUBLANES * LANES)
    return jnp.pad(flat, (0, n - flat.shape[0])).reshape(-1, LANES)


def _full_to_shards(full, axis):
    shp = full.shape
    s = shp[axis] // N_DEV
    cut = full.reshape(shp[:axis] + (N_DEV, s) + shp[axis + 1:])
    return jnp.moveaxis(cut, axis, 0)


def _shards_to_full(parts, axis):
    shp = list(parts.shape[1:])
    shp[axis] *= N_DEV
    return jnp.moveaxis(parts, 0, axis).reshape(tuple(shp))


def kernel(x, p, g_pre, w_in, s5_a_re, s5_a_im, s5_log_dt, s5_b_re, s5_b_im, s5_c_re, s5_c_im, s5_d, w_glu, w_bs, conv_w, conv_b, lru_w_a, lru_b_a, lru_w_x, lru_b_x, lru_lambda, w_bl, w_out, g_post, w_ple, w_ple_gate, loss_target, m_g_pre, m_w_in, m_s5_a_re, m_s5_a_im, m_s5_log_dt, m_s5_b_re, m_s5_b_im, m_s5_c_re, m_s5_c_im, m_s5_d, m_w_glu, m_w_bs, m_conv_w, m_conv_b, m_lru_w_a, m_lru_b_a, m_lru_w_x, m_lru_b_x, m_lru_lambda, m_w_bl, m_w_out, m_g_post, m_w_ple, m_w_ple_gate, v_g_pre, v_w_in, v_s5_a_re, v_s5_a_im, v_s5_log_dt, v_s5_b_re, v_s5_b_im, v_s5_c_re, v_s5_c_im, v_s5_d, v_w_glu, v_w_bs, v_conv_w, v_conv_b, v_lru_w_a, v_lru_b_a, v_lru_w_x, v_lru_b_x, v_lru_lambda, v_w_bl, v_w_out, v_g_post, v_w_ple, v_w_ple_gate):
    given = dict(locals())
    W = {n: given[n] for n in WEIGHTS}
    M = {n: given["m_" + n] for n in WEIGHTS}
    V = {n: given["v_" + n] for n in WEIGHTS}
    xs, target = to_scan_order(x[0]), to_scan_order(loss_target[0])
    ps = [to_scan_order(p[i, 0]) for i in range(DEPTH)]

    mx, my, mc = _place()
    core = jnp.reshape(mc, (1,)).astype(jnp.int32)
    chip = jnp.reshape(2 * mx + my, (1,)).astype(jnp.int32)

    names = list(SHARDED)
    conv_rows = PAIR - CONV_WIDTH

    def layer_shards(i):
        return [W[n][i].astype(bf16) if n in GATHER_BF16 else jnp.pad(W[n][i], ((0, conv_rows), (0, 0))) for n in names]

    def layer_weights(i, gathered):
        full = {n: _shards_to_full(g if n in GATHER_BF16 else g[:, :CONV_WIDTH], SHARDED[n] - 1)
                for n, g in zip(names, gathered)}
        return {n: (full[n] if n in SHARDED else W[n][i]) for n in WEIGHTS}

    act, saved, weights = xs, [], []
    gathered = all_gather(layer_shards(0), "comm_gather_weights")
    for i in range(DEPTH):
        last = i + 1 == DEPTH
        weights.append(layer_weights(i, gathered))
        act, sv, gathered, loss_part = layer_fwd(act, ps[i], weights[i], "_l%d" % i, () if last else layer_shards(i + 1),
                                                 target if last else None)
        saved.append(sv)
    gact = act
    loss = lax.psum(loss_part[0, 0], ("x", "y", "c"))

    def sibling_sums(i, g):
        rep_rows = [_lane_rows(g[n].reshape(W[n].shape[1:])) for n in REPLICATED]
        n_rows = sum(r.shape[0] for r in rep_rows)
        pad_rows = _round_up(n_rows, N_DEV * SUBLANES) - n_rows
        rep_blocks = jnp.concatenate(rep_rows + [jnp.zeros((pad_rows, LANES), f32)]).reshape(N_DEV, -1, LANES)
        blocks = [_full_to_shards(g[n].reshape(weights[i][n].shape), SHARDED[n] - 1) for n in names] + [rep_blocks]
        theirs = exchange_sibling(blocks, "comm_reduce_sibling_l%d" % i)
        parts, wire = [], []
        for k, (b, t) in enumerate(zip(blocks, theirs)):
            part, sent = add_sibling(b, t, core, "reduce_add_sibling_%d_l%d" % (k, i), bf16 if k < len(names) else f32)
            parts.append(part)
            wire.append(sent)
        return parts, wire, [r.shape[0] for r in rep_rows]

    parts, others, rep_sizes, wire = [None] * DEPTH, [None] * DEPTH, None, ()
    for i in reversed(range(DEPTH)):
        gact, g, exchanged = layer_bwd(gact, weights[i], saved[i], "_l%d" % i, wire)
        if i + 1 < DEPTH:
            others[i + 1] = exchanged
        parts[i], wire, rep_sizes = sibling_sums(i, g)
    others[0] = exchange_chips(wire, "comm_reduce_chips")

    red, deltas, new_m, new_v = {}, {}, {}, {}
    for k, n in enumerate(names):
        red[n], deltas[n], new_m[n], new_v[n] = adamw_reduce(
            W[n], [parts[i][k] for i in range(DEPTH)], [others[i][k] for i in range(DEPTH)], M[n], V[n], chip, "adamw_" + n)
    pieces = [add_chips(parts[i][-1], others[i][-1], chip, "reduce_add_chips_l%d" % i) for i in range(DEPTH)]
    rep_all = [r.reshape(-1, LANES) for r in all_gather(pieces, "comm_gather_replicated")]
    off = 0
    for n, rows in zip(REPLICATED, rep_sizes):
        k = math.prod(W[n].shape[1:])
        red[n] = jnp.stack([rep_all[i][off:off + rows].reshape(-1)[:k] for i in range(DEPTH)]).reshape(W[n].shape)
        off += rows
        deltas[n], new_m[n], new_v[n] = adamw(W[n], red[n], M[n], V[n], "adamw_" + n)
    return (loss, from_scan_order(gact)[None], *[red[n] for n in WEIGHTS], *[deltas[n] for n in WEIGHTS],
            *[new_m[n] for n in WEIGHTS], *[new_v[n] for n in WEIGHTS])
```

```python
import math

import jax
import jax.numpy as jnp
from jax import lax
from jax.experimental import pallas as pl
from jax.experimental.pallas import tpu as pltpu

f32 = jnp.float32
bf16 = jnp.bfloat16

D_MODEL = 1024
DEPTH = 2
PLE_DIM = 256
NORM_EPS = 1e-6
S5_WIDTH = 512
S5_GROUP = 16
S5_GROUPS = 32
S5_STATE = 64
S5_NS = S5_GROUPS * S5_STATE
LRU_WIDTH = 1280
LRU_HEADS = 10
LRU_HEAD_DIM = 128
LRU_C = 8.0
CONV_WIDTH = 4
N_DEV = 8

ADAM_LR = 0.001
ADAM_B1 = 0.9
ADAM_B2 = 0.999
ADAM_EPS = 1e-08
ADAM_WD = 0.01
ADAM_STEP = 10

LANES = 128
SUBLANES = 8
SEGS = SUBLANES
SCAN_CHUNK = 256
TOKEN_TILE = 256
TOKEN_TILE_LIGHT = 512
MM_TILE_M = 1024
PAIR = 2 * SUBLANES
VMEM_LIMIT_BYTES = 56 * 1024 * 1024
ELEMENTWISE_BLOCK_BYTES = 1024 * 1024

WEIGHTS = ['g_pre', 'w_in', 's5_a_re', 's5_a_im', 's5_log_dt', 's5_b_re', 's5_b_im', 's5_c_re', 's5_c_im',
           's5_d', 'w_glu', 'w_bs', 'conv_w', 'conv_b', 'lru_w_a', 'lru_b_a', 'lru_w_x', 'lru_b_x',
           'lru_lambda', 'w_bl', 'w_out', 'g_post', 'w_ple', 'w_ple_gate']
SHARDED = {'w_in': 2, 'w_glu': 2, 'w_bs': 2, 'conv_w': 2, 'w_bl': 1, 'w_out': 1, 'w_ple': 2, 'w_ple_gate': 1}
GATHER_BF16 = ['w_in', 'w_glu', 'w_bs', 'w_bl', 'w_out', 'w_ple', 'w_ple_gate']
REPLICATED = [n for n in WEIGHTS if n not in SHARDED]


def _sig(x):
    return 0.5 * jnp.tanh(0.5 * x) + 0.5


def _gelu_parts(x):
    k = math.sqrt(2.0 / math.pi)
    t = jnp.tanh(k * (x + 0.044715 * x * x * x))
    return t, k


def _gelu(x):
    t, _ = _gelu_parts(x)
    return 0.5 * x * (1.0 + t)


def _gelu_grad(x):
    t, k = _gelu_parts(x)
    return 0.5 * (1.0 + t) + 0.5 * x * (1.0 - t * t) * k * (1.0 + 3.0 * 0.044715 * x * x)


def _one_minus_sq(a, log_a):
    z = 2.0 * log_a
    series = -z * (1.0 + z * (0.5 + z * (1.0 / 6.0 + z * (1.0 / 24.0 + z * (1.0 / 120.0)))))
    return jnp.where(z > -0.05, series, 1.0 - a * a)


def _softplus_neg(lam):
    return jnp.maximum(-lam, 0.0) + jnp.log(1.0 + jnp.exp(-jnp.abs(lam)))


def _dot(a, b):
    return jnp.dot(a, b, preferred_element_type=f32)


def _dot_nt(a, b):
    return lax.dot_general(a, b, (((1,), (1,)), ((), ())), preferred_element_type=f32)


def _dot_tn(a, b):
    return lax.dot_general(a, b, (((0,), (0,)), ((), ())), preferred_element_type=f32)


def _S(shape, dtype=f32):
    return jax.ShapeDtypeStruct(shape, dtype)


def _full(shape):
    nd = len(shape)
    return pl.BlockSpec(shape, lambda *_: (0,) * nd)


def _rows(tile, width, col=0):
    return pl.BlockSpec((tile, width), lambda i: (i, col))


def _cp(*semantics):
    return pltpu.CompilerParams(dimension_semantics=semantics or None, vmem_limit_bytes=VMEM_LIMIT_BYTES)


def _tile(n, want):
    t = min(n, want)
    assert n % t == 0, (n, want)
    return t


def _row_tile(R, C=LANES):
    cap = max(SUBLANES, min(R, ELEMENTWISE_BLOCK_BYTES // (4 * C)))
    for t in range(cap - cap % SUBLANES, 0, -SUBLANES):
        if R % t == 0:
            return t
    return R


def _lanes(j):
    return slice(LANES * j, LANES * (j + 1))


def _step_rows(k, n=SUBLANES):
    return pl.ds(pl.multiple_of(k * n, n), n)


def to_scan_order(a):
    T, C = a.shape
    tc = _tile(T, SCAN_CHUNK)
    return a.reshape(T // tc, SEGS, tc // SEGS, C).transpose(0, 2, 1, 3).reshape(T, C)


def from_scan_order(a):
    T, C = a.shape
    tc = _tile(T, SCAN_CHUNK)
    return a.reshape(T // tc, tc // SEGS, SEGS, C).transpose(0, 2, 1, 3).reshape(T, C)


def _resident(shape):
    nd = len(shape)
    return pl.BlockSpec(shape, lambda *_: (0,) * nd, pipeline_mode=pl.Buffered(1))


def mm_tn(a, bs, name):
    M, K = a.shape
    tm = _tile(M, MM_TILE_M)
    nb = len(bs)

    def body(a_ref, *refs):
        m = pl.program_id(0)
        av = a_ref[...]
        for b_ref, o_ref in zip(refs[:nb], refs[nb:]):
            _accumulate(o_ref, _dot_tn(av, b_ref[...]), m)

    outs = pl.pallas_call(
        body, name=name, grid=(M // tm,),
        in_specs=[_rows(tm, K)] + [_rows(tm, b.shape[1]) for b in bs],
        out_specs=[_full((K, b.shape[1])) for b in bs],
        out_shape=[_S((K, b.shape[1])) for b in bs],
        compiler_params=_cp("arbitrary"),
    )(a, *bs)
    return list(outs)


IN_PROJ_WIDTHS = (2 * S5_WIDTH, LRU_WIDTH, LRU_WIDTH, 2 * D_MODEL)


def in_proj_fwd(x, g, win, name):
    T = x.shape[0]
    tm = _tile(T, MM_TILE_M // 2)
    offs = [sum(IN_PROJ_WIDTHS[:k]) for k in range(len(IN_PROJ_WIDTHS))]

    def body(x_ref, g_ref, w_ref, h_ref, *out_refs):
        xv = x_ref[...]
        r = lax.rsqrt(jnp.mean(xv * xv, axis=-1, keepdims=True) + NORM_EPS)
        h = (xv * r * g_ref[...]).astype(bf16)
        h_ref[...] = h
        for o_ref, off, wd in zip(out_refs, offs, IN_PROJ_WIDTHS):
            o_ref[...] = _dot(h, w_ref[:, off:off + wd])

    return pl.pallas_call(
        body, name=name, grid=(T // tm,),
        in_specs=[_rows(tm, D_MODEL), _full((1, D_MODEL)), _resident(win.shape)],
        out_specs=[_rows(tm, D_MODEL)] + [_rows(tm, wd) for wd in IN_PROJ_WIDTHS],
        out_shape=[_S((T, D_MODEL), bf16)] + [_S((T, wd)) for wd in IN_PROJ_WIDTHS],
        compiler_params=_cp("parallel"),
    )(x, g, win)


def _s5_discretise(a_re, a_im, log_dt, b_re_t, b_im_t):
    dt = jnp.exp(log_dt)
    mag = jnp.exp(a_re * dt)
    ab_re = mag * jnp.cos(a_im * dt)
    ab_im = mag * jnp.sin(a_im * dt)
    den = a_re * a_re + a_im * a_im
    nr, ni = ab_re - 1.0, ab_im
    z_re = (nr * a_re + ni * a_im) / den
    z_im = (ni * a_re - nr * a_im) / den
    bb_re = z_re[None] * b_re_t - z_im[None] * b_im_t
    bb_im = z_re[None] * b_im_t + z_im[None] * b_re_t
    return ab_re, ab_im, bb_re, bb_im


def s5_prep(a_re, a_im, log_dt, b_re_t, b_im_t, m, name):
    G, N = a_re.shape

    def body(are_ref, aim_ref, ldt_ref, bre_ref, bim_ref, ab_ref, pw_ref, bb_ref):
        are, aim, ldt = are_ref[...], aim_ref[...], ldt_ref[...]
        ab_re, ab_im, bb_re, bb_im = _s5_discretise(are, aim, ldt, bre_ref[...], bim_ref[...])
        ab_ref[0], ab_ref[1] = ab_re, ab_im
        bb_ref[0], bb_ref[1] = bb_re, bb_im
        dt = jnp.exp(ldt)
        for k in range(m):
            mag = jnp.exp(are * dt * (k + 1.0))
            pw_ref[0, k] = mag * jnp.cos(aim * dt * (k + 1.0))
            pw_ref[1, k] = mag * jnp.sin(aim * dt * (k + 1.0))

    return pl.pallas_call(
        body, name=name,
        out_shape=[_S((2, G, N)), _S((2, m, G, N)), _S((2, S5_GROUP, G, N))], compiler_params=_cp(),
    )(a_re, a_im, log_dt, b_re_t, b_im_t)


def s5_prep_bwd(a_re, a_im, log_dt, b_re_t, b_im_t, g_ab, g_bb, name):
    G, N = a_re.shape

    def body(are_ref, aim_ref, ldt_ref, bre_ref, bim_ref, gab_ref, gbb_ref, o_are, o_aim, o_ldt, o_bre, o_bim):
        _, vjp = jax.vjp(_s5_discretise, are_ref[...], aim_ref[...], ldt_ref[...], bre_ref[...], bim_ref[...])
        g_are, g_aim, g_ldt, g_bre, g_bim = vjp((gab_ref[0], gab_ref[1], gbb_ref[0], gbb_ref[1]))
        o_are[...], o_aim[...], o_ldt[...], o_bre[...], o_bim[...] = g_are, g_aim, g_ldt, g_bre, g_bim

    return pl.pallas_call(
        body, name=name,
        out_shape=[_S((G, N)), _S((G, N)), _S((G, 1)), _S((S5_GROUP, G, N)), _S((S5_GROUP, G, N))],
        compiler_params=_cp(),
    )(a_re, a_im, log_dt, b_re_t, b_im_t, g_ab, g_bb)


NB_S5 = S5_NS // LANES
CB_S5 = S5_WIDTH // LANES
SB_PER_CB = NB_S5 // CB_S5
GRP_PER_SB = LANES // S5_STATE
S5_JB = 8


def _bdb_mask():
    j = jnp.arange(NB_S5)
    own_rows = (j[:, None] % SB_PER_CB == jnp.arange(SB_PER_CB)[None, :]).astype(f32)
    eye = jnp.eye(GRP_PER_SB, dtype=f32)
    return own_rows[:, :, None, None, None, None, None] * eye[None, None, :, None, None, :, None]


def _pack_bdb(bb):
    v = jnp.transpose(bb.reshape(2, S5_GROUP, NB_S5, GRP_PER_SB, S5_STATE), (2, 3, 1, 0, 4))
    full = v[:, None, :, :, :, None, :] * _bdb_mask()
    return full.reshape(NB_S5, LANES, 2 * LANES)


def _unpack_bdb(g_bdb):
    g7 = g_bdb.reshape(NB_S5, SB_PER_CB, GRP_PER_SB, S5_GROUP, 2, GRP_PER_SB, S5_STATE)
    v = jnp.sum(g7 * _bdb_mask(), axis=(1, 5))
    return jnp.transpose(v, (3, 2, 0, 1, 4)).reshape(2, S5_GROUP, S5_GROUPS, S5_STATE)


def _pack_cdb(c_re, c_im):
    gl = S5_GROUPS // CB_S5
    c2 = jnp.stack([c_re, -c_im]).reshape(2, CB_S5, gl, S5_GROUP, S5_STATE)
    eye = jnp.eye(gl, dtype=f32)
    full = jnp.transpose(c2, (1, 0, 2, 4, 3))[:, :, :, :, None, :] * eye[None, None, :, None, :, None]
    return full.reshape(CB_S5, 2 * SB_PER_CB * LANES, LANES)


def _unpack_cdb(g_cdb):
    gl = S5_GROUPS // CB_S5
    g6 = g_cdb.reshape(CB_S5, 2, gl, S5_STATE, gl, S5_GROUP)
    eye = jnp.eye(gl, dtype=f32)
    v = jnp.sum(g6 * eye[None, None, :, None, :, None], axis=4)
    v = jnp.transpose(v, (1, 0, 2, 4, 3)).reshape(2, S5_GROUPS, S5_GROUP, S5_STATE)
    return v[0], -v[1]


def _state_cat(ref, c):
    w = SB_PER_CB * LANES
    return jnp.concatenate([ref[:, w * c:w * (c + 1)], ref[:, S5_NS + w * c:S5_NS + w * (c + 1)]], axis=1)


def _state_pair(ref, j):
    return jnp.concatenate([ref[:, _lanes(j)], ref[:, S5_NS + LANES * j:S5_NS + LANES * (j + 1)]], axis=1)


def s5_fwd(usg, bdb, cdb, dvec, abar_b, ptab_b, name, gather=()):
    T = usg.shape[0]
    tc = _tile(T, SCAN_CHUNK)
    m = tc // SEGS
    nsteps = T // tc
    ng = len(gather)
    assert ptab_b.shape == (2, m, SEGS, S5_NS) and m % 2 == 0

    def body(*refs):
        u_ref, bdb_ref, cdb_ref, d_ref, a_ref, p_ref = refs[:6]
        ys_ref, sre_ref, sim_ref, sbf_ref = refs[6 + ng:10 + ng]
        src_re, src_im, dst_re, dst_im, cin_ref, carry_ref = refs[10 + 2 * ng:16 + 2 * ng]
        i = pl.program_id(0)
        if ng:
            phases = _gather_phases([s.shape for s in gather], refs[6:6 + ng], refs[10 + ng:10 + 2 * ng],
                                    *refs[16 + 2 * ng:])
            for phase, step in zip(phases, (0, nsteps // 2, (3 * nsteps) // 4, nsteps - 1)):
                pl.when(i == step)(phase)

        @pl.when(i == 0)
        def _():
            carry_ref[...] = jnp.zeros_like(carry_ref)

        u = u_ref[...]
        ub = u.astype(bf16)
        for j in range(NB_S5):
            bu = _dot(ub[:, _lanes(j // SB_PER_CB)], bdb_ref[j])
            src_re[:, _lanes(j)] = bu[:, :LANES]
            src_im[:, _lanes(j)] = bu[:, LANES:]
        for j0 in range(0, NB_S5, S5_JB):
            def kstep(k, st):
                rows = _step_rows(k)
                out = []
                for q in range(S5_JB):
                    ln = _lanes(j0 + q)
                    sr, si = st[2 * q], st[2 * q + 1]
                    ar, ai = a_ref[0, :, ln], a_ref[1, :, ln]
                    nr = ar * sr - ai * si + src_re[rows, ln]
                    ni = ar * si + ai * sr + src_im[rows, ln]
                    dst_re[rows, ln] = nr
                    dst_im[rows, ln] = ni
                    out += [nr, ni]
                return tuple(out)

            ends = lax.fori_loop(0, m, kstep, tuple(jnp.zeros((SEGS, LANES), f32) for _ in range(2 * S5_JB)))
            for q in range(S5_JB):
                ln = _lanes(j0 + q)
                er, ei = ends[2 * q], ends[2 * q + 1]
                cr, ci = carry_ref[0, :, ln], carry_ref[1, :, ln]
                amr, ami = p_ref[0, m - 1, 0:1, ln], p_ref[1, m - 1, 0:1, ln]
                rows_r, rows_i = [], []
                for s in range(SEGS):
                    rows_r.append(cr)
                    rows_i.append(ci)
                    cr, ci = (er[s:s + 1, :] + amr * cr - ami * ci, ei[s:s + 1, :] + amr * ci + ami * cr)
                cin_ref[0, 0:SEGS, ln] = _stack_rows(rows_r)
                cin_ref[1, 0:SEGS, ln] = _stack_rows(rows_i)
                carry_ref[0, :, ln] = cr
                carry_ref[1, :, ln] = ci
        cin_ref[:, SEGS:, :] = cin_ref[:, 0:SEGS, :]

        def fix(k2, _):
            rows = _step_rows(k2, PAIR)
            pr = p_ref[0, pl.ds(2 * k2, 2)].reshape(PAIR, S5_NS)
            pi = p_ref[1, pl.ds(2 * k2, 2)].reshape(PAIR, S5_NS)
            cr, ci = cin_ref[0], cin_ref[1]
            sr = dst_re[rows, :] + pr * cr - pi * ci
            si = dst_im[rows, :] + pr * ci + pi * cr
            sre_ref[rows, :] = sr
            sim_ref[rows, :] = si
            sbf_ref[rows, 0:S5_NS] = sr.astype(bf16)
            sbf_ref[rows, S5_NS:] = si.astype(bf16)
            return 0

        lax.fori_loop(0, m // 2, fix, 0)
        for c in range(CB_S5):
            ys_ref[:, _lanes(c)] = _dot(_state_cat(sbf_ref, c), cdb_ref[c]) + d_ref[:, _lanes(c)] * u[:, _lanes(c)]

    st = lambda w: _rows(tc, w)
    outs = pl.pallas_call(
        body, name=name, grid=(nsteps,),
        in_specs=[_rows(tc, S5_WIDTH, 0), _resident(bdb.shape), _resident(cdb.shape), _full((1, S5_WIDTH)),
                  _resident((2, SEGS, S5_NS)), _resident((2, m, SEGS, S5_NS))] + [ANY] * ng,
        out_specs=[st(S5_WIDTH), st(S5_NS), st(S5_NS), st(2 * S5_NS)] + [ANY] * ng,
        out_shape=[_S((T, S5_WIDTH)), _S((T, S5_NS)), _S((T, S5_NS)), _S((T, 2 * S5_NS), bf16)] + (
            _gather_out_shapes(gather) if ng else []),
        scratch_shapes=[pltpu.VMEM((tc, S5_NS), f32)] * 4 + [pltpu.VMEM((2, PAIR, S5_NS), f32),
                                                             pltpu.VMEM((2, 1, S5_NS), f32)] + (
            _gather_semaphores(ng) if ng else []),
        compiler_params=_cp("arbitrary"),
    )(usg, bdb, cdb, dvec, abar_b, ptab_b, *gather)
    return outs[:4], list(outs[4:])


def s5_bwd(gys, usg, s_re, s_im, s_bf, bdb, cdb, dvec, abar_b, ptab_rev_b, name):
    T = gys.shape[0]
    tc = _tile(T, SCAN_CHUNK)
    m = tc // SEGS
    nch = T // tc
    hb = tc // SUBLANES

    def body(gy_ref, u_ref, sre_ref, sim_ref, hre_ref, him_ref, sbf_ref, bdb_ref, cdb_ref, d_ref, a_ref, p_ref,
             gu_ref, gab_ref, gd_ref, gbdb_ref, gcdb_ref,
             src_re, src_im, dst_re, dst_im, lam_ref, cin_ref, acc_ref, carry_ref):
        i = pl.program_id(0)

        @pl.when(i == 0)
        def _():
            carry_ref[...] = jnp.zeros_like(carry_ref)
            for ref in (gab_ref, gd_ref, gbdb_ref, gcdb_ref):
                ref[...] = jnp.zeros_like(ref)

        first = i == nch - 1
        gy = gy_ref[...]
        gyb = gy.astype(bf16)
        u = u_ref[...]
        ub = u.astype(bf16)
        w = SB_PER_CB * LANES
        for c in range(CB_S5):
            gs = _dot_nt(gyb[:, _lanes(c)], cdb_ref[c])
            src_re[:, w * c:w * (c + 1)] = gs[:, :w]
            src_im[:, w * c:w * (c + 1)] = gs[:, w:]
            gcdb_ref[c] += _dot_tn(_state_cat(sbf_ref, c), gyb[:, _lanes(c)])
        for j0 in range(0, NB_S5, S5_JB):
            def kstep(kk, st):
                rows = _step_rows(m - 1 - kk)
                out = []
                for q in range(S5_JB):
                    ln = _lanes(j0 + q)
                    lr, li = st[2 * q], st[2 * q + 1]
                    ar, ai = a_ref[0, :, ln], a_ref[1, :, ln]
                    nr = ar * lr + ai * li + src_re[rows, ln]
                    ni = ar * li - ai * lr + src_im[rows, ln]
                    dst_re[rows, ln] = nr
                    dst_im[rows, ln] = ni
                    out += [nr, ni]
                return tuple(out)

            ends = lax.fori_loop(0, m, kstep, tuple(jnp.zeros((SEGS, LANES), f32) for _ in range(2 * S5_JB)))
            for q in range(S5_JB):
                ln = _lanes(j0 + q)
                er, ei = ends[2 * q], ends[2 * q + 1]
                cr, ci = carry_ref[0, :, ln], carry_ref[1, :, ln]
                amr, ami = p_ref[0, 0, 0:1, ln], p_ref[1, 0, 0:1, ln]
                rows_r, rows_i = [None] * SEGS, [None] * SEGS
                for s in reversed(range(SEGS)):
                    rows_r[s], rows_i[s] = cr, ci
                    cr, ci = (er[s:s + 1, :] + amr * cr + ami * ci, ei[s:s + 1, :] + amr * ci - ami * cr)
                cin_ref[0, 0:SEGS, ln] = _stack_rows(rows_r)
                cin_ref[1, 0:SEGS, ln] = _stack_rows(rows_i)
                carry_ref[0, :, ln] = cr
                carry_ref[1, :, ln] = ci
        cin_ref[:, SEGS:, :] = cin_ref[:, 0:SEGS, :]
        acc_ref[...] = jnp.zeros_like(acc_ref)

        def fix_rows(rows, k2, prev_re, prev_im):
            pr = p_ref[0, pl.ds(2 * k2, 2)].reshape(PAIR, S5_NS)
            pi = p_ref[1, pl.ds(2 * k2, 2)].reshape(PAIR, S5_NS)
            cr, ci = cin_ref[0], cin_ref[1]
            lr = dst_re[rows, :] + pr * cr + pi * ci
            li = dst_im[rows, :] + pr * ci - pi * cr
            lam_ref[rows, 0:S5_NS] = lr.astype(bf16)
            lam_ref[rows, S5_NS:] = li.astype(bf16)
            acc_ref[0] += lr * prev_re + li * prev_im
            acc_ref[1] += li * prev_re - lr * prev_im

        last = slice(tc - SUBLANES, tc)
        wrap_re = _down_a_segment(sre_ref[last, :], jnp.where(first, 0.0, hre_ref[SUBLANES - 1:SUBLANES, :]))
        wrap_im = _down_a_segment(sim_ref[last, :], jnp.where(first, 0.0, him_ref[SUBLANES - 1:SUBLANES, :]))
        fix_rows(pl.ds(0, PAIR), 0, jnp.concatenate([wrap_re, sre_ref[0:SUBLANES, :]], axis=0),
                 jnp.concatenate([wrap_im, sim_ref[0:SUBLANES, :]], axis=0))

        def fix(k2, _):
            prev = pl.ds(pl.multiple_of(k2 * PAIR - SUBLANES, SUBLANES), PAIR)
            fix_rows(_step_rows(k2, PAIR), k2, sre_ref[prev, :], sim_ref[prev, :])
            return 0

        lax.fori_loop(1, m // 2, fix, 0)
        gab_ref[0] += jnp.sum(acc_ref[0], axis=0, keepdims=True)
        gab_ref[1] += jnp.sum(acc_ref[1], axis=0, keepdims=True)
        for c in range(CB_S5):
            x = gy[:, _lanes(c)] * d_ref[:, _lanes(c)]
            for j in range(SB_PER_CB * c, SB_PER_CB * (c + 1)):
                pair = _state_pair(lam_ref, j)
                x = x + _dot_nt(pair, bdb_ref[j])
                gbdb_ref[j] += _dot_tn(ub[:, _lanes(c)], pair)
            gu_ref[:, _lanes(c)] = x.astype(bf16)
        gd_ref[...] += jnp.sum(gy * u, axis=0, keepdims=True)

    rev = lambda i: (nch - 1 - i, 0)
    halo = lambda i: (jnp.maximum((nch - 1 - i) * hb - 1, 0), 0)
    blk = lambda wd: pl.BlockSpec((tc, wd), rev)
    return pl.pallas_call(
        body, name=name, grid=(nch,),
        in_specs=[blk(S5_WIDTH), blk(S5_WIDTH), blk(S5_NS), blk(S5_NS),
                  pl.BlockSpec((SUBLANES, S5_NS), halo), pl.BlockSpec((SUBLANES, S5_NS), halo), blk(2 * S5_NS),
                  _resident(bdb.shape), _resident(cdb.shape), _full((1, S5_WIDTH)),
                  _resident((2, SEGS, S5_NS)), _resident((2, m, SEGS, S5_NS))],
        out_specs=[blk(S5_WIDTH), _full((2, 1, S5_NS)), _full((1, S5_WIDTH)), _full(bdb.shape), _full(cdb.shape)],
        out_shape=[_S((T, S5_WIDTH), bf16), _S((2, 1, S5_NS)), _S((1, S5_WIDTH)), _S(bdb.shape), _S(cdb.shape)],
        scratch_shapes=[pltpu.VMEM((tc, S5_NS), f32)] * 4 + [
            pltpu.VMEM((tc, 2 * S5_NS), bf16), pltpu.VMEM((2, PAIR, S5_NS), f32), pltpu.VMEM((2, PAIR, S5_NS), f32),
            pltpu.VMEM((2, 1, S5_NS), f32)],
        compiler_params=_cp("arbitrary"),
    )(gys, usg, s_re, s_im, s_re, s_im, s_bf, bdb, cdb, dvec, abar_b, ptab_rev_b)


def s5_post_fwd(ys, usg, wglu, wbs, name):
    T = ys.shape[0]
    tm = _tile(T, TOKEN_TILE_LIGHT)

    def body(ys_ref, sg_ref, wglu_ref, wbs_ref, glu_ref, zs_ref):
        glu = _dot(_gelu(ys_ref[...]).astype(bf16), wglu_ref[...])
        sg = sg_ref[...]
        y2 = glu[:, :S5_WIDTH] * _sig(glu[:, S5_WIDTH:]) * (sg * _sig(sg))
        glu_ref[...] = glu
        zs_ref[...] = _dot(y2.astype(bf16), wbs_ref[...])

    return pl.pallas_call(
        body, name=name, grid=(T // tm,),
        in_specs=[_rows(tm, S5_WIDTH), _rows(tm, S5_WIDTH, 1), _resident((S5_WIDTH, 2 * S5_WIDTH)),
                  _resident((S5_WIDTH, D_MODEL))],
        out_specs=[_rows(tm, 2 * S5_WIDTH), _rows(tm, D_MODEL)],
        out_shape=[_S((T, 2 * S5_WIDTH)), _S((T, D_MODEL))], compiler_params=_cp("parallel"),
    )(ys, usg, wglu, wbs)


def _accumulate(ref, part, step):
    @pl.when(step == 0)
    def _():
        ref[...] = part

    @pl.when(step > 0)
    def _():
        ref[...] += part


def s5_post_bwd(gzs, glu, usg, ys, wbs, wglu, name):
    T = ys.shape[0]
    tm = _tile(T, TOKEN_TILE_LIGHT)

    def body(gzs_ref, glu_ref, sg_ref, ys_ref, wbs_ref, wglu_ref, gys_ref, gsg_ref, gwbs_ref, gwglu_ref):
        i = pl.program_id(0)
        glu = glu_ref[...]
        a, b = glu[:, :S5_WIDTH], glu[:, S5_WIDTH:]
        sg = sg_ref[...]
        ys = ys_ref[...]
        sb, ssg = _sig(b), _sig(sg)
        silu = sg * ssg
        _accumulate(gwbs_ref, _dot_tn((a * sb * silu).astype(bf16), gzs_ref[...]), i)
        gy2 = _dot_nt(gzs_ref[...], wbs_ref[...])
        g_a = gy2 * sb * silu
        g_b = gy2 * a * sb * (1.0 - sb) * silu
        gsg_ref[...] = (gy2 * a * sb * ssg * (1.0 + sg * (1.0 - ssg))).astype(bf16)
        gglu = jnp.concatenate([g_a, g_b], axis=1).astype(bf16)
        _accumulate(gwglu_ref, _dot_tn(_gelu(ys).astype(bf16), gglu), i)
        gys_ref[...] = _dot_nt(gglu, wglu_ref[...]) * _gelu_grad(ys)

    return pl.pallas_call(
        body, name=name, grid=(T // tm,),
        in_specs=[_rows(tm, D_MODEL), _rows(tm, 2 * S5_WIDTH), _rows(tm, S5_WIDTH, 1), _rows(tm, S5_WIDTH),
                  _resident((S5_WIDTH, D_MODEL)), _resident((S5_WIDTH, 2 * S5_WIDTH))],
        out_specs=[_rows(tm, S5_WIDTH), _rows(tm, S5_WIDTH), _full((S5_WIDTH, D_MODEL)), _full((S5_WIDTH, 2 * S5_WIDTH))],
        out_shape=[_S((T, S5_WIDTH)), _S((T, S5_WIDTH), bf16), _S((S5_WIDTH, D_MODEL)), _S((S5_WIDTH, 2 * S5_WIDTH))],
        compiler_params=_cp("arbitrary"),
    )(gzs, glu, usg, ys, wbs, wglu)


NB_LRU = LRU_WIDTH // LANES
LRU_JB = 5
TAPS_BACK = CONV_WIDTH - 1
EDGE = TAPS_BACK * SUBLANES
HALO_ROWS = 4 * SUBLANES


def _down_a_segment(blk, entering_row):
    sub = lax.broadcasted_iota(jnp.int32, blk.shape, 0)
    return jnp.where(sub == 0, entering_row, pltpu.roll(blk, 1, 0))


def _up_a_segment(blk, entering_row):
    sub = lax.broadcasted_iota(jnp.int32, blk.shape, 0)
    return jnp.where(sub == SUBLANES - 1, entering_row, pltpu.roll(blk, SUBLANES - 1, 0))


def _stack_rows(rows):
    sub = lax.broadcasted_iota(jnp.int32, (SUBLANES,) + rows[0].shape[1:], 0)
    out = jnp.broadcast_to(rows[0], sub.shape)
    for s in range(1, SUBLANES):
        out = jnp.where(sub == s, rows[s], out)
    return out


def _fill_conv_window(xe, x_ref, xh_ref, is_first, tc):
    xe[EDGE:, :] = x_ref[...]
    for i in range(1, TAPS_BACK + 1):
        row = HALO_ROWS - SUBLANES * i + SUBLANES - 1
        entering = jnp.where(is_first, 0.0, xh_ref[row:row + 1, :])
        blk = x_ref[tc - SUBLANES * i:tc - SUBLANES * (i - 1), :]
        xe[EDGE - SUBLANES * i:EDGE - SUBLANES * (i - 1), :] = _down_a_segment(blk, entering)


def lru_fwd(lx, convw, convb, wa, wx, ba, bx, lam, name):
    T = lx.shape[0]
    tc = _tile(T, SCAN_CHUNK)
    m = tc // SEGS
    hb = tc // HALO_ROWS

    def body(x_ref, xh_ref, cw_ref, cb_ref, wa_ref, wx_ref, ba_ref, bx_ref, lam_ref,
             c_ref, r_ref, i_ref, om_ref, h_ref, xe, src_a, src_b, dst_a, dst_h, cin_ref, carry_ref):
        i = pl.program_id(0)

        @pl.when(i == 0)
        def _():
            carry_ref[...] = jnp.zeros_like(carry_ref)

        _fill_conv_window(xe, x_ref, xh_ref, i == 0, tc)
        c = cb_ref[...] + cw_ref[0:1, :] * xe[0:tc, :]
        for k in range(1, CONV_WIDTH):
            c = c + cw_ref[k:k + 1, :] * xe[SUBLANES * k:SUBLANES * k + tc, :]
        c_ref[...] = c
        sp = _softplus_neg(lam_ref[...])
        for j in range(NB_LRU):
            ln = _lanes(j)
            cj = c[:, ln]
            cjb = cj.astype(bf16)
            r = _sig(_dot(cjb, wa_ref[j]) + ba_ref[:, ln])
            g = _sig(_dot(cjb, wx_ref[j]) + bx_ref[:, ln])
            r_ref[:, ln] = r
            i_ref[:, ln] = g
            log_a = -LRU_C * r * sp[:, ln]
            a = jnp.exp(log_a)
            src_a[:, ln] = a
            om = _one_minus_sq(a, log_a)
            om_ref[:, ln] = om
            src_b[:, ln] = jnp.sqrt(om) * (g * cj)
        for j0 in range(0, NB_LRU, LRU_JB):
            def kstep(k, st):
                rows = _step_rows(k)
                out = []
                for q in range(LRU_JB):
                    ln = _lanes(j0 + q)
                    hh, ac = st[2 * q], st[2 * q + 1]
                    a = src_a[rows, ln]
                    hh = a * hh + src_b[rows, ln]
                    ac = a * ac
                    dst_h[rows, ln] = hh
                    dst_a[rows, ln] = ac
                    out += [hh, ac]
                return tuple(out)

            init = tuple(jnp.zeros((SEGS, LANES), f32) if q % 2 == 0 else jnp.ones((SEGS, LANES), f32)
                         for q in range(2 * LRU_JB))
            ends = lax.fori_loop(0, m, kstep, init)
            for q in range(LRU_JB):
                ln = _lanes(j0 + q)
                eh, ea = ends[2 * q], ends[2 * q + 1]
                cr = carry_ref[:, ln]
                rows_c = []
                for s in range(SEGS):
                    rows_c.append(cr)
                    cr = eh[s:s + 1, :] + ea[s:s + 1, :] * cr
                cin_ref[:, ln] = _stack_rows(rows_c)
                carry_ref[:, ln] = cr

        def fix(k, _):
            rows = _step_rows(k)
            h_ref[rows, :] = dst_h[rows, :] + dst_a[rows, :] * cin_ref[...]
            return 0

        lax.fori_loop(0, m, fix, 0)

    wide = lambda: _rows(tc, LRU_WIDTH)
    buf = lambda rows: pltpu.VMEM((rows, LRU_WIDTH), f32)
    return pl.pallas_call(
        body, name=name, grid=(T // tc,),
        in_specs=[wide(), pl.BlockSpec((HALO_ROWS, LRU_WIDTH), lambda i: (jnp.maximum(i * hb - 1, 0), 0)),
                  _full((CONV_WIDTH, LRU_WIDTH)), _full((1, LRU_WIDTH)),
                  _full((LRU_HEADS, LRU_HEAD_DIM, LRU_HEAD_DIM)), _full((LRU_HEADS, LRU_HEAD_DIM, LRU_HEAD_DIM)),
                  _full((1, LRU_WIDTH)), _full((1, LRU_WIDTH)), _full((1, LRU_WIDTH))],
        out_specs=[wide(), wide(), wide(), wide(), wide()],
        out_shape=[_S((T, LRU_WIDTH))] * 5,
        scratch_shapes=[buf(tc + EDGE), buf(tc), buf(tc), buf(tc), buf(tc), buf(SEGS), buf(1)],
        compiler_params=_cp("arbitrary"),
    )(lx, lx, convw, convb, wa, wx, ba, bx, lam)


def lru_bwd(gh, h, c, r, gi, om, lx, convw, wa, wx, lam, name, exchange=()):
    T = gh.shape[0]
    tc = _tile(T, SCAN_CHUNK)
    m = tc // SEGS
    nch = T // tc
    ne = len(exchange)
    NI = 13

    def body(*refs):
        (gh_ref, h_ref, hh_ref, c_ref, r_ref, i_ref, om_ref, x_ref, xh_ref, cw_ref, wa_ref, wx_ref, lam_ref) = refs[:NI]
        glx_ref, gwa_ref, gwx_ref, gba_ref, gbx_ref, glam_ref, gcb_ref, gcw_ref = refs[NI + ne:NI + 8 + ne]
        (src_a, src_m, dst_a, dst_m, mbuf, hbuf, xe, gce, cin_ref, gcc_ref, carry_ref) = refs[NI + 8 + 2 * ne:NI + 19 + 2 * ne]
        i = pl.program_id(0)
        if ne:
            start, finish = _chips_phases(refs[NI:NI + ne], refs[NI + 8 + ne:NI + 8 + 2 * ne], *refs[NI + 19 + 2 * ne:])
            pl.when(i == 0)(start)
            pl.when(i == nch - 1)(finish)

        @pl.when(i == 0)
        def _():
            carry_ref[...] = jnp.zeros_like(carry_ref)
            gcc_ref[...] = jnp.zeros_like(gcc_ref)
            for ref in (gwa_ref, gwx_ref, gba_ref, gbx_ref, glam_ref, gcb_ref, gcw_ref):
                ref[...] = jnp.zeros_like(ref)

        first = i == nch - 1
        last = slice(tc - SUBLANES, tc)
        hbuf[SUBLANES:, :] = h_ref[...]
        hbuf[0:SUBLANES, :] = _down_a_segment(h_ref[last, :], jnp.where(first, 0.0, hh_ref[SUBLANES - 1:SUBLANES, :]))
        _fill_conv_window(xe, x_ref, xh_ref, first, tc)
        lam_v = lam_ref[...]
        sp = _softplus_neg(lam_v)
        a_all = jnp.exp(-LRU_C * r_ref[...] * sp)
        src_a[...] = a_all
        src_m[...] = a_all * gh_ref[...]
        for j0 in range(0, NB_LRU, LRU_JB):
            def kstep(kk, st):
                rows = _step_rows(m - 1 - kk)
                out = []
                for q in range(LRU_JB):
                    ln = _lanes(j0 + q)
                    mu, ac = st[2 * q], st[2 * q + 1]
                    a = src_a[rows, ln]
                    mu = a * mu + src_m[rows, ln]
                    ac = a * ac
                    dst_m[rows, ln] = mu
                    dst_a[rows, ln] = ac
                    out += [mu, ac]
                return tuple(out)

            init = tuple(jnp.zeros((SEGS, LANES), f32) if q % 2 == 0 else jnp.ones((SEGS, LANES), f32)
                         for q in range(2 * LRU_JB))
            ends = lax.fori_loop(0, m, kstep, init)
            for q in range(LRU_JB):
                ln = _lanes(j0 + q)
                em, ea = ends[2 * q], ends[2 * q + 1]
                cr = carry_ref[:, ln]
                rows_c = [None] * SEGS
                for s in reversed(range(SEGS)):
                    rows_c[s] = cr
                    cr = em[s:s + 1, :] + ea[s:s + 1, :] * cr
                cin_ref[:, ln] = _stack_rows(rows_c)
                carry_ref[:, ln] = cr

        def fix(k, _):
            rows = _step_rows(k)
            mbuf[rows, :] = dst_m[rows, :] + dst_a[rows, :] * cin_ref[...]
            return 0

        lax.fori_loop(0, m, fix, 0)
        mbuf[tc:, :] = _up_a_segment(mbuf[0:SUBLANES, :], cin_ref[SUBLANES - 1:SUBLANES, :])
        sneg = _sig(-lam_v)
        for j in range(NB_LRU):
            ln = _lanes(j)
            lamt = gh_ref[:, ln] + mbuf[SUBLANES:, ln]
            rj, ij, cj = r_ref[:, ln], i_ref[:, ln], c_ref[:, ln]
            a = src_a[:, ln]
            om = om_ref[:, ln]
            inv_mult = lax.rsqrt(om)
            mult = om * inv_mult
            g_a = lamt * hbuf[0:tc, ln]
            g_mult = lamt * ij * cj
            g_i = lamt * mult * cj
            g_c = lamt * mult * ij
            g_log_a = g_a * a - g_mult * a * a * inv_mult
            glam_ref[:, ln] += jnp.sum(g_log_a * rj, axis=0, keepdims=True) * LRU_C * sneg[:, ln]
            g_ra = g_log_a * (-LRU_C) * sp[:, ln] * rj * (1.0 - rj)
            g_ia = g_i * ij * (1.0 - ij)
            gba_ref[:, ln] += jnp.sum(g_ra, axis=0, keepdims=True)
            gbx_ref[:, ln] += jnp.sum(g_ia, axis=0, keepdims=True)
            cjb, grb, gib = cj.astype(bf16), g_ra.astype(bf16), g_ia.astype(bf16)
            gwa_ref[j] += _dot_tn(cjb, grb)
            gwx_ref[j] += _dot_tn(cjb, gib)
            g_c = g_c + _dot_nt(grb, wa_ref[j]) + _dot_nt(gib, wx_ref[j])
            gce[0:tc, ln] = g_c
            gcb_ref[:, ln] += jnp.sum(g_c, axis=0, keepdims=True)
        for d in range(TAPS_BACK):
            blk = slice(SUBLANES * d, SUBLANES * (d + 1))
            gce[tc + SUBLANES * d:tc + SUBLANES * (d + 1), :] = _up_a_segment(gce[blk, :], gcc_ref[SUBLANES * d:SUBLANES * d + 1, :])
        gcc_ref[...] = gce[0:EDGE, :]
        gc = gce[0:tc, :]
        glx = cw_ref[CONV_WIDTH - 1:CONV_WIDTH, :] * gc
        gcw_ref[CONV_WIDTH - 1:CONV_WIDTH, :] += jnp.sum(gc * xe[EDGE:EDGE + tc, :], axis=0, keepdims=True)
        for k in range(CONV_WIDTH - 1):
            off = SUBLANES * (CONV_WIDTH - 1 - k)
            glx = glx + cw_ref[k:k + 1, :] * gce[off:off + tc, :]
            gcw_ref[k:k + 1, :] += jnp.sum(gc * xe[EDGE - off:EDGE - off + tc, :], axis=0, keepdims=True)
        glx_ref[...] = glx.astype(bf16)

    rev = lambda i: (nch - 1 - i, 0)
    halo = lambda rows: (lambda i: (jnp.maximum((nch - 1 - i) * (tc // rows) - 1, 0), 0))
    wide = lambda: pl.BlockSpec((tc, LRU_WIDTH), rev)
    vec = lambda: _full((1, LRU_WIDTH))
    hd = lambda: _full((LRU_HEADS, LRU_HEAD_DIM, LRU_HEAD_DIM))
    buf = lambda rows: pltpu.VMEM((rows, LRU_WIDTH), f32)
    outs = pl.pallas_call(
        body, name=name, grid=(nch,),
        in_specs=[wide(), wide(), pl.BlockSpec((SUBLANES, LRU_WIDTH), halo(SUBLANES)), wide(), wide(), wide(), wide(), wide(),
                  pl.BlockSpec((HALO_ROWS, LRU_WIDTH), halo(HALO_ROWS)), _full((CONV_WIDTH, LRU_WIDTH)), hd(), hd(), vec()]
        + [ANY] * ne,
        out_specs=[wide(), hd(), hd(), vec(), vec(), vec(), vec(), _full((CONV_WIDTH, LRU_WIDTH))] + [ANY] * ne,
        out_shape=[_S((T, LRU_WIDTH), bf16), _S((LRU_HEADS, LRU_HEAD_DIM, LRU_HEAD_DIM)),
                   _S((LRU_HEADS, LRU_HEAD_DIM, LRU_HEAD_DIM)), _S((1, LRU_WIDTH)), _S((1, LRU_WIDTH)),
                   _S((1, LRU_WIDTH)), _S((1, LRU_WIDTH)), _S((CONV_WIDTH, LRU_WIDTH))] + (
            _chips_out_shapes(exchange) if ne else []),
        scratch_shapes=[buf(tc), buf(tc), buf(tc), buf(tc), buf(tc + SUBLANES), buf(tc + SUBLANES), buf(tc + EDGE),
                        buf(tc + EDGE), buf(SEGS), buf(EDGE), buf(1)] + (_chips_semaphores(ne) if ne else []),
        compiler_params=_cp("arbitrary"),
    )(gh, h, h, c, r, gi, om, lx, lx, convw, wa, wx, lam, *exchange)
    return outs[:8], list(outs[8:])


def merge_fwd(h, lg, zs, gsl, x, p, wbl, wout, gpost, wple, wpg, name, target=None):
    T = x.shape[0]
    tm = _tile(T, TOKEN_TILE)
    head = target is not None

    def body(h_ref, lg_ref, zs_ref, gs_ref, gl_ref, x_ref, p_ref, wbl_ref, wout_ref, gp_ref, wple_ref, wpg_ref, *refs):
        zl_ref, mix_ref, xo_ref = refs[head:head + 3]
        lg_v = lg_ref[...]
        yl = h_ref[...] * (lg_v * _sig(lg_v))
        zl = _dot(yl.astype(bf16), wbl_ref[...])
        merged = _sig(gs_ref[...]) * zs_ref[...] + _sig(gl_ref[...]) * zl
        mix = _dot(merged.astype(bf16), wout_ref[...])
        r2 = lax.rsqrt(jnp.mean(mix * mix, axis=-1, keepdims=True) + NORM_EPS)
        x1 = x_ref[...] + mix * r2 * gp_ref[...]
        q = _dot(x1.astype(bf16), wpg_ref[...])
        pe = _dot(p_ref[...].astype(bf16), wple_ref[...])
        zl_ref[...], mix_ref[...] = zl, mix
        x2 = x1 + pe * _sig(q)
        if head:
            e = x2 - refs[0][...]
            xo_ref[...] = e * (1.0 / D_MODEL)
            _accumulate(refs[4], 0.5 * jnp.sum(jnp.sum(e * e, axis=-1, keepdims=True) * (1.0 / D_MODEL), axis=0,
                                               keepdims=True), pl.program_id(0))
        else:
            xo_ref[...] = x2

    dm = lambda: _rows(tm, D_MODEL)
    return pl.pallas_call(
        body, name=name, grid=(T // tm,),
        in_specs=[_rows(tm, LRU_WIDTH), _rows(tm, LRU_WIDTH), dm(), _rows(tm, D_MODEL, 0), _rows(tm, D_MODEL, 1), dm(),
                  _rows(tm, PLE_DIM), _resident((LRU_WIDTH, D_MODEL)), _resident((D_MODEL, D_MODEL)), _full((1, D_MODEL)),
                  _resident((PLE_DIM, D_MODEL)), _resident((D_MODEL, D_MODEL))] + ([dm()] if head else []),
        out_specs=[dm(), dm(), dm()] + ([_full((1, 1))] if head else []),
        out_shape=[_S((T, D_MODEL))] * 3 + ([_S((1, 1))] if head else []),
        compiler_params=_cp("arbitrary" if head else "parallel"),
    )(h, lg, zs, gsl, gsl, x, p, wbl, wout, gpost, wple, wpg, *([target] if head else []))


def post_bwd(gx2, mix, x, p, wpg, wple, gpost, name, exchange=()):
    T = x.shape[0]
    tm = _tile(T, TOKEN_TILE_LIGHT)
    nsteps = T // tm
    ne = len(exchange)

    def body(*refs):
        gx2_ref, mix_ref, x_ref, p_ref, wpg_ref, wple_ref, gp_ref = refs[:7]
        gres_ref, gmix_ref, ggp_ref, gwpg_ref, gwple_ref = refs[7 + ne:12 + ne]
        i = pl.program_id(0)
        if ne:
            start, finish = _sibling_phases(refs[7:7 + ne], refs[12 + ne:12 + 2 * ne], *refs[12 + 2 * ne:])
            pl.when(i == 0)(start)
            pl.when(i == nsteps - 1)(finish)
        gx2 = gx2_ref[...]
        mix = mix_ref[...]
        gp = gp_ref[...]
        r2 = lax.rsqrt(jnp.mean(mix * mix, axis=-1, keepdims=True) + NORM_EPS)
        nrm = mix * r2
        x1b = (x_ref[...] + nrm * gp).astype(bf16)
        pb = p_ref[...].astype(bf16)
        sq = _sig(_dot(x1b, wpg_ref[...]))
        pe = _dot(pb, wple_ref[...])
        gq = (gx2 * pe * sq * (1.0 - sq)).astype(bf16)
        _accumulate(gwple_ref, _dot_tn(pb, (gx2 * sq).astype(bf16)), i)
        _accumulate(gwpg_ref, _dot_tn(x1b, gq), i)
        gx1 = gx2 + _dot_nt(gq, wpg_ref[...])
        gres_ref[...] = gx1
        _accumulate(ggp_ref, jnp.sum(gx1 * nrm, axis=0, keepdims=True), i)
        gy = gx1 * gp
        gmix_ref[...] = (r2 * (gy - nrm * jnp.mean(gy * nrm, axis=-1, keepdims=True))).astype(bf16)

    dm = lambda: _rows(tm, D_MODEL)
    outs = pl.pallas_call(
        body, name=name, grid=(nsteps,),
        in_specs=[dm(), dm(), dm(), _rows(tm, PLE_DIM), _resident((D_MODEL, D_MODEL)), _resident((PLE_DIM, D_MODEL)),
                  _full((1, D_MODEL))] + [ANY] * ne,
        out_specs=[dm(), dm(), _full((1, D_MODEL)), _full((D_MODEL, D_MODEL)), _full((PLE_DIM, D_MODEL))] + [ANY] * ne,
        out_shape=[_S((T, D_MODEL)), _S((T, D_MODEL), bf16), _S((1, D_MODEL)), _S((D_MODEL, D_MODEL)),
                   _S((PLE_DIM, D_MODEL))] + (_sibling_out_shapes(exchange) if ne else []),
        scratch_shapes=_sibling_semaphores(ne) if ne else [],
        compiler_params=_cp("arbitrary"),
    )(gx2, mix, x, p, wpg, wple, gpost, *exchange)
    return outs[:5], list(outs[5:])


def gate_bwd(gmix, zl, zs, gsl, wout, name):
    T = zl.shape[0]
    tm = _tile(T, TOKEN_TILE_LIGHT)

    def body(gmix_ref, zl_ref, zs_ref, gs_ref, gl_ref, wout_ref, gzs_ref, gzl_ref, ggsl_ref, gwout_ref):
        i = pl.program_id(0)
        gmix = gmix_ref[...]
        gmerged = _dot_nt(gmix, wout_ref[...])
        zs, zl = zs_ref[...], zl_ref[...]
        ss, sl = _sig(gs_ref[...]), _sig(gl_ref[...])
        _accumulate(gwout_ref, _dot_tn((ss * zs + sl * zl).astype(bf16), gmix), i)
        gzs_ref[...] = (gmerged * ss).astype(bf16)
        gzl_ref[...] = (gmerged * sl).astype(bf16)
        ggsl_ref[:, :D_MODEL] = (gmerged * zs * ss * (1.0 - ss)).astype(bf16)
        ggsl_ref[:, D_MODEL:] = (gmerged * zl * sl * (1.0 - sl)).astype(bf16)

    dm = lambda: _rows(tm, D_MODEL)
    return pl.pallas_call(
        body, name=name, grid=(T // tm,),
        in_specs=[dm(), dm(), dm(), _rows(tm, D_MODEL, 0), _rows(tm, D_MODEL, 1), _resident((D_MODEL, D_MODEL))],
        out_specs=[dm(), dm(), _rows(tm, 2 * D_MODEL), _full((D_MODEL, D_MODEL))],
        out_shape=[_S((T, D_MODEL), bf16), _S((T, D_MODEL), bf16), _S((T, 2 * D_MODEL), bf16), _S((D_MODEL, D_MODEL))],
        compiler_params=_cp("arbitrary"),
    )(gmix, zl, zs, gsl, gsl, wout)


def lru_out_bwd(gzl, h, lg, wbl, name):
    T = h.shape[0]
    tm = _tile(T, TOKEN_TILE_LIGHT)

    def body(gzl_ref, h_ref, lg_ref, wbl_ref, gh_ref, glg_ref, gwbl_ref):
        i = pl.program_id(0)
        gzl = gzl_ref[...]
        lg_v, hv = lg_ref[...], h_ref[...]
        slg = _sig(lg_v)
        silu = lg_v * slg
        _accumulate(gwbl_ref, _dot_tn((hv * silu).astype(bf16), gzl), i)
        gyl = _dot_nt(gzl, wbl_ref[...])
        gh_ref[...] = gyl * silu
        glg_ref[...] = (gyl * hv * slg * (1.0 + lg_v * (1.0 - slg))).astype(bf16)

    lw = lambda: _rows(tm, LRU_WIDTH)
    return pl.pallas_call(
        body, name=name, grid=(T // tm,),
        in_specs=[_rows(tm, D_MODEL), lw(), lw(), _resident((LRU_WIDTH, D_MODEL))],
        out_specs=[lw(), lw(), _full((LRU_WIDTH, D_MODEL))],
        out_shape=[_S((T, LRU_WIDTH)), _S((T, LRU_WIDTH), bf16), _S((LRU_WIDTH, D_MODEL))],
        compiler_params=_cp("arbitrary"),
    )(gzl, h, lg, wbl)


def in_proj_bwd(pieces, win, x, gres, g, name):
    T = x.shape[0]
    tm = _tile(T, MM_TILE_M // 2)
    widths = [pc.shape[1] for pc in pieces]
    offs = [sum(widths[:k]) for k in range(len(widths))]

    def body(*refs):
        pc_refs = refs[:len(widths)]
        w_ref, x_ref, gres_ref, g_ref, gx_ref, gg_ref = refs[len(widths):]
        i = pl.program_id(0)
        ghv = _dot_nt(pc_refs[0][...], w_ref[:, offs[0]:offs[0] + widths[0]])
        for k in range(1, len(widths)):
            ghv = ghv + _dot_nt(pc_refs[k][...], w_ref[:, offs[k]:offs[k] + widths[k]])
        xv = x_ref[...]
        r = lax.rsqrt(jnp.mean(xv * xv, axis=-1, keepdims=True) + NORM_EPS)
        nrm = xv * r
        gy = ghv * g_ref[...]
        gx_ref[...] = gres_ref[...] + r * (gy - nrm * jnp.mean(gy * nrm, axis=-1, keepdims=True))
        _accumulate(gg_ref, jnp.sum(ghv * nrm, axis=0, keepdims=True), i)

    dm = lambda: _rows(tm, D_MODEL)
    return pl.pallas_call(
        body, name=name, grid=(T // tm,),
        in_specs=[_rows(tm, wd) for wd in widths] + [_resident(win.shape), dm(), dm(), _full((1, D_MODEL))],
        out_specs=[dm(), _full((1, D_MODEL))],
        out_shape=[_S((T, D_MODEL)), _S((1, D_MODEL))], compiler_params=_cp("arbitrary"),
    )(*pieces, win, x, gres, g)


def _s5_operands(w, m, tag):
    b_re_t = jnp.transpose(w['s5_b_re'], (2, 0, 1))
    b_im_t = jnp.transpose(w['s5_b_im'], (2, 0, 1))
    ldt = w['s5_log_dt'][:, None]
    ab, pw, bb = s5_prep(w['s5_a_re'], w['s5_a_im'], ldt, b_re_t, b_im_t, m, "s5_prep" + tag)
    over_sublanes = lambda t: jnp.broadcast_to(t[..., None, :], t.shape[:-1] + (SEGS, S5_NS))
    ptab = pw.reshape(2, m, S5_NS)
    return dict(abar_b=over_sublanes(ab.reshape(2, S5_NS)), ptab_b=over_sublanes(ptab),
                ptab_rev_b=over_sublanes(ptab[:, ::-1, :]), bdb=_pack_bdb(bb).astype(bf16),
                cdb=_pack_cdb(w['s5_c_re'], w['s5_c_im']).astype(bf16), dvec=w['s5_d'][None, :],
                prep_in=(w['s5_a_re'], w['s5_a_im'], ldt, b_re_t, b_im_t))


def layer_fwd(x, p, w, tag, gather=(), target=None):
    T = x.shape[0]
    m = _tile(T, SCAN_CHUNK) // SEGS
    s5 = _s5_operands(w, m, tag)
    h_bf, usg, lx, lg, gsl = in_proj_fwd(x, w['g_pre'][None, :], w['w_in'], "in_proj_fwd" + tag)
    (ys, s_re, s_im, s_bf), gathered = s5_fwd(usg, s5['bdb'], s5['cdb'], s5['dvec'], s5['abar_b'], s5['ptab_b'],
                                              "s5_fwd" + tag, gather)
    glu, zs = s5_post_fwd(ys, usg, w['w_glu'], w['w_bs'], "s5_post_fwd" + tag)
    wa, wx = w['lru_w_a'].astype(bf16), w['lru_w_x'].astype(bf16)
    c, r, gi, om, hs = lru_fwd(lx, w['conv_w'], w['conv_b'][None, :], wa, wx, w['lru_b_a'][None, :],
                               w['lru_b_x'][None, :], w['lru_lambda'][None, :], "lru_fwd" + tag)
    zl, mix, x_out, *loss_part = merge_fwd(hs, lg, zs, gsl, x, p, w['w_bl'], w['w_out'], w['g_post'][None, :],
                                           w['w_ple'], w['w_ple_gate'], "merge_fwd" + tag, target)
    saved = dict(x=x, p=p, h_bf=h_bf, usg=usg, lx=lx, lg=lg, gsl=gsl, ys=ys, s_re=s_re, s_im=s_im, s_bf=s_bf, glu=glu,
                 zs=zs, c=c, r=r, gi=gi, om=om, hs=hs, zl=zl, mix=mix, s5=s5, wa=wa, wx=wx)
    return x_out, saved, gathered, (loss_part[0] if loss_part else None)


def layer_bwd(gx_out, w, sv, tag, reduce_job=None):
    s5 = sv['s5']
    g = {}
    (gres, gmix, g_gpost, g['w_ple_gate'], g['w_ple']), theirs = post_bwd(
        gx_out, sv['mix'], sv['x'], sv['p'], w['w_ple_gate'], w['w_ple'], w['g_post'][None, :], "post_bwd" + tag,
        reduce_job[0] if reduce_job else ())
    parts, wire = reduce_job[1](theirs) if reduce_job else ((), ())
    gzs, gzl, ggsl, g['w_out'] = gate_bwd(gmix, sv['zl'], sv['zs'], sv['gsl'], w['w_out'], "gate_bwd" + tag)
    g_h, g_lg, g['w_bl'] = lru_out_bwd(gzl, sv['hs'], sv['lg'], w['w_bl'], "lru_out_bwd" + tag)
    g['g_post'] = g_gpost[0]
    (g_lx, g_wa, g_wx, g_ba, g_bx, g_lam, g_cb, g_cw), exchanged = lru_bwd(
        g_h, sv['hs'], sv['c'], sv['r'], sv['gi'], sv['om'], sv['lx'], w['conv_w'], sv['wa'], sv['wx'],
        w['lru_lambda'][None, :], "lru_bwd" + tag, wire)
    g['lru_w_a'], g['lru_w_x'] = g_wa, g_wx
    g['lru_b_a'], g['lru_b_x'], g['lru_lambda'], g['conv_b'], g['conv_w'] = g_ba[0], g_bx[0], g_lam[0], g_cb[0], g_cw
    g_ys, g_sg, g['w_bs'], g['w_glu'] = s5_post_bwd(gzs, sv['glu'], sv['usg'], sv['ys'], w['w_bs'], w['w_glu'],
                                                    "s5_post_bwd" + tag)
    g_u, g_ab, g_d, g_bdb, g_cdb = s5_bwd(g_ys, sv['usg'], sv['s_re'], sv['s_im'], sv['s_bf'], s5['bdb'], s5['cdb'],
                                          s5['dvec'], s5['abar_b'], s5['ptab_rev_b'], "s5_bwd" + tag)
    g['s5_d'] = g_d[0]
    g['s5_c_re'], g['s5_c_im'] = _unpack_cdb(g_cdb)
    g_are, g_aim, g_ldt, g_bre_t, g_bim_t = s5_prep_bwd(*s5['prep_in'], g_ab.reshape(2, S5_GROUPS, S5_STATE),
                                                       _unpack_bdb(g_bdb), "s5_prep_bwd" + tag)
    g['s5_a_re'], g['s5_a_im'], g['s5_log_dt'] = g_are, g_aim, g_ldt
    g['s5_b_re'] = jnp.transpose(g_bre_t, (1, 2, 0))
    g['s5_b_im'] = jnp.transpose(g_bim_t, (1, 2, 0))
    pieces = [g_u, g_sg, g_lx, g_lg, ggsl]
    g['w_in'] = jnp.concatenate(mm_tn(sv['h_bf'], pieces[:3], "gw_in_a" + tag) + mm_tn(sv['h_bf'], pieces[3:], "gw_in_b" + tag),
                                axis=1)
    gx, g_gpre = in_proj_bwd(pieces, w['w_in'], sv['x'], gres, w['g_pre'][None, :], "in_proj_bwd" + tag)
    g['g_pre'] = g_gpre[0]
    return gx, g, (parts, exchanged)


def _as_2d(a):
    return a.reshape((-1, a.shape[-1])) if a.ndim > 1 else a.reshape((1, -1))


def _adamw_update(w, gv, m, v):
    nm = ADAM_B1 * m + (1.0 - ADAM_B1) * gv
    nv = ADAM_B2 * v + (1.0 - ADAM_B2) * (gv * gv)
    bc1 = 1.0 - ADAM_B1 ** ADAM_STEP
    bc2 = 1.0 - ADAM_B2 ** ADAM_STEP
    return -ADAM_LR * ((nm / bc1) / (jnp.sqrt(nv / bc2) + ADAM_EPS) + ADAM_WD * w), nm, nv


def adamw(w, g, m, v, name):
    shape = w.shape
    w2, g2, m2, v2 = _as_2d(w), _as_2d(g), _as_2d(m), _as_2d(v)
    R, C = w2.shape
    tr = _row_tile(R, C)

    def body(w_ref, g_ref, m_ref, v_ref, d_ref, nm_ref, nv_ref):
        d_ref[...], nm_ref[...], nv_ref[...] = _adamw_update(w_ref[...], g_ref[...], m_ref[...], v_ref[...])

    spec = lambda: pl.BlockSpec((tr, C), lambda i: (i, 0))
    d, nm, nv = pl.pallas_call(
        body, name=name, grid=(R // tr,), in_specs=[spec() for _ in range(4)], out_specs=[spec() for _ in range(3)],
        out_shape=[_S((R, C))] * 3, compiler_params=_cp("parallel"),
    )(w2, g2, m2, v2)
    return d.reshape(shape), nm.reshape(shape), nv.reshape(shape)


def adamw_reduce(w, parts, theirs, m, v, chip, name):
    shape = w.shape
    C = shape[-1]
    R = math.prod(shape[1:-1])
    w3, m3, v3 = w.reshape(DEPTH, R, C), m.reshape(DEPTH, R, C), v.reshape(DEPTH, R, C)
    tr = _row_tile(R, C)

    def body(chip_ref, w_ref, *refs):
        layer_refs, (m_ref, v_ref, g_ref, d_ref, nm_ref, nv_ref) = refs[:2 * DEPTH], refs[2 * DEPTH:]
        layer = pl.program_id(0)
        for l in range(DEPTH):
            @pl.when(layer == l)
            def _():
                a_ref, t_ref = layer_refs[2 * l], layer_refs[2 * l + 1]
                gv = ((a_ref[0] + t_ref[0].astype(f32)) + t_ref[1].astype(f32)) + t_ref[2].astype(f32)
                g_ref[0] = gv
                d_ref[0], nm_ref[0], nv_ref[0] = _adamw_update(w_ref[0], gv, m_ref[0], v_ref[0])

    spec = lambda: pl.BlockSpec((1, tr, C), lambda l, i, c: (l, i, 0))
    rows_of = lambda l: (lambda ll, i, c: jnp.where(ll == l, i, 0))
    layer_specs = []
    for l in range(DEPTH):
        layer_specs.append(pl.BlockSpec((1, tr, C), lambda ll, i, c, r=rows_of(l): (c[0], r(ll, i, c), 0)))
        layer_specs.append(pl.BlockSpec((3, tr, C), lambda ll, i, c, r=rows_of(l): (0, r(ll, i, c), 0)))
    grid_spec = pltpu.PrefetchScalarGridSpec(
        num_scalar_prefetch=1, grid=(DEPTH, R // tr),
        in_specs=[spec()] + layer_specs + [spec(), spec()], out_specs=[spec() for _ in range(4)])
    operands = [x.reshape(x.shape[0], R, C) for l in range(DEPTH) for x in (parts[l], theirs[l])]
    g, d, nm, nv = pl.pallas_call(
        body, name=name, grid_spec=grid_spec, out_shape=[_S((DEPTH, R, C))] * 4,
        compiler_params=_cp("arbitrary", "arbitrary"),
    )(chip, w3, *operands, m3, v3)
    return g.reshape(shape), d.reshape(shape), nm.reshape(shape), nv.reshape(shape)


MESH = pl.DeviceIdType.MESH
ANY = pl.BlockSpec(memory_space=pl.ANY)


def _place():
    return lax.axis_index("x"), lax.axis_index("y"), lax.axis_index("c")


def _other_chips(mx, my):
    return [(1 - mx, my), (mx, 1 - my), (1 - mx, 1 - my)]


def all_gather(shards, name):
    nb = len(shards)

    def body(*refs):
        phases = _gather_phases([s.shape for s in shards], refs[:nb], refs[nb:2 * nb], *refs[2 * nb:])
        for phase in phases:
            phase()

    outs = pl.pallas_call(
        body, name=name, out_shape=_gather_out_shapes(shards), in_specs=[ANY] * nb, out_specs=[ANY] * nb,
        scratch_shapes=_gather_semaphores(nb),
    )(*shards)
    return list(outs)


GATHER_COPIES = 9
OWN_SIB, OWN_X, OWN_Y, X_SIB, Y_SIB, RELAY_X, RELAY_Y, DIAG0_SIB, DIAG1_SIB = range(GATHER_COPIES)


def _gather_out_shapes(shards):
    return [_S((N_DEV,) + s.shape, s.dtype) for s in shards]


def _gather_semaphores(nb):
    return [pltpu.SemaphoreType.DMA((nb, GATHER_COPIES)), pltpu.SemaphoreType.DMA((nb, GATHER_COPIES)),
            pltpu.SemaphoreType.DMA((nb,))]


def _gather_phases(shapes, x_refs, out_refs, send_sems, recv_sems, local_sems):
    nb = len(shapes)
    mx, my, mc = _place()
    sibling, xn, yn = (mx, my, 1 - mc), (1 - mx, my, mc), (mx, 1 - my, mc)

    def block(b, px, py, pc, half=None):
        ref = out_refs[b].at[4 * px + 2 * py + pc]
        if half is None:
            return ref
        n = shapes[b][0] // 2
        return ref.at[pl.ds(half * n, n)]

    def copy(b, k, dst, to, src=None):
        return pltpu.make_async_remote_copy(
            src_ref=dst if src is None else src, dst_ref=dst, send_sem=send_sems.at[b, k],
            recv_sem=recv_sems.at[b, k], device_id=to, device_id_type=MESH)

    def send(b, k):
        if k in (OWN_X, OWN_Y, OWN_SIB):
            return copy(b, k, block(b, mx, my, mc), {OWN_X: xn, OWN_Y: yn, OWN_SIB: sibling}[k], src=x_refs[b])
        what, to = {RELAY_X: ((1 - mx, my, mc, 0), yn), RELAY_Y: ((mx, 1 - my, mc, 1), xn),
                    X_SIB: ((1 - mx, my, mc), sibling), Y_SIB: ((mx, 1 - my, mc), sibling),
                    DIAG0_SIB: ((1 - mx, 1 - my, mc, 0), sibling), DIAG1_SIB: ((1 - mx, 1 - my, mc, 1), sibling)}[k]
        return copy(b, k, block(b, *what), to)

    def local(b):
        return pltpu.make_async_copy(x_refs[b], block(b, mx, my, mc), local_sems.at[b])

    def send_own():
        for k in (OWN_X, OWN_Y, OWN_SIB):
            for b in range(nb):
                send(b, k).start()
        for b in range(nb):
            local(b).start()

    def relay_neighbours():
        for b in range(nb):
            copy(b, OWN_X, block(b, 1 - mx, my, mc), xn).wait_recv()
            send(b, RELAY_X).start()
            send(b, X_SIB).start()
        for b in range(nb):
            copy(b, OWN_Y, block(b, mx, 1 - my, mc), yn).wait_recv()
            send(b, RELAY_Y).start()
            send(b, Y_SIB).start()

    def hand_on_diagonal():
        for b in range(nb):
            copy(b, RELAY_X, block(b, 1 - mx, 1 - my, mc, 0), yn).wait_recv()
            send(b, DIAG0_SIB).start()
            copy(b, RELAY_Y, block(b, 1 - mx, 1 - my, mc, 1), xn).wait_recv()
            send(b, DIAG1_SIB).start()

    def finish():
        for b in range(nb):
            copy(b, OWN_SIB, block(b, mx, my, 1 - mc), sibling).wait_recv()
            copy(b, X_SIB, block(b, 1 - mx, my, 1 - mc), sibling).wait_recv()
            copy(b, Y_SIB, block(b, mx, 1 - my, 1 - mc), sibling).wait_recv()
            copy(b, DIAG0_SIB, block(b, 1 - mx, 1 - my, 1 - mc, 0), sibling).wait_recv()
            copy(b, DIAG1_SIB, block(b, 1 - mx, 1 - my, 1 - mc, 1), sibling).wait_recv()
        for b in range(nb):
            for k in range(GATHER_COPIES):
                send(b, k).wait_send()
            local(b).wait()

    return send_own, relay_neighbours, hand_on_diagonal, finish


def exchange_sibling(gs, name):
    nb = len(gs)

    def body(*refs):
        start, finish = _sibling_phases(refs[:nb], refs[nb:2 * nb], refs[2 * nb], refs[2 * nb + 1])
        start()
        finish()

    outs = pl.pallas_call(
        body, name=name, out_shape=_sibling_out_shapes(gs), in_specs=[ANY] * nb, out_specs=[ANY] * nb,
        scratch_shapes=_sibling_semaphores(nb),
    )(*gs)
    return list(outs)


def _sibling_out_shapes(gs):
    return [_S((4,) + g.shape[1:], g.dtype) for g in gs]


def _sibling_semaphores(nb):
    return [pltpu.SemaphoreType.DMA((nb, 4)), pltpu.SemaphoreType.DMA((nb, 4))]


def _sibling_phases(g_refs, recv_refs, send_sems, recv_sems):
    mx, my, mc = _place()

    def copies():
        return [pltpu.make_async_remote_copy(
            src_ref=g_refs[b].at[2 * k + 1 - mc], dst_ref=recv_refs[b].at[k], send_sem=send_sems.at[b, k],
            recv_sem=recv_sems.at[b, k], device_id=(mx, my, 1 - mc), device_id_type=MESH)
            for b in range(len(g_refs)) for k in range(4)]

    def start():
        for cp in copies():
            cp.start()

    def finish():
        for cp in copies():
            cp.wait()

    return start, finish


def exchange_chips(parts, name):
    nb = len(parts)

    def body(*refs):
        start, finish = _chips_phases(refs[:nb], refs[nb:2 * nb], refs[2 * nb], refs[2 * nb + 1])
        start()
        finish()

    outs = pl.pallas_call(
        body, name=name, out_shape=_chips_out_shapes(parts), in_specs=[ANY] * nb, out_specs=[ANY] * nb,
        scratch_shapes=_chips_semaphores(nb),
    )(*parts)
    return list(outs)


def _chips_out_shapes(parts):
    return [_S((3,) + a.shape[1:], a.dtype) for a in parts]


def _chips_semaphores(nb):
    return [pltpu.SemaphoreType.DMA((nb, 3)), pltpu.SemaphoreType.DMA((nb, 3))]


def _chips_phases(a_refs, recv_refs, send_sems, recv_sems):
    mx, my, mc = _place()

    def copies():
        return [pltpu.make_async_remote_copy(
            src_ref=a_refs[b].at[2 * px + py], dst_ref=recv_refs[b].at[j], send_sem=send_sems.at[b, j],
            recv_sem=recv_sems.at[b, j], device_id=(px, py, mc), device_id_type=MESH)
            for b in range(len(a_refs)) for j, (px, py) in enumerate(_other_chips(mx, my))]

    def start():
        for cp in copies():
            cp.start()

    def finish():
        for cp in copies():
            cp.wait()

    return start, finish


def add_sibling(g, theirs, core, name, wire_dtype=f32):
    shp = theirs.shape
    C = shp[-1]
    R = math.prod(shp[1:-1])
    tr = _row_tile(R, C)
    narrow = wire_dtype != f32

    def body(core_ref, g_ref, t_ref, o_ref, *wire_ref):
        s = g_ref[...] + t_ref[...]
        o_ref[...] = s
        if narrow:
            wire_ref[0][...] = s.astype(wire_dtype)

    blk = lambda: pl.BlockSpec((1, tr, C), lambda k, i, c: (k, i, 0))
    grid_spec = pltpu.PrefetchScalarGridSpec(
        num_scalar_prefetch=1, grid=(4, R // tr),
        in_specs=[pl.BlockSpec((1, tr, C), lambda k, i, c: (2 * k + c[0], i, 0)), blk()],
        out_specs=[blk(), blk()] if narrow else [blk()])
    outs = pl.pallas_call(
        body, name=name, grid_spec=grid_spec,
        out_shape=[_S((4, R, C), f32)] + ([_S((4, R, C), wire_dtype)] if narrow else []),
        compiler_params=_cp("parallel", "parallel"),
    )(core, g.reshape(N_DEV, R, C), theirs.reshape(4, R, C))
    part = outs[0].reshape(shp)
    return part, (outs[1].reshape(shp) if narrow else part)


def add_chips(a, theirs, chip, name):
    _, R, C = a.shape
    tr = _row_tile(R, C)

    def body(chip_ref, a_ref, t_ref, out_ref):
        out_ref[...] = ((a_ref[0] + t_ref[0]) + t_ref[1]) + t_ref[2]

    grid_spec = pltpu.PrefetchScalarGridSpec(
        num_scalar_prefetch=1, grid=(R // tr,),
        in_specs=[pl.BlockSpec((1, tr, C), lambda i, c: (c[0], i, 0)), pl.BlockSpec((3, tr, C), lambda i, c: (0, i, 0))],
        out_specs=pl.BlockSpec((tr, C), lambda i, c: (i, 0)))
    return pl.pallas_call(
        body, name=name, grid_spec=grid_spec, out_shape=_S((R, C), a.dtype), compiler_params=_cp("parallel"),
    )(chip, a, theirs)


def _round_up(n, q):
    return (n + q - 1) // q * q


def _lane_rows(a):
    flat = a.reshape(-1)
    n = _round_up(flat.shape[0], SUBLANES * LANES)
    return jnp.pad(flat, (0, n - flat.shape[0])).reshape(-1, LANES)


def _full_to_shards(full, axis):
    shp = full.shape
    s = shp[axis] // N_DEV
    cut = full.reshape(shp[:axis] + (N_DEV, s) + shp[axis + 1:])
    return jnp.moveaxis(cut, axis, 0)


def _shards_to_full(parts, axis):
    shp = list(parts.shape[1:])
    shp[axis] *= N_DEV
    return jnp.moveaxis(parts, 0, axis).reshape(tuple(shp))


def kernel(x, p, g_pre, w_in, s5_a_re, s5_a_im, s5_log_dt, s5_b_re, s5_b_im, s5_c_re, s5_c_im, s5_d, w_glu, w_bs, conv_w, conv_b, lru_w_a, lru_b_a, lru_w_x, lru_b_x, lru_lambda, w_bl, w_out, g_post, w_ple, w_ple_gate, loss_target, m_g_pre, m_w_in, m_s5_a_re, m_s5_a_im, m_s5_log_dt, m_s5_b_re, m_s5_b_im, m_s5_c_re, m_s5_c_im, m_s5_d, m_w_glu, m_w_bs, m_conv_w, m_conv_b, m_lru_w_a, m_lru_b_a, m_lru_w_x, m_lru_b_x, m_lru_lambda, m_w_bl, m_w_out, m_g_post, m_w_ple, m_w_ple_gate, v_g_pre, v_w_in, v_s5_a_re, v_s5_a_im, v_s5_log_dt, v_s5_b_re, v_s5_b_im, v_s5_c_re, v_s5_c_im, v_s5_d, v_w_glu, v_w_bs, v_conv_w, v_conv_b, v_lru_w_a, v_lru_b_a, v_lru_w_x, v_lru_b_x, v_lru_lambda, v_w_bl, v_w_out, v_g_post, v_w_ple, v_w_ple_gate):
    given = dict(locals())
    W = {n: given[n] for n in WEIGHTS}
    M = {n: given["m_" + n] for n in WEIGHTS}
    V = {n: given["v_" + n] for n in WEIGHTS}
    xs, target = to_scan_order(x[0]), to_scan_order(loss_target[0])
    ps = [to_scan_order(p[i, 0]) for i in range(DEPTH)]

    mx, my, mc = _place()
    core = jnp.reshape(mc, (1,)).astype(jnp.int32)
    chip = jnp.reshape(2 * mx + my, (1,)).astype(jnp.int32)

    names = list(SHARDED)
    conv_rows = PAIR - CONV_WIDTH

    def layer_shards(i):
        return [W[n][i].astype(bf16) if n in GATHER_BF16 else jnp.pad(W[n][i], ((0, conv_rows), (0, 0))) for n in names]

    def layer_weights(i, gathered):
        full = {n: _shards_to_full(g if n in GATHER_BF16 else g[:, :CONV_WIDTH], SHARDED[n] - 1)
                for n, g in zip(names, gathered)}
        return {n: (full[n] if n in SHARDED else W[n][i]) for n in WEIGHTS}

    act, saved, weights = xs, [], []
    gathered = all_gather(layer_shards(0), "comm_gather_weights")
    for i in range(DEPTH):
        last = i + 1 == DEPTH
        weights.append(layer_weights(i, gathered))
        act, sv, gathered, loss_part = layer_fwd(act, ps[i], weights[i], "_l%d" % i, () if last else layer_shards(i + 1),
                                                 target if last else None)
        saved.append(sv)
    gact = act
    loss = lax.psum(loss_part[0, 0], ("x", "y", "c"))

    def gradient_blocks(i, g):
        rep_rows = [_lane_rows(g[n].reshape(W[n].shape[1:])) for n in REPLICATED]
        n_rows = sum(r.shape[0] for r in rep_rows)
        pad_rows = _round_up(n_rows, N_DEV * SUBLANES) - n_rows
        rep_blocks = jnp.concatenate(rep_rows + [jnp.zeros((pad_rows, LANES), f32)]).reshape(N_DEV, -1, LANES)
        blocks = [_full_to_shards(g[n].reshape(weights[i][n].shape), SHARDED[n] - 1) for n in names] + [rep_blocks]
        return blocks, [r.shape[0] for r in rep_rows]

    def add_siblings(i, blocks, theirs):
        parts, wire = [], []
        for k, (b, t) in enumerate(zip(blocks, theirs)):
            part, sent = add_sibling(b, t, core, "reduce_add_sibling_%d_l%d" % (k, i), bf16 if k < len(names) else f32)
            parts.append(part)
            wire.append(sent)
        return parts, wire

    parts, others, rep_sizes, job = [None] * DEPTH, [None] * DEPTH, None, None
    for i in reversed(range(DEPTH)):
        gact, g, done = layer_bwd(gact, weights[i], saved[i], "_l%d" % i, job)
        if job:
            parts[i + 1], others[i + 1] = done
        blocks, rep_sizes = gradient_blocks(i, g)
        job = (blocks, lambda theirs, i=i, blocks=blocks: add_siblings(i, blocks, theirs))
    parts[0], wire = job[1](exchange_sibling(job[0], "comm_reduce_sibling"))
    others[0] = exchange_chips(wire, "comm_reduce_chips")

    red, deltas, new_m, new_v = {}, {}, {}, {}
    for k, n in enumerate(names):
        red[n], deltas[n], new_m[n], new_v[n] = adamw_reduce(
            W[n], [parts[i][k] for i in range(DEPTH)], [others[i][k] for i in range(DEPTH)], M[n], V[n], chip, "adamw_" + n)
    pieces = [add_chips(parts[i][-1], others[i][-1], chip, "reduce_add_chips_l%d" % i) for i in range(DEPTH)]
    rep_all = [r.reshape(-1, LANES) for r in all_gather(pieces, "comm_gather_replicated")]
    off = 0
    for n, rows in zip(REPLICATED, rep_sizes):
        k = math.prod(W[n].shape[1:])
        red[n] = jnp.stack([rep_all[i][off:off + rows].reshape(-1)[:k] for i in range(DEPTH)]).reshape(W[n].shape)
        off += rows
        deltas[n], new_m[n], new_v[n] = adamw(W[n], red[n], M[n], V[n], "adamw_" + n)
    return (loss, from_scan_order(gact)[None], *[red[n] for n in WEIGHTS], *[deltas[n] for n in WEIGHTS],
            *[new_m[n] for n in WEIGHTS], *[new_v[n] for n in WEIGHTS])
```

```python
import math

import jax
import jax.numpy as jnp
from jax import lax
from jax.experimental import pallas as pl
from jax.experimental.pallas import tpu as pltpu

f32 = jnp.float32
bf16 = jnp.bfloat16

D_MODEL = 1024
DEPTH = 2
PLE_DIM = 256
NORM_EPS = 1e-6
S5_WIDTH = 512
S5_GROUP = 16
S5_GROUPS = 32
S5_STATE = 64
S5_NS = S5_GROUPS * S5_STATE
LRU_WIDTH = 1280
LRU_HEADS = 10
LRU_HEAD_DIM = 128
LRU_C = 8.0
CONV_WIDTH = 4
N_DEV = 8

ADAM_LR = 0.001
ADAM_B1 = 0.9
ADAM_B2 = 0.999
ADAM_EPS = 1e-08
ADAM_WD = 0.01
ADAM_STEP = 10

LANES = 128
SUBLANES = 8
SEGS = SUBLANES
SCAN_CHUNK = 256
TOKEN_TILE = 256
TOKEN_TILE_LIGHT = 512
MM_TILE_M = 1024
PAIR = 2 * SUBLANES
VMEM_LIMIT_BYTES = 56 * 1024 * 1024
ELEMENTWISE_BLOCK_BYTES = 1024 * 1024

WEIGHTS = ['g_pre', 'w_in', 's5_a_re', 's5_a_im', 's5_log_dt', 's5_b_re', 's5_b_im', 's5_c_re', 's5_c_im',
           's5_d', 'w_glu', 'w_bs', 'conv_w', 'conv_b', 'lru_w_a', 'lru_b_a', 'lru_w_x', 'lru_b_x',
           'lru_lambda', 'w_bl', 'w_out', 'g_post', 'w_ple', 'w_ple_gate']
SHARDED = {'w_in': 2, 'w_glu': 2, 'w_bs': 2, 'conv_w': 2, 'w_bl': 1, 'w_out': 1, 'w_ple': 2, 'w_ple_gate': 1}
GATHER_BF16 = ['w_in', 'w_glu', 'w_bs', 'w_bl', 'w_out', 'w_ple', 'w_ple_gate']
REPLICATED = [n for n in WEIGHTS if n not in SHARDED]


def _sig(x):
    return 0.5 * jnp.tanh(0.5 * x) + 0.5


def _gelu_parts(x):
    k = math.sqrt(2.0 / math.pi)
    t = jnp.tanh(k * (x + 0.044715 * x * x * x))
    return t, k


def _gelu(x):
    t, _ = _gelu_parts(x)
    return 0.5 * x * (1.0 + t)


def _gelu_grad(x):
    t, k = _gelu_parts(x)
    return 0.5 * (1.0 + t) + 0.5 * x * (1.0 - t * t) * k * (1.0 + 3.0 * 0.044715 * x * x)


def _one_minus_sq(a, log_a):
    z = 2.0 * log_a
    series = -z * (1.0 + z * (0.5 + z * (1.0 / 6.0 + z * (1.0 / 24.0 + z * (1.0 / 120.0)))))
    return jnp.where(z > -0.05, series, 1.0 - a * a)


def _softplus_neg(lam):
    return jnp.maximum(-lam, 0.0) + jnp.log(1.0 + jnp.exp(-jnp.abs(lam)))


def _dot(a, b):
    return jnp.dot(a, b, preferred_element_type=f32)


def _dot_nt(a, b):
    return lax.dot_general(a, b, (((1,), (1,)), ((), ())), preferred_element_type=f32)


def _dot_tn(a, b):
    return lax.dot_general(a, b, (((0,), (0,)), ((), ())), preferred_element_type=f32)


def _S(shape, dtype=f32):
    return jax.ShapeDtypeStruct(shape, dtype)


def _full(shape):
    nd = len(shape)
    return pl.BlockSpec(shape, lambda *_: (0,) * nd)


def _rows(tile, width, col=0):
    return pl.BlockSpec((tile, width), lambda i: (i, col))


def _cp(*semantics):
    return pltpu.CompilerParams(dimension_semantics=semantics or None, vmem_limit_bytes=VMEM_LIMIT_BYTES)


def _tile(n, want):
    t = min(n, want)
    assert n % t == 0, (n, want)
    return t


def _row_tile(R, C=LANES):
    cap = max(SUBLANES, min(R, ELEMENTWISE_BLOCK_BYTES // (4 * C)))
    for t in range(cap - cap % SUBLANES, 0, -SUBLANES):
        if R % t == 0:
            return t
    return R


def _lanes(j):
    return slice(LANES * j, LANES * (j + 1))


def _step_rows(k, n=SUBLANES):
    return pl.ds(pl.multiple_of(k * n, n), n)


def to_scan_order(a):
    T, C = a.shape
    tc = _tile(T, SCAN_CHUNK)
    return a.reshape(T // tc, SEGS, tc // SEGS, C).transpose(0, 2, 1, 3).reshape(T, C)


def from_scan_order(a):
    T, C = a.shape
    tc = _tile(T, SCAN_CHUNK)
    return a.reshape(T // tc, tc // SEGS, SEGS, C).transpose(0, 2, 1, 3).reshape(T, C)


def _resident(shape):
    nd = len(shape)
    return pl.BlockSpec(shape, lambda *_: (0,) * nd, pipeline_mode=pl.Buffered(1))


def mm_tn(a, bs, name):
    M, K = a.shape
    tm = _tile(M, MM_TILE_M)
    nb = len(bs)

    def body(a_ref, *refs):
        m = pl.program_id(0)
        av = a_ref[...]
        for b_ref, o_ref in zip(refs[:nb], refs[nb:]):
            _accumulate(o_ref, _dot_tn(av, b_ref[...]), m)

    outs = pl.pallas_call(
        body, name=name, grid=(M // tm,),
        in_specs=[_rows(tm, K)] + [_rows(tm, b.shape[1]) for b in bs],
        out_specs=[_full((K, b.shape[1])) for b in bs],
        out_shape=[_S((K, b.shape[1])) for b in bs],
        compiler_params=_cp("arbitrary"),
    )(a, *bs)
    return list(outs)


IN_PROJ_WIDTHS = (2 * S5_WIDTH, LRU_WIDTH, LRU_WIDTH, 2 * D_MODEL)


def in_proj_fwd(x, g, win, name):
    T = x.shape[0]
    tm = _tile(T, MM_TILE_M // 2)
    offs = [sum(IN_PROJ_WIDTHS[:k]) for k in range(len(IN_PROJ_WIDTHS))]

    def body(x_ref, g_ref, w_ref, h_ref, *out_refs):
        xv = x_ref[...]
        r = lax.rsqrt(jnp.mean(xv * xv, axis=-1, keepdims=True) + NORM_EPS)
        h = (xv * r * g_ref[...]).astype(bf16)
        h_ref[...] = h
        for o_ref, off, wd in zip(out_refs, offs, IN_PROJ_WIDTHS):
            o_ref[...] = _dot(h, w_ref[:, off:off + wd])

    return pl.pallas_call(
        body, name=name, grid=(T // tm,),
        in_specs=[_rows(tm, D_MODEL), _full((1, D_MODEL)), _resident(win.shape)],
        out_specs=[_rows(tm, D_MODEL)] + [_rows(tm, wd) for wd in IN_PROJ_WIDTHS],
        out_shape=[_S((T, D_MODEL), bf16)] + [_S((T, wd)) for wd in IN_PROJ_WIDTHS],
        compiler_params=_cp("parallel"),
    )(x, g, win)


def _s5_discretise(a_re, a_im, log_dt, b_re_t, b_im_t):
    dt = jnp.exp(log_dt)
    mag = jnp.exp(a_re * dt)
    ab_re = mag * jnp.cos(a_im * dt)
    ab_im = mag * jnp.sin(a_im * dt)
    den = a_re * a_re + a_im * a_im
    nr, ni = ab_re - 1.0, ab_im
    z_re = (nr * a_re + ni * a_im) / den
    z_im = (ni * a_re - nr * a_im) / den
    bb_re = z_re[None] * b_re_t - z_im[None] * b_im_t
    bb_im = z_re[None] * b_im_t + z_im[None] * b_re_t
    return ab_re, ab_im, bb_re, bb_im


def s5_prep(a_re, a_im, log_dt, b_re_t, b_im_t, m, name):
    G, N = a_re.shape

    def body(are_ref, aim_ref, ldt_ref, bre_ref, bim_ref, ab_ref, pw_ref, bb_ref):
        are, aim, ldt = are_ref[...], aim_ref[...], ldt_ref[...]
        ab_re, ab_im, bb_re, bb_im = _s5_discretise(are, aim, ldt, bre_ref[...], bim_ref[...])
        ab_ref[0], ab_ref[1] = ab_re, ab_im
        bb_ref[0], bb_ref[1] = bb_re, bb_im
        dt = jnp.exp(ldt)
        for k in range(m):
            mag = jnp.exp(are * dt * (k + 1.0))
            pw_ref[0, k] = mag * jnp.cos(aim * dt * (k + 1.0))
            pw_ref[1, k] = mag * jnp.sin(aim * dt * (k + 1.0))

    return pl.pallas_call(
        body, name=name,
        out_shape=[_S((2, G, N)), _S((2, m, G, N)), _S((2, S5_GROUP, G, N))], compiler_params=_cp(),
    )(a_re, a_im, log_dt, b_re_t, b_im_t)


def s5_prep_bwd(a_re, a_im, log_dt, b_re_t, b_im_t, g_ab, g_bb, name):
    G, N = a_re.shape

    def body(are_ref, aim_ref, ldt_ref, bre_ref, bim_ref, gab_ref, gbb_ref, o_are, o_aim, o_ldt, o_bre, o_bim):
        _, vjp = jax.vjp(_s5_discretise, are_ref[...], aim_ref[...], ldt_ref[...], bre_ref[...], bim_ref[...])
        g_are, g_aim, g_ldt, g_bre, g_bim = vjp((gab_ref[0], gab_ref[1], gbb_ref[0], gbb_ref[1]))
        o_are[...], o_aim[...], o_ldt[...], o_bre[...], o_bim[...] = g_are, g_aim, g_ldt, g_bre, g_bim

    return pl.pallas_call(
        body, name=name,
        out_shape=[_S((G, N)), _S((G, N)), _S((G, 1)), _S((S5_GROUP, G, N)), _S((S5_GROUP, G, N))],
        compiler_params=_cp(),
    )(a_re, a_im, log_dt, b_re_t, b_im_t, g_ab, g_bb)


NB_S5 = S5_NS // LANES
CB_S5 = S5_WIDTH // LANES
SB_PER_CB = NB_S5 // CB_S5
GRP_PER_SB = LANES // S5_STATE
S5_JB = 8


def _bdb_mask():
    j = jnp.arange(NB_S5)
    own_rows = (j[:, None] % SB_PER_CB == jnp.arange(SB_PER_CB)[None, :]).astype(f32)
    eye = jnp.eye(GRP_PER_SB, dtype=f32)
    return own_rows[:, :, None, None, None, None, None] * eye[None, None, :, None, None, :, None]


def _pack_bdb(bb):
    v = jnp.transpose(bb.reshape(2, S5_GROUP, NB_S5, GRP_PER_SB, S5_STATE), (2, 3, 1, 0, 4))
    full = v[:, None, :, :, :, None, :] * _bdb_mask()
    return full.reshape(NB_S5, LANES, 2 * LANES)


def _unpack_bdb(g_bdb):
    g7 = g_bdb.reshape(NB_S5, SB_PER_CB, GRP_PER_SB, S5_GROUP, 2, GRP_PER_SB, S5_STATE)
    v = jnp.sum(g7 * _bdb_mask(), axis=(1, 5))
    return jnp.transpose(v, (3, 2, 0, 1, 4)).reshape(2, S5_GROUP, S5_GROUPS, S5_STATE)


def _pack_cdb(c_re, c_im):
    gl = S5_GROUPS // CB_S5
    c2 = jnp.stack([c_re, -c_im]).reshape(2, CB_S5, gl, S5_GROUP, S5_STATE)
    eye = jnp.eye(gl, dtype=f32)
    full = jnp.transpose(c2, (1, 0, 2, 4, 3))[:, :, :, :, None, :] * eye[None, None, :, None, :, None]
    return full.reshape(CB_S5, 2 * SB_PER_CB * LANES, LANES)


def _unpack_cdb(g_cdb):
    gl = S5_GROUPS // CB_S5
    g6 = g_cdb.reshape(CB_S5, 2, gl, S5_STATE, gl, S5_GROUP)
    eye = jnp.eye(gl, dtype=f32)
    v = jnp.sum(g6 * eye[None, None, :, None, :, None], axis=4)
    v = jnp.transpose(v, (1, 0, 2, 4, 3)).reshape(2, S5_GROUPS, S5_GROUP, S5_STATE)
    return v[0], -v[1]


def _state_cat(ref, c):
    w = SB_PER_CB * LANES
    return jnp.concatenate([ref[:, w * c:w * (c + 1)], ref[:, S5_NS + w * c:S5_NS + w * (c + 1)]], axis=1)


def _state_pair(ref, j):
    return jnp.concatenate([ref[:, _lanes(j)], ref[:, S5_NS + LANES * j:S5_NS + LANES * (j + 1)]], axis=1)


def s5_fwd(usg, bdb, cdb, dvec, abar_b, ptab_b, name, gather=()):
    T = usg.shape[0]
    tc = _tile(T, SCAN_CHUNK)
    m = tc // SEGS
    nsteps = T // tc
    ng = len(gather)
    assert ptab_b.shape == (2, m, SEGS, S5_NS) and m % 2 == 0

    def body(*refs):
        u_ref, bdb_ref, cdb_ref, d_ref, a_ref, p_ref = refs[:6]
        ys_ref, sre_ref, sim_ref, sbf_ref = refs[6 + ng:10 + ng]
        src_re, src_im, dst_re, dst_im, cin_ref, carry_ref = refs[10 + 2 * ng:16 + 2 * ng]
        i = pl.program_id(0)
        if ng:
            phases = _gather_phases([s.shape for s in gather], refs[6:6 + ng], refs[10 + ng:10 + 2 * ng],
                                    *refs[16 + 2 * ng:])
            for phase, step in zip(phases, (0, nsteps // 2, (3 * nsteps) // 4, nsteps - 1)):
                pl.when(i == step)(phase)

        @pl.when(i == 0)
        def _():
            carry_ref[...] = jnp.zeros_like(carry_ref)

        u = u_ref[...]
        ub = u.astype(bf16)
        for j in range(NB_S5):
            bu = _dot(ub[:, _lanes(j // SB_PER_CB)], bdb_ref[j])
            src_re[:, _lanes(j)] = bu[:, :LANES]
            src_im[:, _lanes(j)] = bu[:, LANES:]
        for j0 in range(0, NB_S5, S5_JB):
            def kstep(k, st):
                rows = _step_rows(k)
                out = []
                for q in range(S5_JB):
                    ln = _lanes(j0 + q)
                    sr, si = st[2 * q], st[2 * q + 1]
                    ar, ai = a_ref[0, :, ln], a_ref[1, :, ln]
                    nr = ar * sr - ai * si + src_re[rows, ln]
                    ni = ar * si + ai * sr + src_im[rows, ln]
                    dst_re[rows, ln] = nr
                    dst_im[rows, ln] = ni
                    out += [nr, ni]
                return tuple(out)

            ends = lax.fori_loop(0, m, kstep, tuple(jnp.zeros((SEGS, LANES), f32) for _ in range(2 * S5_JB)))
            for q in range(S5_JB):
                ln = _lanes(j0 + q)
                er, ei = ends[2 * q], ends[2 * q + 1]
                cr, ci = carry_ref[0, :, ln], carry_ref[1, :, ln]
                amr, ami = p_ref[0, m - 1, 0:1, ln], p_ref[1, m - 1, 0:1, ln]
                rows_r, rows_i = [], []
                for s in range(SEGS):
                    rows_r.append(cr)
                    rows_i.append(ci)
                    cr, ci = (er[s:s + 1, :] + amr * cr - ami * ci, ei[s:s + 1, :] + amr * ci + ami * cr)
                cin_ref[0, 0:SEGS, ln] = _stack_rows(rows_r)
                cin_ref[1, 0:SEGS, ln] = _stack_rows(rows_i)
                carry_ref[0, :, ln] = cr
                carry_ref[1, :, ln] = ci
        cin_ref[:, SEGS:, :] = cin_ref[:, 0:SEGS, :]

        def fix(k2, _):
            rows = _step_rows(k2, PAIR)
            pr = p_ref[0, pl.ds(2 * k2, 2)].reshape(PAIR, S5_NS)
            pi = p_ref[1, pl.ds(2 * k2, 2)].reshape(PAIR, S5_NS)
            cr, ci = cin_ref[0], cin_ref[1]
            sr = dst_re[rows, :] + pr * cr - pi * ci
            si = dst_im[rows, :] + pr * ci + pi * cr
            sre_ref[rows, :] = sr
            sim_ref[rows, :] = si
            sbf_ref[rows, 0:S5_NS] = sr.astype(bf16)
            sbf_ref[rows, S5_NS:] = si.astype(bf16)
            return 0

        lax.fori_loop(0, m // 2, fix, 0)
        for c in range(CB_S5):
            ys_ref[:, _lanes(c)] = _dot(_state_cat(sbf_ref, c), cdb_ref[c]) + d_ref[:, _lanes(c)] * u[:, _lanes(c)]

    st = lambda w: _rows(tc, w)
    outs = pl.pallas_call(
        body, name=name, grid=(nsteps,),
        in_specs=[_rows(tc, S5_WIDTH, 0), _resident(bdb.shape), _resident(cdb.shape), _full((1, S5_WIDTH)),
                  _resident((2, SEGS, S5_NS)), _resident((2, m, SEGS, S5_NS))] + [ANY] * ng,
        out_specs=[st(S5_WIDTH), st(S5_NS), st(S5_NS), st(2 * S5_NS)] + [ANY] * ng,
        out_shape=[_S((T, S5_WIDTH)), _S((T, S5_NS)), _S((T, S5_NS)), _S((T, 2 * S5_NS), bf16)] + (
            _gather_out_shapes(gather) if ng else []),
        scratch_shapes=[pltpu.VMEM((tc, S5_NS), f32)] * 4 + [pltpu.VMEM((2, PAIR, S5_NS), f32),
                                                             pltpu.VMEM((2, 1, S5_NS), f32)] + (
            _gather_semaphores(ng) if ng else []),
        compiler_params=_cp("arbitrary"),
    )(usg, bdb, cdb, dvec, abar_b, ptab_b, *gather)
    return outs[:4], list(outs[4:])


def s5_bwd(gys, usg, s_re, s_im, s_bf, bdb, cdb, dvec, abar_b, ptab_rev_b, name, exchange=()):
    T = gys.shape[0]
    tc = _tile(T, SCAN_CHUNK)
    m = tc // SEGS
    nch = T // tc
    hb = tc // SUBLANES
    ne = len(exchange)

    def body(*refs):
        (gy_ref, u_ref, sre_ref, sim_ref, hre_ref, him_ref, sbf_ref, bdb_ref, cdb_ref, d_ref, a_ref, p_ref) = refs[:12]
        gu_ref, gab_ref, gd_ref, gbdb_ref, gcdb_ref = refs[12 + ne:17 + ne]
        src_re, src_im, dst_re, dst_im, lam_ref, cin_ref, acc_ref, carry_ref = refs[17 + 2 * ne:25 + 2 * ne]
        i = pl.program_id(0)
        if ne:
            start, finish = _chips_phases(refs[12:12 + ne], refs[17 + ne:17 + 2 * ne], *refs[25 + 2 * ne:])
            pl.when(i == 0)(start)
            pl.when(i == nch - 1)(finish)

        @pl.when(i == 0)
        def _():
            carry_ref[...] = jnp.zeros_like(carry_ref)
            for ref in (gab_ref, gd_ref, gbdb_ref, gcdb_ref):
                ref[...] = jnp.zeros_like(ref)

        first = i == nch - 1
        gy = gy_ref[...]
        gyb = gy.astype(bf16)
        u = u_ref[...]
        ub = u.astype(bf16)
        w = SB_PER_CB * LANES
        for c in range(CB_S5):
            gs = _dot_nt(gyb[:, _lanes(c)], cdb_ref[c])
            src_re[:, w * c:w * (c + 1)] = gs[:, :w]
            src_im[:, w * c:w * (c + 1)] = gs[:, w:]
            gcdb_ref[c] += _dot_tn(_state_cat(sbf_ref, c), gyb[:, _lanes(c)])
        for j0 in range(0, NB_S5, S5_JB):
            def kstep(kk, st):
                rows = _step_rows(m - 1 - kk)
                out = []
                for q in range(S5_JB):
                    ln = _lanes(j0 + q)
                    lr, li = st[2 * q], st[2 * q + 1]
                    ar, ai = a_ref[0, :, ln], a_ref[1, :, ln]
                    nr = ar * lr + ai * li + src_re[rows, ln]
                    ni = ar * li - ai * lr + src_im[rows, ln]
                    dst_re[rows, ln] = nr
                    dst_im[rows, ln] = ni
                    out += [nr, ni]
                return tuple(out)

            ends = lax.fori_loop(0, m, kstep, tuple(jnp.zeros((SEGS, LANES), f32) for _ in range(2 * S5_JB)))
            for q in range(S5_JB):
                ln = _lanes(j0 + q)
                er, ei = ends[2 * q], ends[2 * q + 1]
                cr, ci = carry_ref[0, :, ln], carry_ref[1, :, ln]
                amr, ami = p_ref[0, 0, 0:1, ln], p_ref[1, 0, 0:1, ln]
                rows_r, rows_i = [None] * SEGS, [None] * SEGS
                for s in reversed(range(SEGS)):
                    rows_r[s], rows_i[s] = cr, ci
                    cr, ci = (er[s:s + 1, :] + amr * cr + ami * ci, ei[s:s + 1, :] + amr * ci - ami * cr)
                cin_ref[0, 0:SEGS, ln] = _stack_rows(rows_r)
                cin_ref[1, 0:SEGS, ln] = _stack_rows(rows_i)
                carry_ref[0, :, ln] = cr
                carry_ref[1, :, ln] = ci
        cin_ref[:, SEGS:, :] = cin_ref[:, 0:SEGS, :]
        acc_ref[...] = jnp.zeros_like(acc_ref)

        def fix_rows(rows, k2, prev_re, prev_im):
            pr = p_ref[0, pl.ds(2 * k2, 2)].reshape(PAIR, S5_NS)
            pi = p_ref[1, pl.ds(2 * k2, 2)].reshape(PAIR, S5_NS)
            cr, ci = cin_ref[0], cin_ref[1]
            lr = dst_re[rows, :] + pr * cr + pi * ci
            li = dst_im[rows, :] + pr * ci - pi * cr
            lam_ref[rows, 0:S5_NS] = lr.astype(bf16)
            lam_ref[rows, S5_NS:] = li.astype(bf16)
            acc_ref[0] += lr * prev_re + li * prev_im
            acc_ref[1] += li * prev_re - lr * prev_im

        last = slice(tc - SUBLANES, tc)
        wrap_re = _down_a_segment(sre_ref[last, :], jnp.where(first, 0.0, hre_ref[SUBLANES - 1:SUBLANES, :]))
        wrap_im = _down_a_segment(sim_ref[last, :], jnp.where(first, 0.0, him_ref[SUBLANES - 1:SUBLANES, :]))
        fix_rows(pl.ds(0, PAIR), 0, jnp.concatenate([wrap_re, sre_ref[0:SUBLANES, :]], axis=0),
                 jnp.concatenate([wrap_im, sim_ref[0:SUBLANES, :]], axis=0))

        def fix(k2, _):
            prev = pl.ds(pl.multiple_of(k2 * PAIR - SUBLANES, SUBLANES), PAIR)
            fix_rows(_step_rows(k2, PAIR), k2, sre_ref[prev, :], sim_ref[prev, :])
            return 0

        lax.fori_loop(1, m // 2, fix, 0)
        gab_ref[0] += jnp.sum(acc_ref[0], axis=0, keepdims=True)
        gab_ref[1] += jnp.sum(acc_ref[1], axis=0, keepdims=True)
        for c in range(CB_S5):
            x = gy[:, _lanes(c)] * d_ref[:, _lanes(c)]
            for j in range(SB_PER_CB * c, SB_PER_CB * (c + 1)):
                pair = _state_pair(lam_ref, j)
                x = x + _dot_nt(pair, bdb_ref[j])
                gbdb_ref[j] += _dot_tn(ub[:, _lanes(c)], pair)
            gu_ref[:, _lanes(c)] = x.astype(bf16)
        gd_ref[...] += jnp.sum(gy * u, axis=0, keepdims=True)

    rev = lambda i: (nch - 1 - i, 0)
    halo = lambda i: (jnp.maximum((nch - 1 - i) * hb - 1, 0), 0)
    blk = lambda wd: pl.BlockSpec((tc, wd), rev)
    outs = pl.pallas_call(
        body, name=name, grid=(nch,),
        in_specs=[blk(S5_WIDTH), blk(S5_WIDTH), blk(S5_NS), blk(S5_NS),
                  pl.BlockSpec((SUBLANES, S5_NS), halo), pl.BlockSpec((SUBLANES, S5_NS), halo), blk(2 * S5_NS),
                  _resident(bdb.shape), _resident(cdb.shape), _full((1, S5_WIDTH)),
                  _resident((2, SEGS, S5_NS)), _resident((2, m, SEGS, S5_NS))] + [ANY] * ne,
        out_specs=[blk(S5_WIDTH), _full((2, 1, S5_NS)), _full((1, S5_WIDTH)), _full(bdb.shape), _full(cdb.shape)]
        + [ANY] * ne,
        out_shape=[_S((T, S5_WIDTH), bf16), _S((2, 1, S5_NS)), _S((1, S5_WIDTH)), _S(bdb.shape), _S(cdb.shape)] + (
            _chips_out_shapes(exchange) if ne else []),
        scratch_shapes=[pltpu.VMEM((tc, S5_NS), f32)] * 4 + [
            pltpu.VMEM((tc, 2 * S5_NS), bf16), pltpu.VMEM((2, PAIR, S5_NS), f32), pltpu.VMEM((2, PAIR, S5_NS), f32),
            pltpu.VMEM((2, 1, S5_NS), f32)] + (_chips_semaphores(ne) if ne else []),
        compiler_params=_cp("arbitrary"),
    )(gys, usg, s_re, s_im, s_re, s_im, s_bf, bdb, cdb, dvec, abar_b, ptab_rev_b, *exchange)
    return outs[:5], list(outs[5:])


def s5_post_fwd(ys, usg, wglu, wbs, name):
    T = ys.shape[0]
    tm = _tile(T, TOKEN_TILE_LIGHT)

    def body(ys_ref, sg_ref, wglu_ref, wbs_ref, glu_ref, zs_ref):
        glu = _dot(_gelu(ys_ref[...]).astype(bf16), wglu_ref[...])
        sg = sg_ref[...]
        y2 = glu[:, :S5_WIDTH] * _sig(glu[:, S5_WIDTH:]) * (sg * _sig(sg))
        glu_ref[...] = glu
        zs_ref[...] = _dot(y2.astype(bf16), wbs_ref[...])

    return pl.pallas_call(
        body, name=name, grid=(T // tm,),
        in_specs=[_rows(tm, S5_WIDTH), _rows(tm, S5_WIDTH, 1), _resident((S5_WIDTH, 2 * S5_WIDTH)),
                  _resident((S5_WIDTH, D_MODEL))],
        out_specs=[_rows(tm, 2 * S5_WIDTH), _rows(tm, D_MODEL)],
        out_shape=[_S((T, 2 * S5_WIDTH)), _S((T, D_MODEL))], compiler_params=_cp("parallel"),
    )(ys, usg, wglu, wbs)


def _accumulate(ref, part, step):
    @pl.when(step == 0)
    def _():
        ref[...] = part

    @pl.when(step > 0)
    def _():
        ref[...] += part


def s5_post_bwd(gzs, glu, usg, ys, wbs, wglu, name):
    T = ys.shape[0]
    tm = _tile(T, TOKEN_TILE_LIGHT)

    def body(gzs_ref, glu_ref, sg_ref, ys_ref, wbs_ref, wglu_ref, gys_ref, gsg_ref, gwbs_ref, gwglu_ref):
        i = pl.program_id(0)
        glu = glu_ref[...]
        a, b = glu[:, :S5_WIDTH], glu[:, S5_WIDTH:]
        sg = sg_ref[...]
        ys = ys_ref[...]
        sb, ssg = _sig(b), _sig(sg)
        silu = sg * ssg
        _accumulate(gwbs_ref, _dot_tn((a * sb * silu).astype(bf16), gzs_ref[...]), i)
        gy2 = _dot_nt(gzs_ref[...], wbs_ref[...])
        g_a = gy2 * sb * silu
        g_b = gy2 * a * sb * (1.0 - sb) * silu
        gsg_ref[...] = (gy2 * a * sb * ssg * (1.0 + sg * (1.0 - ssg))).astype(bf16)
        gglu = jnp.concatenate([g_a, g_b], axis=1).astype(bf16)
        _accumulate(gwglu_ref, _dot_tn(_gelu(ys).astype(bf16), gglu), i)
        gys_ref[...] = _dot_nt(gglu, wglu_ref[...]) * _gelu_grad(ys)

    return pl.pallas_call(
        body, name=name, grid=(T // tm,),
        in_specs=[_rows(tm, D_MODEL), _rows(tm, 2 * S5_WIDTH), _rows(tm, S5_WIDTH, 1), _rows(tm, S5_WIDTH),
                  _resident((S5_WIDTH, D_MODEL)), _resident((S5_WIDTH, 2 * S5_WIDTH))],
        out_specs=[_rows(tm, S5_WIDTH), _rows(tm, S5_WIDTH), _full((S5_WIDTH, D_MODEL)), _full((S5_WIDTH, 2 * S5_WIDTH))],
        out_shape=[_S((T, S5_WIDTH)), _S((T, S5_WIDTH), bf16), _S((S5_WIDTH, D_MODEL)), _S((S5_WIDTH, 2 * S5_WIDTH))],
        compiler_params=_cp("arbitrary"),
    )(gzs, glu, usg, ys, wbs, wglu)


NB_LRU = LRU_WIDTH // LANES
LRU_JB = 5
TAPS_BACK = CONV_WIDTH - 1
EDGE = TAPS_BACK * SUBLANES
HALO_ROWS = 4 * SUBLANES


def _down_a_segment(blk, entering_row):
    sub = lax.broadcasted_iota(jnp.int32, blk.shape, 0)
    return jnp.where(sub == 0, entering_row, pltpu.roll(blk, 1, 0))


def _up_a_segment(blk, entering_row):
    sub = lax.broadcasted_iota(jnp.int32, blk.shape, 0)
    return jnp.where(sub == SUBLANES - 1, entering_row, pltpu.roll(blk, SUBLANES - 1, 0))


def _stack_rows(rows):
    sub = lax.broadcasted_iota(jnp.int32, (SUBLANES,) + rows[0].shape[1:], 0)
    out = jnp.broadcast_to(rows[0], sub.shape)
    for s in range(1, SUBLANES):
        out = jnp.where(sub == s, rows[s], out)
    return out


def _fill_conv_window(xe, x_ref, xh_ref, is_first, tc):
    xe[EDGE:, :] = x_ref[...]
    for i in range(1, TAPS_BACK + 1):
        row = HALO_ROWS - SUBLANES * i + SUBLANES - 1
        entering = jnp.where(is_first, 0.0, xh_ref[row:row + 1, :])
        blk = x_ref[tc - SUBLANES * i:tc - SUBLANES * (i - 1), :]
        xe[EDGE - SUBLANES * i:EDGE - SUBLANES * (i - 1), :] = _down_a_segment(blk, entering)


def lru_fwd(lx, convw, convb, wa, wx, ba, bx, lam, name):
    T = lx.shape[0]
    tc = _tile(T, SCAN_CHUNK)
    m = tc // SEGS
    hb = tc // HALO_ROWS

    def body(x_ref, xh_ref, cw_ref, cb_ref, wa_ref, wx_ref, ba_ref, bx_ref, lam_ref,
             c_ref, r_ref, i_ref, om_ref, h_ref, xe, src_a, src_b, dst_a, dst_h, cin_ref, carry_ref):
        i = pl.program_id(0)

        @pl.when(i == 0)
        def _():
            carry_ref[...] = jnp.zeros_like(carry_ref)

        _fill_conv_window(xe, x_ref, xh_ref, i == 0, tc)
        c = cb_ref[...] + cw_ref[0:1, :] * xe[0:tc, :]
        for k in range(1, CONV_WIDTH):
            c = c + cw_ref[k:k + 1, :] * xe[SUBLANES * k:SUBLANES * k + tc, :]
        c_ref[...] = c
        sp = _softplus_neg(lam_ref[...])
        for j in range(NB_LRU):
            ln = _lanes(j)
            cj = c[:, ln]
            cjb = cj.astype(bf16)
            r = _sig(_dot(cjb, wa_ref[j]) + ba_ref[:, ln])
            g = _sig(_dot(cjb, wx_ref[j]) + bx_ref[:, ln])
            r_ref[:, ln] = r
            i_ref[:, ln] = g
            log_a = -LRU_C * r * sp[:, ln]
            a = jnp.exp(log_a)
            src_a[:, ln] = a
            om = _one_minus_sq(a, log_a)
            om_ref[:, ln] = om
            src_b[:, ln] = jnp.sqrt(om) * (g * cj)
        for j0 in range(0, NB_LRU, LRU_JB):
            def kstep(k, st):
                rows = _step_rows(k)
                out = []
                for q in range(LRU_JB):
                    ln = _lanes(j0 + q)
                    hh, ac = st[2 * q], st[2 * q + 1]
                    a = src_a[rows, ln]
                    hh = a * hh + src_b[rows, ln]
                    ac = a * ac
                    dst_h[rows, ln] = hh
                    dst_a[rows, ln] = ac
                    out += [hh, ac]
                return tuple(out)

            init = tuple(jnp.zeros((SEGS, LANES), f32) if q % 2 == 0 else jnp.ones((SEGS, LANES), f32)
                         for q in range(2 * LRU_JB))
            ends = lax.fori_loop(0, m, kstep, init)
            for q in range(LRU_JB):
                ln = _lanes(j0 + q)
                eh, ea = ends[2 * q], ends[2 * q + 1]
                cr = carry_ref[:, ln]
                rows_c = []
                for s in range(SEGS):
                    rows_c.append(cr)
                    cr = eh[s:s + 1, :] + ea[s:s + 1, :] * cr
                cin_ref[:, ln] = _stack_rows(rows_c)
                carry_ref[:, ln] = cr

        def fix(k, _):
            rows = _step_rows(k)
            h_ref[rows, :] = dst_h[rows, :] + dst_a[rows, :] * cin_ref[...]
            return 0

        lax.fori_loop(0, m, fix, 0)

    wide = lambda: _rows(tc, LRU_WIDTH)
    buf = lambda rows: pltpu.VMEM((rows, LRU_WIDTH), f32)
    return pl.pallas_call(
        body, name=name, grid=(T // tc,),
        in_specs=[wide(), pl.BlockSpec((HALO_ROWS, LRU_WIDTH), lambda i: (jnp.maximum(i * hb - 1, 0), 0)),
                  _full((CONV_WIDTH, LRU_WIDTH)), _full((1, LRU_WIDTH)),
                  _full((LRU_HEADS, LRU_HEAD_DIM, LRU_HEAD_DIM)), _full((LRU_HEADS, LRU_HEAD_DIM, LRU_HEAD_DIM)),
                  _full((1, LRU_WIDTH)), _full((1, LRU_WIDTH)), _full((1, LRU_WIDTH))],
        out_specs=[wide(), wide(), wide(), wide(), wide()],
        out_shape=[_S((T, LRU_WIDTH))] * 5,
        scratch_shapes=[buf(tc + EDGE), buf(tc), buf(tc), buf(tc), buf(tc), buf(SEGS), buf(1)],
        compiler_params=_cp("arbitrary"),
    )(lx, lx, convw, convb, wa, wx, ba, bx, lam)


def lru_bwd(gh, h, c, r, gi, om, lx, convw, wa, wx, lam, name, exchange=()):
    T = gh.shape[0]
    tc = _tile(T, SCAN_CHUNK)
    m = tc // SEGS
    nch = T // tc
    ne = len(exchange)
    NI = 13

    def body(*refs):
        (gh_ref, h_ref, hh_ref, c_ref, r_ref, i_ref, om_ref, x_ref, xh_ref, cw_ref, wa_ref, wx_ref, lam_ref) = refs[:NI]
        glx_ref, gwa_ref, gwx_ref, gba_ref, gbx_ref, glam_ref, gcb_ref, gcw_ref = refs[NI + ne:NI + 8 + ne]
        (src_a, src_m, dst_a, dst_m, mbuf, hbuf, xe, gce, cin_ref, gcc_ref, carry_ref) = refs[NI + 8 + 2 * ne:NI + 19 + 2 * ne]
        i = pl.program_id(0)
        if ne:
            start, finish = _chips_phases(refs[NI:NI + ne], refs[NI + 8 + ne:NI + 8 + 2 * ne], *refs[NI + 19 + 2 * ne:])
            pl.when(i == 0)(start)
            pl.when(i == nch - 1)(finish)

        @pl.when(i == 0)
        def _():
            carry_ref[...] = jnp.zeros_like(carry_ref)
            gcc_ref[...] = jnp.zeros_like(gcc_ref)
            for ref in (gwa_ref, gwx_ref, gba_ref, gbx_ref, glam_ref, gcb_ref, gcw_ref):
                ref[...] = jnp.zeros_like(ref)

        first = i == nch - 1
        last = slice(tc - SUBLANES, tc)
        hbuf[SUBLANES:, :] = h_ref[...]
        hbuf[0:SUBLANES, :] = _down_a_segment(h_ref[last, :], jnp.where(first, 0.0, hh_ref[SUBLANES - 1:SUBLANES, :]))
        _fill_conv_window(xe, x_ref, xh_ref, first, tc)
        lam_v = lam_ref[...]
        sp = _softplus_neg(lam_v)
        a_all = jnp.exp(-LRU_C * r_ref[...] * sp)
        src_a[...] = a_all
        src_m[...] = a_all * gh_ref[...]
        for j0 in range(0, NB_LRU, LRU_JB):
            def kstep(kk, st):
                rows = _step_rows(m - 1 - kk)
                out = []
                for q in range(LRU_JB):
                    ln = _lanes(j0 + q)
                    mu, ac = st[2 * q], st[2 * q + 1]
                    a = src_a[rows, ln]
                    mu = a * mu + src_m[rows, ln]
                    ac = a * ac
                    dst_m[rows, ln] = mu
                    dst_a[rows, ln] = ac
                    out += [mu, ac]
                return tuple(out)

            init = tuple(jnp.zeros((SEGS, LANES), f32) if q % 2 == 0 else jnp.ones((SEGS, LANES), f32)
                         for q in range(2 * LRU_JB))
            ends = lax.fori_loop(0, m, kstep, init)
            for q in range(LRU_JB):
                ln = _lanes(j0 + q)
                em, ea = ends[2 * q], ends[2 * q + 1]
                cr = carry_ref[:, ln]
                rows_c = [None] * SEGS
                for s in reversed(range(SEGS)):
                    rows_c[s] = cr
                    cr = em[s:s + 1, :] + ea[s:s + 1, :] * cr
                cin_ref[:, ln] = _stack_rows(rows_c)
                carry_ref[:, ln] = cr

        def fix(k, _):
            rows = _step_rows(k)
            mbuf[rows, :] = dst_m[rows, :] + dst_a[rows, :] * cin_ref[...]
            return 0

        lax.fori_loop(0, m, fix, 0)
        mbuf[tc:, :] = _up_a_segment(mbuf[0:SUBLANES, :], cin_ref[SUBLANES - 1:SUBLANES, :])
        sneg = _sig(-lam_v)
        for j in range(NB_LRU):
            ln = _lanes(j)
            lamt = gh_ref[:, ln] + mbuf[SUBLANES:, ln]
            rj, ij, cj = r_ref[:, ln], i_ref[:, ln], c_ref[:, ln]
            a = src_a[:, ln]
            om = om_ref[:, ln]
            inv_mult = lax.rsqrt(om)
            mult = om * inv_mult
            g_a = lamt * hbuf[0:tc, ln]
            g_mult = lamt * ij * cj
            g_i = lamt * mult * cj
            g_c = lamt * mult * ij
            g_log_a = g_a * a - g_mult * a * a * inv_mult
            glam_ref[:, ln] += jnp.sum(g_log_a * rj, axis=0, keepdims=True) * LRU_C * sneg[:, ln]
            g_ra = g_log_a * (-LRU_C) * sp[:, ln] * rj * (1.0 - rj)
            g_ia = g_i * ij * (1.0 - ij)
            gba_ref[:, ln] += jnp.sum(g_ra, axis=0, keepdims=True)
            gbx_ref[:, ln] += jnp.sum(g_ia, axis=0, keepdims=True)
            cjb, grb, gib = cj.astype(bf16), g_ra.astype(bf16), g_ia.astype(bf16)
            gwa_ref[j] += _dot_tn(cjb, grb)
            gwx_ref[j] += _dot_tn(cjb, gib)
            g_c = g_c + _dot_nt(grb, wa_ref[j]) + _dot_nt(gib, wx_ref[j])
            gce[0:tc, ln] = g_c
            gcb_ref[:, ln] += jnp.sum(g_c, axis=0, keepdims=True)
        for d in range(TAPS_BACK):
            blk = slice(SUBLANES * d, SUBLANES * (d + 1))
            gce[tc + SUBLANES * d:tc + SUBLANES * (d + 1), :] = _up_a_segment(gce[blk, :], gcc_ref[SUBLANES * d:SUBLANES * d + 1, :])
        gcc_ref[...] = gce[0:EDGE, :]
        gc = gce[0:tc, :]
        glx = cw_ref[CONV_WIDTH - 1:CONV_WIDTH, :] * gc
        gcw_ref[CONV_WIDTH - 1:CONV_WIDTH, :] += jnp.sum(gc * xe[EDGE:EDGE + tc, :], axis=0, keepdims=True)
        for k in range(CONV_WIDTH - 1):
            off = SUBLANES * (CONV_WIDTH - 1 - k)
            glx = glx + cw_ref[k:k + 1, :] * gce[off:off + tc, :]
            gcw_ref[k:k + 1, :] += jnp.sum(gc * xe[EDGE - off:EDGE - off + tc, :], axis=0, keepdims=True)
        glx_ref[...] = glx.astype(bf16)

    rev = lambda i: (nch - 1 - i, 0)
    halo = lambda rows: (lambda i: (jnp.maximum((nch - 1 - i) * (tc // rows) - 1, 0), 0))
    wide = lambda: pl.BlockSpec((tc, LRU_WIDTH), rev)
    vec = lambda: _full((1, LRU_WIDTH))
    hd = lambda: _full((LRU_HEADS, LRU_HEAD_DIM, LRU_HEAD_DIM))
    buf = lambda rows: pltpu.VMEM((rows, LRU_WIDTH), f32)
    outs = pl.pallas_call(
        body, name=name, grid=(nch,),
        in_specs=[wide(), wide(), pl.BlockSpec((SUBLANES, LRU_WIDTH), halo(SUBLANES)), wide(), wide(), wide(), wide(), wide(),
                  pl.BlockSpec((HALO_ROWS, LRU_WIDTH), halo(HALO_ROWS)), _full((CONV_WIDTH, LRU_WIDTH)), hd(), hd(), vec()]
        + [ANY] * ne,
        out_specs=[wide(), hd(), hd(), vec(), vec(), vec(), vec(), _full((CONV_WIDTH, LRU_WIDTH))] + [ANY] * ne,
        out_shape=[_S((T, LRU_WIDTH), bf16), _S((LRU_HEADS, LRU_HEAD_DIM, LRU_HEAD_DIM)),
                   _S((LRU_HEADS, LRU_HEAD_DIM, LRU_HEAD_DIM)), _S((1, LRU_WIDTH)), _S((1, LRU_WIDTH)),
                   _S((1, LRU_WIDTH)), _S((1, LRU_WIDTH)), _S((CONV_WIDTH, LRU_WIDTH))] + (
            _chips_out_shapes(exchange) if ne else []),
        scratch_shapes=[buf(tc), buf(tc), buf(tc), buf(tc), buf(tc + SUBLANES), buf(tc + SUBLANES), buf(tc + EDGE),
                        buf(tc + EDGE), buf(SEGS), buf(EDGE), buf(1)] + (_chips_semaphores(ne) if ne else []),
        compiler_params=_cp("arbitrary"),
    )(gh, h, h, c, r, gi, om, lx, lx, convw, wa, wx, lam, *exchange)
    return outs[:8], list(outs[8:])


def merge_fwd(h, lg, zs, gsl, x, p, wbl, wout, gpost, wple, wpg, name, target=None):
    T = x.shape[0]
    tm = _tile(T, TOKEN_TILE)
    head = target is not None

    def body(h_ref, lg_ref, zs_ref, gs_ref, gl_ref, x_ref, p_ref, wbl_ref, wout_ref, gp_ref, wple_ref, wpg_ref, *refs):
        zl_ref, mix_ref, xo_ref = refs[head:head + 3]
        lg_v = lg_ref[...]
        yl = h_ref[...] * (lg_v * _sig(lg_v))
        zl = _dot(yl.astype(bf16), wbl_ref[...])
        merged = _sig(gs_ref[...]) * zs_ref[...] + _sig(gl_ref[...]) * zl
        mix = _dot(merged.astype(bf16), wout_ref[...])
        r2 = lax.rsqrt(jnp.mean(mix * mix, axis=-1, keepdims=True) + NORM_EPS)
        x1 = x_ref[...] + mix * r2 * gp_ref[...]
        q = _dot(x1.astype(bf16), wpg_ref[...])
        pe = _dot(p_ref[...].astype(bf16), wple_ref[...])
        zl_ref[...], mix_ref[...] = zl, mix
        x2 = x1 + pe * _sig(q)
        if head:
            e = x2 - refs[0][...]
            xo_ref[...] = e * (1.0 / D_MODEL)
            _accumulate(refs[4], 0.5 * jnp.sum(jnp.sum(e * e, axis=-1, keepdims=True) * (1.0 / D_MODEL), axis=0,
                                               keepdims=True), pl.program_id(0))
        else:
            xo_ref[...] = x2

    dm = lambda: _rows(tm, D_MODEL)
    return pl.pallas_call(
        body, name=name, grid=(T // tm,),
        in_specs=[_rows(tm, LRU_WIDTH), _rows(tm, LRU_WIDTH), dm(), _rows(tm, D_MODEL, 0), _rows(tm, D_MODEL, 1), dm(),
                  _rows(tm, PLE_DIM), _resident((LRU_WIDTH, D_MODEL)), _resident((D_MODEL, D_MODEL)), _full((1, D_MODEL)),
                  _resident((PLE_DIM, D_MODEL)), _resident((D_MODEL, D_MODEL))] + ([dm()] if head else []),
        out_specs=[dm(), dm(), dm()] + ([_full((1, 1))] if head else []),
        out_shape=[_S((T, D_MODEL))] * 3 + ([_S((1, 1))] if head else []),
        compiler_params=_cp("arbitrary" if head else "parallel"),
    )(h, lg, zs, gsl, gsl, x, p, wbl, wout, gpost, wple, wpg, *([target] if head else []))


def post_bwd(gx2, mix, x, p, wpg, wple, gpost, name, exchange=()):
    T = x.shape[0]
    tm = _tile(T, TOKEN_TILE_LIGHT)
    nsteps = T // tm
    ne = len(exchange)

    def body(*refs):
        gx2_ref, mix_ref, x_ref, p_ref, wpg_ref, wple_ref, gp_ref = refs[:7]
        gres_ref, gmix_ref, ggp_ref, gwpg_ref, gwple_ref = refs[7 + ne:12 + ne]
        i = pl.program_id(0)
        if ne:
            start, finish = _sibling_phases(refs[7:7 + ne], refs[12 + ne:12 + 2 * ne], *refs[12 + 2 * ne:])
            pl.when(i == 0)(start)
            pl.when(i == nsteps - 1)(finish)
        gx2 = gx2_ref[...]
        mix = mix_ref[...]
        gp = gp_ref[...]
        r2 = lax.rsqrt(jnp.mean(mix * mix, axis=-1, keepdims=True) + NORM_EPS)
        nrm = mix * r2
        x1b = (x_ref[...] + nrm * gp).astype(bf16)
        pb = p_ref[...].astype(bf16)
        sq = _sig(_dot(x1b, wpg_ref[...]))
        pe = _dot(pb, wple_ref[...])
        gq = (gx2 * pe * sq * (1.0 - sq)).astype(bf16)
        _accumulate(gwple_ref, _dot_tn(pb, (gx2 * sq).astype(bf16)), i)
        _accumulate(gwpg_ref, _dot_tn(x1b, gq), i)
        gx1 = gx2 + _dot_nt(gq, wpg_ref[...])
        gres_ref[...] = gx1
        _accumulate(ggp_ref, jnp.sum(gx1 * nrm, axis=0, keepdims=True), i)
        gy = gx1 * gp
        gmix_ref[...] = (r2 * (gy - nrm * jnp.mean(gy * nrm, axis=-1, keepdims=True))).astype(bf16)

    dm = lambda: _rows(tm, D_MODEL)
    outs = pl.pallas_call(
        body, name=name, grid=(nsteps,),
        in_specs=[dm(), dm(), dm(), _rows(tm, PLE_DIM), _resident((D_MODEL, D_MODEL)), _resident((PLE_DIM, D_MODEL)),
                  _full((1, D_MODEL))] + [ANY] * ne,
        out_specs=[dm(), dm(), _full((1, D_MODEL)), _full((D_MODEL, D_MODEL)), _full((PLE_DIM, D_MODEL))] + [ANY] * ne,
        out_shape=[_S((T, D_MODEL)), _S((T, D_MODEL), bf16), _S((1, D_MODEL)), _S((D_MODEL, D_MODEL)),
                   _S((PLE_DIM, D_MODEL))] + (_sibling_out_shapes(exchange) if ne else []),
        scratch_shapes=_sibling_semaphores(ne) if ne else [],
        compiler_params=_cp("arbitrary"),
    )(gx2, mix, x, p, wpg, wple, gpost, *exchange)
    return outs[:5], list(outs[5:])


def gate_bwd(gmix, zl, zs, gsl, wout, name):
    T = zl.shape[0]
    tm = _tile(T, TOKEN_TILE_LIGHT)

    def body(gmix_ref, zl_ref, zs_ref, gs_ref, gl_ref, wout_ref, gzs_ref, gzl_ref, ggsl_ref, gwout_ref):
        i = pl.program_id(0)
        gmix = gmix_ref[...]
        gmerged = _dot_nt(gmix, wout_ref[...])
        zs, zl = zs_ref[...], zl_ref[...]
        ss, sl = _sig(gs_ref[...]), _sig(gl_ref[...])
        _accumulate(gwout_ref, _dot_tn((ss * zs + sl * zl).astype(bf16), gmix), i)
        gzs_ref[...] = (gmerged * ss).astype(bf16)
        gzl_ref[...] = (gmerged * sl).astype(bf16)
        ggsl_ref[:, :D_MODEL] = (gmerged * zs * ss * (1.0 - ss)).astype(bf16)
        ggsl_ref[:, D_MODEL:] = (gmerged * zl * sl * (1.0 - sl)).astype(bf16)

    dm = lambda: _rows(tm, D_MODEL)
    return pl.pallas_call(
        body, name=name, grid=(T // tm,),
        in_specs=[dm(), dm(), dm(), _rows(tm, D_MODEL, 0), _rows(tm, D_MODEL, 1), _resident((D_MODEL, D_MODEL))],
        out_specs=[dm(), dm(), _rows(tm, 2 * D_MODEL), _full((D_MODEL, D_MODEL))],
        out_shape=[_S((T, D_MODEL), bf16), _S((T, D_MODEL), bf16), _S((T, 2 * D_MODEL), bf16), _S((D_MODEL, D_MODEL))],
        compiler_params=_cp("arbitrary"),
    )(gmix, zl, zs, gsl, gsl, wout)


def lru_out_bwd(gzl, h, lg, wbl, name):
    T = h.shape[0]
    tm = _tile(T, TOKEN_TILE_LIGHT)

    def body(gzl_ref, h_ref, lg_ref, wbl_ref, gh_ref, glg_ref, gwbl_ref):
        i = pl.program_id(0)
        gzl = gzl_ref[...]
        lg_v, hv = lg_ref[...], h_ref[...]
        slg = _sig(lg_v)
        silu = lg_v * slg
        _accumulate(gwbl_ref, _dot_tn((hv * silu).astype(bf16), gzl), i)
        gyl = _dot_nt(gzl, wbl_ref[...])
        gh_ref[...] = gyl * silu
        glg_ref[...] = (gyl * hv * slg * (1.0 + lg_v * (1.0 - slg))).astype(bf16)

    lw = lambda: _rows(tm, LRU_WIDTH)
    return pl.pallas_call(
        body, name=name, grid=(T // tm,),
        in_specs=[_rows(tm, D_MODEL), lw(), lw(), _resident((LRU_WIDTH, D_MODEL))],
        out_specs=[lw(), lw(), _full((LRU_WIDTH, D_MODEL))],
        out_shape=[_S((T, LRU_WIDTH)), _S((T, LRU_WIDTH), bf16), _S((LRU_WIDTH, D_MODEL))],
        compiler_params=_cp("arbitrary"),
    )(gzl, h, lg, wbl)


def in_proj_bwd(pieces, win, x, gres, g, name):
    T = x.shape[0]
    tm = _tile(T, MM_TILE_M // 2)
    widths = [pc.shape[1] for pc in pieces]
    offs = [sum(widths[:k]) for k in range(len(widths))]

    def body(*refs):
        pc_refs = refs[:len(widths)]
        w_ref, x_ref, gres_ref, g_ref, gx_ref, gg_ref = refs[len(widths):]
        i = pl.program_id(0)
        ghv = _dot_nt(pc_refs[0][...], w_ref[:, offs[0]:offs[0] + widths[0]])
        for k in range(1, len(widths)):
            ghv = ghv + _dot_nt(pc_refs[k][...], w_ref[:, offs[k]:offs[k] + widths[k]])
        xv = x_ref[...]
        r = lax.rsqrt(jnp.mean(xv * xv, axis=-1, keepdims=True) + NORM_EPS)
        nrm = xv * r
        gy = ghv * g_ref[...]
        gx_ref[...] = gres_ref[...] + r * (gy - nrm * jnp.mean(gy * nrm, axis=-1, keepdims=True))
        _accumulate(gg_ref, jnp.sum(ghv * nrm, axis=0, keepdims=True), i)

    dm = lambda: _rows(tm, D_MODEL)
    return pl.pallas_call(
        body, name=name, grid=(T // tm,),
        in_specs=[_rows(tm, wd) for wd in widths] + [_resident(win.shape), dm(), dm(), _full((1, D_MODEL))],
        out_specs=[dm(), _full((1, D_MODEL))],
        out_shape=[_S((T, D_MODEL)), _S((1, D_MODEL))], compiler_params=_cp("arbitrary"),
    )(*pieces, win, x, gres, g)


def _s5_operands(w, m, tag):
    b_re_t = jnp.transpose(w['s5_b_re'], (2, 0, 1))
    b_im_t = jnp.transpose(w['s5_b_im'], (2, 0, 1))
    ldt = w['s5_log_dt'][:, None]
    ab, pw, bb = s5_prep(w['s5_a_re'], w['s5_a_im'], ldt, b_re_t, b_im_t, m, "s5_prep" + tag)
    over_sublanes = lambda t: jnp.broadcast_to(t[..., None, :], t.shape[:-1] + (SEGS, S5_NS))
    ptab = pw.reshape(2, m, S5_NS)
    return dict(abar_b=over_sublanes(ab.reshape(2, S5_NS)), ptab_b=over_sublanes(ptab),
                ptab_rev_b=over_sublanes(ptab[:, ::-1, :]), bdb=_pack_bdb(bb).astype(bf16),
                cdb=_pack_cdb(w['s5_c_re'], w['s5_c_im']).astype(bf16), dvec=w['s5_d'][None, :],
                prep_in=(w['s5_a_re'], w['s5_a_im'], ldt, b_re_t, b_im_t))


def layer_fwd(x, p, w, tag, gather=(), target=None):
    T = x.shape[0]
    m = _tile(T, SCAN_CHUNK) // SEGS
    s5 = _s5_operands(w, m, tag)
    h_bf, usg, lx, lg, gsl = in_proj_fwd(x, w['g_pre'][None, :], w['w_in'], "in_proj_fwd" + tag)
    (ys, s_re, s_im, s_bf), gathered = s5_fwd(usg, s5['bdb'], s5['cdb'], s5['dvec'], s5['abar_b'], s5['ptab_b'],
                                              "s5_fwd" + tag, gather)
    glu, zs = s5_post_fwd(ys, usg, w['w_glu'], w['w_bs'], "s5_post_fwd" + tag)
    wa, wx = w['lru_w_a'].astype(bf16), w['lru_w_x'].astype(bf16)
    c, r, gi, om, hs = lru_fwd(lx, w['conv_w'], w['conv_b'][None, :], wa, wx, w['lru_b_a'][None, :],
                               w['lru_b_x'][None, :], w['lru_lambda'][None, :], "lru_fwd" + tag)
    zl, mix, x_out, *loss_part = merge_fwd(hs, lg, zs, gsl, x, p, w['w_bl'], w['w_out'], w['g_post'][None, :],
                                           w['w_ple'], w['w_ple_gate'], "merge_fwd" + tag, target)
    saved = dict(x=x, p=p, h_bf=h_bf, usg=usg, lx=lx, lg=lg, gsl=gsl, ys=ys, s_re=s_re, s_im=s_im, s_bf=s_bf, glu=glu,
                 zs=zs, c=c, r=r, gi=gi, om=om, hs=hs, zl=zl, mix=mix, s5=s5, wa=wa, wx=wx)
    return x_out, saved, gathered, (loss_part[0] if loss_part else None)


def layer_bwd(gx_out, w, sv, tag, reduce_job=None):
    s5 = sv['s5']
    g = {}
    (gres, gmix, g_gpost, g['w_ple_gate'], g['w_ple']), theirs = post_bwd(
        gx_out, sv['mix'], sv['x'], sv['p'], w['w_ple_gate'], w['w_ple'], w['g_post'][None, :], "post_bwd" + tag,
        reduce_job[0] if reduce_job else ())
    parts, wire = reduce_job[1](theirs) if reduce_job else ((), ())
    gzs, gzl, ggsl, g['w_out'] = gate_bwd(gmix, sv['zl'], sv['zs'], sv['gsl'], w['w_out'], "gate_bwd" + tag)
    g_h, g_lg, g['w_bl'] = lru_out_bwd(gzl, sv['hs'], sv['lg'], w['w_bl'], "lru_out_bwd" + tag)
    g['g_post'] = g_gpost[0]
    (g_lx, g_wa, g_wx, g_ba, g_bx, g_lam, g_cb, g_cw), _ = lru_bwd(
        g_h, sv['hs'], sv['c'], sv['r'], sv['gi'], sv['om'], sv['lx'], w['conv_w'], sv['wa'], sv['wx'],
        w['lru_lambda'][None, :], "lru_bwd" + tag)
    g['lru_w_a'], g['lru_w_x'] = g_wa, g_wx
    g['lru_b_a'], g['lru_b_x'], g['lru_lambda'], g['conv_b'], g['conv_w'] = g_ba[0], g_bx[0], g_lam[0], g_cb[0], g_cw
    g_ys, g_sg, g['w_bs'], g['w_glu'] = s5_post_bwd(gzs, sv['glu'], sv['usg'], sv['ys'], w['w_bs'], w['w_glu'],
                                                    "s5_post_bwd" + tag)
    (g_u, g_ab, g_d, g_bdb, g_cdb), exchanged = s5_bwd(
        g_ys, sv['usg'], sv['s_re'], sv['s_im'], sv['s_bf'], s5['bdb'], s5['cdb'], s5['dvec'], s5['abar_b'],
        s5['ptab_rev_b'], "s5_bwd" + tag, wire)
    g['s5_d'] = g_d[0]
    g['s5_c_re'], g['s5_c_im'] = _unpack_cdb(g_cdb)
    g_are, g_aim, g_ldt, g_bre_t, g_bim_t = s5_prep_bwd(*s5['prep_in'], g_ab.reshape(2, S5_GROUPS, S5_STATE),
                                                       _unpack_bdb(g_bdb), "s5_prep_bwd" + tag)
    g['s5_a_re'], g['s5_a_im'], g['s5_log_dt'] = g_are, g_aim, g_ldt
    g['s5_b_re'] = jnp.transpose(g_bre_t, (1, 2, 0))
    g['s5_b_im'] = jnp.transpose(g_bim_t, (1, 2, 0))
    pieces = [g_u, g_sg, g_lx, g_lg, ggsl]
    g['w_in'] = jnp.concatenate(mm_tn(sv['h_bf'], pieces[:3], "gw_in_a" + tag) + mm_tn(sv['h_bf'], pieces[3:], "gw_in_b" + tag),
                                axis=1)
    gx, g_gpre = in_proj_bwd(pieces, w['w_in'], sv['x'], gres, w['g_pre'][None, :], "in_proj_bwd" + tag)
    g['g_pre'] = g_gpre[0]
    return gx, g, (parts, exchanged)


def _as_2d(a):
    return a.reshape((-1, a.shape[-1])) if a.ndim > 1 else a.reshape((1, -1))


def _adamw_update(w, gv, m, v):
    nm = ADAM_B1 * m + (1.0 - ADAM_B1) * gv
    nv = ADAM_B2 * v + (1.0 - ADAM_B2) * (gv * gv)
    bc1 = 1.0 - ADAM_B1 ** ADAM_STEP
    bc2 = 1.0 - ADAM_B2 ** ADAM_STEP
    return -ADAM_LR * ((nm / bc1) / (jnp.sqrt(nv / bc2) + ADAM_EPS) + ADAM_WD * w), nm, nv


def adamw(w, g, m, v, name):
    shape = w.shape
    w2, g2, m2, v2 = _as_2d(w), _as_2d(g), _as_2d(m), _as_2d(v)
    R, C = w2.shape
    tr = _row_tile(R, C)

    def body(w_ref, g_ref, m_ref, v_ref, d_ref, nm_ref, nv_ref):
        d_ref[...], nm_ref[...], nv_ref[...] = _adamw_update(w_ref[...], g_ref[...], m_ref[...], v_ref[...])

    spec = lambda: pl.BlockSpec((tr, C), lambda i: (i, 0))
    d, nm, nv = pl.pallas_call(
        body, name=name, grid=(R // tr,), in_specs=[spec() for _ in range(4)], out_specs=[spec() for _ in range(3)],
        out_shape=[_S((R, C))] * 3, compiler_params=_cp("parallel"),
    )(w2, g2, m2, v2)
    return d.reshape(shape), nm.reshape(shape), nv.reshape(shape)


def adamw_reduce(w, parts, theirs, m, v, chip, name):
    shape = w.shape
    C = shape[-1]
    R = math.prod(shape[1:-1])
    w3, m3, v3 = w.reshape(DEPTH, R, C), m.reshape(DEPTH, R, C), v.reshape(DEPTH, R, C)
    tr = _row_tile(R, C)

    def body(chip_ref, w_ref, *refs):
        layer_refs, (m_ref, v_ref, g_ref, d_ref, nm_ref, nv_ref) = refs[:2 * DEPTH], refs[2 * DEPTH:]
        layer = pl.program_id(0)
        for l in range(DEPTH):
            @pl.when(layer == l)
            def _():
                a_ref, t_ref = layer_refs[2 * l], layer_refs[2 * l + 1]
                gv = ((a_ref[0] + t_ref[0].astype(f32)) + t_ref[1].astype(f32)) + t_ref[2].astype(f32)
                g_ref[0] = gv
                d_ref[0], nm_ref[0], nv_ref[0] = _adamw_update(w_ref[0], gv, m_ref[0], v_ref[0])

    spec = lambda: pl.BlockSpec((1, tr, C), lambda l, i, c: (l, i, 0))
    rows_of = lambda l: (lambda ll, i, c: jnp.where(ll == l, i, 0))
    layer_specs = []
    for l in range(DEPTH):
        layer_specs.append(pl.BlockSpec((1, tr, C), lambda ll, i, c, r=rows_of(l): (c[0], r(ll, i, c), 0)))
        layer_specs.append(pl.BlockSpec((3, tr, C), lambda ll, i, c, r=rows_of(l): (0, r(ll, i, c), 0)))
    grid_spec = pltpu.PrefetchScalarGridSpec(
        num_scalar_prefetch=1, grid=(DEPTH, R // tr),
        in_specs=[spec()] + layer_specs + [spec(), spec()], out_specs=[spec() for _ in range(4)])
    operands = [x.reshape(x.shape[0], R, C) for l in range(DEPTH) for x in (parts[l], theirs[l])]
    g, d, nm, nv = pl.pallas_call(
        body, name=name, grid_spec=grid_spec, out_shape=[_S((DEPTH, R, C))] * 4,
        compiler_params=_cp("arbitrary", "arbitrary"),
    )(chip, w3, *operands, m3, v3)
    return g.reshape(shape), d.reshape(shape), nm.reshape(shape), nv.reshape(shape)


MESH = pl.DeviceIdType.MESH
ANY = pl.BlockSpec(memory_space=pl.ANY)


def _place():
    return lax.axis_index("x"), lax.axis_index("y"), lax.axis_index("c")


def _other_chips(mx, my):
    return [(1 - mx, my), (mx, 1 - my), (1 - mx, 1 - my)]


def all_gather(shards, name):
    nb = len(shards)

    def body(*refs):
        phases = _gather_phases([s.shape for s in shards], refs[:nb], refs[nb:2 * nb], *refs[2 * nb:])
        for phase in phases:
            phase()

    outs = pl.pallas_call(
        body, name=name, out_shape=_gather_out_shapes(shards), in_specs=[ANY] * nb, out_specs=[ANY] * nb,
        scratch_shapes=_gather_semaphores(nb),
    )(*shards)
    return list(outs)


GATHER_COPIES = 9
OWN_SIB, OWN_X, OWN_Y, X_SIB, Y_SIB, RELAY_X, RELAY_Y, DIAG0_SIB, DIAG1_SIB = range(GATHER_COPIES)


def _gather_out_shapes(shards):
    return [_S((N_DEV,) + s.shape, s.dtype) for s in shards]


def _gather_semaphores(nb):
    return [pltpu.SemaphoreType.DMA((nb, GATHER_COPIES)), pltpu.SemaphoreType.DMA((nb, GATHER_COPIES)),
            pltpu.SemaphoreType.DMA((nb,))]


def _gather_phases(shapes, x_refs, out_refs, send_sems, recv_sems, local_sems):
    nb = len(shapes)
    mx, my, mc = _place()
    sibling, xn, yn = (mx, my, 1 - mc), (1 - mx, my, mc), (mx, 1 - my, mc)

    def block(b, px, py, pc, half=None):
        ref = out_refs[b].at[4 * px + 2 * py + pc]
        if half is None:
            return ref
        n = shapes[b][0] // 2
        return ref.at[pl.ds(half * n, n)]

    def copy(b, k, dst, to, src=None):
        return pltpu.make_async_remote_copy(
            src_ref=dst if src is None else src, dst_ref=dst, send_sem=send_sems.at[b, k],
            recv_sem=recv_sems.at[b, k], device_id=to, device_id_type=MESH)

    def send(b, k):
        if k in (OWN_X, OWN_Y, OWN_SIB):
            return copy(b, k, block(b, mx, my, mc), {OWN_X: xn, OWN_Y: yn, OWN_SIB: sibling}[k], src=x_refs[b])
        what, to = {RELAY_X: ((1 - mx, my, mc, 0), yn), RELAY_Y: ((mx, 1 - my, mc, 1), xn),
                    X_SIB: ((1 - mx, my, mc), sibling), Y_SIB: ((mx, 1 - my, mc), sibling),
                    DIAG0_SIB: ((1 - mx, 1 - my, mc, 0), sibling), DIAG1_SIB: ((1 - mx, 1 - my, mc, 1), sibling)}[k]
        return copy(b, k, block(b, *what), to)

    def local(b):
        return pltpu.make_async_copy(x_refs[b], block(b, mx, my, mc), local_sems.at[b])

    def send_own():
        for k in (OWN_X, OWN_Y, OWN_SIB):
            for b in range(nb):
                send(b, k).start()
        for b in range(nb):
            local(b).start()

    def relay_neighbours():
        for b in range(nb):
            copy(b, OWN_X, block(b, 1 - mx, my, mc), xn).wait_recv()
            send(b, RELAY_X).start()
            send(b, X_SIB).start()
        for b in range(nb):
            copy(b, OWN_Y, block(b, mx, 1 - my, mc), yn).wait_recv()
            send(b, RELAY_Y).start()
            send(b, Y_SIB).start()

    def hand_on_diagonal():
        for b in range(nb):
            copy(b, RELAY_X, block(b, 1 - mx, 1 - my, mc, 0), yn).wait_recv()
            send(b, DIAG0_SIB).start()
            copy(b, RELAY_Y, block(b, 1 - mx, 1 - my, mc, 1), xn).wait_recv()
            send(b, DIAG1_SIB).start()

    def finish():
        for b in range(nb):
            copy(b, OWN_SIB, block(b, mx, my, 1 - mc), sibling).wait_recv()
            copy(b, X_SIB, block(b, 1 - mx, my, 1 - mc), sibling).wait_recv()
            copy(b, Y_SIB, block(b, mx, 1 - my, 1 - mc), sibling).wait_recv()
            copy(b, DIAG0_SIB, block(b, 1 - mx, 1 - my, 1 - mc, 0), sibling).wait_recv()
            copy(b, DIAG1_SIB, block(b, 1 - mx, 1 - my, 1 - mc, 1), sibling).wait_recv()
        for b in range(nb):
            for k in range(GATHER_COPIES):
                send(b, k).wait_send()
            local(b).wait()

    return send_own, relay_neighbours, hand_on_diagonal, finish


def exchange_sibling(gs, name):
    nb = len(gs)

    def body(*refs):
        start, finish = _sibling_phases(refs[:nb], refs[nb:2 * nb], refs[2 * nb], refs[2 * nb + 1])
        start()
        finish()

    outs = pl.pallas_call(
        body, name=name, out_shape=_sibling_out_shapes(gs), in_specs=[ANY] * nb, out_specs=[ANY] * nb,
        scratch_shapes=_sibling_semaphores(nb),
    )(*gs)
    return list(outs)


def _sibling_out_shapes(gs):
    return [_S((4,) + g.shape[1:], g.dtype) for g in gs]


def _sibling_semaphores(nb):
    return [pltpu.SemaphoreType.DMA((nb, 4)), pltpu.SemaphoreType.DMA((nb, 4))]


def _sibling_phases(g_refs, recv_refs, send_sems, recv_sems):
    mx, my, mc = _place()

    def copies():
        return [pltpu.make_async_remote_copy(
            src_ref=g_refs[b].at[2 * k + 1 - mc], dst_ref=recv_refs[b].at[k], send_sem=send_sems.at[b, k],
            recv_sem=recv_sems.at[b, k], device_id=(mx, my, 1 - mc), device_id_type=MESH)
            for b in range(len(g_refs)) for k in range(4)]

    def start():
        for cp in copies():
            cp.start()

    def finish():
        for cp in copies():
            cp.wait()

    return start, finish


def exchange_chips(parts, name):
    nb = len(parts)

    def body(*refs):
        start, finish = _chips_phases(refs[:nb], refs[nb:2 * nb], refs[2 * nb], refs[2 * nb + 1])
        start()
        finish()

    outs = pl.pallas_call(
        body, name=name, out_shape=_chips_out_shapes(parts), in_specs=[ANY] * nb, out_specs=[ANY] * nb,
        scratch_shapes=_chips_semaphores(nb),
    )(*parts)
    return list(outs)


def _chips_out_shapes(parts):
    return [_S((3,) + a.shape[1:], a.dtype) for a in parts]


def _chips_semaphores(nb):
    return [pltpu.SemaphoreType.DMA((nb, 3)), pltpu.SemaphoreType.DMA((nb, 3))]


def _chips_phases(a_refs, recv_refs, send_sems, recv_sems):
    mx, my, mc = _place()

    def copies():
        return [pltpu.make_async_remote_copy(
            src_ref=a_refs[b].at[2 * px + py], dst_ref=recv_refs[b].at[j], send_sem=send_sems.at[b, j],
            recv_sem=recv_sems.at[b, j], device_id=(px, py, mc), device_id_type=MESH)
            for b in range(len(a_refs)) for j, (px, py) in enumerate(_other_chips(mx, my))]

    def start():
        for cp in copies():
            cp.start()

    def finish():
        for cp in copies():
            cp.wait()

    return start, finish


def add_sibling(g, theirs, core, name, wire_dtype=f32):
    shp = theirs.shape
    C = shp[-1]
    R = math.prod(shp[1:-1])
    tr = _row_tile(R, C)
    narrow = wire_dtype != f32

    def body(core_ref, g_ref, t_ref, o_ref, *wire_ref):
        s = g_ref[...] + t_ref[...]
        o_ref[...] = s
        if narrow:
            wire_ref[0][...] = s.astype(wire_dtype)

    blk = lambda: pl.BlockSpec((1, tr, C), lambda k, i, c: (k, i, 0))
    grid_spec = pltpu.PrefetchScalarGridSpec(
        num_scalar_prefetch=1, grid=(4, R // tr),
        in_specs=[pl.BlockSpec((1, tr, C), lambda k, i, c: (2 * k + c[0], i, 0)), blk()],
        out_specs=[blk(), blk()] if narrow else [blk()])
    outs = pl.pallas_call(
        body, name=name, grid_spec=grid_spec,
        out_shape=[_S((4, R, C), f32)] + ([_S((4, R, C), wire_dtype)] if narrow else []),
        compiler_params=_cp("parallel", "parallel"),
    )(core, g.reshape(N_DEV, R, C), theirs.reshape(4, R, C))
    part = outs[0].reshape(shp)
    return part, (outs[1].reshape(shp) if narrow else part)


def add_chips(a, theirs, chip, name):
    _, R, C = a.shape
    tr = _row_tile(R, C)

    def body(chip_ref, a_ref, t_ref, out_ref):
        out_ref[...] = ((a_ref[0] + t_ref[0]) + t_ref[1]) + t_ref[2]

    grid_spec = pltpu.PrefetchScalarGridSpec(
        num_scalar_prefetch=1, grid=(R // tr,),
        in_specs=[pl.BlockSpec((1, tr, C), lambda i, c: (c[0], i, 0)), pl.BlockSpec((3, tr, C), lambda i, c: (0, i, 0))],
        out_specs=pl.BlockSpec((tr, C), lambda i, c: (i, 0)))
    return pl.pallas_call(
        body, name=name, grid_spec=grid_spec, out_shape=_S((R, C), a.dtype), compiler_params=_cp("parallel"),
    )(chip, a, theirs)


def _round_up(n, q):
    return (n + q - 1) // q * q


def _lane_rows(a):
    flat = a.reshape(-1)
    n = _round_up(flat.shape[0], SUBLANES * LANES)
    return jnp.pad(flat, (0, n - flat.shape[0])).reshape(-1, LANES)


def _full_to_shards(full, axis):
    shp = full.shape
    s = shp[axis] // N_DEV
    cut = full.reshape(shp[:axis] + (N_DEV, s) + shp[axis + 1:])
    return jnp.moveaxis(cut, axis, 0)


def _shards_to_full(parts, axis):
    shp = list(parts.shape[1:])
    shp[axis] *= N_DEV
    return jnp.moveaxis(parts, 0, axis).reshape(tuple(shp))


def kernel(x, p, g_pre, w_in, s5_a_re, s5_a_im, s5_log_dt, s5_b_re, s5_b_im, s5_c_re, s5_c_im, s5_d, w_glu, w_bs, conv_w, conv_b, lru_w_a, lru_b_a, lru_w_x, lru_b_x, lru_lambda, w_bl, w_out, g_post, w_ple, w_ple_gate, loss_target, m_g_pre, m_w_in, m_s5_a_re, m_s5_a_im, m_s5_log_dt, m_s5_b_re, m_s5_b_im, m_s5_c_re, m_s5_c_im, m_s5_d, m_w_glu, m_w_bs, m_conv_w, m_conv_b, m_lru_w_a, m_lru_b_a, m_lru_w_x, m_lru_b_x, m_lru_lambda, m_w_bl, m_w_out, m_g_post, m_w_ple, m_w_ple_gate, v_g_pre, v_w_in, v_s5_a_re, v_s5_a_im, v_s5_log_dt, v_s5_b_re, v_s5_b_im, v_s5_c_re, v_s5_c_im, v_s5_d, v_w_glu, v_w_bs, v_conv_w, v_conv_b, v_lru_w_a, v_lru_b_a, v_lru_w_x, v_lru_b_x, v_lru_lambda, v_w_bl, v_w_out, v_g_post, v_w_ple, v_w_ple_gate):
    given = dict(locals())
    W = {n: given[n] for n in WEIGHTS}
    M = {n: given["m_" + n] for n in WEIGHTS}
    V = {n: given["v_" + n] for n in WEIGHTS}
    xs, target = to_scan_order(x[0]), to_scan_order(loss_target[0])
    ps = [to_scan_order(p[i, 0]) for i in range(DEPTH)]

    mx, my, mc = _place()
    core = jnp.reshape(mc, (1,)).astype(jnp.int32)
    chip = jnp.reshape(2 * mx + my, (1,)).astype(jnp.int32)

    names = list(SHARDED)
    conv_rows = PAIR - CONV_WIDTH

    def layer_shards(i):
        return [W[n][i].astype(bf16) if n in GATHER_BF16 else jnp.pad(W[n][i], ((0, conv_rows), (0, 0))) for n in names]

    def layer_weights(i, gathered):
        full = {n: _shards_to_full(g if n in GATHER_BF16 else g[:, :CONV_WIDTH], SHARDED[n] - 1)
                for n, g in zip(names, gathered)}
        return {n: (full[n] if n in SHARDED else W[n][i]) for n in WEIGHTS}

    act, saved, weights = xs, [], []
    gathered = all_gather(layer_shards(0), "comm_gather_weights")
    for i in range(DEPTH):
        last = i + 1 == DEPTH
        weights.append(layer_weights(i, gathered))
        act, sv, gathered, loss_part = layer_fwd(act, ps[i], weights[i], "_l%d" % i, () if last else layer_shards(i + 1),
                                                 target if last else None)
        saved.append(sv)
    gact = act
    loss = lax.psum(loss_part[0, 0], ("x", "y", "c"))

    def gradient_blocks(i, g):
        rep_rows = [_lane_rows(g[n].reshape(W[n].shape[1:])) for n in REPLICATED]
        n_rows = sum(r.shape[0] for r in rep_rows)
        pad_rows = _round_up(n_rows, N_DEV * SUBLANES) - n_rows
        rep_blocks = jnp.concatenate(rep_rows + [jnp.zeros((pad_rows, LANES), f32)]).reshape(N_DEV, -1, LANES)
        blocks = [_full_to_shards(g[n].reshape(weights[i][n].shape), SHARDED[n] - 1) for n in names] + [rep_blocks]
        return blocks, [r.shape[0] for r in rep_rows]

    def add_siblings(i, blocks, theirs):
        parts, wire = [], []
        for k, (b, t) in enumerate(zip(blocks, theirs)):
            part, sent = add_sibling(b, t, core, "reduce_add_sibling_%d_l%d" % (k, i), bf16 if k < len(names) else f32)
            parts.append(part)
            wire.append(sent)
        return parts, wire

    parts, others, rep_sizes, job = [None] * DEPTH, [None] * DEPTH, None, None
    for i in reversed(range(DEPTH)):
        gact, g, done = layer_bwd(gact, weights[i], saved[i], "_l%d" % i, job)
        if job:
            parts[i + 1], others[i + 1] = done
        blocks, rep_sizes = gradient_blocks(i, g)
        job = (blocks, lambda theirs, i=i, blocks=blocks: add_siblings(i, blocks, theirs))
    parts[0], wire = job[1](exchange_sibling(job[0], "comm_reduce_sibling"))
    others[0] = exchange_chips(wire, "comm_reduce_chips")

    red, deltas, new_m, new_v = {}, {}, {}, {}
    for k, n in enumerate(names):
        red[n], deltas[n], new_m[n], new_v[n] = adamw_reduce(
            W[n], [parts[i][k] for i in range(DEPTH)], [others[i][k] for i in range(DEPTH)], M[n], V[n], chip, "adamw_" + n)
    pieces = [add_chips(parts[i][-1], others[i][-1], chip, "reduce_add_chips_l%d" % i) for i in range(DEPTH)]
    rep_all = [r.reshape(-1, LANES) for r in all_gather(pieces, "comm_gather_replicated")]
    off = 0
    for n, rows in zip(REPLICATED, rep_sizes):
        k = math.prod(W[n].shape[1:])
        red[n] = jnp.stack([rep_all[i][off:off + rows].reshape(-1)[:k] for i in range(DEPTH)]).reshape(W[n].shape)
        off += rows
        deltas[n], new_m[n], new_v[n] = adamw(W[n], red[n], M[n], V[n], "adamw_" + n)
    return (loss, from_scan_order(gact)[None], *[red[n] for n in WEIGHTS], *[deltas[n] for n in WEIGHTS],
            *[new_m[n] for n in WEIGHTS], *[new_v[n] for n in WEIGHTS])
```
